```python
import jax, jax.numpy as jnp
from jax import lax
import numpy as np

D_MODEL = 1024
BATCH = 8
SEQ = 2048
DEPTH = 2

HEAD_DIM = 64
ROT_DIM = HEAD_DIM // 4
ROPE_THETA = 500000.0
BLOCK = 128
EPS = 1e-5
NEG_INF = -1e30

A_Q_HEADS = 16
A_KV_HEADS = 4
A_REP = A_Q_HEADS // A_KV_HEADS
A_WINDOW = 128
A_Q_W = A_Q_HEADS * HEAD_DIM
A_KV_W = A_KV_HEADS * HEAD_DIM

B_GROUPS = 12
B_CH = 64
B_WIDTH = B_GROUPS * B_CH
B_CHUNK = 128

C_PAIRS = ((128, 1), (512, 4), (2048, 16))
C_HEADS_PER_GROUP = 4
C_HEADS = C_HEADS_PER_GROUP * len(C_PAIRS)
C_QKV_W = 3 * C_HEADS * HEAD_DIM
C_OUT_W = C_HEADS_PER_GROUP * HEAD_DIM

N_BRANCH = 3
GATE_W = N_BRANCH * D_MODEL
IN_SPLITS = [A_Q_W, A_Q_W + A_KV_W, A_Q_W + 2 * A_KV_W, A_Q_W + 2 * A_KV_W + 2 * B_WIDTH,
             A_Q_W + 2 * A_KV_W + 2 * B_WIDTH + C_QKV_W]
D_IN = A_Q_W + 2 * A_KV_W + 2 * B_WIDTH + C_QKV_W + GATE_W

N_EXPERT_GROUPS = 4
EXPERTS_PER_GROUP = 8
N_EXPERTS = N_EXPERT_GROUPS * EXPERTS_PER_GROUP
TOP_K = 2
D_EXPERT = 256

kernel_name = "hybrid_gated_swa_sgu_dilated_hmoe"


def rms_norm(x, g):
    xf = x.astype(jnp.float32)
    y = xf * lax.rsqrt(jnp.mean(xf * xf, -1, keepdims=True) + EPS)
    return (y * g.astype(jnp.float32)).astype(x.dtype)


def layer_norm(x, g, bias):
    xf = x.astype(jnp.float32)
    mu = jnp.mean(xf, -1, keepdims=True)
    var = jnp.mean(jnp.square(xf - mu), -1, keepdims=True)
    y = (xf - mu) * lax.rsqrt(var + EPS) * g.astype(jnp.float32) + bias.astype(jnp.float32)
    return y.astype(x.dtype)


def rope_tables(positions):
    inv = ROPE_THETA ** (-jnp.arange(0, ROT_DIM, 2, dtype=jnp.float32) / ROT_DIM)
    ang = positions.astype(jnp.float32)[..., None] * inv
    return jnp.cos(ang)[:, None], jnp.sin(ang)[:, None]


def partial_rope(t, cos, sin):
    half = ROT_DIM // 2
    t1 = t[..., :half].astype(jnp.float32)
    t2 = t[..., half:ROT_DIM].astype(jnp.float32)
    rot = jnp.concatenate([t1 * cos - t2 * sin, t2 * cos + t1 * sin], -1).astype(t.dtype)
    return jnp.concatenate([rot, t[..., ROT_DIM:]], -1)


def split_heads(t, n_heads):
    b, s, _ = t.shape
    return t.reshape(b, s, n_heads, HEAD_DIM).transpose(0, 2, 1, 3)


def banded_attention(q, k, v, max_dist, sink=None):
    b, hk, r, L, hd = q.shape
    nb = L // BLOCK
    qb = q.reshape(b, hk, r, nb, BLOCK, hd)

    def band(t):
        tb = t.reshape(b, hk, nb, BLOCK, hd)
        prev = jnp.pad(tb[:, :, :-1], ((0, 0), (0, 0), (1, 0), (0, 0), (0, 0)))
        return jnp.concatenate([prev, tb], axis=3)

    kb, vb = band(k), band(v)
    s = jnp.einsum("bhrnqd,bhnkd->bhrnqk", qb, kb).astype(jnp.float32) * (hd ** -0.5)
    qi = jnp.arange(BLOCK)[:, None] + BLOCK
    kj = jnp.arange(2 * BLOCK)[None, :]
    dist = qi - kj
    blk = jnp.arange(nb)[:, None, None]
    valid = (dist >= 0) & (dist <= max_dist) & ((blk > 0) | (kj >= BLOCK))
    s = jnp.where(valid, s, NEG_INF)
    m = jnp.max(s, -1, keepdims=True)
    if sink is not None:
        sk = sink.astype(jnp.float32).reshape(1, hk, r, 1, 1, 1)
        m = jnp.maximum(m, sk)
    p = jnp.exp(s - m)
    denom = jnp.sum(p, -1, keepdims=True)
    if sink is not None:
        denom = denom + jnp.exp(sk - m)
    o = jnp.einsum("bhrnqk,bhnkd->bhrnqd", (p / denom).astype(v.dtype), vb)
    lse = (m + jnp.log(denom))[..., 0]
    return o.reshape(b, hk, r, L, hd), lse.reshape(b, hk, r, L)


def dilated_group(q, k, v, window, dilation):
    b, h, s, hd = q.shape
    n = s // dilation
    lr = -(-n // BLOCK) * BLOCK

    def to_res(t):
        t = t.reshape(b, h, n, dilation, hd).transpose(0, 1, 3, 2, 4).reshape(b, h * dilation, n, hd)
        return jnp.pad(t, ((0, 0), (0, 0), (0, lr - n), (0, 0)))

    o, lse = banded_attention(to_res(q)[:, :, None], to_res(k), to_res(v), window // dilation)
    o = o[:, :, 0, :n].reshape(b, h, dilation, n, hd).transpose(0, 1, 3, 2, 4).reshape(b, h, s, hd)
    lse = lse[:, :, 0, :n].reshape(b, h, dilation, n).transpose(0, 1, 3, 2).reshape(b, h, s)
    return o, lse


def spatial_gating(zb, ln_g, ln_b, w_s, b_s):
    b, s, _ = zb.shape
    uv = jax.nn.gelu(zb)
    u, v = uv[..., :B_WIDTH], uv[..., B_WIDTH:]
    v = layer_norm(v, ln_g, ln_b)
    vc = v.reshape(b, s // B_CHUNK, B_CHUNK, B_GROUPS, B_CH)
    ws = w_s * jnp.tril(jnp.ones((B_CHUNK, B_CHUNK), w_s.dtype))
    sv = jnp.einsum("gts,bnsgc->bntgc", ws, vc) + b_s.T[None, None, :, :, None]
    return u * sv.reshape(b, s, B_WIDTH)


def hier_moe(h, w_rg, w_re, w_gate, w_up, w_down):
    b, s, d = h.shape
    t = h.reshape(b * s, d)
    pg = jax.nn.softmax((t @ w_rg).astype(jnp.float32), -1)
    pg_top, g_idx = lax.top_k(pg, 1)
    le = (t @ w_re).astype(jnp.float32).reshape(-1, N_EXPERT_GROUPS, EXPERTS_PER_GROUP)
    le_sel = jnp.take_along_axis(le, g_idx[:, :, None], axis=1)[:, 0]
    pe_top, e_idx = lax.top_k(jax.nn.softmax(le_sel, -1), TOP_K)
    w = pg_top * pe_top / jnp.sum(pe_top, -1, keepdims=True)
    ids = g_idx * EXPERTS_PER_GROUP + e_idx
    combine = jnp.sum(jax.nn.one_hot(ids, N_EXPERTS, dtype=jnp.float32) * w[..., None], 1)
    hg = jnp.einsum("nd,edf->nef", t, w_gate)
    hu = jnp.einsum("nd,edf->nef", t, w_up)
    a = jax.nn.silu(hg) * hu * combine[..., None].astype(t.dtype)
    return jnp.einsum("nef,efd->nd", a, w_down).reshape(b, s, d)


def setup_inputs(seed: int = 0) -> dict:
    key = jax.random.key(seed)
    ks = jax.random.split(key, 24)
    f32 = jnp.float32

    def nrm(k, shape, scale):
        return jax.random.normal(k, shape, f32) * scale

    L = DEPTH
    positions = (jax.random.randint(ks[1], (BATCH, 1), 0, 4096, dtype=jnp.int32)
                 + jnp.arange(SEQ, dtype=jnp.int32)[None, :])
    return {
        "x": nrm(ks[0], (BATCH, SEQ, D_MODEL), 1.0),
        "positions": positions,
        "norm1_g": 1.0 + nrm(ks[2], (L, D_MODEL), 0.02),
        "w_in": nrm(ks[3], (L, D_MODEL, D_IN), D_MODEL ** -0.5),
        "b_in": nrm(ks[4], (L, D_IN), 0.02),
        "attn_sinks": nrm(ks[5], (L, A_Q_HEADS), 0.5),
        "sgu_ln_g": 1.0 + nrm(ks[6], (L, B_WIDTH), 0.02),
        "sgu_ln_b": nrm(ks[7], (L, B_WIDTH), 0.02),
        "w_spatial": nrm(ks[8], (L, B_GROUPS, B_CHUNK, B_CHUNK), B_CHUNK ** -0.5),
        "b_spatial": 1.0 + nrm(ks[9], (L, B_GROUPS, B_CHUNK), 0.02),
        "w_proj_a": nrm(ks[10], (L, A_Q_W, D_MODEL), A_Q_W ** -0.5),
        "w_proj_b": nrm(ks[11], (L, B_WIDTH, D_MODEL), B_WIDTH ** -0.5),
        "w_proj_c": nrm(ks[12], (L, C_OUT_W, D_MODEL), C_OUT_W ** -0.5),
        "w_out": nrm(ks[13], (L, D_MODEL, D_MODEL), D_MODEL ** -0.5),
        "norm2_g": 1.0 + nrm(ks[14], (L, D_MODEL), 0.02),
        "w_router_group": nrm(ks[15], (L, D_MODEL, N_EXPERT_GROUPS), D_MODEL ** -0.5),
        "w_router_expert": nrm(ks[16], (L, D_MODEL, N_EXPERTS), D_MODEL ** -0.5),
        "w_expert_gate": nrm(ks[17], (L, N_EXPERTS, D_MODEL, D_EXPERT), D_MODEL ** -0.5),
        "w_expert_up": nrm(ks[18], (L, N_EXPERTS, D_MODEL, D_EXPERT), D_MODEL ** -0.5),
        "w_expert_down": nrm(ks[19], (L, N_EXPERTS, D_EXPERT, D_MODEL), D_EXPERT ** -0.5),
        "norm_f_g": 1.0 + nrm(ks[20], (D_MODEL,), 0.02),
    }


def reference(x, positions, norm1_g, w_in, b_in, attn_sinks, sgu_ln_g, sgu_ln_b, w_spatial, b_spatial,
              w_proj_a, w_proj_b, w_proj_c, w_out, norm2_g, w_router_group, w_router_expert,
              w_expert_gate, w_expert_up, w_expert_down, norm_f_g):
    b, s, _ = x.shape
    cos, sin = rope_tables(positions)
    for l in range(DEPTH):
        h = rms_norm(x, norm1_g[l])
        z = jnp.einsum("bsd,de->bse", h, w_in[l]) + b_in[l]
        za_q, za_k, za_v, zb, zc, zg = jnp.split(z, IN_SPLITS, axis=-1)

        qa = partial_rope(split_heads(za_q, A_Q_HEADS), cos, sin).reshape(b, A_KV_HEADS, A_REP, s, HEAD_DIM)
        ka = partial_rope(split_heads(za_k, A_KV_HEADS), cos, sin)
        va = split_heads(za_v, A_KV_HEADS)
        oa, _ = banded_attention(qa, ka, va, A_WINDOW - 1, attn_sinks[l].reshape(A_KV_HEADS, A_REP))
        oa = oa.reshape(b, A_Q_HEADS, s, HEAD_DIM).transpose(0, 2, 1, 3).reshape(b, s, A_Q_W)

        ob = spatial_gating(zb, sgu_ln_g[l], sgu_ln_b[l], w_spatial[l], b_spatial[l])

        zc = zc.reshape(b, s, 3, C_HEADS, HEAD_DIM).transpose(2, 0, 3, 1, 4)
        qc = partial_rope(zc[0], cos, sin)
        kc = partial_rope(zc[1], cos, sin)
        vc = zc[2]
        outs, lses = [], []
        for g, (win, dil) in enumerate(C_PAIRS):
            sl = slice(g * C_HEADS_PER_GROUP, (g + 1) * C_HEADS_PER_GROUP)
            o_g, lse_g = dilated_group(qc[:, sl], kc[:, sl], vc[:, sl], win, dil)
            outs.append(o_g)
            lses.append(lse_g)
        wgt = jax.nn.softmax(jnp.stack(lses, 0), axis=0)
        oc = jnp.sum(wgt[..., None] * jnp.stack(outs, 0).astype(jnp.float32), 0).astype(x.dtype)
        oc = oc.transpose(0, 2, 1, 3).reshape(b, s, C_OUT_W)

        gates = jax.nn.sigmoid(zg).reshape(b, s, N_BRANCH, D_MODEL)
        merged = (gates[:, :, 0] * (oa @ w_proj_a[l])
                  + gates[:, :, 1] * (ob @ w_proj_b[l])
                  + gates[:, :, 2] * (oc @ w_proj_c[l]))
        x = x + merged @ w_out[l]

        x = x + hier_moe(rms_norm(x, norm2_g[l]), w_router_group[l], w_router_expert[l],
                         w_expert_gate[l], w_expert_up[l], w_expert_down[l])
    return rms_norm(x, norm_f_g)
```

```python
import functools

import jax
import jax.numpy as jnp
from jax import lax
from jax.experimental import pallas as pl
from jax.experimental.pallas import tpu as pltpu

F32 = jnp.float32
BF16 = jnp.bfloat16

D_MODEL = 1024
HEAD_DIM = 64
ROT_DIM = HEAD_DIM // 4
ROPE_THETA = 500000.0
BLOCK = 128
EPS = 1e-5
NEG_INF = -1e30

A_Q_HEADS = 16
A_KV_HEADS = 4
A_REP = A_Q_HEADS // A_KV_HEADS
A_WINDOW = 128
A_Q_W = A_Q_HEADS * HEAD_DIM
A_KV_W = A_KV_HEADS * HEAD_DIM

B_GROUPS = 12
B_CH = 64
B_WIDTH = B_GROUPS * B_CH
B_CHUNK = 128

C_PAIRS = ((128, 1), (512, 4), (2048, 16))
C_HEADS_PER_GROUP = 4
C_HEADS = C_HEADS_PER_GROUP * len(C_PAIRS)
C_OUT_W = C_HEADS_PER_GROUP * HEAD_DIM

N_BRANCH = 3
GATE_W = N_BRANCH * D_MODEL
D_IN = A_Q_W + 2 * A_KV_W + 2 * B_WIDTH + 3 * C_HEADS * HEAD_DIM + GATE_W

N_EXPERT_GROUPS = 4
EXPERTS_PER_GROUP = 8
N_EXPERTS = N_EXPERT_GROUPS * EXPERTS_PER_GROUP
D_EXPERT = 256

LANES = 128
MXU_W = 256
VMEM_LIMIT = 56 * 1024 * 1024

IN_TN = 256
COL_AQ = 0
COL_AK = A_Q_W // IN_TN
COL_AV = COL_AK + A_KV_W // IN_TN
COL_BU = COL_AV + A_KV_W // IN_TN
COL_BV = COL_BU + B_WIDTH // IN_TN
COL_CQ = COL_BV + B_WIDTH // IN_TN
COL_CK = COL_CQ + C_HEADS * HEAD_DIM // IN_TN
COL_CV = COL_CK + C_HEADS * HEAD_DIM // IN_TN
COL_G = COL_CV + C_HEADS * HEAD_DIM // IN_TN
N_COL_TILES = D_IN // IN_TN

ROUTE_W = LANES
ROUTE_E0 = N_EXPERT_GROUPS


def _cparams(sem):
    return pltpu.CompilerParams(dimension_semantics=sem, vmem_limit_bytes=VMEM_LIMIT)


def _rope_table_kernel(pos_ref, inv_ref, c_ref, s1_ref, s2_ref):
    lane = lax.broadcasted_iota(jnp.int32, (1, LANES), 1)
    d = lane & (HEAD_DIM - 1)
    ang = pos_ref[...].astype(F32) * inv_ref[...]
    c = jnp.cos(ang)
    s = jnp.sin(ang)
    half = ROT_DIM // 2
    c_ref[...] = jnp.where(d < ROT_DIM, c, 1.0)
    s1_ref[...] = jnp.where(d < half, -s, 0.0)
    s2_ref[...] = jnp.where((d >= half) & (d < ROT_DIM), s, 0.0)


def _rope_tables(positions):
    n = positions.size
    inv = ROPE_THETA ** (-jnp.arange(0, ROT_DIM, 2, dtype=F32) / ROT_DIM)
    inv_lane = jnp.tile(inv, LANES // inv.shape[0]).reshape(1, LANES)
    pos_b = jnp.broadcast_to(positions.reshape(n, 1), (n, LANES))
    tm = 1024
    spec = pl.BlockSpec((tm, LANES), lambda i: (i, 0))
    return pl.pallas_call(
        _rope_table_kernel,
        out_shape=[jax.ShapeDtypeStruct((n, LANES), F32)] * 3,
        grid=(n // tm,),
        in_specs=[spec, pl.BlockSpec((1, LANES), lambda i: (0, 0))],
        out_specs=[spec, spec, spec],
        compiler_params=_cparams(("parallel",)),
        name="rope_tables",
    )(pos_b, inv_lane)


def _in_proj_kernel(x_ref, g_ref, w_ref, b_ref, c_ref, s1_ref, s2_ref, z_ref, h_ref):
    j = pl.program_id(1)

    @pl.when(j == 0)
    def _():
        x = x_ref[0]
        ms = jnp.mean(x * x, -1, keepdims=True)
        h_ref[...] = (x * lax.rsqrt(ms + EPS) * g_ref[...]).astype(BF16)

    acc = jnp.dot(h_ref[...], w_ref[...], preferred_element_type=F32) + b_ref[...]
    is_q = (j < COL_AK) | ((j >= COL_CQ) & (j < COL_CK))
    is_rope = (j < COL_AV) | ((j >= COL_CQ) & (j < COL_CV))

    @pl.when(is_rope)
    def _():
        scale = jnp.where(is_q, HEAD_DIM ** -0.5, 1.0).astype(F32)
        for t in range(IN_TN // LANES):
            a = acc[:, t * LANES:(t + 1) * LANES]
            r = (a * c_ref[...] + pltpu.roll(a, LANES - ROT_DIM // 2, 1) * s1_ref[...]
                 + pltpu.roll(a, ROT_DIM // 2, 1) * s2_ref[...])
            z_ref[0, :, t * LANES:(t + 1) * LANES] = (r * scale).astype(BF16)

    @pl.when(jnp.logical_not(is_rope))
    def _():
        z_ref[0] = acc.astype(BF16)


def _in_proj(x, g, w, b, tabs):
    bsz, s, d = x.shape
    c, s1, s2 = tabs
    tab_spec = pl.BlockSpec((s, LANES), lambda bi, j: (bi, 0))
    return pl.pallas_call(
        _in_proj_kernel,
        out_shape=jax.ShapeDtypeStruct((bsz, s, D_IN), BF16),
        grid=(bsz, N_COL_TILES),
        in_specs=[
            pl.BlockSpec((1, s, d), lambda bi, j: (bi, 0, 0)),
            pl.BlockSpec((1, d), lambda bi, j: (0, 0)),
            pl.BlockSpec((d, IN_TN), lambda bi, j: (0, j)),
            pl.BlockSpec((1, IN_TN), lambda bi, j: (0, j)),
            tab_spec, tab_spec, tab_spec,
        ],
        out_specs=pl.BlockSpec((1, s, IN_TN), lambda bi, j: (bi, 0, j)),
        scratch_shapes=[pltpu.VMEM((s, d), BF16)],
        compiler_params=_cparams(("parallel", "arbitrary")),
        name="in_proj",
    )(x, g.reshape(1, d), w, b.reshape(1, D_IN), c, s1, s2)


def _band_attention(q_ref, k_rows, v_rows, o_ref, lse_ref, length, max_dist, sink_col):
    nb = length // BLOCK
    kb_rows = min(2 * BLOCK, length)
    stack = C_HEADS_PER_GROUP * BLOCK
    seg = lax.broadcasted_iota(jnp.int32, (1, MXU_W), 1) // HEAD_DIM
    row = lax.broadcasted_iota(jnp.int32, (stack, 1), 0)
    rowseg = row // BLOCK
    own = seg == rowseg
    qi_local = row - rowseg * BLOCK
    kj_local = lax.broadcasted_iota(jnp.int32, (1, kb_rows), 1)

    def step(n, carry):
        r0 = pl.multiple_of(n * BLOCK, BLOCK)
        k0 = pl.multiple_of(jnp.maximum(n - 1, 0) * BLOCK, BLOCK)
        q = q_ref[pl.ds(r0, BLOCK), :]
        qs = jnp.where(own, jnp.concatenate([q] * C_HEADS_PER_GROUP, axis=0), jnp.zeros((), q.dtype))
        kb = k_rows(k0, kb_rows)
        vb = v_rows(k0, kb_rows)
        s = lax.dot_general(qs, kb, (((1,), (1,)), ((), ())), preferred_element_type=F32)
        dist = (r0 + qi_local) - (k0 + kj_local)
        s = jnp.where((dist >= 0) & (dist <= max_dist), s, NEG_INF)
        m = jnp.max(s, -1, keepdims=True)
        if sink_col is not None:
            m = jnp.maximum(m, sink_col)
        p = jnp.exp(s - m)
        denom = jnp.sum(p, -1, keepdims=True)
        if sink_col is not None:
            denom = denom + jnp.exp(sink_col - m)
        o_all = jnp.dot(p.astype(BF16), vb, preferred_element_type=F32) / denom
        o = jnp.zeros((BLOCK, MXU_W), F32)
        for h in range(C_HEADS_PER_GROUP):
            o = o + jnp.where(seg == h, o_all[h * BLOCK:(h + 1) * BLOCK], 0.0)
        o_ref[pl.ds(r0, BLOCK), :] = o.astype(o_ref.dtype)
        if lse_ref is not None:
            lse_all = m + jnp.log(denom)
            lse = jnp.zeros((BLOCK, MXU_W), F32)
            for h in range(C_HEADS_PER_GROUP):
                lse = lse + jnp.where(seg == h, lse_all[h * BLOCK:(h + 1) * BLOCK], 0.0)
            lse_ref[pl.ds(r0, BLOCK), :] = lse
        return carry

    if nb == 1:
        step(0, 0)
    else:
        lax.fori_loop(0, nb, step, 0)


def _attn_a_kernel(sink_ref, q_ref, k_ref, v_ref, o_ref, k4_ref, v4_ref):
    length = q_ref.shape[1]
    row = lax.broadcasted_iota(jnp.int32, (A_REP * BLOCK, 1), 0) // BLOCK
    for g in range(A_KV_HEADS):
        kg = k_ref[0, :, g * HEAD_DIM:(g + 1) * HEAD_DIM]
        vg = v_ref[0, :, g * HEAD_DIM:(g + 1) * HEAD_DIM]
        k4_ref[...] = jnp.concatenate([kg] * A_REP, axis=1)
        v4_ref[...] = jnp.concatenate([vg] * A_REP, axis=1)
        sink_col = jnp.zeros((A_REP * BLOCK, 1), F32)
        for r in range(A_REP):
            sink_col = jnp.where(row == r, sink_ref[g * A_REP + r], sink_col)
        _band_attention(
            q_ref.at[0, :, g * MXU_W:(g + 1) * MXU_W],
            lambda st, sz: k4_ref[pl.ds(st, sz), :],
            lambda st, sz: v4_ref[pl.ds(st, sz), :],
            o_ref.at[0, :, g * MXU_W:(g + 1) * MXU_W],
            None, length, A_WINDOW - 1, sink_col)


def _attn_a(z, sinks):
    bsz, s, _ = z.shape
    grid_spec = pltpu.PrefetchScalarGridSpec(
        num_scalar_prefetch=1,
        grid=(bsz,),
        in_specs=[
            pl.BlockSpec((1, s, A_Q_W), lambda bi, sk: (bi, 0, 0)),
            pl.BlockSpec((1, s, A_KV_W), lambda bi, sk: (bi, 0, COL_AK)),
            pl.BlockSpec((1, s, A_KV_W), lambda bi, sk: (bi, 0, COL_AV)),
        ],
        out_specs=pl.BlockSpec((1, s, A_Q_W), lambda bi, sk: (bi, 0, 0)),
        scratch_shapes=[pltpu.VMEM((s, MXU_W), BF16), pltpu.VMEM((s, MXU_W), BF16)],
    )
    return pl.pallas_call(
        _attn_a_kernel,
        out_shape=jax.ShapeDtypeStruct((bsz, s, A_Q_W), BF16),
        grid_spec=grid_spec,
        compiler_params=_cparams(("parallel",)),
        name="attn_a",
    )(sinks, z, z, z)


def _attn_c_kernel(q_ref, k_ref, v_ref, o_ref, lse_ref, *, max_dist):
    length = q_ref.shape[1]
    _band_attention(
        q_ref.at[0],
        lambda st, sz: k_ref[0, pl.ds(st, sz), :],
        lambda st, sz: v_ref[0, pl.ds(st, sz), :],
        o_ref.at[0], lse_ref.at[0], length, max_dist, None)


def _attn_c_group(z, g, window, dilation):
    bsz, s, _ = z.shape
    n = s // dilation
    zr = z.reshape(bsz, n, dilation * D_IN)
    tiles = N_COL_TILES

    def in_spec(col):
        return pl.BlockSpec((1, n, MXU_W), lambda bi, r: (bi, 0, r * tiles + col + g))

    out_spec = pl.BlockSpec((1, n, MXU_W), lambda bi, r: (bi, 0, r))
    o, lse = pl.pallas_call(
        functools.partial(_attn_c_kernel, max_dist=window // dilation),
        out_shape=[jax.ShapeDtypeStruct((bsz, n, dilation * MXU_W), BF16),
                   jax.ShapeDtypeStruct((bsz, n, dilation * MXU_W), F32)],
        grid=(bsz, dilation),
        in_specs=[in_spec(COL_CQ), in_spec(COL_CK), in_spec(COL_CV)],
        out_specs=[out_spec, out_spec],
        compiler_params=_cparams(("parallel", "parallel")),
        name=f"attn_c{g}",
    )(zr, zr, zr)
    return o.reshape(bsz * s, MXU_W), lse.reshape(bsz * s, MXU_W)


def _sgu_kernel(zu_ref, zv_ref, lng_ref, lnb_ref, ws_ref, bs_ref, o_ref):
    tm = zu_ref.shape[1]
    u = jax.nn.gelu(zu_ref[0].astype(F32))
    v = jax.nn.gelu(zv_ref[0].astype(F32))
    mu = jnp.mean(v, -1, keepdims=True)
    var = jnp.mean(jnp.square(v - mu), -1, keepdims=True)
    vn = ((v - mu) * lax.rsqrt(var + EPS) * lng_ref[...] + lnb_ref[...]).astype(BF16)
    causal = (lax.broadcasted_iota(jnp.int32, (B_CHUNK, 1), 0)
              >= lax.broadcasted_iota(jnp.int32, (1, B_CHUNK), 1))
    first = lax.broadcasted_iota(jnp.int32, (1, LANES), 1) < B_CH
    zero = jnp.zeros((), BF16)
    for p in range(B_WIDTH // LANES):
        w0 = jnp.where(causal, ws_ref[2 * p], zero)
        w1 = jnp.where(causal, ws_ref[2 * p + 1], zero)
        cols = slice(p * LANES, (p + 1) * LANES)
        for c in range(tm // B_CHUNK):
            rows = slice(c * B_CHUNK, (c + 1) * B_CHUNK)
            vv = vn[rows, cols]
            sv = jnp.where(first,
                           jnp.dot(w0, vv, preferred_element_type=F32),
                           jnp.dot(w1, vv, preferred_element_type=F32)) + bs_ref[:, cols]
            o_ref[0, rows, cols] = (u[rows, cols] * sv).astype(o_ref.dtype)


def _sgu(z, ln_g, ln_b, w_s, b_s):
    bsz, s, _ = z.shape
    tm = 512
    bias = jnp.repeat(b_s.T, B_CH, axis=1)
    col_u = COL_BU * IN_TN // B_WIDTH
    col_v = COL_BV * IN_TN // B_WIDTH
    return pl.pallas_call(
        _sgu_kernel,
        out_shape=jax.ShapeDtypeStruct((bsz, s, B_WIDTH), BF16),
        grid=(bsz, s // tm),
        in_specs=[
            pl.BlockSpec((1, tm, B_WIDTH), lambda bi, i: (bi, i, col_u)),
            pl.BlockSpec((1, tm, B_WIDTH), lambda bi, i: (bi, i, col_v)),
            pl.BlockSpec((1, B_WIDTH), lambda bi, i: (0, 0)),
            pl.BlockSpec((1, B_WIDTH), lambda bi, i: (0, 0)),
            pl.BlockSpec((B_GROUPS, B_CHUNK, B_CHUNK), lambda bi, i: (0, 0, 0)),
            pl.BlockSpec((B_CHUNK, B_WIDTH), lambda bi, i: (0, 0)),
        ],
        out_specs=pl.BlockSpec((1, tm, B_WIDTH), lambda bi, i: (bi, i, 0)),
        compiler_params=_cparams(("parallel", "parallel")),
        name="sgu",
    )(z, z, ln_g.reshape(1, B_WIDTH), ln_b.reshape(1, B_WIDTH), w_s.astype(BF16), bias)


def _route(logits):
    lane = lax.broadcasted_iota(jnp.int32, (1, ROUTE_W), 1)
    lane_f = lane.astype(F32)
    is_g = lane < N_EXPERT_GROUPS
    lg = jnp.where(is_g, logits, NEG_INF)
    mg = jnp.max(lg, -1, keepdims=True)
    pg_top = 1.0 / jnp.sum(jnp.where(is_g, jnp.exp(lg - mg), 0.0), -1, keepdims=True)
    g_idx = jnp.min(jnp.where(lg == mg, lane_f, float(ROUTE_W)), -1, keepdims=True)
    e_group = ((lane - ROUTE_E0) // EXPERTS_PER_GROUP).astype(F32)
    in_grp = (lane >= ROUTE_E0) & (lane < ROUTE_E0 + N_EXPERTS) & (e_group == g_idx)
    le = jnp.where(in_grp, logits, NEG_INF)
    m1 = jnp.max(le, -1, keepdims=True)
    i1 = jnp.min(jnp.where(le == m1, lane_f, float(ROUTE_W)), -1, keepdims=True)
    le2 = jnp.where(lane_f == i1, NEG_INF, le)
    m2 = jnp.max(le2, -1, keepdims=True)
    i2 = jnp.min(jnp.where(le2 == m2, lane_f, float(ROUTE_W)), -1, keepdims=True)
    t = jnp.exp(m2 - m1)
    w1 = pg_top / (1.0 + t)
    w2 = w1 * t
    return jnp.where(lane_f == i1, w1, 0.0) + jnp.where(lane_f == i2, w2, 0.0)


def _merge_kernel(x_ref, oa_ref, ob_ref, o1_ref, o2_ref, o3_ref, l1_ref, l2_ref, l3_ref,
                  g0_ref, g1_ref, g2_ref, g3_ref, wa_ref, wb_ref, wc_ref, wo_ref, n2_ref, wr_ref,
                  xo_ref, h2_ref, route_ref):
    l1, l2, l3 = l1_ref[...], l2_ref[...], l3_ref[...]
    m = jnp.maximum(jnp.maximum(l1, l2), l3)
    e1, e2, e3 = jnp.exp(l1 - m), jnp.exp(l2 - m), jnp.exp(l3 - m)
    oc = (e1 * o1_ref[...].astype(F32) + e2 * o2_ref[...].astype(F32)
          + e3 * o3_ref[...].astype(F32)) / (e1 + e2 + e3)
    zg = jnp.concatenate([g0_ref[...], g1_ref[...], g2_ref[...], g3_ref[...]], axis=1).astype(F32)
    gates = jax.nn.sigmoid(zg)
    pa = jnp.dot(oa_ref[...], wa_ref[...], preferred_element_type=F32)
    pb = jnp.dot(ob_ref[...], wb_ref[...], preferred_element_type=F32)
    pc = jnp.dot(oc.astype(BF16), wc_ref[...], preferred_element_type=F32)
    merged = (gates[:, :D_MODEL] * pa + gates[:, D_MODEL:2 * D_MODEL] * pb
              + gates[:, 2 * D_MODEL:] * pc)
    xn = x_ref[...] + jnp.dot(merged.astype(BF16), wo_ref[...], preferred_element_type=F32)
    xo_ref[...] = xn
    ms = jnp.mean(xn * xn, -1, keepdims=True)
    h2 = (xn * lax.rsqrt(ms + EPS) * n2_ref[...]).astype(BF16)
    h2_ref[...] = h2
    route_ref[...] = _route(jnp.dot(h2, wr_ref[...], preferred_element_type=F32))


def _merge(x2, oa, ob, oc_parts, z2, wa, wb, wc, wo, n2g, w_rg, w_re):
    n, d = x2.shape
    tm = 512
    (o1, l1), (o2, l2), (o3, l3) = oc_parts
    wr = jnp.concatenate(
        [w_rg, w_re, jnp.zeros((d, ROUTE_W - N_EXPERT_GROUPS - N_EXPERTS), w_rg.dtype)], axis=1)
    gate_w = GATE_W // 4
    gate_col0 = COL_G * IN_TN // gate_w

    def rows(width):
        return pl.BlockSpec((tm, width), lambda i: (i, 0))

    def full(shape):
        return pl.BlockSpec(shape, lambda i: (0, 0))

    gate_specs = [pl.BlockSpec((tm, gate_w), lambda i, k=k: (i, gate_col0 + k)) for k in range(4)]
    return pl.pallas_call(
        _merge_kernel,
        out_shape=[jax.ShapeDtypeStruct((n, d), F32),
                   jax.ShapeDtypeStruct((n, d), BF16),
                   jax.ShapeDtypeStruct((n, ROUTE_W), F32)],
        grid=(n // tm,),
        in_specs=[rows(d), rows(A_Q_W), rows(B_WIDTH),
                  rows(C_OUT_W), rows(C_OUT_W), rows(C_OUT_W),
                  rows(C_OUT_W), rows(C_OUT_W), rows(C_OUT_W),
                  *gate_specs,
                  full((A_Q_W, d)), full((B_WIDTH, d)), full((C_OUT_W, d)), full((d, d)),
                  full((1, d)), full((d, ROUTE_W))],
        out_specs=[rows(d), rows(d), rows(ROUTE_W)],
        compiler_params=_cparams(("parallel",)),
        name="merge",
    )(x2, oa, ob, o1, o2, o3, l1, l2, l3, z2, z2, z2, z2,
      wa.astype(BF16), wb.astype(BF16), wc.astype(BF16), wo.astype(BF16),
      n2g.reshape(1, d), wr.astype(BF16))


def _moe_dense_kernel(h_ref, cw_ref, x_ref, wg_ref, wu_ref, wd_ref, o_ref, acc_ref):
    e = pl.program_id(1)

    @pl.when(e == 0)
    def _():
        acc_ref[...] = jnp.zeros_like(acc_ref)

    h = h_ref[...]
    hg = jnp.dot(h, wg_ref[0], preferred_element_type=F32)
    hu = jnp.dot(h, wu_ref[0], preferred_element_type=F32)
    lane = lax.broadcasted_iota(jnp.int32, (1, ROUTE_W), 1)
    c = jnp.sum(jnp.where(lane == e + ROUTE_E0, cw_ref[...], 0.0), -1, keepdims=True)
    a = jax.nn.silu(hg) * hu * c
    acc_ref[...] += jnp.dot(a.astype(BF16), wd_ref[0], preferred_element_type=F32)

    @pl.when(e == pl.num_programs(1) - 1)
    def _():
        o_ref[...] = x_ref[...] + acc_ref[...]


def _moe(h2, route, x2, w_gate, w_up, w_down):
    n, d = x2.shape
    tm = 1024
    return pl.pallas_call(
        _moe_dense_kernel,
        out_shape=jax.ShapeDtypeStruct((n, d), F32),
        grid=(n // tm, N_EXPERTS),
        in_specs=[
            pl.BlockSpec((tm, d), lambda i, e: (i, 0)),
            pl.BlockSpec((tm, ROUTE_W), lambda i, e: (i, 0)),
            pl.BlockSpec((tm, d), lambda i, e: (i, 0)),
            pl.BlockSpec((1, d, D_EXPERT), lambda i, e: (e, 0, 0)),
            pl.BlockSpec((1, d, D_EXPERT), lambda i, e: (e, 0, 0)),
            pl.BlockSpec((1, D_EXPERT, d), lambda i, e: (e, 0, 0)),
        ],
        out_specs=pl.BlockSpec((tm, d), lambda i, e: (i, 0)),
        scratch_shapes=[pltpu.VMEM((tm, d), F32)],
        compiler_params=_cparams(("parallel", "arbitrary")),
        name="moe",
    )(h2, route, x2, w_gate.astype(BF16), w_up.astype(BF16), w_down.astype(BF16))


def _final_norm_kernel(x_ref, g_ref, o_ref):
    x = x_ref[...]
    ms = jnp.mean(x * x, -1, keepdims=True)
    o_ref[...] = x * lax.rsqrt(ms + EPS) * g_ref[...]


def _final_norm(x2, g):
    n, d = x2.shape
    tm = 1024
    return pl.pallas_call(
        _final_norm_kernel,
        out_shape=jax.ShapeDtypeStruct((n, d), F32),
        grid=(n // tm,),
        in_specs=[pl.BlockSpec((tm, d), lambda i: (i, 0)), pl.BlockSpec((1, d), lambda i: (0, 0))],
        out_specs=pl.BlockSpec((tm, d), lambda i: (i, 0)),
        compiler_params=_cparams(("parallel",)),
        name="final_norm",
    )(x2, g.reshape(1, d))


def kernel(x, positions, norm1_g, w_in, b_in, attn_sinks, sgu_ln_g, sgu_ln_b, w_spatial, b_spatial,
           w_proj_a, w_proj_b, w_proj_c, w_out, norm2_g, w_router_group, w_router_expert,
           w_expert_gate, w_expert_up, w_expert_down, norm_f_g):
    bsz, s, d = x.shape
    depth = w_in.shape[0]
    assert d == D_MODEL and s % (C_PAIRS[-1][1] * BLOCK) == 0
    tabs = _rope_tables(positions)
    for l in range(depth):
        z = _in_proj(x, norm1_g[l], w_in[l].astype(BF16), b_in[l], tabs)
        oa = _attn_a(z, attn_sinks[l])
        ob = _sgu(z, sgu_ln_g[l], sgu_ln_b[l], w_spatial[l], b_spatial[l])
        oc_parts = [_attn_c_group(z, g, win, dil) for g, (win, dil) in enumerate(C_PAIRS)]
        x2, h2, route = _merge(
            x.reshape(bsz * s, d), oa.reshape(bsz * s, A_Q_W), ob.reshape(bsz * s, B_WIDTH),
            oc_parts, z.reshape(bsz * s, D_IN), w_proj_a[l], w_proj_b[l], w_proj_c[l], w_out[l],
            norm2_g[l], w_router_group[l], w_router_expert[l])
        x = _moe(h2, route, x2, w_expert_gate[l], w_expert_up[l], w_expert_down[l]).reshape(bsz, s, d)
    return _final_norm(x.reshape(bsz * s, d), norm_f_g).reshape(bsz, s, d)
```

```python
import functools

import jax
import jax.numpy as jnp
from jax import lax
from jax.experimental import pallas as pl
from jax.experimental.pallas import tpu as pltpu

F32 = jnp.float32
BF16 = jnp.bfloat16

D_MODEL = 1024
HEAD_DIM = 64
ROT_DIM = HEAD_DIM // 4
ROPE_THETA = 500000.0
BLOCK = 128
EPS = 1e-5
NEG_INF = -1e30

A_Q_HEADS = 16
A_KV_HEADS = 4
A_REP = A_Q_HEADS // A_KV_HEADS
A_WINDOW = 128
A_Q_W = A_Q_HEADS * HEAD_DIM
A_KV_W = A_KV_HEADS * HEAD_DIM

B_GROUPS = 12
B_CH = 64
B_WIDTH = B_GROUPS * B_CH
B_CHUNK = 128

C_PAIRS = ((128, 1), (512, 4), (2048, 16))
C_HEADS_PER_GROUP = 4
C_HEADS = C_HEADS_PER_GROUP * len(C_PAIRS)
C_OUT_W = C_HEADS_PER_GROUP * HEAD_DIM

N_BRANCH = 3
GATE_W = N_BRANCH * D_MODEL
D_IN = A_Q_W + 2 * A_KV_W + 2 * B_WIDTH + 3 * C_HEADS * HEAD_DIM + GATE_W

N_EXPERT_GROUPS = 4
EXPERTS_PER_GROUP = 8
N_EXPERTS = N_EXPERT_GROUPS * EXPERTS_PER_GROUP
D_EXPERT = 256

LANES = 128
MXU_W = 256
VMEM_LIMIT = 56 * 1024 * 1024

IN_TN = 256
COL_AQ = 0
COL_AK = A_Q_W // IN_TN
COL_AV = COL_AK + A_KV_W // IN_TN
COL_BU = COL_AV + A_KV_W // IN_TN
COL_BV = COL_BU + B_WIDTH // IN_TN
COL_CQ = COL_BV + B_WIDTH // IN_TN
COL_CK = COL_CQ + C_HEADS * HEAD_DIM // IN_TN
COL_CV = COL_CK + C_HEADS * HEAD_DIM // IN_TN
COL_G = COL_CV + C_HEADS * HEAD_DIM // IN_TN
N_COL_TILES = D_IN // IN_TN

ROUTE_W = LANES
ROUTE_E0 = N_EXPERT_GROUPS


def _cparams(sem):
    return pltpu.CompilerParams(dimension_semantics=sem, vmem_limit_bytes=VMEM_LIMIT)


def _rope_table_kernel(pos_ref, inv_ref, c_ref, s1_ref, s2_ref):
    lane = lax.broadcasted_iota(jnp.int32, (1, LANES), 1)
    d = lane & (HEAD_DIM - 1)
    ang = pos_ref[...].astype(F32) * inv_ref[...]
    c = jnp.cos(ang)
    s = jnp.sin(ang)
    half = ROT_DIM // 2
    c_ref[...] = jnp.where(d < ROT_DIM, c, 1.0)
    s1_ref[...] = jnp.where(d < half, -s, 0.0)
    s2_ref[...] = jnp.where((d >= half) & (d < ROT_DIM), s, 0.0)


def _rope_tables(positions):
    n = positions.size
    inv = ROPE_THETA ** (-jnp.arange(0, ROT_DIM, 2, dtype=F32) / ROT_DIM)
    inv_lane = jnp.tile(inv, LANES // inv.shape[0]).reshape(1, LANES)
    pos_b = jnp.broadcast_to(positions.reshape(n, 1), (n, LANES))
    tm = 1024
    spec = pl.BlockSpec((tm, LANES), lambda i: (i, 0))
    return pl.pallas_call(
        _rope_table_kernel,
        out_shape=[jax.ShapeDtypeStruct((n, LANES), F32)] * 3,
        grid=(n // tm,),
        in_specs=[spec, pl.BlockSpec((1, LANES), lambda i: (0, 0))],
        out_specs=[spec, spec, spec],
        compiler_params=_cparams(("parallel",)),
        name="rope_tables",
    )(pos_b, inv_lane)


def _in_proj_kernel(x_ref, g_ref, w_ref, b_ref, c_ref, s1_ref, s2_ref, z_ref, h_ref):
    j = pl.program_id(1)

    @pl.when(j == 0)
    def _():
        x = x_ref[0]
        ms = jnp.mean(x * x, -1, keepdims=True)
        h_ref[...] = (x * lax.rsqrt(ms + EPS) * g_ref[...]).astype(BF16)

    acc = jnp.dot(h_ref[...], w_ref[...], preferred_element_type=F32) + b_ref[...]
    is_q = (j < COL_AK) | ((j >= COL_CQ) & (j < COL_CK))
    is_rope = (j < COL_AV) | ((j >= COL_CQ) & (j < COL_CV))

    @pl.when(is_rope)
    def _():
        scale = jnp.where(is_q, HEAD_DIM ** -0.5, 1.0).astype(F32)
        for t in range(IN_TN // LANES):
            a = acc[:, t * LANES:(t + 1) * LANES]
            r = (a * c_ref[...] + pltpu.roll(a, LANES - ROT_DIM // 2, 1) * s1_ref[...]
                 + pltpu.roll(a, ROT_DIM // 2, 1) * s2_ref[...])
            z_ref[0, :, t * LANES:(t + 1) * LANES] = (r * scale).astype(BF16)

    @pl.when(jnp.logical_not(is_rope))
    def _():
        z_ref[0] = acc.astype(BF16)


def _in_proj(x, g, w, b, tabs):
    bsz, s, d = x.shape
    c, s1, s2 = tabs
    tab_spec = pl.BlockSpec((s, LANES), lambda bi, j: (bi, 0))
    return pl.pallas_call(
        _in_proj_kernel,
        out_shape=jax.ShapeDtypeStruct((bsz, s, D_IN), BF16),
        grid=(bsz, N_COL_TILES),
        in_specs=[
            pl.BlockSpec((1, s, d), lambda bi, j: (bi, 0, 0)),
            pl.BlockSpec((1, d), lambda bi, j: (0, 0)),
            pl.BlockSpec((d, IN_TN), lambda bi, j: (0, j)),
            pl.BlockSpec((1, IN_TN), lambda bi, j: (0, j)),
            tab_spec, tab_spec, tab_spec,
        ],
        out_specs=pl.BlockSpec((1, s, IN_TN), lambda bi, j: (bi, 0, j)),
        scratch_shapes=[pltpu.VMEM((s, d), BF16)],
        compiler_params=_cparams(("parallel", "arbitrary")),
        name="in_proj",
    )(x, g.reshape(1, d), w, b.reshape(1, D_IN), c, s1, s2)


STACK = C_HEADS_PER_GROUP * BLOCK
STEP_UNROLL = 2


def _head_stack(blk):
    seg = lax.broadcasted_iota(jnp.int32, (1, MXU_W), 1) // HEAD_DIM
    rowseg = lax.broadcasted_iota(jnp.int32, (STACK, 1), 0) // BLOCK
    return jnp.where(seg == rowseg, jnp.concatenate([blk] * C_HEADS_PER_GROUP, axis=0),
                     jnp.zeros((), blk.dtype))


def _band_step(q, kst_ref, vst_ref, v_prev, blk, n, has_prev, diag_key, sinks):
    contract = (((1,), (1,)), ((), ()))
    ii = lax.broadcasted_iota(jnp.int32, (BLOCK, 1), 0)
    jj = lax.broadcasted_iota(jnp.int32, (1, BLOCK), 1)
    upper = jj > ii
    seg = lax.broadcasted_iota(jnp.int32, (1, MXU_W), 1) // HEAD_DIM
    zero_b = jnp.zeros((), BF16)
    s_cur = lax.dot_general(q, kst_ref[blk], contract, preferred_element_type=F32)
    if has_prev:
        prev = jnp.maximum(blk - 1, 0)
        s_prev = lax.dot_general(q, kst_ref[prev], contract, preferred_element_type=F32)
        bias = jnp.where(n == 0, NEG_INF, 0.0).astype(F32)
    p_cur, p_prev, p_diag, inv, lse = [], [], [], [], []
    for h in range(C_HEADS_PER_GROUP):
        cols = slice(h * BLOCK, (h + 1) * BLOCK)
        if has_prev:
            sp = s_prev[:, cols] + bias
            f = jnp.where(upper, sp, s_cur[:, cols])
        else:
            f = jnp.where(upper, NEG_INF, s_cur[:, cols])
        m = jnp.max(f, -1, keepdims=True)
        if diag_key:
            sd = jnp.sum(jnp.where(jj == ii, sp, 0.0), -1, keepdims=True)
            m = jnp.maximum(m, sd)
        if sinks is not None:
            m = jnp.maximum(m, sinks[h])
        p = jnp.exp(f - m)
        den = jnp.sum(p, -1, keepdims=True)
        if diag_key:
            pd = jnp.exp(sd - m)
            den = den + pd
            p_diag.append(pd)
        if sinks is not None:
            den = den + jnp.exp(sinks[h] - m)
        pb = p.astype(BF16)
        p_cur.append(jnp.where(upper, zero_b, pb))
        if has_prev:
            p_prev.append(jnp.where(upper, pb, zero_b))
        inv.append(1.0 / den)
        lse.append(m + jnp.log(den))
    o = jnp.dot(jnp.concatenate(p_cur, axis=1), vst_ref[blk], preferred_element_type=F32)
    if has_prev:
        o = o + jnp.dot(jnp.concatenate(p_prev, axis=1), vst_ref[prev], preferred_element_type=F32)

    def per_head(head_cols):
        out = head_cols[0]
        for h in range(1, C_HEADS_PER_GROUP):
            out = jnp.where(seg == h, head_cols[h], out)
        return out

    if diag_key:
        o = o + per_head(p_diag) * v_prev.astype(F32)
    return o * per_head(inv), lse, per_head


def _attn_a_kernel(sink_ref, q_ref, k_ref, v_ref, o_ref, kst_ref, vst_ref):
    nb = q_ref.shape[1] // BLOCK
    for g in range(A_KV_HEADS):
        kv_cols = slice(g * HEAD_DIM, (g + 1) * HEAD_DIM)
        q_cols = slice(g * MXU_W, (g + 1) * MXU_W)

        def build(n, carry):
            rows = pl.ds(pl.multiple_of(n * BLOCK, BLOCK), BLOCK)
            kst_ref[n] = _head_stack(jnp.concatenate([k_ref[0, rows, kv_cols]] * A_REP, axis=1))
            vst_ref[n] = _head_stack(jnp.concatenate([v_ref[0, rows, kv_cols]] * A_REP, axis=1))
            return carry

        lax.fori_loop(0, nb, build, 0)
        sinks = [sink_ref[g * A_REP + r] for r in range(A_REP)]

        def step(n, carry):
            rows = pl.ds(pl.multiple_of(n * BLOCK, BLOCK), BLOCK)
            o, _, _ = _band_step(q_ref[0, rows, q_cols], kst_ref, vst_ref, None, n, n,
                                 True, False, sinks)
            o_ref[0, rows, q_cols] = o.astype(o_ref.dtype)
            return carry

        lax.fori_loop(0, nb, step, 0, unroll=STEP_UNROLL)


def _attn_a(z, sinks):
    bsz, s, _ = z.shape
    nb = s // BLOCK
    grid_spec = pltpu.PrefetchScalarGridSpec(
        num_scalar_prefetch=1,
        grid=(bsz,),
        in_specs=[
            pl.BlockSpec((1, s, A_Q_W), lambda bi, sk: (bi, 0, 0)),
            pl.BlockSpec((1, s, A_KV_W), lambda bi, sk: (bi, 0, COL_AK)),
            pl.BlockSpec((1, s, A_KV_W), lambda bi, sk: (bi, 0, COL_AV)),
        ],
        out_specs=pl.BlockSpec((1, s, A_Q_W), lambda bi, sk: (bi, 0, 0)),
        scratch_shapes=[pltpu.VMEM((nb, STACK, MXU_W), BF16), pltpu.VMEM((nb, STACK, MXU_W), BF16)],
    )
    return pl.pallas_call(
        _attn_a_kernel,
        out_shape=jax.ShapeDtypeStruct((bsz, s, A_Q_W), BF16),
        grid_spec=grid_spec,
        compiler_params=_cparams(("parallel",)),
        name="attn_a",
    )(sinks, z, z, z)


def _attn_c_kernel(q_ref, k_ref, v_ref, oc_ref, stage_ref, pq_ref, pv_ref, kst_ref, vst_ref,
                   og_ref, lg_ref):
    s = q_ref.shape[1]
    nblk = s // BLOCK
    halves = MXU_W // LANES

    def class_rows(idx, nb, d):
        r = idx // nb
        n = idx - r * nb
        start = r + n * (BLOCK * d)
        rows = pl.ds(start, BLOCK, stride=d) if d > 1 else pl.ds(pl.multiple_of(start, BLOCK), BLOCK)
        return n, rows

    def stage(src_ref, cols):
        x = src_ref[0, :, cols].astype(F32)
        for t in range(halves):
            stage_ref[t] = x[:, t * LANES:(t + 1) * LANES]

    def staged_block(rows):
        return jnp.concatenate([stage_ref[t, rows, :] for t in range(halves)], axis=1).astype(BF16)

    for g, (win, d) in enumerate(C_PAIRS):
        cols = slice(g * MXU_W, (g + 1) * MXU_W)
        nb = nblk // d
        has_prev = nb > 1
        assert win // d == BLOCK

        def build_k(idx, carry):
            _, rows = class_rows(idx, nb, d)
            kst_ref[idx] = _head_stack(staged_block(rows))
            return carry

        def build_v(idx, carry):
            _, rows = class_rows(idx, nb, d)
            blk = staged_block(rows)
            pv_ref[pl.ds(pl.multiple_of(idx * BLOCK, BLOCK), BLOCK), :] = blk
            vst_ref[idx] = _head_stack(blk)
            return carry

        def build_q(idx, carry):
            _, rows = class_rows(idx, nb, d)
            pq_ref[pl.ds(pl.multiple_of(idx * BLOCK, BLOCK), BLOCK), :] = staged_block(rows)
            return carry

        stage(k_ref, cols)
        lax.fori_loop(0, nblk, build_k, 0)
        stage(v_ref, cols)
        lax.fori_loop(0, nblk, build_v, 0)
        stage(q_ref, cols)
        lax.fori_loop(0, nblk, build_q, 0)

        def step(idx, carry):
            n, rows = class_rows(idx, nb, d)
            here = pl.ds(pl.multiple_of(idx * BLOCK, BLOCK), BLOCK)
            before = pl.ds(pl.multiple_of(jnp.maximum(idx - 1, 0) * BLOCK, BLOCK), BLOCK)
            v_prev = pv_ref[before, :] if has_prev else None
            o, lse, per_head = _band_step(pq_ref[here, :], kst_ref, vst_ref, v_prev, idx, n,
                                          has_prev, has_prev, None)
            lse_full = per_head(lse) + jnp.zeros((BLOCK, MXU_W), F32)
            for t in range(halves):
                og_ref[g, t, rows, :] = o[:, t * LANES:(t + 1) * LANES]
                lg_ref[g, t, rows, :] = lse_full[:, t * LANES:(t + 1) * LANES]
            return carry

        lax.fori_loop(0, nblk, step, 0, unroll=STEP_UNROLL)

    chunk = 2 * BLOCK

    def combine(i, carry):
        rows = pl.ds(pl.multiple_of(i * chunk, chunk), chunk)
        for t in range(halves):
            ls = [lg_ref[g, t, rows, :] for g in range(len(C_PAIRS))]
            m = jnp.maximum(jnp.maximum(ls[0], ls[1]), ls[2])
            es = [jnp.exp(l - m) for l in ls]
            num = (es[0] * og_ref[0, t, rows, :] + es[1] * og_ref[1, t, rows, :]
                   + es[2] * og_ref[2, t, rows, :])
            oc_ref[0, rows, t * LANES:(t + 1) * LANES] = (
                num / (es[0] + es[1] + es[2])).astype(oc_ref.dtype)
        return carry

    lax.fori_loop(0, s // chunk, combine, 0)


def _attn_c(z):
    bsz, s, _ = z.shape
    nblk = s // BLOCK
    width = C_HEADS * HEAD_DIM
    halves = MXU_W // LANES

    def in_spec(col):
        return pl.BlockSpec((1, s, width), lambda bi: (bi, 0, col * IN_TN // width))

    return pl.pallas_call(
        _attn_c_kernel,
        out_shape=jax.ShapeDtypeStruct((bsz, s, C_OUT_W), BF16),
        grid=(bsz,),
        in_specs=[in_spec(COL_CQ), in_spec(COL_CK), in_spec(COL_CV)],
        out_specs=pl.BlockSpec((1, s, C_OUT_W), lambda bi: (bi, 0, 0)),
        scratch_shapes=[
            pltpu.VMEM((halves, s, LANES), F32),
            pltpu.VMEM((s, MXU_W), BF16),
            pltpu.VMEM((s, MXU_W), BF16),
            pltpu.VMEM((nblk, STACK, MXU_W), BF16),
            pltpu.VMEM((nblk, STACK, MXU_W), BF16),
            pltpu.VMEM((len(C_PAIRS), halves, s, LANES), F32),
            pltpu.VMEM((len(C_PAIRS), halves, s, LANES), F32),
        ],
        compiler_params=_cparams(("parallel",)),
        name="attn_c",
    )(z, z, z)


def _sgu_kernel(zu_ref, zv_ref, lng_ref, lnb_ref, ws_ref, bs_ref, o_ref):
    tm = zu_ref.shape[1]
    u = jax.nn.gelu(zu_ref[0].astype(F32))
    v = jax.nn.gelu(zv_ref[0].astype(F32))
    mu = jnp.mean(v, -1, keepdims=True)
    var = jnp.mean(jnp.square(v - mu), -1, keepdims=True)
    vn = ((v - mu) * lax.rsqrt(var + EPS) * lng_ref[...] + lnb_ref[...]).astype(BF16)
    causal = (lax.broadcasted_iota(jnp.int32, (B_CHUNK, 1), 0)
              >= lax.broadcasted_iota(jnp.int32, (1, B_CHUNK), 1))
    first = lax.broadcasted_iota(jnp.int32, (1, LANES), 1) < B_CH
    zero = jnp.zeros((), BF16)
    for p in range(B_WIDTH // LANES):
        w0 = jnp.where(causal, ws_ref[2 * p], zero)
        w1 = jnp.where(causal, ws_ref[2 * p + 1], zero)
        cols = slice(p * LANES, (p + 1) * LANES)
        for c in range(tm // B_CHUNK):
            rows = slice(c * B_CHUNK, (c + 1) * B_CHUNK)
            vv = vn[rows, cols]
            sv = jnp.where(first,
                           jnp.dot(w0, vv, preferred_element_type=F32),
                           jnp.dot(w1, vv, preferred_element_type=F32)) + bs_ref[:, cols]
            o_ref[0, rows, cols] = (u[rows, cols] * sv).astype(o_ref.dtype)


def _sgu(z, ln_g, ln_b, w_s, b_s):
    bsz, s, _ = z.shape
    tm = 512
    bias = jnp.repeat(b_s.T, B_CH, axis=1)
    col_u = COL_BU * IN_TN // B_WIDTH
    col_v = COL_BV * IN_TN // B_WIDTH
    return pl.pallas_call(
        _sgu_kernel,
        out_shape=jax.ShapeDtypeStruct((bsz, s, B_WIDTH), BF16),
        grid=(bsz, s // tm),
        in_specs=[
            pl.BlockSpec((1, tm, B_WIDTH), lambda bi, i: (bi, i, col_u)),
            pl.BlockSpec((1, tm, B_WIDTH), lambda bi, i: (bi, i, col_v)),
            pl.BlockSpec((1, B_WIDTH), lambda bi, i: (0, 0)),
            pl.BlockSpec((1, B_WIDTH), lambda bi, i: (0, 0)),
            pl.BlockSpec((B_GROUPS, B_CHUNK, B_CHUNK), lambda bi, i: (0, 0, 0)),
            pl.BlockSpec((B_CHUNK, B_WIDTH), lambda bi, i: (0, 0)),
        ],
        out_specs=pl.BlockSpec((1, tm, B_WIDTH), lambda bi, i: (bi, i, 0)),
        compiler_params=_cparams(("parallel", "parallel")),
        name="sgu",
    )(z, z, ln_g.reshape(1, B_WIDTH), ln_b.reshape(1, B_WIDTH), w_s.astype(BF16), bias)


def _route(logits):
    lane = lax.broadcasted_iota(jnp.int32, (1, ROUTE_W), 1)
    lane_f = lane.astype(F32)
    is_g = lane < N_EXPERT_GROUPS
    lg = jnp.where(is_g, logits, NEG_INF)
    mg = jnp.max(lg, -1, keepdims=True)
    pg_top = 1.0 / jnp.sum(jnp.where(is_g, jnp.exp(lg - mg), 0.0), -1, keepdims=True)
    g_idx = jnp.min(jnp.where(lg == mg, lane_f, float(ROUTE_W)), -1, keepdims=True)
    e_group = ((lane - ROUTE_E0) // EXPERTS_PER_GROUP).astype(F32)
    in_grp = (lane >= ROUTE_E0) & (lane < ROUTE_E0 + N_EXPERTS) & (e_group == g_idx)
    le = jnp.where(in_grp, logits, NEG_INF)
    m1 = jnp.max(le, -1, keepdims=True)
    i1 = jnp.min(jnp.where(le == m1, lane_f, float(ROUTE_W)), -1, keepdims=True)
    le2 = jnp.where(lane_f == i1, NEG_INF, le)
    m2 = jnp.max(le2, -1, keepdims=True)
    i2 = jnp.min(jnp.where(le2 == m2, lane_f, float(ROUTE_W)), -1, keepdims=True)
    t = jnp.exp(m2 - m1)
    w1 = pg_top / (1.0 + t)
    w2 = w1 * t
    return jnp.where(lane_f == i1, w1, 0.0) + jnp.where(lane_f == i2, w2, 0.0)


def _merge_kernel(x_ref, oa_ref, ob_ref, oc_ref,
                  g0_ref, g1_ref, g2_ref, g3_ref, wa_ref, wb_ref, wc_ref, wo_ref, n2_ref, wr_ref,
                  xo_ref, h2_ref, route_ref):
    zg = jnp.concatenate([g0_ref[...], g1_ref[...], g2_ref[...], g3_ref[...]], axis=1).astype(F32)
    gates = jax.nn.sigmoid(zg)
    pa = jnp.dot(oa_ref[...], wa_ref[...], preferred_element_type=F32)
    pb = jnp.dot(ob_ref[...], wb_ref[...], preferred_element_type=F32)
    pc = jnp.dot(oc_ref[...], wc_ref[...], preferred_element_type=F32)
    merged = (gates[:, :D_MODEL] * pa + gates[:, D_MODEL:2 * D_MODEL] * pb
              + gates[:, 2 * D_MODEL:] * pc)
    xn = x_ref[...] + jnp.dot(merged.astype(BF16), wo_ref[...], preferred_element_type=F32)
    xo_ref[...] = xn
    ms = jnp.mean(xn * xn, -1, keepdims=True)
    h2 = (xn * lax.rsqrt(ms + EPS) * n2_ref[...]).astype(BF16)
    h2_ref[...] = h2
    route_ref[...] = _route(jnp.dot(h2, wr_ref[...], preferred_element_type=F32))


def _merge(x2, oa, ob, oc, z2, wa, wb, wc, wo, n2g, w_rg, w_re):
    n, d = x2.shape
    tm = 512
    wr = jnp.concatenate(
        [w_rg, w_re, jnp.zeros((d, ROUTE_W - N_EXPERT_GROUPS - N_EXPERTS), w_rg.dtype)], axis=1)
    gate_w = GATE_W // 4
    gate_col0 = COL_G * IN_TN // gate_w

    def rows(width):
        return pl.BlockSpec((tm, width), lambda i: (i, 0))

    def full(shape):
        return pl.BlockSpec(shape, lambda i: (0, 0))

    gate_specs = [pl.BlockSpec((tm, gate_w), lambda i, k=k: (i, gate_col0 + k)) for k in range(4)]
    return pl.pallas_call(
        _merge_kernel,
        out_shape=[jax.ShapeDtypeStruct((n, d), F32),
                   jax.ShapeDtypeStruct((n, d), BF16),
                   jax.ShapeDtypeStruct((n, ROUTE_W), F32)],
        grid=(n // tm,),
        in_specs=[rows(d), rows(A_Q_W), rows(B_WIDTH), rows(C_OUT_W),
                  *gate_specs,
                  full((A_Q_W, d)), full((B_WIDTH, d)), full((C_OUT_W, d)), full((d, d)),
                  full((1, d)), full((d, ROUTE_W))],
        out_specs=[rows(d), rows(d), rows(ROUTE_W)],
        compiler_params=_cparams(("parallel",)),
        name="merge",
    )(x2, oa, ob, oc, z2, z2, z2, z2,
      wa.astype(BF16), wb.astype(BF16), wc.astype(BF16), wo.astype(BF16),
      n2g.reshape(1, d), wr.astype(BF16))


def _moe_dense_kernel(h_ref, cw_ref, x_ref, wg_ref, wu_ref, wd_ref, o_ref, acc_ref):
    e = pl.program_id(1)

    @pl.when(e == 0)
    def _():
        acc_ref[...] = jnp.zeros_like(acc_ref)

    h = h_ref[...]
    hg = jnp.dot(h, wg_ref[0], preferred_element_type=F32)
    hu = jnp.dot(h, wu_ref[0], preferred_element_type=F32)
    lane = lax.broadcasted_iota(jnp.int32, (1, ROUTE_W), 1)
    c = jnp.sum(jnp.where(lane == e + ROUTE_E0, cw_ref[...], 0.0), -1, keepdims=True)
    a = jax.nn.silu(hg) * hu * c
    acc_ref[...] += jnp.dot(a.astype(BF16), wd_ref[0], preferred_element_type=F32)

    @pl.when(e == pl.num_programs(1) - 1)
    def _():
        o_ref[...] = x_ref[...] + acc_ref[...]


def _moe(h2, route, x2, w_gate, w_up, w_down):
    n, d = x2.shape
    tm = 1024
    return pl.pallas_call(
        _moe_dense_kernel,
        out_shape=jax.ShapeDtypeStruct((n, d), F32),
        grid=(n // tm, N_EXPERTS),
        in_specs=[
            pl.BlockSpec((tm, d), lambda i, e: (i, 0)),
            pl.BlockSpec((tm, ROUTE_W), lambda i, e: (i, 0)),
            pl.BlockSpec((tm, d), lambda i, e: (i, 0)),
            pl.BlockSpec((1, d, D_EXPERT), lambda i, e: (e, 0, 0)),
            pl.BlockSpec((1, d, D_EXPERT), lambda i, e: (e, 0, 0)),
            pl.BlockSpec((1, D_EXPERT, d), lambda i, e: (e, 0, 0)),
        ],
        out_specs=pl.BlockSpec((tm, d), lambda i, e: (i, 0)),
        scratch_shapes=[pltpu.VMEM((tm, d), F32)],
        compiler_params=_cparams(("parallel", "arbitrary")),
        name="moe",
    )(h2, route, x2, w_gate.astype(BF16), w_up.astype(BF16), w_down.astype(BF16))


def _final_norm_kernel(x_ref, g_ref, o_ref):
    x = x_ref[...]
    ms = jnp.mean(x * x, -1, keepdims=True)
    o_ref[...] = x * lax.rsqrt(ms + EPS) * g_ref[...]


def _final_norm(x2, g):
    n, d = x2.shape
    tm = 1024
    return pl.pallas_call(
        _final_norm_kernel,
        out_shape=jax.ShapeDtypeStruct((n, d), F32),
        grid=(n // tm,),
        in_specs=[pl.BlockSpec((tm, d), lambda i: (i, 0)), pl.BlockSpec((1, d), lambda i: (0, 0))],
        out_specs=pl.BlockSpec((tm, d), lambda i: (i, 0)),
        compiler_params=_cparams(("parallel",)),
        name="final_norm",
    )(x2, g.reshape(1, d))


def kernel(x, positions, norm1_g, w_in, b_in, attn_sinks, sgu_ln_g, sgu_ln_b, w_spatial, b_spatial,
           w_proj_a, w_proj_b, w_proj_c, w_out, norm2_g, w_router_group, w_router_expert,
           w_expert_gate, w_expert_up, w_expert_down, norm_f_g):
    bsz, s, d = x.shape
    depth = w_in.shape[0]
    assert d == D_MODEL and s % (C_PAIRS[-1][1] * BLOCK) == 0
    tabs = _rope_tables(positions)
    for l in range(depth):
        z = _in_proj(x, norm1_g[l], w_in[l].astype(BF16), b_in[l], tabs)
        oa = _attn_a(z, attn_sinks[l])
        ob = _sgu(z, sgu_ln_g[l], sgu_ln_b[l], w_spatial[l], b_spatial[l])
        oc = _attn_c(z)
        x2, h2, route = _merge(
            x.reshape(bsz * s, d), oa.reshape(bsz * s, A_Q_W), ob.reshape(bsz * s, B_WIDTH),
            oc.reshape(bsz * s, C_OUT_W), z.reshape(bsz * s, D_IN), w_proj_a[l], w_proj_b[l], w_proj_c[l], w_out[l],
            norm2_g[l], w_router_group[l], w_router_expert[l])
        x = _moe(h2, route, x2, w_expert_gate[l], w_expert_up[l], w_expert_down[l]).reshape(bsz, s, d)
    return _final_norm(x.reshape(bsz * s, d), norm_f_g).reshape(bsz, s, d)
```

```python
import functools

import jax
import jax.numpy as jnp
from jax import lax
from jax.experimental import pallas as pl
from jax.experimental.pallas import tpu as pltpu
from jax.experimental.pallas import tpu_sc as plsc

F32 = jnp.float32
BF16 = jnp.bfloat16

D_MODEL = 1024
HEAD_DIM = 64
ROT_DIM = HEAD_DIM // 4
ROPE_THETA = 500000.0
BLOCK = 128
EPS = 1e-5
NEG_INF = -1e30

A_Q_HEADS = 16
A_KV_HEADS = 4
A_REP = A_Q_HEADS // A_KV_HEADS
A_WINDOW = 128
A_Q_W = A_Q_HEADS * HEAD_DIM
A_KV_W = A_KV_HEADS * HEAD_DIM

B_GROUPS = 12
B_CH = 64
B_WIDTH = B_GROUPS * B_CH
B_CHUNK = 128

C_PAIRS = ((128, 1), (512, 4), (2048, 16))
C_HEADS_PER_GROUP = 4
C_HEADS = C_HEADS_PER_GROUP * len(C_PAIRS)
C_OUT_W = C_HEADS_PER_GROUP * HEAD_DIM

N_BRANCH = 3
GATE_W = N_BRANCH * D_MODEL
D_IN = A_Q_W + 2 * A_KV_W + 2 * B_WIDTH + 3 * C_HEADS * HEAD_DIM + GATE_W

N_EXPERT_GROUPS = 4
EXPERTS_PER_GROUP = 8
N_EXPERTS = N_EXPERT_GROUPS * EXPERTS_PER_GROUP
D_EXPERT = 256

LANES = 128
MXU_W = 256
VMEM_LIMIT = 56 * 1024 * 1024

IN_TN = 256
COL_AQ = 0
COL_AK = A_Q_W // IN_TN
COL_AV = COL_AK + A_KV_W // IN_TN
COL_BU = COL_AV + A_KV_W // IN_TN
COL_BV = COL_BU + B_WIDTH // IN_TN
COL_CQ = COL_BV + B_WIDTH // IN_TN
COL_CK = COL_CQ + C_HEADS * HEAD_DIM // IN_TN
COL_CV = COL_CK + C_HEADS * HEAD_DIM // IN_TN
COL_G = COL_CV + C_HEADS * HEAD_DIM // IN_TN
N_COL_TILES = D_IN // IN_TN

ROUTE_W = LANES
ROUTE_E0 = N_EXPERT_GROUPS


def _cparams(sem):
    return pltpu.CompilerParams(dimension_semantics=sem, vmem_limit_bytes=VMEM_LIMIT)


def _rope_table_kernel(pos_ref, inv_ref, c_ref, s1_ref, s2_ref):
    lane = lax.broadcasted_iota(jnp.int32, (1, LANES), 1)
    d = lane & (HEAD_DIM - 1)
    ang = pos_ref[...].astype(F32) * inv_ref[...]
    c = jnp.cos(ang)
    s = jnp.sin(ang)
    half = ROT_DIM // 2
    c_ref[...] = jnp.where(d < ROT_DIM, c, 1.0)
    s1_ref[...] = jnp.where(d < half, -s, 0.0)
    s2_ref[...] = jnp.where((d >= half) & (d < ROT_DIM), s, 0.0)


def _rope_tables(positions):
    n = positions.size
    inv = ROPE_THETA ** (-jnp.arange(0, ROT_DIM, 2, dtype=F32) / ROT_DIM)
    inv_lane = jnp.tile(inv, LANES // inv.shape[0]).reshape(1, LANES)
    pos_b = jnp.broadcast_to(positions.reshape(n, 1), (n, LANES))
    tm = 1024
    spec = pl.BlockSpec((tm, LANES), lambda i: (i, 0))
    return pl.pallas_call(
        _rope_table_kernel,
        out_shape=[jax.ShapeDtypeStruct((n, LANES), F32)] * 3,
        grid=(n // tm,),
        in_specs=[spec, pl.BlockSpec((1, LANES), lambda i: (0, 0))],
        out_specs=[spec, spec, spec],
        compiler_params=_cparams(("parallel",)),
        name="rope_tables",
    )(pos_b, inv_lane)


def _in_proj_kernel(x_ref, g_ref, w_ref, b_ref, c_ref, s1_ref, s2_ref, z_ref, h_ref):
    j = pl.program_id(1)

    @pl.when(j == 0)
    def _():
        x = x_ref[0]
        ms = jnp.mean(x * x, -1, keepdims=True)
        h_ref[...] = (x * lax.rsqrt(ms + EPS) * g_ref[...]).astype(BF16)

    acc = jnp.dot(h_ref[...], w_ref[...], preferred_element_type=F32) + b_ref[...]
    is_q = (j < COL_AK) | ((j >= COL_CQ) & (j < COL_CK))
    is_rope = (j < COL_AV) | ((j >= COL_CQ) & (j < COL_CV))

    @pl.when(is_rope)
    def _():
        scale = jnp.where(is_q, HEAD_DIM ** -0.5, 1.0).astype(F32)
        for t in range(IN_TN // LANES):
            a = acc[:, t * LANES:(t + 1) * LANES]
            r = (a * c_ref[...] + pltpu.roll(a, LANES - ROT_DIM // 2, 1) * s1_ref[...]
                 + pltpu.roll(a, ROT_DIM // 2, 1) * s2_ref[...])
            z_ref[0, :, t * LANES:(t + 1) * LANES] = (r * scale).astype(BF16)

    @pl.when(jnp.logical_not(is_rope))
    def _():
        z_ref[0] = acc.astype(BF16)


def _in_proj(x, g, w, b, tabs):
    bsz, s, d = x.shape
    c, s1, s2 = tabs
    tab_spec = pl.BlockSpec((s, LANES), lambda bi, j: (bi, 0))
    return pl.pallas_call(
        _in_proj_kernel,
        out_shape=jax.ShapeDtypeStruct((bsz, s, D_IN), BF16),
        grid=(bsz, N_COL_TILES),
        in_specs=[
            pl.BlockSpec((1, s, d), lambda bi, j: (bi, 0, 0)),
            pl.BlockSpec((1, d), lambda bi, j: (0, 0)),
            pl.BlockSpec((d, IN_TN), lambda bi, j: (0, j)),
            pl.BlockSpec((1, IN_TN), lambda bi, j: (0, j)),
            tab_spec, tab_spec, tab_spec,
        ],
        out_specs=pl.BlockSpec((1, s, IN_TN), lambda bi, j: (bi, 0, j)),
        scratch_shapes=[pltpu.VMEM((s, d), BF16)],
        compiler_params=_cparams(("parallel", "arbitrary")),
        name="in_proj",
    )(x, g.reshape(1, d), w, b.reshape(1, D_IN), c, s1, s2)


STACK = C_HEADS_PER_GROUP * BLOCK
STEP_UNROLL = 2


def _head_stack(blk):
    seg = lax.broadcasted_iota(jnp.int32, (1, MXU_W), 1) // HEAD_DIM
    rowseg = lax.broadcasted_iota(jnp.int32, (STACK, 1), 0) // BLOCK
    return jnp.where(seg == rowseg, jnp.concatenate([blk] * C_HEADS_PER_GROUP, axis=0),
                     jnp.zeros((), blk.dtype))


def _band_step(q, kst_ref, vst_ref, v_prev, blk, n, has_prev, diag_key, sinks):
    contract = (((1,), (1,)), ((), ()))
    ii = lax.broadcasted_iota(jnp.int32, (BLOCK, 1), 0)
    jj = lax.broadcasted_iota(jnp.int32, (1, BLOCK), 1)
    upper = jj > ii
    seg = lax.broadcasted_iota(jnp.int32, (1, MXU_W), 1) // HEAD_DIM
    zero_b = jnp.zeros((), BF16)
    s_cur = lax.dot_general(q, kst_ref[blk], contract, preferred_element_type=F32)
    if has_prev:
        prev = jnp.maximum(blk - 1, 0)
        s_prev = lax.dot_general(q, kst_ref[prev], contract, preferred_element_type=F32)
        bias = jnp.where(n == 0, NEG_INF, 0.0).astype(F32)
    p_cur, p_prev, p_diag, inv, lse = [], [], [], [], []
    for h in range(C_HEADS_PER_GROUP):
        cols = slice(h * BLOCK, (h + 1) * BLOCK)
        if has_prev:
            sp = s_prev[:, cols] + bias
            f = jnp.where(upper, sp, s_cur[:, cols])
        else:
            f = jnp.where(upper, NEG_INF, s_cur[:, cols])
        m = jnp.max(f, -1, keepdims=True)
        if diag_key:
            sd = jnp.sum(jnp.where(jj == ii, sp, 0.0), -1, keepdims=True)
            m = jnp.maximum(m, sd)
        if sinks is not None:
            m = jnp.maximum(m, sinks[h])
        p = jnp.exp(f - m)
        den = jnp.sum(p, -1, keepdims=True)
        if diag_key:
            pd = jnp.exp(sd - m)
            den = den + pd
            p_diag.append(pd)
        if sinks is not None:
            den = den + jnp.exp(sinks[h] - m)
        pb = p.astype(BF16)
        p_cur.append(jnp.where(upper, zero_b, pb))
        if has_prev:
            p_prev.append(jnp.where(upper, pb, zero_b))
        inv.append(1.0 / den)
        lse.append(m + jnp.log(den))
    o = jnp.dot(jnp.concatenate(p_cur, axis=1), vst_ref[blk], preferred_element_type=F32)
    if has_prev:
        o = o + jnp.dot(jnp.concatenate(p_prev, axis=1), vst_ref[prev], preferred_element_type=F32)

    def per_head(head_cols):
        out = head_cols[0]
        for h in range(1, C_HEADS_PER_GROUP):
            out = jnp.where(seg == h, head_cols[h], out)
        return out

    if diag_key:
        o = o + per_head(p_diag) * v_prev.astype(F32)
    return o * per_head(inv), lse, per_head


def _attn_a_kernel(sink_ref, q_ref, k_ref, v_ref, o_ref, kst_ref, vst_ref):
    nb = q_ref.shape[1] // BLOCK
    for g in range(A_KV_HEADS):
        kv_cols = slice(g * HEAD_DIM, (g + 1) * HEAD_DIM)
        q_cols = slice(g * MXU_W, (g + 1) * MXU_W)

        def build(n, carry):
            rows = pl.ds(pl.multiple_of(n * BLOCK, BLOCK), BLOCK)
            kst_ref[n] = _head_stack(jnp.concatenate([k_ref[0, rows, kv_cols]] * A_REP, axis=1))
            vst_ref[n] = _head_stack(jnp.concatenate([v_ref[0, rows, kv_cols]] * A_REP, axis=1))
            return carry

        lax.fori_loop(0, nb, build, 0)
        sinks = [sink_ref[g * A_REP + r] for r in range(A_REP)]

        def step(n, carry):
            rows = pl.ds(pl.multiple_of(n * BLOCK, BLOCK), BLOCK)
            o, _, _ = _band_step(q_ref[0, rows, q_cols], kst_ref, vst_ref, None, n, n,
                                 True, False, sinks)
            o_ref[0, rows, q_cols] = o.astype(o_ref.dtype)
            return carry

        lax.fori_loop(0, nb, step, 0, unroll=STEP_UNROLL)


def _attn_a(z, sinks):
    bsz, s, _ = z.shape
    nb = s // BLOCK
    grid_spec = pltpu.PrefetchScalarGridSpec(
        num_scalar_prefetch=1,
        grid=(bsz,),
        in_specs=[
            pl.BlockSpec((1, s, A_Q_W), lambda bi, sk: (bi, 0, 0)),
            pl.BlockSpec((1, s, A_KV_W), lambda bi, sk: (bi, 0, COL_AK)),
            pl.BlockSpec((1, s, A_KV_W), lambda bi, sk: (bi, 0, COL_AV)),
        ],
        out_specs=pl.BlockSpec((1, s, A_Q_W), lambda bi, sk: (bi, 0, 0)),
        scratch_shapes=[pltpu.VMEM((nb, STACK, MXU_W), BF16), pltpu.VMEM((nb, STACK, MXU_W), BF16)],
    )
    return pl.pallas_call(
        _attn_a_kernel,
        out_shape=jax.ShapeDtypeStruct((bsz, s, A_Q_W), BF16),
        grid_spec=grid_spec,
        compiler_params=_cparams(("parallel",)),
        name="attn_a",
    )(sinks, z, z, z)


def _attn_c_kernel(q_ref, k_ref, v_ref, oc_ref, stage_ref, pq_ref, pv_ref, kst_ref, vst_ref,
                   og_ref, lg_ref):
    s = q_ref.shape[1]
    nblk = s // BLOCK
    halves = MXU_W // LANES

    def class_rows(idx, nb, d):
        r = idx // nb
        n = idx - r * nb
        start = r + n * (BLOCK * d)
        rows = pl.ds(start, BLOCK, stride=d) if d > 1 else pl.ds(pl.multiple_of(start, BLOCK), BLOCK)
        return n, rows

    def stage(src_ref, cols):
        x = src_ref[0, :, cols].astype(F32)
        for t in range(halves):
            stage_ref[t] = x[:, t * LANES:(t + 1) * LANES]

    def staged_block(rows):
        return jnp.concatenate([stage_ref[t, rows, :] for t in range(halves)], axis=1).astype(BF16)

    for g, (win, d) in enumerate(C_PAIRS):
        cols = slice(g * MXU_W, (g + 1) * MXU_W)
        nb = nblk // d
        has_prev = nb > 1
        assert win // d == BLOCK

        def build_k(idx, carry):
            _, rows = class_rows(idx, nb, d)
            kst_ref[idx] = _head_stack(staged_block(rows))
            return carry

        def build_v(idx, carry):
            _, rows = class_rows(idx, nb, d)
            blk = staged_block(rows)
            pv_ref[pl.ds(pl.multiple_of(idx * BLOCK, BLOCK), BLOCK), :] = blk
            vst_ref[idx] = _head_stack(blk)
            return carry

        def build_q(idx, carry):
            _, rows = class_rows(idx, nb, d)
            pq_ref[pl.ds(pl.multiple_of(idx * BLOCK, BLOCK), BLOCK), :] = staged_block(rows)
            return carry

        stage(k_ref, cols)
        lax.fori_loop(0, nblk, build_k, 0)
        stage(v_ref, cols)
        lax.fori_loop(0, nblk, build_v, 0)
        stage(q_ref, cols)
        lax.fori_loop(0, nblk, build_q, 0)

        def step(idx, carry):
            n, rows = class_rows(idx, nb, d)
            here = pl.ds(pl.multiple_of(idx * BLOCK, BLOCK), BLOCK)
            before = pl.ds(pl.multiple_of(jnp.maximum(idx - 1, 0) * BLOCK, BLOCK), BLOCK)
            v_prev = pv_ref[before, :] if has_prev else None
            o, lse, per_head = _band_step(pq_ref[here, :], kst_ref, vst_ref, v_prev, idx, n,
                                          has_prev, has_prev, None)
            lse_full = per_head(lse) + jnp.zeros((BLOCK, MXU_W), F32)
            for t in range(halves):
                og_ref[g, t, rows, :] = o[:, t * LANES:(t + 1) * LANES]
                lg_ref[g, t, rows, :] = lse_full[:, t * LANES:(t + 1) * LANES]
            return carry

        lax.fori_loop(0, nblk, step, 0, unroll=STEP_UNROLL)

    chunk = 2 * BLOCK

    def combine(i, carry):
        rows = pl.ds(pl.multiple_of(i * chunk, chunk), chunk)
        for t in range(halves):
            ls = [lg_ref[g, t, rows, :] for g in range(len(C_PAIRS))]
            m = jnp.maximum(jnp.maximum(ls[0], ls[1]), ls[2])
            es = [jnp.exp(l - m) for l in ls]
            num = (es[0] * og_ref[0, t, rows, :] + es[1] * og_ref[1, t, rows, :]
                   + es[2] * og_ref[2, t, rows, :])
            oc_ref[0, rows, t * LANES:(t + 1) * LANES] = (
                num / (es[0] + es[1] + es[2])).astype(oc_ref.dtype)
        return carry

    lax.fori_loop(0, s // chunk, combine, 0)


def _attn_c(z):
    bsz, s, _ = z.shape
    nblk = s // BLOCK
    width = C_HEADS * HEAD_DIM
    halves = MXU_W // LANES

    def in_spec(col):
        return pl.BlockSpec((1, s, width), lambda bi: (bi, 0, col * IN_TN // width))

    return pl.pallas_call(
        _attn_c_kernel,
        out_shape=jax.ShapeDtypeStruct((bsz, s, C_OUT_W), BF16),
        grid=(bsz,),
        in_specs=[in_spec(COL_CQ), in_spec(COL_CK), in_spec(COL_CV)],
        out_specs=pl.BlockSpec((1, s, C_OUT_W), lambda bi: (bi, 0, 0)),
        scratch_shapes=[
            pltpu.VMEM((halves, s, LANES), F32),
            pltpu.VMEM((s, MXU_W), BF16),
            pltpu.VMEM((s, MXU_W), BF16),
            pltpu.VMEM((nblk, STACK, MXU_W), BF16),
            pltpu.VMEM((nblk, STACK, MXU_W), BF16),
            pltpu.VMEM((len(C_PAIRS), halves, s, LANES), F32),
            pltpu.VMEM((len(C_PAIRS), halves, s, LANES), F32),
        ],
        compiler_params=_cparams(("parallel",)),
        name="attn_c",
    )(z, z, z)


def _sgu_kernel(zu_ref, zv_ref, lng_ref, lnb_ref, ws_ref, bs_ref, o_ref):
    tm = zu_ref.shape[1]
    u = jax.nn.gelu(zu_ref[0].astype(F32))
    v = jax.nn.gelu(zv_ref[0].astype(F32))
    mu = jnp.mean(v, -1, keepdims=True)
    var = jnp.mean(jnp.square(v - mu), -1, keepdims=True)
    vn = ((v - mu) * lax.rsqrt(var + EPS) * lng_ref[...] + lnb_ref[...]).astype(BF16)
    causal = (lax.broadcasted_iota(jnp.int32, (B_CHUNK, 1), 0)
              >= lax.broadcasted_iota(jnp.int32, (1, B_CHUNK), 1))
    first = lax.broadcasted_iota(jnp.int32, (1, LANES), 1) < B_CH
    zero = jnp.zeros((), BF16)
    for p in range(B_WIDTH // LANES):
        w0 = jnp.where(causal, ws_ref[2 * p], zero)
        w1 = jnp.where(causal, ws_ref[2 * p + 1], zero)
        cols = slice(p * LANES, (p + 1) * LANES)
        for c in range(tm // B_CHUNK):
            rows = slice(c * B_CHUNK, (c + 1) * B_CHUNK)
            vv = vn[rows, cols]
            sv = jnp.where(first,
                           jnp.dot(w0, vv, preferred_element_type=F32),
                           jnp.dot(w1, vv, preferred_element_type=F32)) + bs_ref[:, cols]
            o_ref[0, rows, cols] = (u[rows, cols] * sv).astype(o_ref.dtype)


def _sgu(z, ln_g, ln_b, w_s, b_s):
    bsz, s, _ = z.shape
    tm = 512
    bias = jnp.repeat(b_s.T, B_CH, axis=1)
    col_u = COL_BU * IN_TN // B_WIDTH
    col_v = COL_BV * IN_TN // B_WIDTH
    return pl.pallas_call(
        _sgu_kernel,
        out_shape=jax.ShapeDtypeStruct((bsz, s, B_WIDTH), BF16),
        grid=(bsz, s // tm),
        in_specs=[
            pl.BlockSpec((1, tm, B_WIDTH), lambda bi, i: (bi, i, col_u)),
            pl.BlockSpec((1, tm, B_WIDTH), lambda bi, i: (bi, i, col_v)),
            pl.BlockSpec((1, B_WIDTH), lambda bi, i: (0, 0)),
            pl.BlockSpec((1, B_WIDTH), lambda bi, i: (0, 0)),
            pl.BlockSpec((B_GROUPS, B_CHUNK, B_CHUNK), lambda bi, i: (0, 0, 0)),
            pl.BlockSpec((B_CHUNK, B_WIDTH), lambda bi, i: (0, 0)),
        ],
        out_specs=pl.BlockSpec((1, tm, B_WIDTH), lambda bi, i: (bi, i, 0)),
        compiler_params=_cparams(("parallel", "parallel")),
        name="sgu",
    )(z, z, ln_g.reshape(1, B_WIDTH), ln_b.reshape(1, B_WIDTH), w_s.astype(BF16), bias)


def _route(logits):
    lane = lax.broadcasted_iota(jnp.int32, (1, ROUTE_W), 1)
    lane_f = lane.astype(F32)
    is_g = lane < N_EXPERT_GROUPS
    lg = jnp.where(is_g, logits, NEG_INF)
    mg = jnp.max(lg, -1, keepdims=True)
    pg_top = 1.0 / jnp.sum(jnp.where(is_g, jnp.exp(lg - mg), 0.0), -1, keepdims=True)
    g_idx = jnp.min(jnp.where(lg == mg, lane_f, float(ROUTE_W)), -1, keepdims=True)
    e_group = ((lane - ROUTE_E0) // EXPERTS_PER_GROUP).astype(F32)
    in_grp = (lane >= ROUTE_E0) & (lane < ROUTE_E0 + N_EXPERTS) & (e_group == g_idx)
    le = jnp.where(in_grp, logits, NEG_INF)
    m1 = jnp.max(le, -1, keepdims=True)
    i1 = jnp.min(jnp.where(le == m1, lane_f, float(ROUTE_W)), -1, keepdims=True)
    le2 = jnp.where(lane_f == i1, NEG_INF, le)
    m2 = jnp.max(le2, -1, keepdims=True)
    i2 = jnp.min(jnp.where(le2 == m2, lane_f, float(ROUTE_W)), -1, keepdims=True)
    t = jnp.exp(m2 - m1)
    w1 = pg_top / (1.0 + t)
    w2 = w1 * t
    picks = (i1 - ROUTE_E0, i2 - ROUTE_E0, w1, w2)
    out = jnp.zeros(logits.shape, F32)
    for k, val in enumerate(picks):
        out = jnp.where(lane == k, val, out)
    return out


def _merge_kernel(x_ref, oa_ref, ob_ref, oc_ref,
                  g0_ref, g1_ref, g2_ref, g3_ref, wa_ref, wb_ref, wc_ref, wo_ref, n2_ref, wr_ref,
                  xo_ref, h2_ref, route_ref):
    zg = jnp.concatenate([g0_ref[...], g1_ref[...], g2_ref[...], g3_ref[...]], axis=1).astype(F32)
    gates = jax.nn.sigmoid(zg)
    pa = jnp.dot(oa_ref[...], wa_ref[...], preferred_element_type=F32)
    pb = jnp.dot(ob_ref[...], wb_ref[...], preferred_element_type=F32)
    pc = jnp.dot(oc_ref[...], wc_ref[...], preferred_element_type=F32)
    merged = (gates[:, :D_MODEL] * pa + gates[:, D_MODEL:2 * D_MODEL] * pb
              + gates[:, 2 * D_MODEL:] * pc)
    xn = x_ref[...] + jnp.dot(merged.astype(BF16), wo_ref[...], preferred_element_type=F32)
    xo_ref[...] = xn
    ms = jnp.mean(xn * xn, -1, keepdims=True)
    h2 = xn * lax.rsqrt(ms + EPS) * n2_ref[...]
    h2_ref[...] = _pack_bf16_pairs(h2)
    route_ref[...] = _route(jnp.dot(h2.astype(BF16), wr_ref[...], preferred_element_type=F32))


def _merge(x2, oa, ob, oc, z2, wa, wb, wc, wo, n2g, w_rg, w_re):
    n, d = x2.shape
    tm = 512
    wr = jnp.concatenate(
        [w_rg, w_re, jnp.zeros((d, ROUTE_W - N_EXPERT_GROUPS - N_EXPERTS), w_rg.dtype)], axis=1)
    gate_w = GATE_W // 4
    gate_col0 = COL_G * IN_TN // gate_w

    def rows(width):
        return pl.BlockSpec((tm, width), lambda i: (i, 0))

    def full(shape):
        return pl.BlockSpec(shape, lambda i: (0, 0))

    gate_specs = [pl.BlockSpec((tm, gate_w), lambda i, k=k: (i, gate_col0 + k)) for k in range(4)]
    return pl.pallas_call(
        _merge_kernel,
        out_shape=[jax.ShapeDtypeStruct((n, d), F32),
                   jax.ShapeDtypeStruct((n, d // 2), jnp.int32),
                   jax.ShapeDtypeStruct((n, ROUTE_W), F32)],
        grid=(n // tm,),
        in_specs=[rows(d), rows(A_Q_W), rows(B_WIDTH), rows(C_OUT_W),
                  *gate_specs,
                  full((A_Q_W, d)), full((B_WIDTH, d)), full((C_OUT_W, d)), full((d, d)),
                  full((1, d)), full((d, ROUTE_W))],
        out_specs=[rows(d), rows(d // 2), rows(ROUTE_W)],
        compiler_params=_cparams(("parallel",)),
        name="merge",
    )(x2, oa, ob, oc, z2, z2, z2, z2,
      wa.astype(BF16), wb.astype(BF16), wc.astype(BF16), wo.astype(BF16),
      n2g.reshape(1, d), wr.astype(BF16))


MOE_TM = 256
SC_CORES = 2
SC_SUBCORES = 16
SC_WORKERS = SC_CORES * SC_SUBCORES
SC_CHUNK = 64


def _pack_bf16_pairs(x):
    w = x.shape[1] // 2
    lo = pltpu.bitcast(x[:, :w].astype(BF16).astype(F32), jnp.int32)
    hi = pltpu.bitcast(x[:, w:].astype(BF16).astype(F32), jnp.int32)
    return (hi & jnp.int32(-65536)) | lax.shift_right_logical(lo, jnp.int32(16))


def _unpack_bf16_pairs(p):
    lo = pltpu.bitcast(lax.shift_left(p, jnp.int32(16)), F32)
    hi = pltpu.bitcast(p & jnp.int32(-65536), F32)
    return jnp.concatenate([lo, hi], axis=1)


def _sc_gather_rows(table, idx):
    rows, width = idx.shape[0], table.shape[1]
    per_worker = rows // SC_WORKERS
    n_chunks = per_worker // SC_CHUNK
    assert per_worker * SC_WORKERS == rows and n_chunks * SC_CHUNK == per_worker
    mesh = plsc.VectorSubcoreMesh(core_axis_name="c", subcore_axis_name="s",
                                  num_cores=SC_CORES, num_subcores=SC_SUBCORES)

    @functools.partial(
        pl.kernel, mesh=mesh,
        out_type=jax.ShapeDtypeStruct((rows, width), table.dtype),
        scratch_types=[pltpu.VMEM((n_chunks, SC_CHUNK), jnp.int32),
                       pltpu.VMEM((SC_CHUNK, width), table.dtype),
                       pltpu.SemaphoreType.DMA],
        name="sc_gather_rows",
    )
    def gather(table_hbm, idx_hbm, out_hbm, idx_v, rows_v, sem):
        wid = lax.axis_index("s") * SC_CORES + lax.axis_index("c")
        pltpu.sync_copy(idx_hbm.at[wid], idx_v)
        base = wid * per_worker

        @pl.loop(0, n_chunks)
        def _(c):
            pltpu.async_copy(table_hbm.at[idx_v.at[c]], rows_v, sem).wait()
            pltpu.sync_copy(rows_v, out_hbm.at[pl.ds(base + c * SC_CHUNK, SC_CHUNK)])

    return gather(table, idx.reshape(SC_WORKERS, n_chunks, SC_CHUNK))


def _dispatch_plan(route, n_slots):
    n = route.shape[0]
    e = route[:, 0:2].astype(jnp.int32).reshape(-1)
    onehot = (e[:, None] == jnp.arange(N_EXPERTS, dtype=jnp.int32)[None, :]).astype(jnp.int32)
    csum = jnp.cumsum(onehot, axis=0)
    rank = jnp.sum((csum - onehot) * onehot, axis=1)
    counts = csum[-1]
    padded = (counts + MOE_TM - 1) // MOE_TM * MOE_TM
    ends = jnp.cumsum(padded)
    pos = (ends - padded)[e] + rank
    token = jnp.arange(2 * n, dtype=jnp.int32) // 2
    src = jnp.zeros((n_slots,), jnp.int32).at[pos].set(token, unique_indices=True)
    tile_start = jnp.arange(n_slots // MOE_TM, dtype=jnp.int32) * MOE_TM
    tile_expert = jnp.minimum(jnp.searchsorted(ends, tile_start, side="right"), N_EXPERTS - 1)
    tile_valid = (tile_start < ends[-1]).astype(jnp.int32)
    first = jnp.concatenate([jnp.ones((1,), jnp.int32),
                             (tile_expert[1:] != tile_expert[:-1]).astype(jnp.int32)])
    return src, pos.reshape(n, 2).T.reshape(-1), tile_expert.astype(jnp.int32), tile_valid, first


def _moe_tile_kernel(te_ref, tv_ref, tf_ref, xs_ref, wg_ref, wu_ref, wd_ref, ys_ref,
                     wg_s, wu_s, wd_s):
    t = pl.program_id(0)

    @pl.when(tf_ref[t] != 0)
    def _():
        wg_s[...] = wg_ref[0].astype(BF16)
        wu_s[...] = wu_ref[0].astype(BF16)
        wd_s[...] = wd_ref[0].astype(BF16)

    @pl.when(tv_ref[t] != 0)
    def _():
        x = _unpack_bf16_pairs(xs_ref[...]).astype(BF16)
        hg = jnp.dot(x, wg_s[...], preferred_element_type=F32)
        hu = jnp.dot(x, wu_s[...], preferred_element_type=F32)
        a = (jax.nn.silu(hg) * hu).astype(BF16)
        ys_ref[...] = _pack_bf16_pairs(jnp.dot(a, wd_s[...], preferred_element_type=F32))

    @pl.when(tv_ref[t] == 0)
    def _():
        ys_ref[...] = jnp.zeros_like(ys_ref)


def _moe_tiles(xs, tile_expert, tile_valid, tile_first, w_gate, w_up, w_down):
    n_slots, half = xs.shape
    d = 2 * half
    grid_spec = pltpu.PrefetchScalarGridSpec(
        num_scalar_prefetch=3,
        grid=(n_slots // MOE_TM,),
        in_specs=[
            pl.BlockSpec((MOE_TM, half), lambda t, te, tv, tf: (t, 0)),
            pl.BlockSpec((1, d, D_EXPERT), lambda t, te, tv, tf: (te[t], 0, 0)),
            pl.BlockSpec((1, d, D_EXPERT), lambda t, te, tv, tf: (te[t], 0, 0)),
            pl.BlockSpec((1, D_EXPERT, d), lambda t, te, tv, tf: (te[t], 0, 0)),
        ],
        out_specs=pl.BlockSpec((MOE_TM, half), lambda t, te, tv, tf: (t, 0)),
        scratch_shapes=[pltpu.VMEM((d, D_EXPERT), BF16), pltpu.VMEM((d, D_EXPERT), BF16),
                        pltpu.VMEM((D_EXPERT, d), BF16)],
    )
    return pl.pallas_call(
        _moe_tile_kernel,
        out_shape=jax.ShapeDtypeStruct((n_slots, half), jnp.int32),
        grid_spec=grid_spec,
        compiler_params=_cparams(("arbitrary",)),
        name="moe_tiles",
    )(tile_expert, tile_valid, tile_first, xs, w_gate, w_up, w_down)


def _moe_combine_kernel(x_ref, y0_ref, y1_ref, r_ref, o_ref):
    w0 = r_ref[:, 2:3]
    w1 = r_ref[:, 3:4]
    o_ref[...] = (x_ref[...] + w0 * _unpack_bf16_pairs(y0_ref[0])
                  + w1 * _unpack_bf16_pairs(y1_ref[0]))


def _moe_combine(x2, yg, route):
    n, d = x2.shape
    tm = 1024
    return pl.pallas_call(
        _moe_combine_kernel,
        out_shape=jax.ShapeDtypeStruct((n, d), F32),
        grid=(n // tm,),
        in_specs=[
            pl.BlockSpec((tm, d), lambda i: (i, 0)),
            pl.BlockSpec((1, tm, d // 2), lambda i: (0, i, 0)),
            pl.BlockSpec((1, tm, d // 2), lambda i: (1, i, 0)),
            pl.BlockSpec((tm, ROUTE_W), lambda i: (i, 0)),
        ],
        out_specs=pl.BlockSpec((tm, d), lambda i: (i, 0)),
        compiler_params=_cparams(("parallel",)),
        name="moe_combine",
    )(x2, yg, yg, route)


def _moe(h2p, route, x2, w_gate, w_up, w_down):
    n, d = x2.shape
    n_slots = 2 * n + N_EXPERTS * MOE_TM
    src, pos, tile_expert, tile_valid, tile_first = _dispatch_plan(route, n_slots)
    xs = _sc_gather_rows(h2p, src)
    ys = _moe_tiles(xs, tile_expert, tile_valid, tile_first, w_gate, w_up, w_down)
    yg = _sc_gather_rows(ys, pos).reshape(2, n, d // 2)
    return _moe_combine(x2, yg, route)


def _final_norm_kernel(x_ref, g_ref, o_ref):
    x = x_ref[...]
    ms = jnp.mean(x * x, -1, keepdims=True)
    o_ref[...] = x * lax.rsqrt(ms + EPS) * g_ref[...]


def _final_norm(x2, g):
    n, d = x2.shape
    tm = 1024
    return pl.pallas_call(
        _final_norm_kernel,
        out_shape=jax.ShapeDtypeStruct((n, d), F32),
        grid=(n // tm,),
        in_specs=[pl.BlockSpec((tm, d), lambda i: (i, 0)), pl.BlockSpec((1, d), lambda i: (0, 0))],
        out_specs=pl.BlockSpec((tm, d), lambda i: (i, 0)),
        compiler_params=_cparams(("parallel",)),
        name="final_norm",
    )(x2, g.reshape(1, d))


def kernel(x, positions, norm1_g, w_in, b_in, attn_sinks, sgu_ln_g, sgu_ln_b, w_spatial, b_spatial,
           w_proj_a, w_proj_b, w_proj_c, w_out, norm2_g, w_router_group, w_router_expert,
           w_expert_gate, w_expert_up, w_expert_down, norm_f_g):
    bsz, s, d = x.shape
    depth = w_in.shape[0]
    assert d == D_MODEL and s % (C_PAIRS[-1][1] * BLOCK) == 0
    tabs = _rope_tables(positions)
    for l in range(depth):
        z = _in_proj(x, norm1_g[l], w_in[l].astype(BF16), b_in[l], tabs)
        oa = _attn_a(z, attn_sinks[l])
        ob = _sgu(z, sgu_ln_g[l], sgu_ln_b[l], w_spatial[l], b_spatial[l])
        oc = _attn_c(z)
        x2, h2, route = _merge(
            x.reshape(bsz * s, d), oa.reshape(bsz * s, A_Q_W), ob.reshape(bsz * s, B_WIDTH),
            oc.reshape(bsz * s, C_OUT_W), z.reshape(bsz * s, D_IN), w_proj_a[l], w_proj_b[l], w_proj_c[l], w_out[l],
            norm2_g[l], w_router_group[l], w_router_expert[l])
        x = _moe(h2, route, x2, w_expert_gate[l], w_expert_up[l], w_expert_down[l]).reshape(bsz, s, d)
    return _final_norm(x.reshape(bsz * s, d), norm_f_g).reshape(bsz, s, d)
```

```python
import functools

import jax
import jax.numpy as jnp
from jax import lax
from jax.experimental import pallas as pl
from jax.experimental.pallas import tpu as pltpu
from jax.experimental.pallas import tpu_sc as plsc

F32 = jnp.float32
BF16 = jnp.bfloat16

D_MODEL = 1024
HEAD_DIM = 64
ROT_DIM = HEAD_DIM // 4
ROPE_THETA = 500000.0
BLOCK = 128
EPS = 1e-5
NEG_INF = -1e30

A_Q_HEADS = 16
A_KV_HEADS = 4
A_REP = A_Q_HEADS // A_KV_HEADS
A_WINDOW = 128
A_Q_W = A_Q_HEADS * HEAD_DIM
A_KV_W = A_KV_HEADS * HEAD_DIM

B_GROUPS = 12
B_CH = 64
B_WIDTH = B_GROUPS * B_CH
B_CHUNK = 128

C_PAIRS = ((128, 1), (512, 4), (2048, 16))
C_HEADS_PER_GROUP = 4
C_HEADS = C_HEADS_PER_GROUP * len(C_PAIRS)
C_OUT_W = C_HEADS_PER_GROUP * HEAD_DIM

N_BRANCH = 3
GATE_W = N_BRANCH * D_MODEL
D_IN = A_Q_W + 2 * A_KV_W + 2 * B_WIDTH + 3 * C_HEADS * HEAD_DIM + GATE_W

N_EXPERT_GROUPS = 4
EXPERTS_PER_GROUP = 8
N_EXPERTS = N_EXPERT_GROUPS * EXPERTS_PER_GROUP
TOP_K = 2
D_EXPERT = 256

LANES = 128
MXU_W = 256
VMEM_LIMIT = 56 * 1024 * 1024

IN_TN = 256
COL_AQ = 0
COL_AK = A_Q_W // IN_TN
COL_AV = COL_AK + A_KV_W // IN_TN
COL_BU = COL_AV + A_KV_W // IN_TN
COL_BV = COL_BU + B_WIDTH // IN_TN
COL_CQ = COL_BV + B_WIDTH // IN_TN
COL_CK = COL_CQ + C_HEADS * HEAD_DIM // IN_TN
COL_CV = COL_CK + C_HEADS * HEAD_DIM // IN_TN
COL_G = COL_CV + C_HEADS * HEAD_DIM // IN_TN
N_COL_TILES = D_IN // IN_TN

ROUTE_W = LANES
ROUTE_E0 = N_EXPERT_GROUPS


def _cparams(sem):
    return pltpu.CompilerParams(dimension_semantics=sem, vmem_limit_bytes=VMEM_LIMIT)


def _rope_table_kernel(pos_ref, inv_ref, c_ref, s1_ref, s2_ref):
    lane = lax.broadcasted_iota(jnp.int32, (1, LANES), 1)
    d = lane & (HEAD_DIM - 1)
    ang = pos_ref[...].astype(F32) * inv_ref[...]
    c = jnp.cos(ang)
    s = jnp.sin(ang)
    half = ROT_DIM // 2
    c_ref[...] = jnp.where(d < ROT_DIM, c, 1.0)
    s1_ref[...] = jnp.where(d < half, -s, 0.0)
    s2_ref[...] = jnp.where((d >= half) & (d < ROT_DIM), s, 0.0)


def _rope_tables(positions):
    n = positions.size
    inv = ROPE_THETA ** (-jnp.arange(0, ROT_DIM, 2, dtype=F32) / ROT_DIM)
    inv_lane = jnp.tile(inv, LANES // inv.shape[0]).reshape(1, LANES)
    pos_b = jnp.broadcast_to(positions.reshape(n, 1), (n, LANES))
    tm = 1024
    spec = pl.BlockSpec((tm, LANES), lambda i: (i, 0))
    return pl.pallas_call(
        _rope_table_kernel,
        out_shape=[jax.ShapeDtypeStruct((n, LANES), F32)] * 3,
        grid=(n // tm,),
        in_specs=[spec, pl.BlockSpec((1, LANES), lambda i: (0, 0))],
        out_specs=[spec, spec, spec],
        compiler_params=_cparams(("parallel",)),
        name="rope_tables",
    )(pos_b, inv_lane)


def _in_proj_kernel(x_ref, g_ref, w_ref, b_ref, c_ref, s1_ref, s2_ref, z_ref, h_ref):
    j = pl.program_id(1)

    @pl.when(j == 0)
    def _():
        x = x_ref[0]
        ms = jnp.mean(x * x, -1, keepdims=True)
        h_ref[...] = (x * lax.rsqrt(ms + EPS) * g_ref[...]).astype(BF16)

    acc = jnp.dot(h_ref[...], w_ref[...], preferred_element_type=F32) + b_ref[...]
    is_q = (j < COL_AK) | ((j >= COL_CQ) & (j < COL_CK))
    is_rope = (j < COL_AV) | ((j >= COL_CQ) & (j < COL_CV))

    @pl.when(is_rope)
    def _():
        scale = jnp.where(is_q, HEAD_DIM ** -0.5, 1.0).astype(F32)
        for t in range(IN_TN // LANES):
            a = acc[:, t * LANES:(t + 1) * LANES]
            r = (a * c_ref[...] + pltpu.roll(a, LANES - ROT_DIM // 2, 1) * s1_ref[...]
                 + pltpu.roll(a, ROT_DIM // 2, 1) * s2_ref[...])
            z_ref[0, :, t * LANES:(t + 1) * LANES] = (r * scale).astype(BF16)

    @pl.when(jnp.logical_not(is_rope))
    def _():
        z_ref[0] = acc.astype(BF16)


def _in_proj(x, g, w, b, tabs):
    bsz, s, d = x.shape
    c, s1, s2 = tabs
    tab_spec = pl.BlockSpec((s, LANES), lambda bi, j: (bi, 0))
    return pl.pallas_call(
        _in_proj_kernel,
        out_shape=jax.ShapeDtypeStruct((bsz, s, D_IN), BF16),
        grid=(bsz, N_COL_TILES),
        in_specs=[
            pl.BlockSpec((1, s, d), lambda bi, j: (bi, 0, 0)),
            pl.BlockSpec((1, d), lambda bi, j: (0, 0)),
            pl.BlockSpec((d, IN_TN), lambda bi, j: (0, j)),
            pl.BlockSpec((1, IN_TN), lambda bi, j: (0, j)),
            tab_spec, tab_spec, tab_spec,
        ],
        out_specs=pl.BlockSpec((1, s, IN_TN), lambda bi, j: (bi, 0, j)),
        scratch_shapes=[pltpu.VMEM((s, d), BF16)],
        compiler_params=_cparams(("parallel", "arbitrary")),
        name="in_proj",
    )(x, g.reshape(1, d), w, b.reshape(1, D_IN), c, s1, s2)


STACK = C_HEADS_PER_GROUP * BLOCK
STEP_UNROLL = 2


def _head_stack(blk):
    seg = lax.broadcasted_iota(jnp.int32, (1, MXU_W), 1) // HEAD_DIM
    rowseg = lax.broadcasted_iota(jnp.int32, (STACK, 1), 0) // BLOCK
    return jnp.where(seg == rowseg, jnp.concatenate([blk] * C_HEADS_PER_GROUP, axis=0),
                     jnp.zeros((), blk.dtype))


def _band_step(q, kst_ref, vst_ref, v_prev, blk, n, has_prev, diag_key, sinks):
    contract = (((1,), (1,)), ((), ()))
    ii = lax.broadcasted_iota(jnp.int32, (BLOCK, 1), 0)
    jj = lax.broadcasted_iota(jnp.int32, (1, BLOCK), 1)
    upper = jj > ii
    seg = lax.broadcasted_iota(jnp.int32, (1, MXU_W), 1) // HEAD_DIM
    zero_b = jnp.zeros((), BF16)
    s_cur = lax.dot_general(q, kst_ref[blk], contract, preferred_element_type=F32)
    if has_prev:
        prev = jnp.maximum(blk - 1, 0)
        s_prev = lax.dot_general(q, kst_ref[prev], contract, preferred_element_type=F32)
        bias = jnp.where(n == 0, NEG_INF, 0.0).astype(F32)
    p_cur, p_prev, p_diag, inv, lse = [], [], [], [], []
    for h in range(C_HEADS_PER_GROUP):
        cols = slice(h * BLOCK, (h + 1) * BLOCK)
        if has_prev:
            sp = s_prev[:, cols] + bias
            f = jnp.where(upper, sp, s_cur[:, cols])
        else:
            f = jnp.where(upper, NEG_INF, s_cur[:, cols])
        m = jnp.max(f, -1, keepdims=True)
        if diag_key:
            sd = jnp.sum(jnp.where(jj == ii, sp, 0.0), -1, keepdims=True)
            m = jnp.maximum(m, sd)
        if sinks is not None:
            m = jnp.maximum(m, sinks[h])
        p = jnp.exp(f - m)
        den = jnp.sum(p, -1, keepdims=True)
        if diag_key:
            pd = jnp.exp(sd - m)
            den = den + pd
            p_diag.append(pd)
        if sinks is not None:
            den = den + jnp.exp(sinks[h] - m)
        pb = p.astype(BF16)
        p_cur.append(jnp.where(upper, zero_b, pb))
        if has_prev:
            p_prev.append(jnp.where(upper, pb, zero_b))
        inv.append(1.0 / den)
        lse.append(m + jnp.log(den))
    o = jnp.dot(jnp.concatenate(p_cur, axis=1), vst_ref[blk], preferred_element_type=F32)
    if has_prev:
        o = o + jnp.dot(jnp.concatenate(p_prev, axis=1), vst_ref[prev], preferred_element_type=F32)

    def per_head(head_cols):
        out = head_cols[0]
        for h in range(1, C_HEADS_PER_GROUP):
            out = jnp.where(seg == h, head_cols[h], out)
        return out

    if diag_key:
        o = o + per_head(p_diag) * v_prev.astype(F32)
    return o * per_head(inv), lse, per_head


def _attn_a_kernel(sink_ref, q_ref, k_ref, v_ref, o_ref, kst_ref, vst_ref):
    nb = q_ref.shape[1] // BLOCK
    for g in range(A_KV_HEADS):
        kv_cols = slice(g * HEAD_DIM, (g + 1) * HEAD_DIM)
        q_cols = slice(g * MXU_W, (g + 1) * MXU_W)

        def build(n, carry):
            rows = pl.ds(pl.multiple_of(n * BLOCK, BLOCK), BLOCK)
            kst_ref[n] = _head_stack(jnp.concatenate([k_ref[0, rows, kv_cols]] * A_REP, axis=1))
            vst_ref[n] = _head_stack(jnp.concatenate([v_ref[0, rows, kv_cols]] * A_REP, axis=1))
            return carry

        lax.fori_loop(0, nb, build, 0)
        sinks = [sink_ref[g * A_REP + r] for r in range(A_REP)]

        def step(n, carry):
            rows = pl.ds(pl.multiple_of(n * BLOCK, BLOCK), BLOCK)
            o, _, _ = _band_step(q_ref[0, rows, q_cols], kst_ref, vst_ref, None, n, n,
                                 True, False, sinks)
            o_ref[0, rows, q_cols] = o.astype(o_ref.dtype)
            return carry

        lax.fori_loop(0, nb, step, 0, unroll=STEP_UNROLL)


def _attn_a(z, sinks):
    bsz, s, _ = z.shape
    nb = s // BLOCK
    grid_spec = pltpu.PrefetchScalarGridSpec(
        num_scalar_prefetch=1,
        grid=(bsz,),
        in_specs=[
            pl.BlockSpec((1, s, A_Q_W), lambda bi, sk: (bi, 0, 0)),
            pl.BlockSpec((1, s, A_KV_W), lambda bi, sk: (bi, 0, COL_AK)),
            pl.BlockSpec((1, s, A_KV_W), lambda bi, sk: (bi, 0, COL_AV)),
        ],
        out_specs=pl.BlockSpec((1, s, A_Q_W), lambda bi, sk: (bi, 0, 0)),
        scratch_shapes=[pltpu.VMEM((nb, STACK, MXU_W), BF16), pltpu.VMEM((nb, STACK, MXU_W), BF16)],
    )
    return pl.pallas_call(
        _attn_a_kernel,
        out_shape=jax.ShapeDtypeStruct((bsz, s, A_Q_W), BF16),
        grid_spec=grid_spec,
        compiler_params=_cparams(("parallel",)),
        name="attn_a",
    )(sinks, z, z, z)


def _attn_c_kernel(q_ref, k_ref, v_ref, oc_ref, stage_ref, pq_ref, pv_ref, kst_ref, vst_ref,
                   og_ref, lg_ref):
    s = q_ref.shape[1]
    nblk = s // BLOCK
    halves = MXU_W // LANES

    def class_rows(idx, nb, d):
        r = idx // nb
        n = idx - r * nb
        start = r + n * (BLOCK * d)
        rows = pl.ds(start, BLOCK, stride=d) if d > 1 else pl.ds(pl.multiple_of(start, BLOCK), BLOCK)
        return n, rows

    def stage(src_ref, cols):
        x = src_ref[0, :, cols].astype(F32)
        for t in range(halves):
            stage_ref[t] = x[:, t * LANES:(t + 1) * LANES]

    def staged_block(rows):
        return jnp.concatenate([stage_ref[t, rows, :] for t in range(halves)], axis=1).astype(BF16)

    for g, (win, d) in enumerate(C_PAIRS):
        cols = slice(g * MXU_W, (g + 1) * MXU_W)
        nb = nblk // d
        has_prev = nb > 1
        assert win // d == BLOCK

        def build_k(idx, carry):
            _, rows = class_rows(idx, nb, d)
            kst_ref[idx] = _head_stack(staged_block(rows))
            return carry

        def build_v(idx, carry):
            _, rows = class_rows(idx, nb, d)
            blk = staged_block(rows)
            pv_ref[pl.ds(pl.multiple_of(idx * BLOCK, BLOCK), BLOCK), :] = blk
            vst_ref[idx] = _head_stack(blk)
            return carry

        def build_q(idx, carry):
            _, rows = class_rows(idx, nb, d)
            pq_ref[pl.ds(pl.multiple_of(idx * BLOCK, BLOCK), BLOCK), :] = staged_block(rows)
            return carry

        stage(k_ref, cols)
        lax.fori_loop(0, nblk, build_k, 0)
        stage(v_ref, cols)
        lax.fori_loop(0, nblk, build_v, 0)
        stage(q_ref, cols)
        lax.fori_loop(0, nblk, build_q, 0)

        def step(idx, carry):
            n, rows = class_rows(idx, nb, d)
            here = pl.ds(pl.multiple_of(idx * BLOCK, BLOCK), BLOCK)
            before = pl.ds(pl.multiple_of(jnp.maximum(idx - 1, 0) * BLOCK, BLOCK), BLOCK)
            v_prev = pv_ref[before, :] if has_prev else None
            o, lse, per_head = _band_step(pq_ref[here, :], kst_ref, vst_ref, v_prev, idx, n,
                                          has_prev, has_prev, None)
            lse_full = per_head(lse) + jnp.zeros((BLOCK, MXU_W), F32)
            for t in range(halves):
                og_ref[g, t, rows, :] = o[:, t * LANES:(t + 1) * LANES]
                lg_ref[g, t, rows, :] = lse_full[:, t * LANES:(t + 1) * LANES]
            return carry

        lax.fori_loop(0, nblk, step, 0, unroll=STEP_UNROLL)

    chunk = 2 * BLOCK

    def combine(i, carry):
        rows = pl.ds(pl.multiple_of(i * chunk, chunk), chunk)
        for t in range(halves):
            ls = [lg_ref[g, t, rows, :] for g in range(len(C_PAIRS))]
            m = jnp.maximum(jnp.maximum(ls[0], ls[1]), ls[2])
            es = [jnp.exp(l - m) for l in ls]
            num = (es[0] * og_ref[0, t, rows, :] + es[1] * og_ref[1, t, rows, :]
                   + es[2] * og_ref[2, t, rows, :])
            oc_ref[0, rows, t * LANES:(t + 1) * LANES] = (
                num / (es[0] + es[1] + es[2])).astype(oc_ref.dtype)
        return carry

    lax.fori_loop(0, s // chunk, combine, 0)


def _attn_c(z):
    bsz, s, _ = z.shape
    nblk = s // BLOCK
    width = C_HEADS * HEAD_DIM
    halves = MXU_W // LANES

    def in_spec(col):
        return pl.BlockSpec((1, s, width), lambda bi: (bi, 0, col * IN_TN // width))

    return pl.pallas_call(
        _attn_c_kernel,
        out_shape=jax.ShapeDtypeStruct((bsz, s, C_OUT_W), BF16),
        grid=(bsz,),
        in_specs=[in_spec(COL_CQ), in_spec(COL_CK), in_spec(COL_CV)],
        out_specs=pl.BlockSpec((1, s, C_OUT_W), lambda bi: (bi, 0, 0)),
        scratch_shapes=[
            pltpu.VMEM((halves, s, LANES), F32),
            pltpu.VMEM((s, MXU_W), BF16),
            pltpu.VMEM((s, MXU_W), BF16),
            pltpu.VMEM((nblk, STACK, MXU_W), BF16),
            pltpu.VMEM((nblk, STACK, MXU_W), BF16),
            pltpu.VMEM((len(C_PAIRS), halves, s, LANES), F32),
            pltpu.VMEM((len(C_PAIRS), halves, s, LANES), F32),
        ],
        compiler_params=_cparams(("parallel",)),
        name="attn_c",
    )(z, z, z)


def _sgu_kernel(zu_ref, zv_ref, lng_ref, lnb_ref, ws_ref, bs_ref, o_ref):
    tm = zu_ref.shape[1]
    u = jax.nn.gelu(zu_ref[0].astype(F32))
    v = jax.nn.gelu(zv_ref[0].astype(F32))
    mu = jnp.mean(v, -1, keepdims=True)
    var = jnp.mean(jnp.square(v - mu), -1, keepdims=True)
    vn = ((v - mu) * lax.rsqrt(var + EPS) * lng_ref[...] + lnb_ref[...]).astype(BF16)
    causal = (lax.broadcasted_iota(jnp.int32, (B_CHUNK, 1), 0)
              >= lax.broadcasted_iota(jnp.int32, (1, B_CHUNK), 1))
    first = lax.broadcasted_iota(jnp.int32, (1, LANES), 1) < B_CH
    zero = jnp.zeros((), BF16)
    for p in range(B_WIDTH // LANES):
        w0 = jnp.where(causal, ws_ref[2 * p], zero)
        w1 = jnp.where(causal, ws_ref[2 * p + 1], zero)
        cols = slice(p * LANES, (p + 1) * LANES)
        for c in range(tm // B_CHUNK):
            rows = slice(c * B_CHUNK, (c + 1) * B_CHUNK)
            vv = vn[rows, cols]
            sv = jnp.where(first,
                           jnp.dot(w0, vv, preferred_element_type=F32),
                           jnp.dot(w1, vv, preferred_element_type=F32)) + bs_ref[:, cols]
            o_ref[0, rows, cols] = (u[rows, cols] * sv).astype(o_ref.dtype)


def _sgu(z, ln_g, ln_b, w_s, b_s):
    bsz, s, _ = z.shape
    tm = 512
    bias = jnp.repeat(b_s.T, B_CH, axis=1)
    col_u = COL_BU * IN_TN // B_WIDTH
    col_v = COL_BV * IN_TN // B_WIDTH
    return pl.pallas_call(
        _sgu_kernel,
        out_shape=jax.ShapeDtypeStruct((bsz, s, B_WIDTH), BF16),
        grid=(bsz, s // tm),
        in_specs=[
            pl.BlockSpec((1, tm, B_WIDTH), lambda bi, i: (bi, i, col_u)),
            pl.BlockSpec((1, tm, B_WIDTH), lambda bi, i: (bi, i, col_v)),
            pl.BlockSpec((1, B_WIDTH), lambda bi, i: (0, 0)),
            pl.BlockSpec((1, B_WIDTH), lambda bi, i: (0, 0)),
            pl.BlockSpec((B_GROUPS, B_CHUNK, B_CHUNK), lambda bi, i: (0, 0, 0)),
            pl.BlockSpec((B_CHUNK, B_WIDTH), lambda bi, i: (0, 0)),
        ],
        out_specs=pl.BlockSpec((1, tm, B_WIDTH), lambda bi, i: (bi, i, 0)),
        compiler_params=_cparams(("parallel", "parallel")),
        name="sgu",
    )(z, z, ln_g.reshape(1, B_WIDTH), ln_b.reshape(1, B_WIDTH), w_s.astype(BF16), bias)


def _route(logits):
    lane = lax.broadcasted_iota(jnp.int32, (1, ROUTE_W), 1)
    lane_f = lane.astype(F32)
    is_g = lane < N_EXPERT_GROUPS
    lg = jnp.where(is_g, logits, NEG_INF)
    mg = jnp.max(lg, -1, keepdims=True)
    pg_top = 1.0 / jnp.sum(jnp.where(is_g, jnp.exp(lg - mg), 0.0), -1, keepdims=True)
    g_idx = jnp.min(jnp.where(lg == mg, lane_f, float(ROUTE_W)), -1, keepdims=True)
    e_group = ((lane - ROUTE_E0) // EXPERTS_PER_GROUP).astype(F32)
    in_grp = (lane >= ROUTE_E0) & (lane < ROUTE_E0 + N_EXPERTS) & (e_group == g_idx)
    le = jnp.where(in_grp, logits, NEG_INF)
    m1 = jnp.max(le, -1, keepdims=True)
    i1 = jnp.min(jnp.where(le == m1, lane_f, float(ROUTE_W)), -1, keepdims=True)
    le2 = jnp.where(lane_f == i1, NEG_INF, le)
    m2 = jnp.max(le2, -1, keepdims=True)
    i2 = jnp.min(jnp.where(le2 == m2, lane_f, float(ROUTE_W)), -1, keepdims=True)
    t = jnp.exp(m2 - m1)
    w1 = pg_top / (1.0 + t)
    w2 = w1 * t
    picks = (i1 - ROUTE_E0, i2 - ROUTE_E0, w1, w2)
    out = jnp.zeros(logits.shape, F32)
    for k, val in enumerate(picks):
        out = jnp.where(lane == k, val, out)
    return out


def _merge_kernel(x_ref, oa_ref, ob_ref, oc_ref,
                  g0_ref, g1_ref, g2_ref, g3_ref, wa_ref, wb_ref, wc_ref, wo_ref, n2_ref, wr_ref,
                  xo_ref, h2_ref, route_ref):
    zg = jnp.concatenate([g0_ref[...], g1_ref[...], g2_ref[...], g3_ref[...]], axis=1).astype(F32)
    gates = jax.nn.sigmoid(zg)
    pa = jnp.dot(oa_ref[...], wa_ref[...], preferred_element_type=F32)
    pb = jnp.dot(ob_ref[...], wb_ref[...], preferred_element_type=F32)
    pc = jnp.dot(oc_ref[...], wc_ref[...], preferred_element_type=F32)
    merged = (gates[:, :D_MODEL] * pa + gates[:, D_MODEL:2 * D_MODEL] * pb
              + gates[:, 2 * D_MODEL:] * pc)
    xn = x_ref[...] + jnp.dot(merged.astype(BF16), wo_ref[...], preferred_element_type=F32)
    xo_ref[...] = xn
    ms = jnp.mean(xn * xn, -1, keepdims=True)
    h2 = xn * lax.rsqrt(ms + EPS) * n2_ref[...]
    h2_ref[...] = _pack_bf16_pairs(h2)
    route_ref[...] = _route(jnp.dot(h2.astype(BF16), wr_ref[...], preferred_element_type=F32))


def _merge(x2, oa, ob, oc, z2, wa, wb, wc, wo, n2g, w_rg, w_re):
    n, d = x2.shape
    tm = 512
    wr = jnp.concatenate(
        [w_rg, w_re, jnp.zeros((d, ROUTE_W - N_EXPERT_GROUPS - N_EXPERTS), w_rg.dtype)], axis=1)
    gate_w = GATE_W // 4
    gate_col0 = COL_G * IN_TN // gate_w

    def rows(width):
        return pl.BlockSpec((tm, width), lambda i: (i, 0))

    def full(shape):
        return pl.BlockSpec(shape, lambda i: (0, 0))

    gate_specs = [pl.BlockSpec((tm, gate_w), lambda i, k=k: (i, gate_col0 + k)) for k in range(4)]
    return pl.pallas_call(
        _merge_kernel,
        out_shape=[jax.ShapeDtypeStruct((n, d), F32),
                   jax.ShapeDtypeStruct((n, d // 2), jnp.int32),
                   jax.ShapeDtypeStruct((n, ROUTE_W), F32)],
        grid=(n // tm,),
        in_specs=[rows(d), rows(A_Q_W), rows(B_WIDTH), rows(C_OUT_W),
                  *gate_specs,
                  full((A_Q_W, d)), full((B_WIDTH, d)), full((C_OUT_W, d)), full((d, d)),
                  full((1, d)), full((d, ROUTE_W))],
        out_specs=[rows(d), rows(d // 2), rows(ROUTE_W)],
        compiler_params=_cparams(("parallel",)),
        name="merge",
    )(x2, oa, ob, oc, z2, z2, z2, z2,
      wa.astype(BF16), wb.astype(BF16), wc.astype(BF16), wo.astype(BF16),
      n2g.reshape(1, d), wr.astype(BF16))


MOE_TM = 512
SC_CORES = 2
SC_SUBCORES = 16
SC_WORKERS = SC_CORES * SC_SUBCORES
SC_CHUNK = 128


def _pack_bf16_pairs(x):
    w = x.shape[1] // 2
    lo = pltpu.bitcast(x[:, :w].astype(BF16).astype(F32), jnp.int32)
    hi = pltpu.bitcast(x[:, w:].astype(BF16).astype(F32), jnp.int32)
    return (hi & jnp.int32(-65536)) | lax.shift_right_logical(lo, jnp.int32(16))


def _unpack_bf16_pairs(p):
    lo = pltpu.bitcast(lax.shift_left(p, jnp.int32(16)), F32)
    hi = pltpu.bitcast(p & jnp.int32(-65536), F32)
    return jnp.concatenate([lo, hi], axis=1)


def _sc_gather_rows(table, idx):
    rows, width = idx.shape[0], table.shape[1]
    per_worker = rows // SC_WORKERS
    n_chunks = per_worker // SC_CHUNK
    assert per_worker * SC_WORKERS == rows and n_chunks * SC_CHUNK == per_worker
    mesh = plsc.VectorSubcoreMesh(core_axis_name="c", subcore_axis_name="s",
                                  num_cores=SC_CORES, num_subcores=SC_SUBCORES)

    @functools.partial(
        pl.kernel, mesh=mesh,
        out_type=jax.ShapeDtypeStruct((rows, width), table.dtype),
        scratch_types=[pltpu.VMEM((n_chunks, SC_CHUNK), jnp.int32),
                       pltpu.VMEM((SC_CHUNK, width), table.dtype),
                       pltpu.SemaphoreType.DMA],
        name="sc_gather_rows",
    )
    def gather(table_hbm, idx_hbm, out_hbm, idx_v, rows_v, sem):
        wid = lax.axis_index("s") * SC_CORES + lax.axis_index("c")
        pltpu.sync_copy(idx_hbm.at[wid], idx_v)
        base = wid * per_worker

        @pl.loop(0, n_chunks)
        def _(c):
            pltpu.async_copy(table_hbm.at[idx_v.at[c]], rows_v, sem).wait()
            pltpu.sync_copy(rows_v, out_hbm.at[pl.ds(base + c * SC_CHUNK, SC_CHUNK)])

    return gather(table, idx.reshape(SC_WORKERS, n_chunks, SC_CHUNK))


def _sc_scatter_rows(rows, pos, n_slots):
    n, width = rows.shape
    per_worker = n // SC_WORKERS
    n_chunks = per_worker // SC_CHUNK
    assert per_worker * SC_WORKERS == n and n_chunks * SC_CHUNK == per_worker
    mesh = plsc.VectorSubcoreMesh(core_axis_name="c", subcore_axis_name="s",
                                  num_cores=SC_CORES, num_subcores=SC_SUBCORES)

    @functools.partial(
        pl.kernel, mesh=mesh,
        out_type=jax.ShapeDtypeStruct((n_slots, width), rows.dtype),
        scratch_types=[pltpu.VMEM((TOP_K, n_chunks, SC_CHUNK), jnp.int32),
                       pltpu.VMEM((SC_CHUNK, width), rows.dtype),
                       pltpu.SemaphoreType.DMA],
        name="sc_scatter_rows",
    )
    def scatter(rows_hbm, pos_hbm, out_hbm, pos_v, rows_v, sem):
        wid = lax.axis_index("s") * SC_CORES + lax.axis_index("c")
        for k in range(TOP_K):
            pltpu.sync_copy(pos_hbm.at[k, wid], pos_v.at[k])
        base = wid * per_worker

        @pl.loop(0, n_chunks)
        def _(c):
            pltpu.sync_copy(rows_hbm.at[pl.ds(base + c * SC_CHUNK, SC_CHUNK)], rows_v)
            for k in range(TOP_K):
                pltpu.async_copy(rows_v, out_hbm.at[pos_v.at[k, c]], sem).wait()

    return scatter(rows, pos.reshape(TOP_K, SC_WORKERS, n_chunks, SC_CHUNK))


def _dispatch_plan(route, n_slots, expert_base):
    n = route.shape[0]
    e = route[:, 0:TOP_K].astype(jnp.int32).reshape(-1)
    experts = jnp.arange(N_EXPERTS, dtype=jnp.int32)
    onehot = (e[:, None] == experts[None, :]).astype(jnp.int32)
    csum = jnp.cumsum(onehot, axis=0)
    rank = jnp.sum((csum - onehot) * onehot, axis=1)
    counts = csum[-1]
    padded = (counts + MOE_TM - 1) // MOE_TM * MOE_TM
    ends = jnp.cumsum(padded)
    starts = ends - padded
    pos = starts[e] + rank
    tile_start = jnp.arange(n_slots // MOE_TM, dtype=jnp.int32) * MOE_TM
    tile_e = jnp.minimum(jnp.sum((tile_start[:, None] >= ends[None, :]).astype(jnp.int32), axis=1),
                         N_EXPERTS - 1)
    tile_rows = jnp.clip(counts[tile_e] - (tile_start - starts[tile_e]), 0, MOE_TM)
    tile_rows = jnp.where(tile_start < ends[-1], tile_rows, 0)
    first = jnp.concatenate([jnp.ones((1,), jnp.int32),
                             (tile_e[1:] != tile_e[:-1]).astype(jnp.int32)])
    return pos.reshape(n, TOP_K).T, tile_e + expert_base, tile_rows.astype(jnp.int32), first


def _moe_tile_kernel(te_ref, tr_ref, tf_ref, xs_ref, wg_ref, wu_ref, wd_ref, ys_ref,
                     wg_s, wu_s, wd_s):
    t = pl.program_id(0)

    @pl.when(tf_ref[t] != 0)
    def _():
        wg_s[...] = wg_ref[0].astype(BF16)
        wu_s[...] = wu_ref[0].astype(BF16)
        wd_s[...] = wd_ref[0].astype(BF16)

    @pl.when(tr_ref[t] != 0)
    def _():
        occupied = lax.broadcasted_iota(jnp.int32, (MOE_TM, 1), 0) < tr_ref[t]
        x = jnp.where(occupied, _unpack_bf16_pairs(xs_ref[...]), 0.0).astype(BF16)
        hg = jnp.dot(x, wg_s[...], preferred_element_type=F32)
        hu = jnp.dot(x, wu_s[...], preferred_element_type=F32)
        a = (jax.nn.silu(hg) * hu).astype(BF16)
        ys_ref[...] = _pack_bf16_pairs(jnp.dot(a, wd_s[...], preferred_element_type=F32))

    @pl.when(tr_ref[t] == 0)
    def _():
        ys_ref[...] = jnp.zeros_like(ys_ref)


def _moe_tiles(xs, tile_expert, tile_rows, tile_first, w_gate, w_up, w_down):
    n_slots, half = xs.shape
    d = 2 * half
    grid_spec = pltpu.PrefetchScalarGridSpec(
        num_scalar_prefetch=3,
        grid=(n_slots // MOE_TM,),
        in_specs=[
            pl.BlockSpec((MOE_TM, half), lambda t, te, tr, tf: (t, 0)),
            pl.BlockSpec((1, d, D_EXPERT), lambda t, te, tr, tf: (te[t], 0, 0)),
            pl.BlockSpec((1, d, D_EXPERT), lambda t, te, tr, tf: (te[t], 0, 0)),
            pl.BlockSpec((1, D_EXPERT, d), lambda t, te, tr, tf: (te[t], 0, 0)),
        ],
        out_specs=pl.BlockSpec((MOE_TM, half), lambda t, te, tr, tf: (t, 0)),
        scratch_shapes=[pltpu.VMEM((d, D_EXPERT), BF16), pltpu.VMEM((d, D_EXPERT), BF16),
                        pltpu.VMEM((D_EXPERT, d), BF16)],
    )
    return pl.pallas_call(
        _moe_tile_kernel,
        out_shape=jax.ShapeDtypeStruct((n_slots, half), jnp.int32),
        grid_spec=grid_spec,
        compiler_params=_cparams(("arbitrary",)),
        name="moe_tiles",
    )(tile_expert, tile_rows, tile_first, xs, w_gate, w_up, w_down)


def _moe_combine_kernel(x_ref, y0_ref, y1_ref, r_ref, o_ref):
    w0 = r_ref[:, 2:3]
    w1 = r_ref[:, 3:4]
    o_ref[...] = (x_ref[...] + w0 * _unpack_bf16_pairs(y0_ref[0])
                  + w1 * _unpack_bf16_pairs(y1_ref[0]))


def _moe_combine(x2, yg, route):
    n, d = x2.shape
    tm = 1024
    return pl.pallas_call(
        _moe_combine_kernel,
        out_shape=jax.ShapeDtypeStruct((n, d), F32),
        grid=(n // tm,),
        in_specs=[
            pl.BlockSpec((tm, d), lambda i: (i, 0)),
            pl.BlockSpec((1, tm, d // 2), lambda i: (0, i, 0)),
            pl.BlockSpec((1, tm, d // 2), lambda i: (1, i, 0)),
            pl.BlockSpec((tm, ROUTE_W), lambda i: (i, 0)),
        ],
        out_specs=pl.BlockSpec((tm, d), lambda i: (i, 0)),
        compiler_params=_cparams(("parallel",)),
        name="moe_combine",
    )(x2, yg, yg, route)


def _moe(h2p, route, x2, layer, w_gate, w_up, w_down):
    n, d = x2.shape
    n_slots = TOP_K * n + N_EXPERTS * MOE_TM
    pos, tile_expert, tile_rows, tile_first = _dispatch_plan(route, n_slots, layer * N_EXPERTS)
    xs = _sc_scatter_rows(h2p, pos, n_slots)
    ys = _moe_tiles(xs, tile_expert, tile_rows, tile_first, w_gate, w_up, w_down)
    yg = _sc_gather_rows(ys, pos.reshape(-1)).reshape(TOP_K, n, d // 2)
    return _moe_combine(x2, yg, route)


def _final_norm_kernel(x_ref, g_ref, o_ref):
    x = x_ref[...]
    ms = jnp.mean(x * x, -1, keepdims=True)
    o_ref[...] = x * lax.rsqrt(ms + EPS) * g_ref[...]


def _final_norm(x2, g):
    n, d = x2.shape
    tm = 1024
    return pl.pallas_call(
        _final_norm_kernel,
        out_shape=jax.ShapeDtypeStruct((n, d), F32),
        grid=(n // tm,),
        in_specs=[pl.BlockSpec((tm, d), lambda i: (i, 0)), pl.BlockSpec((1, d), lambda i: (0, 0))],
        out_specs=pl.BlockSpec((tm, d), lambda i: (i, 0)),
        compiler_params=_cparams(("parallel",)),
        name="final_norm",
    )(x2, g.reshape(1, d))


def kernel(x, positions, norm1_g, w_in, b_in, attn_sinks, sgu_ln_g, sgu_ln_b, w_spatial, b_spatial,
           w_proj_a, w_proj_b, w_proj_c, w_out, norm2_g, w_router_group, w_router_expert,
           w_expert_gate, w_expert_up, w_expert_down, norm_f_g):
    bsz, s, d = x.shape
    depth = w_in.shape[0]
    wg_all = w_expert_gate.reshape(depth * N_EXPERTS, d, D_EXPERT)
    wu_all = w_expert_up.reshape(depth * N_EXPERTS, d, D_EXPERT)
    wd_all = w_expert_down.reshape(depth * N_EXPERTS, D_EXPERT, d)
    assert d == D_MODEL and s % (C_PAIRS[-1][1] * BLOCK) == 0
    tabs = _rope_tables(positions)
    for l in range(depth):
        z = _in_proj(x, norm1_g[l], w_in[l].astype(BF16), b_in[l], tabs)
        oa = _attn_a(z, attn_sinks[l])
        ob = _sgu(z, sgu_ln_g[l], sgu_ln_b[l], w_spatial[l], b_spatial[l])
        oc = _attn_c(z)
        x2, h2, route = _merge(
            x.reshape(bsz * s, d), oa.reshape(bsz * s, A_Q_W), ob.reshape(bsz * s, B_WIDTH),
            oc.reshape(bsz * s, C_OUT_W), z.reshape(bsz * s, D_IN), w_proj_a[l], w_proj_b[l], w_proj_c[l], w_out[l],
            norm2_g[l], w_router_group[l], w_router_expert[l])
        x = _moe(h2, route, x2, l, wg_all, wu_all, wd_all).reshape(bsz, s, d)
    return _final_norm(x.reshape(bsz * s, d), norm_f_g).reshape(bsz, s, d)
```

```python
import functools

import jax
import jax.numpy as jnp
from jax import lax
from jax.experimental import pallas as pl
from jax.experimental.pallas import tpu as pltpu
from jax.experimental.pallas import tpu_sc as plsc

F32 = jnp.float32
BF16 = jnp.bfloat16

D_MODEL = 1024
HEAD_DIM = 64
ROT_DIM = HEAD_DIM // 4
ROPE_THETA = 500000.0
BLOCK = 128
EPS = 1e-5
NEG_INF = -1e30

A_Q_HEADS = 16
A_KV_HEADS = 4
A_REP = A_Q_HEADS // A_KV_HEADS
A_WINDOW = 128
A_Q_W = A_Q_HEADS * HEAD_DIM
A_KV_W = A_KV_HEADS * HEAD_DIM

B_GROUPS = 12
B_CH = 64
B_WIDTH = B_GROUPS * B_CH
B_CHUNK = 128

C_PAIRS = ((128, 1), (512, 4), (2048, 16))
C_HEADS_PER_GROUP = 4
C_HEADS = C_HEADS_PER_GROUP * len(C_PAIRS)
C_OUT_W = C_HEADS_PER_GROUP * HEAD_DIM

N_BRANCH = 3
GATE_W = N_BRANCH * D_MODEL
D_IN = A_Q_W + 2 * A_KV_W + 2 * B_WIDTH + 3 * C_HEADS * HEAD_DIM + GATE_W

N_EXPERT_GROUPS = 4
EXPERTS_PER_GROUP = 8
N_EXPERTS = N_EXPERT_GROUPS * EXPERTS_PER_GROUP
TOP_K = 2
D_EXPERT = 256

LANES = 128
MXU_W = 256
VMEM_LIMIT = 56 * 1024 * 1024

IN_TN = 256
COL_AQ = 0
COL_AK = A_Q_W // IN_TN
COL_AV = COL_AK + A_KV_W // IN_TN
COL_BU = COL_AV + A_KV_W // IN_TN
COL_BV = COL_BU + B_WIDTH // IN_TN
COL_CQ = COL_BV + B_WIDTH // IN_TN
COL_CK = COL_CQ + C_HEADS * HEAD_DIM // IN_TN
COL_CV = COL_CK + C_HEADS * HEAD_DIM // IN_TN
COL_G = COL_CV + C_HEADS * HEAD_DIM // IN_TN
N_COL_TILES = D_IN // IN_TN

ROUTE_W = LANES
ROUTE_E0 = N_EXPERT_GROUPS


def _cparams(sem):
    return pltpu.CompilerParams(dimension_semantics=sem, vmem_limit_bytes=VMEM_LIMIT)


def _rope_table_kernel(pos_ref, inv_ref, c_ref, s1_ref, s2_ref):
    lane = lax.broadcasted_iota(jnp.int32, (1, LANES), 1)
    d = lane & (HEAD_DIM - 1)
    ang = pos_ref[...].astype(F32) * inv_ref[...]
    c = jnp.cos(ang)
    s = jnp.sin(ang)
    half = ROT_DIM // 2
    c_ref[...] = jnp.where(d < ROT_DIM, c, 1.0)
    s1_ref[...] = jnp.where(d < half, -s, 0.0)
    s2_ref[...] = jnp.where((d >= half) & (d < ROT_DIM), s, 0.0)


def _rope_tables(positions):
    n = positions.size
    inv = ROPE_THETA ** (-jnp.arange(0, ROT_DIM, 2, dtype=F32) / ROT_DIM)
    inv_lane = jnp.tile(inv, LANES // inv.shape[0]).reshape(1, LANES)
    pos_b = jnp.broadcast_to(positions.reshape(n, 1), (n, LANES))
    tm = 1024
    spec = pl.BlockSpec((tm, LANES), lambda i: (i, 0))
    return pl.pallas_call(
        _rope_table_kernel,
        out_shape=[jax.ShapeDtypeStruct((n, LANES), F32)] * 3,
        grid=(n // tm,),
        in_specs=[spec, pl.BlockSpec((1, LANES), lambda i: (0, 0))],
        out_specs=[spec, spec, spec],
        compiler_params=_cparams(("parallel",)),
        name="rope_tables",
    )(pos_b, inv_lane)


def _in_proj_kernel(x_ref, g_ref, w_ref, b_ref, c_ref, s1_ref, s2_ref, z_ref, h_ref):
    j = pl.program_id(1)

    @pl.when(j == 0)
    def _():
        x = x_ref[0]
        ms = jnp.mean(x * x, -1, keepdims=True)
        h_ref[...] = (x * lax.rsqrt(ms + EPS) * g_ref[...]).astype(BF16)

    acc = jnp.dot(h_ref[...], w_ref[...], preferred_element_type=F32) + b_ref[...]
    is_q = (j < COL_AK) | ((j >= COL_CQ) & (j < COL_CK))
    is_rope = (j < COL_AV) | ((j >= COL_CQ) & (j < COL_CV))

    @pl.when(is_rope)
    def _():
        scale = jnp.where(is_q, HEAD_DIM ** -0.5, 1.0).astype(F32)
        for t in range(IN_TN // LANES):
            a = acc[:, t * LANES:(t + 1) * LANES]
            r = (a * c_ref[...] + pltpu.roll(a, LANES - ROT_DIM // 2, 1) * s1_ref[...]
                 + pltpu.roll(a, ROT_DIM // 2, 1) * s2_ref[...])
            z_ref[0, :, t * LANES:(t + 1) * LANES] = (r * scale).astype(BF16)

    @pl.when(jnp.logical_not(is_rope))
    def _():
        z_ref[0] = acc.astype(BF16)


def _in_proj(x, g, w, b, tabs):
    bsz, s, d = x.shape
    c, s1, s2 = tabs
    tab_spec = pl.BlockSpec((s, LANES), lambda bi, j: (bi, 0))
    return pl.pallas_call(
        _in_proj_kernel,
        out_shape=jax.ShapeDtypeStruct((bsz, s, D_IN), BF16),
        grid=(bsz, N_COL_TILES),
        in_specs=[
            pl.BlockSpec((1, s, d), lambda bi, j: (bi, 0, 0)),
            pl.BlockSpec((1, d), lambda bi, j: (0, 0)),
            pl.BlockSpec((d, IN_TN), lambda bi, j: (0, j)),
            pl.BlockSpec((1, IN_TN), lambda bi, j: (0, j)),
            tab_spec, tab_spec, tab_spec,
        ],
        out_specs=pl.BlockSpec((1, s, IN_TN), lambda bi, j: (bi, 0, j)),
        scratch_shapes=[pltpu.VMEM((s, d), BF16)],
        compiler_params=_cparams(("parallel", "arbitrary")),
        name="in_proj",
    )(x, g.reshape(1, d), w, b.reshape(1, D_IN), c, s1, s2)


STACK = C_HEADS_PER_GROUP * BLOCK
STEP_UNROLL = 4


def _head_stack(blk):
    seg = lax.broadcasted_iota(jnp.int32, (1, MXU_W), 1) // HEAD_DIM
    rowseg = lax.broadcasted_iota(jnp.int32, (STACK, 1), 0) // BLOCK
    return jnp.where(seg == rowseg, jnp.concatenate([blk] * C_HEADS_PER_GROUP, axis=0),
                     jnp.zeros((), blk.dtype))


def _band_step(q, kp_ref, vp_ref, blk, n, has_prev, diag_key, sink_col, want_lse):
    return _band_blocks([(q, blk, n)], kp_ref, vp_ref, has_prev, diag_key, sink_col, want_lse)[0]


def _band_blocks(jobs, kp_ref, vp_ref, has_prev, diag_key, sink_col, want_lse):
    contract = (((1,), (1,)), ((), ()))
    i_loc = lax.broadcasted_iota(jnp.int32, (STACK, 1), 0) & (BLOCK - 1)
    jj = lax.broadcasted_iota(jnp.int32, (1, BLOCK), 1)
    upper = jj > i_loc
    seg = lax.broadcasted_iota(jnp.int32, (1, MXU_W), 1) // HEAD_DIM
    zero_b = jnp.zeros((), BF16)

    def key_rows(blk):
        lo = pl.multiple_of(blk * BLOCK, BLOCK)
        if has_prev:
            return pl.ds(lo, 2 * BLOCK)
        return pl.ds(pl.multiple_of(lo + BLOCK, BLOCK), BLOCK)

    scores = [lax.dot_general(_head_stack(q), kp_ref[key_rows(blk), :], contract,
                              preferred_element_type=F32) for q, blk, _ in jobs]

    def own_lanes(x):
        out = x[0:BLOCK]
        for h in range(1, C_HEADS_PER_GROUP):
            out = jnp.where(seg == h, x[h * BLOCK:(h + 1) * BLOCK], out)
        return out

    probs, stats = [], []
    for s, (_, blk, n) in zip(scores, jobs):
        if has_prev:
            sp = s[:, :BLOCK] + jnp.where(n == 0, NEG_INF, 0.0).astype(F32)
            f = jnp.where(upper, sp, s[:, BLOCK:])
        else:
            f = jnp.where(upper, NEG_INF, s)
        m = jnp.max(f, -1, keepdims=True)
        if diag_key:
            sd = jnp.sum(jnp.where(jj == i_loc, sp, 0.0), -1, keepdims=True)
            m = jnp.maximum(m, sd)
        if sink_col is not None:
            m = jnp.maximum(m, sink_col)
        p = jnp.exp(f - m)
        den = jnp.sum(p, -1, keepdims=True)
        pd = None
        if diag_key:
            pd = jnp.exp(sd - m)
            den = den + pd
        if sink_col is not None:
            den = den + jnp.exp(sink_col - m)
        inv = 1.0 / den
        pn = (p * inv).astype(BF16)
        if has_prev:
            below = zero_b
            if diag_key:
                below = jnp.where(jj == i_loc, pd * inv, 0.0).astype(BF16)
            pn = jnp.concatenate([jnp.where(upper, pn, below), jnp.where(upper, zero_b, pn)],
                                 axis=1)
        probs.append(pn)
        stats.append((m, den))

    outs = []
    for pn, (m, den), (_, blk, _) in zip(probs, stats, jobs):
        o = own_lanes(jnp.dot(pn, vp_ref[key_rows(blk), :], preferred_element_type=F32))
        lse = own_lanes(m + jnp.log(den)) + jnp.zeros((BLOCK, MXU_W), F32) if want_lse else None
        outs.append((o, lse))
    return outs


def _attn_a_kernel(sink_ref, q_ref, k_ref, v_ref, o_ref, kp_ref, vp_ref):
    nb = q_ref.shape[1] // BLOCK
    rowseg = lax.broadcasted_iota(jnp.int32, (STACK, 1), 0) // BLOCK
    kp_ref[pl.ds(0, BLOCK), :] = jnp.zeros((BLOCK, MXU_W), BF16)
    vp_ref[pl.ds(0, BLOCK), :] = jnp.zeros((BLOCK, MXU_W), BF16)
    for g in range(A_KV_HEADS):
        kv_cols = slice(g * HEAD_DIM, (g + 1) * HEAD_DIM)
        q_cols = slice(g * MXU_W, (g + 1) * MXU_W)

        def build(n, carry):
            rows = pl.ds(pl.multiple_of(n * BLOCK, BLOCK), BLOCK)
            dst = pl.ds(pl.multiple_of((n + 1) * BLOCK, BLOCK), BLOCK)
            kp_ref[dst, :] = jnp.concatenate([k_ref[0, rows, kv_cols]] * A_REP, axis=1)
            vp_ref[dst, :] = jnp.concatenate([v_ref[0, rows, kv_cols]] * A_REP, axis=1)
            return carry

        lax.fori_loop(0, nb, build, 0)
        sink_col = jnp.zeros((STACK, 1), F32)
        for r in range(A_REP):
            sink_col = jnp.where(rowseg == r, sink_ref[g * A_REP + r], sink_col)

        def step(it, carry):
            blocks = [it * STEP_UNROLL + u for u in range(STEP_UNROLL)]
            rows = [pl.ds(pl.multiple_of(n * BLOCK, BLOCK), BLOCK) for n in blocks]
            jobs = [(q_ref[0, r, q_cols], n, n) for r, n in zip(rows, blocks)]
            outs = _band_blocks(jobs, kp_ref, vp_ref, True, False, sink_col, False)
            for r, (o, _) in zip(rows, outs):
                o_ref[0, r, q_cols] = o.astype(o_ref.dtype)
            return carry

        lax.fori_loop(0, nb // STEP_UNROLL, step, 0)


def _attn_a(z, sinks):
    bsz, s, _ = z.shape
    nb = s // BLOCK
    grid_spec = pltpu.PrefetchScalarGridSpec(
        num_scalar_prefetch=1,
        grid=(bsz,),
        in_specs=[
            pl.BlockSpec((1, s, A_Q_W), lambda bi, sk: (bi, 0, 0)),
            pl.BlockSpec((1, s, A_KV_W), lambda bi, sk: (bi, 0, COL_AK)),
            pl.BlockSpec((1, s, A_KV_W), lambda bi, sk: (bi, 0, COL_AV)),
        ],
        out_specs=pl.BlockSpec((1, s, A_Q_W), lambda bi, sk: (bi, 0, 0)),
        scratch_shapes=[pltpu.VMEM(((nb + 1) * BLOCK, MXU_W), BF16),
                        pltpu.VMEM(((nb + 1) * BLOCK, MXU_W), BF16)],
    )
    return pl.pallas_call(
        _attn_a_kernel,
        out_shape=jax.ShapeDtypeStruct((bsz, s, A_Q_W), BF16),
        grid_spec=grid_spec,
        compiler_params=_cparams(("parallel",)),
        name="attn_a",
    )(sinks, z, z, z)


def _attn_c_kernel(q_ref, k_ref, v_ref, oc_ref, stage_ref, pq_ref, pk_ref, pv_ref, og_ref, lg_ref):
    s = q_ref.shape[1]
    nblk = s // BLOCK
    halves = MXU_W // LANES
    pk_ref[pl.ds(0, BLOCK), :] = jnp.zeros((BLOCK, MXU_W), BF16)
    pv_ref[pl.ds(0, BLOCK), :] = jnp.zeros((BLOCK, MXU_W), BF16)

    def class_rows(idx, nb, d):
        r = idx // nb
        n = idx - r * nb
        start = r + n * (BLOCK * d)
        rows = pl.ds(start, BLOCK, stride=d) if d > 1 else pl.ds(pl.multiple_of(start, BLOCK), BLOCK)
        return n, rows

    def stage(src_ref, cols):
        x = src_ref[0, :, cols].astype(F32)
        for t in range(halves):
            stage_ref[t] = x[:, t * LANES:(t + 1) * LANES]

    def staged_block(rows):
        return jnp.concatenate([stage_ref[t, rows, :] for t in range(halves)], axis=1).astype(BF16)

    for g, (win, d) in enumerate(C_PAIRS):
        cols = slice(g * MXU_W, (g + 1) * MXU_W)
        nb = nblk // d
        has_prev = nb > 1
        assert win // d == BLOCK

        def class_major(dst_ref, offset):
            def build(idx, carry):
                _, rows = class_rows(idx, nb, d)
                dst = pl.ds(pl.multiple_of((idx + offset) * BLOCK, BLOCK), BLOCK)
                dst_ref[dst, :] = staged_block(rows)
                return carry
            return build

        stage(k_ref, cols)
        lax.fori_loop(0, nblk, class_major(pk_ref, 1), 0)
        stage(v_ref, cols)
        lax.fori_loop(0, nblk, class_major(pv_ref, 1), 0)
        stage(q_ref, cols)
        lax.fori_loop(0, nblk, class_major(pq_ref, 0), 0)

        def step(it, carry):
            jobs, dsts = [], []
            for u in range(STEP_UNROLL):
                idx = it * STEP_UNROLL + u
                n, rows = class_rows(idx, nb, d)
                here = pl.ds(pl.multiple_of(idx * BLOCK, BLOCK), BLOCK)
                jobs.append((pq_ref[here, :], idx, n))
                dsts.append(rows)
            outs = _band_blocks(jobs, pk_ref, pv_ref, has_prev, has_prev, None, True)
            for rows, (o, lse) in zip(dsts, outs):
                for t in range(halves):
                    og_ref[g, t, rows, :] = o[:, t * LANES:(t + 1) * LANES]
                    lg_ref[g, t, rows, :] = lse[:, t * LANES:(t + 1) * LANES]
            return carry

        lax.fori_loop(0, nblk // STEP_UNROLL, step, 0)

    chunk = 2 * BLOCK

    def combine(i, carry):
        rows = pl.ds(pl.multiple_of(i * chunk, chunk), chunk)
        for t in range(halves):
            ls = [lg_ref[g, t, rows, :] for g in range(len(C_PAIRS))]
            m = jnp.maximum(jnp.maximum(ls[0], ls[1]), ls[2])
            es = [jnp.exp(l - m) for l in ls]
            num = (es[0] * og_ref[0, t, rows, :] + es[1] * og_ref[1, t, rows, :]
                   + es[2] * og_ref[2, t, rows, :])
            oc_ref[0, rows, t * LANES:(t + 1) * LANES] = (
                num / (es[0] + es[1] + es[2])).astype(oc_ref.dtype)
        return carry

    lax.fori_loop(0, s // chunk, combine, 0)


def _attn_c(z):
    bsz, s, _ = z.shape
    nblk = s // BLOCK
    width = C_HEADS * HEAD_DIM
    halves = MXU_W // LANES

    def in_spec(col):
        return pl.BlockSpec((1, s, width), lambda bi: (bi, 0, col * IN_TN // width))

    return pl.pallas_call(
        _attn_c_kernel,
        out_shape=jax.ShapeDtypeStruct((bsz, s, C_OUT_W), BF16),
        grid=(bsz,),
        in_specs=[in_spec(COL_CQ), in_spec(COL_CK), in_spec(COL_CV)],
        out_specs=pl.BlockSpec((1, s, C_OUT_W), lambda bi: (bi, 0, 0)),
        scratch_shapes=[
            pltpu.VMEM((halves, s, LANES), F32),
            pltpu.VMEM((s, MXU_W), BF16),
            pltpu.VMEM((s + BLOCK, MXU_W), BF16),
            pltpu.VMEM((s + BLOCK, MXU_W), BF16),
            pltpu.VMEM((len(C_PAIRS), halves, s, LANES), F32),
            pltpu.VMEM((len(C_PAIRS), halves, s, LANES), F32),
        ],
        compiler_params=_cparams(("parallel",)),
        name="attn_c",
    )(z, z, z)


def _sgu_kernel(zu_ref, zv_ref, lng_ref, lnb_ref, ws_ref, bs_ref, o_ref):
    tm = zu_ref.shape[1]
    u = jax.nn.gelu(zu_ref[0].astype(F32))
    v = jax.nn.gelu(zv_ref[0].astype(F32))
    mu = jnp.mean(v, -1, keepdims=True)
    var = jnp.mean(jnp.square(v - mu), -1, keepdims=True)
    vn = ((v - mu) * lax.rsqrt(var + EPS) * lng_ref[...] + lnb_ref[...]).astype(BF16)
    causal = (lax.broadcasted_iota(jnp.int32, (B_CHUNK, 1), 0)
              >= lax.broadcasted_iota(jnp.int32, (1, B_CHUNK), 1))
    first = lax.broadcasted_iota(jnp.int32, (1, LANES), 1) < B_CH
    zero = jnp.zeros((), BF16)
    for p in range(B_WIDTH // LANES):
        w0 = jnp.where(causal, ws_ref[2 * p], zero)
        w1 = jnp.where(causal, ws_ref[2 * p + 1], zero)
        cols = slice(p * LANES, (p + 1) * LANES)
        for c in range(tm // B_CHUNK):
            rows = slice(c * B_CHUNK, (c + 1) * B_CHUNK)
            vv = vn[rows, cols]
            sv = jnp.where(first,
                           jnp.dot(w0, vv, preferred_element_type=F32),
                           jnp.dot(w1, vv, preferred_element_type=F32)) + bs_ref[:, cols]
            o_ref[0, rows, cols] = (u[rows, cols] * sv).astype(o_ref.dtype)


def _sgu(z, ln_g, ln_b, w_s, b_s):
    bsz, s, _ = z.shape
    tm = 512
    bias = jnp.repeat(b_s.T, B_CH, axis=1)
    col_u = COL_BU * IN_TN // B_WIDTH
    col_v = COL_BV * IN_TN // B_WIDTH
    return pl.pallas_call(
        _sgu_kernel,
        out_shape=jax.ShapeDtypeStruct((bsz, s, B_WIDTH), BF16),
        grid=(bsz, s // tm),
        in_specs=[
            pl.BlockSpec((1, tm, B_WIDTH), lambda bi, i: (bi, i, col_u)),
            pl.BlockSpec((1, tm, B_WIDTH), lambda bi, i: (bi, i, col_v)),
            pl.BlockSpec((1, B_WIDTH), lambda bi, i: (0, 0)),
            pl.BlockSpec((1, B_WIDTH), lambda bi, i: (0, 0)),
            pl.BlockSpec((B_GROUPS, B_CHUNK, B_CHUNK), lambda bi, i: (0, 0, 0)),
            pl.BlockSpec((B_CHUNK, B_WIDTH), lambda bi, i: (0, 0)),
        ],
        out_specs=pl.BlockSpec((1, tm, B_WIDTH), lambda bi, i: (bi, i, 0)),
        compiler_params=_cparams(("parallel", "parallel")),
        name="sgu",
    )(z, z, ln_g.reshape(1, B_WIDTH), ln_b.reshape(1, B_WIDTH), w_s.astype(BF16), bias)


def _route(logits):
    lane = lax.broadcasted_iota(jnp.int32, (1, ROUTE_W), 1)
    lane_f = lane.astype(F32)
    is_g = lane < N_EXPERT_GROUPS
    lg = jnp.where(is_g, logits, NEG_INF)
    mg = jnp.max(lg, -1, keepdims=True)
    pg_top = 1.0 / jnp.sum(jnp.where(is_g, jnp.exp(lg - mg), 0.0), -1, keepdims=True)
    g_idx = jnp.min(jnp.where(lg == mg, lane_f, float(ROUTE_W)), -1, keepdims=True)
    e_group = ((lane - ROUTE_E0) // EXPERTS_PER_GROUP).astype(F32)
    in_grp = (lane >= ROUTE_E0) & (lane < ROUTE_E0 + N_EXPERTS) & (e_group == g_idx)
    le = jnp.where(in_grp, logits, NEG_INF)
    m1 = jnp.max(le, -1, keepdims=True)
    i1 = jnp.min(jnp.where(le == m1, lane_f, float(ROUTE_W)), -1, keepdims=True)
    le2 = jnp.where(lane_f == i1, NEG_INF, le)
    m2 = jnp.max(le2, -1, keepdims=True)
    i2 = jnp.min(jnp.where(le2 == m2, lane_f, float(ROUTE_W)), -1, keepdims=True)
    t = jnp.exp(m2 - m1)
    w1 = pg_top / (1.0 + t)
    w2 = w1 * t
    picks = (i1 - ROUTE_E0, i2 - ROUTE_E0, w1, w2)
    out = jnp.zeros(logits.shape, F32)
    for k, val in enumerate(picks):
        out = jnp.where(lane == k, val, out)
    return out


def _merge_kernel(x_ref, oa_ref, ob_ref, oc_ref,
                  g0_ref, g1_ref, g2_ref, g3_ref, wa_ref, wb_ref, wc_ref, wo_ref, n2_ref, wr_ref,
                  xo_ref, h2_ref, route_ref):
    zg = jnp.concatenate([g0_ref[...], g1_ref[...], g2_ref[...], g3_ref[...]], axis=1).astype(F32)
    gates = jax.nn.sigmoid(zg)
    pa = jnp.dot(oa_ref[...], wa_ref[...], preferred_element_type=F32)
    pb = jnp.dot(ob_ref[...], wb_ref[...], preferred_element_type=F32)
    pc = jnp.dot(oc_ref[...], wc_ref[...], preferred_element_type=F32)
    merged = (gates[:, :D_MODEL] * pa + gates[:, D_MODEL:2 * D_MODEL] * pb
              + gates[:, 2 * D_MODEL:] * pc)
    xn = x_ref[...] + jnp.dot(merged.astype(BF16), wo_ref[...], preferred_element_type=F32)
    xo_ref[...] = xn
    ms = jnp.mean(xn * xn, -1, keepdims=True)
    h2 = xn * lax.rsqrt(ms + EPS) * n2_ref[...]
    h2_ref[...] = _pack_bf16_pairs(h2)
    route_ref[...] = _route(jnp.dot(h2.astype(BF16), wr_ref[...], preferred_element_type=F32))


def _merge(x2, oa, ob, oc, z2, wa, wb, wc, wo, n2g, w_rg, w_re):
    n, d = x2.shape
    tm = 512
    wr = jnp.concatenate(
        [w_rg, w_re, jnp.zeros((d, ROUTE_W - N_EXPERT_GROUPS - N_EXPERTS), w_rg.dtype)], axis=1)
    gate_w = GATE_W // 4
    gate_col0 = COL_G * IN_TN // gate_w

    def rows(width):
        return pl.BlockSpec((tm, width), lambda i: (i, 0))

    def full(shape):
        return pl.BlockSpec(shape, lambda i: (0, 0))

    gate_specs = [pl.BlockSpec((tm, gate_w), lambda i, k=k: (i, gate_col0 + k)) for k in range(4)]
    return pl.pallas_call(
        _merge_kernel,
        out_shape=[jax.ShapeDtypeStruct((n, d), F32),
                   jax.ShapeDtypeStruct((n, d // 2), jnp.int32),
                   jax.ShapeDtypeStruct((n, ROUTE_W), F32)],
        grid=(n // tm,),
        in_specs=[rows(d), rows(A_Q_W), rows(B_WIDTH), rows(C_OUT_W),
                  *gate_specs,
                  full((A_Q_W, d)), full((B_WIDTH, d)), full((C_OUT_W, d)), full((d, d)),
                  full((1, d)), full((d, ROUTE_W))],
        out_specs=[rows(d), rows(d // 2), rows(ROUTE_W)],
        compiler_params=_cparams(("parallel",)),
        name="merge",
    )(x2, oa, ob, oc, z2, z2, z2, z2,
      wa.astype(BF16), wb.astype(BF16), wc.astype(BF16), wo.astype(BF16),
      n2g.reshape(1, d), wr.astype(BF16))


MOE_TM = 512
SC_CORES = 2
SC_SUBCORES = 16
SC_WORKERS = SC_CORES * SC_SUBCORES
SC_CHUNK = 128


def _pack_bf16_pairs(x):
    w = x.shape[1] // 2
    lo = pltpu.bitcast(x[:, :w].astype(BF16).astype(F32), jnp.int32)
    hi = pltpu.bitcast(x[:, w:].astype(BF16).astype(F32), jnp.int32)
    return (hi & jnp.int32(-65536)) | lax.shift_right_logical(lo, jnp.int32(16))


def _unpack_bf16_pairs(p):
    lo = pltpu.bitcast(lax.shift_left(p, jnp.int32(16)), F32)
    hi = pltpu.bitcast(p & jnp.int32(-65536), F32)
    return jnp.concatenate([lo, hi], axis=1)


def _sc_gather_rows(table, idx):
    rows, width = idx.shape[0], table.shape[1]
    per_worker = rows // SC_WORKERS
    n_chunks = per_worker // SC_CHUNK
    assert per_worker * SC_WORKERS == rows and n_chunks * SC_CHUNK == per_worker
    mesh = plsc.VectorSubcoreMesh(core_axis_name="c", subcore_axis_name="s",
                                  num_cores=SC_CORES, num_subcores=SC_SUBCORES)

    @functools.partial(
        pl.kernel, mesh=mesh,
        out_type=jax.ShapeDtypeStruct((rows, width), table.dtype),
        scratch_types=[pltpu.VMEM((n_chunks, SC_CHUNK), jnp.int32),
                       pltpu.VMEM((SC_CHUNK, width), table.dtype),
                       pltpu.SemaphoreType.DMA],
        name="sc_gather_rows",
    )
    def gather(table_hbm, idx_hbm, out_hbm, idx_v, rows_v, sem):
        wid = lax.axis_index("s") * SC_CORES + lax.axis_index("c")
        pltpu.sync_copy(idx_hbm.at[wid], idx_v)
        base = wid * per_worker

        @pl.loop(0, n_chunks)
        def _(c):
            pltpu.async_copy(table_hbm.at[idx_v.at[c]], rows_v, sem).wait()
            pltpu.sync_copy(rows_v, out_hbm.at[pl.ds(base + c * SC_CHUNK, SC_CHUNK)])

    return gather(table, idx.reshape(SC_WORKERS, n_chunks, SC_CHUNK))


def _sc_scatter_rows(rows, pos, n_slots):
    n, width = rows.shape
    per_worker = n // SC_WORKERS
    n_chunks = per_worker // SC_CHUNK
    assert per_worker * SC_WORKERS == n and n_chunks * SC_CHUNK == per_worker
    mesh = plsc.VectorSubcoreMesh(core_axis_name="c", subcore_axis_name="s",
                                  num_cores=SC_CORES, num_subcores=SC_SUBCORES)

    @functools.partial(
        pl.kernel, mesh=mesh,
        out_type=jax.ShapeDtypeStruct((n_slots, width), rows.dtype),
        scratch_types=[pltpu.VMEM((TOP_K, n_chunks, SC_CHUNK), jnp.int32),
                       pltpu.VMEM((SC_CHUNK, width), rows.dtype),
                       pltpu.SemaphoreType.DMA],
        name="sc_scatter_rows",
    )
    def scatter(rows_hbm, pos_hbm, out_hbm, pos_v, rows_v, sem):
        wid = lax.axis_index("s") * SC_CORES + lax.axis_index("c")
        for k in range(TOP_K):
            pltpu.sync_copy(pos_hbm.at[k, wid], pos_v.at[k])
        base = wid * per_worker

        @pl.loop(0, n_chunks)
        def _(c):
            pltpu.sync_copy(rows_hbm.at[pl.ds(base + c * SC_CHUNK, SC_CHUNK)], rows_v)
            for k in range(TOP_K):
                pltpu.async_copy(rows_v, out_hbm.at[pos_v.at[k, c]], sem).wait()

    return scatter(rows, pos.reshape(TOP_K, SC_WORKERS, n_chunks, SC_CHUNK))


def _dispatch_plan(route, n_slots, expert_base):
    n = route.shape[0]
    e = route[:, 0:TOP_K].astype(jnp.int32).reshape(-1)
    experts = jnp.arange(N_EXPERTS, dtype=jnp.int32)
    onehot = (e[:, None] == experts[None, :]).astype(jnp.int32)
    csum = jnp.cumsum(onehot, axis=0)
    rank = jnp.sum((csum - onehot) * onehot, axis=1)
    counts = csum[-1]
    padded = (counts + MOE_TM - 1) // MOE_TM * MOE_TM
    ends = jnp.cumsum(padded)
    starts = ends - padded
    pos = starts[e] + rank
    tile_start = jnp.arange(n_slots // MOE_TM, dtype=jnp.int32) * MOE_TM
    tile_e = jnp.minimum(jnp.sum((tile_start[:, None] >= ends[None, :]).astype(jnp.int32), axis=1),
                         N_EXPERTS - 1)
    tile_rows = jnp.clip(counts[tile_e] - (tile_start - starts[tile_e]), 0, MOE_TM)
    tile_rows = jnp.where(tile_start < ends[-1], tile_rows, 0)
    first = jnp.concatenate([jnp.ones((1,), jnp.int32),
                             (tile_e[1:] != tile_e[:-1]).astype(jnp.int32)])
    return pos.reshape(n, TOP_K).T, tile_e + expert_base, tile_rows.astype(jnp.int32), first


def _moe_tile_kernel(te_ref, tr_ref, tf_ref, xs_ref, wg_ref, wu_ref, wd_ref, ys_ref,
                     wg_s, wu_s, wd_s):
    t = pl.program_id(0)

    @pl.when(tf_ref[t] != 0)
    def _():
        wg_s[...] = wg_ref[0].astype(BF16)
        wu_s[...] = wu_ref[0].astype(BF16)
        wd_s[...] = wd_ref[0].astype(BF16)

    @pl.when(tr_ref[t] != 0)
    def _():
        occupied = lax.broadcasted_iota(jnp.int32, (MOE_TM, 1), 0) < tr_ref[t]
        x = jnp.where(occupied, _unpack_bf16_pairs(xs_ref[...]), 0.0).astype(BF16)
        hg = jnp.dot(x, wg_s[...], preferred_element_type=F32)
        hu = jnp.dot(x, wu_s[...], preferred_element_type=F32)
        a = (jax.nn.silu(hg) * hu).astype(BF16)
        ys_ref[...] = _pack_bf16_pairs(jnp.dot(a, wd_s[...], preferred_element_type=F32))

    @pl.when(tr_ref[t] == 0)
    def _():
        ys_ref[...] = jnp.zeros_like(ys_ref)


def _moe_tiles(xs, tile_expert, tile_rows, tile_first, w_gate, w_up, w_down):
    n_slots, half = xs.shape
    d = 2 * half
    grid_spec = pltpu.PrefetchScalarGridSpec(
        num_scalar_prefetch=3,
        grid=(n_slots // MOE_TM,),
        in_specs=[
            pl.BlockSpec((MOE_TM, half), lambda t, te, tr, tf: (t, 0)),
            pl.BlockSpec((1, d, D_EXPERT), lambda t, te, tr, tf: (te[t], 0, 0)),
            pl.BlockSpec((1, d, D_EXPERT), lambda t, te, tr, tf: (te[t], 0, 0)),
            pl.BlockSpec((1, D_EXPERT, d), lambda t, te, tr, tf: (te[t], 0, 0)),
        ],
        out_specs=pl.BlockSpec((MOE_TM, half), lambda t, te, tr, tf: (t, 0)),
        scratch_shapes=[pltpu.VMEM((d, D_EXPERT), BF16), pltpu.VMEM((d, D_EXPERT), BF16),
                        pltpu.VMEM((D_EXPERT, d), BF16)],
    )
    return pl.pallas_call(
        _moe_tile_kernel,
        out_shape=jax.ShapeDtypeStruct((n_slots, half), jnp.int32),
        grid_spec=grid_spec,
        compiler_params=_cparams(("arbitrary",)),
        name="moe_tiles",
    )(tile_expert, tile_rows, tile_first, xs, w_gate, w_up, w_down)


def _moe_combine_kernel(x_ref, y0_ref, y1_ref, r_ref, g_ref, o_ref, *, final_norm):
    w0 = r_ref[:, 2:3]
    w1 = r_ref[:, 3:4]
    x = x_ref[...] + w0 * _unpack_bf16_pairs(y0_ref[0]) + w1 * _unpack_bf16_pairs(y1_ref[0])
    if final_norm:
        ms = jnp.mean(x * x, -1, keepdims=True)
        x = x * lax.rsqrt(ms + EPS) * g_ref[...]
    o_ref[...] = x


def _moe_combine(x2, yg, route, norm_g, final_norm):
    n, d = x2.shape
    tm = 1024
    return pl.pallas_call(
        functools.partial(_moe_combine_kernel, final_norm=final_norm),
        out_shape=jax.ShapeDtypeStruct((n, d), F32),
        grid=(n // tm,),
        in_specs=[
            pl.BlockSpec((tm, d), lambda i: (i, 0)),
            pl.BlockSpec((1, tm, d // 2), lambda i: (0, i, 0)),
            pl.BlockSpec((1, tm, d // 2), lambda i: (1, i, 0)),
            pl.BlockSpec((tm, ROUTE_W), lambda i: (i, 0)),
            pl.BlockSpec((1, d), lambda i: (0, 0)),
        ],
        out_specs=pl.BlockSpec((tm, d), lambda i: (i, 0)),
        compiler_params=_cparams(("parallel",)),
        name="moe_combine",
    )(x2, yg, yg, route, norm_g.reshape(1, d))


def _moe(h2p, route, x2, layer, w_gate, w_up, w_down, norm_f_g, final_norm):
    n, d = x2.shape
    n_slots = TOP_K * n + N_EXPERTS * MOE_TM
    pos, tile_expert, tile_rows, tile_first = _dispatch_plan(route, n_slots, layer * N_EXPERTS)
    xs = _sc_scatter_rows(h2p, pos, n_slots)
    ys = _moe_tiles(xs, tile_expert, tile_rows, tile_first, w_gate, w_up, w_down)
    yg = _sc_gather_rows(ys, pos.reshape(-1)).reshape(TOP_K, n, d // 2)
    return _moe_combine(x2, yg, route, norm_f_g, final_norm)


def kernel(x, positions, norm1_g, w_in, b_in, attn_sinks, sgu_ln_g, sgu_ln_b, w_spatial, b_spatial,
           w_proj_a, w_proj_b, w_proj_c, w_out, norm2_g, w_router_group, w_router_expert,
           w_expert_gate, w_expert_up, w_expert_down, norm_f_g):
    bsz, s, d = x.shape
    depth = w_in.shape[0]
    wg_all = w_expert_gate.reshape(depth * N_EXPERTS, d, D_EXPERT)
    wu_all = w_expert_up.reshape(depth * N_EXPERTS, d, D_EXPERT)
    wd_all = w_expert_down.reshape(depth * N_EXPERTS, D_EXPERT, d)
    assert d == D_MODEL and s % (C_PAIRS[-1][1] * BLOCK) == 0
    tabs = _rope_tables(positions)
    for l in range(depth):
        z = _in_proj(x, norm1_g[l], w_in[l].astype(BF16), b_in[l], tabs)
        oa = _attn_a(z, attn_sinks[l])
        ob = _sgu(z, sgu_ln_g[l], sgu_ln_b[l], w_spatial[l], b_spatial[l])
        oc = _attn_c(z)
        x2, h2, route = _merge(
            x.reshape(bsz * s, d), oa.reshape(bsz * s, A_Q_W), ob.reshape(bsz * s, B_WIDTH),
            oc.reshape(bsz * s, C_OUT_W), z.reshape(bsz * s, D_IN), w_proj_a[l], w_proj_b[l], w_proj_c[l], w_out[l],
            norm2_g[l], w_router_group[l], w_router_expert[l])
        x = _moe(h2, route, x2, l, wg_all, wu_all, wd_all, norm_f_g, l == depth - 1).reshape(bsz, s, d)
    return x
```

```python
import functools

import jax
import jax.numpy as jnp
from jax import lax
from jax.experimental import pallas as pl
from jax.experimental.pallas import tpu as pltpu
from jax.experimental.pallas import tpu_sc as plsc

F32 = jnp.float32
BF16 = jnp.bfloat16

D_MODEL = 1024
HEAD_DIM = 64
ROT_DIM = HEAD_DIM // 4
ROPE_THETA = 500000.0
BLOCK = 128
EPS = 1e-5
NEG_INF = -1e30

A_Q_HEADS = 16
A_KV_HEADS = 4
A_REP = A_Q_HEADS // A_KV_HEADS
A_WINDOW = 128
A_Q_W = A_Q_HEADS * HEAD_DIM
A_KV_W = A_KV_HEADS * HEAD_DIM

B_GROUPS = 12
B_CH = 64
B_WIDTH = B_GROUPS * B_CH
B_CHUNK = 128

C_PAIRS = ((128, 1), (512, 4), (2048, 16))
C_HEADS_PER_GROUP = 4
C_HEADS = C_HEADS_PER_GROUP * len(C_PAIRS)
C_OUT_W = C_HEADS_PER_GROUP * HEAD_DIM

N_BRANCH = 3
GATE_W = N_BRANCH * D_MODEL
D_IN = A_Q_W + 2 * A_KV_W + 2 * B_WIDTH + 3 * C_HEADS * HEAD_DIM + GATE_W

N_EXPERT_GROUPS = 4
EXPERTS_PER_GROUP = 8
N_EXPERTS = N_EXPERT_GROUPS * EXPERTS_PER_GROUP
TOP_K = 2
D_EXPERT = 256

LANES = 128
MXU_W = 256
VMEM_LIMIT = 56 * 1024 * 1024

IN_TN = 256
COL_AQ = 0
COL_AK = A_Q_W // IN_TN
COL_AV = COL_AK + A_KV_W // IN_TN
COL_BU = COL_AV + A_KV_W // IN_TN
COL_BV = COL_BU + B_WIDTH // IN_TN
COL_CQ = COL_BV + B_WIDTH // IN_TN
COL_CK = COL_CQ + C_HEADS * HEAD_DIM // IN_TN
COL_CV = COL_CK + C_HEADS * HEAD_DIM // IN_TN
COL_G = COL_CV + C_HEADS * HEAD_DIM // IN_TN
N_COL_TILES = D_IN // IN_TN

ROUTE_W = LANES
ROUTE_E0 = N_EXPERT_GROUPS


def _cparams(sem):
    return pltpu.CompilerParams(dimension_semantics=sem, vmem_limit_bytes=VMEM_LIMIT)


def _rope_table_kernel(pos_ref, inv_ref, c_ref, s1_ref, s2_ref):
    lane = lax.broadcasted_iota(jnp.int32, (1, LANES), 1)
    d = lane & (HEAD_DIM - 1)
    ang = pos_ref[...].astype(F32) * inv_ref[...]
    c = jnp.cos(ang)
    s = jnp.sin(ang)
    half = ROT_DIM // 2
    c_ref[...] = jnp.where(d < ROT_DIM, c, 1.0)
    s1_ref[...] = jnp.where(d < half, -s, 0.0)
    s2_ref[...] = jnp.where((d >= half) & (d < ROT_DIM), s, 0.0)


def _rope_tables(positions):
    n = positions.size
    inv = ROPE_THETA ** (-jnp.arange(0, ROT_DIM, 2, dtype=F32) / ROT_DIM)
    inv_lane = jnp.tile(inv, LANES // inv.shape[0]).reshape(1, LANES)
    pos_b = jnp.broadcast_to(positions.reshape(n, 1), (n, LANES))
    tm = 1024
    spec = pl.BlockSpec((tm, LANES), lambda i: (i, 0))
    return pl.pallas_call(
        _rope_table_kernel,
        out_shape=[jax.ShapeDtypeStruct((n, LANES), F32)] * 3,
        grid=(n // tm,),
        in_specs=[spec, pl.BlockSpec((1, LANES), lambda i: (0, 0))],
        out_specs=[spec, spec, spec],
        compiler_params=_cparams(("parallel",)),
        name="rope_tables",
    )(pos_b, inv_lane)


def _is_rope_tile(tile):
    return (tile < COL_AV) | ((tile >= COL_CQ) & (tile < COL_CV))


def _in_proj_kernel(x_ref, g_ref, w_ref, b_ref, c_ref, s1_ref, s2_ref, z_ref, h_ref,
                    acc0_ref, acc1_ref):
    j = pl.program_id(1)

    @pl.when(j == 0)
    def _():
        x = x_ref[0]
        ms = jnp.mean(x * x, -1, keepdims=True)
        h_ref[...] = (x * lax.rsqrt(ms + EPS) * g_ref[...]).astype(BF16)

    tile = j - 1
    is_q = (tile < COL_AK) | ((tile >= COL_CQ) & (tile < COL_CK))
    is_rope = _is_rope_tile(tile)

    def matmul_into(acc_ref):
        acc_ref[...] = jnp.dot(h_ref[...], w_ref[...], preferred_element_type=F32) + b_ref[...]

    def finish(acc_ref, rope):
        if not rope:
            z_ref[0] = acc_ref[...].astype(BF16)
            return
        scale = jnp.where(is_q, HEAD_DIM ** -0.5, 1.0).astype(F32)
        for t in range(IN_TN // LANES):
            a = acc_ref[:, t * LANES:(t + 1) * LANES]
            r = (a * c_ref[...] + pltpu.roll(a, LANES - ROT_DIM // 2, 1) * s1_ref[...]
                 + pltpu.roll(a, ROT_DIM // 2, 1) * s2_ref[...])
            z_ref[0, :, t * LANES:(t + 1) * LANES] = (r * scale).astype(BF16)

    accs = (acc0_ref, acc1_ref)
    inner = (j > 0) & (j < N_COL_TILES)
    for parity in range(2):
        for rope in (False, True):
            @pl.when(inner & (j % 2 == parity) & (is_rope == rope))
            def _(parity=parity, rope=rope):
                matmul_into(accs[parity])
                finish(accs[1 - parity], rope)

    @pl.when(j == 0)
    def _():
        matmul_into(accs[0])

    @pl.when(j == N_COL_TILES)
    def _():
        finish(accs[(N_COL_TILES - 1) % 2], bool(_is_rope_tile(N_COL_TILES - 1)))


def _in_proj(x, g, w, b, tabs):
    bsz, s, d = x.shape
    c, s1, s2 = tabs
    tab_spec = pl.BlockSpec((s, LANES), lambda bi, j: (bi, 0))
    last = N_COL_TILES - 1
    return pl.pallas_call(
        _in_proj_kernel,
        out_shape=jax.ShapeDtypeStruct((bsz, s, D_IN), BF16),
        grid=(bsz, N_COL_TILES + 1),
        in_specs=[
            pl.BlockSpec((1, s, d), lambda bi, j: (bi, 0, 0)),
            pl.BlockSpec((1, d), lambda bi, j: (0, 0)),
            pl.BlockSpec((d, IN_TN), lambda bi, j: (0, jnp.minimum(j, last))),
            pl.BlockSpec((1, IN_TN), lambda bi, j: (0, jnp.minimum(j, last))),
            tab_spec, tab_spec, tab_spec,
        ],
        out_specs=pl.BlockSpec((1, s, IN_TN), lambda bi, j: (bi, 0, jnp.maximum(j - 1, 0))),
        scratch_shapes=[pltpu.VMEM((s, d), BF16), pltpu.VMEM((s, IN_TN), F32),
                        pltpu.VMEM((s, IN_TN), F32)],
        compiler_params=_cparams(("parallel", "arbitrary")),
        name="in_proj",
    )(x, g.reshape(1, d), w, b.reshape(1, D_IN), c, s1, s2)


STACK = C_HEADS_PER_GROUP * BLOCK
STEP_UNROLL = 4


def _head_stack(blk):
    seg = lax.broadcasted_iota(jnp.int32, (1, MXU_W), 1) // HEAD_DIM
    rowseg = lax.broadcasted_iota(jnp.int32, (STACK, 1), 0) // BLOCK
    return jnp.where(seg == rowseg, jnp.concatenate([blk] * C_HEADS_PER_GROUP, axis=0),
                     jnp.zeros((), blk.dtype))


def _band_step(q, kp_ref, vp_ref, blk, n, has_prev, diag_key, sink_col, want_lse):
    return _band_blocks([(q, blk, n)], kp_ref, vp_ref, has_prev, diag_key, sink_col, want_lse)[0]


def _band_blocks(jobs, kp_ref, vp_ref, has_prev, diag_key, sink_col, want_lse):
    contract = (((1,), (1,)), ((), ()))
    i_loc = lax.broadcasted_iota(jnp.int32, (STACK, 1), 0) & (BLOCK - 1)
    jj = lax.broadcasted_iota(jnp.int32, (1, BLOCK), 1)
    upper = jj > i_loc
    seg = lax.broadcasted_iota(jnp.int32, (1, MXU_W), 1) // HEAD_DIM
    zero_b = jnp.zeros((), BF16)

    def key_rows(blk):
        lo = pl.multiple_of(blk * BLOCK, BLOCK)
        if has_prev:
            return pl.ds(lo, 2 * BLOCK)
        return pl.ds(pl.multiple_of(lo + BLOCK, BLOCK), BLOCK)

    scores = [lax.dot_general(_head_stack(q), kp_ref[key_rows(blk), :], contract,
                              preferred_element_type=F32) for q, blk, _ in jobs]

    def own_lanes(x):
        out = x[0:BLOCK]
        for h in range(1, C_HEADS_PER_GROUP):
            out = jnp.where(seg == h, x[h * BLOCK:(h + 1) * BLOCK], out)
        return out

    probs, stats = [], []
    for s, (_, blk, n) in zip(scores, jobs):
        if has_prev:
            sp = s[:, :BLOCK] + jnp.where(n == 0, NEG_INF, 0.0).astype(F32)
            f = jnp.where(upper, sp, s[:, BLOCK:])
        else:
            f = jnp.where(upper, NEG_INF, s)
        m = jnp.max(f, -1, keepdims=True)
        if diag_key:
            sd = jnp.sum(jnp.where(jj == i_loc, sp, 0.0), -1, keepdims=True)
            m = jnp.maximum(m, sd)
        if sink_col is not None:
            m = jnp.maximum(m, sink_col)
        p = jnp.exp(f - m)
        den = jnp.sum(p, -1, keepdims=True)
        pd = None
        if diag_key:
            pd = jnp.exp(sd - m)
            den = den + pd
        if sink_col is not None:
            den = den + jnp.exp(sink_col - m)
        inv = 1.0 / den
        pn = (p * inv).astype(BF16)
        if has_prev:
            below = zero_b
            if diag_key:
                below = jnp.where(jj == i_loc, pd * inv, 0.0).astype(BF16)
            pn = jnp.concatenate([jnp.where(upper, pn, below), jnp.where(upper, zero_b, pn)],
                                 axis=1)
        probs.append(pn)
        stats.append((m, den))

    outs = []
    for pn, (m, den), (_, blk, _) in zip(probs, stats, jobs):
        o = own_lanes(jnp.dot(pn, vp_ref[key_rows(blk), :], preferred_element_type=F32))
        lse = own_lanes(m + jnp.log(den)) + jnp.zeros((BLOCK, MXU_W), F32) if want_lse else None
        outs.append((o, lse))
    return outs


def _attn_a_kernel(sink_ref, q_ref, k_ref, v_ref, o_ref, kp_ref, vp_ref):
    nb = q_ref.shape[1] // BLOCK
    rowseg = lax.broadcasted_iota(jnp.int32, (STACK, 1), 0) // BLOCK
    kp_ref[pl.ds(0, BLOCK), :] = jnp.zeros((BLOCK, MXU_W), BF16)
    vp_ref[pl.ds(0, BLOCK), :] = jnp.zeros((BLOCK, MXU_W), BF16)
    for g in range(A_KV_HEADS):
        kv_cols = slice(g * HEAD_DIM, (g + 1) * HEAD_DIM)
        q_cols = slice(g * MXU_W, (g + 1) * MXU_W)

        def build(n, carry):
            rows = pl.ds(pl.multiple_of(n * BLOCK, BLOCK), BLOCK)
            dst = pl.ds(pl.multiple_of((n + 1) * BLOCK, BLOCK), BLOCK)
            kp_ref[dst, :] = jnp.concatenate([k_ref[0, rows, kv_cols]] * A_REP, axis=1)
            vp_ref[dst, :] = jnp.concatenate([v_ref[0, rows, kv_cols]] * A_REP, axis=1)
            return carry

        lax.fori_loop(0, nb, build, 0)
        sink_col = jnp.zeros((STACK, 1), F32)
        for r in range(A_REP):
            sink_col = jnp.where(rowseg == r, sink_ref[g * A_REP + r], sink_col)

        def step(it, carry):
            blocks = [it * STEP_UNROLL + u for u in range(STEP_UNROLL)]
            rows = [pl.ds(pl.multiple_of(n * BLOCK, BLOCK), BLOCK) for n in blocks]
            jobs = [(q_ref[0, r, q_cols], n, n) for r, n in zip(rows, blocks)]
            outs = _band_blocks(jobs, kp_ref, vp_ref, True, False, sink_col, False)
            for r, (o, _) in zip(rows, outs):
                o_ref[0, r, q_cols] = o.astype(o_ref.dtype)
            return carry

        lax.fori_loop(0, nb // STEP_UNROLL, step, 0)


def _attn_a(z, sinks):
    bsz, s, _ = z.shape
    nb = s // BLOCK
    grid_spec = pltpu.PrefetchScalarGridSpec(
        num_scalar_prefetch=1,
        grid=(bsz,),
        in_specs=[
            pl.BlockSpec((1, s, A_Q_W), lambda bi, sk: (bi, 0, 0)),
            pl.BlockSpec((1, s, A_KV_W), lambda bi, sk: (bi, 0, COL_AK)),
            pl.BlockSpec((1, s, A_KV_W), lambda bi, sk: (bi, 0, COL_AV)),
        ],
        out_specs=pl.BlockSpec((1, s, A_Q_W), lambda bi, sk: (bi, 0, 0)),
        scratch_shapes=[pltpu.VMEM(((nb + 1) * BLOCK, MXU_W), BF16),
                        pltpu.VMEM(((nb + 1) * BLOCK, MXU_W), BF16)],
    )
    return pl.pallas_call(
        _attn_a_kernel,
        out_shape=jax.ShapeDtypeStruct((bsz, s, A_Q_W), BF16),
        grid_spec=grid_spec,
        compiler_params=_cparams(("parallel",)),
        name="attn_a",
    )(sinks, z, z, z)


def _attn_c_kernel(q_ref, k_ref, v_ref, oc_ref, stage_ref, pq_ref, pk_ref, pv_ref, og_ref, lg_ref):
    s = q_ref.shape[1]
    nblk = s // BLOCK
    halves = MXU_W // LANES
    pk_ref[pl.ds(0, BLOCK), :] = jnp.zeros((BLOCK, MXU_W), BF16)
    pv_ref[pl.ds(0, BLOCK), :] = jnp.zeros((BLOCK, MXU_W), BF16)

    def class_rows(idx, nb, d):
        r = idx // nb
        n = idx - r * nb
        start = r + n * (BLOCK * d)
        rows = pl.ds(start, BLOCK, stride=d) if d > 1 else pl.ds(pl.multiple_of(start, BLOCK), BLOCK)
        return n, rows

    def stage(src_ref, cols):
        x = src_ref[0, :, cols].astype(F32)
        for t in range(halves):
            stage_ref[t] = x[:, t * LANES:(t + 1) * LANES]

    def staged_block(rows):
        return jnp.concatenate([stage_ref[t, rows, :] for t in range(halves)], axis=1).astype(BF16)

    for g, (win, d) in enumerate(C_PAIRS):
        cols = slice(g * MXU_W, (g + 1) * MXU_W)
        nb = nblk // d
        has_prev = nb > 1
        assert win // d == BLOCK

        def class_major(dst_ref, offset):
            def build(idx, carry):
                _, rows = class_rows(idx, nb, d)
                dst = pl.ds(pl.multiple_of((idx + offset) * BLOCK, BLOCK), BLOCK)
                dst_ref[dst, :] = staged_block(rows)
                return carry
            return build

        stage(k_ref, cols)
        lax.fori_loop(0, nblk, class_major(pk_ref, 1), 0)
        stage(v_ref, cols)
        lax.fori_loop(0, nblk, class_major(pv_ref, 1), 0)
        stage(q_ref, cols)
        lax.fori_loop(0, nblk, class_major(pq_ref, 0), 0)

        def step(it, carry):
            jobs, dsts = [], []
            for u in range(STEP_UNROLL):
                idx = it * STEP_UNROLL + u
                n, rows = class_rows(idx, nb, d)
                here = pl.ds(pl.multiple_of(idx * BLOCK, BLOCK), BLOCK)
                jobs.append((pq_ref[here, :], idx, n))
                dsts.append(rows)
            outs = _band_blocks(jobs, pk_ref, pv_ref, has_prev, has_prev, None, True)
            for rows, (o, lse) in zip(dsts, outs):
                for t in range(halves):
                    og_ref[g, t, rows, :] = o[:, t * LANES:(t + 1) * LANES]
                    lg_ref[g, t, rows, :] = lse[:, t * LANES:(t + 1) * LANES]
            return carry

        lax.fori_loop(0, nblk // STEP_UNROLL, step, 0)

    chunk = 2 * BLOCK

    def combine(i, carry):
        rows = pl.ds(pl.multiple_of(i * chunk, chunk), chunk)
        for t in range(halves):
            ls = [lg_ref[g, t, rows, :] for g in range(len(C_PAIRS))]
            m = jnp.maximum(jnp.maximum(ls[0], ls[1]), ls[2])
            es = [jnp.exp(l - m) for l in ls]
            num = (es[0] * og_ref[0, t, rows, :] + es[1] * og_ref[1, t, rows, :]
                   + es[2] * og_ref[2, t, rows, :])
            oc_ref[0, rows, t * LANES:(t + 1) * LANES] = (
                num / (es[0] + es[1] + es[2])).astype(oc_ref.dtype)
        return carry

    lax.fori_loop(0, s // chunk, combine, 0)


def _attn_c(z):
    bsz, s, _ = z.shape
    nblk = s // BLOCK
    width = C_HEADS * HEAD_DIM
    halves = MXU_W // LANES

    def in_spec(col):
        return pl.BlockSpec((1, s, width), lambda bi: (bi, 0, col * IN_TN // width))

    return pl.pallas_call(
        _attn_c_kernel,
        out_shape=jax.ShapeDtypeStruct((bsz, s, C_OUT_W), BF16),
        grid=(bsz,),
        in_specs=[in_spec(COL_CQ), in_spec(COL_CK), in_spec(COL_CV)],
        out_specs=pl.BlockSpec((1, s, C_OUT_W), lambda bi: (bi, 0, 0)),
        scratch_shapes=[
            pltpu.VMEM((halves, s, LANES), F32),
            pltpu.VMEM((s, MXU_W), BF16),
            pltpu.VMEM((s + BLOCK, MXU_W), BF16),
            pltpu.VMEM((s + BLOCK, MXU_W), BF16),
            pltpu.VMEM((len(C_PAIRS), halves, s, LANES), F32),
            pltpu.VMEM((len(C_PAIRS), halves, s, LANES), F32),
        ],
        compiler_params=_cparams(("parallel",)),
        name="attn_c",
    )(z, z, z)


def _sgu_kernel(zu_ref, zv_ref, lng_ref, lnb_ref, ws_ref, bs_ref, o_ref):
    tm = zu_ref.shape[1]
    u = jax.nn.gelu(zu_ref[0].astype(F32))
    v = jax.nn.gelu(zv_ref[0].astype(F32))
    mu = jnp.mean(v, -1, keepdims=True)
    var = jnp.mean(jnp.square(v - mu), -1, keepdims=True)
    vn = ((v - mu) * lax.rsqrt(var + EPS) * lng_ref[...] + lnb_ref[...]).astype(BF16)
    causal = (lax.broadcasted_iota(jnp.int32, (B_CHUNK, 1), 0)
              >= lax.broadcasted_iota(jnp.int32, (1, B_CHUNK), 1))
    first = lax.broadcasted_iota(jnp.int32, (1, LANES), 1) < B_CH
    zero = jnp.zeros((), BF16)
    for p in range(B_WIDTH // LANES):
        w0 = jnp.where(causal, ws_ref[2 * p], zero)
        w1 = jnp.where(causal, ws_ref[2 * p + 1], zero)
        cols = slice(p * LANES, (p + 1) * LANES)
        for c in range(tm // B_CHUNK):
            rows = slice(c * B_CHUNK, (c + 1) * B_CHUNK)
            vv = vn[rows, cols]
            sv = jnp.where(first,
                           jnp.dot(w0, vv, preferred_element_type=F32),
                           jnp.dot(w1, vv, preferred_element_type=F32)) + bs_ref[:, cols]
            o_ref[0, rows, cols] = (u[rows, cols] * sv).astype(o_ref.dtype)


def _sgu(z, ln_g, ln_b, w_s, b_s):
    bsz, s, _ = z.shape
    tm = 512
    bias = jnp.repeat(b_s.T, B_CH, axis=1)
    col_u = COL_BU * IN_TN // B_WIDTH
    col_v = COL_BV * IN_TN // B_WIDTH
    return pl.pallas_call(
        _sgu_kernel,
        out_shape=jax.ShapeDtypeStruct((bsz, s, B_WIDTH), BF16),
        grid=(bsz, s // tm),
        in_specs=[
            pl.BlockSpec((1, tm, B_WIDTH), lambda bi, i: (bi, i, col_u)),
            pl.BlockSpec((1, tm, B_WIDTH), lambda bi, i: (bi, i, col_v)),
            pl.BlockSpec((1, B_WIDTH), lambda bi, i: (0, 0)),
            pl.BlockSpec((1, B_WIDTH), lambda bi, i: (0, 0)),
            pl.BlockSpec((B_GROUPS, B_CHUNK, B_CHUNK), lambda bi, i: (0, 0, 0)),
            pl.BlockSpec((B_CHUNK, B_WIDTH), lambda bi, i: (0, 0)),
        ],
        out_specs=pl.BlockSpec((1, tm, B_WIDTH), lambda bi, i: (bi, i, 0)),
        compiler_params=_cparams(("parallel", "parallel")),
        name="sgu",
    )(z, z, ln_g.reshape(1, B_WIDTH), ln_b.reshape(1, B_WIDTH), w_s.astype(BF16), bias)


def _route(logits):
    lane = lax.broadcasted_iota(jnp.int32, (1, ROUTE_W), 1)
    lane_f = lane.astype(F32)
    is_g = lane < N_EXPERT_GROUPS
    lg = jnp.where(is_g, logits, NEG_INF)
    mg = jnp.max(lg, -1, keepdims=True)
    pg_top = 1.0 / jnp.sum(jnp.where(is_g, jnp.exp(lg - mg), 0.0), -1, keepdims=True)
    g_idx = jnp.min(jnp.where(lg == mg, lane_f, float(ROUTE_W)), -1, keepdims=True)
    e_group = ((lane - ROUTE_E0) // EXPERTS_PER_GROUP).astype(F32)
    in_grp = (lane >= ROUTE_E0) & (lane < ROUTE_E0 + N_EXPERTS) & (e_group == g_idx)
    le = jnp.where(in_grp, logits, NEG_INF)
    m1 = jnp.max(le, -1, keepdims=True)
    i1 = jnp.min(jnp.where(le == m1, lane_f, float(ROUTE_W)), -1, keepdims=True)
    le2 = jnp.where(lane_f == i1, NEG_INF, le)
    m2 = jnp.max(le2, -1, keepdims=True)
    i2 = jnp.min(jnp.where(le2 == m2, lane_f, float(ROUTE_W)), -1, keepdims=True)
    t = jnp.exp(m2 - m1)
    w1 = pg_top / (1.0 + t)
    w2 = w1 * t
    picks = (i1 - ROUTE_E0, i2 - ROUTE_E0, w1, w2)
    out = jnp.zeros(logits.shape, F32)
    for k, val in enumerate(picks):
        out = jnp.where(lane == k, val, out)
    return out


def _merge_kernel(x_ref, oa_ref, ob_ref, oc_ref,
                  g0_ref, g1_ref, g2_ref, g3_ref, wa_ref, wb_ref, wc_ref, wo_ref, n2_ref, wr_ref,
                  xo_ref, h2_ref, route_ref):
    zg = jnp.concatenate([g0_ref[...], g1_ref[...], g2_ref[...], g3_ref[...]], axis=1).astype(F32)
    gates = jax.nn.sigmoid(zg)
    pa = jnp.dot(oa_ref[...], wa_ref[...], preferred_element_type=F32)
    pb = jnp.dot(ob_ref[...], wb_ref[...], preferred_element_type=F32)
    pc = jnp.dot(oc_ref[...], wc_ref[...], preferred_element_type=F32)
    merged = (gates[:, :D_MODEL] * pa + gates[:, D_MODEL:2 * D_MODEL] * pb
              + gates[:, 2 * D_MODEL:] * pc)
    xn = x_ref[...] + jnp.dot(merged.astype(BF16), wo_ref[...], preferred_element_type=F32)
    xo_ref[...] = xn
    ms = jnp.mean(xn * xn, -1, keepdims=True)
    h2 = xn * lax.rsqrt(ms + EPS) * n2_ref[...]
    h2_ref[...] = _pack_bf16_pairs(h2)
    route_ref[...] = _route(jnp.dot(h2.astype(BF16), wr_ref[...], preferred_element_type=F32))


def _merge(x2, oa, ob, oc, z2, wa, wb, wc, wo, n2g, w_rg, w_re):
    n, d = x2.shape
    tm = 512
    wr = jnp.concatenate(
        [w_rg, w_re, jnp.zeros((d, ROUTE_W - N_EXPERT_GROUPS - N_EXPERTS), w_rg.dtype)], axis=1)
    gate_w = GATE_W // 4
    gate_col0 = COL_G * IN_TN // gate_w

    def rows(width):
        return pl.BlockSpec((tm, width), lambda i: (i, 0))

    def full(shape):
        return pl.BlockSpec(shape, lambda i: (0, 0))

    gate_specs = [pl.BlockSpec((tm, gate_w), lambda i, k=k: (i, gate_col0 + k)) for k in range(4)]
    return pl.pallas_call(
        _merge_kernel,
        out_shape=[jax.ShapeDtypeStruct((n, d), F32),
                   jax.ShapeDtypeStruct((n, d // 2), jnp.int32),
                   jax.ShapeDtypeStruct((n, ROUTE_W), F32)],
        grid=(n // tm,),
        in_specs=[rows(d), rows(A_Q_W), rows(B_WIDTH), rows(C_OUT_W),
                  *gate_specs,
                  full((A_Q_W, d)), full((B_WIDTH, d)), full((C_OUT_W, d)), full((d, d)),
                  full((1, d)), full((d, ROUTE_W))],
        out_specs=[rows(d), rows(d // 2), rows(ROUTE_W)],
        compiler_params=_cparams(("parallel",)),
        name="merge",
    )(x2, oa, ob, oc, z2, z2, z2, z2,
      wa.astype(BF16), wb.astype(BF16), wc.astype(BF16), wo.astype(BF16),
      n2g.reshape(1, d), wr.astype(BF16))


MOE_TM = 512
SC_CORES = 2
SC_SUBCORES = 16
SC_WORKERS = SC_CORES * SC_SUBCORES
SC_CHUNK = 128


def _pack_bf16_pairs(x):
    w = x.shape[1] // 2
    lo = pltpu.bitcast(x[:, :w].astype(BF16).astype(F32), jnp.int32)
    hi = pltpu.bitcast(x[:, w:].astype(BF16).astype(F32), jnp.int32)
    return (hi & jnp.int32(-65536)) | lax.shift_right_logical(lo, jnp.int32(16))


def _unpack_bf16_pairs(p):
    lo = pltpu.bitcast(lax.shift_left(p, jnp.int32(16)), F32)
    hi = pltpu.bitcast(p & jnp.int32(-65536), F32)
    return jnp.concatenate([lo, hi], axis=1)


def _sc_gather_rows(table, idx):
    rows, width = idx.shape[0], table.shape[1]
    per_worker = rows // SC_WORKERS
    n_chunks = per_worker // SC_CHUNK
    assert per_worker * SC_WORKERS == rows and n_chunks * SC_CHUNK == per_worker
    mesh = plsc.VectorSubcoreMesh(core_axis_name="c", subcore_axis_name="s",
                                  num_cores=SC_CORES, num_subcores=SC_SUBCORES)

    @functools.partial(
        pl.kernel, mesh=mesh,
        out_type=jax.ShapeDtypeStruct((rows, width), table.dtype),
        scratch_types=[pltpu.VMEM((n_chunks, SC_CHUNK), jnp.int32),
                       pltpu.VMEM((SC_CHUNK, width), table.dtype),
                       pltpu.SemaphoreType.DMA],
        name="sc_gather_rows",
    )
    def gather(table_hbm, idx_hbm, out_hbm, idx_v, rows_v, sem):
        wid = lax.axis_index("s") * SC_CORES + lax.axis_index("c")
        pltpu.sync_copy(idx_hbm.at[wid], idx_v)
        base = wid * per_worker

        @pl.loop(0, n_chunks)
        def _(c):
            pltpu.async_copy(table_hbm.at[idx_v.at[c]], rows_v, sem).wait()
            pltpu.sync_copy(rows_v, out_hbm.at[pl.ds(base + c * SC_CHUNK, SC_CHUNK)])

    return gather(table, idx.reshape(SC_WORKERS, n_chunks, SC_CHUNK))


def _sc_scatter_rows(rows, pos, n_slots):
    n, width = rows.shape
    per_worker = n // SC_WORKERS
    n_chunks = per_worker // SC_CHUNK
    assert per_worker * SC_WORKERS == n and n_chunks * SC_CHUNK == per_worker
    mesh = plsc.VectorSubcoreMesh(core_axis_name="c", subcore_axis_name="s",
                                  num_cores=SC_CORES, num_subcores=SC_SUBCORES)

    @functools.partial(
        pl.kernel, mesh=mesh,
        out_type=jax.ShapeDtypeStruct((n_slots, width), rows.dtype),
        scratch_types=[pltpu.VMEM((TOP_K, n_chunks, SC_CHUNK), jnp.int32),
                       pltpu.VMEM((SC_CHUNK, width), rows.dtype),
                       pltpu.SemaphoreType.DMA],
        name="sc_scatter_rows",
    )
    def scatter(rows_hbm, pos_hbm, out_hbm, pos_v, rows_v, sem):
        wid = lax.axis_index("s") * SC_CORES + lax.axis_index("c")
        for k in range(TOP_K):
            pltpu.sync_copy(pos_hbm.at[k, wid], pos_v.at[k])
        base = wid * per_worker

        @pl.loop(0, n_chunks)
        def _(c):
            pltpu.sync_copy(rows_hbm.at[pl.ds(base + c * SC_CHUNK, SC_CHUNK)], rows_v)
            for k in range(TOP_K):
                pltpu.async_copy(rows_v, out_hbm.at[pos_v.at[k, c]], sem).wait()

    return scatter(rows, pos.reshape(TOP_K, SC_WORKERS, n_chunks, SC_CHUNK))


def _dispatch_plan(route, n_slots, expert_base):
    n = route.shape[0]
    e = route[:, 0:TOP_K].astype(jnp.int32).reshape(-1)
    experts = jnp.arange(N_EXPERTS, dtype=jnp.int32)
    onehot = (e[:, None] == experts[None, :]).astype(jnp.int32)
    csum = jnp.cumsum(onehot, axis=0)
    rank = jnp.sum((csum - onehot) * onehot, axis=1)
    counts = csum[-1]
    padded = (counts + MOE_TM - 1) // MOE_TM * MOE_TM
    ends = jnp.cumsum(padded)
    starts = ends - padded
    pos = starts[e] + rank
    tile_start = jnp.arange(n_slots // MOE_TM, dtype=jnp.int32) * MOE_TM
    tile_e = jnp.minimum(jnp.sum((tile_start[:, None] >= ends[None, :]).astype(jnp.int32), axis=1),
                         N_EXPERTS - 1)
    tile_rows = jnp.clip(counts[tile_e] - (tile_start - starts[tile_e]), 0, MOE_TM)
    tile_rows = jnp.where(tile_start < ends[-1], tile_rows, 0)
    first = jnp.concatenate([jnp.ones((1,), jnp.int32),
                             (tile_e[1:] != tile_e[:-1]).astype(jnp.int32)])
    return pos.reshape(n, TOP_K).T, tile_e + expert_base, tile_rows.astype(jnp.int32), first


def _moe_tile_kernel(te_ref, tr_ref, tf_ref, xs_ref, wg_ref, wu_ref, wd_ref, ys_ref,
                     wg_s, wu_s, wd_s):
    t = pl.program_id(0)

    @pl.when(tf_ref[t] != 0)
    def _():
        wg_s[...] = wg_ref[0].astype(BF16)
        wu_s[...] = wu_ref[0].astype(BF16)
        wd_s[...] = wd_ref[0].astype(BF16)

    @pl.when(tr_ref[t] != 0)
    def _():
        occupied = lax.broadcasted_iota(jnp.int32, (MOE_TM, 1), 0) < tr_ref[t]
        x = jnp.where(occupied, _unpack_bf16_pairs(xs_ref[...]), 0.0).astype(BF16)
        hg = jnp.dot(x, wg_s[...], preferred_element_type=F32)
        hu = jnp.dot(x, wu_s[...], preferred_element_type=F32)
        a = (jax.nn.silu(hg) * hu).astype(BF16)
        ys_ref[...] = _pack_bf16_pairs(jnp.dot(a, wd_s[...], preferred_element_type=F32))

    @pl.when(tr_ref[t] == 0)
    def _():
        ys_ref[...] = jnp.zeros_like(ys_ref)


def _moe_tiles(xs, tile_expert, tile_rows, tile_first, w_gate, w_up, w_down):
    n_slots, half = xs.shape
    d = 2 * half
    grid_spec = pltpu.PrefetchScalarGridSpec(
        num_scalar_prefetch=3,
        grid=(n_slots // MOE_TM,),
        in_specs=[
            pl.BlockSpec((MOE_TM, half), lambda t, te, tr, tf: (t, 0)),
            pl.BlockSpec((1, d, D_EXPERT), lambda t, te, tr, tf: (te[t], 0, 0)),
            pl.BlockSpec((1, d, D_EXPERT), lambda t, te, tr, tf: (te[t], 0, 0)),
            pl.BlockSpec((1, D_EXPERT, d), lambda t, te, tr, tf: (te[t], 0, 0)),
        ],
        out_specs=pl.BlockSpec((MOE_TM, half), lambda t, te, tr, tf: (t, 0)),
        scratch_shapes=[pltpu.VMEM((d, D_EXPERT), BF16), pltpu.VMEM((d, D_EXPERT), BF16),
                        pltpu.VMEM((D_EXPERT, d), BF16)],
    )
    return pl.pallas_call(
        _moe_tile_kernel,
        out_shape=jax.ShapeDtypeStruct((n_slots, half), jnp.int32),
        grid_spec=grid_spec,
        compiler_params=_cparams(("arbitrary",)),
        name="moe_tiles",
    )(tile_expert, tile_rows, tile_first, xs, w_gate, w_up, w_down)


def _moe_combine_kernel(x_ref, y0_ref, y1_ref, r_ref, g_ref, o_ref, *, final_norm):
    w0 = r_ref[:, 2:3]
    w1 = r_ref[:, 3:4]
    x = x_ref[...] + w0 * _unpack_bf16_pairs(y0_ref[0]) + w1 * _unpack_bf16_pairs(y1_ref[0])
    if final_norm:
        ms = jnp.mean(x * x, -1, keepdims=True)
        x = x * lax.rsqrt(ms + EPS) * g_ref[...]
    o_ref[...] = x


def _moe_combine(x2, yg, route, norm_g, final_norm):
    n, d = x2.shape
    tm = 1024
    return pl.pallas_call(
        functools.partial(_moe_combine_kernel, final_norm=final_norm),
        out_shape=jax.ShapeDtypeStruct((n, d), F32),
        grid=(n // tm,),
        in_specs=[
            pl.BlockSpec((tm, d), lambda i: (i, 0)),
            pl.BlockSpec((1, tm, d // 2), lambda i: (0, i, 0)),
            pl.BlockSpec((1, tm, d // 2), lambda i: (1, i, 0)),
            pl.BlockSpec((tm, ROUTE_W), lambda i: (i, 0)),
            pl.BlockSpec((1, d), lambda i: (0, 0)),
        ],
        out_specs=pl.BlockSpec((tm, d), lambda i: (i, 0)),
        compiler_params=_cparams(("parallel",)),
        name="moe_combine",
    )(x2, yg, yg, route, norm_g.reshape(1, d))


def _moe(h2p, route, x2, layer, w_gate, w_up, w_down, norm_f_g, final_norm):
    n, d = x2.shape
    n_slots = TOP_K * n + N_EXPERTS * MOE_TM
    pos, tile_expert, tile_rows, tile_first = _dispatch_plan(route, n_slots, layer * N_EXPERTS)
    xs = _sc_scatter_rows(h2p, pos, n_slots)
    ys = _moe_tiles(xs, tile_expert, tile_rows, tile_first, w_gate, w_up, w_down)
    yg = _sc_gather_rows(ys, pos.reshape(-1)).reshape(TOP_K, n, d // 2)
    return _moe_combine(x2, yg, route, norm_f_g, final_norm)


def kernel(x, positions, norm1_g, w_in, b_in, attn_sinks, sgu_ln_g, sgu_ln_b, w_spatial, b_spatial,
           w_proj_a, w_proj_b, w_proj_c, w_out, norm2_g, w_router_group, w_router_expert,
           w_expert_gate, w_expert_up, w_expert_down, norm_f_g):
    bsz, s, d = x.shape
    depth = w_in.shape[0]
    wg_all = w_expert_gate.reshape(depth * N_EXPERTS, d, D_EXPERT)
    wu_all = w_expert_up.reshape(depth * N_EXPERTS, d, D_EXPERT)
    wd_all = w_expert_down.reshape(depth * N_EXPERTS, D_EXPERT, d)
    assert d == D_MODEL and s % (C_PAIRS[-1][1] * BLOCK) == 0
    tabs = _rope_tables(positions)
    for l in range(depth):
        z = _in_proj(x, norm1_g[l], w_in[l].astype(BF16), b_in[l], tabs)
        oa = _attn_a(z, attn_sinks[l])
        ob = _sgu(z, sgu_ln_g[l], sgu_ln_b[l], w_spatial[l], b_spatial[l])
        oc = _attn_c(z)
        x2, h2, route = _merge(
            x.reshape(bsz * s, d), oa.reshape(bsz * s, A_Q_W), ob.reshape(bsz * s, B_WIDTH),
            oc.reshape(bsz * s, C_OUT_W), z.reshape(bsz * s, D_IN), w_proj_a[l], w_proj_b[l], w_proj_c[l], w_out[l],
            norm2_g[l], w_router_group[l], w_router_expert[l])
        x = _moe(h2, route, x2, l, wg_all, wu_all, wd_all, norm_f_g, l == depth - 1).reshape(bsz, s, d)
    return x
```

```python
import functools

import jax
import jax.numpy as jnp
from jax import lax
from jax.experimental import pallas as pl
from jax.experimental.pallas import tpu as pltpu
from jax.experimental.pallas import tpu_sc as plsc

F32 = jnp.float32
BF16 = jnp.bfloat16

D_MODEL = 1024
HEAD_DIM = 64
ROT_DIM = HEAD_DIM // 4
ROPE_THETA = 500000.0
BLOCK = 128
EPS = 1e-5
NEG_INF = -1e30

A_Q_HEADS = 16
A_KV_HEADS = 4
A_REP = A_Q_HEADS // A_KV_HEADS
A_WINDOW = 128
A_Q_W = A_Q_HEADS * HEAD_DIM
A_KV_W = A_KV_HEADS * HEAD_DIM

B_GROUPS = 12
B_CH = 64
B_WIDTH = B_GROUPS * B_CH
B_CHUNK = 128

C_PAIRS = ((128, 1), (512, 4), (2048, 16))
C_HEADS_PER_GROUP = 4
C_HEADS = C_HEADS_PER_GROUP * len(C_PAIRS)
C_OUT_W = C_HEADS_PER_GROUP * HEAD_DIM

N_BRANCH = 3
GATE_W = N_BRANCH * D_MODEL
D_IN = A_Q_W + 2 * A_KV_W + 2 * B_WIDTH + 3 * C_HEADS * HEAD_DIM + GATE_W

N_EXPERT_GROUPS = 4
EXPERTS_PER_GROUP = 8
N_EXPERTS = N_EXPERT_GROUPS * EXPERTS_PER_GROUP
TOP_K = 2
D_EXPERT = 256

LANES = 128
MXU_W = 256
VMEM_LIMIT = 56 * 1024 * 1024

IN_TN = 256
COL_AQ = 0
COL_AK = A_Q_W // IN_TN
COL_AV = COL_AK + A_KV_W // IN_TN
COL_BU = COL_AV + A_KV_W // IN_TN
COL_BV = COL_BU + B_WIDTH // IN_TN
COL_CQ = COL_BV + B_WIDTH // IN_TN
COL_CK = COL_CQ + C_HEADS * HEAD_DIM // IN_TN
COL_CV = COL_CK + C_HEADS * HEAD_DIM // IN_TN
COL_G = COL_CV + C_HEADS * HEAD_DIM // IN_TN
N_COL_TILES = D_IN // IN_TN

ROUTE_W = LANES
ROUTE_E0 = N_EXPERT_GROUPS


def _cparams(sem):
    return pltpu.CompilerParams(dimension_semantics=sem, vmem_limit_bytes=VMEM_LIMIT)


def _rope_table_kernel(pos_ref, inv_ref, c_ref, s1_ref, s2_ref):
    lane = lax.broadcasted_iota(jnp.int32, (1, LANES), 1)
    d = lane & (HEAD_DIM - 1)
    ang = pos_ref[...].astype(F32) * inv_ref[...]
    c = jnp.cos(ang)
    s = jnp.sin(ang)
    half = ROT_DIM // 2
    c_ref[...] = jnp.where(d < ROT_DIM, c, 1.0)
    s1_ref[...] = jnp.where(d < half, -s, 0.0)
    s2_ref[...] = jnp.where((d >= half) & (d < ROT_DIM), s, 0.0)


def _rope_tables(positions):
    n = positions.size
    inv = ROPE_THETA ** (-jnp.arange(0, ROT_DIM, 2, dtype=F32) / ROT_DIM)
    inv_lane = jnp.tile(inv, LANES // inv.shape[0]).reshape(1, LANES)
    pos_b = jnp.broadcast_to(positions.reshape(n, 1), (n, LANES))
    tm = 1024
    spec = pl.BlockSpec((tm, LANES), lambda i: (i, 0))
    return pl.pallas_call(
        _rope_table_kernel,
        out_shape=[jax.ShapeDtypeStruct((n, LANES), F32)] * 3,
        grid=(n // tm,),
        in_specs=[spec, pl.BlockSpec((1, LANES), lambda i: (0, 0))],
        out_specs=[spec, spec, spec],
        compiler_params=_cparams(("parallel",)),
        name="rope_tables",
    )(pos_b, inv_lane)


def _is_rope_tile(tile):
    return (tile < COL_AV) | ((tile >= COL_CQ) & (tile < COL_CV))


def _in_proj_kernel(x_ref, g_ref, w_ref, b_ref, c_ref, s1_ref, s2_ref, z_ref, h_ref,
                    acc0_ref, acc1_ref):
    j = pl.program_id(1)

    @pl.when(j == 0)
    def _():
        x = x_ref[0]
        ms = jnp.mean(x * x, -1, keepdims=True)
        h_ref[...] = (x * lax.rsqrt(ms + EPS) * g_ref[...]).astype(BF16)

    tile = j - 1
    is_q = (tile < COL_AK) | ((tile >= COL_CQ) & (tile < COL_CK))
    is_rope = _is_rope_tile(tile)

    def matmul_into(acc_ref):
        acc_ref[...] = jnp.dot(h_ref[...], w_ref[...], preferred_element_type=F32) + b_ref[...]

    def finish(acc_ref, rope):
        if not rope:
            z_ref[0] = acc_ref[...].astype(BF16)
            return
        scale = jnp.where(is_q, HEAD_DIM ** -0.5, 1.0).astype(F32)
        for t in range(IN_TN // LANES):
            a = acc_ref[:, t * LANES:(t + 1) * LANES]
            r = (a * c_ref[...] + pltpu.roll(a, LANES - ROT_DIM // 2, 1) * s1_ref[...]
                 + pltpu.roll(a, ROT_DIM // 2, 1) * s2_ref[...])
            z_ref[0, :, t * LANES:(t + 1) * LANES] = (r * scale).astype(BF16)

    accs = (acc0_ref, acc1_ref)
    inner = (j > 0) & (j < N_COL_TILES)
    for parity in range(2):
        for rope in (False, True):
            @pl.when(inner & (j % 2 == parity) & (is_rope == rope))
            def _(parity=parity, rope=rope):
                matmul_into(accs[parity])
                finish(accs[1 - parity], rope)

    @pl.when(j == 0)
    def _():
        matmul_into(accs[0])

    @pl.when(j == N_COL_TILES)
    def _():
        finish(accs[(N_COL_TILES - 1) % 2], bool(_is_rope_tile(N_COL_TILES - 1)))


def _in_proj(x, g, w, b, tabs):
    bsz, s, d = x.shape
    c, s1, s2 = tabs
    tab_spec = pl.BlockSpec((s, LANES), lambda bi, j: (bi, 0))
    last = N_COL_TILES - 1
    return pl.pallas_call(
        _in_proj_kernel,
        out_shape=jax.ShapeDtypeStruct((bsz, s, D_IN), BF16),
        grid=(bsz, N_COL_TILES + 1),
        in_specs=[
            pl.BlockSpec((1, s, d), lambda bi, j: (bi, 0, 0)),
            pl.BlockSpec((1, d), lambda bi, j: (0, 0)),
            pl.BlockSpec((d, IN_TN), lambda bi, j: (0, jnp.minimum(j, last))),
            pl.BlockSpec((1, IN_TN), lambda bi, j: (0, jnp.minimum(j, last))),
            tab_spec, tab_spec, tab_spec,
        ],
        out_specs=pl.BlockSpec((1, s, IN_TN), lambda bi, j: (bi, 0, jnp.maximum(j - 1, 0))),
        scratch_shapes=[pltpu.VMEM((s, d), BF16), pltpu.VMEM((s, IN_TN), F32),
                        pltpu.VMEM((s, IN_TN), F32)],
        compiler_params=_cparams(("parallel", "arbitrary")),
        name="in_proj",
    )(x, g.reshape(1, d), w, b.reshape(1, D_IN), c, s1, s2)


STACK = C_HEADS_PER_GROUP * BLOCK
STEP_UNROLL = 4


def _head_stack(blk):
    seg = lax.broadcasted_iota(jnp.int32, (1, MXU_W), 1) // HEAD_DIM
    rowseg = lax.broadcasted_iota(jnp.int32, (STACK, 1), 0) // BLOCK
    return jnp.where(seg == rowseg, jnp.concatenate([blk] * C_HEADS_PER_GROUP, axis=0),
                     jnp.zeros((), blk.dtype))


def _band_step(q, kp_ref, vp_ref, blk, n, has_prev, diag_key, sink_col, want_lse):
    return _band_blocks([(q, blk, n)], kp_ref, vp_ref, has_prev, diag_key, sink_col, want_lse)[0]


def _band_blocks(jobs, kp_ref, vp_ref, has_prev, diag_key, sink_col, want_lse):
    contract = (((1,), (1,)), ((), ()))
    i_loc = lax.broadcasted_iota(jnp.int32, (STACK, 1), 0) & (BLOCK - 1)
    jj = lax.broadcasted_iota(jnp.int32, (1, BLOCK), 1)
    upper = jj > i_loc
    seg = lax.broadcasted_iota(jnp.int32, (1, MXU_W), 1) // HEAD_DIM
    zero_b = jnp.zeros((), BF16)

    def key_rows(blk):
        lo = pl.multiple_of(blk * BLOCK, BLOCK)
        if has_prev:
            return pl.ds(lo, 2 * BLOCK)
        return pl.ds(pl.multiple_of(lo + BLOCK, BLOCK), BLOCK)

    scores = [lax.dot_general(_head_stack(q), kp_ref[key_rows(blk), :], contract,
                              preferred_element_type=F32) for q, blk, _ in jobs]

    def own_lanes(x):
        out = x[0:BLOCK]
        for h in range(1, C_HEADS_PER_GROUP):
            out = jnp.where(seg == h, x[h * BLOCK:(h + 1) * BLOCK], out)
        return out

    probs, stats = [], []
    for s, (_, blk, n) in zip(scores, jobs):
        if has_prev:
            sp = s[:, :BLOCK] + jnp.where(n == 0, NEG_INF, 0.0).astype(F32)
            f = jnp.where(upper, sp, s[:, BLOCK:])
        else:
            f = jnp.where(upper, NEG_INF, s)
        m = jnp.max(f, -1, keepdims=True)
        if diag_key:
            sd = jnp.sum(jnp.where(jj == i_loc, sp, 0.0), -1, keepdims=True)
            m = jnp.maximum(m, sd)
        if sink_col is not None:
            m = jnp.maximum(m, sink_col)
        p = jnp.exp(f - m)
        den = jnp.sum(p, -1, keepdims=True)
        pd = None
        if diag_key:
            pd = jnp.exp(sd - m)
            den = den + pd
        if sink_col is not None:
            den = den + jnp.exp(sink_col - m)
        inv = 1.0 / den
        pn = (p * inv).astype(BF16)
        if has_prev:
            below = zero_b
            if diag_key:
                below = jnp.where(jj == i_loc, pd * inv, 0.0).astype(BF16)
            pn = jnp.concatenate([jnp.where(upper, pn, below), jnp.where(upper, zero_b, pn)],
                                 axis=1)
        probs.append(pn)
        stats.append((m, den))

    outs = []
    for pn, (m, den), (_, blk, _) in zip(probs, stats, jobs):
        o = own_lanes(jnp.dot(pn, vp_ref[key_rows(blk), :], preferred_element_type=F32))
        lse = own_lanes(m + jnp.log(den)) + jnp.zeros((BLOCK, MXU_W), F32) if want_lse else None
        outs.append((o, lse))
    return outs


def _replicate_head(blk, g):
    words = pltpu.bitcast(blk, jnp.uint32)
    tile = words[:, (g // 2) * LANES:(g // 2 + 1) * LANES]
    swapped = pltpu.roll(tile, HEAD_DIM, 1)
    low = lax.broadcasted_iota(jnp.int32, (1, LANES), 1) < HEAD_DIM
    both = jnp.where(low, tile, swapped) if g % 2 == 0 else jnp.where(low, swapped, tile)
    return pltpu.bitcast(jnp.concatenate([both, both], axis=1), BF16)


def _attn_a_kernel(sink_ref, q_ref, k_ref, v_ref, o_ref, kp_ref, vp_ref):
    nb = q_ref.shape[1] // BLOCK
    rowseg = lax.broadcasted_iota(jnp.int32, (STACK, 1), 0) // BLOCK
    kp_ref[pl.ds(0, BLOCK), :] = jnp.zeros((BLOCK, MXU_W), BF16)
    vp_ref[pl.ds(0, BLOCK), :] = jnp.zeros((BLOCK, MXU_W), BF16)
    for g in range(A_KV_HEADS):
        kv_cols = slice(g * HEAD_DIM, (g + 1) * HEAD_DIM)
        q_cols = slice(g * MXU_W, (g + 1) * MXU_W)

        def build(n, carry):
            rows = pl.ds(pl.multiple_of(n * BLOCK, BLOCK), BLOCK)
            dst = pl.ds(pl.multiple_of((n + 1) * BLOCK, BLOCK), BLOCK)
            kp_ref[dst, :] = _replicate_head(k_ref[0, rows, :], g)
            vp_ref[dst, :] = _replicate_head(v_ref[0, rows, :], g)
            return carry

        lax.fori_loop(0, nb, build, 0, unroll=8)
        sink_col = jnp.zeros((STACK, 1), F32)
        for r in range(A_REP):
            sink_col = jnp.where(rowseg == r, sink_ref[g * A_REP + r], sink_col)

        def step(it, carry):
            blocks = [it * STEP_UNROLL + u for u in range(STEP_UNROLL)]
            rows = [pl.ds(pl.multiple_of(n * BLOCK, BLOCK), BLOCK) for n in blocks]
            jobs = [(q_ref[0, r, q_cols], n, n) for r, n in zip(rows, blocks)]
            outs = _band_blocks(jobs, kp_ref, vp_ref, True, False, sink_col, False)
            for r, (o, _) in zip(rows, outs):
                o_ref[0, r, q_cols] = o.astype(o_ref.dtype)
            return carry

        lax.fori_loop(0, nb // STEP_UNROLL, step, 0)


def _attn_a(z, sinks):
    bsz, s, _ = z.shape
    nb = s // BLOCK
    grid_spec = pltpu.PrefetchScalarGridSpec(
        num_scalar_prefetch=1,
        grid=(bsz,),
        in_specs=[
            pl.BlockSpec((1, s, A_Q_W), lambda bi, sk: (bi, 0, 0)),
            pl.BlockSpec((1, s, A_KV_W), lambda bi, sk: (bi, 0, COL_AK)),
            pl.BlockSpec((1, s, A_KV_W), lambda bi, sk: (bi, 0, COL_AV)),
        ],
        out_specs=pl.BlockSpec((1, s, A_Q_W), lambda bi, sk: (bi, 0, 0)),
        scratch_shapes=[pltpu.VMEM(((nb + 1) * BLOCK, MXU_W), BF16),
                        pltpu.VMEM(((nb + 1) * BLOCK, MXU_W), BF16)],
    )
    return pl.pallas_call(
        _attn_a_kernel,
        out_shape=jax.ShapeDtypeStruct((bsz, s, A_Q_W), BF16),
        grid_spec=grid_spec,
        compiler_params=_cparams(("parallel",)),
        name="attn_a",
    )(sinks, z, z, z)


def _attn_c_kernel(q_ref, k_ref, v_ref, oc_ref, stage_ref, pq_ref, pk_ref, pv_ref, og_ref, lg_ref):
    s = q_ref.shape[1]
    nblk = s // BLOCK
    halves = MXU_W // LANES
    pk_ref[pl.ds(0, BLOCK), :] = jnp.zeros((BLOCK, MXU_W), BF16)
    pv_ref[pl.ds(0, BLOCK), :] = jnp.zeros((BLOCK, MXU_W), BF16)

    def class_rows(idx, nb, d):
        r = idx // nb
        n = idx - r * nb
        start = r + n * (BLOCK * d)
        rows = pl.ds(start, BLOCK, stride=d) if d > 1 else pl.ds(pl.multiple_of(start, BLOCK), BLOCK)
        return n, rows

    def stage(src_ref, cols):
        x = src_ref[0, :, cols].astype(F32)
        for t in range(halves):
            stage_ref[t] = x[:, t * LANES:(t + 1) * LANES]

    def staged_block(rows):
        return jnp.concatenate([stage_ref[t, rows, :] for t in range(halves)], axis=1).astype(BF16)

    for g, (win, d) in enumerate(C_PAIRS):
        cols = slice(g * MXU_W, (g + 1) * MXU_W)
        nb = nblk // d
        has_prev = nb > 1
        assert win // d == BLOCK

        def class_major(dst_ref, offset):
            def build(idx, carry):
                _, rows = class_rows(idx, nb, d)
                dst = pl.ds(pl.multiple_of((idx + offset) * BLOCK, BLOCK), BLOCK)
                dst_ref[dst, :] = staged_block(rows)
                return carry
            return build

        for src_ref, dst_ref, offset in ((k_ref, pk_ref, 1), (v_ref, pv_ref, 1), (q_ref, pq_ref, 0)):
            stage(src_ref, cols)
            lax.fori_loop(0, nblk, class_major(dst_ref, offset), 0, unroll=4)

        def step(it, carry):
            jobs, dsts = [], []
            for u in range(STEP_UNROLL):
                idx = it * STEP_UNROLL + u
                n, rows = class_rows(idx, nb, d)
                here = pl.ds(pl.multiple_of(idx * BLOCK, BLOCK), BLOCK)
                jobs.append((pq_ref[here, :], idx, n))
                dsts.append(rows)
            outs = _band_blocks(jobs, pk_ref, pv_ref, has_prev, has_prev, None, True)
            for rows, (o, lse) in zip(dsts, outs):
                for t in range(halves):
                    og_ref[g, t, rows, :] = o[:, t * LANES:(t + 1) * LANES]
                    lg_ref[g, t, rows, :] = lse[:, t * LANES:(t + 1) * LANES]
            return carry

        lax.fori_loop(0, nblk // STEP_UNROLL, step, 0)

    chunk = 2 * BLOCK

    def combine(i, carry):
        rows = pl.ds(pl.multiple_of(i * chunk, chunk), chunk)
        for t in range(halves):
            ls = [lg_ref[g, t, rows, :] for g in range(len(C_PAIRS))]
            m = jnp.maximum(jnp.maximum(ls[0], ls[1]), ls[2])
            es = [jnp.exp(l - m) for l in ls]
            num = (es[0] * og_ref[0, t, rows, :] + es[1] * og_ref[1, t, rows, :]
                   + es[2] * og_ref[2, t, rows, :])
            oc_ref[0, rows, t * LANES:(t + 1) * LANES] = (
                num / (es[0] + es[1] + es[2])).astype(oc_ref.dtype)
        return carry

    lax.fori_loop(0, s // chunk, combine, 0)


def _attn_c(z):
    bsz, s, _ = z.shape
    nblk = s // BLOCK
    width = C_HEADS * HEAD_DIM
    halves = MXU_W // LANES

    def in_spec(col):
        return pl.BlockSpec((1, s, width), lambda bi: (bi, 0, col * IN_TN // width))

    return pl.pallas_call(
        _attn_c_kernel,
        out_shape=jax.ShapeDtypeStruct((bsz, s, C_OUT_W), BF16),
        grid=(bsz,),
        in_specs=[in_spec(COL_CQ), in_spec(COL_CK), in_spec(COL_CV)],
        out_specs=pl.BlockSpec((1, s, C_OUT_W), lambda bi: (bi, 0, 0)),
        scratch_shapes=[
            pltpu.VMEM((halves, s, LANES), F32),
            pltpu.VMEM((s, MXU_W), BF16),
            pltpu.VMEM((s + BLOCK, MXU_W), BF16),
            pltpu.VMEM((s + BLOCK, MXU_W), BF16),
            pltpu.VMEM((len(C_PAIRS), halves, s, LANES), F32),
            pltpu.VMEM((len(C_PAIRS), halves, s, LANES), F32),
        ],
        compiler_params=_cparams(("parallel",)),
        name="attn_c",
    )(z, z, z)


def _sgu_kernel(zu_ref, zv_ref, lng_ref, lnb_ref, ws_ref, bs_ref, o_ref):
    tm = zu_ref.shape[1]
    u = jax.nn.gelu(zu_ref[0].astype(F32))
    v = jax.nn.gelu(zv_ref[0].astype(F32))
    mu = jnp.mean(v, -1, keepdims=True)
    var = jnp.mean(jnp.square(v - mu), -1, keepdims=True)
    vn = ((v - mu) * lax.rsqrt(var + EPS) * lng_ref[...] + lnb_ref[...]).astype(BF16)
    causal = (lax.broadcasted_iota(jnp.int32, (B_CHUNK, 1), 0)
              >= lax.broadcasted_iota(jnp.int32, (1, B_CHUNK), 1))
    first = lax.broadcasted_iota(jnp.int32, (1, LANES), 1) < B_CH
    zero = jnp.zeros((), BF16)
    for p in range(B_WIDTH // LANES):
        w0 = jnp.where(causal, ws_ref[2 * p], zero)
        w1 = jnp.where(causal, ws_ref[2 * p + 1], zero)
        cols = slice(p * LANES, (p + 1) * LANES)
        for c in range(tm // B_CHUNK):
            rows = slice(c * B_CHUNK, (c + 1) * B_CHUNK)
            vv = vn[rows, cols]
            sv = jnp.where(first,
                           jnp.dot(w0, vv, preferred_element_type=F32),
                           jnp.dot(w1, vv, preferred_element_type=F32)) + bs_ref[:, cols]
            o_ref[0, rows, cols] = (u[rows, cols] * sv).astype(o_ref.dtype)


def _sgu(z, ln_g, ln_b, w_s, b_s):
    bsz, s, _ = z.shape
    tm = 512
    bias = jnp.repeat(b_s.T, B_CH, axis=1)
    col_u = COL_BU * IN_TN // B_WIDTH
    col_v = COL_BV * IN_TN // B_WIDTH
    return pl.pallas_call(
        _sgu_kernel,
        out_shape=jax.ShapeDtypeStruct((bsz, s, B_WIDTH), BF16),
        grid=(bsz, s // tm),
        in_specs=[
            pl.BlockSpec((1, tm, B_WIDTH), lambda bi, i: (bi, i, col_u)),
            pl.BlockSpec((1, tm, B_WIDTH), lambda bi, i: (bi, i, col_v)),
            pl.BlockSpec((1, B_WIDTH), lambda bi, i: (0, 0)),
            pl.BlockSpec((1, B_WIDTH), lambda bi, i: (0, 0)),
            pl.BlockSpec((B_GROUPS, B_CHUNK, B_CHUNK), lambda bi, i: (0, 0, 0)),
            pl.BlockSpec((B_CHUNK, B_WIDTH), lambda bi, i: (0, 0)),
        ],
        out_specs=pl.BlockSpec((1, tm, B_WIDTH), lambda bi, i: (bi, i, 0)),
        compiler_params=_cparams(("parallel", "parallel")),
        name="sgu",
    )(z, z, ln_g.reshape(1, B_WIDTH), ln_b.reshape(1, B_WIDTH), w_s.astype(BF16), bias)


def _route(logits):
    lane = lax.broadcasted_iota(jnp.int32, (1, ROUTE_W), 1)
    lane_f = lane.astype(F32)
    is_g = lane < N_EXPERT_GROUPS
    lg = jnp.where(is_g, logits, NEG_INF)
    mg = jnp.max(lg, -1, keepdims=True)
    pg_top = 1.0 / jnp.sum(jnp.where(is_g, jnp.exp(lg - mg), 0.0), -1, keepdims=True)
    g_idx = jnp.min(jnp.where(lg == mg, lane_f, float(ROUTE_W)), -1, keepdims=True)
    e_group = ((lane - ROUTE_E0) // EXPERTS_PER_GROUP).astype(F32)
    in_grp = (lane >= ROUTE_E0) & (lane < ROUTE_E0 + N_EXPERTS) & (e_group == g_idx)
    le = jnp.where(in_grp, logits, NEG_INF)
    m1 = jnp.max(le, -1, keepdims=True)
    i1 = jnp.min(jnp.where(le == m1, lane_f, float(ROUTE_W)), -1, keepdims=True)
    le2 = jnp.where(lane_f == i1, NEG_INF, le)
    m2 = jnp.max(le2, -1, keepdims=True)
    i2 = jnp.min(jnp.where(le2 == m2, lane_f, float(ROUTE_W)), -1, keepdims=True)
    t = jnp.exp(m2 - m1)
    w1 = pg_top / (1.0 + t)
    w2 = w1 * t
    picks = (i1 - ROUTE_E0, i2 - ROUTE_E0, w1, w2)
    out = jnp.zeros(logits.shape, F32)
    for k, val in enumerate(picks):
        out = jnp.where(lane == k, val, out)
    return out


def _merge_kernel(x_ref, oa_ref, ob_ref, oc_ref,
                  g0_ref, g1_ref, g2_ref, g3_ref, wa_ref, wb_ref, wc_ref, wo_ref, n2_ref, wr_ref,
                  xo_ref, h2_ref, route_ref):
    zg = jnp.concatenate([g0_ref[...], g1_ref[...], g2_ref[...], g3_ref[...]], axis=1).astype(F32)
    gates = jax.nn.sigmoid(zg)
    pa = jnp.dot(oa_ref[...], wa_ref[...], preferred_element_type=F32)
    pb = jnp.dot(ob_ref[...], wb_ref[...], preferred_element_type=F32)
    pc = jnp.dot(oc_ref[...], wc_ref[...], preferred_element_type=F32)
    merged = (gates[:, :D_MODEL] * pa + gates[:, D_MODEL:2 * D_MODEL] * pb
              + gates[:, 2 * D_MODEL:] * pc)
    xn = x_ref[...] + jnp.dot(merged.astype(BF16), wo_ref[...], preferred_element_type=F32)
    xo_ref[...] = xn
    ms = jnp.mean(xn * xn, -1, keepdims=True)
    h2 = xn * lax.rsqrt(ms + EPS) * n2_ref[...]
    h2_ref[...] = _pack_bf16_pairs(h2)
    route_ref[...] = _route(jnp.dot(h2.astype(BF16), wr_ref[...], preferred_element_type=F32))


def _merge(x2, oa, ob, oc, z2, wa, wb, wc, wo, n2g, w_rg, w_re):
    n, d = x2.shape
    tm = 512
    wr = jnp.concatenate(
        [w_rg, w_re, jnp.zeros((d, ROUTE_W - N_EXPERT_GROUPS - N_EXPERTS), w_rg.dtype)], axis=1)
    gate_w = GATE_W // 4
    gate_col0 = COL_G * IN_TN // gate_w

    def rows(width):
        return pl.BlockSpec((tm, width), lambda i: (i, 0))

    def full(shape):
        return pl.BlockSpec(shape, lambda i: (0, 0))

    gate_specs = [pl.BlockSpec((tm, gate_w), lambda i, k=k: (i, gate_col0 + k)) for k in range(4)]
    return pl.pallas_call(
        _merge_kernel,
        out_shape=[jax.ShapeDtypeStruct((n, d), F32),
                   jax.ShapeDtypeStruct((n, d // 2), jnp.int32),
                   jax.ShapeDtypeStruct((n, ROUTE_W), F32)],
        grid=(n // tm,),
        in_specs=[rows(d), rows(A_Q_W), rows(B_WIDTH), rows(C_OUT_W),
                  *gate_specs,
                  full((A_Q_W, d)), full((B_WIDTH, d)), full((C_OUT_W, d)), full((d, d)),
                  full((1, d)), full((d, ROUTE_W))],
        out_specs=[rows(d), rows(d // 2), rows(ROUTE_W)],
        compiler_params=_cparams(("parallel",)),
        name="merge",
    )(x2, oa, ob, oc, z2, z2, z2, z2,
      wa.astype(BF16), wb.astype(BF16), wc.astype(BF16), wo.astype(BF16),
      n2g.reshape(1, d), wr.astype(BF16))


MOE_TM = 1024
SC_CORES = 2
SC_SUBCORES = 16
SC_WORKERS = SC_CORES * SC_SUBCORES
SC_CHUNK = 128


def _pack_bf16_pairs(x):
    w = x.shape[1] // 2
    lo = pltpu.bitcast(x[:, :w].astype(BF16).astype(F32), jnp.int32)
    hi = pltpu.bitcast(x[:, w:].astype(BF16).astype(F32), jnp.int32)
    return (hi & jnp.int32(-65536)) | lax.shift_right_logical(lo, jnp.int32(16))


def _unpack_bf16_pairs(p):
    lo = pltpu.bitcast(lax.shift_left(p, jnp.int32(16)), F32)
    hi = pltpu.bitcast(p & jnp.int32(-65536), F32)
    return jnp.concatenate([lo, hi], axis=1)


def _sc_gather_rows(table, idx):
    rows, width = idx.shape[0], table.shape[1]
    per_worker = rows // SC_WORKERS
    n_chunks = per_worker // SC_CHUNK
    assert per_worker * SC_WORKERS == rows and n_chunks * SC_CHUNK == per_worker
    mesh = plsc.VectorSubcoreMesh(core_axis_name="c", subcore_axis_name="s",
                                  num_cores=SC_CORES, num_subcores=SC_SUBCORES)

    @functools.partial(
        pl.kernel, mesh=mesh,
        out_type=jax.ShapeDtypeStruct((rows, width), table.dtype),
        scratch_types=[pltpu.VMEM((n_chunks, SC_CHUNK), jnp.int32),
                       pltpu.VMEM((SC_CHUNK, width), table.dtype),
                       pltpu.SemaphoreType.DMA],
        name="sc_gather_rows",
    )
    def gather(table_hbm, idx_hbm, out_hbm, idx_v, rows_v, sem):
        wid = lax.axis_index("s") * SC_CORES + lax.axis_index("c")
        pltpu.sync_copy(idx_hbm.at[wid], idx_v)
        base = wid * per_worker

        @pl.loop(0, n_chunks)
        def _(c):
            pltpu.async_copy(table_hbm.at[idx_v.at[c]], rows_v, sem).wait()
            pltpu.sync_copy(rows_v, out_hbm.at[pl.ds(base + c * SC_CHUNK, SC_CHUNK)])

    return gather(table, idx.reshape(SC_WORKERS, n_chunks, SC_CHUNK))


def _sc_scatter_rows(rows, pos, n_slots):
    n, width = rows.shape
    per_worker = n // SC_WORKERS
    n_chunks = per_worker // SC_CHUNK
    assert per_worker * SC_WORKERS == n and n_chunks * SC_CHUNK == per_worker
    mesh = plsc.VectorSubcoreMesh(core_axis_name="c", subcore_axis_name="s",
                                  num_cores=SC_CORES, num_subcores=SC_SUBCORES)

    @functools.partial(
        pl.kernel, mesh=mesh,
        out_type=jax.ShapeDtypeStruct((n_slots, width), rows.dtype),
        scratch_types=[pltpu.VMEM((TOP_K, n_chunks, SC_CHUNK), jnp.int32),
                       pltpu.VMEM((SC_CHUNK, width), rows.dtype),
                       pltpu.SemaphoreType.DMA],
        name="sc_scatter_rows",
    )
    def scatter(rows_hbm, pos_hbm, out_hbm, pos_v, rows_v, sem):
        wid = lax.axis_index("s") * SC_CORES + lax.axis_index("c")
        for k in range(TOP_K):
            pltpu.sync_copy(pos_hbm.at[k, wid], pos_v.at[k])
        base = wid * per_worker

        @pl.loop(0, n_chunks)
        def _(c):
            pltpu.sync_copy(rows_hbm.at[pl.ds(base + c * SC_CHUNK, SC_CHUNK)], rows_v)
            for k in range(TOP_K):
                pltpu.async_copy(rows_v, out_hbm.at[pos_v.at[k, c]], sem).wait()

    return scatter(rows, pos.reshape(TOP_K, SC_WORKERS, n_chunks, SC_CHUNK))


def _dispatch_plan(route, n_slots, expert_base):
    n = route.shape[0]
    e = route[:, 0:TOP_K].astype(jnp.int32).reshape(-1)
    experts = jnp.arange(N_EXPERTS, dtype=jnp.int32)
    onehot = (e[:, None] == experts[None, :]).astype(jnp.int32)
    csum = jnp.cumsum(onehot, axis=0)
    rank = jnp.sum((csum - onehot) * onehot, axis=1)
    counts = csum[-1]
    padded = (counts + MOE_TM - 1) // MOE_TM * MOE_TM
    ends = jnp.cumsum(padded)
    starts = ends - padded
    pos = starts[e] + rank
    tile_start = jnp.arange(n_slots // MOE_TM, dtype=jnp.int32) * MOE_TM
    tile_e = jnp.minimum(jnp.sum((tile_start[:, None] >= ends[None, :]).astype(jnp.int32), axis=1),
                         N_EXPERTS - 1)
    tile_rows = jnp.clip(counts[tile_e] - (tile_start - starts[tile_e]), 0, MOE_TM)
    tile_rows = jnp.where(tile_start < ends[-1], tile_rows, 0)
    first = jnp.concatenate([jnp.ones((1,), jnp.int32),
                             (tile_e[1:] != tile_e[:-1]).astype(jnp.int32)])
    return pos.reshape(n, TOP_K).T, tile_e + expert_base, tile_rows.astype(jnp.int32), first


def _moe_tile_kernel(te_ref, tr_ref, tf_ref, xs_ref, wg_ref, wu_ref, wd_ref, ys_ref,
                     wg_s, wu_s, wd_s):
    t = pl.program_id(0)

    @pl.when(tf_ref[t] != 0)
    def _():
        wg_s[...] = wg_ref[0].astype(BF16)
        wu_s[...] = wu_ref[0].astype(BF16)
        wd_s[...] = wd_ref[0].astype(BF16)

    @pl.when(tr_ref[t] != 0)
    def _():
        occupied = lax.broadcasted_iota(jnp.int32, (MOE_TM, 1), 0) < tr_ref[t]
        x = jnp.where(occupied, _unpack_bf16_pairs(xs_ref[...]), 0.0).astype(BF16)
        hg = jnp.dot(x, wg_s[...], preferred_element_type=F32)
        hu = jnp.dot(x, wu_s[...], preferred_element_type=F32)
        a = (jax.nn.silu(hg) * hu).astype(BF16)
        ys_ref[...] = _pack_bf16_pairs(jnp.dot(a, wd_s[...], preferred_element_type=F32))

    @pl.when(tr_ref[t] == 0)
    def _():
        ys_ref[...] = jnp.zeros_like(ys_ref)


def _moe_tiles(xs, tile_expert, tile_rows, tile_first, w_gate, w_up, w_down):
    n_slots, half = xs.shape
    d = 2 * half
    grid_spec = pltpu.PrefetchScalarGridSpec(
        num_scalar_prefetch=3,
        grid=(n_slots // MOE_TM,),
        in_specs=[
            pl.BlockSpec((MOE_TM, half), lambda t, te, tr, tf: (t, 0)),
            pl.BlockSpec((1, d, D_EXPERT), lambda t, te, tr, tf: (te[t], 0, 0)),
            pl.BlockSpec((1, d, D_EXPERT), lambda t, te, tr, tf: (te[t], 0, 0)),
            pl.BlockSpec((1, D_EXPERT, d), lambda t, te, tr, tf: (te[t], 0, 0)),
        ],
        out_specs=pl.BlockSpec((MOE_TM, half), lambda t, te, tr, tf: (t, 0)),
        scratch_shapes=[pltpu.VMEM((d, D_EXPERT), BF16), pltpu.VMEM((d, D_EXPERT), BF16),
                        pltpu.VMEM((D_EXPERT, d), BF16)],
    )
    return pl.pallas_call(
        _moe_tile_kernel,
        out_shape=jax.ShapeDtypeStruct((n_slots, half), jnp.int32),
        grid_spec=grid_spec,
        compiler_params=_cparams(("arbitrary",)),
        name="moe_tiles",
    )(tile_expert, tile_rows, tile_first, xs, w_gate, w_up, w_down)


def _moe_combine_kernel(x_ref, y0_ref, y1_ref, r_ref, g_ref, o_ref, *, final_norm):
    w0 = r_ref[:, 2:3]
    w1 = r_ref[:, 3:4]
    x = x_ref[...] + w0 * _unpack_bf16_pairs(y0_ref[0]) + w1 * _unpack_bf16_pairs(y1_ref[0])
    if final_norm:
        ms = jnp.mean(x * x, -1, keepdims=True)
        x = x * lax.rsqrt(ms + EPS) * g_ref[...]
    o_ref[...] = x


def _moe_combine(x2, yg, route, norm_g, final_norm):
    n, d = x2.shape
    tm = 1024
    return pl.pallas_call(
        functools.partial(_moe_combine_kernel, final_norm=final_norm),
        out_shape=jax.ShapeDtypeStruct((n, d), F32),
        grid=(n // tm,),
        in_specs=[
            pl.BlockSpec((tm, d), lambda i: (i, 0)),
            pl.BlockSpec((1, tm, d // 2), lambda i: (0, i, 0)),
            pl.BlockSpec((1, tm, d // 2), lambda i: (1, i, 0)),
            pl.BlockSpec((tm, ROUTE_W), lambda i: (i, 0)),
            pl.BlockSpec((1, d), lambda i: (0, 0)),
        ],
        out_specs=pl.BlockSpec((tm, d), lambda i: (i, 0)),
        compiler_params=_cparams(("parallel",)),
        name="moe_combine",
    )(x2, yg, yg, route, norm_g.reshape(1, d))


def _moe(h2p, route, x2, layer, w_gate, w_up, w_down, norm_f_g, final_norm):
    n, d = x2.shape
    n_slots = TOP_K * n + N_EXPERTS * MOE_TM
    pos, tile_expert, tile_rows, tile_first = _dispatch_plan(route, n_slots, layer * N_EXPERTS)
    xs = _sc_scatter_rows(h2p, pos, n_slots)
    ys = _moe_tiles(xs, tile_expert, tile_rows, tile_first, w_gate, w_up, w_down)
    yg = _sc_gather_rows(ys, pos.reshape(-1)).reshape(TOP_K, n, d // 2)
    return _moe_combine(x2, yg, route, norm_f_g, final_norm)


def kernel(x, positions, norm1_g, w_in, b_in, attn_sinks, sgu_ln_g, sgu_ln_b, w_spatial, b_spatial,
           w_proj_a, w_proj_b, w_proj_c, w_out, norm2_g, w_router_group, w_router_expert,
           w_expert_gate, w_expert_up, w_expert_down, norm_f_g):
    bsz, s, d = x.shape
    depth = w_in.shape[0]
    wg_all = w_expert_gate.reshape(depth * N_EXPERTS, d, D_EXPERT)
    wu_all = w_expert_up.reshape(depth * N_EXPERTS, d, D_EXPERT)
    wd_all = w_expert_down.reshape(depth * N_EXPERTS, D_EXPERT, d)
    assert d == D_MODEL and s % (C_PAIRS[-1][1] * BLOCK) == 0
    tabs = _rope_tables(positions)
    for l in range(depth):
        z = _in_proj(x, norm1_g[l], w_in[l].astype(BF16), b_in[l], tabs)
        oa = _attn_a(z, attn_sinks[l])
        ob = _sgu(z, sgu_ln_g[l], sgu_ln_b[l], w_spatial[l], b_spatial[l])
        oc = _attn_c(z)
        x2, h2, route = _merge(
            x.reshape(bsz * s, d), oa.reshape(bsz * s, A_Q_W), ob.reshape(bsz * s, B_WIDTH),
            oc.reshape(bsz * s, C_OUT_W), z.reshape(bsz * s, D_IN), w_proj_a[l], w_proj_b[l], w_proj_c[l], w_out[l],
            norm2_g[l], w_router_group[l], w_router_expert[l])
        x = _moe(h2, route, x2, l, wg_all, wu_all, wd_all, norm_f_g, l == depth - 1).reshape(bsz, s, d)
    return x
```

```python
import functools

import jax
import jax.numpy as jnp
from jax import lax
from jax.experimental import pallas as pl
from jax.experimental.pallas import tpu as pltpu
from jax.experimental.pallas import tpu_sc as plsc

F32 = jnp.float32
BF16 = jnp.bfloat16

D_MODEL = 1024
HEAD_DIM = 64
ROT_DIM = HEAD_DIM // 4
ROPE_THETA = 500000.0
BLOCK = 128
EPS = 1e-5
NEG_INF = -1e30

A_Q_HEADS = 16
A_KV_HEADS = 4
A_REP = A_Q_HEADS // A_KV_HEADS
A_WINDOW = 128
A_Q_W = A_Q_HEADS * HEAD_DIM
A_KV_W = A_KV_HEADS * HEAD_DIM

B_GROUPS = 12
B_CH = 64
B_WIDTH = B_GROUPS * B_CH
B_CHUNK = 128

C_PAIRS = ((128, 1), (512, 4), (2048, 16))
C_HEADS_PER_GROUP = 4
C_HEADS = C_HEADS_PER_GROUP * len(C_PAIRS)
C_OUT_W = C_HEADS_PER_GROUP * HEAD_DIM

N_BRANCH = 3
GATE_W = N_BRANCH * D_MODEL
D_IN = A_Q_W + 2 * A_KV_W + 2 * B_WIDTH + 3 * C_HEADS * HEAD_DIM + GATE_W

N_EXPERT_GROUPS = 4
EXPERTS_PER_GROUP = 8
N_EXPERTS = N_EXPERT_GROUPS * EXPERTS_PER_GROUP
TOP_K = 2
D_EXPERT = 256

LANES = 128
MXU_W = 256
VMEM_LIMIT = 56 * 1024 * 1024

IN_TN = 256
COL_AQ = 0
COL_AK = A_Q_W // IN_TN
COL_AV = COL_AK + A_KV_W // IN_TN
COL_BU = COL_AV + A_KV_W // IN_TN
COL_BV = COL_BU + B_WIDTH // IN_TN
COL_CQ = COL_BV + B_WIDTH // IN_TN
COL_CK = COL_CQ + C_HEADS * HEAD_DIM // IN_TN
COL_CV = COL_CK + C_HEADS * HEAD_DIM // IN_TN
COL_G = COL_CV + C_HEADS * HEAD_DIM // IN_TN
N_COL_TILES = D_IN // IN_TN

ROUTE_W = LANES
ROUTE_E0 = N_EXPERT_GROUPS


def _cparams(sem):
    return pltpu.CompilerParams(dimension_semantics=sem, vmem_limit_bytes=VMEM_LIMIT)


def _rope_table_kernel(pos_ref, inv_ref, c_ref, s1_ref, s2_ref):
    lane = lax.broadcasted_iota(jnp.int32, (1, LANES), 1)
    d = lane & (HEAD_DIM - 1)
    ang = pos_ref[...].astype(F32) * inv_ref[...]
    c = jnp.cos(ang)
    s = jnp.sin(ang)
    half = ROT_DIM // 2
    c_ref[...] = jnp.where(d < ROT_DIM, c, 1.0)
    s1_ref[...] = jnp.where(d < half, -s, 0.0)
    s2_ref[...] = jnp.where((d >= half) & (d < ROT_DIM), s, 0.0)


def _rope_tables(positions):
    n = positions.size
    inv = ROPE_THETA ** (-jnp.arange(0, ROT_DIM, 2, dtype=F32) / ROT_DIM)
    inv_lane = jnp.tile(inv, LANES // inv.shape[0]).reshape(1, LANES)
    pos_b = jnp.broadcast_to(positions.reshape(n, 1), (n, LANES))
    tm = 1024
    spec = pl.BlockSpec((tm, LANES), lambda i: (i, 0))
    return pl.pallas_call(
        _rope_table_kernel,
        out_shape=[jax.ShapeDtypeStruct((n, LANES), F32)] * 3,
        grid=(n // tm,),
        in_specs=[spec, pl.BlockSpec((1, LANES), lambda i: (0, 0))],
        out_specs=[spec, spec, spec],
        compiler_params=_cparams(("parallel",)),
        name="rope_tables",
    )(pos_b, inv_lane)


IN_STEP_TILES = 3
IN_STEP_W = IN_STEP_TILES * IN_TN
IN_STEPS = N_COL_TILES // IN_STEP_TILES


def _is_rope_tile(tile):
    return tile < COL_AV or COL_CQ <= tile < COL_CV


_STEP_ROPES = tuple(tuple(_is_rope_tile(step * IN_STEP_TILES + u) for u in range(IN_STEP_TILES))
                    for step in range(IN_STEPS))


def _in_proj_kernel(x_ref, g_ref, w_ref, b_ref, c_ref, s1_ref, s2_ref, z_ref, h_ref,
                    acc0_ref, acc1_ref):
    j = pl.program_id(1)

    @pl.when(j == 0)
    def _():
        x = x_ref[0]
        ms = jnp.mean(x * x, -1, keepdims=True)
        h_ref[...] = (x * lax.rsqrt(ms + EPS) * g_ref[...]).astype(BF16)

    prev = j - 1

    def matmul_into(acc_ref):
        acc_ref[...] = jnp.dot(h_ref[...], w_ref[...], preferred_element_type=F32) + b_ref[...]

    def finish(acc_ref, ropes):
        for u, rope in enumerate(ropes):
            cols = slice(u * IN_TN, (u + 1) * IN_TN)
            if not rope:
                z_ref[0, :, cols] = acc_ref[:, cols].astype(BF16)
                continue
            tile = prev * IN_STEP_TILES + u
            is_q = (tile < COL_AK) | ((tile >= COL_CQ) & (tile < COL_CK))
            scale = jnp.where(is_q, HEAD_DIM ** -0.5, 1.0).astype(F32)
            for t in range(u * IN_TN // LANES, (u + 1) * IN_TN // LANES):
                lanes = slice(t * LANES, (t + 1) * LANES)
                a = acc_ref[:, lanes]
                r = (a * c_ref[...] + pltpu.roll(a, LANES - ROT_DIM // 2, 1) * s1_ref[...]
                     + pltpu.roll(a, ROT_DIM // 2, 1) * s2_ref[...])
                z_ref[0, :, lanes] = (r * scale).astype(BF16)

    accs = (acc0_ref, acc1_ref)
    patterns = sorted(set(_STEP_ROPES))
    pattern_id = jnp.int32(0)
    for step, ropes in enumerate(_STEP_ROPES):
        pattern_id = jnp.where(prev == step, patterns.index(ropes), pattern_id)
    inner = (j > 0) & (j < IN_STEPS)
    for parity in range(2):
        for pid, ropes in enumerate(patterns):
            @pl.when(inner & (j % 2 == parity) & (pattern_id == pid))
            def _(parity=parity, ropes=ropes):
                matmul_into(accs[parity])
                finish(accs[1 - parity], ropes)

    @pl.when(j == 0)
    def _():
        matmul_into(accs[0])

    @pl.when(j == IN_STEPS)
    def _():
        finish(accs[(IN_STEPS - 1) % 2], _STEP_ROPES[-1])


def _in_proj(x, g, w, b, tabs):
    bsz, s, d = x.shape
    c, s1, s2 = tabs
    tab_spec = pl.BlockSpec((s, LANES), lambda bi, j: (bi, 0))
    last = IN_STEPS - 1
    return pl.pallas_call(
        _in_proj_kernel,
        out_shape=jax.ShapeDtypeStruct((bsz, s, D_IN), BF16),
        grid=(bsz, IN_STEPS + 1),
        in_specs=[
            pl.BlockSpec((1, s, d), lambda bi, j: (bi, 0, 0)),
            pl.BlockSpec((1, d), lambda bi, j: (0, 0)),
            pl.BlockSpec((d, IN_STEP_W), lambda bi, j: (0, jnp.minimum(j, last))),
            pl.BlockSpec((1, IN_STEP_W), lambda bi, j: (0, jnp.minimum(j, last))),
            tab_spec, tab_spec, tab_spec,
        ],
        out_specs=pl.BlockSpec((1, s, IN_STEP_W), lambda bi, j: (bi, 0, jnp.maximum(j - 1, 0))),
        scratch_shapes=[pltpu.VMEM((s, d), BF16), pltpu.VMEM((s, IN_STEP_W), F32),
                        pltpu.VMEM((s, IN_STEP_W), F32)],
        compiler_params=_cparams(("parallel", "arbitrary")),
        name="in_proj",
    )(x, g.reshape(1, d), w, b.reshape(1, D_IN), c, s1, s2)


STACK = C_HEADS_PER_GROUP * BLOCK
STEP_UNROLL = 4


def _head_stack(blk):
    seg = lax.broadcasted_iota(jnp.int32, (1, MXU_W), 1) // HEAD_DIM
    rowseg = lax.broadcasted_iota(jnp.int32, (STACK, 1), 0) // BLOCK
    return jnp.where(seg == rowseg, jnp.concatenate([blk] * C_HEADS_PER_GROUP, axis=0),
                     jnp.zeros((), blk.dtype))


def _band_step(q, kp_ref, vp_ref, blk, n, has_prev, diag_key, sink_col, want_lse):
    return _band_blocks([(q, blk, n)], kp_ref, vp_ref, has_prev, diag_key, sink_col, want_lse)[0]


def _band_blocks(jobs, kp_ref, vp_ref, has_prev, diag_key, sink_col, want_lse):
    contract = (((1,), (1,)), ((), ()))
    i_loc = lax.broadcasted_iota(jnp.int32, (STACK, 1), 0) & (BLOCK - 1)
    jj = lax.broadcasted_iota(jnp.int32, (1, BLOCK), 1)
    upper = jj > i_loc
    seg = lax.broadcasted_iota(jnp.int32, (1, MXU_W), 1) // HEAD_DIM
    zero_b = jnp.zeros((), BF16)

    def key_rows(blk):
        lo = pl.multiple_of(blk * BLOCK, BLOCK)
        if has_prev:
            return pl.ds(lo, 2 * BLOCK)
        return pl.ds(pl.multiple_of(lo + BLOCK, BLOCK), BLOCK)

    scores = [lax.dot_general(_head_stack(q), kp_ref[key_rows(blk), :], contract,
                              preferred_element_type=F32) for q, blk, _ in jobs]

    def own_lanes(x):
        out = x[0:BLOCK]
        for h in range(1, C_HEADS_PER_GROUP):
            out = jnp.where(seg == h, x[h * BLOCK:(h + 1) * BLOCK], out)
        return out

    probs, stats = [], []
    for s, (_, blk, n) in zip(scores, jobs):
        if has_prev:
            sp = s[:, :BLOCK] + jnp.where(n == 0, NEG_INF, 0.0).astype(F32)
            f = jnp.where(upper, sp, s[:, BLOCK:])
        else:
            f = jnp.where(upper, NEG_INF, s)
        m = jnp.max(f, -1, keepdims=True)
        if diag_key:
            sd = jnp.sum(jnp.where(jj == i_loc, sp, 0.0), -1, keepdims=True)
            m = jnp.maximum(m, sd)
        if sink_col is not None:
            m = jnp.maximum(m, sink_col)
        p = jnp.exp(f - m)
        den = jnp.sum(p, -1, keepdims=True)
        pd = None
        if diag_key:
            pd = jnp.exp(sd - m)
            den = den + pd
        if sink_col is not None:
            den = den + jnp.exp(sink_col - m)
        inv = 1.0 / den
        pn = (p * inv).astype(BF16)
        if has_prev:
            below = zero_b
            if diag_key:
                below = jnp.where(jj == i_loc, pd * inv, 0.0).astype(BF16)
            pn = jnp.concatenate([jnp.where(upper, pn, below), jnp.where(upper, zero_b, pn)],
                                 axis=1)
        probs.append(pn)
        stats.append((m, den))

    outs = []
    for pn, (m, den), (_, blk, _) in zip(probs, stats, jobs):
        o = own_lanes(jnp.dot(pn, vp_ref[key_rows(blk), :], preferred_element_type=F32))
        lse = own_lanes(m + jnp.log(den)) + jnp.zeros((BLOCK, MXU_W), F32) if want_lse else None
        outs.append((o, lse))
    return outs


def _replicate_head(blk, g):
    words = pltpu.bitcast(blk, jnp.uint32)
    tile = words[:, (g // 2) * LANES:(g // 2 + 1) * LANES]
    swapped = pltpu.roll(tile, HEAD_DIM, 1)
    low = lax.broadcasted_iota(jnp.int32, (1, LANES), 1) < HEAD_DIM
    both = jnp.where(low, tile, swapped) if g % 2 == 0 else jnp.where(low, swapped, tile)
    return pltpu.bitcast(jnp.concatenate([both, both], axis=1), BF16)


def _attn_a_kernel(sink_ref, q_ref, k_ref, v_ref, o_ref, kp_ref, vp_ref):
    nb = q_ref.shape[1] // BLOCK
    rowseg = lax.broadcasted_iota(jnp.int32, (STACK, 1), 0) // BLOCK
    kp_ref[pl.ds(0, BLOCK), :] = jnp.zeros((BLOCK, MXU_W), BF16)
    vp_ref[pl.ds(0, BLOCK), :] = jnp.zeros((BLOCK, MXU_W), BF16)
    for g in range(A_KV_HEADS):
        kv_cols = slice(g * HEAD_DIM, (g + 1) * HEAD_DIM)
        q_cols = slice(g * MXU_W, (g + 1) * MXU_W)

        def build(n, carry):
            rows = pl.ds(pl.multiple_of(n * BLOCK, BLOCK), BLOCK)
            dst = pl.ds(pl.multiple_of((n + 1) * BLOCK, BLOCK), BLOCK)
            kp_ref[dst, :] = _replicate_head(k_ref[0, rows, :], g)
            vp_ref[dst, :] = _replicate_head(v_ref[0, rows, :], g)
            return carry

        lax.fori_loop(0, nb, build, 0, unroll=8)
        sink_col = jnp.zeros((STACK, 1), F32)
        for r in range(A_REP):
            sink_col = jnp.where(rowseg == r, sink_ref[g * A_REP + r], sink_col)

        def step(it, carry):
            blocks = [it * STEP_UNROLL + u for u in range(STEP_UNROLL)]
            rows = [pl.ds(pl.multiple_of(n * BLOCK, BLOCK), BLOCK) for n in blocks]
            jobs = [(q_ref[0, r, q_cols], n, n) for r, n in zip(rows, blocks)]
            outs = _band_blocks(jobs, kp_ref, vp_ref, True, False, sink_col, False)
            for r, (o, _) in zip(rows, outs):
                o_ref[0, r, q_cols] = o.astype(o_ref.dtype)
            return carry

        lax.fori_loop(0, nb // STEP_UNROLL, step, 0)


def _attn_a(z, sinks):
    bsz, s, _ = z.shape
    nb = s // BLOCK
    grid_spec = pltpu.PrefetchScalarGridSpec(
        num_scalar_prefetch=1,
        grid=(bsz,),
        in_specs=[
            pl.BlockSpec((1, s, A_Q_W), lambda bi, sk: (bi, 0, 0)),
            pl.BlockSpec((1, s, A_KV_W), lambda bi, sk: (bi, 0, COL_AK)),
            pl.BlockSpec((1, s, A_KV_W), lambda bi, sk: (bi, 0, COL_AV)),
        ],
        out_specs=pl.BlockSpec((1, s, A_Q_W), lambda bi, sk: (bi, 0, 0)),
        scratch_shapes=[pltpu.VMEM(((nb + 1) * BLOCK, MXU_W), BF16),
                        pltpu.VMEM(((nb + 1) * BLOCK, MXU_W), BF16)],
    )
    return pl.pallas_call(
        _attn_a_kernel,
        out_shape=jax.ShapeDtypeStruct((bsz, s, A_Q_W), BF16),
        grid_spec=grid_spec,
        compiler_params=_cparams(("parallel",)),
        name="attn_a",
    )(sinks, z, z, z)


def _attn_c_kernel(q_ref, k_ref, v_ref, oc_ref, stage_ref, pq_ref, pk_ref, pv_ref, og_ref, lg_ref):
    s = q_ref.shape[1]
    nblk = s // BLOCK
    halves = MXU_W // LANES
    pk_ref[pl.ds(0, BLOCK), :] = jnp.zeros((BLOCK, MXU_W), BF16)
    pv_ref[pl.ds(0, BLOCK), :] = jnp.zeros((BLOCK, MXU_W), BF16)

    def class_rows(idx, nb, d):
        r = idx // nb
        n = idx - r * nb
        start = r + n * (BLOCK * d)
        rows = pl.ds(start, BLOCK, stride=d) if d > 1 else pl.ds(pl.multiple_of(start, BLOCK), BLOCK)
        return n, rows

    def stage(src_ref, cols):
        x = src_ref[0, :, cols].astype(F32)
        for t in range(halves):
            stage_ref[t] = x[:, t * LANES:(t + 1) * LANES]

    def staged_block(rows):
        return jnp.concatenate([stage_ref[t, rows, :] for t in range(halves)], axis=1).astype(BF16)

    for g, (win, d) in enumerate(C_PAIRS):
        cols = slice(g * MXU_W, (g + 1) * MXU_W)
        nb = nblk // d
        has_prev = nb > 1
        assert win // d == BLOCK

        def class_major(dst_ref, offset):
            def build(idx, carry):
                _, rows = class_rows(idx, nb, d)
                dst = pl.ds(pl.multiple_of((idx + offset) * BLOCK, BLOCK), BLOCK)
                dst_ref[dst, :] = staged_block(rows)
                return carry
            return build

        for src_ref, dst_ref, offset in ((k_ref, pk_ref, 1), (v_ref, pv_ref, 1), (q_ref, pq_ref, 0)):
            stage(src_ref, cols)
            lax.fori_loop(0, nblk, class_major(dst_ref, offset), 0, unroll=4)

        def step(it, carry):
            jobs, dsts = [], []
            for u in range(STEP_UNROLL):
                idx = it * STEP_UNROLL + u
                n, rows = class_rows(idx, nb, d)
                here = pl.ds(pl.multiple_of(idx * BLOCK, BLOCK), BLOCK)
                jobs.append((pq_ref[here, :], idx, n))
                dsts.append(rows)
            outs = _band_blocks(jobs, pk_ref, pv_ref, has_prev, has_prev, None, True)
            for rows, (o, lse) in zip(dsts, outs):
                for t in range(halves):
                    og_ref[g, t, rows, :] = o[:, t * LANES:(t + 1) * LANES]
                    lg_ref[g, t, rows, :] = lse[:, t * LANES:(t + 1) * LANES]
            return carry

        lax.fori_loop(0, nblk // STEP_UNROLL, step, 0)

    chunk = 2 * BLOCK

    def combine(i, carry):
        rows = pl.ds(pl.multiple_of(i * chunk, chunk), chunk)
        for t in range(halves):
            ls = [lg_ref[g, t, rows, :] for g in range(len(C_PAIRS))]
            m = jnp.maximum(jnp.maximum(ls[0], ls[1]), ls[2])
            es = [jnp.exp(l - m) for l in ls]
            num = (es[0] * og_ref[0, t, rows, :] + es[1] * og_ref[1, t, rows, :]
                   + es[2] * og_ref[2, t, rows, :])
            oc_ref[0, rows, t * LANES:(t + 1) * LANES] = (
                num / (es[0] + es[1] + es[2])).astype(oc_ref.dtype)
        return carry

    lax.fori_loop(0, s // chunk, combine, 0)


def _attn_c(z):
    bsz, s, _ = z.shape
    nblk = s // BLOCK
    width = C_HEADS * HEAD_DIM
    halves = MXU_W // LANES

    def in_spec(col):
        return pl.BlockSpec((1, s, width), lambda bi: (bi, 0, col * IN_TN // width))

    return pl.pallas_call(
        _attn_c_kernel,
        out_shape=jax.ShapeDtypeStruct((bsz, s, C_OUT_W), BF16),
        grid=(bsz,),
        in_specs=[in_spec(COL_CQ), in_spec(COL_CK), in_spec(COL_CV)],
        out_specs=pl.BlockSpec((1, s, C_OUT_W), lambda bi: (bi, 0, 0)),
        scratch_shapes=[
            pltpu.VMEM((halves, s, LANES), F32),
            pltpu.VMEM((s, MXU_W), BF16),
            pltpu.VMEM((s + BLOCK, MXU_W), BF16),
            pltpu.VMEM((s + BLOCK, MXU_W), BF16),
            pltpu.VMEM((len(C_PAIRS), halves, s, LANES), F32),
            pltpu.VMEM((len(C_PAIRS), halves, s, LANES), F32),
        ],
        compiler_params=_cparams(("parallel",)),
        name="attn_c",
    )(z, z, z)


def _sgu_kernel(zu_ref, zv_ref, lng_ref, lnb_ref, ws_ref, bs_ref, o_ref):
    tm = zu_ref.shape[1]
    u = jax.nn.gelu(zu_ref[0].astype(F32))
    v = jax.nn.gelu(zv_ref[0].astype(F32))
    mu = jnp.mean(v, -1, keepdims=True)
    var = jnp.mean(jnp.square(v - mu), -1, keepdims=True)
    vn = ((v - mu) * lax.rsqrt(var + EPS) * lng_ref[...] + lnb_ref[...]).astype(BF16)
    causal = (lax.broadcasted_iota(jnp.int32, (B_CHUNK, 1), 0)
              >= lax.broadcasted_iota(jnp.int32, (1, B_CHUNK), 1))
    first = lax.broadcasted_iota(jnp.int32, (1, LANES), 1) < B_CH
    zero = jnp.zeros((), BF16)
    for p in range(B_WIDTH // LANES):
        w0 = jnp.where(causal, ws_ref[2 * p], zero)
        w1 = jnp.where(causal, ws_ref[2 * p + 1], zero)
        cols = slice(p * LANES, (p + 1) * LANES)
        for c in range(tm // B_CHUNK):
            rows = slice(c * B_CHUNK, (c + 1) * B_CHUNK)
            vv = vn[rows, cols]
            sv = jnp.where(first,
                           jnp.dot(w0, vv, preferred_element_type=F32),
                           jnp.dot(w1, vv, preferred_element_type=F32)) + bs_ref[:, cols]
            o_ref[0, rows, cols] = (u[rows, cols] * sv).astype(o_ref.dtype)


def _sgu(z, ln_g, ln_b, w_s, b_s):
    bsz, s, _ = z.shape
    tm = 512
    bias = jnp.repeat(b_s.T, B_CH, axis=1)
    col_u = COL_BU * IN_TN // B_WIDTH
    col_v = COL_BV * IN_TN // B_WIDTH
    return pl.pallas_call(
        _sgu_kernel,
        out_shape=jax.ShapeDtypeStruct((bsz, s, B_WIDTH), BF16),
        grid=(bsz, s // tm),
        in_specs=[
            pl.BlockSpec((1, tm, B_WIDTH), lambda bi, i: (bi, i, col_u)),
            pl.BlockSpec((1, tm, B_WIDTH), lambda bi, i: (bi, i, col_v)),
            pl.BlockSpec((1, B_WIDTH), lambda bi, i: (0, 0)),
            pl.BlockSpec((1, B_WIDTH), lambda bi, i: (0, 0)),
            pl.BlockSpec((B_GROUPS, B_CHUNK, B_CHUNK), lambda bi, i: (0, 0, 0)),
            pl.BlockSpec((B_CHUNK, B_WIDTH), lambda bi, i: (0, 0)),
        ],
        out_specs=pl.BlockSpec((1, tm, B_WIDTH), lambda bi, i: (bi, i, 0)),
        compiler_params=_cparams(("parallel", "parallel")),
        name="sgu",
    )(z, z, ln_g.reshape(1, B_WIDTH), ln_b.reshape(1, B_WIDTH), w_s.astype(BF16), bias)


def _route(logits):
    lane = lax.broadcasted_iota(jnp.int32, (1, ROUTE_W), 1)
    lane_f = lane.astype(F32)
    is_g = lane < N_EXPERT_GROUPS
    lg = jnp.where(is_g, logits, NEG_INF)
    mg = jnp.max(lg, -1, keepdims=True)
    pg_top = 1.0 / jnp.sum(jnp.where(is_g, jnp.exp(lg - mg), 0.0), -1, keepdims=True)
    g_idx = jnp.min(jnp.where(lg == mg, lane_f, float(ROUTE_W)), -1, keepdims=True)
    e_group = ((lane - ROUTE_E0) // EXPERTS_PER_GROUP).astype(F32)
    in_grp = (lane >= ROUTE_E0) & (lane < ROUTE_E0 + N_EXPERTS) & (e_group == g_idx)
    le = jnp.where(in_grp, logits, NEG_INF)
    m1 = jnp.max(le, -1, keepdims=True)
    i1 = jnp.min(jnp.where(le == m1, lane_f, float(ROUTE_W)), -1, keepdims=True)
    le2 = jnp.where(lane_f == i1, NEG_INF, le)
    m2 = jnp.max(le2, -1, keepdims=True)
    i2 = jnp.min(jnp.where(le2 == m2, lane_f, float(ROUTE_W)), -1, keepdims=True)
    t = jnp.exp(m2 - m1)
    w1 = pg_top / (1.0 + t)
    w2 = w1 * t
    picks = (i1 - ROUTE_E0, i2 - ROUTE_E0, w1, w2)
    out = jnp.zeros(logits.shape, F32)
    for k, val in enumerate(picks):
        out = jnp.where(lane == k, val, out)
    return out


def _merge_kernel(x_ref, oa_ref, ob_ref, oc_ref,
                  g0_ref, g1_ref, g2_ref, g3_ref, wa_ref, wb_ref, wc_ref, wo_ref, n2_ref, wr_ref,
                  xo_ref, h2_ref, route_ref):
    zg = jnp.concatenate([g0_ref[...], g1_ref[...], g2_ref[...], g3_ref[...]], axis=1).astype(F32)
    gates = jax.nn.sigmoid(zg)
    pa = jnp.dot(oa_ref[...], wa_ref[...], preferred_element_type=F32)
    pb = jnp.dot(ob_ref[...], wb_ref[...], preferred_element_type=F32)
    pc = jnp.dot(oc_ref[...], wc_ref[...], preferred_element_type=F32)
    merged = (gates[:, :D_MODEL] * pa + gates[:, D_MODEL:2 * D_MODEL] * pb
              + gates[:, 2 * D_MODEL:] * pc)
    xn = x_ref[...] + jnp.dot(merged.astype(BF16), wo_ref[...], preferred_element_type=F32)
    xo_ref[...] = xn
    ms = jnp.mean(xn * xn, -1, keepdims=True)
    h2 = xn * lax.rsqrt(ms + EPS) * n2_ref[...]
    h2_ref[...] = _pack_bf16_pairs(h2)
    route_ref[...] = _route(jnp.dot(h2.astype(BF16), wr_ref[...], preferred_element_type=F32))


def _merge(x2, oa, ob, oc, z2, wa, wb, wc, wo, n2g, w_rg, w_re):
    n, d = x2.shape
    tm = 512
    wr = jnp.concatenate(
        [w_rg, w_re, jnp.zeros((d, ROUTE_W - N_EXPERT_GROUPS - N_EXPERTS), w_rg.dtype)], axis=1)
    gate_w = GATE_W // 4
    gate_col0 = COL_G * IN_TN // gate_w

    def rows(width):
        return pl.BlockSpec((tm, width), lambda i: (i, 0))

    def full(shape):
        return pl.BlockSpec(shape, lambda i: (0, 0))

    gate_specs = [pl.BlockSpec((tm, gate_w), lambda i, k=k: (i, gate_col0 + k)) for k in range(4)]
    return pl.pallas_call(
        _merge_kernel,
        out_shape=[jax.ShapeDtypeStruct((n, d), F32),
                   jax.ShapeDtypeStruct((n, d // 2), jnp.int32),
                   jax.ShapeDtypeStruct((n, ROUTE_W), F32)],
        grid=(n // tm,),
        in_specs=[rows(d), rows(A_Q_W), rows(B_WIDTH), rows(C_OUT_W),
                  *gate_specs,
                  full((A_Q_W, d)), full((B_WIDTH, d)), full((C_OUT_W, d)), full((d, d)),
                  full((1, d)), full((d, ROUTE_W))],
        out_specs=[rows(d), rows(d // 2), rows(ROUTE_W)],
        compiler_params=_cparams(("parallel",)),
        name="merge",
    )(x2, oa, ob, oc, z2, z2, z2, z2,
      wa.astype(BF16), wb.astype(BF16), wc.astype(BF16), wo.astype(BF16),
      n2g.reshape(1, d), wr.astype(BF16))


MOE_TM = 1024
SC_CORES = 2
SC_SUBCORES = 16
SC_WORKERS = SC_CORES * SC_SUBCORES
SC_CHUNK = 128


def _pack_bf16_pairs(x):
    w = x.shape[1] // 2
    lo = pltpu.bitcast(x[:, :w].astype(BF16).astype(F32), jnp.int32)
    hi = pltpu.bitcast(x[:, w:].astype(BF16).astype(F32), jnp.int32)
    return (hi & jnp.int32(-65536)) | lax.shift_right_logical(lo, jnp.int32(16))


def _unpack_bf16_pairs(p):
    lo = pltpu.bitcast(lax.shift_left(p, jnp.int32(16)), F32)
    hi = pltpu.bitcast(p & jnp.int32(-65536), F32)
    return jnp.concatenate([lo, hi], axis=1)


def _sc_gather_rows(table, idx):
    rows, width = idx.shape[0], table.shape[1]
    per_worker = rows // SC_WORKERS
    n_chunks = per_worker // SC_CHUNK
    assert per_worker * SC_WORKERS == rows and n_chunks * SC_CHUNK == per_worker
    mesh = plsc.VectorSubcoreMesh(core_axis_name="c", subcore_axis_name="s",
                                  num_cores=SC_CORES, num_subcores=SC_SUBCORES)

    @functools.partial(
        pl.kernel, mesh=mesh,
        out_type=jax.ShapeDtypeStruct((rows, width), table.dtype),
        scratch_types=[pltpu.VMEM((n_chunks, SC_CHUNK), jnp.int32),
                       pltpu.VMEM((SC_CHUNK, width), table.dtype),
                       pltpu.SemaphoreType.DMA],
        name="sc_gather_rows",
    )
    def gather(table_hbm, idx_hbm, out_hbm, idx_v, rows_v, sem):
        wid = lax.axis_index("s") * SC_CORES + lax.axis_index("c")
        pltpu.sync_copy(idx_hbm.at[wid], idx_v)
        base = wid * per_worker

        @pl.loop(0, n_chunks)
        def _(c):
            pltpu.async_copy(table_hbm.at[idx_v.at[c]], rows_v, sem).wait()
            pltpu.sync_copy(rows_v, out_hbm.at[pl.ds(base + c * SC_CHUNK, SC_CHUNK)])

    return gather(table, idx.reshape(SC_WORKERS, n_chunks, SC_CHUNK))


def _sc_scatter_rows(rows, pos, n_slots):
    n, width = rows.shape
    per_worker = n // SC_WORKERS
    n_chunks = per_worker // SC_CHUNK
    assert per_worker * SC_WORKERS == n and n_chunks * SC_CHUNK == per_worker
    mesh = plsc.VectorSubcoreMesh(core_axis_name="c", subcore_axis_name="s",
                                  num_cores=SC_CORES, num_subcores=SC_SUBCORES)

    @functools.partial(
        pl.kernel, mesh=mesh,
        out_type=jax.ShapeDtypeStruct((n_slots, width), rows.dtype),
        scratch_types=[pltpu.VMEM((TOP_K, n_chunks, SC_CHUNK), jnp.int32),
                       pltpu.VMEM((SC_CHUNK, width), rows.dtype),
                       pltpu.SemaphoreType.DMA],
        name="sc_scatter_rows",
    )
    def scatter(rows_hbm, pos_hbm, out_hbm, pos_v, rows_v, sem):
        wid = lax.axis_index("s") * SC_CORES + lax.axis_index("c")
        for k in range(TOP_K):
            pltpu.sync_copy(pos_hbm.at[k, wid], pos_v.at[k])
        base = wid * per_worker

        @pl.loop(0, n_chunks)
        def _(c):
            pltpu.sync_copy(rows_hbm.at[pl.ds(base + c * SC_CHUNK, SC_CHUNK)], rows_v)
            for k in range(TOP_K):
                pltpu.async_copy(rows_v, out_hbm.at[pos_v.at[k, c]], sem).wait()

    return scatter(rows, pos.reshape(TOP_K, SC_WORKERS, n_chunks, SC_CHUNK))


def _dispatch_plan(route, n_slots, expert_base):
    n = route.shape[0]
    e = route[:, 0:TOP_K].astype(jnp.int32).reshape(-1)
    experts = jnp.arange(N_EXPERTS, dtype=jnp.int32)
    onehot = (e[:, None] == experts[None, :]).astype(jnp.int32)
    csum = jnp.cumsum(onehot, axis=0)
    rank = jnp.sum((csum - onehot) * onehot, axis=1)
    counts = csum[-1]
    padded = (counts + MOE_TM - 1) // MOE_TM * MOE_TM
    ends = jnp.cumsum(padded)
    starts = ends - padded
    pos = starts[e] + rank
    tile_start = jnp.arange(n_slots // MOE_TM, dtype=jnp.int32) * MOE_TM
    tile_e = jnp.minimum(jnp.sum((tile_start[:, None] >= ends[None, :]).astype(jnp.int32), axis=1),
                         N_EXPERTS - 1)
    tile_rows = jnp.clip(counts[tile_e] - (tile_start - starts[tile_e]), 0, MOE_TM)
    tile_rows = jnp.where(tile_start < ends[-1], tile_rows, 0)
    first = jnp.concatenate([jnp.ones((1,), jnp.int32),
                             (tile_e[1:] != tile_e[:-1]).astype(jnp.int32)])
    return pos.reshape(n, TOP_K).T, tile_e + expert_base, tile_rows.astype(jnp.int32), first


def _moe_tile_kernel(te_ref, tr_ref, tf_ref, xs_ref, wg_ref, wu_ref, wd_ref, ys_ref,
                     wg_s, wu_s, wd_s):
    t = pl.program_id(0)

    @pl.when(tf_ref[t] != 0)
    def _():
        wg_s[...] = wg_ref[0].astype(BF16)
        wu_s[...] = wu_ref[0].astype(BF16)
        wd_s[...] = wd_ref[0].astype(BF16)

    @pl.when(tr_ref[t] != 0)
    def _():
        occupied = lax.broadcasted_iota(jnp.int32, (MOE_TM, 1), 0) < tr_ref[t]
        x = jnp.where(occupied, _unpack_bf16_pairs(xs_ref[...]), 0.0).astype(BF16)
        hg = jnp.dot(x, wg_s[...], preferred_element_type=F32)
        hu = jnp.dot(x, wu_s[...], preferred_element_type=F32)
        a = (jax.nn.silu(hg) * hu).astype(BF16)
        ys_ref[...] = _pack_bf16_pairs(jnp.dot(a, wd_s[...], preferred_element_type=F32))

    @pl.when(tr_ref[t] == 0)
    def _():
        ys_ref[...] = jnp.zeros_like(ys_ref)


def _moe_tiles(xs, tile_expert, tile_rows, tile_first, w_gate, w_up, w_down):
    n_slots, half = xs.shape
    d = 2 * half
    grid_spec = pltpu.PrefetchScalarGridSpec(
        num_scalar_prefetch=3,
        grid=(n_slots // MOE_TM,),
        in_specs=[
            pl.BlockSpec((MOE_TM, half), lambda t, te, tr, tf: (t, 0)),
            pl.BlockSpec((1, d, D_EXPERT), lambda t, te, tr, tf: (te[t], 0, 0)),
            pl.BlockSpec((1, d, D_EXPERT), lambda t, te, tr, tf: (te[t], 0, 0)),
            pl.BlockSpec((1, D_EXPERT, d), lambda t, te, tr, tf: (te[t], 0, 0)),
        ],
        out_specs=pl.BlockSpec((MOE_TM, half), lambda t, te, tr, tf: (t, 0)),
        scratch_shapes=[pltpu.VMEM((d, D_EXPERT), BF16), pltpu.VMEM((d, D_EXPERT), BF16),
                        pltpu.VMEM((D_EXPERT, d), BF16)],
    )
    return pl.pallas_call(
        _moe_tile_kernel,
        out_shape=jax.ShapeDtypeStruct((n_slots, half), jnp.int32),
        grid_spec=grid_spec,
        compiler_params=_cparams(("arbitrary",)),
        name="moe_tiles",
    )(tile_expert, tile_rows, tile_first, xs, w_gate, w_up, w_down)


def _moe_combine_kernel(x_ref, y0_ref, y1_ref, r_ref, g_ref, o_ref, *, final_norm):
    w0 = r_ref[:, 2:3]
    w1 = r_ref[:, 3:4]
    x = x_ref[...] + w0 * _unpack_bf16_pairs(y0_ref[0]) + w1 * _unpack_bf16_pairs(y1_ref[0])
    if final_norm:
        ms = jnp.mean(x * x, -1, keepdims=True)
        x = x * lax.rsqrt(ms + EPS) * g_ref[...]
    o_ref[...] = x


def _moe_combine(x2, yg, route, norm_g, final_norm):
    n, d = x2.shape
    tm = 1024
    return pl.pallas_call(
        functools.partial(_moe_combine_kernel, final_norm=final_norm),
        out_shape=jax.ShapeDtypeStruct((n, d), F32),
        grid=(n // tm,),
        in_specs=[
            pl.BlockSpec((tm, d), lambda i: (i, 0)),
            pl.BlockSpec((1, tm, d // 2), lambda i: (0, i, 0)),
            pl.BlockSpec((1, tm, d // 2), lambda i: (1, i, 0)),
            pl.BlockSpec((tm, ROUTE_W), lambda i: (i, 0)),
            pl.BlockSpec((1, d), lambda i: (0, 0)),
        ],
        out_specs=pl.BlockSpec((tm, d), lambda i: (i, 0)),
        compiler_params=_cparams(("parallel",)),
        name="moe_combine",
    )(x2, yg, yg, route, norm_g.reshape(1, d))


def _moe(h2p, route, x2, layer, w_gate, w_up, w_down, norm_f_g, final_norm):
    n, d = x2.shape
    n_slots = TOP_K * n + N_EXPERTS * MOE_TM
    pos, tile_expert, tile_rows, tile_first = _dispatch_plan(route, n_slots, layer * N_EXPERTS)
    xs = _sc_scatter_rows(h2p, pos, n_slots)
    ys = _moe_tiles(xs, tile_expert, tile_rows, tile_first, w_gate, w_up, w_down)
    yg = _sc_gather_rows(ys, pos.reshape(-1)).reshape(TOP_K, n, d // 2)
    return _moe_combine(x2, yg, route, norm_f_g, final_norm)


def kernel(x, positions, norm1_g, w_in, b_in, attn_sinks, sgu_ln_g, sgu_ln_b, w_spatial, b_spatial,
           w_proj_a, w_proj_b, w_proj_c, w_out, norm2_g, w_router_group, w_router_expert,
           w_expert_gate, w_expert_up, w_expert_down, norm_f_g):
    bsz, s, d = x.shape
    depth = w_in.shape[0]
    wg_all = w_expert_gate.reshape(depth * N_EXPERTS, d, D_EXPERT)
    wu_all = w_expert_up.reshape(depth * N_EXPERTS, d, D_EXPERT)
    wd_all = w_expert_down.reshape(depth * N_EXPERTS, D_EXPERT, d)
    assert d == D_MODEL and s % (C_PAIRS[-1][1] * BLOCK) == 0
    tabs = _rope_tables(positions)
    for l in range(depth):
        z = _in_proj(x, norm1_g[l], w_in[l].astype(BF16), b_in[l], tabs)
        oa = _attn_a(z, attn_sinks[l])
        ob = _sgu(z, sgu_ln_g[l], sgu_ln_b[l], w_spatial[l], b_spatial[l])
        oc = _attn_c(z)
        x2, h2, route = _merge(
            x.reshape(bsz * s, d), oa.reshape(bsz * s, A_Q_W), ob.reshape(bsz * s, B_WIDTH),
            oc.reshape(bsz * s, C_OUT_W), z.reshape(bsz * s, D_IN), w_proj_a[l], w_proj_b[l], w_proj_c[l], w_out[l],
            norm2_g[l], w_router_group[l], w_router_expert[l])
        x = _moe(h2, route, x2, l, wg_all, wu_all, wd_all, norm_f_g, l == depth - 1).reshape(bsz, s, d)
    return x
```

```python
import functools

import jax
import jax.numpy as jnp
from jax import lax
from jax.experimental import pallas as pl
from jax.experimental.pallas import tpu as pltpu
from jax.experimental.pallas import tpu_sc as plsc

F32 = jnp.float32
BF16 = jnp.bfloat16

D_MODEL = 1024
HEAD_DIM = 64
ROT_DIM = HEAD_DIM // 4
ROPE_THETA = 500000.0
BLOCK = 128
EPS = 1e-5
NEG_INF = -1e30

A_Q_HEADS = 16
A_KV_HEADS = 4
A_REP = A_Q_HEADS // A_KV_HEADS
A_WINDOW = 128
A_Q_W = A_Q_HEADS * HEAD_DIM
A_KV_W = A_KV_HEADS * HEAD_DIM

B_GROUPS = 12
B_CH = 64
B_WIDTH = B_GROUPS * B_CH
B_CHUNK = 128

C_PAIRS = ((128, 1), (512, 4), (2048, 16))
C_HEADS_PER_GROUP = 4
C_HEADS = C_HEADS_PER_GROUP * len(C_PAIRS)
C_OUT_W = C_HEADS_PER_GROUP * HEAD_DIM

N_BRANCH = 3
GATE_W = N_BRANCH * D_MODEL
D_IN = A_Q_W + 2 * A_KV_W + 2 * B_WIDTH + 3 * C_HEADS * HEAD_DIM + GATE_W

N_EXPERT_GROUPS = 4
EXPERTS_PER_GROUP = 8
N_EXPERTS = N_EXPERT_GROUPS * EXPERTS_PER_GROUP
TOP_K = 2
D_EXPERT = 256

LANES = 128
MXU_W = 256
VMEM_LIMIT = 56 * 1024 * 1024

IN_TN = 256
COL_AQ = 0
COL_AK = A_Q_W // IN_TN
COL_AV = COL_AK + A_KV_W // IN_TN
COL_BU = COL_AV + A_KV_W // IN_TN
COL_BV = COL_BU + B_WIDTH // IN_TN
COL_CQ = COL_BV + B_WIDTH // IN_TN
COL_CK = COL_CQ + C_HEADS * HEAD_DIM // IN_TN
COL_CV = COL_CK + C_HEADS * HEAD_DIM // IN_TN
COL_G = COL_CV + C_HEADS * HEAD_DIM // IN_TN
N_COL_TILES = D_IN // IN_TN

ROUTE_W = LANES
ROUTE_E0 = N_EXPERT_GROUPS


def _cparams(sem):
    return pltpu.CompilerParams(dimension_semantics=sem, vmem_limit_bytes=VMEM_LIMIT)


def _rope_table_kernel(pos_ref, inv_ref, c_ref, s1_ref, s2_ref):
    lane = lax.broadcasted_iota(jnp.int32, (1, LANES), 1)
    d = lane & (HEAD_DIM - 1)
    ang = pos_ref[...].astype(F32) * inv_ref[...]
    c = jnp.cos(ang)
    s = jnp.sin(ang)
    half = ROT_DIM // 2
    c_ref[...] = jnp.where(d < ROT_DIM, c, 1.0)
    s1_ref[...] = jnp.where(d < half, -s, 0.0)
    s2_ref[...] = jnp.where((d >= half) & (d < ROT_DIM), s, 0.0)


def _rope_tables(positions):
    n = positions.size
    inv = ROPE_THETA ** (-jnp.arange(0, ROT_DIM, 2, dtype=F32) / ROT_DIM)
    inv_lane = jnp.tile(inv, LANES // inv.shape[0]).reshape(1, LANES)
    pos_b = jnp.broadcast_to(positions.reshape(n, 1), (n, LANES))
    tm = 1024
    spec = pl.BlockSpec((tm, LANES), lambda i: (i, 0))
    return pl.pallas_call(
        _rope_table_kernel,
        out_shape=[jax.ShapeDtypeStruct((n, LANES), F32)] * 3,
        grid=(n // tm,),
        in_specs=[spec, pl.BlockSpec((1, LANES), lambda i: (0, 0))],
        out_specs=[spec, spec, spec],
        compiler_params=_cparams(("parallel",)),
        name="rope_tables",
    )(pos_b, inv_lane)


IN_STEP_TILES = 3
IN_STEP_W = IN_STEP_TILES * IN_TN
IN_STEPS = N_COL_TILES // IN_STEP_TILES


def _is_rope_tile(tile):
    return tile < COL_AV or COL_CQ <= tile < COL_CV


_STEP_ROPES = tuple(tuple(_is_rope_tile(step * IN_STEP_TILES + u) for u in range(IN_STEP_TILES))
                    for step in range(IN_STEPS))


def _in_proj_kernel(x_ref, g_ref, w_ref, b_ref, c_ref, s1_ref, s2_ref, z_ref, h_ref,
                    acc0_ref, acc1_ref):
    j = pl.program_id(1)

    @pl.when(j == 0)
    def _():
        x = x_ref[0]
        ms = jnp.mean(x * x, -1, keepdims=True)
        h_ref[...] = (x * lax.rsqrt(ms + EPS) * g_ref[...]).astype(BF16)

    prev = j - 1

    def matmul_into(acc_ref):
        acc_ref[...] = jnp.dot(h_ref[...], w_ref[...], preferred_element_type=F32) + b_ref[...]

    def finish(acc_ref, ropes):
        for u, rope in enumerate(ropes):
            cols = slice(u * IN_TN, (u + 1) * IN_TN)
            if not rope:
                z_ref[0, :, cols] = acc_ref[:, cols].astype(BF16)
                continue
            tile = prev * IN_STEP_TILES + u
            is_q = (tile < COL_AK) | ((tile >= COL_CQ) & (tile < COL_CK))
            scale = jnp.where(is_q, HEAD_DIM ** -0.5, 1.0).astype(F32)
            for t in range(u * IN_TN // LANES, (u + 1) * IN_TN // LANES):
                lanes = slice(t * LANES, (t + 1) * LANES)
                a = acc_ref[:, lanes]
                r = (a * c_ref[...] + pltpu.roll(a, LANES - ROT_DIM // 2, 1) * s1_ref[...]
                     + pltpu.roll(a, ROT_DIM // 2, 1) * s2_ref[...])
                z_ref[0, :, lanes] = (r * scale).astype(BF16)

    accs = (acc0_ref, acc1_ref)
    patterns = sorted(set(_STEP_ROPES))
    pattern_id = jnp.int32(0)
    for step, ropes in enumerate(_STEP_ROPES):
        pattern_id = jnp.where(prev == step, patterns.index(ropes), pattern_id)
    inner = (j > 0) & (j < IN_STEPS)
    for parity in range(2):
        for pid, ropes in enumerate(patterns):
            @pl.when(inner & (j % 2 == parity) & (pattern_id == pid))
            def _(parity=parity, ropes=ropes):
                matmul_into(accs[parity])
                finish(accs[1 - parity], ropes)

    @pl.when(j == 0)
    def _():
        matmul_into(accs[0])

    @pl.when(j == IN_STEPS)
    def _():
        finish(accs[(IN_STEPS - 1) % 2], _STEP_ROPES[-1])


def _in_proj(x, g, w, b, tabs):
    bsz, s, d = x.shape
    c, s1, s2 = tabs
    tab_spec = pl.BlockSpec((s, LANES), lambda bi, j: (bi, 0))
    last = IN_STEPS - 1
    return pl.pallas_call(
        _in_proj_kernel,
        out_shape=jax.ShapeDtypeStruct((bsz, s, D_IN), BF16),
        grid=(bsz, IN_STEPS + 1),
        in_specs=[
            pl.BlockSpec((1, s, d), lambda bi, j: (bi, 0, 0)),
            pl.BlockSpec((1, d), lambda bi, j: (0, 0)),
            pl.BlockSpec((d, IN_STEP_W), lambda bi, j: (0, jnp.minimum(j, last))),
            pl.BlockSpec((1, IN_STEP_W), lambda bi, j: (0, jnp.minimum(j, last))),
            tab_spec, tab_spec, tab_spec,
        ],
        out_specs=pl.BlockSpec((1, s, IN_STEP_W), lambda bi, j: (bi, 0, jnp.maximum(j - 1, 0))),
        scratch_shapes=[pltpu.VMEM((s, d), BF16), pltpu.VMEM((s, IN_STEP_W), F32),
                        pltpu.VMEM((s, IN_STEP_W), F32)],
        compiler_params=_cparams(("parallel", "arbitrary")),
        name="in_proj",
    )(x, g.reshape(1, d), w, b.reshape(1, D_IN), c, s1, s2)


STACK = C_HEADS_PER_GROUP * BLOCK
STEP_UNROLL = 4


def _head_stack(blk):
    seg = lax.broadcasted_iota(jnp.int32, (1, MXU_W), 1) // HEAD_DIM
    rowseg = lax.broadcasted_iota(jnp.int32, (STACK, 1), 0) // BLOCK
    return jnp.where(seg == rowseg, jnp.concatenate([blk] * C_HEADS_PER_GROUP, axis=0),
                     jnp.zeros((), blk.dtype))


def _band_step(q, kp_ref, vp_ref, blk, n, has_prev, diag_key, sink_col, want_lse):
    return _band_blocks([(q, blk, n)], kp_ref, vp_ref, has_prev, diag_key, sink_col, want_lse)[0]


def _band_blocks(jobs, kp_ref, vp_ref, has_prev, diag_key, sink_col, want_lse):
    contract = (((1,), (1,)), ((), ()))
    i_loc = lax.broadcasted_iota(jnp.int32, (STACK, 1), 0) & (BLOCK - 1)
    jj = lax.broadcasted_iota(jnp.int32, (1, BLOCK), 1)
    upper = jj > i_loc
    seg = lax.broadcasted_iota(jnp.int32, (1, MXU_W), 1) // HEAD_DIM
    zero_b = jnp.zeros((), BF16)

    def key_rows(blk):
        lo = pl.multiple_of(blk * BLOCK, BLOCK)
        if has_prev:
            return pl.ds(lo, 2 * BLOCK)
        return pl.ds(pl.multiple_of(lo + BLOCK, BLOCK), BLOCK)

    scores = [lax.dot_general(_head_stack(q), kp_ref[key_rows(blk), :], contract,
                              preferred_element_type=F32) for q, blk, _ in jobs]

    def own_lanes(x):
        out = x[0:BLOCK]
        for h in range(1, C_HEADS_PER_GROUP):
            out = jnp.where(seg == h, x[h * BLOCK:(h + 1) * BLOCK], out)
        return out

    probs, stats = [], []
    for s, (_, blk, n) in zip(scores, jobs):
        if has_prev:
            sp = s[:, :BLOCK] + jnp.where(n == 0, NEG_INF, 0.0).astype(F32)
            f = jnp.where(upper, sp, s[:, BLOCK:])
        else:
            f = jnp.where(upper, NEG_INF, s)
        m = jnp.max(f, -1, keepdims=True)
        if diag_key:
            sd = jnp.sum(jnp.where(jj == i_loc, sp, 0.0), -1, keepdims=True)
            m = jnp.maximum(m, sd)
        if sink_col is not None:
            m = jnp.maximum(m, sink_col)
        p = jnp.exp(f - m)
        den = jnp.sum(p, -1, keepdims=True)
        pd = None
        if diag_key:
            pd = jnp.exp(sd - m)
            den = den + pd
        if sink_col is not None:
            den = den + jnp.exp(sink_col - m)
        inv = 1.0 / den
        pn = (p * inv).astype(BF16)
        if has_prev:
            below = zero_b
            if diag_key:
                below = jnp.where(jj == i_loc, pd * inv, 0.0).astype(BF16)
            pn = jnp.concatenate([jnp.where(upper, pn, below), jnp.where(upper, zero_b, pn)],
                                 axis=1)
        probs.append(pn)
        stats.append((m, den))

    outs = []
    for pn, (m, den), (_, blk, _) in zip(probs, stats, jobs):
        o = own_lanes(jnp.dot(pn, vp_ref[key_rows(blk), :], preferred_element_type=F32))
        lse = own_lanes(m + jnp.log(den)) + jnp.zeros((BLOCK, MXU_W), F32) if want_lse else None
        outs.append((o, lse))
    return outs


def _replicate_head(blk, g):
    words = pltpu.bitcast(blk, jnp.uint32)
    tile = words[:, (g // 2) * LANES:(g // 2 + 1) * LANES]
    swapped = pltpu.roll(tile, HEAD_DIM, 1)
    low = lax.broadcasted_iota(jnp.int32, (1, LANES), 1) < HEAD_DIM
    both = jnp.where(low, tile, swapped) if g % 2 == 0 else jnp.where(low, swapped, tile)
    return pltpu.bitcast(jnp.concatenate([both, both], axis=1), BF16)


def _attn_a_kernel(sink_ref, q_ref, k_ref, v_ref, o_ref, kp_ref, vp_ref):
    nb = q_ref.shape[1] // BLOCK
    rowseg = lax.broadcasted_iota(jnp.int32, (STACK, 1), 0) // BLOCK
    kp_ref[pl.ds(0, BLOCK), :] = jnp.zeros((BLOCK, MXU_W), BF16)
    vp_ref[pl.ds(0, BLOCK), :] = jnp.zeros((BLOCK, MXU_W), BF16)
    for g in range(A_KV_HEADS):
        kv_cols = slice(g * HEAD_DIM, (g + 1) * HEAD_DIM)
        q_cols = slice(g * MXU_W, (g + 1) * MXU_W)

        def build(n, carry):
            rows = pl.ds(pl.multiple_of(n * BLOCK, BLOCK), BLOCK)
            dst = pl.ds(pl.multiple_of((n + 1) * BLOCK, BLOCK), BLOCK)
            kp_ref[dst, :] = _replicate_head(k_ref[0, rows, :], g)
            vp_ref[dst, :] = _replicate_head(v_ref[0, rows, :], g)
            return carry

        lax.fori_loop(0, nb, build, 0, unroll=8)
        sink_col = jnp.zeros((STACK, 1), F32)
        for r in range(A_REP):
            sink_col = jnp.where(rowseg == r, sink_ref[g * A_REP + r], sink_col)

        def step(it, carry):
            blocks = [it * STEP_UNROLL + u for u in range(STEP_UNROLL)]
            rows = [pl.ds(pl.multiple_of(n * BLOCK, BLOCK), BLOCK) for n in blocks]
            jobs = [(q_ref[0, r, q_cols], n, n) for r, n in zip(rows, blocks)]
            outs = _band_blocks(jobs, kp_ref, vp_ref, True, False, sink_col, False)
            for r, (o, _) in zip(rows, outs):
                o_ref[0, r, q_cols] = o.astype(o_ref.dtype)
            return carry

        lax.fori_loop(0, nb // STEP_UNROLL, step, 0)


def _attn_a(z, sinks):
    bsz, s, _ = z.shape
    nb = s // BLOCK
    grid_spec = pltpu.PrefetchScalarGridSpec(
        num_scalar_prefetch=1,
        grid=(bsz,),
        in_specs=[
            pl.BlockSpec((1, s, A_Q_W), lambda bi, sk: (bi, 0, 0)),
            pl.BlockSpec((1, s, A_KV_W), lambda bi, sk: (bi, 0, COL_AK)),
            pl.BlockSpec((1, s, A_KV_W), lambda bi, sk: (bi, 0, COL_AV)),
        ],
        out_specs=pl.BlockSpec((1, s, A_Q_W), lambda bi, sk: (bi, 0, 0)),
        scratch_shapes=[pltpu.VMEM(((nb + 1) * BLOCK, MXU_W), BF16),
                        pltpu.VMEM(((nb + 1) * BLOCK, MXU_W), BF16)],
    )
    return pl.pallas_call(
        _attn_a_kernel,
        out_shape=jax.ShapeDtypeStruct((bsz, s, A_Q_W), BF16),
        grid_spec=grid_spec,
        compiler_params=_cparams(("parallel",)),
        name="attn_a",
    )(sinks, z, z, z)


def _attn_c_kernel(q_ref, k_ref, v_ref, oc_ref, stage_ref, pq_ref, pk_ref, pv_ref, og_ref, lg_ref):
    s = q_ref.shape[1]
    nblk = s // BLOCK
    halves = MXU_W // LANES
    pk_ref[pl.ds(0, BLOCK), :] = jnp.zeros((BLOCK, MXU_W), BF16)
    pv_ref[pl.ds(0, BLOCK), :] = jnp.zeros((BLOCK, MXU_W), BF16)

    def class_rows(idx, nb, d):
        r = idx // nb
        n = idx - r * nb
        start = r + n * (BLOCK * d)
        rows = pl.ds(start, BLOCK, stride=d) if d > 1 else pl.ds(pl.multiple_of(start, BLOCK), BLOCK)
        return n, rows

    def stage(src_ref, cols):
        x = src_ref[0, :, cols].astype(F32)
        for t in range(halves):
            stage_ref[t] = x[:, t * LANES:(t + 1) * LANES]

    def staged_block(rows):
        return jnp.concatenate([stage_ref[t, rows, :] for t in range(halves)], axis=1).astype(BF16)

    for g, (win, d) in enumerate(C_PAIRS):
        cols = slice(g * MXU_W, (g + 1) * MXU_W)
        nb = nblk // d
        has_prev = nb > 1
        assert win // d == BLOCK

        def class_major(dst_ref, offset):
            def build(idx, carry):
                _, rows = class_rows(idx, nb, d)
                dst = pl.ds(pl.multiple_of((idx + offset) * BLOCK, BLOCK), BLOCK)
                dst_ref[dst, :] = staged_block(rows)
                return carry
            return build

        for src_ref, dst_ref, offset in ((k_ref, pk_ref, 1), (v_ref, pv_ref, 1), (q_ref, pq_ref, 0)):
            stage(src_ref, cols)
            lax.fori_loop(0, nblk, class_major(dst_ref, offset), 0, unroll=4)

        def step(it, carry):
            jobs, dsts = [], []
            for u in range(STEP_UNROLL):
                idx = it * STEP_UNROLL + u
                n, rows = class_rows(idx, nb, d)
                here = pl.ds(pl.multiple_of(idx * BLOCK, BLOCK), BLOCK)
                jobs.append((pq_ref[here, :], idx, n))
                dsts.append(rows)
            outs = _band_blocks(jobs, pk_ref, pv_ref, has_prev, has_prev, None, True)
            for rows, (o, lse) in zip(dsts, outs):
                for t in range(halves):
                    og_ref[g, t, rows, :] = o[:, t * LANES:(t + 1) * LANES]
                    lg_ref[g, t, rows, :] = lse[:, t * LANES:(t + 1) * LANES]
            return carry

        lax.fori_loop(0, nblk // STEP_UNROLL, step, 0)

    chunk = 2 * BLOCK

    def combine(i, carry):
        rows = pl.ds(pl.multiple_of(i * chunk, chunk), chunk)
        for t in range(halves):
            ls = [lg_ref[g, t, rows, :] for g in range(len(C_PAIRS))]
            m = jnp.maximum(jnp.maximum(ls[0], ls[1]), ls[2])
            es = [jnp.exp(l - m) for l in ls]
            num = (es[0] * og_ref[0, t, rows, :] + es[1] * og_ref[1, t, rows, :]
                   + es[2] * og_ref[2, t, rows, :])
            oc_ref[0, rows, t * LANES:(t + 1) * LANES] = (
                num / (es[0] + es[1] + es[2])).astype(oc_ref.dtype)
        return carry

    lax.fori_loop(0, s // chunk, combine, 0)


def _attn_c(z):
    bsz, s, _ = z.shape
    nblk = s // BLOCK
    width = C_HEADS * HEAD_DIM
    halves = MXU_W // LANES

    def in_spec(col):
        return pl.BlockSpec((1, s, width), lambda bi: (bi, 0, col * IN_TN // width))

    return pl.pallas_call(
        _attn_c_kernel,
        out_shape=jax.ShapeDtypeStruct((bsz, s, C_OUT_W), BF16),
        grid=(bsz,),
        in_specs=[in_spec(COL_CQ), in_spec(COL_CK), in_spec(COL_CV)],
        out_specs=pl.BlockSpec((1, s, C_OUT_W), lambda bi: (bi, 0, 0)),
        scratch_shapes=[
            pltpu.VMEM((halves, s, LANES), F32),
            pltpu.VMEM((s, MXU_W), BF16),
            pltpu.VMEM((s + BLOCK, MXU_W), BF16),
            pltpu.VMEM((s + BLOCK, MXU_W), BF16),
            pltpu.VMEM((len(C_PAIRS), halves, s, LANES), F32),
            pltpu.VMEM((len(C_PAIRS), halves, s, LANES), F32),
        ],
        compiler_params=_cparams(("parallel",)),
        name="attn_c",
    )(z, z, z)


def _sgu_kernel(zu_ref, zv_ref, lng_ref, lnb_ref, ws_ref, bs_ref, o_ref):
    tm = zu_ref.shape[1]
    u = jax.nn.gelu(zu_ref[0].astype(F32))
    v = jax.nn.gelu(zv_ref[0].astype(F32))
    mu = jnp.mean(v, -1, keepdims=True)
    var = jnp.mean(jnp.square(v - mu), -1, keepdims=True)
    vn = ((v - mu) * lax.rsqrt(var + EPS) * lng_ref[...] + lnb_ref[...]).astype(BF16)
    causal = (lax.broadcasted_iota(jnp.int32, (B_CHUNK, 1), 0)
              >= lax.broadcasted_iota(jnp.int32, (1, B_CHUNK), 1))
    first = lax.broadcasted_iota(jnp.int32, (1, LANES), 1) < B_CH
    zero = jnp.zeros((), BF16)
    for p in range(B_WIDTH // LANES):
        w0 = jnp.where(causal, ws_ref[2 * p], zero)
        w1 = jnp.where(causal, ws_ref[2 * p + 1], zero)
        cols = slice(p * LANES, (p + 1) * LANES)
        for c in range(tm // B_CHUNK):
            rows = slice(c * B_CHUNK, (c + 1) * B_CHUNK)
            vv = vn[rows, cols]
            sv = jnp.where(first,
                           jnp.dot(w0, vv, preferred_element_type=F32),
                           jnp.dot(w1, vv, preferred_element_type=F32)) + bs_ref[:, cols]
            o_ref[0, rows, cols] = (u[rows, cols] * sv).astype(o_ref.dtype)


def _sgu(z, ln_g, ln_b, w_s, b_s):
    bsz, s, _ = z.shape
    tm = 512
    bias = jnp.repeat(b_s.T, B_CH, axis=1)
    col_u = COL_BU * IN_TN // B_WIDTH
    col_v = COL_BV * IN_TN // B_WIDTH
    return pl.pallas_call(
        _sgu_kernel,
        out_shape=jax.ShapeDtypeStruct((bsz, s, B_WIDTH), BF16),
        grid=(bsz, s // tm),
        in_specs=[
            pl.BlockSpec((1, tm, B_WIDTH), lambda bi, i: (bi, i, col_u)),
            pl.BlockSpec((1, tm, B_WIDTH), lambda bi, i: (bi, i, col_v)),
            pl.BlockSpec((1, B_WIDTH), lambda bi, i: (0, 0)),
            pl.BlockSpec((1, B_WIDTH), lambda bi, i: (0, 0)),
            pl.BlockSpec((B_GROUPS, B_CHUNK, B_CHUNK), lambda bi, i: (0, 0, 0)),
            pl.BlockSpec((B_CHUNK, B_WIDTH), lambda bi, i: (0, 0)),
        ],
        out_specs=pl.BlockSpec((1, tm, B_WIDTH), lambda bi, i: (bi, i, 0)),
        compiler_params=_cparams(("parallel", "parallel")),
        name="sgu",
    )(z, z, ln_g.reshape(1, B_WIDTH), ln_b.reshape(1, B_WIDTH), w_s.astype(BF16), bias)


def _route(logits):
    lane = lax.broadcasted_iota(jnp.int32, (1, ROUTE_W), 1)
    lane_f = lane.astype(F32)
    is_g = lane < N_EXPERT_GROUPS
    lg = jnp.where(is_g, logits, NEG_INF)
    mg = jnp.max(lg, -1, keepdims=True)
    pg_top = 1.0 / jnp.sum(jnp.where(is_g, jnp.exp(lg - mg), 0.0), -1, keepdims=True)
    g_idx = jnp.min(jnp.where(lg == mg, lane_f, float(ROUTE_W)), -1, keepdims=True)
    e_group = ((lane - ROUTE_E0) // EXPERTS_PER_GROUP).astype(F32)
    in_grp = (lane >= ROUTE_E0) & (lane < ROUTE_E0 + N_EXPERTS) & (e_group == g_idx)
    le = jnp.where(in_grp, logits, NEG_INF)
    m1 = jnp.max(le, -1, keepdims=True)
    i1 = jnp.min(jnp.where(le == m1, lane_f, float(ROUTE_W)), -1, keepdims=True)
    le2 = jnp.where(lane_f == i1, NEG_INF, le)
    m2 = jnp.max(le2, -1, keepdims=True)
    i2 = jnp.min(jnp.where(le2 == m2, lane_f, float(ROUTE_W)), -1, keepdims=True)
    t = jnp.exp(m2 - m1)
    w1 = pg_top / (1.0 + t)
    w2 = w1 * t
    picks = (i1 - ROUTE_E0, i2 - ROUTE_E0, w1, w2)
    out = jnp.zeros(logits.shape, F32)
    for k, val in enumerate(picks):
        out = jnp.where(lane == k, val, out)
    return out


def _merge_kernel(x_ref, oa_ref, ob_ref, oc_ref,
                  g0_ref, g1_ref, g2_ref, g3_ref, wa_ref, wb_ref, wc_ref, wo_ref, n2_ref, wr_ref,
                  xo_ref, h2_ref, route_ref, xn0_ref, xn1_ref):
    i = pl.program_id(0)
    n_tiles = pl.num_programs(0) - 1

    def residual_into(xn_ref):
        zg = jnp.concatenate([g0_ref[...], g1_ref[...], g2_ref[...], g3_ref[...]],
                             axis=1).astype(F32)
        gates = jax.nn.sigmoid(zg)
        pa = jnp.dot(oa_ref[...], wa_ref[...], preferred_element_type=F32)
        pb = jnp.dot(ob_ref[...], wb_ref[...], preferred_element_type=F32)
        pc = jnp.dot(oc_ref[...], wc_ref[...], preferred_element_type=F32)
        merged = (gates[:, :D_MODEL] * pa + gates[:, D_MODEL:2 * D_MODEL] * pb
                  + gates[:, 2 * D_MODEL:] * pc)
        xn_ref[...] = x_ref[...] + jnp.dot(merged.astype(BF16), wo_ref[...],
                                           preferred_element_type=F32)

    def tail_from(xn_ref):
        xn = xn_ref[...]
        xo_ref[...] = xn
        ms = jnp.mean(xn * xn, -1, keepdims=True)
        h2 = xn * lax.rsqrt(ms + EPS) * n2_ref[...]
        h2_ref[...] = _pack_bf16_pairs(h2)
        route_ref[...] = _route(jnp.dot(h2.astype(BF16), wr_ref[...], preferred_element_type=F32))

    bufs = (xn0_ref, xn1_ref)
    for parity in range(2):
        @pl.when((i > 0) & (i < n_tiles) & (i % 2 == parity))
        def _(parity=parity):
            tail_from(bufs[1 - parity])
            residual_into(bufs[parity])

        @pl.when((i == n_tiles) & (i % 2 == parity))
        def _(parity=parity):
            tail_from(bufs[1 - parity])

    @pl.when(i == 0)
    def _():
        residual_into(bufs[0])


def _merge(x2, oa, ob, oc, z2, wa, wb, wc, wo, n2g, w_rg, w_re):
    n, d = x2.shape
    tm = 512
    wr = jnp.concatenate(
        [w_rg, w_re, jnp.zeros((d, ROUTE_W - N_EXPERT_GROUPS - N_EXPERTS), w_rg.dtype)], axis=1)
    gate_w = GATE_W // 4
    gate_col0 = COL_G * IN_TN // gate_w

    n_tiles = n // tm
    last = n_tiles - 1

    def rows(width):
        return pl.BlockSpec((tm, width), lambda i: (jnp.minimum(i, last), 0))

    def lagged_rows(width):
        return pl.BlockSpec((tm, width), lambda i: (jnp.maximum(i - 1, 0), 0))

    def full(shape):
        return pl.BlockSpec(shape, lambda i: (0, 0))

    gate_specs = [pl.BlockSpec((tm, gate_w), lambda i, k=k: (jnp.minimum(i, last), gate_col0 + k))
                  for k in range(4)]
    return pl.pallas_call(
        _merge_kernel,
        out_shape=[jax.ShapeDtypeStruct((n, d), F32),
                   jax.ShapeDtypeStruct((n, d // 2), jnp.int32),
                   jax.ShapeDtypeStruct((n, ROUTE_W), F32)],
        grid=(n_tiles + 1,),
        in_specs=[rows(d), rows(A_Q_W), rows(B_WIDTH), rows(C_OUT_W),
                  *gate_specs,
                  full((A_Q_W, d)), full((B_WIDTH, d)), full((C_OUT_W, d)), full((d, d)),
                  full((1, d)), full((d, ROUTE_W))],
        out_specs=[lagged_rows(d), lagged_rows(d // 2), lagged_rows(ROUTE_W)],
        scratch_shapes=[pltpu.VMEM((tm, d), F32), pltpu.VMEM((tm, d), F32)],
        compiler_params=_cparams(("arbitrary",)),
        name="merge",
    )(x2, oa, ob, oc, z2, z2, z2, z2,
      wa.astype(BF16), wb.astype(BF16), wc.astype(BF16), wo.astype(BF16),
      n2g.reshape(1, d), wr.astype(BF16))


MOE_TM = 1024
SC_CORES = 2
SC_SUBCORES = 16
SC_WORKERS = SC_CORES * SC_SUBCORES
SC_CHUNK = 128


def _pack_bf16_pairs(x):
    w = x.shape[1] // 2
    lo = pltpu.bitcast(x[:, :w].astype(BF16).astype(F32), jnp.int32)
    hi = pltpu.bitcast(x[:, w:].astype(BF16).astype(F32), jnp.int32)
    return (hi & jnp.int32(-65536)) | lax.shift_right_logical(lo, jnp.int32(16))


def _unpack_bf16_pairs(p):
    lo = pltpu.bitcast(lax.shift_left(p, jnp.int32(16)), F32)
    hi = pltpu.bitcast(p & jnp.int32(-65536), F32)
    return jnp.concatenate([lo, hi], axis=1)


def _sc_gather_rows(table, idx):
    rows, width = idx.shape[0], table.shape[1]
    per_worker = rows // SC_WORKERS
    n_chunks = per_worker // SC_CHUNK
    assert per_worker * SC_WORKERS == rows and n_chunks * SC_CHUNK == per_worker
    mesh = plsc.VectorSubcoreMesh(core_axis_name="c", subcore_axis_name="s",
                                  num_cores=SC_CORES, num_subcores=SC_SUBCORES)

    @functools.partial(
        pl.kernel, mesh=mesh,
        out_type=jax.ShapeDtypeStruct((rows, width), table.dtype),
        scratch_types=[pltpu.VMEM((n_chunks, SC_CHUNK), jnp.int32),
                       pltpu.VMEM((SC_CHUNK, width), table.dtype),
                       pltpu.SemaphoreType.DMA],
        name="sc_gather_rows",
    )
    def gather(table_hbm, idx_hbm, out_hbm, idx_v, rows_v, sem):
        wid = lax.axis_index("s") * SC_CORES + lax.axis_index("c")
        pltpu.sync_copy(idx_hbm.at[wid], idx_v)
        base = wid * per_worker

        @pl.loop(0, n_chunks)
        def _(c):
            pltpu.async_copy(table_hbm.at[idx_v.at[c]], rows_v, sem).wait()
            pltpu.sync_copy(rows_v, out_hbm.at[pl.ds(base + c * SC_CHUNK, SC_CHUNK)])

    return gather(table, idx.reshape(SC_WORKERS, n_chunks, SC_CHUNK))


def _sc_scatter_rows(rows, pos, n_slots):
    n, width = rows.shape
    per_worker = n // SC_WORKERS
    n_chunks = per_worker // SC_CHUNK
    assert per_worker * SC_WORKERS == n and n_chunks * SC_CHUNK == per_worker
    mesh = plsc.VectorSubcoreMesh(core_axis_name="c", subcore_axis_name="s",
                                  num_cores=SC_CORES, num_subcores=SC_SUBCORES)

    @functools.partial(
        pl.kernel, mesh=mesh,
        out_type=jax.ShapeDtypeStruct((n_slots, width), rows.dtype),
        scratch_types=[pltpu.VMEM((TOP_K, n_chunks, SC_CHUNK), jnp.int32),
                       pltpu.VMEM((SC_CHUNK, width), rows.dtype),
                       pltpu.SemaphoreType.DMA],
        name="sc_scatter_rows",
    )
    def scatter(rows_hbm, pos_hbm, out_hbm, pos_v, rows_v, sem):
        wid = lax.axis_index("s") * SC_CORES + lax.axis_index("c")
        for k in range(TOP_K):
            pltpu.sync_copy(pos_hbm.at[k, wid], pos_v.at[k])
        base = wid * per_worker

        @pl.loop(0, n_chunks)
        def _(c):
            pltpu.sync_copy(rows_hbm.at[pl.ds(base + c * SC_CHUNK, SC_CHUNK)], rows_v)
            for k in range(TOP_K):
                pltpu.async_copy(rows_v, out_hbm.at[pos_v.at[k, c]], sem).wait()

    return scatter(rows, pos.reshape(TOP_K, SC_WORKERS, n_chunks, SC_CHUNK))


def _dispatch_plan(route, n_slots, expert_base):
    n = route.shape[0]
    e = route[:, 0:TOP_K].astype(jnp.int32).reshape(-1)
    experts = jnp.arange(N_EXPERTS, dtype=jnp.int32)
    onehot = (e[:, None] == experts[None, :]).astype(jnp.int32)
    csum = jnp.cumsum(onehot, axis=0)
    rank = jnp.sum((csum - onehot) * onehot, axis=1)
    counts = csum[-1]
    padded = (counts + MOE_TM - 1) // MOE_TM * MOE_TM
    ends = jnp.cumsum(padded)
    starts = ends - padded
    pos = starts[e] + rank
    tile_start = jnp.arange(n_slots // MOE_TM, dtype=jnp.int32) * MOE_TM
    tile_e = jnp.minimum(jnp.sum((tile_start[:, None] >= ends[None, :]).astype(jnp.int32), axis=1),
                         N_EXPERTS - 1)
    tile_rows = jnp.clip(counts[tile_e] - (tile_start - starts[tile_e]), 0, MOE_TM)
    tile_rows = jnp.where(tile_start < ends[-1], tile_rows, 0)
    first = jnp.concatenate([jnp.ones((1,), jnp.int32),
                             (tile_e[1:] != tile_e[:-1]).astype(jnp.int32)])
    n_tiles = n_slots // MOE_TM
    tile_id = tile_start // MOE_TM
    n_used = ends[-1] // MOE_TM
    blocks = jnp.stack([jnp.minimum(tile_id, n_used - 1),
                        jnp.where(tile_id < n_used, tile_id, n_tiles - 1)]).astype(jnp.int32)
    return (pos.reshape(n, TOP_K).T, tile_e + expert_base, tile_rows.astype(jnp.int32), first,
            blocks)


def _moe_tile_kernel(te_ref, tr_ref, tf_ref, tb_ref, xs_ref, wg_ref, wu_ref, wd_ref, ys_ref,
                     wg_s, wu_s, wd_s):
    t = pl.program_id(0)

    @pl.when(tf_ref[t] != 0)
    def _():
        wg_s[...] = wg_ref[0].astype(BF16)
        wu_s[...] = wu_ref[0].astype(BF16)
        wd_s[...] = wd_ref[0].astype(BF16)

    @pl.when(tr_ref[t] != 0)
    def _():
        occupied = lax.broadcasted_iota(jnp.int32, (MOE_TM, 1), 0) < tr_ref[t]
        x = jnp.where(occupied, _unpack_bf16_pairs(xs_ref[...]), 0.0).astype(BF16)
        hg = jnp.dot(x, wg_s[...], preferred_element_type=F32)
        hu = jnp.dot(x, wu_s[...], preferred_element_type=F32)
        a = (jax.nn.silu(hg) * hu).astype(BF16)
        ys_ref[...] = _pack_bf16_pairs(jnp.dot(a, wd_s[...], preferred_element_type=F32))

    @pl.when(tr_ref[t] == 0)
    def _():
        ys_ref[...] = jnp.zeros_like(ys_ref)


def _moe_tiles(xs, tile_expert, tile_rows, tile_first, tile_blocks, w_gate, w_up, w_down):
    n_slots, half = xs.shape
    d = 2 * half
    grid_spec = pltpu.PrefetchScalarGridSpec(
        num_scalar_prefetch=4,
        grid=(n_slots // MOE_TM,),
        in_specs=[
            pl.BlockSpec((MOE_TM, half), lambda t, te, tr, tf, tb: (tb[0, t], 0)),
            pl.BlockSpec((1, d, D_EXPERT), lambda t, te, tr, tf, tb: (te[t], 0, 0)),
            pl.BlockSpec((1, d, D_EXPERT), lambda t, te, tr, tf, tb: (te[t], 0, 0)),
            pl.BlockSpec((1, D_EXPERT, d), lambda t, te, tr, tf, tb: (te[t], 0, 0)),
        ],
        out_specs=pl.BlockSpec((MOE_TM, half), lambda t, te, tr, tf, tb: (tb[1, t], 0)),
        scratch_shapes=[pltpu.VMEM((d, D_EXPERT), BF16), pltpu.VMEM((d, D_EXPERT), BF16),
                        pltpu.VMEM((D_EXPERT, d), BF16)],
    )
    return pl.pallas_call(
        _moe_tile_kernel,
        out_shape=jax.ShapeDtypeStruct((n_slots, half), jnp.int32),
        grid_spec=grid_spec,
        compiler_params=_cparams(("arbitrary",)),
        name="moe_tiles",
    )(tile_expert, tile_rows, tile_first, tile_blocks, xs, w_gate, w_up, w_down)


def _moe_combine_kernel(x_ref, y0_ref, y1_ref, r_ref, g_ref, o_ref, *, final_norm):
    w0 = r_ref[:, 2:3]
    w1 = r_ref[:, 3:4]
    x = x_ref[...] + w0 * _unpack_bf16_pairs(y0_ref[0]) + w1 * _unpack_bf16_pairs(y1_ref[0])
    if final_norm:
        ms = jnp.mean(x * x, -1, keepdims=True)
        x = x * lax.rsqrt(ms + EPS) * g_ref[...]
    o_ref[...] = x


def _moe_combine(x2, yg, route, norm_g, final_norm):
    n, d = x2.shape
    tm = 1024
    return pl.pallas_call(
        functools.partial(_moe_combine_kernel, final_norm=final_norm),
        out_shape=jax.ShapeDtypeStruct((n, d), F32),
        grid=(n // tm,),
        in_specs=[
            pl.BlockSpec((tm, d), lambda i: (i, 0)),
            pl.BlockSpec((1, tm, d // 2), lambda i: (0, i, 0)),
            pl.BlockSpec((1, tm, d // 2), lambda i: (1, i, 0)),
            pl.BlockSpec((tm, ROUTE_W), lambda i: (i, 0)),
            pl.BlockSpec((1, d), lambda i: (0, 0)),
        ],
        out_specs=pl.BlockSpec((tm, d), lambda i: (i, 0)),
        compiler_params=_cparams(("parallel",)),
        name="moe_combine",
    )(x2, yg, yg, route, norm_g.reshape(1, d))


def _moe(h2p, route, x2, layer, w_gate, w_up, w_down, norm_f_g, final_norm):
    n, d = x2.shape
    n_slots = TOP_K * n + N_EXPERTS * MOE_TM
    pos, tile_expert, tile_rows, tile_first, tile_blocks = _dispatch_plan(route, n_slots, layer * N_EXPERTS)
    xs = _sc_scatter_rows(h2p, pos, n_slots)
    ys = _moe_tiles(xs, tile_expert, tile_rows, tile_first, tile_blocks, w_gate, w_up, w_down)
    yg = _sc_gather_rows(ys, pos.reshape(-1)).reshape(TOP_K, n, d // 2)
    return _moe_combine(x2, yg, route, norm_f_g, final_norm)


def kernel(x, positions, norm1_g, w_in, b_in, attn_sinks, sgu_ln_g, sgu_ln_b, w_spatial, b_spatial,
           w_proj_a, w_proj_b, w_proj_c, w_out, norm2_g, w_router_group, w_router_expert,
           w_expert_gate, w_expert_up, w_expert_down, norm_f_g):
    bsz, s, d = x.shape
    depth = w_in.shape[0]
    wg_all = w_expert_gate.reshape(depth * N_EXPERTS, d, D_EXPERT)
    wu_all = w_expert_up.reshape(depth * N_EXPERTS, d, D_EXPERT)
    wd_all = w_expert_down.reshape(depth * N_EXPERTS, D_EXPERT, d)
    assert d == D_MODEL and s % (C_PAIRS[-1][1] * BLOCK) == 0
    tabs = _rope_tables(positions)
    for l in range(depth):
        z = _in_proj(x, norm1_g[l], w_in[l].astype(BF16), b_in[l], tabs)
        oa = _attn_a(z, attn_sinks[l])
        ob = _sgu(z, sgu_ln_g[l], sgu_ln_b[l], w_spatial[l], b_spatial[l])
        oc = _attn_c(z)
        x2, h2, route = _merge(
            x.reshape(bsz * s, d), oa.reshape(bsz * s, A_Q_W), ob.reshape(bsz * s, B_WIDTH),
            oc.reshape(bsz * s, C_OUT_W), z.reshape(bsz * s, D_IN), w_proj_a[l], w_proj_b[l], w_proj_c[l], w_out[l],
            norm2_g[l], w_router_group[l], w_router_expert[l])
        x = _moe(h2, route, x2, l, wg_all, wu_all, wd_all, norm_f_g, l == depth - 1).reshape(bsz, s, d)
    return x
```

```python
import functools

import jax
import jax.numpy as jnp
from jax import lax
from jax.experimental import pallas as pl
from jax.experimental.pallas import tpu as pltpu
from jax.experimental.pallas import tpu_sc as plsc

F32 = jnp.float32
BF16 = jnp.bfloat16

D_MODEL = 1024
HEAD_DIM = 64
ROT_DIM = HEAD_DIM // 4
ROPE_THETA = 500000.0
BLOCK = 128
EPS = 1e-5
NEG_INF = -1e30

A_Q_HEADS = 16
A_KV_HEADS = 4
A_REP = A_Q_HEADS // A_KV_HEADS
A_WINDOW = 128
A_Q_W = A_Q_HEADS * HEAD_DIM
A_KV_W = A_KV_HEADS * HEAD_DIM

B_GROUPS = 12
B_CH = 64
B_WIDTH = B_GROUPS * B_CH
B_CHUNK = 128

C_PAIRS = ((128, 1), (512, 4), (2048, 16))
C_HEADS_PER_GROUP = 4
C_HEADS = C_HEADS_PER_GROUP * len(C_PAIRS)
C_OUT_W = C_HEADS_PER_GROUP * HEAD_DIM

N_BRANCH = 3
GATE_W = N_BRANCH * D_MODEL
D_IN = A_Q_W + 2 * A_KV_W + 2 * B_WIDTH + 3 * C_HEADS * HEAD_DIM + GATE_W

N_EXPERT_GROUPS = 4
EXPERTS_PER_GROUP = 8
N_EXPERTS = N_EXPERT_GROUPS * EXPERTS_PER_GROUP
TOP_K = 2
D_EXPERT = 256

LANES = 128
MXU_W = 256
VMEM_LIMIT = 56 * 1024 * 1024

IN_TN = 256
COL_AQ = 0
COL_AK = A_Q_W // IN_TN
COL_AV = COL_AK + A_KV_W // IN_TN
COL_BU = COL_AV + A_KV_W // IN_TN
COL_BV = COL_BU + B_WIDTH // IN_TN
COL_CQ = COL_BV + B_WIDTH // IN_TN
COL_CK = COL_CQ + C_HEADS * HEAD_DIM // IN_TN
COL_CV = COL_CK + C_HEADS * HEAD_DIM // IN_TN
COL_G = COL_CV + C_HEADS * HEAD_DIM // IN_TN
N_COL_TILES = D_IN // IN_TN

ROUTE_W = LANES
ROUTE_E0 = N_EXPERT_GROUPS
ROUTE_RANK0 = 4
SUBLANES = 8


def _cparams(sem):
    return pltpu.CompilerParams(dimension_semantics=sem, vmem_limit_bytes=VMEM_LIMIT)


def _rope_table_kernel(pos_ref, inv_ref, c_ref, s1_ref, s2_ref):
    lane = lax.broadcasted_iota(jnp.int32, (1, LANES), 1)
    d = lane & (HEAD_DIM - 1)
    ang = pos_ref[...].astype(F32) * inv_ref[...]
    c = jnp.cos(ang)
    s = jnp.sin(ang)
    half = ROT_DIM // 2
    c_ref[...] = jnp.where(d < ROT_DIM, c, 1.0)
    s1_ref[...] = jnp.where(d < half, -s, 0.0)
    s2_ref[...] = jnp.where((d >= half) & (d < ROT_DIM), s, 0.0)


def _rope_tables(positions):
    n = positions.size
    inv = ROPE_THETA ** (-jnp.arange(0, ROT_DIM, 2, dtype=F32) / ROT_DIM)
    inv_lane = jnp.tile(inv, LANES // inv.shape[0]).reshape(1, LANES)
    pos_b = jnp.broadcast_to(positions.reshape(n, 1), (n, LANES))
    tm = 1024
    spec = pl.BlockSpec((tm, LANES), lambda i: (i, 0))
    return pl.pallas_call(
        _rope_table_kernel,
        out_shape=[jax.ShapeDtypeStruct((n, LANES), F32)] * 3,
        grid=(n // tm,),
        in_specs=[spec, pl.BlockSpec((1, LANES), lambda i: (0, 0))],
        out_specs=[spec, spec, spec],
        compiler_params=_cparams(("parallel",)),
        name="rope_tables",
    )(pos_b, inv_lane)


IN_STEP_TILES = 3
IN_STEP_W = IN_STEP_TILES * IN_TN
IN_STEPS = N_COL_TILES // IN_STEP_TILES


def _is_rope_tile(tile):
    return tile < COL_AV or COL_CQ <= tile < COL_CV


_STEP_ROPES = tuple(tuple(_is_rope_tile(step * IN_STEP_TILES + u) for u in range(IN_STEP_TILES))
                    for step in range(IN_STEPS))


def _in_proj_kernel(x_ref, g_ref, w_ref, b_ref, c_ref, s1_ref, s2_ref, z_ref, h_ref,
                    acc0_ref, acc1_ref):
    j = pl.program_id(1)

    @pl.when(j == 0)
    def _():
        x = x_ref[0]
        ms = jnp.mean(x * x, -1, keepdims=True)
        h_ref[...] = (x * lax.rsqrt(ms + EPS) * g_ref[...]).astype(BF16)

    prev = j - 1

    def matmul_into(acc_ref):
        acc_ref[...] = jnp.dot(h_ref[...], w_ref[...], preferred_element_type=F32) + b_ref[...]

    def finish(acc_ref, ropes):
        for u, rope in enumerate(ropes):
            cols = slice(u * IN_TN, (u + 1) * IN_TN)
            if not rope:
                z_ref[0, :, cols] = acc_ref[:, cols].astype(BF16)
                continue
            tile = prev * IN_STEP_TILES + u
            is_q = (tile < COL_AK) | ((tile >= COL_CQ) & (tile < COL_CK))
            scale = jnp.where(is_q, HEAD_DIM ** -0.5, 1.0).astype(F32)
            for t in range(u * IN_TN // LANES, (u + 1) * IN_TN // LANES):
                lanes = slice(t * LANES, (t + 1) * LANES)
                a = acc_ref[:, lanes]
                r = (a * c_ref[...] + pltpu.roll(a, LANES - ROT_DIM // 2, 1) * s1_ref[...]
                     + pltpu.roll(a, ROT_DIM // 2, 1) * s2_ref[...])
                z_ref[0, :, lanes] = (r * scale).astype(BF16)

    accs = (acc0_ref, acc1_ref)
    patterns = sorted(set(_STEP_ROPES))
    pattern_id = jnp.int32(0)
    for step, ropes in enumerate(_STEP_ROPES):
        pattern_id = jnp.where(prev == step, patterns.index(ropes), pattern_id)
    inner = (j > 0) & (j < IN_STEPS)
    for parity in range(2):
        for pid, ropes in enumerate(patterns):
            @pl.when(inner & (j % 2 == parity) & (pattern_id == pid))
            def _(parity=parity, ropes=ropes):
                matmul_into(accs[parity])
                finish(accs[1 - parity], ropes)

    @pl.when(j == 0)
    def _():
        matmul_into(accs[0])

    @pl.when(j == IN_STEPS)
    def _():
        finish(accs[(IN_STEPS - 1) % 2], _STEP_ROPES[-1])


def _in_proj(x, g, w, b, tabs):
    bsz, s, d = x.shape
    c, s1, s2 = tabs
    tab_spec = pl.BlockSpec((s, LANES), lambda bi, j: (bi, 0))
    last = IN_STEPS - 1
    return pl.pallas_call(
        _in_proj_kernel,
        out_shape=jax.ShapeDtypeStruct((bsz, s, D_IN), BF16),
        grid=(bsz, IN_STEPS + 1),
        in_specs=[
            pl.BlockSpec((1, s, d), lambda bi, j: (bi, 0, 0)),
            pl.BlockSpec((1, d), lambda bi, j: (0, 0)),
            pl.BlockSpec((d, IN_STEP_W), lambda bi, j: (0, jnp.minimum(j, last))),
            pl.BlockSpec((1, IN_STEP_W), lambda bi, j: (0, jnp.minimum(j, last))),
            tab_spec, tab_spec, tab_spec,
        ],
        out_specs=pl.BlockSpec((1, s, IN_STEP_W), lambda bi, j: (bi, 0, jnp.maximum(j - 1, 0))),
        scratch_shapes=[pltpu.VMEM((s, d), BF16), pltpu.VMEM((s, IN_STEP_W), F32),
                        pltpu.VMEM((s, IN_STEP_W), F32)],
        compiler_params=_cparams(("parallel", "arbitrary")),
        name="in_proj",
    )(x, g.reshape(1, d), w, b.reshape(1, D_IN), c, s1, s2)


STACK = C_HEADS_PER_GROUP * BLOCK
STEP_UNROLL = 4


def _head_stack(blk):
    seg = lax.broadcasted_iota(jnp.int32, (1, MXU_W), 1) // HEAD_DIM
    rowseg = lax.broadcasted_iota(jnp.int32, (STACK, 1), 0) // BLOCK
    return jnp.where(seg == rowseg, jnp.concatenate([blk] * C_HEADS_PER_GROUP, axis=0),
                     jnp.zeros((), blk.dtype))


def _band_step(q, kp_ref, vp_ref, blk, n, has_prev, diag_key, sink_col, want_lse):
    return _band_blocks([(q, blk, n)], kp_ref, vp_ref, has_prev, diag_key, sink_col, want_lse)[0]


def _band_blocks(jobs, kp_ref, vp_ref, has_prev, diag_key, sink_col, want_lse):
    contract = (((1,), (1,)), ((), ()))
    i_loc = lax.broadcasted_iota(jnp.int32, (STACK, 1), 0) & (BLOCK - 1)
    jj = lax.broadcasted_iota(jnp.int32, (1, BLOCK), 1)
    upper = jj > i_loc
    seg = lax.broadcasted_iota(jnp.int32, (1, MXU_W), 1) // HEAD_DIM
    zero_b = jnp.zeros((), BF16)

    def key_rows(blk):
        lo = pl.multiple_of(blk * BLOCK, BLOCK)
        if has_prev:
            return pl.ds(lo, 2 * BLOCK)
        return pl.ds(pl.multiple_of(lo + BLOCK, BLOCK), BLOCK)

    scores = [lax.dot_general(_head_stack(q), kp_ref[key_rows(blk), :], contract,
                              preferred_element_type=F32) for q, blk, _ in jobs]

    def own_lanes(x):
        out = x[0:BLOCK]
        for h in range(1, C_HEADS_PER_GROUP):
            out = jnp.where(seg == h, x[h * BLOCK:(h + 1) * BLOCK], out)
        return out

    probs, stats = [], []
    for s, (_, blk, n) in zip(scores, jobs):
        if has_prev:
            sp = s[:, :BLOCK] + jnp.where(n == 0, NEG_INF, 0.0).astype(F32)
            f = jnp.where(upper, sp, s[:, BLOCK:])
        else:
            f = jnp.where(upper, NEG_INF, s)
        m = jnp.max(f, -1, keepdims=True)
        if diag_key:
            sd = jnp.sum(jnp.where(jj == i_loc, sp, 0.0), -1, keepdims=True)
            m = jnp.maximum(m, sd)
        if sink_col is not None:
            m = jnp.maximum(m, sink_col)
        p = jnp.exp(f - m)
        den = jnp.sum(p, -1, keepdims=True)
        pd = None
        if diag_key:
            pd = jnp.exp(sd - m)
            den = den + pd
        if sink_col is not None:
            den = den + jnp.exp(sink_col - m)
        inv = 1.0 / den
        pn = (p * inv).astype(BF16)
        if has_prev:
            below = zero_b
            if diag_key:
                below = jnp.where(jj == i_loc, pd * inv, 0.0).astype(BF16)
            pn = jnp.concatenate([jnp.where(upper, pn, below), jnp.where(upper, zero_b, pn)],
                                 axis=1)
        probs.append(pn)
        stats.append((m, den))

    outs = []
    for pn, (m, den), (_, blk, _) in zip(probs, stats, jobs):
        o = own_lanes(jnp.dot(pn, vp_ref[key_rows(blk), :], preferred_element_type=F32))
        lse = own_lanes(m + jnp.log(den)) + jnp.zeros((BLOCK, MXU_W), F32) if want_lse else None
        outs.append((o, lse))
    return outs


def _replicate_head(blk, g):
    words = pltpu.bitcast(blk, jnp.uint32)
    tile = words[:, (g // 2) * LANES:(g // 2 + 1) * LANES]
    swapped = pltpu.roll(tile, HEAD_DIM, 1)
    low = lax.broadcasted_iota(jnp.int32, (1, LANES), 1) < HEAD_DIM
    both = jnp.where(low, tile, swapped) if g % 2 == 0 else jnp.where(low, swapped, tile)
    return pltpu.bitcast(jnp.concatenate([both, both], axis=1), BF16)


def _attn_a_kernel(sink_ref, q_ref, k_ref, v_ref, o_ref, kp_ref, vp_ref):
    nb = q_ref.shape[1] // BLOCK
    rowseg = lax.broadcasted_iota(jnp.int32, (STACK, 1), 0) // BLOCK
    kp_ref[pl.ds(0, BLOCK), :] = jnp.zeros((BLOCK, MXU_W), BF16)
    vp_ref[pl.ds(0, BLOCK), :] = jnp.zeros((BLOCK, MXU_W), BF16)
    for g in range(A_KV_HEADS):
        kv_cols = slice(g * HEAD_DIM, (g + 1) * HEAD_DIM)
        q_cols = slice(g * MXU_W, (g + 1) * MXU_W)

        def build(n, carry):
            rows = pl.ds(pl.multiple_of(n * BLOCK, BLOCK), BLOCK)
            dst = pl.ds(pl.multiple_of((n + 1) * BLOCK, BLOCK), BLOCK)
            kp_ref[dst, :] = _replicate_head(k_ref[0, rows, :], g)
            vp_ref[dst, :] = _replicate_head(v_ref[0, rows, :], g)
            return carry

        lax.fori_loop(0, nb, build, 0, unroll=8)
        sink_col = jnp.zeros((STACK, 1), F32)
        for r in range(A_REP):
            sink_col = jnp.where(rowseg == r, sink_ref[g * A_REP + r], sink_col)

        def step(it, carry):
            blocks = [it * STEP_UNROLL + u for u in range(STEP_UNROLL)]
            rows = [pl.ds(pl.multiple_of(n * BLOCK, BLOCK), BLOCK) for n in blocks]
            jobs = [(q_ref[0, r, q_cols], n, n) for r, n in zip(rows, blocks)]
            outs = _band_blocks(jobs, kp_ref, vp_ref, True, False, sink_col, False)
            for r, (o, _) in zip(rows, outs):
                o_ref[0, r, q_cols] = o.astype(o_ref.dtype)
            return carry

        lax.fori_loop(0, nb // STEP_UNROLL, step, 0)


def _attn_a(z, sinks):
    bsz, s, _ = z.shape
    nb = s // BLOCK
    grid_spec = pltpu.PrefetchScalarGridSpec(
        num_scalar_prefetch=1,
        grid=(bsz,),
        in_specs=[
            pl.BlockSpec((1, s, A_Q_W), lambda bi, sk: (bi, 0, 0)),
            pl.BlockSpec((1, s, A_KV_W), lambda bi, sk: (bi, 0, COL_AK)),
            pl.BlockSpec((1, s, A_KV_W), lambda bi, sk: (bi, 0, COL_AV)),
        ],
        out_specs=pl.BlockSpec((1, s, A_Q_W), lambda bi, sk: (bi, 0, 0)),
        scratch_shapes=[pltpu.VMEM(((nb + 1) * BLOCK, MXU_W), BF16),
                        pltpu.VMEM(((nb + 1) * BLOCK, MXU_W), BF16)],
    )
    return pl.pallas_call(
        _attn_a_kernel,
        out_shape=jax.ShapeDtypeStruct((bsz, s, A_Q_W), BF16),
        grid_spec=grid_spec,
        compiler_params=_cparams(("parallel",)),
        name="attn_a",
    )(sinks, z, z, z)


def _attn_c_kernel(q_ref, k_ref, v_ref, oc_ref, stage_ref, pq_ref, pk_ref, pv_ref, og_ref, lg_ref):
    s = q_ref.shape[1]
    nblk = s // BLOCK
    halves = MXU_W // LANES
    pk_ref[pl.ds(0, BLOCK), :] = jnp.zeros((BLOCK, MXU_W), BF16)
    pv_ref[pl.ds(0, BLOCK), :] = jnp.zeros((BLOCK, MXU_W), BF16)

    def class_rows(idx, nb, d):
        r = idx // nb
        n = idx - r * nb
        start = r + n * (BLOCK * d)
        rows = pl.ds(start, BLOCK, stride=d) if d > 1 else pl.ds(pl.multiple_of(start, BLOCK), BLOCK)
        return n, rows

    def stage(src_ref, cols):
        x = src_ref[0, :, cols].astype(F32)
        for t in range(halves):
            stage_ref[t] = x[:, t * LANES:(t + 1) * LANES]

    def staged_block(rows):
        return jnp.concatenate([stage_ref[t, rows, :] for t in range(halves)], axis=1).astype(BF16)

    for g, (win, d) in enumerate(C_PAIRS):
        cols = slice(g * MXU_W, (g + 1) * MXU_W)
        nb = nblk // d
        has_prev = nb > 1
        assert win // d == BLOCK

        def class_major(dst_ref, offset):
            def build(idx, carry):
                _, rows = class_rows(idx, nb, d)
                dst = pl.ds(pl.multiple_of((idx + offset) * BLOCK, BLOCK), BLOCK)
                dst_ref[dst, :] = staged_block(rows)
                return carry
            return build

        for src_ref, dst_ref, offset in ((k_ref, pk_ref, 1), (v_ref, pv_ref, 1), (q_ref, pq_ref, 0)):
            stage(src_ref, cols)
            lax.fori_loop(0, nblk, class_major(dst_ref, offset), 0, unroll=4)

        def step(it, carry):
            jobs, dsts = [], []
            for u in range(STEP_UNROLL):
                idx = it * STEP_UNROLL + u
                n, rows = class_rows(idx, nb, d)
                here = pl.ds(pl.multiple_of(idx * BLOCK, BLOCK), BLOCK)
                jobs.append((pq_ref[here, :], idx, n))
                dsts.append(rows)
            outs = _band_blocks(jobs, pk_ref, pv_ref, has_prev, has_prev, None, True)
            for rows, (o, lse) in zip(dsts, outs):
                for t in range(halves):
                    og_ref[g, t, rows, :] = o[:, t * LANES:(t + 1) * LANES]
                    lg_ref[g, t, rows, :] = lse[:, t * LANES:(t + 1) * LANES]
            return carry

        lax.fori_loop(0, nblk // STEP_UNROLL, step, 0)

    chunk = 2 * BLOCK

    def combine(i, carry):
        rows = pl.ds(pl.multiple_of(i * chunk, chunk), chunk)
        for t in range(halves):
            ls = [lg_ref[g, t, rows, :] for g in range(len(C_PAIRS))]
            m = jnp.maximum(jnp.maximum(ls[0], ls[1]), ls[2])
            es = [jnp.exp(l - m) for l in ls]
            num = (es[0] * og_ref[0, t, rows, :] + es[1] * og_ref[1, t, rows, :]
                   + es[2] * og_ref[2, t, rows, :])
            oc_ref[0, rows, t * LANES:(t + 1) * LANES] = (
                num / (es[0] + es[1] + es[2])).astype(oc_ref.dtype)
        return carry

    lax.fori_loop(0, s // chunk, combine, 0)


def _attn_c(z):
    bsz, s, _ = z.shape
    nblk = s // BLOCK
    width = C_HEADS * HEAD_DIM
    halves = MXU_W // LANES

    def in_spec(col):
        return pl.BlockSpec((1, s, width), lambda bi: (bi, 0, col * IN_TN // width))

    return pl.pallas_call(
        _attn_c_kernel,
        out_shape=jax.ShapeDtypeStruct((bsz, s, C_OUT_W), BF16),
        grid=(bsz,),
        in_specs=[in_spec(COL_CQ), in_spec(COL_CK), in_spec(COL_CV)],
        out_specs=pl.BlockSpec((1, s, C_OUT_W), lambda bi: (bi, 0, 0)),
        scratch_shapes=[
            pltpu.VMEM((halves, s, LANES), F32),
            pltpu.VMEM((s, MXU_W), BF16),
            pltpu.VMEM((s + BLOCK, MXU_W), BF16),
            pltpu.VMEM((s + BLOCK, MXU_W), BF16),
            pltpu.VMEM((len(C_PAIRS), halves, s, LANES), F32),
            pltpu.VMEM((len(C_PAIRS), halves, s, LANES), F32),
        ],
        compiler_params=_cparams(("parallel",)),
        name="attn_c",
    )(z, z, z)


def _sgu_kernel(zu_ref, zv_ref, lng_ref, lnb_ref, ws_ref, bs_ref, o_ref):
    tm = zu_ref.shape[1]
    u = jax.nn.gelu(zu_ref[0].astype(F32))
    v = jax.nn.gelu(zv_ref[0].astype(F32))
    mu = jnp.mean(v, -1, keepdims=True)
    var = jnp.mean(jnp.square(v - mu), -1, keepdims=True)
    vn = ((v - mu) * lax.rsqrt(var + EPS) * lng_ref[...] + lnb_ref[...]).astype(BF16)
    causal = (lax.broadcasted_iota(jnp.int32, (B_CHUNK, 1), 0)
              >= lax.broadcasted_iota(jnp.int32, (1, B_CHUNK), 1))
    first = lax.broadcasted_iota(jnp.int32, (1, LANES), 1) < B_CH
    zero = jnp.zeros((), BF16)
    for p in range(B_WIDTH // LANES):
        w0 = jnp.where(causal, ws_ref[2 * p], zero)
        w1 = jnp.where(causal, ws_ref[2 * p + 1], zero)
        cols = slice(p * LANES, (p + 1) * LANES)
        for c in range(tm // B_CHUNK):
            rows = slice(c * B_CHUNK, (c + 1) * B_CHUNK)
            vv = vn[rows, cols]
            sv = jnp.where(first,
                           jnp.dot(w0, vv, preferred_element_type=F32),
                           jnp.dot(w1, vv, preferred_element_type=F32)) + bs_ref[:, cols]
            o_ref[0, rows, cols] = (u[rows, cols] * sv).astype(o_ref.dtype)


def _sgu(z, ln_g, ln_b, w_s, b_s):
    bsz, s, _ = z.shape
    tm = 512
    bias = jnp.repeat(b_s.T, B_CH, axis=1)
    col_u = COL_BU * IN_TN // B_WIDTH
    col_v = COL_BV * IN_TN // B_WIDTH
    return pl.pallas_call(
        _sgu_kernel,
        out_shape=jax.ShapeDtypeStruct((bsz, s, B_WIDTH), BF16),
        grid=(bsz, s // tm),
        in_specs=[
            pl.BlockSpec((1, tm, B_WIDTH), lambda bi, i: (bi, i, col_u)),
            pl.BlockSpec((1, tm, B_WIDTH), lambda bi, i: (bi, i, col_v)),
            pl.BlockSpec((1, B_WIDTH), lambda bi, i: (0, 0)),
            pl.BlockSpec((1, B_WIDTH), lambda bi, i: (0, 0)),
            pl.BlockSpec((B_GROUPS, B_CHUNK, B_CHUNK), lambda bi, i: (0, 0, 0)),
            pl.BlockSpec((B_CHUNK, B_WIDTH), lambda bi, i: (0, 0)),
        ],
        out_specs=pl.BlockSpec((1, tm, B_WIDTH), lambda bi, i: (bi, i, 0)),
        compiler_params=_cparams(("parallel", "parallel")),
        name="sgu",
    )(z, z, ln_g.reshape(1, B_WIDTH), ln_b.reshape(1, B_WIDTH), w_s.astype(BF16), bias)


def _route(logits):
    lane = lax.broadcasted_iota(jnp.int32, (1, ROUTE_W), 1)
    lane_f = lane.astype(F32)
    is_g = lane < N_EXPERT_GROUPS
    lg = jnp.where(is_g, logits, NEG_INF)
    mg = jnp.max(lg, -1, keepdims=True)
    pg_top = 1.0 / jnp.sum(jnp.where(is_g, jnp.exp(lg - mg), 0.0), -1, keepdims=True)
    g_idx = jnp.min(jnp.where(lg == mg, lane_f, float(ROUTE_W)), -1, keepdims=True)
    e_group = ((lane - ROUTE_E0) // EXPERTS_PER_GROUP).astype(F32)
    in_grp = (lane >= ROUTE_E0) & (lane < ROUTE_E0 + N_EXPERTS) & (e_group == g_idx)
    le = jnp.where(in_grp, logits, NEG_INF)
    m1 = jnp.max(le, -1, keepdims=True)
    i1 = jnp.min(jnp.where(le == m1, lane_f, float(ROUTE_W)), -1, keepdims=True)
    le2 = jnp.where(lane_f == i1, NEG_INF, le)
    m2 = jnp.max(le2, -1, keepdims=True)
    i2 = jnp.min(jnp.where(le2 == m2, lane_f, float(ROUTE_W)), -1, keepdims=True)
    t = jnp.exp(m2 - m1)
    w1 = pg_top / (1.0 + t)
    w2 = w1 * t
    picks = (i1 - ROUTE_E0, i2 - ROUTE_E0, w1, w2)
    out = jnp.zeros(logits.shape, F32)
    for k, val in enumerate(picks):
        out = jnp.where(lane == k, val, out)
    return out


def _merge_kernel(x_ref, oa_ref, ob_ref, oc_ref,
                  g0_ref, g1_ref, g2_ref, g3_ref, wa_ref, wb_ref, wc_ref, wo_ref, n2_ref, wr_ref,
                  xo_ref, h2_ref, route_ref, counts_ref, xn0_ref, xn1_ref, cnt_ref):
    i = pl.program_id(0)
    n_tiles = pl.num_programs(0) - 1

    def residual_into(xn_ref):
        zg = jnp.concatenate([g0_ref[...], g1_ref[...], g2_ref[...], g3_ref[...]],
                             axis=1).astype(F32)
        gates = jax.nn.sigmoid(zg)
        pa = jnp.dot(oa_ref[...], wa_ref[...], preferred_element_type=F32)
        pb = jnp.dot(ob_ref[...], wb_ref[...], preferred_element_type=F32)
        pc = jnp.dot(oc_ref[...], wc_ref[...], preferred_element_type=F32)
        merged = (gates[:, :D_MODEL] * pa + gates[:, D_MODEL:2 * D_MODEL] * pb
                  + gates[:, 2 * D_MODEL:] * pc)
        xn_ref[...] = x_ref[...] + jnp.dot(merged.astype(BF16), wo_ref[...],
                                           preferred_element_type=F32)

    def tail_from(xn_ref):
        xn = xn_ref[...]
        xo_ref[...] = xn
        ms = jnp.mean(xn * xn, -1, keepdims=True)
        h2 = xn * lax.rsqrt(ms + EPS) * n2_ref[...]
        h2_ref[...] = _pack_bf16_pairs(h2)
        route = _route(jnp.dot(h2.astype(BF16), wr_ref[...], preferred_element_type=F32))
        tm = route.shape[0]
        lane = lax.broadcasted_iota(jnp.int32, (1, ROUTE_W), 1)
        lane_f = lane.astype(F32)
        row = lax.broadcasted_iota(jnp.int32, (tm, 1), 0)
        hit = [lane_f == route[:, k:k + 1] for k in range(TOP_K)]
        onehot = jnp.where(hit[0], 1.0, 0.0) + jnp.where(hit[1], 1.0, 0.0)
        scan = onehot
        shift = 1
        while shift < tm:
            scan = scan + jnp.where(row >= shift, pltpu.roll(scan, shift, 0), 0.0)
            shift *= 2
        before = scan - onehot + cnt_ref[0:1, :]
        for k in range(TOP_K):
            rank = jnp.sum(jnp.where(hit[k], before, 0.0), -1, keepdims=True)
            route = jnp.where(lane == ROUTE_RANK0 + k, rank, route)
        route_ref[...] = route
        cnt = cnt_ref[0:1, :] + jnp.sum(onehot, axis=0, keepdims=True)
        cnt_ref[...] = jnp.broadcast_to(cnt, cnt_ref.shape)
        counts_ref[...] = jnp.broadcast_to(cnt, counts_ref.shape)

    bufs = (xn0_ref, xn1_ref)
    for parity in range(2):
        @pl.when((i > 0) & (i < n_tiles) & (i % 2 == parity))
        def _(parity=parity):
            tail_from(bufs[1 - parity])
            residual_into(bufs[parity])

        @pl.when((i == n_tiles) & (i % 2 == parity))
        def _(parity=parity):
            tail_from(bufs[1 - parity])

    @pl.when(i == 0)
    def _():
        cnt_ref[...] = jnp.zeros_like(cnt_ref)
        residual_into(bufs[0])


def _merge(x2, oa, ob, oc, z2, wa, wb, wc, wo, n2g, w_rg, w_re):
    n, d = x2.shape
    tm = 512
    wr = jnp.concatenate(
        [w_rg, w_re, jnp.zeros((d, ROUTE_W - N_EXPERT_GROUPS - N_EXPERTS), w_rg.dtype)], axis=1)
    gate_w = GATE_W // 4
    gate_col0 = COL_G * IN_TN // gate_w

    n_tiles = n // tm
    last = n_tiles - 1

    def rows(width):
        return pl.BlockSpec((tm, width), lambda i: (jnp.minimum(i, last), 0))

    def lagged_rows(width):
        return pl.BlockSpec((tm, width), lambda i: (jnp.maximum(i - 1, 0), 0))

    def full(shape):
        return pl.BlockSpec(shape, lambda i: (0, 0))

    gate_specs = [pl.BlockSpec((tm, gate_w), lambda i, k=k: (jnp.minimum(i, last), gate_col0 + k))
                  for k in range(4)]
    return pl.pallas_call(
        _merge_kernel,
        out_shape=[jax.ShapeDtypeStruct((n, d), F32),
                   jax.ShapeDtypeStruct((n, d // 2), jnp.int32),
                   jax.ShapeDtypeStruct((n, ROUTE_W), F32),
                   jax.ShapeDtypeStruct((SUBLANES, ROUTE_W), F32)],
        grid=(n_tiles + 1,),
        in_specs=[rows(d), rows(A_Q_W), rows(B_WIDTH), rows(C_OUT_W),
                  *gate_specs,
                  full((A_Q_W, d)), full((B_WIDTH, d)), full((C_OUT_W, d)), full((d, d)),
                  full((1, d)), full((d, ROUTE_W))],
        out_specs=[lagged_rows(d), lagged_rows(d // 2), lagged_rows(ROUTE_W),
                   full((SUBLANES, ROUTE_W))],
        scratch_shapes=[pltpu.VMEM((tm, d), F32), pltpu.VMEM((tm, d), F32),
                        pltpu.VMEM((SUBLANES, ROUTE_W), F32)],
        compiler_params=_cparams(("arbitrary",)),
        name="merge",
    )(x2, oa, ob, oc, z2, z2, z2, z2,
      wa.astype(BF16), wb.astype(BF16), wc.astype(BF16), wo.astype(BF16),
      n2g.reshape(1, d), wr.astype(BF16))


MOE_TM = 1024
SC_CORES = 2
SC_SUBCORES = 16
SC_WORKERS = SC_CORES * SC_SUBCORES
SC_CHUNK = 128


def _pack_bf16_pairs(x):
    w = x.shape[1] // 2
    lo = pltpu.bitcast(x[:, :w].astype(BF16).astype(F32), jnp.int32)
    hi = pltpu.bitcast(x[:, w:].astype(BF16).astype(F32), jnp.int32)
    return (hi & jnp.int32(-65536)) | lax.shift_right_logical(lo, jnp.int32(16))


def _unpack_bf16_pairs(p):
    lo = pltpu.bitcast(lax.shift_left(p, jnp.int32(16)), F32)
    hi = pltpu.bitcast(p & jnp.int32(-65536), F32)
    return jnp.concatenate([lo, hi], axis=1)


def _sc_gather_rows(table, idx):
    rows, width = idx.shape[0], table.shape[1]
    per_worker = rows // SC_WORKERS
    n_chunks = per_worker // SC_CHUNK
    assert per_worker * SC_WORKERS == rows and n_chunks * SC_CHUNK == per_worker
    mesh = plsc.VectorSubcoreMesh(core_axis_name="c", subcore_axis_name="s",
                                  num_cores=SC_CORES, num_subcores=SC_SUBCORES)

    @functools.partial(
        pl.kernel, mesh=mesh,
        out_type=jax.ShapeDtypeStruct((rows, width), table.dtype),
        scratch_types=[pltpu.VMEM((n_chunks, SC_CHUNK), jnp.int32),
                       pltpu.VMEM((SC_CHUNK, width), table.dtype),
                       pltpu.SemaphoreType.DMA],
        name="sc_gather_rows",
    )
    def gather(table_hbm, idx_hbm, out_hbm, idx_v, rows_v, sem):
        wid = lax.axis_index("s") * SC_CORES + lax.axis_index("c")
        pltpu.sync_copy(idx_hbm.at[wid], idx_v)
        base = wid * per_worker

        @pl.loop(0, n_chunks)
        def _(c):
            pltpu.async_copy(table_hbm.at[idx_v.at[c]], rows_v, sem).wait()
            pltpu.sync_copy(rows_v, out_hbm.at[pl.ds(base + c * SC_CHUNK, SC_CHUNK)])

    return gather(table, idx.reshape(SC_WORKERS, n_chunks, SC_CHUNK))


def _sc_scatter_rows(rows, pos, n_slots):
    n, width = rows.shape
    per_worker = n // SC_WORKERS
    n_chunks = per_worker // SC_CHUNK
    assert per_worker * SC_WORKERS == n and n_chunks * SC_CHUNK == per_worker
    mesh = plsc.VectorSubcoreMesh(core_axis_name="c", subcore_axis_name="s",
                                  num_cores=SC_CORES, num_subcores=SC_SUBCORES)

    @functools.partial(
        pl.kernel, mesh=mesh,
        out_type=jax.ShapeDtypeStruct((n_slots, width), rows.dtype),
        scratch_types=[pltpu.VMEM((TOP_K, n_chunks, SC_CHUNK), jnp.int32),
                       pltpu.VMEM((SC_CHUNK, width), rows.dtype),
                       pltpu.SemaphoreType.DMA],
        name="sc_scatter_rows",
    )
    def scatter(rows_hbm, pos_hbm, out_hbm, pos_v, rows_v, sem):
        wid = lax.axis_index("s") * SC_CORES + lax.axis_index("c")
        for k in range(TOP_K):
            pltpu.sync_copy(pos_hbm.at[k, wid], pos_v.at[k])
        base = wid * per_worker

        @pl.loop(0, n_chunks)
        def _(c):
            pltpu.sync_copy(rows_hbm.at[pl.ds(base + c * SC_CHUNK, SC_CHUNK)], rows_v)
            for k in range(TOP_K):
                pltpu.async_copy(rows_v, out_hbm.at[pos_v.at[k, c]], sem).wait()

    return scatter(rows, pos.reshape(TOP_K, SC_WORKERS, n_chunks, SC_CHUNK))


def _dispatch_plan(route, counts, n_slots, expert_base):
    e = route[:, 0:TOP_K].astype(jnp.int32)
    rank = route[:, ROUTE_RANK0:ROUTE_RANK0 + TOP_K].astype(jnp.int32)
    counts = counts[0, :N_EXPERTS].astype(jnp.int32)
    padded = (counts + MOE_TM - 1) // MOE_TM * MOE_TM
    ends = jnp.cumsum(padded)
    starts = ends - padded
    pos = (starts[e] + rank).T
    tile_start = jnp.arange(n_slots // MOE_TM, dtype=jnp.int32) * MOE_TM
    tile_e = jnp.minimum(jnp.sum((tile_start[:, None] >= ends[None, :]).astype(jnp.int32), axis=1),
                         N_EXPERTS - 1)
    tile_rows = jnp.clip(counts[tile_e] - (tile_start - starts[tile_e]), 0, MOE_TM)
    tile_rows = jnp.where(tile_start < ends[-1], tile_rows, 0)
    first = jnp.concatenate([jnp.ones((1,), jnp.int32),
                             (tile_e[1:] != tile_e[:-1]).astype(jnp.int32)])
    n_tiles = n_slots // MOE_TM
    tile_id = tile_start // MOE_TM
    n_used = ends[-1] // MOE_TM
    blocks = jnp.stack([jnp.minimum(tile_id, n_used - 1),
                        jnp.where(tile_id < n_used, tile_id, n_tiles - 1)]).astype(jnp.int32)
    return pos, tile_e + expert_base, tile_rows.astype(jnp.int32), first, blocks


def _moe_tile_kernel(te_ref, tr_ref, tf_ref, tb_ref, xs_ref, wg_ref, wu_ref, wd_ref, ys_ref,
                     wg_s, wu_s, wd_s):
    t = pl.program_id(0)

    @pl.when(tf_ref[t] != 0)
    def _():
        wg_s[...] = wg_ref[0].astype(BF16)
        wu_s[...] = wu_ref[0].astype(BF16)
        wd_s[...] = wd_ref[0].astype(BF16)

    @pl.when(tr_ref[t] != 0)
    def _():
        occupied = lax.broadcasted_iota(jnp.int32, (MOE_TM, 1), 0) < tr_ref[t]
        x = jnp.where(occupied, _unpack_bf16_pairs(xs_ref[...]), 0.0).astype(BF16)
        hg = jnp.dot(x, wg_s[...], preferred_element_type=F32)
        hu = jnp.dot(x, wu_s[...], preferred_element_type=F32)
        a = (jax.nn.silu(hg) * hu).astype(BF16)
        ys_ref[...] = _pack_bf16_pairs(jnp.dot(a, wd_s[...], preferred_element_type=F32))

    @pl.when(tr_ref[t] == 0)
    def _():
        ys_ref[...] = jnp.zeros_like(ys_ref)


def _moe_tiles(xs, tile_expert, tile_rows, tile_first, tile_blocks, w_gate, w_up, w_down):
    n_slots, half = xs.shape
    d = 2 * half
    grid_spec = pltpu.PrefetchScalarGridSpec(
        num_scalar_prefetch=4,
        grid=(n_slots // MOE_TM,),
        in_specs=[
            pl.BlockSpec((MOE_TM, half), lambda t, te, tr, tf, tb: (tb[0, t], 0)),
            pl.BlockSpec((1, d, D_EXPERT), lambda t, te, tr, tf, tb: (te[t], 0, 0)),
            pl.BlockSpec((1, d, D_EXPERT), lambda t, te, tr, tf, tb: (te[t], 0, 0)),
            pl.BlockSpec((1, D_EXPERT, d), lambda t, te, tr, tf, tb: (te[t], 0, 0)),
        ],
        out_specs=pl.BlockSpec((MOE_TM, half), lambda t, te, tr, tf, tb: (tb[1, t], 0)),
        scratch_shapes=[pltpu.VMEM((d, D_EXPERT), BF16), pltpu.VMEM((d, D_EXPERT), BF16),
                        pltpu.VMEM((D_EXPERT, d), BF16)],
    )
    return pl.pallas_call(
        _moe_tile_kernel,
        out_shape=jax.ShapeDtypeStruct((n_slots, half), jnp.int32),
        grid_spec=grid_spec,
        compiler_params=_cparams(("arbitrary",)),
        name="moe_tiles",
    )(tile_expert, tile_rows, tile_first, tile_blocks, xs, w_gate, w_up, w_down)


def _moe_combine_kernel(x_ref, y0_ref, y1_ref, r_ref, g_ref, o_ref, *, final_norm):
    w0 = r_ref[:, 2:3]
    w1 = r_ref[:, 3:4]
    x = x_ref[...] + w0 * _unpack_bf16_pairs(y0_ref[0]) + w1 * _unpack_bf16_pairs(y1_ref[0])
    if final_norm:
        ms = jnp.mean(x * x, -1, keepdims=True)
        x = x * lax.rsqrt(ms + EPS) * g_ref[...]
    o_ref[...] = x


def _moe_combine(x2, yg, route, norm_g, final_norm):
    n, d = x2.shape
    tm = 1024
    return pl.pallas_call(
        functools.partial(_moe_combine_kernel, final_norm=final_norm),
        out_shape=jax.ShapeDtypeStruct((n, d), F32),
        grid=(n // tm,),
        in_specs=[
            pl.BlockSpec((tm, d), lambda i: (i, 0)),
            pl.BlockSpec((1, tm, d // 2), lambda i: (0, i, 0)),
            pl.BlockSpec((1, tm, d // 2), lambda i: (1, i, 0)),
            pl.BlockSpec((tm, ROUTE_W), lambda i: (i, 0)),
            pl.BlockSpec((1, d), lambda i: (0, 0)),
        ],
        out_specs=pl.BlockSpec((tm, d), lambda i: (i, 0)),
        compiler_params=_cparams(("parallel",)),
        name="moe_combine",
    )(x2, yg, yg, route, norm_g.reshape(1, d))


def _moe(h2p, route, counts, x2, layer, w_gate, w_up, w_down, norm_f_g, final_norm):
    n, d = x2.shape
    n_slots = TOP_K * n + N_EXPERTS * MOE_TM
    pos, tile_expert, tile_rows, tile_first, tile_blocks = _dispatch_plan(
        route, counts, n_slots, layer * N_EXPERTS)
    xs = _sc_scatter_rows(h2p, pos, n_slots)
    ys = _moe_tiles(xs, tile_expert, tile_rows, tile_first, tile_blocks, w_gate, w_up, w_down)
    yg = _sc_gather_rows(ys, pos.reshape(-1)).reshape(TOP_K, n, d // 2)
    return _moe_combine(x2, yg, route, norm_f_g, final_norm)


def kernel(x, positions, norm1_g, w_in, b_in, attn_sinks, sgu_ln_g, sgu_ln_b, w_spatial, b_spatial,
           w_proj_a, w_proj_b, w_proj_c, w_out, norm2_g, w_router_group, w_router_expert,
           w_expert_gate, w_expert_up, w_expert_down, norm_f_g):
    bsz, s, d = x.shape
    depth = w_in.shape[0]
    wg_all = w_expert_gate.reshape(depth * N_EXPERTS, d, D_EXPERT)
    wu_all = w_expert_up.reshape(depth * N_EXPERTS, d, D_EXPERT)
    wd_all = w_expert_down.reshape(depth * N_EXPERTS, D_EXPERT, d)
    assert d == D_MODEL and s % (C_PAIRS[-1][1] * BLOCK) == 0
    tabs = _rope_tables(positions)
    for l in range(depth):
        z = _in_proj(x, norm1_g[l], w_in[l].astype(BF16), b_in[l], tabs)
        oa = _attn_a(z, attn_sinks[l])
        ob = _sgu(z, sgu_ln_g[l], sgu_ln_b[l], w_spatial[l], b_spatial[l])
        oc = _attn_c(z)
        x2, h2, route, counts = _merge(
            x.reshape(bsz * s, d), oa.reshape(bsz * s, A_Q_W), ob.reshape(bsz * s, B_WIDTH),
            oc.reshape(bsz * s, C_OUT_W), z.reshape(bsz * s, D_IN), w_proj_a[l], w_proj_b[l], w_proj_c[l], w_out[l],
            norm2_g[l], w_router_group[l], w_router_expert[l])
        x = _moe(h2, route, counts, x2, l, wg_all, wu_all, wd_all, norm_f_g, l == depth - 1).reshape(bsz, s, d)
    return x
```

```python
import functools

import jax
import jax.numpy as jnp
from jax import lax
from jax.experimental import pallas as pl
from jax.experimental.pallas import tpu as pltpu
from jax.experimental.pallas import tpu_sc as plsc

F32 = jnp.float32
BF16 = jnp.bfloat16

D_MODEL = 1024
HEAD_DIM = 64
ROT_DIM = HEAD_DIM // 4
ROPE_THETA = 500000.0
BLOCK = 128
EPS = 1e-5
NEG_INF = -1e30

A_Q_HEADS = 16
A_KV_HEADS = 4
A_REP = A_Q_HEADS // A_KV_HEADS
A_WINDOW = 128
A_Q_W = A_Q_HEADS * HEAD_DIM
A_KV_W = A_KV_HEADS * HEAD_DIM

B_GROUPS = 12
B_CH = 64
B_WIDTH = B_GROUPS * B_CH
B_CHUNK = 128

C_PAIRS = ((128, 1), (512, 4), (2048, 16))
C_HEADS_PER_GROUP = 4
C_HEADS = C_HEADS_PER_GROUP * len(C_PAIRS)
C_OUT_W = C_HEADS_PER_GROUP * HEAD_DIM

N_BRANCH = 3
GATE_W = N_BRANCH * D_MODEL
D_IN = A_Q_W + 2 * A_KV_W + 2 * B_WIDTH + 3 * C_HEADS * HEAD_DIM + GATE_W

N_EXPERT_GROUPS = 4
EXPERTS_PER_GROUP = 8
N_EXPERTS = N_EXPERT_GROUPS * EXPERTS_PER_GROUP
TOP_K = 2
D_EXPERT = 256

LANES = 128
MXU_W = 256
VMEM_LIMIT = 56 * 1024 * 1024

IN_TN = 256
COL_AQ = 0
COL_AK = A_Q_W // IN_TN
COL_AV = COL_AK + A_KV_W // IN_TN
COL_BU = COL_AV + A_KV_W // IN_TN
COL_BV = COL_BU + B_WIDTH // IN_TN
COL_CQ = COL_BV + B_WIDTH // IN_TN
COL_CK = COL_CQ + C_HEADS * HEAD_DIM // IN_TN
COL_CV = COL_CK + C_HEADS * HEAD_DIM // IN_TN
COL_G = COL_CV + C_HEADS * HEAD_DIM // IN_TN
N_COL_TILES = D_IN // IN_TN

ROUTE_W = LANES
ROUTE_E0 = N_EXPERT_GROUPS
ROUTE_RANK0 = 4
SUBLANES = 8


def _cparams(sem):
    return pltpu.CompilerParams(dimension_semantics=sem, vmem_limit_bytes=VMEM_LIMIT)


def _rope_table_kernel(pos_ref, inv_ref, c_ref, s1_ref, s2_ref):
    lane = lax.broadcasted_iota(jnp.int32, (1, LANES), 1)
    d = lane & (HEAD_DIM - 1)
    ang = pos_ref[...].astype(F32) * inv_ref[...]
    c = jnp.cos(ang)
    s = jnp.sin(ang)
    half = ROT_DIM // 2
    c_ref[...] = jnp.where(d < ROT_DIM, c, 1.0)
    s1_ref[...] = jnp.where(d < half, -s, 0.0)
    s2_ref[...] = jnp.where((d >= half) & (d < ROT_DIM), s, 0.0)


def _rope_tables(positions):
    n = positions.size
    inv = ROPE_THETA ** (-jnp.arange(0, ROT_DIM, 2, dtype=F32) / ROT_DIM)
    inv_lane = jnp.tile(inv, LANES // inv.shape[0]).reshape(1, LANES)
    pos_b = jnp.broadcast_to(positions.reshape(n, 1), (n, LANES))
    tm = 1024
    spec = pl.BlockSpec((tm, LANES), lambda i: (i, 0))
    return pl.pallas_call(
        _rope_table_kernel,
        out_shape=[jax.ShapeDtypeStruct((n, LANES), F32)] * 3,
        grid=(n // tm,),
        in_specs=[spec, pl.BlockSpec((1, LANES), lambda i: (0, 0))],
        out_specs=[spec, spec, spec],
        compiler_params=_cparams(("parallel",)),
        name="rope_tables",
    )(pos_b, inv_lane)


IN_STEP_TILES = 3
IN_STEP_W = IN_STEP_TILES * IN_TN
IN_STEPS = N_COL_TILES // IN_STEP_TILES


def _is_rope_tile(tile):
    return tile < COL_AV or COL_CQ <= tile < COL_CV


_STEP_ROPES = tuple(tuple(_is_rope_tile(step * IN_STEP_TILES + u) for u in range(IN_STEP_TILES))
                    for step in range(IN_STEPS))


def _in_proj_kernel(x_ref, g_ref, w_ref, b_ref, c_ref, s1_ref, s2_ref, z_ref, h_ref,
                    acc0_ref, acc1_ref):
    j = pl.program_id(1)

    @pl.when(j == 0)
    def _():
        x = x_ref[0]
        ms = jnp.mean(x * x, -1, keepdims=True)
        h_ref[...] = (x * lax.rsqrt(ms + EPS) * g_ref[...]).astype(BF16)

    prev = j - 1

    def matmul_into(acc_ref):
        acc_ref[...] = jnp.dot(h_ref[...], w_ref[...], preferred_element_type=F32) + b_ref[...]

    def finish(acc_ref, ropes):
        for u, rope in enumerate(ropes):
            cols = slice(u * IN_TN, (u + 1) * IN_TN)
            if not rope:
                z_ref[0, :, cols] = acc_ref[:, cols].astype(BF16)
                continue
            tile = prev * IN_STEP_TILES + u
            is_q = (tile < COL_AK) | ((tile >= COL_CQ) & (tile < COL_CK))
            scale = jnp.where(is_q, HEAD_DIM ** -0.5, 1.0).astype(F32)
            for t in range(u * IN_TN // LANES, (u + 1) * IN_TN // LANES):
                lanes = slice(t * LANES, (t + 1) * LANES)
                a = acc_ref[:, lanes]
                r = (a * c_ref[...] + pltpu.roll(a, LANES - ROT_DIM // 2, 1) * s1_ref[...]
                     + pltpu.roll(a, ROT_DIM // 2, 1) * s2_ref[...])
                z_ref[0, :, lanes] = (r * scale).astype(BF16)

    accs = (acc0_ref, acc1_ref)
    patterns = sorted(set(_STEP_ROPES))
    pattern_id = jnp.int32(0)
    for step, ropes in enumerate(_STEP_ROPES):
        pattern_id = jnp.where(prev == step, patterns.index(ropes), pattern_id)
    inner = (j > 0) & (j < IN_STEPS)
    for parity in range(2):
        for pid, ropes in enumerate(patterns):
            @pl.when(inner & (j % 2 == parity) & (pattern_id == pid))
            def _(parity=parity, ropes=ropes):
                matmul_into(accs[parity])
                finish(accs[1 - parity], ropes)

    @pl.when(j == 0)
    def _():
        matmul_into(accs[0])

    @pl.when(j == IN_STEPS)
    def _():
        finish(accs[(IN_STEPS - 1) % 2], _STEP_ROPES[-1])


def _in_proj(x, g, w, b, tabs):
    bsz, s, d = x.shape
    c, s1, s2 = tabs
    tab_spec = pl.BlockSpec((s, LANES), lambda bi, j: (bi, 0))
    last = IN_STEPS - 1
    return pl.pallas_call(
        _in_proj_kernel,
        out_shape=jax.ShapeDtypeStruct((bsz, s, D_IN), BF16),
        grid=(bsz, IN_STEPS + 1),
        in_specs=[
            pl.BlockSpec((1, s, d), lambda bi, j: (bi, 0, 0)),
            pl.BlockSpec((1, d), lambda bi, j: (0, 0)),
            pl.BlockSpec((d, IN_STEP_W), lambda bi, j: (0, jnp.minimum(j, last))),
            pl.BlockSpec((1, IN_STEP_W), lambda bi, j: (0, jnp.minimum(j, last))),
            tab_spec, tab_spec, tab_spec,
        ],
        out_specs=pl.BlockSpec((1, s, IN_STEP_W), lambda bi, j: (bi, 0, jnp.maximum(j - 1, 0))),
        scratch_shapes=[pltpu.VMEM((s, d), BF16), pltpu.VMEM((s, IN_STEP_W), F32),
                        pltpu.VMEM((s, IN_STEP_W), F32)],
        compiler_params=_cparams(("parallel", "arbitrary")),
        name="in_proj",
    )(x, g.reshape(1, d), w, b.reshape(1, D_IN), c, s1, s2)


STACK = C_HEADS_PER_GROUP * BLOCK
STEP_UNROLL = 4


def _head_stack(blk):
    seg = lax.broadcasted_iota(jnp.int32, (1, MXU_W), 1) // HEAD_DIM
    rowseg = lax.broadcasted_iota(jnp.int32, (STACK, 1), 0) // BLOCK
    return jnp.where(seg == rowseg, jnp.concatenate([blk] * C_HEADS_PER_GROUP, axis=0),
                     jnp.zeros((), blk.dtype))


def _band_step(q, kp_ref, vp_ref, blk, n, has_prev, diag_key, sink_col, want_lse):
    return _band_blocks([(q, blk, n)], kp_ref, vp_ref, has_prev, diag_key, sink_col, want_lse)[0]


def _band_blocks(jobs, kp_ref, vp_ref, has_prev, diag_key, sink_col, want_lse):
    contract = (((1,), (1,)), ((), ()))
    i_loc = lax.broadcasted_iota(jnp.int32, (STACK, 1), 0) & (BLOCK - 1)
    jj = lax.broadcasted_iota(jnp.int32, (1, BLOCK), 1)
    upper = jj > i_loc
    seg = lax.broadcasted_iota(jnp.int32, (1, MXU_W), 1) // HEAD_DIM
    zero_b = jnp.zeros((), BF16)

    def key_rows(blk):
        lo = pl.multiple_of(blk * BLOCK, BLOCK)
        if has_prev:
            return pl.ds(lo, 2 * BLOCK)
        return pl.ds(pl.multiple_of(lo + BLOCK, BLOCK), BLOCK)

    scores = [lax.dot_general(_head_stack(q), kp_ref[key_rows(blk), :], contract,
                              preferred_element_type=F32) for q, blk, _ in jobs]

    def own_lanes(x):
        out = x[0:BLOCK]
        for h in range(1, C_HEADS_PER_GROUP):
            out = jnp.where(seg == h, x[h * BLOCK:(h + 1) * BLOCK], out)
        return out

    probs, stats = [], []
    for s, (_, blk, n) in zip(scores, jobs):
        if has_prev:
            sp = s[:, :BLOCK] + jnp.where(n == 0, NEG_INF, 0.0).astype(F32)
            f = jnp.where(upper, sp, s[:, BLOCK:])
        else:
            f = jnp.where(upper, NEG_INF, s)
        m = jnp.max(f, -1, keepdims=True)
        if diag_key:
            sd = jnp.sum(jnp.where(jj == i_loc, sp, 0.0), -1, keepdims=True)
            m = jnp.maximum(m, sd)
        if sink_col is not None:
            m = jnp.maximum(m, sink_col)
        p = jnp.exp(f - m)
        den = jnp.sum(p, -1, keepdims=True)
        pd = None
        if diag_key:
            pd = jnp.exp(sd - m)
            den = den + pd
        if sink_col is not None:
            den = den + jnp.exp(sink_col - m)
        inv = 1.0 / den
        pn = (p * inv).astype(BF16)
        if has_prev:
            below = zero_b
            if diag_key:
                below = jnp.where(jj == i_loc, pd * inv, 0.0).astype(BF16)
            pn = jnp.concatenate([jnp.where(upper, pn, below), jnp.where(upper, zero_b, pn)],
                                 axis=1)
        probs.append(pn)
        stats.append((m, den))

    outs = []
    for pn, (m, den), (_, blk, _) in zip(probs, stats, jobs):
        o = own_lanes(jnp.dot(pn, vp_ref[key_rows(blk), :], preferred_element_type=F32))
        lse = own_lanes(m + jnp.log(den)) + jnp.zeros((BLOCK, MXU_W), F32) if want_lse else None
        outs.append((o, lse))
    return outs


def _replicate_head(blk, g):
    words = pltpu.bitcast(blk, jnp.uint32)
    tile = words[:, (g // 2) * LANES:(g // 2 + 1) * LANES]
    swapped = pltpu.roll(tile, HEAD_DIM, 1)
    low = lax.broadcasted_iota(jnp.int32, (1, LANES), 1) < HEAD_DIM
    both = jnp.where(low, tile, swapped) if g % 2 == 0 else jnp.where(low, swapped, tile)
    return pltpu.bitcast(jnp.concatenate([both, both], axis=1), BF16)


def _attn_a_kernel(sink_ref, q_ref, k_ref, v_ref, o_ref, kp_ref, vp_ref):
    nb = q_ref.shape[1] // BLOCK
    rowseg = lax.broadcasted_iota(jnp.int32, (STACK, 1), 0) // BLOCK
    kp_ref[pl.ds(0, BLOCK), :] = jnp.zeros((BLOCK, MXU_W), BF16)
    vp_ref[pl.ds(0, BLOCK), :] = jnp.zeros((BLOCK, MXU_W), BF16)
    for g in range(A_KV_HEADS):
        kv_cols = slice(g * HEAD_DIM, (g + 1) * HEAD_DIM)
        q_cols = slice(g * MXU_W, (g + 1) * MXU_W)

        def build(n, carry):
            rows = pl.ds(pl.multiple_of(n * BLOCK, BLOCK), BLOCK)
            dst = pl.ds(pl.multiple_of((n + 1) * BLOCK, BLOCK), BLOCK)
            kp_ref[dst, :] = _replicate_head(k_ref[0, rows, :], g)
            vp_ref[dst, :] = _replicate_head(v_ref[0, rows, :], g)
            return carry

        lax.fori_loop(0, nb, build, 0, unroll=8)
        sink_col = jnp.zeros((STACK, 1), F32)
        for r in range(A_REP):
            sink_col = jnp.where(rowseg == r, sink_ref[g * A_REP + r], sink_col)

        def step(it, carry):
            blocks = [it * STEP_UNROLL + u for u in range(STEP_UNROLL)]
            rows = [pl.ds(pl.multiple_of(n * BLOCK, BLOCK), BLOCK) for n in blocks]
            jobs = [(q_ref[0, r, q_cols], n, n) for r, n in zip(rows, blocks)]
            outs = _band_blocks(jobs, kp_ref, vp_ref, True, False, sink_col, False)
            for r, (o, _) in zip(rows, outs):
                o_ref[0, r, q_cols] = o.astype(o_ref.dtype)
            return carry

        lax.fori_loop(0, nb // STEP_UNROLL, step, 0)


def _attn_a(z, sinks):
    bsz, s, _ = z.shape
    nb = s // BLOCK
    grid_spec = pltpu.PrefetchScalarGridSpec(
        num_scalar_prefetch=1,
        grid=(bsz,),
        in_specs=[
            pl.BlockSpec((1, s, A_Q_W), lambda bi, sk: (bi, 0, 0)),
            pl.BlockSpec((1, s, A_KV_W), lambda bi, sk: (bi, 0, COL_AK)),
            pl.BlockSpec((1, s, A_KV_W), lambda bi, sk: (bi, 0, COL_AV)),
        ],
        out_specs=pl.BlockSpec((1, s, A_Q_W), lambda bi, sk: (bi, 0, 0)),
        scratch_shapes=[pltpu.VMEM(((nb + 1) * BLOCK, MXU_W), BF16),
                        pltpu.VMEM(((nb + 1) * BLOCK, MXU_W), BF16)],
    )
    return pl.pallas_call(
        _attn_a_kernel,
        out_shape=jax.ShapeDtypeStruct((bsz, s, A_Q_W), BF16),
        grid_spec=grid_spec,
        compiler_params=_cparams(("parallel",)),
        name="attn_a",
    )(sinks, z, z, z)


def _attn_c_kernel(q_ref, k_ref, v_ref, oc_ref, stage_ref, pq_ref, pk_ref, pv_ref, og_ref, lg_ref):
    s = q_ref.shape[1]
    nblk = s // BLOCK
    halves = MXU_W // LANES
    pk_ref[pl.ds(0, BLOCK), :] = jnp.zeros((BLOCK, MXU_W), BF16)
    pv_ref[pl.ds(0, BLOCK), :] = jnp.zeros((BLOCK, MXU_W), BF16)

    def class_rows(idx, nb, d):
        r = idx // nb
        n = idx - r * nb
        start = r + n * (BLOCK * d)
        rows = pl.ds(start, BLOCK, stride=d) if d > 1 else pl.ds(pl.multiple_of(start, BLOCK), BLOCK)
        return n, rows

    def stage(src_ref, cols):
        x = src_ref[0, :, cols].astype(F32)
        for t in range(halves):
            stage_ref[t] = x[:, t * LANES:(t + 1) * LANES]

    def staged_block(rows):
        return jnp.concatenate([stage_ref[t, rows, :] for t in range(halves)], axis=1).astype(BF16)

    for g, (win, d) in enumerate(C_PAIRS):
        cols = slice(g * MXU_W, (g + 1) * MXU_W)
        nb = nblk // d
        has_prev = nb > 1
        assert win // d == BLOCK

        def class_major(dst_ref, offset):
            def build(idx, carry):
                _, rows = class_rows(idx, nb, d)
                dst = pl.ds(pl.multiple_of((idx + offset) * BLOCK, BLOCK), BLOCK)
                dst_ref[dst, :] = staged_block(rows)
                return carry
            return build

        for src_ref, dst_ref, offset in ((k_ref, pk_ref, 1), (v_ref, pv_ref, 1), (q_ref, pq_ref, 0)):
            stage(src_ref, cols)
            lax.fori_loop(0, nblk, class_major(dst_ref, offset), 0, unroll=4)

        def step(it, carry):
            jobs, dsts = [], []
            for u in range(STEP_UNROLL):
                idx = it * STEP_UNROLL + u
                n, rows = class_rows(idx, nb, d)
                here = pl.ds(pl.multiple_of(idx * BLOCK, BLOCK), BLOCK)
                jobs.append((pq_ref[here, :], idx, n))
                dsts.append(rows)
            outs = _band_blocks(jobs, pk_ref, pv_ref, has_prev, has_prev, None, True)
            for rows, (o, lse) in zip(dsts, outs):
                for t in range(halves):
                    og_ref[g, t, rows, :] = o[:, t * LANES:(t + 1) * LANES]
                    lg_ref[g, t, rows, :] = lse[:, t * LANES:(t + 1) * LANES]
            return carry

        lax.fori_loop(0, nblk // STEP_UNROLL, step, 0)

    chunk = 2 * BLOCK

    def combine(i, carry):
        rows = pl.ds(pl.multiple_of(i * chunk, chunk), chunk)
        for t in range(halves):
            ls = [lg_ref[g, t, rows, :] for g in range(len(C_PAIRS))]
            m = jnp.maximum(jnp.maximum(ls[0], ls[1]), ls[2])
            es = [jnp.exp(l - m) for l in ls]
            num = (es[0] * og_ref[0, t, rows, :] + es[1] * og_ref[1, t, rows, :]
                   + es[2] * og_ref[2, t, rows, :])
            oc_ref[0, rows, t * LANES:(t + 1) * LANES] = (
                num / (es[0] + es[1] + es[2])).astype(oc_ref.dtype)
        return carry

    lax.fori_loop(0, s // chunk, combine, 0)


def _attn_c(z):
    bsz, s, _ = z.shape
    nblk = s // BLOCK
    width = C_HEADS * HEAD_DIM
    halves = MXU_W // LANES

    def in_spec(col):
        return pl.BlockSpec((1, s, width), lambda bi: (bi, 0, col * IN_TN // width))

    return pl.pallas_call(
        _attn_c_kernel,
        out_shape=jax.ShapeDtypeStruct((bsz, s, C_OUT_W), BF16),
        grid=(bsz,),
        in_specs=[in_spec(COL_CQ), in_spec(COL_CK), in_spec(COL_CV)],
        out_specs=pl.BlockSpec((1, s, C_OUT_W), lambda bi: (bi, 0, 0)),
        scratch_shapes=[
            pltpu.VMEM((halves, s, LANES), F32),
            pltpu.VMEM((s, MXU_W), BF16),
            pltpu.VMEM((s + BLOCK, MXU_W), BF16),
            pltpu.VMEM((s + BLOCK, MXU_W), BF16),
            pltpu.VMEM((len(C_PAIRS), halves, s, LANES), F32),
            pltpu.VMEM((len(C_PAIRS), halves, s, LANES), F32),
        ],
        compiler_params=_cparams(("parallel",)),
        name="attn_c",
    )(z, z, z)


def _sgu_kernel(zu_ref, zv_ref, lng_ref, lnb_ref, ws_ref, bs_ref, o_ref):
    tm = zu_ref.shape[1]
    u = jax.nn.gelu(zu_ref[0].astype(F32))
    v = jax.nn.gelu(zv_ref[0].astype(F32))
    mu = jnp.mean(v, -1, keepdims=True)
    var = jnp.mean(jnp.square(v - mu), -1, keepdims=True)
    vn = ((v - mu) * lax.rsqrt(var + EPS) * lng_ref[...] + lnb_ref[...]).astype(BF16)
    causal = (lax.broadcasted_iota(jnp.int32, (B_CHUNK, 1), 0)
              >= lax.broadcasted_iota(jnp.int32, (1, B_CHUNK), 1))
    first = lax.broadcasted_iota(jnp.int32, (1, LANES), 1) < B_CH
    zero = jnp.zeros((), BF16)
    for p in range(B_WIDTH // LANES):
        w0 = jnp.where(causal, ws_ref[2 * p], zero)
        w1 = jnp.where(causal, ws_ref[2 * p + 1], zero)
        cols = slice(p * LANES, (p + 1) * LANES)
        for c in range(tm // B_CHUNK):
            rows = slice(c * B_CHUNK, (c + 1) * B_CHUNK)
            vv = vn[rows, cols]
            sv = jnp.where(first,
                           jnp.dot(w0, vv, preferred_element_type=F32),
                           jnp.dot(w1, vv, preferred_element_type=F32)) + bs_ref[:, cols]
            o_ref[0, rows, cols] = (u[rows, cols] * sv).astype(o_ref.dtype)


def _sgu(z, ln_g, ln_b, w_s, b_s):
    bsz, s, _ = z.shape
    tm = 512
    bias = jnp.repeat(b_s.T, B_CH, axis=1)
    col_u = COL_BU * IN_TN // B_WIDTH
    col_v = COL_BV * IN_TN // B_WIDTH
    return pl.pallas_call(
        _sgu_kernel,
        out_shape=jax.ShapeDtypeStruct((bsz, s, B_WIDTH), BF16),
        grid=(bsz, s // tm),
        in_specs=[
            pl.BlockSpec((1, tm, B_WIDTH), lambda bi, i: (bi, i, col_u)),
            pl.BlockSpec((1, tm, B_WIDTH), lambda bi, i: (bi, i, col_v)),
            pl.BlockSpec((1, B_WIDTH), lambda bi, i: (0, 0)),
            pl.BlockSpec((1, B_WIDTH), lambda bi, i: (0, 0)),
            pl.BlockSpec((B_GROUPS, B_CHUNK, B_CHUNK), lambda bi, i: (0, 0, 0)),
            pl.BlockSpec((B_CHUNK, B_WIDTH), lambda bi, i: (0, 0)),
        ],
        out_specs=pl.BlockSpec((1, tm, B_WIDTH), lambda bi, i: (bi, i, 0)),
        compiler_params=_cparams(("parallel", "parallel")),
        name="sgu",
    )(z, z, ln_g.reshape(1, B_WIDTH), ln_b.reshape(1, B_WIDTH), w_s.astype(BF16), bias)


def _route(logits):
    lane = lax.broadcasted_iota(jnp.int32, (1, ROUTE_W), 1)
    lane_f = lane.astype(F32)
    is_g = lane < N_EXPERT_GROUPS
    lg = jnp.where(is_g, logits, NEG_INF)
    mg = jnp.max(lg, -1, keepdims=True)
    pg_top = 1.0 / jnp.sum(jnp.where(is_g, jnp.exp(lg - mg), 0.0), -1, keepdims=True)
    g_idx = jnp.min(jnp.where(lg == mg, lane_f, float(ROUTE_W)), -1, keepdims=True)
    e_group = ((lane - ROUTE_E0) // EXPERTS_PER_GROUP).astype(F32)
    in_grp = (lane >= ROUTE_E0) & (lane < ROUTE_E0 + N_EXPERTS) & (e_group == g_idx)
    le = jnp.where(in_grp, logits, NEG_INF)
    m1 = jnp.max(le, -1, keepdims=True)
    i1 = jnp.min(jnp.where(le == m1, lane_f, float(ROUTE_W)), -1, keepdims=True)
    le2 = jnp.where(lane_f == i1, NEG_INF, le)
    m2 = jnp.max(le2, -1, keepdims=True)
    i2 = jnp.min(jnp.where(le2 == m2, lane_f, float(ROUTE_W)), -1, keepdims=True)
    t = jnp.exp(m2 - m1)
    w1 = pg_top / (1.0 + t)
    w2 = w1 * t
    picks = (i1 - ROUTE_E0, i2 - ROUTE_E0, w1, w2)
    out = jnp.zeros(logits.shape, F32)
    for k, val in enumerate(picks):
        out = jnp.where(lane == k, val, out)
    return out


def _merge_kernel(x_ref, oa_ref, ob_ref, oc_ref,
                  g0_ref, g1_ref, g2_ref, g3_ref, wa_ref, wb_ref, wc_ref, wo_ref, n2_ref, wr_ref,
                  xo_ref, h2_ref, route_ref, counts_ref, pos_ref, xn0_ref, xn1_ref, cnt_ref):
    i = pl.program_id(0)
    n_tiles = pl.num_programs(0) - 1

    def residual_into(xn_ref):
        zg = jnp.concatenate([g0_ref[...], g1_ref[...], g2_ref[...], g3_ref[...]],
                             axis=1).astype(F32)
        gates = jax.nn.sigmoid(zg)
        pa = jnp.dot(oa_ref[...], wa_ref[...], preferred_element_type=F32)
        pb = jnp.dot(ob_ref[...], wb_ref[...], preferred_element_type=F32)
        pc = jnp.dot(oc_ref[...], wc_ref[...], preferred_element_type=F32)
        merged = (gates[:, :D_MODEL] * pa + gates[:, D_MODEL:2 * D_MODEL] * pb
                  + gates[:, 2 * D_MODEL:] * pc)
        xn_ref[...] = x_ref[...] + jnp.dot(merged.astype(BF16), wo_ref[...],
                                           preferred_element_type=F32)

    def tail_from(xn_ref):
        xn = xn_ref[...]
        xo_ref[...] = xn
        ms = jnp.mean(xn * xn, -1, keepdims=True)
        h2 = xn * lax.rsqrt(ms + EPS) * n2_ref[...]
        h2_ref[...] = _pack_bf16_pairs(h2)
        route = _route(jnp.dot(h2.astype(BF16), wr_ref[...], preferred_element_type=F32))
        tm = route.shape[0]
        lane = lax.broadcasted_iota(jnp.int32, (1, ROUTE_W), 1)
        lane_f = lane.astype(F32)
        row = lax.broadcasted_iota(jnp.int32, (tm, 1), 0)
        hit = [lane_f == route[:, k:k + 1] for k in range(TOP_K)]
        onehot = jnp.where(hit[0], 1.0, 0.0) + jnp.where(hit[1], 1.0, 0.0)
        scan = onehot
        shift = 1
        while shift < tm:
            scan = scan + jnp.where(row >= shift, pltpu.roll(scan, shift, 0), 0.0)
            shift *= 2
        before = scan - onehot + cnt_ref[0:1, :]
        capacity = pos_ref.shape[1] * LANES
        diag = (row & (LANES - 1)) == lane
        first_row = pl.multiple_of((i - 1) * (tm // LANES), tm // LANES)
        for k in range(TOP_K):
            rank = jnp.sum(jnp.where(hit[k], before, 0.0), -1, keepdims=True)
            slot = jnp.where(diag, route[:, k:k + 1] * capacity + rank, 0.0)
            dense = jnp.concatenate(
                [jnp.sum(slot[r * LANES:(r + 1) * LANES], axis=0, keepdims=True)
                 for r in range(tm // LANES)], axis=0)
            pos_ref[k, pl.ds(first_row, tm // LANES), :] = dense.astype(jnp.int32)
        route_ref[...] = route
        cnt = cnt_ref[0:1, :] + jnp.sum(onehot, axis=0, keepdims=True)
        cnt_ref[...] = jnp.broadcast_to(cnt, cnt_ref.shape)
        counts_ref[...] = jnp.broadcast_to(cnt, counts_ref.shape)

    bufs = (xn0_ref, xn1_ref)
    for parity in range(2):
        @pl.when((i > 0) & (i < n_tiles) & (i % 2 == parity))
        def _(parity=parity):
            tail_from(bufs[1 - parity])
            residual_into(bufs[parity])

        @pl.when((i == n_tiles) & (i % 2 == parity))
        def _(parity=parity):
            tail_from(bufs[1 - parity])

    @pl.when(i == 0)
    def _():
        cnt_ref[...] = jnp.zeros_like(cnt_ref)
        residual_into(bufs[0])


def _merge(x2, oa, ob, oc, z2, wa, wb, wc, wo, n2g, w_rg, w_re):
    n, d = x2.shape
    tm = 512
    wr = jnp.concatenate(
        [w_rg, w_re, jnp.zeros((d, ROUTE_W - N_EXPERT_GROUPS - N_EXPERTS), w_rg.dtype)], axis=1)
    gate_w = GATE_W // 4
    gate_col0 = COL_G * IN_TN // gate_w

    n_tiles = n // tm
    last = n_tiles - 1

    def rows(width):
        return pl.BlockSpec((tm, width), lambda i: (jnp.minimum(i, last), 0))

    def lagged_rows(width):
        return pl.BlockSpec((tm, width), lambda i: (jnp.maximum(i - 1, 0), 0))

    def full(shape):
        return pl.BlockSpec(shape, lambda i: (0, 0))

    gate_specs = [pl.BlockSpec((tm, gate_w), lambda i, k=k: (jnp.minimum(i, last), gate_col0 + k))
                  for k in range(4)]
    return pl.pallas_call(
        _merge_kernel,
        out_shape=[jax.ShapeDtypeStruct((n, d), F32),
                   jax.ShapeDtypeStruct((n, d // 2), jnp.int32),
                   jax.ShapeDtypeStruct((n, ROUTE_W), F32),
                   jax.ShapeDtypeStruct((SUBLANES, ROUTE_W), F32),
                   jax.ShapeDtypeStruct((TOP_K, n // LANES, LANES), jnp.int32)],
        grid=(n_tiles + 1,),
        in_specs=[rows(d), rows(A_Q_W), rows(B_WIDTH), rows(C_OUT_W),
                  *gate_specs,
                  full((A_Q_W, d)), full((B_WIDTH, d)), full((C_OUT_W, d)), full((d, d)),
                  full((1, d)), full((d, ROUTE_W))],
        out_specs=[lagged_rows(d), lagged_rows(d // 2), lagged_rows(ROUTE_W),
                   full((SUBLANES, ROUTE_W)),
                   pl.BlockSpec((TOP_K, n // LANES, LANES), lambda i: (0, 0, 0))],
        scratch_shapes=[pltpu.VMEM((tm, d), F32), pltpu.VMEM((tm, d), F32),
                        pltpu.VMEM((SUBLANES, ROUTE_W), F32)],
        compiler_params=_cparams(("arbitrary",)),
        name="merge",
    )(x2, oa, ob, oc, z2, z2, z2, z2,
      wa.astype(BF16), wb.astype(BF16), wc.astype(BF16), wo.astype(BF16),
      n2g.reshape(1, d), wr.astype(BF16))


MOE_TM = 1024
SC_CORES = 2
SC_SUBCORES = 16
SC_WORKERS = SC_CORES * SC_SUBCORES
SC_CHUNK = 128


def _pack_bf16_pairs(x):
    w = x.shape[1] // 2
    lo = pltpu.bitcast(x[:, :w].astype(BF16).astype(F32), jnp.int32)
    hi = pltpu.bitcast(x[:, w:].astype(BF16).astype(F32), jnp.int32)
    return (hi & jnp.int32(-65536)) | lax.shift_right_logical(lo, jnp.int32(16))


def _unpack_bf16_pairs(p):
    lo = pltpu.bitcast(lax.shift_left(p, jnp.int32(16)), F32)
    hi = pltpu.bitcast(p & jnp.int32(-65536), F32)
    return jnp.concatenate([lo, hi], axis=1)


def _sc_gather_rows(table, idx):
    rows, width = idx.shape[0], table.shape[1]
    per_worker = rows // SC_WORKERS
    n_chunks = per_worker // SC_CHUNK
    assert per_worker * SC_WORKERS == rows and n_chunks * SC_CHUNK == per_worker
    mesh = plsc.VectorSubcoreMesh(core_axis_name="c", subcore_axis_name="s",
                                  num_cores=SC_CORES, num_subcores=SC_SUBCORES)

    @functools.partial(
        pl.kernel, mesh=mesh,
        out_type=jax.ShapeDtypeStruct((rows, width), table.dtype),
        scratch_types=[pltpu.VMEM((n_chunks, SC_CHUNK), jnp.int32),
                       pltpu.VMEM((SC_CHUNK, width), table.dtype),
                       pltpu.SemaphoreType.DMA],
        name="sc_gather_rows",
    )
    def gather(table_hbm, idx_hbm, out_hbm, idx_v, rows_v, sem):
        wid = lax.axis_index("s") * SC_CORES + lax.axis_index("c")
        pltpu.sync_copy(idx_hbm.at[wid], idx_v)
        base = wid * per_worker

        @pl.loop(0, n_chunks)
        def _(c):
            pltpu.async_copy(table_hbm.at[idx_v.at[c]], rows_v, sem).wait()
            pltpu.sync_copy(rows_v, out_hbm.at[pl.ds(base + c * SC_CHUNK, SC_CHUNK)])

    return gather(table, idx.reshape(SC_WORKERS, n_chunks, SC_CHUNK))


def _sc_scatter_rows(rows, pos, n_slots):
    n, width = rows.shape
    per_worker = n // SC_WORKERS
    n_chunks = per_worker // SC_CHUNK
    assert per_worker * SC_WORKERS == n and n_chunks * SC_CHUNK == per_worker
    mesh = plsc.VectorSubcoreMesh(core_axis_name="c", subcore_axis_name="s",
                                  num_cores=SC_CORES, num_subcores=SC_SUBCORES)

    @functools.partial(
        pl.kernel, mesh=mesh,
        out_type=jax.ShapeDtypeStruct((n_slots, width), rows.dtype),
        scratch_types=[pltpu.VMEM((TOP_K, n_chunks, SC_CHUNK), jnp.int32),
                       pltpu.VMEM((SC_CHUNK, width), rows.dtype),
                       pltpu.SemaphoreType.DMA],
        name="sc_scatter_rows",
    )
    def scatter(rows_hbm, pos_hbm, out_hbm, pos_v, rows_v, sem):
        wid = lax.axis_index("s") * SC_CORES + lax.axis_index("c")
        for k in range(TOP_K):
            pltpu.sync_copy(pos_hbm.at[k, wid], pos_v.at[k])
        base = wid * per_worker

        @pl.loop(0, n_chunks)
        def _(c):
            pltpu.sync_copy(rows_hbm.at[pl.ds(base + c * SC_CHUNK, SC_CHUNK)], rows_v)
            for k in range(TOP_K):
                pltpu.async_copy(rows_v, out_hbm.at[pos_v.at[k, c]], sem).wait()

    return scatter(rows, pos.reshape(TOP_K, SC_WORKERS, n_chunks, SC_CHUNK))


def _dispatch_plan(counts, capacity, n_steps, expert_base):
    counts = counts[0, :N_EXPERTS].astype(jnp.int32)
    tiles = (counts + MOE_TM - 1) // MOE_TM
    tile_ends = jnp.cumsum(tiles)
    step = jnp.arange(n_steps, dtype=jnp.int32)
    used = step < tile_ends[-1]
    expert = jnp.minimum(jnp.sum((step[:, None] >= tile_ends[None, :]).astype(jnp.int32), axis=1),
                         N_EXPERTS - 1)
    chunk = step - (tile_ends - tiles)[expert]
    rows = jnp.where(used, jnp.clip(counts[expert] - chunk * MOE_TM, 0, MOE_TM), 0)
    first = jnp.concatenate([jnp.ones((1,), jnp.int32),
                             (expert[1:] != expert[:-1]).astype(jnp.int32)])
    block = expert * (capacity // MOE_TM) + chunk
    last_used = jnp.sum(jnp.where(step == tile_ends[-1] - 1, block, 0))
    spare = N_EXPERTS * (capacity // MOE_TM)
    blocks = jnp.stack([jnp.where(used, block, last_used), jnp.where(used, block, spare)])
    return expert + expert_base, rows.astype(jnp.int32), first, blocks.astype(jnp.int32)


def _moe_tile_kernel(te_ref, tr_ref, tf_ref, tb_ref, xs_ref, wg_ref, wu_ref, wd_ref, ys_ref,
                     wg_s, wu_s, wd_s):
    t = pl.program_id(0)

    @pl.when(tf_ref[t] != 0)
    def _():
        wg_s[...] = wg_ref[0].astype(BF16)
        wu_s[...] = wu_ref[0].astype(BF16)
        wd_s[...] = wd_ref[0].astype(BF16)

    @pl.when(tr_ref[t] != 0)
    def _():
        occupied = lax.broadcasted_iota(jnp.int32, (MOE_TM, 1), 0) < tr_ref[t]
        x = jnp.where(occupied, _unpack_bf16_pairs(xs_ref[...]), 0.0).astype(BF16)
        hg = jnp.dot(x, wg_s[...], preferred_element_type=F32)
        hu = jnp.dot(x, wu_s[...], preferred_element_type=F32)
        a = (jax.nn.silu(hg) * hu).astype(BF16)
        ys_ref[...] = _pack_bf16_pairs(jnp.dot(a, wd_s[...], preferred_element_type=F32))

    @pl.when(tr_ref[t] == 0)
    def _():
        ys_ref[...] = jnp.zeros_like(ys_ref)


def _moe_tiles(xs, tile_expert, tile_rows, tile_first, tile_blocks, w_gate, w_up, w_down):
    n_slots, half = xs.shape
    d = 2 * half
    grid_spec = pltpu.PrefetchScalarGridSpec(
        num_scalar_prefetch=4,
        grid=(tile_expert.shape[0],),
        in_specs=[
            pl.BlockSpec((MOE_TM, half), lambda t, te, tr, tf, tb: (tb[0, t], 0)),
            pl.BlockSpec((1, d, D_EXPERT), lambda t, te, tr, tf, tb: (te[t], 0, 0)),
            pl.BlockSpec((1, d, D_EXPERT), lambda t, te, tr, tf, tb: (te[t], 0, 0)),
            pl.BlockSpec((1, D_EXPERT, d), lambda t, te, tr, tf, tb: (te[t], 0, 0)),
        ],
        out_specs=pl.BlockSpec((MOE_TM, half), lambda t, te, tr, tf, tb: (tb[1, t], 0)),
        scratch_shapes=[pltpu.VMEM((d, D_EXPERT), BF16), pltpu.VMEM((d, D_EXPERT), BF16),
                        pltpu.VMEM((D_EXPERT, d), BF16)],
    )
    return pl.pallas_call(
        _moe_tile_kernel,
        out_shape=jax.ShapeDtypeStruct((n_slots, half), jnp.int32),
        grid_spec=grid_spec,
        compiler_params=_cparams(("arbitrary",)),
        name="moe_tiles",
    )(tile_expert, tile_rows, tile_first, tile_blocks, xs, w_gate, w_up, w_down)


def _moe_combine_kernel(x_ref, y0_ref, y1_ref, r_ref, g_ref, o_ref, *, final_norm):
    w0 = r_ref[:, 2:3]
    w1 = r_ref[:, 3:4]
    x = x_ref[...] + w0 * _unpack_bf16_pairs(y0_ref[0]) + w1 * _unpack_bf16_pairs(y1_ref[0])
    if final_norm:
        ms = jnp.mean(x * x, -1, keepdims=True)
        x = x * lax.rsqrt(ms + EPS) * g_ref[...]
    o_ref[...] = x


def _moe_combine(x2, yg, route, norm_g, final_norm):
    n, d = x2.shape
    tm = 1024
    return pl.pallas_call(
        functools.partial(_moe_combine_kernel, final_norm=final_norm),
        out_shape=jax.ShapeDtypeStruct((n, d), F32),
        grid=(n // tm,),
        in_specs=[
            pl.BlockSpec((tm, d), lambda i: (i, 0)),
            pl.BlockSpec((1, tm, d // 2), lambda i: (0, i, 0)),
            pl.BlockSpec((1, tm, d // 2), lambda i: (1, i, 0)),
            pl.BlockSpec((tm, ROUTE_W), lambda i: (i, 0)),
            pl.BlockSpec((1, d), lambda i: (0, 0)),
        ],
        out_specs=pl.BlockSpec((tm, d), lambda i: (i, 0)),
        compiler_params=_cparams(("parallel",)),
        name="moe_combine",
    )(x2, yg, yg, route, norm_g.reshape(1, d))


def _moe(h2p, route, counts, pos, x2, layer, w_gate, w_up, w_down, norm_f_g, final_norm):
    n, d = x2.shape
    n_slots = N_EXPERTS * n + MOE_TM
    n_steps = TOP_K * n // MOE_TM + N_EXPERTS
    tile_expert, tile_rows, tile_first, tile_blocks = _dispatch_plan(
        counts, n, n_steps, layer * N_EXPERTS)
    xs = _sc_scatter_rows(h2p, pos, n_slots)
    ys = _moe_tiles(xs, tile_expert, tile_rows, tile_first, tile_blocks, w_gate, w_up, w_down)
    yg = _sc_gather_rows(ys, pos.reshape(-1)).reshape(TOP_K, n, d // 2)
    return _moe_combine(x2, yg, route, norm_f_g, final_norm)


def kernel(x, positions, norm1_g, w_in, b_in, attn_sinks, sgu_ln_g, sgu_ln_b, w_spatial, b_spatial,
           w_proj_a, w_proj_b, w_proj_c, w_out, norm2_g, w_router_group, w_router_expert,
           w_expert_gate, w_expert_up, w_expert_down, norm_f_g):
    bsz, s, d = x.shape
    depth = w_in.shape[0]
    wg_all = w_expert_gate.reshape(depth * N_EXPERTS, d, D_EXPERT)
    wu_all = w_expert_up.reshape(depth * N_EXPERTS, d, D_EXPERT)
    wd_all = w_expert_down.reshape(depth * N_EXPERTS, D_EXPERT, d)
    assert d == D_MODEL and s % (C_PAIRS[-1][1] * BLOCK) == 0
    tabs = _rope_tables(positions)
    for l in range(depth):
        z = _in_proj(x, norm1_g[l], w_in[l].astype(BF16), b_in[l], tabs)
        oa = _attn_a(z, attn_sinks[l])
        ob = _sgu(z, sgu_ln_g[l], sgu_ln_b[l], w_spatial[l], b_spatial[l])
        oc = _attn_c(z)
        x2, h2, route, counts, pos = _merge(
            x.reshape(bsz * s, d), oa.reshape(bsz * s, A_Q_W), ob.reshape(bsz * s, B_WIDTH),
            oc.reshape(bsz * s, C_OUT_W), z.reshape(bsz * s, D_IN), w_proj_a[l], w_proj_b[l], w_proj_c[l], w_out[l],
            norm2_g[l], w_router_group[l], w_router_expert[l])
        x = _moe(h2, route, counts, pos, x2, l, wg_all, wu_all, wd_all, norm_f_g, l == depth - 1).reshape(bsz, s, d)
    return x
```

```python
import functools

import jax
import jax.numpy as jnp
from jax import lax
from jax.experimental import pallas as pl
from jax.experimental.pallas import tpu as pltpu
from jax.experimental.pallas import tpu_sc as plsc

F32 = jnp.float32
BF16 = jnp.bfloat16

D_MODEL = 1024
HEAD_DIM = 64
ROT_DIM = HEAD_DIM // 4
ROPE_THETA = 500000.0
BLOCK = 128
EPS = 1e-5
NEG_INF = -1e30

A_Q_HEADS = 16
A_KV_HEADS = 4
A_REP = A_Q_HEADS // A_KV_HEADS
A_WINDOW = 128
A_Q_W = A_Q_HEADS * HEAD_DIM
A_KV_W = A_KV_HEADS * HEAD_DIM

B_GROUPS = 12
B_CH = 64
B_WIDTH = B_GROUPS * B_CH
B_CHUNK = 128

C_PAIRS = ((128, 1), (512, 4), (2048, 16))
C_HEADS_PER_GROUP = 4
C_HEADS = C_HEADS_PER_GROUP * len(C_PAIRS)
C_OUT_W = C_HEADS_PER_GROUP * HEAD_DIM

N_BRANCH = 3
GATE_W = N_BRANCH * D_MODEL
D_IN = A_Q_W + 2 * A_KV_W + 2 * B_WIDTH + 3 * C_HEADS * HEAD_DIM + GATE_W

N_EXPERT_GROUPS = 4
EXPERTS_PER_GROUP = 8
N_EXPERTS = N_EXPERT_GROUPS * EXPERTS_PER_GROUP
TOP_K = 2
D_EXPERT = 256

LANES = 128
MXU_W = 256
VMEM_LIMIT = 56 * 1024 * 1024

IN_TN = 256
COL_AQ = 0
COL_AK = A_Q_W // IN_TN
COL_AV = COL_AK + A_KV_W // IN_TN
COL_BU = COL_AV + A_KV_W // IN_TN
COL_BV = COL_BU + B_WIDTH // IN_TN
COL_CQ = COL_BV + B_WIDTH // IN_TN
COL_CK = COL_CQ + C_HEADS * HEAD_DIM // IN_TN
COL_CV = COL_CK + C_HEADS * HEAD_DIM // IN_TN
COL_G = COL_CV + C_HEADS * HEAD_DIM // IN_TN
N_COL_TILES = D_IN // IN_TN

ROUTE_W = LANES
ROUTE_E0 = N_EXPERT_GROUPS
ROUTE_RANK0 = 4
SUBLANES = 8


def _cparams(sem):
    return pltpu.CompilerParams(dimension_semantics=sem, vmem_limit_bytes=VMEM_LIMIT)


def _rope_table_kernel(pos_ref, inv_ref, c_ref, s1_ref, s2_ref):
    lane = lax.broadcasted_iota(jnp.int32, (1, LANES), 1)
    d = lane & (HEAD_DIM - 1)
    ang = pos_ref[...].astype(F32) * inv_ref[...]
    c = jnp.cos(ang)
    s = jnp.sin(ang)
    half = ROT_DIM // 2
    c_ref[...] = jnp.where(d < ROT_DIM, c, 1.0)
    s1_ref[...] = jnp.where(d < half, -s, 0.0)
    s2_ref[...] = jnp.where((d >= half) & (d < ROT_DIM), s, 0.0)


def _rope_tables(positions):
    n = positions.size
    inv = ROPE_THETA ** (-jnp.arange(0, ROT_DIM, 2, dtype=F32) / ROT_DIM)
    inv_lane = jnp.tile(inv, LANES // inv.shape[0]).reshape(1, LANES)
    pos_b = jnp.broadcast_to(positions.reshape(n, 1), (n, LANES))
    tm = 1024
    spec = pl.BlockSpec((tm, LANES), lambda i: (i, 0))
    return pl.pallas_call(
        _rope_table_kernel,
        out_shape=[jax.ShapeDtypeStruct((n, LANES), F32)] * 3,
        grid=(n // tm,),
        in_specs=[spec, pl.BlockSpec((1, LANES), lambda i: (0, 0))],
        out_specs=[spec, spec, spec],
        compiler_params=_cparams(("parallel",)),
        name="rope_tables",
    )(pos_b, inv_lane)


IN_STEP_TILES = 3
IN_STEP_W = IN_STEP_TILES * IN_TN
IN_STEPS = N_COL_TILES // IN_STEP_TILES


def _is_rope_tile(tile):
    return tile < COL_AV or COL_CQ <= tile < COL_CV


_STEP_ROPES = tuple(tuple(_is_rope_tile(step * IN_STEP_TILES + u) for u in range(IN_STEP_TILES))
                    for step in range(IN_STEPS))


def _in_proj_kernel(x_ref, g_ref, w_ref, b_ref, c_ref, s1_ref, s2_ref, z_ref, h_ref,
                    acc0_ref, acc1_ref):
    j = pl.program_id(1)

    @pl.when(j == 0)
    def _():
        x = x_ref[0]
        ms = jnp.mean(x * x, -1, keepdims=True)
        h_ref[...] = (x * lax.rsqrt(ms + EPS) * g_ref[...]).astype(BF16)

    prev = j - 1

    def matmul_into(acc_ref):
        acc_ref[...] = jnp.dot(h_ref[...], w_ref[0], preferred_element_type=F32) + b_ref[0]

    def finish(acc_ref, ropes):
        for u, rope in enumerate(ropes):
            cols = slice(u * IN_TN, (u + 1) * IN_TN)
            if not rope:
                z_ref[0, :, cols] = acc_ref[:, cols].astype(BF16)
                continue
            tile = prev * IN_STEP_TILES + u
            is_q = (tile < COL_AK) | ((tile >= COL_CQ) & (tile < COL_CK))
            scale = jnp.where(is_q, HEAD_DIM ** -0.5, 1.0).astype(F32)
            for t in range(u * IN_TN // LANES, (u + 1) * IN_TN // LANES):
                lanes = slice(t * LANES, (t + 1) * LANES)
                a = acc_ref[:, lanes]
                r = (a * c_ref[...] + pltpu.roll(a, LANES - ROT_DIM // 2, 1) * s1_ref[...]
                     + pltpu.roll(a, ROT_DIM // 2, 1) * s2_ref[...])
                z_ref[0, :, lanes] = (r * scale).astype(BF16)

    accs = (acc0_ref, acc1_ref)
    patterns = sorted(set(_STEP_ROPES))
    pattern_id = jnp.int32(0)
    for step, ropes in enumerate(_STEP_ROPES):
        pattern_id = jnp.where(prev == step, patterns.index(ropes), pattern_id)
    inner = (j > 0) & (j < IN_STEPS)
    for parity in range(2):
        for pid, ropes in enumerate(patterns):
            @pl.when(inner & (j % 2 == parity) & (pattern_id == pid))
            def _(parity=parity, ropes=ropes):
                matmul_into(accs[parity])
                finish(accs[1 - parity], ropes)

    @pl.when(j == 0)
    def _():
        matmul_into(accs[0])

    @pl.when(j == IN_STEPS)
    def _():
        finish(accs[(IN_STEPS - 1) % 2], _STEP_ROPES[-1])


def _in_proj(x, g, w_all, b_all, layer, tabs):
    bsz, s, d = x.shape
    c, s1, s2 = tabs
    tab_spec = pl.BlockSpec((s, LANES), lambda bi, j: (bi, 0))
    last = IN_STEPS - 1
    return pl.pallas_call(
        _in_proj_kernel,
        out_shape=jax.ShapeDtypeStruct((bsz, s, D_IN), BF16),
        grid=(bsz, IN_STEPS + 1),
        in_specs=[
            pl.BlockSpec((1, s, d), lambda bi, j: (bi, 0, 0)),
            pl.BlockSpec((1, d), lambda bi, j: (0, 0)),
            pl.BlockSpec((1, d, IN_STEP_W), lambda bi, j: (layer, 0, jnp.minimum(j, last))),
            pl.BlockSpec((1, 1, IN_STEP_W), lambda bi, j: (layer, 0, jnp.minimum(j, last))),
            tab_spec, tab_spec, tab_spec,
        ],
        out_specs=pl.BlockSpec((1, s, IN_STEP_W), lambda bi, j: (bi, 0, jnp.maximum(j - 1, 0))),
        scratch_shapes=[pltpu.VMEM((s, d), BF16), pltpu.VMEM((s, IN_STEP_W), F32),
                        pltpu.VMEM((s, IN_STEP_W), F32)],
        compiler_params=_cparams(("parallel", "arbitrary")),
        name="in_proj",
    )(x, g.reshape(1, d), w_all, b_all, c, s1, s2)


STACK = C_HEADS_PER_GROUP * BLOCK
STEP_UNROLL = 4


def _head_stack(blk):
    seg = lax.broadcasted_iota(jnp.int32, (1, MXU_W), 1) // HEAD_DIM
    rowseg = lax.broadcasted_iota(jnp.int32, (STACK, 1), 0) // BLOCK
    return jnp.where(seg == rowseg, jnp.concatenate([blk] * C_HEADS_PER_GROUP, axis=0),
                     jnp.zeros((), blk.dtype))


def _band_step(q, kp_ref, vp_ref, blk, n, has_prev, diag_key, sink_col, want_lse):
    return _band_blocks([(q, blk, n)], kp_ref, vp_ref, has_prev, diag_key, sink_col, want_lse)[0]


def _band_blocks(jobs, kp_ref, vp_ref, has_prev, diag_key, sink_col, want_lse):
    contract = (((1,), (1,)), ((), ()))
    i_loc = lax.broadcasted_iota(jnp.int32, (STACK, 1), 0) & (BLOCK - 1)
    jj = lax.broadcasted_iota(jnp.int32, (1, BLOCK), 1)
    upper = jj > i_loc
    seg = lax.broadcasted_iota(jnp.int32, (1, MXU_W), 1) // HEAD_DIM
    zero_b = jnp.zeros((), BF16)

    def key_rows(blk):
        lo = pl.multiple_of(blk * BLOCK, BLOCK)
        if has_prev:
            return pl.ds(lo, 2 * BLOCK)
        return pl.ds(pl.multiple_of(lo + BLOCK, BLOCK), BLOCK)

    scores = [lax.dot_general(_head_stack(q), kp_ref[key_rows(blk), :], contract,
                              preferred_element_type=F32) for q, blk, _ in jobs]

    def own_lanes(x):
        out = x[0:BLOCK]
        for h in range(1, C_HEADS_PER_GROUP):
            out = jnp.where(seg == h, x[h * BLOCK:(h + 1) * BLOCK], out)
        return out

    probs, stats = [], []
    for s, (_, blk, n) in zip(scores, jobs):
        if has_prev:
            sp = s[:, :BLOCK] + jnp.where(n == 0, NEG_INF, 0.0).astype(F32)
            f = jnp.where(upper, sp, s[:, BLOCK:])
        else:
            f = jnp.where(upper, NEG_INF, s)
        m = jnp.max(f, -1, keepdims=True)
        if diag_key:
            sd = jnp.sum(jnp.where(jj == i_loc, sp, 0.0), -1, keepdims=True)
            m = jnp.maximum(m, sd)
        if sink_col is not None:
            m = jnp.maximum(m, sink_col)
        p = jnp.exp(f - m)
        den = jnp.sum(p, -1, keepdims=True)
        pd = None
        if diag_key:
            pd = jnp.exp(sd - m)
            den = den + pd
        if sink_col is not None:
            den = den + jnp.exp(sink_col - m)
        inv = 1.0 / den
        pn = (p * inv).astype(BF16)
        if has_prev:
            below = zero_b
            if diag_key:
                below = jnp.where(jj == i_loc, pd * inv, 0.0).astype(BF16)
            pn = jnp.concatenate([jnp.where(upper, pn, below), jnp.where(upper, zero_b, pn)],
                                 axis=1)
        probs.append(pn)
        stats.append((m, den))

    outs = []
    for pn, (m, den), (_, blk, _) in zip(probs, stats, jobs):
        o = own_lanes(jnp.dot(pn, vp_ref[key_rows(blk), :], preferred_element_type=F32))
        lse = own_lanes(m + jnp.log(den)) + jnp.zeros((BLOCK, MXU_W), F32) if want_lse else None
        outs.append((o, lse))
    return outs


def _replicate_head(blk, g):
    words = pltpu.bitcast(blk, jnp.uint32)
    tile = words[:, (g // 2) * LANES:(g // 2 + 1) * LANES]
    swapped = pltpu.roll(tile, HEAD_DIM, 1)
    low = lax.broadcasted_iota(jnp.int32, (1, LANES), 1) < HEAD_DIM
    both = jnp.where(low, tile, swapped) if g % 2 == 0 else jnp.where(low, swapped, tile)
    return pltpu.bitcast(jnp.concatenate([both, both], axis=1), BF16)


def _attn_a_kernel(sink_ref, q_ref, k_ref, v_ref, o_ref, kp_ref, vp_ref):
    nb = q_ref.shape[1] // BLOCK
    rowseg = lax.broadcasted_iota(jnp.int32, (STACK, 1), 0) // BLOCK
    kp_ref[pl.ds(0, BLOCK), :] = jnp.zeros((BLOCK, MXU_W), BF16)
    vp_ref[pl.ds(0, BLOCK), :] = jnp.zeros((BLOCK, MXU_W), BF16)
    for g in range(A_KV_HEADS):
        kv_cols = slice(g * HEAD_DIM, (g + 1) * HEAD_DIM)
        q_cols = slice(g * MXU_W, (g + 1) * MXU_W)

        def build(n, carry):
            rows = pl.ds(pl.multiple_of(n * BLOCK, BLOCK), BLOCK)
            dst = pl.ds(pl.multiple_of((n + 1) * BLOCK, BLOCK), BLOCK)
            kp_ref[dst, :] = _replicate_head(k_ref[0, rows, :], g)
            vp_ref[dst, :] = _replicate_head(v_ref[0, rows, :], g)
            return carry

        lax.fori_loop(0, nb, build, 0, unroll=8)
        sink_col = jnp.zeros((STACK, 1), F32)
        for r in range(A_REP):
            sink_col = jnp.where(rowseg == r, sink_ref[g * A_REP + r], sink_col)

        def step(it, carry):
            blocks = [it * STEP_UNROLL + u for u in range(STEP_UNROLL)]
            rows = [pl.ds(pl.multiple_of(n * BLOCK, BLOCK), BLOCK) for n in blocks]
            jobs = [(q_ref[0, r, q_cols], n, n) for r, n in zip(rows, blocks)]
            outs = _band_blocks(jobs, kp_ref, vp_ref, True, False, sink_col, False)
            for r, (o, _) in zip(rows, outs):
                o_ref[0, r, q_cols] = o.astype(o_ref.dtype)
            return carry

        lax.fori_loop(0, nb // STEP_UNROLL, step, 0)


def _attn_a(z, sinks):
    bsz, s, _ = z.shape
    nb = s // BLOCK
    grid_spec = pltpu.PrefetchScalarGridSpec(
        num_scalar_prefetch=1,
        grid=(bsz,),
        in_specs=[
            pl.BlockSpec((1, s, A_Q_W), lambda bi, sk: (bi, 0, 0)),
            pl.BlockSpec((1, s, A_KV_W), lambda bi, sk: (bi, 0, COL_AK)),
            pl.BlockSpec((1, s, A_KV_W), lambda bi, sk: (bi, 0, COL_AV)),
        ],
        out_specs=pl.BlockSpec((1, s, A_Q_W), lambda bi, sk: (bi, 0, 0)),
        scratch_shapes=[pltpu.VMEM(((nb + 1) * BLOCK, MXU_W), BF16),
                        pltpu.VMEM(((nb + 1) * BLOCK, MXU_W), BF16)],
    )
    return pl.pallas_call(
        _attn_a_kernel,
        out_shape=jax.ShapeDtypeStruct((bsz, s, A_Q_W), BF16),
        grid_spec=grid_spec,
        compiler_params=_cparams(("parallel",)),
        name="attn_a",
    )(sinks, z, z, z)


def _attn_c_kernel(q_ref, k_ref, v_ref, oc_ref, stage_ref, pq_ref, pk_ref, pv_ref, og_ref, lg_ref):
    s = q_ref.shape[1]
    nblk = s // BLOCK
    halves = MXU_W // LANES
    pk_ref[pl.ds(0, BLOCK), :] = jnp.zeros((BLOCK, MXU_W), BF16)
    pv_ref[pl.ds(0, BLOCK), :] = jnp.zeros((BLOCK, MXU_W), BF16)

    def class_rows(idx, nb, d):
        r = idx // nb
        n = idx - r * nb
        start = r + n * (BLOCK * d)
        rows = pl.ds(start, BLOCK, stride=d) if d > 1 else pl.ds(pl.multiple_of(start, BLOCK), BLOCK)
        return n, rows

    def stage(src_ref, cols):
        x = src_ref[0, :, cols].astype(F32)
        for t in range(halves):
            stage_ref[t] = x[:, t * LANES:(t + 1) * LANES]

    def staged_block(rows):
        return jnp.concatenate([stage_ref[t, rows, :] for t in range(halves)], axis=1).astype(BF16)

    for g, (win, d) in enumerate(C_PAIRS):
        cols = slice(g * MXU_W, (g + 1) * MXU_W)
        nb = nblk // d
        has_prev = nb > 1
        assert win // d == BLOCK

        def class_major(dst_ref, offset):
            def build(idx, carry):
                _, rows = class_rows(idx, nb, d)
                dst = pl.ds(pl.multiple_of((idx + offset) * BLOCK, BLOCK), BLOCK)
                dst_ref[dst, :] = staged_block(rows)
                return carry
            return build

        for src_ref, dst_ref, offset in ((k_ref, pk_ref, 1), (v_ref, pv_ref, 1), (q_ref, pq_ref, 0)):
            stage(src_ref, cols)
            lax.fori_loop(0, nblk, class_major(dst_ref, offset), 0, unroll=4)

        def step(it, carry):
            jobs, dsts = [], []
            for u in range(STEP_UNROLL):
                idx = it * STEP_UNROLL + u
                n, rows = class_rows(idx, nb, d)
                here = pl.ds(pl.multiple_of(idx * BLOCK, BLOCK), BLOCK)
                jobs.append((pq_ref[here, :], idx, n))
                dsts.append(rows)
            outs = _band_blocks(jobs, pk_ref, pv_ref, has_prev, has_prev, None, True)
            for rows, (o, lse) in zip(dsts, outs):
                for t in range(halves):
                    og_ref[g, t, rows, :] = o[:, t * LANES:(t + 1) * LANES]
                    lg_ref[g, t, rows, :] = lse[:, t * LANES:(t + 1) * LANES]
            return carry

        lax.fori_loop(0, nblk // STEP_UNROLL, step, 0)

    chunk = 2 * BLOCK

    def combine(i, carry):
        rows = pl.ds(pl.multiple_of(i * chunk, chunk), chunk)
        for t in range(halves):
            ls = [lg_ref[g, t, rows, :] for g in range(len(C_PAIRS))]
            m = jnp.maximum(jnp.maximum(ls[0], ls[1]), ls[2])
            es = [jnp.exp(l - m) for l in ls]
            num = (es[0] * og_ref[0, t, rows, :] + es[1] * og_ref[1, t, rows, :]
                   + es[2] * og_ref[2, t, rows, :])
            oc_ref[0, rows, t * LANES:(t + 1) * LANES] = (
                num / (es[0] + es[1] + es[2])).astype(oc_ref.dtype)
        return carry

    lax.fori_loop(0, s // chunk, combine, 0)


def _attn_c(z):
    bsz, s, _ = z.shape
    nblk = s // BLOCK
    width = C_HEADS * HEAD_DIM
    halves = MXU_W // LANES

    def in_spec(col):
        return pl.BlockSpec((1, s, width), lambda bi: (bi, 0, col * IN_TN // width))

    return pl.pallas_call(
        _attn_c_kernel,
        out_shape=jax.ShapeDtypeStruct((bsz, s, C_OUT_W), BF16),
        grid=(bsz,),
        in_specs=[in_spec(COL_CQ), in_spec(COL_CK), in_spec(COL_CV)],
        out_specs=pl.BlockSpec((1, s, C_OUT_W), lambda bi: (bi, 0, 0)),
        scratch_shapes=[
            pltpu.VMEM((halves, s, LANES), F32),
            pltpu.VMEM((s, MXU_W), BF16),
            pltpu.VMEM((s + BLOCK, MXU_W), BF16),
            pltpu.VMEM((s + BLOCK, MXU_W), BF16),
            pltpu.VMEM((len(C_PAIRS), halves, s, LANES), F32),
            pltpu.VMEM((len(C_PAIRS), halves, s, LANES), F32),
        ],
        compiler_params=_cparams(("parallel",)),
        name="attn_c",
    )(z, z, z)


def _sgu_kernel(zu_ref, zv_ref, lng_ref, lnb_ref, ws_ref, bs_ref, o_ref):
    tm = zu_ref.shape[1]
    u = jax.nn.gelu(zu_ref[0].astype(F32))
    v = jax.nn.gelu(zv_ref[0].astype(F32))
    mu = jnp.mean(v, -1, keepdims=True)
    var = jnp.mean(jnp.square(v - mu), -1, keepdims=True)
    vn = ((v - mu) * lax.rsqrt(var + EPS) * lng_ref[...] + lnb_ref[...]).astype(BF16)
    causal = (lax.broadcasted_iota(jnp.int32, (B_CHUNK, 1), 0)
              >= lax.broadcasted_iota(jnp.int32, (1, B_CHUNK), 1))
    first = lax.broadcasted_iota(jnp.int32, (1, LANES), 1) < B_CH
    zero = jnp.zeros((), BF16)
    for p in range(B_WIDTH // LANES):
        w0 = jnp.where(causal, ws_ref[2 * p], zero)
        w1 = jnp.where(causal, ws_ref[2 * p + 1], zero)
        cols = slice(p * LANES, (p + 1) * LANES)
        for c in range(tm // B_CHUNK):
            rows = slice(c * B_CHUNK, (c + 1) * B_CHUNK)
            vv = vn[rows, cols]
            sv = jnp.where(first,
                           jnp.dot(w0, vv, preferred_element_type=F32),
                           jnp.dot(w1, vv, preferred_element_type=F32)) + bs_ref[:, cols]
            o_ref[0, rows, cols] = (u[rows, cols] * sv).astype(o_ref.dtype)


def _sgu(z, ln_g, ln_b, w_s, b_s):
    bsz, s, _ = z.shape
    tm = 512
    bias = jnp.repeat(b_s.T, B_CH, axis=1)
    col_u = COL_BU * IN_TN // B_WIDTH
    col_v = COL_BV * IN_TN // B_WIDTH
    return pl.pallas_call(
        _sgu_kernel,
        out_shape=jax.ShapeDtypeStruct((bsz, s, B_WIDTH), BF16),
        grid=(bsz, s // tm),
        in_specs=[
            pl.BlockSpec((1, tm, B_WIDTH), lambda bi, i: (bi, i, col_u)),
            pl.BlockSpec((1, tm, B_WIDTH), lambda bi, i: (bi, i, col_v)),
            pl.BlockSpec((1, B_WIDTH), lambda bi, i: (0, 0)),
            pl.BlockSpec((1, B_WIDTH), lambda bi, i: (0, 0)),
            pl.BlockSpec((B_GROUPS, B_CHUNK, B_CHUNK), lambda bi, i: (0, 0, 0)),
            pl.BlockSpec((B_CHUNK, B_WIDTH), lambda bi, i: (0, 0)),
        ],
        out_specs=pl.BlockSpec((1, tm, B_WIDTH), lambda bi, i: (bi, i, 0)),
        compiler_params=_cparams(("parallel", "parallel")),
        name="sgu",
    )(z, z, ln_g.reshape(1, B_WIDTH), ln_b.reshape(1, B_WIDTH), w_s.astype(BF16), bias)


def _route(logits):
    lane = lax.broadcasted_iota(jnp.int32, (1, ROUTE_W), 1)
    lane_f = lane.astype(F32)
    is_g = lane < N_EXPERT_GROUPS
    lg = jnp.where(is_g, logits, NEG_INF)
    mg = jnp.max(lg, -1, keepdims=True)
    pg_top = 1.0 / jnp.sum(jnp.where(is_g, jnp.exp(lg - mg), 0.0), -1, keepdims=True)
    g_idx = jnp.min(jnp.where(lg == mg, lane_f, float(ROUTE_W)), -1, keepdims=True)
    e_group = ((lane - ROUTE_E0) // EXPERTS_PER_GROUP).astype(F32)
    in_grp = (lane >= ROUTE_E0) & (lane < ROUTE_E0 + N_EXPERTS) & (e_group == g_idx)
    le = jnp.where(in_grp, logits, NEG_INF)
    m1 = jnp.max(le, -1, keepdims=True)
    i1 = jnp.min(jnp.where(le == m1, lane_f, float(ROUTE_W)), -1, keepdims=True)
    le2 = jnp.where(lane_f == i1, NEG_INF, le)
    m2 = jnp.max(le2, -1, keepdims=True)
    i2 = jnp.min(jnp.where(le2 == m2, lane_f, float(ROUTE_W)), -1, keepdims=True)
    t = jnp.exp(m2 - m1)
    w1 = pg_top / (1.0 + t)
    w2 = w1 * t
    picks = (i1 - ROUTE_E0, i2 - ROUTE_E0, w1, w2)
    out = jnp.zeros(logits.shape, F32)
    for k, val in enumerate(picks):
        out = jnp.where(lane == k, val, out)
    return out


def _merge_kernel(x_ref, oa_ref, ob_ref, oc_ref,
                  g0_ref, g1_ref, g2_ref, g3_ref, wa_ref, wb_ref, wc_ref, wo_ref, n2_ref, wr_ref,
                  xo_ref, h2_ref, route_ref, counts_ref, pos_ref, xn0_ref, xn1_ref, cnt_ref):
    i = pl.program_id(0)
    n_tiles = pl.num_programs(0) - 1

    def residual_into(xn_ref):
        zg = jnp.concatenate([g0_ref[...], g1_ref[...], g2_ref[...], g3_ref[...]],
                             axis=1).astype(F32)
        gates = jax.nn.sigmoid(zg)
        pa = jnp.dot(oa_ref[...], wa_ref[...], preferred_element_type=F32)
        pb = jnp.dot(ob_ref[...], wb_ref[...], preferred_element_type=F32)
        pc = jnp.dot(oc_ref[...], wc_ref[...], preferred_element_type=F32)
        merged = (gates[:, :D_MODEL] * pa + gates[:, D_MODEL:2 * D_MODEL] * pb
                  + gates[:, 2 * D_MODEL:] * pc)
        xn_ref[...] = x_ref[...] + jnp.dot(merged.astype(BF16), wo_ref[...],
                                           preferred_element_type=F32)

    def tail_from(xn_ref):
        xn = xn_ref[...]
        xo_ref[...] = xn
        ms = jnp.mean(xn * xn, -1, keepdims=True)
        h2 = xn * lax.rsqrt(ms + EPS) * n2_ref[...]
        h2_ref[...] = _pack_bf16_pairs(h2)
        route = _route(jnp.dot(h2.astype(BF16), wr_ref[...], preferred_element_type=F32))
        tm = route.shape[0]
        lane = lax.broadcasted_iota(jnp.int32, (1, ROUTE_W), 1)
        lane_f = lane.astype(F32)
        row = lax.broadcasted_iota(jnp.int32, (tm, 1), 0)
        hit = [lane_f == route[:, k:k + 1] for k in range(TOP_K)]
        onehot = jnp.where(hit[0], 1.0, 0.0) + jnp.where(hit[1], 1.0, 0.0)
        scan = onehot
        shift = 1
        while shift < tm:
            scan = scan + jnp.where(row >= shift, pltpu.roll(scan, shift, 0), 0.0)
            shift *= 2
        before = scan - onehot + cnt_ref[0:1, :]
        capacity = pos_ref.shape[1] * LANES
        diag = (row & (LANES - 1)) == lane
        first_row = pl.multiple_of((i - 1) * (tm // LANES), tm // LANES)
        for k in range(TOP_K):
            rank = jnp.sum(jnp.where(hit[k], before, 0.0), -1, keepdims=True)
            slot = jnp.where(diag, route[:, k:k + 1] * capacity + rank, 0.0)
            dense = jnp.concatenate(
                [jnp.sum(slot[r * LANES:(r + 1) * LANES], axis=0, keepdims=True)
                 for r in range(tm // LANES)], axis=0)
            pos_ref[k, pl.ds(first_row, tm // LANES), :] = dense.astype(jnp.int32)
        route_ref[...] = route
        cnt = cnt_ref[0:1, :] + jnp.sum(onehot, axis=0, keepdims=True)
        cnt_ref[...] = jnp.broadcast_to(cnt, cnt_ref.shape)
        counts_ref[...] = jnp.broadcast_to(cnt, counts_ref.shape)

    bufs = (xn0_ref, xn1_ref)
    for parity in range(2):
        @pl.when((i > 0) & (i < n_tiles) & (i % 2 == parity))
        def _(parity=parity):
            tail_from(bufs[1 - parity])
            residual_into(bufs[parity])

        @pl.when((i == n_tiles) & (i % 2 == parity))
        def _(parity=parity):
            tail_from(bufs[1 - parity])

    @pl.when(i == 0)
    def _():
        cnt_ref[...] = jnp.zeros_like(cnt_ref)
        residual_into(bufs[0])


def _merge(x2, oa, ob, oc, z2, wa, wb, wc, wo, n2g, w_rg, w_re):
    n, d = x2.shape
    tm = 512
    wr = jnp.concatenate(
        [w_rg, w_re, jnp.zeros((d, ROUTE_W - N_EXPERT_GROUPS - N_EXPERTS), w_rg.dtype)], axis=1)
    gate_w = GATE_W // 4
    gate_col0 = COL_G * IN_TN // gate_w

    n_tiles = n // tm
    last = n_tiles - 1

    def rows(width):
        return pl.BlockSpec((tm, width), lambda i: (jnp.minimum(i, last), 0))

    def lagged_rows(width):
        return pl.BlockSpec((tm, width), lambda i: (jnp.maximum(i - 1, 0), 0))

    def full(shape):
        return pl.BlockSpec(shape, lambda i: (0, 0))

    gate_specs = [pl.BlockSpec((tm, gate_w), lambda i, k=k: (jnp.minimum(i, last), gate_col0 + k))
                  for k in range(4)]
    return pl.pallas_call(
        _merge_kernel,
        out_shape=[jax.ShapeDtypeStruct((n, d), F32),
                   jax.ShapeDtypeStruct((n, d // 2), jnp.int32),
                   jax.ShapeDtypeStruct((n, ROUTE_W), F32),
                   jax.ShapeDtypeStruct((SUBLANES, ROUTE_W), F32),
                   jax.ShapeDtypeStruct((TOP_K, n // LANES, LANES), jnp.int32)],
        grid=(n_tiles + 1,),
        in_specs=[rows(d), rows(A_Q_W), rows(B_WIDTH), rows(C_OUT_W),
                  *gate_specs,
                  full((A_Q_W, d)), full((B_WIDTH, d)), full((C_OUT_W, d)), full((d, d)),
                  full((1, d)), full((d, ROUTE_W))],
        out_specs=[lagged_rows(d), lagged_rows(d // 2), lagged_rows(ROUTE_W),
                   full((SUBLANES, ROUTE_W)),
                   pl.BlockSpec((TOP_K, n // LANES, LANES), lambda i: (0, 0, 0))],
        scratch_shapes=[pltpu.VMEM((tm, d), F32), pltpu.VMEM((tm, d), F32),
                        pltpu.VMEM((SUBLANES, ROUTE_W), F32)],
        compiler_params=_cparams(("arbitrary",)),
        name="merge",
    )(x2, oa, ob, oc, z2, z2, z2, z2,
      wa.astype(BF16), wb.astype(BF16), wc.astype(BF16), wo.astype(BF16),
      n2g.reshape(1, d), wr.astype(BF16))


MOE_TM = 1024
SC_CORES = 2
SC_SUBCORES = 16
SC_WORKERS = SC_CORES * SC_SUBCORES
SC_CHUNK = 128


def _pack_bf16_pairs(x):
    w = x.shape[1] // 2
    lo = pltpu.bitcast(x[:, :w].astype(BF16).astype(F32), jnp.int32)
    hi = pltpu.bitcast(x[:, w:].astype(BF16).astype(F32), jnp.int32)
    return (hi & jnp.int32(-65536)) | lax.shift_right_logical(lo, jnp.int32(16))


def _unpack_bf16_pairs(p):
    lo = pltpu.bitcast(lax.shift_left(p, jnp.int32(16)), F32)
    hi = pltpu.bitcast(p & jnp.int32(-65536), F32)
    return jnp.concatenate([lo, hi], axis=1)


def _sc_gather_rows(table, idx):
    rows, width = idx.shape[0], table.shape[1]
    per_worker = rows // SC_WORKERS
    n_chunks = per_worker // SC_CHUNK
    assert per_worker * SC_WORKERS == rows and n_chunks * SC_CHUNK == per_worker
    mesh = plsc.VectorSubcoreMesh(core_axis_name="c", subcore_axis_name="s",
                                  num_cores=SC_CORES, num_subcores=SC_SUBCORES)

    @functools.partial(
        pl.kernel, mesh=mesh,
        out_type=jax.ShapeDtypeStruct((rows, width), table.dtype),
        scratch_types=[pltpu.VMEM((n_chunks, SC_CHUNK), jnp.int32),
                       pltpu.VMEM((SC_CHUNK, width), table.dtype),
                       pltpu.SemaphoreType.DMA],
        name="sc_gather_rows",
    )
    def gather(table_hbm, idx_hbm, out_hbm, idx_v, rows_v, sem):
        wid = lax.axis_index("s") * SC_CORES + lax.axis_index("c")
        pltpu.sync_copy(idx_hbm.at[wid], idx_v)
        base = wid * per_worker

        @pl.loop(0, n_chunks)
        def _(c):
            pltpu.async_copy(table_hbm.at[idx_v.at[c]], rows_v, sem).wait()
            pltpu.sync_copy(rows_v, out_hbm.at[pl.ds(base + c * SC_CHUNK, SC_CHUNK)])

    return gather(table, idx.reshape(SC_WORKERS, n_chunks, SC_CHUNK))


def _sc_scatter_rows(rows, pos, n_slots):
    n, width = rows.shape
    per_worker = n // SC_WORKERS
    n_chunks = per_worker // SC_CHUNK
    assert per_worker * SC_WORKERS == n and n_chunks * SC_CHUNK == per_worker
    mesh = plsc.VectorSubcoreMesh(core_axis_name="c", subcore_axis_name="s",
                                  num_cores=SC_CORES, num_subcores=SC_SUBCORES)

    @functools.partial(
        pl.kernel, mesh=mesh,
        out_type=jax.ShapeDtypeStruct((n_slots, width), rows.dtype),
        scratch_types=[pltpu.VMEM((TOP_K, n_chunks, SC_CHUNK), jnp.int32),
                       pltpu.VMEM((SC_CHUNK, width), rows.dtype),
                       pltpu.SemaphoreType.DMA],
        name="sc_scatter_rows",
    )
    def scatter(rows_hbm, pos_hbm, out_hbm, pos_v, rows_v, sem):
        wid = lax.axis_index("s") * SC_CORES + lax.axis_index("c")
        for k in range(TOP_K):
            pltpu.sync_copy(pos_hbm.at[k, wid], pos_v.at[k])
        base = wid * per_worker

        @pl.loop(0, n_chunks)
        def _(c):
            pltpu.sync_copy(rows_hbm.at[pl.ds(base + c * SC_CHUNK, SC_CHUNK)], rows_v)
            for k in range(TOP_K):
                pltpu.async_copy(rows_v, out_hbm.at[pos_v.at[k, c]], sem).wait()

    return scatter(rows, pos.reshape(TOP_K, SC_WORKERS, n_chunks, SC_CHUNK))


def _dispatch_plan(counts, capacity, n_steps, expert_base):
    counts = counts[0, :N_EXPERTS].astype(jnp.int32)
    tiles = (counts + MOE_TM - 1) // MOE_TM
    tile_ends = jnp.cumsum(tiles)
    step = jnp.arange(n_steps, dtype=jnp.int32)
    used = step < tile_ends[-1]
    expert = jnp.minimum(jnp.sum((step[:, None] >= tile_ends[None, :]).astype(jnp.int32), axis=1),
                         N_EXPERTS - 1)
    chunk = step - (tile_ends - tiles)[expert]
    rows = jnp.where(used, jnp.clip(counts[expert] - chunk * MOE_TM, 0, MOE_TM), 0)
    first = jnp.concatenate([jnp.ones((1,), jnp.int32),
                             (expert[1:] != expert[:-1]).astype(jnp.int32)])
    block = expert * (capacity // MOE_TM) + chunk
    last_used = jnp.sum(jnp.where(step == tile_ends[-1] - 1, block, 0))
    spare = N_EXPERTS * (capacity // MOE_TM)
    blocks = jnp.stack([jnp.where(used, block, last_used), jnp.where(used, block, spare)])
    return expert + expert_base, rows.astype(jnp.int32), first, blocks.astype(jnp.int32)


def _moe_tile_kernel(te_ref, tr_ref, tf_ref, tb_ref, xs_ref, wg_ref, wu_ref, wd_ref, ys_ref,
                     wg_s, wu_s, wd_s):
    t = pl.program_id(0)

    @pl.when(tf_ref[t] != 0)
    def _():
        wg_s[...] = wg_ref[0].astype(BF16)
        wu_s[...] = wu_ref[0].astype(BF16)
        wd_s[...] = wd_ref[0].astype(BF16)

    @pl.when(tr_ref[t] != 0)
    def _():
        occupied = lax.broadcasted_iota(jnp.int32, (MOE_TM, 1), 0) < tr_ref[t]
        x = jnp.where(occupied, _unpack_bf16_pairs(xs_ref[...]), 0.0).astype(BF16)
        hg = jnp.dot(x, wg_s[...], preferred_element_type=F32)
        hu = jnp.dot(x, wu_s[...], preferred_element_type=F32)
        a = (jax.nn.silu(hg) * hu).astype(BF16)
        ys_ref[...] = _pack_bf16_pairs(jnp.dot(a, wd_s[...], preferred_element_type=F32))

    @pl.when(tr_ref[t] == 0)
    def _():
        ys_ref[...] = jnp.zeros_like(ys_ref)


def _moe_tiles(xs, tile_expert, tile_rows, tile_first, tile_blocks, w_gate, w_up, w_down):
    n_slots, half = xs.shape
    d = 2 * half
    grid_spec = pltpu.PrefetchScalarGridSpec(
        num_scalar_prefetch=4,
        grid=(tile_expert.shape[0],),
        in_specs=[
            pl.BlockSpec((MOE_TM, half), lambda t, te, tr, tf, tb: (tb[0, t], 0)),
            pl.BlockSpec((1, d, D_EXPERT), lambda t, te, tr, tf, tb: (te[t], 0, 0)),
            pl.BlockSpec((1, d, D_EXPERT), lambda t, te, tr, tf, tb: (te[t], 0, 0)),
            pl.BlockSpec((1, D_EXPERT, d), lambda t, te, tr, tf, tb: (te[t], 0, 0)),
        ],
        out_specs=pl.BlockSpec((MOE_TM, half), lambda t, te, tr, tf, tb: (tb[1, t], 0)),
        scratch_shapes=[pltpu.VMEM((d, D_EXPERT), BF16), pltpu.VMEM((d, D_EXPERT), BF16),
                        pltpu.VMEM((D_EXPERT, d), BF16)],
    )
    return pl.pallas_call(
        _moe_tile_kernel,
        out_shape=jax.ShapeDtypeStruct((n_slots, half), jnp.int32),
        grid_spec=grid_spec,
        compiler_params=_cparams(("arbitrary",)),
        name="moe_tiles",
    )(tile_expert, tile_rows, tile_first, tile_blocks, xs, w_gate, w_up, w_down)


def _moe_combine_kernel(x_ref, y0_ref, y1_ref, r_ref, g_ref, o_ref, *, final_norm):
    w0 = r_ref[:, 2:3]
    w1 = r_ref[:, 3:4]
    x = x_ref[...] + w0 * _unpack_bf16_pairs(y0_ref[0]) + w1 * _unpack_bf16_pairs(y1_ref[0])
    if final_norm:
        ms = jnp.mean(x * x, -1, keepdims=True)
        x = x * lax.rsqrt(ms + EPS) * g_ref[...]
    o_ref[...] = x


def _moe_combine(x2, yg, route, norm_g, final_norm):
    n, d = x2.shape
    tm = 1024
    return pl.pallas_call(
        functools.partial(_moe_combine_kernel, final_norm=final_norm),
        out_shape=jax.ShapeDtypeStruct((n, d), F32),
        grid=(n // tm,),
        in_specs=[
            pl.BlockSpec((tm, d), lambda i: (i, 0)),
            pl.BlockSpec((1, tm, d // 2), lambda i: (0, i, 0)),
            pl.BlockSpec((1, tm, d // 2), lambda i: (1, i, 0)),
            pl.BlockSpec((tm, ROUTE_W), lambda i: (i, 0)),
            pl.BlockSpec((1, d), lambda i: (0, 0)),
        ],
        out_specs=pl.BlockSpec((tm, d), lambda i: (i, 0)),
        compiler_params=_cparams(("parallel",)),
        name="moe_combine",
    )(x2, yg, yg, route, norm_g.reshape(1, d))


def _moe(h2p, route, counts, pos, x2, layer, w_gate, w_up, w_down, norm_f_g, final_norm):
    n, d = x2.shape
    n_slots = N_EXPERTS * n + MOE_TM
    n_steps = TOP_K * n // MOE_TM + N_EXPERTS
    tile_expert, tile_rows, tile_first, tile_blocks = _dispatch_plan(
        counts, n, n_steps, layer * N_EXPERTS)
    xs = _sc_scatter_rows(h2p, pos, n_slots)
    ys = _moe_tiles(xs, tile_expert, tile_rows, tile_first, tile_blocks, w_gate, w_up, w_down)
    yg = _sc_gather_rows(ys, pos.reshape(-1)).reshape(TOP_K, n, d // 2)
    return _moe_combine(x2, yg, route, norm_f_g, final_norm)


def kernel(x, positions, norm1_g, w_in, b_in, attn_sinks, sgu_ln_g, sgu_ln_b, w_spatial, b_spatial,
           w_proj_a, w_proj_b, w_proj_c, w_out, norm2_g, w_router_group, w_router_expert,
           w_expert_gate, w_expert_up, w_expert_down, norm_f_g):
    bsz, s, d = x.shape
    depth = w_in.shape[0]
    wg_all = w_expert_gate.reshape(depth * N_EXPERTS, d, D_EXPERT)
    wu_all = w_expert_up.reshape(depth * N_EXPERTS, d, D_EXPERT)
    wd_all = w_expert_down.reshape(depth * N_EXPERTS, D_EXPERT, d)
    w_in_b = w_in.astype(BF16)
    assert d == D_MODEL and s % (C_PAIRS[-1][1] * BLOCK) == 0
    tabs = _rope_tables(positions)
    for l in range(depth):
        z = _in_proj(x, norm1_g[l], w_in_b, b_in.reshape(depth, 1, D_IN), l, tabs)
        oa = _attn_a(z, attn_sinks[l])
        ob = _sgu(z, sgu_ln_g[l], sgu_ln_b[l], w_spatial[l], b_spatial[l])
        oc = _attn_c(z)
        x2, h2, route, counts, pos = _merge(
            x.reshape(bsz * s, d), oa.reshape(bsz * s, A_Q_W), ob.reshape(bsz * s, B_WIDTH),
            oc.reshape(bsz * s, C_OUT_W), z.reshape(bsz * s, D_IN), w_proj_a[l], w_proj_b[l], w_proj_c[l], w_out[l],
            norm2_g[l], w_router_group[l], w_router_expert[l])
        x = _moe(h2, route, counts, pos, x2, l, wg_all, wu_all, wd_all, norm_f_g, l == depth - 1).reshape(bsz, s, d)
    return x
```

```python
import functools

import jax
import jax.numpy as jnp
from jax import lax
from jax.experimental import pallas as pl
from jax.experimental.pallas import tpu as pltpu
from jax.experimental.pallas import tpu_sc as plsc

F32 = jnp.float32
BF16 = jnp.bfloat16

D_MODEL = 1024
HEAD_DIM = 64
ROT_DIM = HEAD_DIM // 4
ROPE_THETA = 500000.0
BLOCK = 128
EPS = 1e-5
NEG_INF = -1e30

A_Q_HEADS = 16
A_KV_HEADS = 4
A_REP = A_Q_HEADS // A_KV_HEADS
A_WINDOW = 128
A_Q_W = A_Q_HEADS * HEAD_DIM
A_KV_W = A_KV_HEADS * HEAD_DIM

B_GROUPS = 12
B_CH = 64
B_WIDTH = B_GROUPS * B_CH
B_CHUNK = 128

C_PAIRS = ((128, 1), (512, 4), (2048, 16))
C_HEADS_PER_GROUP = 4
C_HEADS = C_HEADS_PER_GROUP * len(C_PAIRS)
C_OUT_W = C_HEADS_PER_GROUP * HEAD_DIM

N_BRANCH = 3
GATE_W = N_BRANCH * D_MODEL
D_IN = A_Q_W + 2 * A_KV_W + 2 * B_WIDTH + 3 * C_HEADS * HEAD_DIM + GATE_W

N_EXPERT_GROUPS = 4
EXPERTS_PER_GROUP = 8
N_EXPERTS = N_EXPERT_GROUPS * EXPERTS_PER_GROUP
TOP_K = 2
D_EXPERT = 256

LANES = 128
MXU_W = 256
VMEM_LIMIT = 56 * 1024 * 1024

IN_TN = 256
COL_AQ = 0
COL_AK = A_Q_W // IN_TN
COL_AV = COL_AK + A_KV_W // IN_TN
COL_BU = COL_AV + A_KV_W // IN_TN
COL_BV = COL_BU + B_WIDTH // IN_TN
COL_CQ = COL_BV + B_WIDTH // IN_TN
COL_CK = COL_CQ + C_HEADS * HEAD_DIM // IN_TN
COL_CV = COL_CK + C_HEADS * HEAD_DIM // IN_TN
COL_G = COL_CV + C_HEADS * HEAD_DIM // IN_TN
N_COL_TILES = D_IN // IN_TN

ROUTE_W = LANES
ROUTE_E0 = N_EXPERT_GROUPS
ROUTE_RANK0 = 4
SUBLANES = 8


def _cparams(sem):
    return pltpu.CompilerParams(dimension_semantics=sem, vmem_limit_bytes=VMEM_LIMIT)


def _rope_table_kernel(pos_ref, inv_ref, c_ref, s1_ref, s2_ref):
    lane = lax.broadcasted_iota(jnp.int32, (1, LANES), 1)
    d = lane & (HEAD_DIM - 1)
    ang = pos_ref[...].astype(F32) * inv_ref[...]
    c = jnp.cos(ang)
    s = jnp.sin(ang)
    half = ROT_DIM // 2
    c_ref[...] = jnp.where(d < ROT_DIM, c, 1.0)
    s1_ref[...] = jnp.where(d < half, -s, 0.0)
    s2_ref[...] = jnp.where((d >= half) & (d < ROT_DIM), s, 0.0)


def _rope_tables(positions):
    n = positions.size
    inv = ROPE_THETA ** (-jnp.arange(0, ROT_DIM, 2, dtype=F32) / ROT_DIM)
    inv_lane = jnp.tile(inv, LANES // inv.shape[0]).reshape(1, LANES)
    pos_b = jnp.broadcast_to(positions.reshape(n, 1), (n, LANES))
    tm = 1024
    spec = pl.BlockSpec((tm, LANES), lambda i: (i, 0))
    return pl.pallas_call(
        _rope_table_kernel,
        out_shape=[jax.ShapeDtypeStruct((n, LANES), F32)] * 3,
        grid=(n // tm,),
        in_specs=[spec, pl.BlockSpec((1, LANES), lambda i: (0, 0))],
        out_specs=[spec, spec, spec],
        compiler_params=_cparams(("parallel",)),
        name="rope_tables",
    )(pos_b, inv_lane)


IN_STEP_TILES = 3
IN_STEP_W = IN_STEP_TILES * IN_TN
IN_STEPS = N_COL_TILES // IN_STEP_TILES


def _is_rope_tile(tile):
    return tile < COL_AV or COL_CQ <= tile < COL_CV


_STEP_ROPES = tuple(tuple(_is_rope_tile(step * IN_STEP_TILES + u) for u in range(IN_STEP_TILES))
                    for step in range(IN_STEPS))


def _in_proj_kernel(x_ref, g_ref, w_ref, b_ref, c_ref, s1_ref, s2_ref, z_ref, h_ref,
                    acc0_ref, acc1_ref):
    j = pl.program_id(1)

    @pl.when(j == 0)
    def _():
        x = x_ref[0]
        ms = jnp.mean(x * x, -1, keepdims=True)
        h_ref[...] = (x * lax.rsqrt(ms + EPS) * g_ref[...]).astype(BF16)

    prev = j - 1

    def matmul_into(acc_ref):
        acc_ref[...] = jnp.dot(h_ref[...], w_ref[0], preferred_element_type=F32) + b_ref[0]

    def finish(acc_ref, ropes):
        for u, rope in enumerate(ropes):
            cols = slice(u * IN_TN, (u + 1) * IN_TN)
            if not rope:
                z_ref[0, :, cols] = acc_ref[:, cols].astype(BF16)
                continue
            tile = prev * IN_STEP_TILES + u
            is_q = (tile < COL_AK) | ((tile >= COL_CQ) & (tile < COL_CK))
            scale = jnp.where(is_q, HEAD_DIM ** -0.5, 1.0).astype(F32)
            for t in range(u * IN_TN // LANES, (u + 1) * IN_TN // LANES):
                lanes = slice(t * LANES, (t + 1) * LANES)
                a = acc_ref[:, lanes]
                r = (a * c_ref[...] + pltpu.roll(a, LANES - ROT_DIM // 2, 1) * s1_ref[...]
                     + pltpu.roll(a, ROT_DIM // 2, 1) * s2_ref[...])
                z_ref[0, :, lanes] = (r * scale).astype(BF16)

    accs = (acc0_ref, acc1_ref)
    patterns = sorted(set(_STEP_ROPES))
    pattern_id = jnp.int32(0)
    for step, ropes in enumerate(_STEP_ROPES):
        pattern_id = jnp.where(prev == step, patterns.index(ropes), pattern_id)
    inner = (j > 0) & (j < IN_STEPS)
    for parity in range(2):
        for pid, ropes in enumerate(patterns):
            @pl.when(inner & (j % 2 == parity) & (pattern_id == pid))
            def _(parity=parity, ropes=ropes):
                matmul_into(accs[parity])
                finish(accs[1 - parity], ropes)

    @pl.when(j == 0)
    def _():
        matmul_into(accs[0])

    @pl.when(j == IN_STEPS)
    def _():
        finish(accs[(IN_STEPS - 1) % 2], _STEP_ROPES[-1])


def _in_proj(x, g, w_all, b_all, layer, tabs):
    bsz, s, d = x.shape
    c, s1, s2 = tabs
    tab_spec = pl.BlockSpec((s, LANES), lambda bi, j: (bi, 0))
    last = IN_STEPS - 1
    return pl.pallas_call(
        _in_proj_kernel,
        out_shape=jax.ShapeDtypeStruct((bsz, s, D_IN), BF16),
        grid=(bsz, IN_STEPS + 1),
        in_specs=[
            pl.BlockSpec((1, s, d), lambda bi, j: (bi, 0, 0)),
            pl.BlockSpec((1, d), lambda bi, j: (0, 0)),
            pl.BlockSpec((1, d, IN_STEP_W), lambda bi, j: (layer, 0, jnp.minimum(j, last))),
            pl.BlockSpec((1, 1, IN_STEP_W), lambda bi, j: (layer, 0, jnp.minimum(j, last))),
            tab_spec, tab_spec, tab_spec,
        ],
        out_specs=pl.BlockSpec((1, s, IN_STEP_W), lambda bi, j: (bi, 0, jnp.maximum(j - 1, 0))),
        scratch_shapes=[pltpu.VMEM((s, d), BF16), pltpu.VMEM((s, IN_STEP_W), F32),
                        pltpu.VMEM((s, IN_STEP_W), F32)],
        compiler_params=_cparams(("parallel", "arbitrary")),
        name="in_proj",
    )(x, g.reshape(1, d), w_all, b_all, c, s1, s2)


STACK = C_HEADS_PER_GROUP * BLOCK
STEP_UNROLL = 4


def _head_stack(blk):
    seg = lax.broadcasted_iota(jnp.int32, (1, MXU_W), 1) // HEAD_DIM
    rowseg = lax.broadcasted_iota(jnp.int32, (STACK, 1), 0) // BLOCK
    return jnp.where(seg == rowseg, jnp.concatenate([blk] * C_HEADS_PER_GROUP, axis=0),
                     jnp.zeros((), blk.dtype))


def _band_step(q, kp_ref, vp_ref, blk, n, has_prev, diag_key, sink_col, want_lse):
    return _band_blocks([(q, blk, n)], kp_ref, vp_ref, has_prev, diag_key, sink_col, want_lse)[0]


def _band_blocks(jobs, kp_ref, vp_ref, has_prev, diag_key, sink_col, want_lse):
    contract = (((1,), (1,)), ((), ()))
    i_loc = lax.broadcasted_iota(jnp.int32, (STACK, 1), 0) & (BLOCK - 1)
    jj = lax.broadcasted_iota(jnp.int32, (1, BLOCK), 1)
    upper = jj > i_loc
    seg = lax.broadcasted_iota(jnp.int32, (1, MXU_W), 1) // HEAD_DIM
    zero_b = jnp.zeros((), BF16)

    def key_rows(blk):
        lo = pl.multiple_of(blk * BLOCK, BLOCK)
        if has_prev:
            return pl.ds(lo, 2 * BLOCK)
        return pl.ds(pl.multiple_of(lo + BLOCK, BLOCK), BLOCK)

    scores = [lax.dot_general(_head_stack(q), kp_ref[key_rows(blk), :], contract,
                              preferred_element_type=F32) for q, blk, _ in jobs]

    def own_lanes(x):
        out = x[0:BLOCK]
        for h in range(1, C_HEADS_PER_GROUP):
            out = jnp.where(seg == h, x[h * BLOCK:(h + 1) * BLOCK], out)
        return out

    probs, stats = [], []
    for s, (_, blk, n) in zip(scores, jobs):
        if has_prev:
            sp = s[:, :BLOCK] + jnp.where(n == 0, NEG_INF, 0.0).astype(F32)
            f = jnp.where(upper, sp, s[:, BLOCK:])
        else:
            f = jnp.where(upper, NEG_INF, s)
        m = jnp.max(f, -1, keepdims=True)
        if diag_key:
            sd = jnp.sum(jnp.where(jj == i_loc, sp, 0.0), -1, keepdims=True)
            m = jnp.maximum(m, sd)
        if sink_col is not None:
            m = jnp.maximum(m, sink_col)
        p = jnp.exp(f - m)
        den = jnp.sum(p, -1, keepdims=True)
        pd = None
        if diag_key:
            pd = jnp.exp(sd - m)
            den = den + pd
        if sink_col is not None:
            den = den + jnp.exp(sink_col - m)
        inv = 1.0 / den
        pn = (p * inv).astype(BF16)
        if has_prev:
            below = zero_b
            if diag_key:
                below = jnp.where(jj == i_loc, pd * inv, 0.0).astype(BF16)
            pn = jnp.concatenate([jnp.where(upper, pn, below), jnp.where(upper, zero_b, pn)],
                                 axis=1)
        probs.append(pn)
        stats.append((m, den))

    outs = []
    for pn, (m, den), (_, blk, _) in zip(probs, stats, jobs):
        o = own_lanes(jnp.dot(pn, vp_ref[key_rows(blk), :], preferred_element_type=F32))
        lse = own_lanes(m + jnp.log(den)) + jnp.zeros((BLOCK, MXU_W), F32) if want_lse else None
        outs.append((o, lse))
    return outs


def _replicate_head(blk, g):
    words = pltpu.bitcast(blk, jnp.uint32)
    tile = words[:, (g // 2) * LANES:(g // 2 + 1) * LANES]
    swapped = pltpu.roll(tile, HEAD_DIM, 1)
    low = lax.broadcasted_iota(jnp.int32, (1, LANES), 1) < HEAD_DIM
    both = jnp.where(low, tile, swapped) if g % 2 == 0 else jnp.where(low, swapped, tile)
    return pltpu.bitcast(jnp.concatenate([both, both], axis=1), BF16)


def _attn_a_kernel(sink_ref, q_ref, k_ref, v_ref, o_ref, kp_ref, vp_ref):
    nb = q_ref.shape[1] // BLOCK
    rowseg = lax.broadcasted_iota(jnp.int32, (STACK, 1), 0) // BLOCK
    kp_ref[pl.ds(0, BLOCK), :] = jnp.zeros((BLOCK, MXU_W), BF16)
    vp_ref[pl.ds(0, BLOCK), :] = jnp.zeros((BLOCK, MXU_W), BF16)
    for g in range(A_KV_HEADS):
        kv_cols = slice(g * HEAD_DIM, (g + 1) * HEAD_DIM)
        q_cols = slice(g * MXU_W, (g + 1) * MXU_W)

        def build(n, carry):
            rows = pl.ds(pl.multiple_of(n * BLOCK, BLOCK), BLOCK)
            dst = pl.ds(pl.multiple_of((n + 1) * BLOCK, BLOCK), BLOCK)
            kp_ref[dst, :] = _replicate_head(k_ref[0, rows, :], g)
            vp_ref[dst, :] = _replicate_head(v_ref[0, rows, :], g)
            return carry

        lax.fori_loop(0, nb, build, 0, unroll=8)
        sink_col = jnp.zeros((STACK, 1), F32)
        for r in range(A_REP):
            sink_col = jnp.where(rowseg == r, sink_ref[g * A_REP + r], sink_col)

        def step(it, carry):
            blocks = [it * STEP_UNROLL + u for u in range(STEP_UNROLL)]
            rows = [pl.ds(pl.multiple_of(n * BLOCK, BLOCK), BLOCK) for n in blocks]
            jobs = [(q_ref[0, r, q_cols], n, n) for r, n in zip(rows, blocks)]
            outs = _band_blocks(jobs, kp_ref, vp_ref, True, False, sink_col, False)
            for r, (o, _) in zip(rows, outs):
                o_ref[0, r, q_cols] = o.astype(o_ref.dtype)
            return carry

        lax.fori_loop(0, nb // STEP_UNROLL, step, 0)


def _attn_a(z, sinks):
    bsz, s, _ = z.shape
    nb = s // BLOCK
    grid_spec = pltpu.PrefetchScalarGridSpec(
        num_scalar_prefetch=1,
        grid=(bsz,),
        in_specs=[
            pl.BlockSpec((1, s, A_Q_W), lambda bi, sk: (bi, 0, 0)),
            pl.BlockSpec((1, s, A_KV_W), lambda bi, sk: (bi, 0, COL_AK)),
            pl.BlockSpec((1, s, A_KV_W), lambda bi, sk: (bi, 0, COL_AV)),
        ],
        out_specs=pl.BlockSpec((1, s, A_Q_W), lambda bi, sk: (bi, 0, 0)),
        scratch_shapes=[pltpu.VMEM(((nb + 1) * BLOCK, MXU_W), BF16),
                        pltpu.VMEM(((nb + 1) * BLOCK, MXU_W), BF16)],
    )
    return pl.pallas_call(
        _attn_a_kernel,
        out_shape=jax.ShapeDtypeStruct((bsz, s, A_Q_W), BF16),
        grid_spec=grid_spec,
        compiler_params=_cparams(("parallel",)),
        name="attn_a",
    )(sinks, z, z, z)


def _attn_c_kernel(q_ref, k_ref, v_ref, oc_ref, stage_ref, pq_ref, pk_ref, pv_ref, og_ref, lg_ref):
    s = q_ref.shape[1]
    nblk = s // BLOCK
    halves = MXU_W // LANES
    pk_ref[pl.ds(0, BLOCK), :] = jnp.zeros((BLOCK, MXU_W), BF16)
    pv_ref[pl.ds(0, BLOCK), :] = jnp.zeros((BLOCK, MXU_W), BF16)

    def class_rows(idx, nb, d):
        r = idx // nb
        n = idx - r * nb
        start = r + n * (BLOCK * d)
        rows = pl.ds(start, BLOCK, stride=d) if d > 1 else pl.ds(pl.multiple_of(start, BLOCK), BLOCK)
        return n, rows

    def stage(src_ref, cols):
        x = src_ref[0, :, cols].astype(F32)
        for t in range(halves):
            stage_ref[t] = x[:, t * LANES:(t + 1) * LANES]

    def staged_block(rows):
        return jnp.concatenate([stage_ref[t, rows, :] for t in range(halves)], axis=1).astype(BF16)

    for g, (win, d) in enumerate(C_PAIRS):
        cols = slice(g * MXU_W, (g + 1) * MXU_W)
        nb = nblk // d
        has_prev = nb > 1
        assert win // d == BLOCK

        def class_major(dst_ref, offset):
            def build(idx, carry):
                _, rows = class_rows(idx, nb, d)
                dst = pl.ds(pl.multiple_of((idx + offset) * BLOCK, BLOCK), BLOCK)
                dst_ref[dst, :] = staged_block(rows)
                return carry
            return build

        for src_ref, dst_ref, offset in ((k_ref, pk_ref, 1), (v_ref, pv_ref, 1), (q_ref, pq_ref, 0)):
            stage(src_ref, cols)
            lax.fori_loop(0, nblk, class_major(dst_ref, offset), 0, unroll=4)

        def step(it, carry):
            jobs, dsts = [], []
            for u in range(STEP_UNROLL):
                idx = it * STEP_UNROLL + u
                n, rows = class_rows(idx, nb, d)
                here = pl.ds(pl.multiple_of(idx * BLOCK, BLOCK), BLOCK)
                jobs.append((pq_ref[here, :], idx, n))
                dsts.append(rows)
            outs = _band_blocks(jobs, pk_ref, pv_ref, has_prev, has_prev, None, True)
            for rows, (o, lse) in zip(dsts, outs):
                for t in range(halves):
                    og_ref[g, t, rows, :] = o[:, t * LANES:(t + 1) * LANES]
                    lg_ref[g, t, rows, :] = lse[:, t * LANES:(t + 1) * LANES]
            return carry

        lax.fori_loop(0, nblk // STEP_UNROLL, step, 0)

    chunk = 2 * BLOCK

    def combine(i, carry):
        rows = pl.ds(pl.multiple_of(i * chunk, chunk), chunk)
        for t in range(halves):
            ls = [lg_ref[g, t, rows, :] for g in range(len(C_PAIRS))]
            m = jnp.maximum(jnp.maximum(ls[0], ls[1]), ls[2])
            es = [jnp.exp(l - m) for l in ls]
            num = (es[0] * og_ref[0, t, rows, :] + es[1] * og_ref[1, t, rows, :]
                   + es[2] * og_ref[2, t, rows, :])
            oc_ref[0, rows, t * LANES:(t + 1) * LANES] = (
                num / (es[0] + es[1] + es[2])).astype(oc_ref.dtype)
        return carry

    lax.fori_loop(0, s // chunk, combine, 0)


def _attn_c(z):
    bsz, s, _ = z.shape
    nblk = s // BLOCK
    width = C_HEADS * HEAD_DIM
    halves = MXU_W // LANES

    def in_spec(col):
        return pl.BlockSpec((1, s, width), lambda bi: (bi, 0, col * IN_TN // width))

    return pl.pallas_call(
        _attn_c_kernel,
        out_shape=jax.ShapeDtypeStruct((bsz, s, C_OUT_W), BF16),
        grid=(bsz,),
        in_specs=[in_spec(COL_CQ), in_spec(COL_CK), in_spec(COL_CV)],
        out_specs=pl.BlockSpec((1, s, C_OUT_W), lambda bi: (bi, 0, 0)),
        scratch_shapes=[
            pltpu.VMEM((halves, s, LANES), F32),
            pltpu.VMEM((s, MXU_W), BF16),
            pltpu.VMEM((s + BLOCK, MXU_W), BF16),
            pltpu.VMEM((s + BLOCK, MXU_W), BF16),
            pltpu.VMEM((len(C_PAIRS), halves, s, LANES), F32),
            pltpu.VMEM((len(C_PAIRS), halves, s, LANES), F32),
        ],
        compiler_params=_cparams(("parallel",)),
        name="attn_c",
    )(z, z, z)


def _sgu_kernel(zu_ref, zv_ref, lng_ref, lnb_ref, ws_ref, bs_ref, o_ref):
    tm = zu_ref.shape[1]
    u = jax.nn.gelu(zu_ref[0].astype(F32))
    v = jax.nn.gelu(zv_ref[0].astype(F32))
    mu = jnp.mean(v, -1, keepdims=True)
    var = jnp.mean(jnp.square(v - mu), -1, keepdims=True)
    vn = ((v - mu) * lax.rsqrt(var + EPS) * lng_ref[...] + lnb_ref[...]).astype(BF16)
    causal = (lax.broadcasted_iota(jnp.int32, (B_CHUNK, 1), 0)
              >= lax.broadcasted_iota(jnp.int32, (1, B_CHUNK), 1))
    first = lax.broadcasted_iota(jnp.int32, (1, LANES), 1) < B_CH
    zero = jnp.zeros((), BF16)
    for p in range(B_WIDTH // LANES):
        w0 = jnp.where(causal, ws_ref[2 * p], zero)
        w1 = jnp.where(causal, ws_ref[2 * p + 1], zero)
        cols = slice(p * LANES, (p + 1) * LANES)
        for c in range(tm // B_CHUNK):
            rows = slice(c * B_CHUNK, (c + 1) * B_CHUNK)
            vv = vn[rows, cols]
            sv = jnp.where(first,
                           jnp.dot(w0, vv, preferred_element_type=F32),
                           jnp.dot(w1, vv, preferred_element_type=F32)) + bs_ref[:, cols]
            o_ref[0, rows, cols] = (u[rows, cols] * sv).astype(o_ref.dtype)


def _sgu(z, ln_g, ln_b, w_s, b_s):
    bsz, s, _ = z.shape
    tm = 512
    bias = jnp.repeat(b_s.T, B_CH, axis=1)
    col_u = COL_BU * IN_TN // B_WIDTH
    col_v = COL_BV * IN_TN // B_WIDTH
    return pl.pallas_call(
        _sgu_kernel,
        out_shape=jax.ShapeDtypeStruct((bsz, s, B_WIDTH), BF16),
        grid=(bsz, s // tm),
        in_specs=[
            pl.BlockSpec((1, tm, B_WIDTH), lambda bi, i: (bi, i, col_u)),
            pl.BlockSpec((1, tm, B_WIDTH), lambda bi, i: (bi, i, col_v)),
            pl.BlockSpec((1, B_WIDTH), lambda bi, i: (0, 0)),
            pl.BlockSpec((1, B_WIDTH), lambda bi, i: (0, 0)),
            pl.BlockSpec((B_GROUPS, B_CHUNK, B_CHUNK), lambda bi, i: (0, 0, 0)),
            pl.BlockSpec((B_CHUNK, B_WIDTH), lambda bi, i: (0, 0)),
        ],
        out_specs=pl.BlockSpec((1, tm, B_WIDTH), lambda bi, i: (bi, i, 0)),
        compiler_params=_cparams(("parallel", "parallel")),
        name="sgu",
    )(z, z, ln_g.reshape(1, B_WIDTH), ln_b.reshape(1, B_WIDTH), w_s.astype(BF16), bias)


def _route(logits):
    lane = lax.broadcasted_iota(jnp.int32, (1, ROUTE_W), 1)
    lane_f = lane.astype(F32)
    is_g = lane < N_EXPERT_GROUPS
    lg = jnp.where(is_g, logits, NEG_INF)
    mg = jnp.max(lg, -1, keepdims=True)
    pg_top = 1.0 / jnp.sum(jnp.where(is_g, jnp.exp(lg - mg), 0.0), -1, keepdims=True)
    g_idx = jnp.min(jnp.where(lg == mg, lane_f, float(ROUTE_W)), -1, keepdims=True)
    e_group = ((lane - ROUTE_E0) // EXPERTS_PER_GROUP).astype(F32)
    in_grp = (lane >= ROUTE_E0) & (lane < ROUTE_E0 + N_EXPERTS) & (e_group == g_idx)
    le = jnp.where(in_grp, logits, NEG_INF)
    m1 = jnp.max(le, -1, keepdims=True)
    i1 = jnp.min(jnp.where(le == m1, lane_f, float(ROUTE_W)), -1, keepdims=True)
    le2 = jnp.where(lane_f == i1, NEG_INF, le)
    m2 = jnp.max(le2, -1, keepdims=True)
    i2 = jnp.min(jnp.where(le2 == m2, lane_f, float(ROUTE_W)), -1, keepdims=True)
    t = jnp.exp(m2 - m1)
    w1 = pg_top / (1.0 + t)
    w2 = w1 * t
    picks = (i1 - ROUTE_E0, i2 - ROUTE_E0, w1, w2)
    out = jnp.zeros(logits.shape, F32)
    for k, val in enumerate(picks):
        out = jnp.where(lane == k, val, out)
    return out


def _merge_kernel(x_ref, oa_ref, ob_ref, oc_ref,
                  g0_ref, g1_ref, g2_ref, g3_ref, wa_ref, wb_ref, wc_ref, wo_ref, n2_ref, wr_ref,
                  xo_ref, h2_ref, route_ref, counts_ref, pos_ref, xn0_ref, xn1_ref, cnt_ref):
    i = pl.program_id(0)
    n_tiles = pl.num_programs(0) - 1

    def residual_into(xn_ref):
        zg = jnp.concatenate([g0_ref[...], g1_ref[...], g2_ref[...], g3_ref[...]],
                             axis=1).astype(F32)
        gates = jax.nn.sigmoid(zg)
        pa = jnp.dot(oa_ref[...], wa_ref[...], preferred_element_type=F32)
        pb = jnp.dot(ob_ref[...], wb_ref[...], preferred_element_type=F32)
        pc = jnp.dot(oc_ref[...], wc_ref[...], preferred_element_type=F32)
        merged = (gates[:, :D_MODEL] * pa + gates[:, D_MODEL:2 * D_MODEL] * pb
                  + gates[:, 2 * D_MODEL:] * pc)
        xn_ref[...] = x_ref[...] + jnp.dot(merged.astype(BF16), wo_ref[...],
                                           preferred_element_type=F32)

    def tail_from(xn_ref):
        xn = xn_ref[...]
        xo_ref[...] = xn
        ms = jnp.mean(xn * xn, -1, keepdims=True)
        h2 = xn * lax.rsqrt(ms + EPS) * n2_ref[...]
        h2_ref[...] = _pack_bf16_pairs(h2)
        route = _route(jnp.dot(h2.astype(BF16), wr_ref[...], preferred_element_type=F32))
        tm = route.shape[0]
        lane = lax.broadcasted_iota(jnp.int32, (1, ROUTE_W), 1)
        lane_f = lane.astype(F32)
        row = lax.broadcasted_iota(jnp.int32, (tm, 1), 0)
        hit = [lane_f == route[:, k:k + 1] for k in range(TOP_K)]
        onehot = jnp.where(hit[0], 1.0, 0.0) + jnp.where(hit[1], 1.0, 0.0)
        scan = onehot
        shift = 1
        while shift < tm:
            scan = scan + jnp.where(row >= shift, pltpu.roll(scan, shift, 0), 0.0)
            shift *= 2
        before = scan - onehot + cnt_ref[0:1, :]
        capacity = pos_ref.shape[1] * LANES
        diag = (row & (LANES - 1)) == lane
        first_row = pl.multiple_of((i - 1) * (tm // LANES), tm // LANES)
        for k in range(TOP_K):
            rank = jnp.sum(jnp.where(hit[k], before, 0.0), -1, keepdims=True)
            slot = jnp.where(diag, route[:, k:k + 1] * capacity + rank, 0.0)
            dense = jnp.concatenate(
                [jnp.sum(slot[r * LANES:(r + 1) * LANES], axis=0, keepdims=True)
                 for r in range(tm // LANES)], axis=0)
            pos_ref[k, pl.ds(first_row, tm // LANES), :] = dense.astype(jnp.int32)
        route_ref[...] = route
        cnt = cnt_ref[0:1, :] + jnp.sum(onehot, axis=0, keepdims=True)
        cnt_ref[...] = jnp.broadcast_to(cnt, cnt_ref.shape)
        counts_ref[...] = jnp.broadcast_to(cnt, counts_ref.shape)

    bufs = (xn0_ref, xn1_ref)
    for parity in range(2):
        @pl.when((i > 0) & (i < n_tiles) & (i % 2 == parity))
        def _(parity=parity):
            tail_from(bufs[1 - parity])
            residual_into(bufs[parity])

        @pl.when((i == n_tiles) & (i % 2 == parity))
        def _(parity=parity):
            tail_from(bufs[1 - parity])

    @pl.when(i == 0)
    def _():
        cnt_ref[...] = jnp.zeros_like(cnt_ref)
        residual_into(bufs[0])


def _merge(x2, oa, ob, oc, z2, wa, wb, wc, wo, n2g, w_rg, w_re):
    n, d = x2.shape
    tm = 512
    wr = jnp.concatenate(
        [w_rg, w_re, jnp.zeros((d, ROUTE_W - N_EXPERT_GROUPS - N_EXPERTS), w_rg.dtype)], axis=1)
    gate_w = GATE_W // 4
    gate_col0 = COL_G * IN_TN // gate_w

    n_tiles = n // tm
    last = n_tiles - 1

    def rows(width):
        return pl.BlockSpec((tm, width), lambda i: (jnp.minimum(i, last), 0))

    def lagged_rows(width):
        return pl.BlockSpec((tm, width), lambda i: (jnp.maximum(i - 1, 0), 0))

    def full(shape):
        return pl.BlockSpec(shape, lambda i: (0, 0))

    gate_specs = [pl.BlockSpec((tm, gate_w), lambda i, k=k: (jnp.minimum(i, last), gate_col0 + k))
                  for k in range(4)]
    return pl.pallas_call(
        _merge_kernel,
        out_shape=[jax.ShapeDtypeStruct((n, d), F32),
                   jax.ShapeDtypeStruct((n, d // 2), jnp.int32),
                   jax.ShapeDtypeStruct((n, ROUTE_W), F32),
                   jax.ShapeDtypeStruct((SUBLANES, ROUTE_W), F32),
                   jax.ShapeDtypeStruct((TOP_K, n // LANES, LANES), jnp.int32)],
        grid=(n_tiles + 1,),
        in_specs=[rows(d), rows(A_Q_W), rows(B_WIDTH), rows(C_OUT_W),
                  *gate_specs,
                  full((A_Q_W, d)), full((B_WIDTH, d)), full((C_OUT_W, d)), full((d, d)),
                  full((1, d)), full((d, ROUTE_W))],
        out_specs=[lagged_rows(d), lagged_rows(d // 2), lagged_rows(ROUTE_W),
                   full((SUBLANES, ROUTE_W)),
                   pl.BlockSpec((TOP_K, n // LANES, LANES), lambda i: (0, 0, 0))],
        scratch_shapes=[pltpu.VMEM((tm, d), F32), pltpu.VMEM((tm, d), F32),
                        pltpu.VMEM((SUBLANES, ROUTE_W), F32)],
        compiler_params=_cparams(("arbitrary",)),
        name="merge",
    )(x2, oa, ob, oc, z2, z2, z2, z2,
      wa.astype(BF16), wb.astype(BF16), wc.astype(BF16), wo.astype(BF16),
      n2g.reshape(1, d), wr.astype(BF16))


MOE_TM = 1024
SC_CORES = 2
SC_SUBCORES = 16
SC_WORKERS = SC_CORES * SC_SUBCORES
SC_CHUNK = 128
COMBINE_PARTS = 2


def _pack_bf16_pairs(x):
    w = x.shape[1] // 2
    lo = pltpu.bitcast(x[:, :w].astype(BF16).astype(F32), jnp.int32)
    hi = pltpu.bitcast(x[:, w:].astype(BF16).astype(F32), jnp.int32)
    return (hi & jnp.int32(-65536)) | lax.shift_right_logical(lo, jnp.int32(16))


def _unpack_bf16_pairs(p):
    lo = pltpu.bitcast(lax.shift_left(p, jnp.int32(16)), F32)
    hi = pltpu.bitcast(p & jnp.int32(-65536), F32)
    return jnp.concatenate([lo, hi], axis=1)


def _sc_gather_rows(table, idx):
    rows, width = idx.shape[0], table.shape[1]
    per_worker = rows // SC_WORKERS
    n_chunks = per_worker // SC_CHUNK
    assert per_worker * SC_WORKERS == rows and n_chunks * SC_CHUNK == per_worker
    mesh = plsc.VectorSubcoreMesh(core_axis_name="c", subcore_axis_name="s",
                                  num_cores=SC_CORES, num_subcores=SC_SUBCORES)

    @functools.partial(
        pl.kernel, mesh=mesh,
        out_type=jax.ShapeDtypeStruct((rows, width), table.dtype),
        scratch_types=[pltpu.VMEM((n_chunks, SC_CHUNK), jnp.int32),
                       pltpu.VMEM((SC_CHUNK, width), table.dtype),
                       pltpu.SemaphoreType.DMA],
        name="sc_gather_rows",
    )
    def gather(table_hbm, idx_hbm, out_hbm, idx_v, rows_v, sem):
        wid = lax.axis_index("s") * SC_CORES + lax.axis_index("c")
        pltpu.sync_copy(idx_hbm.at[wid], idx_v)
        base = wid * per_worker

        @pl.loop(0, n_chunks)
        def _(c):
            pltpu.async_copy(table_hbm.at[idx_v.at[c]], rows_v, sem).wait()
            pltpu.sync_copy(rows_v, out_hbm.at[pl.ds(base + c * SC_CHUNK, SC_CHUNK)])

    return gather(table, idx.reshape(SC_WORKERS, n_chunks, SC_CHUNK))


def _sc_scatter_rows(rows, pos, n_slots):
    n, width = rows.shape
    per_worker = n // SC_WORKERS
    n_chunks = per_worker // SC_CHUNK
    assert per_worker * SC_WORKERS == n and n_chunks * SC_CHUNK == per_worker
    mesh = plsc.VectorSubcoreMesh(core_axis_name="c", subcore_axis_name="s",
                                  num_cores=SC_CORES, num_subcores=SC_SUBCORES)

    @functools.partial(
        pl.kernel, mesh=mesh,
        out_type=jax.ShapeDtypeStruct((n_slots, width), rows.dtype),
        scratch_types=[pltpu.VMEM((TOP_K, n_chunks, SC_CHUNK), jnp.int32),
                       pltpu.VMEM((SC_CHUNK, width), rows.dtype),
                       pltpu.SemaphoreType.DMA],
        name="sc_scatter_rows",
    )
    def scatter(rows_hbm, pos_hbm, out_hbm, pos_v, rows_v, sem):
        wid = lax.axis_index("s") * SC_CORES + lax.axis_index("c")
        for k in range(TOP_K):
            pltpu.sync_copy(pos_hbm.at[k, wid], pos_v.at[k])
        base = wid * per_worker

        @pl.loop(0, n_chunks)
        def _(c):
            pltpu.sync_copy(rows_hbm.at[pl.ds(base + c * SC_CHUNK, SC_CHUNK)], rows_v)
            for k in range(TOP_K):
                pltpu.async_copy(rows_v, out_hbm.at[pos_v.at[k, c]], sem).wait()

    return scatter(rows, pos.reshape(TOP_K, SC_WORKERS, n_chunks, SC_CHUNK))


def _dispatch_plan(counts, capacity, n_steps, expert_base):
    counts = counts[0, :N_EXPERTS].astype(jnp.int32)
    tiles = (counts + MOE_TM - 1) // MOE_TM
    tile_ends = jnp.cumsum(tiles)
    step = jnp.arange(n_steps, dtype=jnp.int32)
    used = step < tile_ends[-1]
    expert = jnp.minimum(jnp.sum((step[:, None] >= tile_ends[None, :]).astype(jnp.int32), axis=1),
                         N_EXPERTS - 1)
    chunk = step - (tile_ends - tiles)[expert]
    rows = jnp.where(used, jnp.clip(counts[expert] - chunk * MOE_TM, 0, MOE_TM), 0)
    first = jnp.concatenate([jnp.ones((1,), jnp.int32),
                             (expert[1:] != expert[:-1]).astype(jnp.int32)])
    block = expert * (capacity // MOE_TM) + chunk
    last_used = jnp.sum(jnp.where(step == tile_ends[-1] - 1, block, 0))
    spare = N_EXPERTS * (capacity // MOE_TM)
    blocks = jnp.stack([jnp.where(used, block, last_used), jnp.where(used, block, spare)])
    return expert + expert_base, rows.astype(jnp.int32), first, blocks.astype(jnp.int32)


def _moe_tile_kernel(te_ref, tr_ref, tf_ref, tb_ref, xs_ref, wg_ref, wu_ref, wd_ref, ys_ref,
                     wg_s, wu_s, wd_s):
    t = pl.program_id(0)

    @pl.when(tf_ref[t] != 0)
    def _():
        wg_s[...] = wg_ref[0].astype(BF16)
        wu_s[...] = wu_ref[0].astype(BF16)
        wd_s[...] = wd_ref[0].astype(BF16)

    @pl.when(tr_ref[t] != 0)
    def _():
        occupied = lax.broadcasted_iota(jnp.int32, (MOE_TM, 1), 0) < tr_ref[t]
        x = jnp.where(occupied, _unpack_bf16_pairs(xs_ref[...]), 0.0).astype(BF16)
        hg = jnp.dot(x, wg_s[...], preferred_element_type=F32)
        hu = jnp.dot(x, wu_s[...], preferred_element_type=F32)
        a = (jax.nn.silu(hg) * hu).astype(BF16)
        ys_ref[...] = _pack_bf16_pairs(jnp.dot(a, wd_s[...], preferred_element_type=F32))

    @pl.when(tr_ref[t] == 0)
    def _():
        ys_ref[...] = jnp.zeros_like(ys_ref)


def _moe_tiles(xs, tile_expert, tile_rows, tile_first, tile_blocks, w_gate, w_up, w_down):
    n_slots, half = xs.shape
    d = 2 * half
    grid_spec = pltpu.PrefetchScalarGridSpec(
        num_scalar_prefetch=4,
        grid=(tile_expert.shape[0],),
        in_specs=[
            pl.BlockSpec((MOE_TM, half), lambda t, te, tr, tf, tb: (tb[0, t], 0)),
            pl.BlockSpec((1, d, D_EXPERT), lambda t, te, tr, tf, tb: (te[t], 0, 0)),
            pl.BlockSpec((1, d, D_EXPERT), lambda t, te, tr, tf, tb: (te[t], 0, 0)),
            pl.BlockSpec((1, D_EXPERT, d), lambda t, te, tr, tf, tb: (te[t], 0, 0)),
        ],
        out_specs=pl.BlockSpec((MOE_TM, half), lambda t, te, tr, tf, tb: (tb[1, t], 0)),
        scratch_shapes=[pltpu.VMEM((d, D_EXPERT), BF16), pltpu.VMEM((d, D_EXPERT), BF16),
                        pltpu.VMEM((D_EXPERT, d), BF16)],
    )
    return pl.pallas_call(
        _moe_tile_kernel,
        out_shape=jax.ShapeDtypeStruct((n_slots, half), jnp.int32),
        grid_spec=grid_spec,
        compiler_params=_cparams(("arbitrary",)),
        name="moe_tiles",
    )(tile_expert, tile_rows, tile_first, tile_blocks, xs, w_gate, w_up, w_down)


def _moe_combine_kernel(x_ref, y0_ref, y1_ref, r_ref, g_ref, *rest, final_norm):
    o_ref = rest[-1]
    w0 = r_ref[:, 2:3]
    w1 = r_ref[:, 3:4]
    x = x_ref[...] + w0 * _unpack_bf16_pairs(y0_ref[0]) + w1 * _unpack_bf16_pairs(y1_ref[0])
    if final_norm:
        ms = jnp.mean(x * x, -1, keepdims=True)
        x = x * lax.rsqrt(ms + EPS) * g_ref[...]
    o_ref[...] = x


def _moe_combine(x2, yg, route, norm_g, final_norm, part, n_parts, earlier):
    n, d = x2.shape
    tm = 1024
    tiles = n // tm // n_parts
    first = part * tiles
    in_specs = [
        pl.BlockSpec((tm, d), lambda i: (first + i, 0)),
        pl.BlockSpec((1, tm, d // 2), lambda i: (0, i, 0)),
        pl.BlockSpec((1, tm, d // 2), lambda i: (1, i, 0)),
        pl.BlockSpec((tm, ROUTE_W), lambda i: (first + i, 0)),
        pl.BlockSpec((1, d), lambda i: (0, 0)),
    ]
    args = [x2, yg, yg, route, norm_g.reshape(1, d)]
    aliases = {}
    if earlier is not None:
        in_specs.append(pl.BlockSpec(memory_space=pl.ANY))
        args.append(earlier)
        aliases = {len(args) - 1: 0}
    return pl.pallas_call(
        functools.partial(_moe_combine_kernel, final_norm=final_norm),
        out_shape=jax.ShapeDtypeStruct((n, d), F32),
        grid=(tiles,),
        in_specs=in_specs,
        out_specs=pl.BlockSpec((tm, d), lambda i: (first + i, 0)),
        input_output_aliases=aliases,
        compiler_params=_cparams(("parallel",)),
        name="moe_combine",
    )(*args)


def _moe(h2p, route, counts, pos, x2, layer, w_gate, w_up, w_down, norm_f_g, final_norm):
    n, d = x2.shape
    n_slots = N_EXPERTS * n + MOE_TM
    n_steps = TOP_K * n // MOE_TM + N_EXPERTS
    tile_expert, tile_rows, tile_first, tile_blocks = _dispatch_plan(
        counts, n, n_steps, layer * N_EXPERTS)
    xs = _sc_scatter_rows(h2p, pos, n_slots)
    ys = _moe_tiles(xs, tile_expert, tile_rows, tile_first, tile_blocks, w_gate, w_up, w_down)
    out = None
    rows = pos.shape[1] // COMBINE_PARTS
    for part in range(COMBINE_PARTS):
        idx = pos[:, part * rows:(part + 1) * rows, :].reshape(-1)
        yg = _sc_gather_rows(ys, idx).reshape(TOP_K, n // COMBINE_PARTS, d // 2)
        out = _moe_combine(x2, yg, route, norm_f_g, final_norm, part, COMBINE_PARTS, out)
    return out


def kernel(x, positions, norm1_g, w_in, b_in, attn_sinks, sgu_ln_g, sgu_ln_b, w_spatial, b_spatial,
           w_proj_a, w_proj_b, w_proj_c, w_out, norm2_g, w_router_group, w_router_expert,
           w_expert_gate, w_expert_up, w_expert_down, norm_f_g):
    bsz, s, d = x.shape
    depth = w_in.shape[0]
    wg_all = w_expert_gate.reshape(depth * N_EXPERTS, d, D_EXPERT)
    wu_all = w_expert_up.reshape(depth * N_EXPERTS, d, D_EXPERT)
    wd_all = w_expert_down.reshape(depth * N_EXPERTS, D_EXPERT, d)
    w_in_b = w_in.astype(BF16)
    assert d == D_MODEL and s % (C_PAIRS[-1][1] * BLOCK) == 0
    tabs = _rope_tables(positions)
    for l in range(depth):
        z = _in_proj(x, norm1_g[l], w_in_b, b_in.reshape(depth, 1, D_IN), l, tabs)
        oa = _attn_a(z, attn_sinks[l])
        ob = _sgu(z, sgu_ln_g[l], sgu_ln_b[l], w_spatial[l], b_spatial[l])
        oc = _attn_c(z)
        x2, h2, route, counts, pos = _merge(
            x.reshape(bsz * s, d), oa.reshape(bsz * s, A_Q_W), ob.reshape(bsz * s, B_WIDTH),
            oc.reshape(bsz * s, C_OUT_W), z.reshape(bsz * s, D_IN), w_proj_a[l], w_proj_b[l], w_proj_c[l], w_out[l],
            norm2_g[l], w_router_group[l], w_router_expert[l])
        x = _moe(h2, route, counts, pos, x2, l, wg_all, wu_all, wd_all, norm_f_g, l == depth - 1).reshape(bsz, s, d)
    return x
```

```python
import functools

import jax
import jax.numpy as jnp
from jax import lax
from jax.experimental import pallas as pl
from jax.experimental.pallas import tpu as pltpu
from jax.experimental.pallas import tpu_sc as plsc

F32 = jnp.float32
BF16 = jnp.bfloat16

D_MODEL = 1024
HEAD_DIM = 64
ROT_DIM = HEAD_DIM // 4
ROPE_THETA = 500000.0
BLOCK = 128
EPS = 1e-5
NEG_INF = -1e30

A_Q_HEADS = 16
A_KV_HEADS = 4
A_REP = A_Q_HEADS // A_KV_HEADS
A_WINDOW = 128
A_Q_W = A_Q_HEADS * HEAD_DIM
A_KV_W = A_KV_HEADS * HEAD_DIM

B_GROUPS = 12
B_CH = 64
B_WIDTH = B_GROUPS * B_CH
B_CHUNK = 128

C_PAIRS = ((128, 1), (512, 4), (2048, 16))
C_HEADS_PER_GROUP = 4
C_HEADS = C_HEADS_PER_GROUP * len(C_PAIRS)
C_OUT_W = C_HEADS_PER_GROUP * HEAD_DIM

N_BRANCH = 3
GATE_W = N_BRANCH * D_MODEL
D_IN = A_Q_W + 2 * A_KV_W + 2 * B_WIDTH + 3 * C_HEADS * HEAD_DIM + GATE_W

N_EXPERT_GROUPS = 4
EXPERTS_PER_GROUP = 8
N_EXPERTS = N_EXPERT_GROUPS * EXPERTS_PER_GROUP
TOP_K = 2
D_EXPERT = 256

LANES = 128
MXU_W = 256
VMEM_LIMIT = 56 * 1024 * 1024

IN_TN = 256
COL_AQ = 0
COL_AK = A_Q_W // IN_TN
COL_AV = COL_AK + A_KV_W // IN_TN
COL_BU = COL_AV + A_KV_W // IN_TN
COL_BV = COL_BU + B_WIDTH // IN_TN
COL_CQ = COL_BV + B_WIDTH // IN_TN
COL_CK = COL_CQ + C_HEADS * HEAD_DIM // IN_TN
COL_CV = COL_CK + C_HEADS * HEAD_DIM // IN_TN
COL_G = COL_CV + C_HEADS * HEAD_DIM // IN_TN
N_COL_TILES = D_IN // IN_TN

ROUTE_W = LANES
ROUTE_E0 = N_EXPERT_GROUPS
ROUTE_RANK0 = 4
SUBLANES = 8


def _cparams(sem):
    return pltpu.CompilerParams(dimension_semantics=sem, vmem_limit_bytes=VMEM_LIMIT)


def _rope_table_kernel(pos_ref, inv_ref, c_ref, s1_ref, s2_ref):
    lane = lax.broadcasted_iota(jnp.int32, (1, LANES), 1)
    d = lane & (HEAD_DIM - 1)
    ang = pos_ref[...].astype(F32) * inv_ref[...]
    c = jnp.cos(ang)
    s = jnp.sin(ang)
    half = ROT_DIM // 2
    c_ref[...] = jnp.where(d < ROT_DIM, c, 1.0)
    s1_ref[...] = jnp.where(d < half, -s, 0.0)
    s2_ref[...] = jnp.where((d >= half) & (d < ROT_DIM), s, 0.0)


def _rope_tables(positions):
    n = positions.size
    inv = ROPE_THETA ** (-jnp.arange(0, ROT_DIM, 2, dtype=F32) / ROT_DIM)
    inv_lane = jnp.tile(inv, LANES // inv.shape[0]).reshape(1, LANES)
    pos_b = jnp.broadcast_to(positions.reshape(n, 1), (n, LANES))
    tm = 1024
    spec = pl.BlockSpec((tm, LANES), lambda i: (i, 0))
    return pl.pallas_call(
        _rope_table_kernel,
        out_shape=[jax.ShapeDtypeStruct((n, LANES), F32)] * 3,
        grid=(n // tm,),
        in_specs=[spec, pl.BlockSpec((1, LANES), lambda i: (0, 0))],
        out_specs=[spec, spec, spec],
        compiler_params=_cparams(("parallel",)),
        name="rope_tables",
    )(pos_b, inv_lane)


IN_STEP_TILES = 3
IN_STEP_W = IN_STEP_TILES * IN_TN
IN_STEPS = N_COL_TILES // IN_STEP_TILES


def _is_rope_tile(tile):
    return tile < COL_AV or COL_CQ <= tile < COL_CV


_STEP_ROPES = tuple(tuple(_is_rope_tile(step * IN_STEP_TILES + u) for u in range(IN_STEP_TILES))
                    for step in range(IN_STEPS))


def _in_proj_kernel(x_ref, g_ref, w_ref, b_ref, c_ref, s1_ref, s2_ref, z_ref, h_ref,
                    acc0_ref, acc1_ref):
    j = pl.program_id(1)

    @pl.when(j == 0)
    def _():
        x = x_ref[0]
        ms = jnp.mean(x * x, -1, keepdims=True)
        h_ref[...] = (x * lax.rsqrt(ms + EPS) * g_ref[...]).astype(BF16)

    prev = j - 1

    def matmul_into(acc_ref):
        acc_ref[...] = jnp.dot(h_ref[...], w_ref[0], preferred_element_type=F32) + b_ref[0]

    def finish(acc_ref, ropes):
        for u, rope in enumerate(ropes):
            cols = slice(u * IN_TN, (u + 1) * IN_TN)
            if not rope:
                z_ref[0, :, cols] = acc_ref[:, cols].astype(BF16)
                continue
            tile = prev * IN_STEP_TILES + u
            is_q = (tile < COL_AK) | ((tile >= COL_CQ) & (tile < COL_CK))
            scale = jnp.where(is_q, HEAD_DIM ** -0.5, 1.0).astype(F32)
            for t in range(u * IN_TN // LANES, (u + 1) * IN_TN // LANES):
                lanes = slice(t * LANES, (t + 1) * LANES)
                a = acc_ref[:, lanes]
                r = (a * c_ref[...] + pltpu.roll(a, LANES - ROT_DIM // 2, 1) * s1_ref[...]
                     + pltpu.roll(a, ROT_DIM // 2, 1) * s2_ref[...])
                z_ref[0, :, lanes] = (r * scale).astype(BF16)

    accs = (acc0_ref, acc1_ref)
    patterns = sorted(set(_STEP_ROPES))
    pattern_id = jnp.int32(0)
    for step, ropes in enumerate(_STEP_ROPES):
        pattern_id = jnp.where(prev == step, patterns.index(ropes), pattern_id)
    inner = (j > 0) & (j < IN_STEPS)
    for parity in range(2):
        for pid, ropes in enumerate(patterns):
            @pl.when(inner & (j % 2 == parity) & (pattern_id == pid))
            def _(parity=parity, ropes=ropes):
                matmul_into(accs[parity])
                finish(accs[1 - parity], ropes)

    @pl.when(j == 0)
    def _():
        matmul_into(accs[0])

    @pl.when(j == IN_STEPS)
    def _():
        finish(accs[(IN_STEPS - 1) % 2], _STEP_ROPES[-1])


def _in_proj(x, g, w_all, b_all, layer, tabs):
    bsz, s, d = x.shape
    c, s1, s2 = tabs
    tab_spec = pl.BlockSpec((s, LANES), lambda bi, j: (bi, 0))
    last = IN_STEPS - 1
    return pl.pallas_call(
        _in_proj_kernel,
        out_shape=jax.ShapeDtypeStruct((bsz, s, D_IN), BF16),
        grid=(bsz, IN_STEPS + 1),
        in_specs=[
            pl.BlockSpec((1, s, d), lambda bi, j: (bi, 0, 0)),
            pl.BlockSpec((1, d), lambda bi, j: (0, 0)),
            pl.BlockSpec((1, d, IN_STEP_W), lambda bi, j: (layer, 0, jnp.minimum(j, last))),
            pl.BlockSpec((1, 1, IN_STEP_W), lambda bi, j: (layer, 0, jnp.minimum(j, last))),
            tab_spec, tab_spec, tab_spec,
        ],
        out_specs=pl.BlockSpec((1, s, IN_STEP_W), lambda bi, j: (bi, 0, jnp.maximum(j - 1, 0))),
        scratch_shapes=[pltpu.VMEM((s, d), BF16), pltpu.VMEM((s, IN_STEP_W), F32),
                        pltpu.VMEM((s, IN_STEP_W), F32)],
        compiler_params=_cparams(("parallel", "arbitrary")),
        name="in_proj",
    )(x, g.reshape(1, d), w_all, b_all, c, s1, s2)


STACK = C_HEADS_PER_GROUP * BLOCK
STEP_UNROLL = 4
A_STEP_UNROLL = 8
FAST_STRIDE = 4


def _head_stack(blk):
    seg = lax.broadcasted_iota(jnp.int32, (1, MXU_W), 1) // HEAD_DIM
    rowseg = lax.broadcasted_iota(jnp.int32, (STACK, 1), 0) // BLOCK
    return jnp.where(seg == rowseg, jnp.concatenate([blk] * C_HEADS_PER_GROUP, axis=0),
                     jnp.zeros((), blk.dtype))


def _band_step(q, kp_ref, vp_ref, blk, n, has_prev, diag_key, sink_col, want_lse):
    return _band_blocks([(q, blk, n)], kp_ref, vp_ref, has_prev, diag_key, sink_col, want_lse)[0]


def _band_blocks(jobs, kp_ref, vp_ref, has_prev, diag_key, sink_col, want_lse):
    contract = (((1,), (1,)), ((), ()))
    i_loc = lax.broadcasted_iota(jnp.int32, (STACK, 1), 0) & (BLOCK - 1)
    jj = lax.broadcasted_iota(jnp.int32, (1, BLOCK), 1)
    upper = jj > i_loc
    seg = lax.broadcasted_iota(jnp.int32, (1, MXU_W), 1) // HEAD_DIM
    zero_b = jnp.zeros((), BF16)

    def key_rows(blk):
        lo = pl.multiple_of(blk * BLOCK, BLOCK)
        if has_prev:
            return pl.ds(lo, 2 * BLOCK)
        return pl.ds(pl.multiple_of(lo + BLOCK, BLOCK), BLOCK)

    scores = [lax.dot_general(_head_stack(q), kp_ref[key_rows(blk), :], contract,
                              preferred_element_type=F32) for q, blk, _ in jobs]

    def own_lanes(x):
        out = x[0:BLOCK]
        for h in range(1, C_HEADS_PER_GROUP):
            out = jnp.where(seg == h, x[h * BLOCK:(h + 1) * BLOCK], out)
        return out

    probs, stats = [], []
    for s, (_, blk, n) in zip(scores, jobs):
        if has_prev:
            sp = s[:, :BLOCK] + jnp.where(n == 0, NEG_INF, 0.0).astype(F32)
            f = jnp.where(upper, sp, s[:, BLOCK:])
        else:
            f = jnp.where(upper, NEG_INF, s)
        m = jnp.max(f, -1, keepdims=True)
        if diag_key:
            sd = jnp.sum(jnp.where(jj == i_loc, sp, 0.0), -1, keepdims=True)
            m = jnp.maximum(m, sd)
        if sink_col is not None:
            m = jnp.maximum(m, sink_col)
        p = jnp.exp(f - m)
        den = jnp.sum(p, -1, keepdims=True)
        pd = None
        if diag_key:
            pd = jnp.exp(sd - m)
            den = den + pd
        if sink_col is not None:
            den = den + jnp.exp(sink_col - m)
        inv = 1.0 / den
        pn = (p * inv).astype(BF16)
        if has_prev:
            below = zero_b
            if diag_key:
                below = jnp.where(jj == i_loc, pd * inv, 0.0).astype(BF16)
            pn = jnp.concatenate([jnp.where(upper, pn, below), jnp.where(upper, zero_b, pn)],
                                 axis=1)
        probs.append(pn)
        stats.append((m, den))

    outs = []
    for pn, (m, den), (_, blk, _) in zip(probs, stats, jobs):
        o = own_lanes(jnp.dot(pn, vp_ref[key_rows(blk), :], preferred_element_type=F32))
        lse = own_lanes(m + jnp.log(den)) + jnp.zeros((BLOCK, MXU_W), F32) if want_lse else None
        outs.append((o, lse))
    return outs


def _replicate_head(blk, g):
    words = pltpu.bitcast(blk, jnp.uint32)
    tile = words[:, (g // 2) * LANES:(g // 2 + 1) * LANES]
    swapped = pltpu.roll(tile, HEAD_DIM, 1)
    low = lax.broadcasted_iota(jnp.int32, (1, LANES), 1) < HEAD_DIM
    both = jnp.where(low, tile, swapped) if g % 2 == 0 else jnp.where(low, swapped, tile)
    return pltpu.bitcast(jnp.concatenate([both, both], axis=1), BF16)


def _attn_a_kernel(sink_ref, q_ref, k_ref, v_ref, o_ref, kp_ref, vp_ref):
    nb = q_ref.shape[1] // BLOCK
    rowseg = lax.broadcasted_iota(jnp.int32, (STACK, 1), 0) // BLOCK
    kp_ref[pl.ds(0, BLOCK), :] = jnp.zeros((BLOCK, MXU_W), BF16)
    vp_ref[pl.ds(0, BLOCK), :] = jnp.zeros((BLOCK, MXU_W), BF16)
    for g in range(A_KV_HEADS):
        q_cols = slice(g * MXU_W, (g + 1) * MXU_W)

        def build(n, carry):
            rows = pl.ds(pl.multiple_of(n * BLOCK, BLOCK), BLOCK)
            dst = pl.ds(pl.multiple_of((n + 1) * BLOCK, BLOCK), BLOCK)
            kp_ref[dst, :] = _replicate_head(k_ref[0, rows, :], g)
            vp_ref[dst, :] = _replicate_head(v_ref[0, rows, :], g)
            return carry

        lax.fori_loop(0, nb, build, 0, unroll=8)
        sink_col = jnp.zeros((STACK, 1), F32)
        for r in range(A_REP):
            sink_col = jnp.where(rowseg == r, sink_ref[g * A_REP + r], sink_col)

        def step(it, carry):
            blocks = [it * A_STEP_UNROLL + u for u in range(A_STEP_UNROLL)]
            rows = [pl.ds(pl.multiple_of(n * BLOCK, BLOCK), BLOCK) for n in blocks]
            jobs = [(q_ref[0, r, q_cols], n, n) for r, n in zip(rows, blocks)]
            outs = _band_blocks(jobs, kp_ref, vp_ref, True, False, sink_col, False)
            for r, (o, _) in zip(rows, outs):
                o_ref[0, r, q_cols] = o.astype(o_ref.dtype)
            return carry

        lax.fori_loop(0, nb // A_STEP_UNROLL, step, 0)


def _attn_a(z, sinks):
    bsz, s, _ = z.shape
    nb = s // BLOCK
    grid_spec = pltpu.PrefetchScalarGridSpec(
        num_scalar_prefetch=1,
        grid=(bsz,),
        in_specs=[
            pl.BlockSpec((1, s, A_Q_W), lambda bi, sk: (bi, 0, 0)),
            pl.BlockSpec((1, s, A_KV_W), lambda bi, sk: (bi, 0, COL_AK)),
            pl.BlockSpec((1, s, A_KV_W), lambda bi, sk: (bi, 0, COL_AV)),
        ],
        out_specs=pl.BlockSpec((1, s, A_Q_W), lambda bi, sk: (bi, 0, 0)),
        scratch_shapes=[pltpu.VMEM(((nb + 1) * BLOCK, MXU_W), BF16),
                        pltpu.VMEM(((nb + 1) * BLOCK, MXU_W), BF16)],
    )
    return pl.pallas_call(
        _attn_a_kernel,
        out_shape=jax.ShapeDtypeStruct((bsz, s, A_Q_W), BF16),
        grid_spec=grid_spec,
        compiler_params=_cparams(("parallel",)),
        name="attn_a",
    )(sinks, z, z, z)


def _attn_c_kernel(q_ref, k_ref, v_ref, oc_ref, stage_ref, stage2_ref, pq_ref, pk_ref, pv_ref,
                   og_ref, lg_ref):
    s = q_ref.shape[1]
    nblk = s // BLOCK
    halves = MXU_W // LANES
    pk_ref[pl.ds(0, BLOCK), :] = jnp.zeros((BLOCK, MXU_W), BF16)
    pv_ref[pl.ds(0, BLOCK), :] = jnp.zeros((BLOCK, MXU_W), BF16)

    def class_rows(idx, nb, d):
        r = idx // nb
        n = idx - r * nb
        start = r + n * (BLOCK * d)
        rows = pl.ds(start, BLOCK, stride=d) if d > 1 else pl.ds(pl.multiple_of(start, BLOCK), BLOCK)
        return n, rows

    quarter = s // FAST_STRIDE

    def stage(src_ref, cols, d):
        x = src_ref[0, :, cols].astype(F32)
        for t in range(halves):
            stage_ref[t] = x[:, t * LANES:(t + 1) * LANES]
        if d > FAST_STRIDE:
            for t in range(halves):
                for r1 in range(FAST_STRIDE):
                    stage2_ref[t, pl.ds(r1 * quarter, quarter), :] = (
                        stage_ref[t, pl.ds(r1, quarter, stride=FAST_STRIDE), :])

    def staged_block(idx, nb, d):
        if d > FAST_STRIDE:
            assert nb == 1 and d == FAST_STRIDE * FAST_STRIDE
            start = (idx % FAST_STRIDE) * quarter + idx // FAST_STRIDE
            rows = pl.ds(start, BLOCK, stride=FAST_STRIDE)
            parts = [stage2_ref[t, rows, :] for t in range(halves)]
        else:
            _, rows = class_rows(idx, nb, d)
            parts = [stage_ref[t, rows, :] for t in range(halves)]
        return jnp.concatenate(parts, axis=1).astype(BF16)

    for g, (win, d) in enumerate(C_PAIRS):
        cols = slice(g * MXU_W, (g + 1) * MXU_W)
        nb = nblk // d
        has_prev = nb > 1
        assert win // d == BLOCK

        def class_major(dst_ref, offset):
            def build(idx, carry):
                dst = pl.ds(pl.multiple_of((idx + offset) * BLOCK, BLOCK), BLOCK)
                dst_ref[dst, :] = staged_block(idx, nb, d)
                return carry
            return build

        for src_ref, dst_ref, offset in ((k_ref, pk_ref, 1), (v_ref, pv_ref, 1), (q_ref, pq_ref, 0)):
            stage(src_ref, cols, d)
            lax.fori_loop(0, nblk, class_major(dst_ref, offset), 0, unroll=4)

        def step(it, carry):
            jobs, dsts = [], []
            for u in range(STEP_UNROLL):
                idx = it * STEP_UNROLL + u
                n, rows = class_rows(idx, nb, d)
                here = pl.ds(pl.multiple_of(idx * BLOCK, BLOCK), BLOCK)
                jobs.append((pq_ref[here, :], idx, n))
                dsts.append(rows)
            outs = _band_blocks(jobs, pk_ref, pv_ref, has_prev, has_prev, None, True)
            for rows, (o, lse) in zip(dsts, outs):
                for t in range(halves):
                    og_ref[g, t, rows, :] = o[:, t * LANES:(t + 1) * LANES]
                    lg_ref[g, t, rows, :] = lse[:, t * LANES:(t + 1) * LANES]
            return carry

        lax.fori_loop(0, nblk // STEP_UNROLL, step, 0)

    chunk = 2 * BLOCK

    def combine(i, carry):
        rows = pl.ds(pl.multiple_of(i * chunk, chunk), chunk)
        for t in range(halves):
            ls = [lg_ref[g, t, rows, :] for g in range(len(C_PAIRS))]
            m = jnp.maximum(jnp.maximum(ls[0], ls[1]), ls[2])
            es = [jnp.exp(l - m) for l in ls]
            num = (es[0] * og_ref[0, t, rows, :] + es[1] * og_ref[1, t, rows, :]
                   + es[2] * og_ref[2, t, rows, :])
            oc_ref[0, rows, t * LANES:(t + 1) * LANES] = (
                num / (es[0] + es[1] + es[2])).astype(oc_ref.dtype)
        return carry

    lax.fori_loop(0, s // chunk, combine, 0)


def _attn_c(z):
    bsz, s, _ = z.shape
    width = C_HEADS * HEAD_DIM
    halves = MXU_W // LANES

    def in_spec(col):
        return pl.BlockSpec((1, s, width), lambda bi: (bi, 0, col * IN_TN // width))

    return pl.pallas_call(
        _attn_c_kernel,
        out_shape=jax.ShapeDtypeStruct((bsz, s, C_OUT_W), BF16),
        grid=(bsz,),
        in_specs=[in_spec(COL_CQ), in_spec(COL_CK), in_spec(COL_CV)],
        out_specs=pl.BlockSpec((1, s, C_OUT_W), lambda bi: (bi, 0, 0)),
        scratch_shapes=[
            pltpu.VMEM((halves, s, LANES), F32),
            pltpu.VMEM((halves, s, LANES), F32),
            pltpu.VMEM((s, MXU_W), BF16),
            pltpu.VMEM((s + BLOCK, MXU_W), BF16),
            pltpu.VMEM((s + BLOCK, MXU_W), BF16),
            pltpu.VMEM((len(C_PAIRS), halves, s, LANES), F32),
            pltpu.VMEM((len(C_PAIRS), halves, s, LANES), F32),
        ],
        compiler_params=_cparams(("parallel",)),
        name="attn_c",
    )(z, z, z)


def _sgu_kernel(zu_ref, zv_ref, lng_ref, lnb_ref, ws_ref, bs_ref, o_ref):
    tm = zu_ref.shape[1]
    u = jax.nn.gelu(zu_ref[0].astype(F32))
    v = jax.nn.gelu(zv_ref[0].astype(F32))
    mu = jnp.mean(v, -1, keepdims=True)
    var = jnp.mean(jnp.square(v - mu), -1, keepdims=True)
    vn = ((v - mu) * lax.rsqrt(var + EPS) * lng_ref[...] + lnb_ref[...]).astype(BF16)
    causal = (lax.broadcasted_iota(jnp.int32, (B_CHUNK, 1), 0)
              >= lax.broadcasted_iota(jnp.int32, (1, B_CHUNK), 1))
    first = lax.broadcasted_iota(jnp.int32, (1, LANES), 1) < B_CH
    zero = jnp.zeros((), BF16)
    for p in range(B_WIDTH // LANES):
        w0 = jnp.where(causal, ws_ref[2 * p], zero)
        w1 = jnp.where(causal, ws_ref[2 * p + 1], zero)
        cols = slice(p * LANES, (p + 1) * LANES)
        for c in range(tm // B_CHUNK):
            rows = slice(c * B_CHUNK, (c + 1) * B_CHUNK)
            vv = vn[rows, cols]
            sv = jnp.where(first,
                           jnp.dot(w0, vv, preferred_element_type=F32),
                           jnp.dot(w1, vv, preferred_element_type=F32)) + bs_ref[:, cols]
            o_ref[0, rows, cols] = (u[rows, cols] * sv).astype(o_ref.dtype)


def _sgu(z, ln_g, ln_b, w_s, b_s):
    bsz, s, _ = z.shape
    tm = 512
    bias = jnp.repeat(b_s.T, B_CH, axis=1)
    col_u = COL_BU * IN_TN // B_WIDTH
    col_v = COL_BV * IN_TN // B_WIDTH
    return pl.pallas_call(
        _sgu_kernel,
        out_shape=jax.ShapeDtypeStruct((bsz, s, B_WIDTH), BF16),
        grid=(bsz, s // tm),
        in_specs=[
            pl.BlockSpec((1, tm, B_WIDTH), lambda bi, i: (bi, i, col_u)),
            pl.BlockSpec((1, tm, B_WIDTH), lambda bi, i: (bi, i, col_v)),
            pl.BlockSpec((1, B_WIDTH), lambda bi, i: (0, 0)),
            pl.BlockSpec((1, B_WIDTH), lambda bi, i: (0, 0)),
            pl.BlockSpec((B_GROUPS, B_CHUNK, B_CHUNK), lambda bi, i: (0, 0, 0)),
            pl.BlockSpec((B_CHUNK, B_WIDTH), lambda bi, i: (0, 0)),
        ],
        out_specs=pl.BlockSpec((1, tm, B_WIDTH), lambda bi, i: (bi, i, 0)),
        compiler_params=_cparams(("parallel", "parallel")),
        name="sgu",
    )(z, z, ln_g.reshape(1, B_WIDTH), ln_b.reshape(1, B_WIDTH), w_s.astype(BF16), bias)


def _route(logits):
    lane = lax.broadcasted_iota(jnp.int32, (1, ROUTE_W), 1)
    lane_f = lane.astype(F32)
    is_g = lane < N_EXPERT_GROUPS
    lg = jnp.where(is_g, logits, NEG_INF)
    mg = jnp.max(lg, -1, keepdims=True)
    pg_top = 1.0 / jnp.sum(jnp.where(is_g, jnp.exp(lg - mg), 0.0), -1, keepdims=True)
    g_idx = jnp.min(jnp.where(lg == mg, lane_f, float(ROUTE_W)), -1, keepdims=True)
    e_group = ((lane - ROUTE_E0) // EXPERTS_PER_GROUP).astype(F32)
    in_grp = (lane >= ROUTE_E0) & (lane < ROUTE_E0 + N_EXPERTS) & (e_group == g_idx)
    le = jnp.where(in_grp, logits, NEG_INF)
    m1 = jnp.max(le, -1, keepdims=True)
    i1 = jnp.min(jnp.where(le == m1, lane_f, float(ROUTE_W)), -1, keepdims=True)
    le2 = jnp.where(lane_f == i1, NEG_INF, le)
    m2 = jnp.max(le2, -1, keepdims=True)
    i2 = jnp.min(jnp.where(le2 == m2, lane_f, float(ROUTE_W)), -1, keepdims=True)
    t = jnp.exp(m2 - m1)
    w1 = pg_top / (1.0 + t)
    w2 = w1 * t
    picks = (i1 - ROUTE_E0, i2 - ROUTE_E0, w1, w2)
    out = jnp.zeros(logits.shape, F32)
    for k, val in enumerate(picks):
        out = jnp.where(lane == k, val, out)
    return out


def _merge_kernel(x_ref, oa_ref, ob_ref, oc_ref,
                  g0_ref, g1_ref, g2_ref, g3_ref, wa_ref, wb_ref, wc_ref, wo_ref, n2_ref, wr_ref,
                  xo_ref, h2_ref, route_ref, counts_ref, pos_ref, xn0_ref, xn1_ref, cnt_ref):
    i = pl.program_id(0)
    n_tiles = pl.num_programs(0) - 1

    def residual_into(xn_ref):
        zg = jnp.concatenate([g0_ref[...], g1_ref[...], g2_ref[...], g3_ref[...]],
                             axis=1).astype(F32)
        gates = jax.nn.sigmoid(zg)
        pa = jnp.dot(oa_ref[...], wa_ref[...], preferred_element_type=F32)
        pb = jnp.dot(ob_ref[...], wb_ref[...], preferred_element_type=F32)
        pc = jnp.dot(oc_ref[...], wc_ref[...], preferred_element_type=F32)
        merged = (gates[:, :D_MODEL] * pa + gates[:, D_MODEL:2 * D_MODEL] * pb
                  + gates[:, 2 * D_MODEL:] * pc)
        xn_ref[...] = x_ref[...] + jnp.dot(merged.astype(BF16), wo_ref[...],
                                           preferred_element_type=F32)

    def tail_from(xn_ref):
        xn = xn_ref[...]
        xo_ref[...] = xn
        ms = jnp.mean(xn * xn, -1, keepdims=True)
        h2 = xn * lax.rsqrt(ms + EPS) * n2_ref[...]
        h2_ref[...] = _pack_bf16_pairs(h2)
        route = _route(jnp.dot(h2.astype(BF16), wr_ref[...], preferred_element_type=F32))
        tm = route.shape[0]
        lane = lax.broadcasted_iota(jnp.int32, (1, ROUTE_W), 1)
        lane_f = lane.astype(F32)
        row = lax.broadcasted_iota(jnp.int32, (tm, 1), 0)
        hit = [lane_f == route[:, k:k + 1] for k in range(TOP_K)]
        onehot = jnp.where(hit[0], 1.0, 0.0) + jnp.where(hit[1], 1.0, 0.0)
        scan = onehot
        shift = 1
        while shift < tm:
            scan = scan + jnp.where(row >= shift, pltpu.roll(scan, shift, 0), 0.0)
            shift *= 2
        before = scan - onehot + cnt_ref[0:1, :]
        capacity = pos_ref.shape[1] * LANES
        diag = (row & (LANES - 1)) == lane
        first_row = pl.multiple_of((i - 1) * (tm // LANES), tm // LANES)
        for k in range(TOP_K):
            rank = jnp.sum(jnp.where(hit[k], before, 0.0), -1, keepdims=True)
            slot = jnp.where(diag, route[:, k:k + 1] * capacity + rank, 0.0)
            dense = jnp.concatenate(
                [jnp.sum(slot[r * LANES:(r + 1) * LANES], axis=0, keepdims=True)
                 for r in range(tm // LANES)], axis=0)
            pos_ref[k, pl.ds(first_row, tm // LANES), :] = dense.astype(jnp.int32)
        route_ref[...] = route
        cnt = cnt_ref[0:1, :] + jnp.sum(onehot, axis=0, keepdims=True)
        cnt_ref[...] = jnp.broadcast_to(cnt, cnt_ref.shape)
        counts_ref[...] = jnp.broadcast_to(cnt, counts_ref.shape)

    bufs = (xn0_ref, xn1_ref)
    for parity in range(2):
        @pl.when((i > 0) & (i < n_tiles) & (i % 2 == parity))
        def _(parity=parity):
            tail_from(bufs[1 - parity])
            residual_into(bufs[parity])

        @pl.when((i == n_tiles) & (i % 2 == parity))
        def _(parity=parity):
            tail_from(bufs[1 - parity])

    @pl.when(i == 0)
    def _():
        cnt_ref[...] = jnp.zeros_like(cnt_ref)
        residual_into(bufs[0])


def _merge(x2, oa, ob, oc, z2, wa, wb, wc, wo, n2g, w_rg, w_re):
    n, d = x2.shape
    tm = 512
    wr = jnp.concatenate(
        [w_rg, w_re, jnp.zeros((d, ROUTE_W - N_EXPERT_GROUPS - N_EXPERTS), w_rg.dtype)], axis=1)
    gate_w = GATE_W // 4
    gate_col0 = COL_G * IN_TN // gate_w

    n_tiles = n // tm
    last = n_tiles - 1

    def rows(width):
        return pl.BlockSpec((tm, width), lambda i: (jnp.minimum(i, last), 0))

    def lagged_rows(width):
        return pl.BlockSpec((tm, width), lambda i: (jnp.maximum(i - 1, 0), 0))

    def full(shape):
        return pl.BlockSpec(shape, lambda i: (0, 0))

    gate_specs = [pl.BlockSpec((tm, gate_w), lambda i, k=k: (jnp.minimum(i, last), gate_col0 + k))
                  for k in range(4)]
    return pl.pallas_call(
        _merge_kernel,
        out_shape=[jax.ShapeDtypeStruct((n, d), F32),
                   jax.ShapeDtypeStruct((n, d // 2), jnp.int32),
                   jax.ShapeDtypeStruct((n, ROUTE_W), F32),
                   jax.ShapeDtypeStruct((SUBLANES, ROUTE_W), F32),
                   jax.ShapeDtypeStruct((TOP_K, n // LANES, LANES), jnp.int32)],
        grid=(n_tiles + 1,),
        in_specs=[rows(d), rows(A_Q_W), rows(B_WIDTH), rows(C_OUT_W),
                  *gate_specs,
                  full((A_Q_W, d)), full((B_WIDTH, d)), full((C_OUT_W, d)), full((d, d)),
                  full((1, d)), full((d, ROUTE_W))],
        out_specs=[lagged_rows(d), lagged_rows(d // 2), lagged_rows(ROUTE_W),
                   full((SUBLANES, ROUTE_W)),
                   pl.BlockSpec((TOP_K, n // LANES, LANES), lambda i: (0, 0, 0))],
        scratch_shapes=[pltpu.VMEM((tm, d), F32), pltpu.VMEM((tm, d), F32),
                        pltpu.VMEM((SUBLANES, ROUTE_W), F32)],
        compiler_params=_cparams(("arbitrary",)),
        name="merge",
    )(x2, oa, ob, oc, z2, z2, z2, z2,
      wa.astype(BF16), wb.astype(BF16), wc.astype(BF16), wo.astype(BF16),
      n2g.reshape(1, d), wr.astype(BF16))


MOE_TM = 1024
SC_CORES = 2
SC_SUBCORES = 16
SC_WORKERS = SC_CORES * SC_SUBCORES
SC_CHUNK = 128


def _pack_bf16_pairs(x):
    w = x.shape[1] // 2
    lo = pltpu.bitcast(x[:, :w].astype(BF16).astype(F32), jnp.int32)
    hi = pltpu.bitcast(x[:, w:].astype(BF16).astype(F32), jnp.int32)
    return (hi & jnp.int32(-65536)) | lax.shift_right_logical(lo, jnp.int32(16))


def _unpack_bf16_pairs(p):
    lo = pltpu.bitcast(lax.shift_left(p, jnp.int32(16)), F32)
    hi = pltpu.bitcast(p & jnp.int32(-65536), F32)
    return jnp.concatenate([lo, hi], axis=1)


def _sc_gather_rows(table, idx):
    rows, width = idx.shape[0], table.shape[1]
    per_worker = rows // SC_WORKERS
    n_chunks = per_worker // SC_CHUNK
    assert per_worker * SC_WORKERS == rows and n_chunks * SC_CHUNK == per_worker
    mesh = plsc.VectorSubcoreMesh(core_axis_name="c", subcore_axis_name="s",
                                  num_cores=SC_CORES, num_subcores=SC_SUBCORES)

    @functools.partial(
        pl.kernel, mesh=mesh,
        out_type=jax.ShapeDtypeStruct((rows, width), table.dtype),
        scratch_types=[pltpu.VMEM((n_chunks, SC_CHUNK), jnp.int32),
                       pltpu.VMEM((SC_CHUNK, width), table.dtype),
                       pltpu.SemaphoreType.DMA],
        name="sc_gather_rows",
    )
    def gather(table_hbm, idx_hbm, out_hbm, idx_v, rows_v, sem):
        wid = lax.axis_index("s") * SC_CORES + lax.axis_index("c")
        pltpu.sync_copy(idx_hbm.at[wid], idx_v)
        base = wid * per_worker

        @pl.loop(0, n_chunks)
        def _(c):
            pltpu.async_copy(table_hbm.at[idx_v.at[c]], rows_v, sem).wait()
            pltpu.sync_copy(rows_v, out_hbm.at[pl.ds(base + c * SC_CHUNK, SC_CHUNK)])

    return gather(table, idx.reshape(SC_WORKERS, n_chunks, SC_CHUNK))


def _sc_scatter_rows(rows, pos, n_slots):
    n, width = rows.shape
    per_worker = n // SC_WORKERS
    n_chunks = per_worker // SC_CHUNK
    assert per_worker * SC_WORKERS == n and n_chunks * SC_CHUNK == per_worker
    mesh = plsc.VectorSubcoreMesh(core_axis_name="c", subcore_axis_name="s",
                                  num_cores=SC_CORES, num_subcores=SC_SUBCORES)

    @functools.partial(
        pl.kernel, mesh=mesh,
        out_type=jax.ShapeDtypeStruct((n_slots, width), rows.dtype),
        scratch_types=[pltpu.VMEM((TOP_K, n_chunks, SC_CHUNK), jnp.int32),
                       pltpu.VMEM((SC_CHUNK, width), rows.dtype),
                       pltpu.SemaphoreType.DMA],
        name="sc_scatter_rows",
    )
    def scatter(rows_hbm, pos_hbm, out_hbm, pos_v, rows_v, sem):
        wid = lax.axis_index("s") * SC_CORES + lax.axis_index("c")
        for k in range(TOP_K):
            pltpu.sync_copy(pos_hbm.at[k, wid], pos_v.at[k])
        base = wid * per_worker

        @pl.loop(0, n_chunks)
        def _(c):
            pltpu.sync_copy(rows_hbm.at[pl.ds(base + c * SC_CHUNK, SC_CHUNK)], rows_v)
            for k in range(TOP_K):
                pltpu.async_copy(rows_v, out_hbm.at[pos_v.at[k, c]], sem).wait()

    return scatter(rows, pos.reshape(TOP_K, SC_WORKERS, n_chunks, SC_CHUNK))


def _dispatch_plan(counts, capacity, n_steps, expert_base):
    counts = counts[0, :N_EXPERTS].astype(jnp.int32)
    tiles = (counts + MOE_TM - 1) // MOE_TM
    tile_ends = jnp.cumsum(tiles)
    step = jnp.arange(n_steps, dtype=jnp.int32)
    used = step < tile_ends[-1]
    expert = jnp.minimum(jnp.sum((step[:, None] >= tile_ends[None, :]).astype(jnp.int32), axis=1),
                         N_EXPERTS - 1)
    chunk = step - (tile_ends - tiles)[expert]
    rows = jnp.where(used, jnp.clip(counts[expert] - chunk * MOE_TM, 0, MOE_TM), 0)
    first = jnp.concatenate([jnp.ones((1,), jnp.int32),
                             (expert[1:] != expert[:-1]).astype(jnp.int32)])
    block = expert * (capacity // MOE_TM) + chunk
    last_used = jnp.sum(jnp.where(step == tile_ends[-1] - 1, block, 0))
    spare = N_EXPERTS * (capacity // MOE_TM)
    blocks = jnp.stack([jnp.where(used, block, last_used), jnp.where(used, block, spare)])
    return expert + expert_base, rows.astype(jnp.int32), first, blocks.astype(jnp.int32)


def _moe_tile_kernel(te_ref, tr_ref, tf_ref, tb_ref, xs_ref, wg_ref, wu_ref, wd_ref, ys_ref,
                     wg_s, wu_s, wd_s):
    t = pl.program_id(0)

    @pl.when(tf_ref[t] != 0)
    def _():
        wg_s[...] = wg_ref[0].astype(BF16)
        wu_s[...] = wu_ref[0].astype(BF16)
        wd_s[...] = wd_ref[0].astype(BF16)

    @pl.when(tr_ref[t] != 0)
    def _():
        occupied = lax.broadcasted_iota(jnp.int32, (MOE_TM, 1), 0) < tr_ref[t]
        x = jnp.where(occupied, _unpack_bf16_pairs(xs_ref[...]), 0.0).astype(BF16)
        hg = jnp.dot(x, wg_s[...], preferred_element_type=F32)
        hu = jnp.dot(x, wu_s[...], preferred_element_type=F32)
        a = (jax.nn.silu(hg) * hu).astype(BF16)
        ys_ref[...] = _pack_bf16_pairs(jnp.dot(a, wd_s[...], preferred_element_type=F32))

    @pl.when(tr_ref[t] == 0)
    def _():
        ys_ref[...] = jnp.zeros_like(ys_ref)


def _moe_tiles(xs, tile_expert, tile_rows, tile_first, tile_blocks, w_gate, w_up, w_down):
    n_slots, half = xs.shape
    d = 2 * half
    grid_spec = pltpu.PrefetchScalarGridSpec(
        num_scalar_prefetch=4,
        grid=(tile_expert.shape[0],),
        in_specs=[
            pl.BlockSpec((MOE_TM, half), lambda t, te, tr, tf, tb: (tb[0, t], 0)),
            pl.BlockSpec((1, d, D_EXPERT), lambda t, te, tr, tf, tb: (te[t], 0, 0)),
            pl.BlockSpec((1, d, D_EXPERT), lambda t, te, tr, tf, tb: (te[t], 0, 0)),
            pl.BlockSpec((1, D_EXPERT, d), lambda t, te, tr, tf, tb: (te[t], 0, 0)),
        ],
        out_specs=pl.BlockSpec((MOE_TM, half), lambda t, te, tr, tf, tb: (tb[1, t], 0)),
        scratch_shapes=[pltpu.VMEM((d, D_EXPERT), BF16), pltpu.VMEM((d, D_EXPERT), BF16),
                        pltpu.VMEM((D_EXPERT, d), BF16)],
    )
    return pl.pallas_call(
        _moe_tile_kernel,
        out_shape=jax.ShapeDtypeStruct((n_slots, half), jnp.int32),
        grid_spec=grid_spec,
        compiler_params=_cparams(("arbitrary",)),
        name="moe_tiles",
    )(tile_expert, tile_rows, tile_first, tile_blocks, xs, w_gate, w_up, w_down)


def _moe_combine_kernel(x_ref, y0_ref, y1_ref, r_ref, g_ref, o_ref, *, final_norm):
    w0 = r_ref[:, 2:3]
    w1 = r_ref[:, 3:4]
    x = x_ref[...] + w0 * _unpack_bf16_pairs(y0_ref[0]) + w1 * _unpack_bf16_pairs(y1_ref[0])
    if final_norm:
        ms = jnp.mean(x * x, -1, keepdims=True)
        x = x * lax.rsqrt(ms + EPS) * g_ref[...]
    o_ref[...] = x


def _moe_combine(x2, yg, route, norm_g, final_norm):
    n, d = x2.shape
    tm = 1024
    return pl.pallas_call(
        functools.partial(_moe_combine_kernel, final_norm=final_norm),
        out_shape=jax.ShapeDtypeStruct((n, d), F32),
        grid=(n // tm,),
        in_specs=[
            pl.BlockSpec((tm, d), lambda i: (i, 0)),
            pl.BlockSpec((1, tm, d // 2), lambda i: (0, i, 0)),
            pl.BlockSpec((1, tm, d // 2), lambda i: (1, i, 0)),
            pl.BlockSpec((tm, ROUTE_W), lambda i: (i, 0)),
            pl.BlockSpec((1, d), lambda i: (0, 0)),
        ],
        out_specs=pl.BlockSpec((tm, d), lambda i: (i, 0)),
        compiler_params=_cparams(("parallel",)),
        name="moe_combine",
    )(x2, yg, yg, route, norm_g.reshape(1, d))


def _moe(h2p, route, counts, pos, x2, layer, w_gate, w_up, w_down, norm_f_g, final_norm):
    n, d = x2.shape
    n_slots = N_EXPERTS * n + MOE_TM
    n_steps = TOP_K * n // MOE_TM + N_EXPERTS
    tile_expert, tile_rows, tile_first, tile_blocks = _dispatch_plan(
        counts, n, n_steps, layer * N_EXPERTS)
    xs = _sc_scatter_rows(h2p, pos, n_slots)
    ys = _moe_tiles(xs, tile_expert, tile_rows, tile_first, tile_blocks, w_gate, w_up, w_down)
    yg = _sc_gather_rows(ys, pos.reshape(-1)).reshape(TOP_K, n, d // 2)
    return _moe_combine(x2, yg, route, norm_f_g, final_norm)


def kernel(x, positions, norm1_g, w_in, b_in, attn_sinks, sgu_ln_g, sgu_ln_b, w_spatial, b_spatial,
           w_proj_a, w_proj_b, w_proj_c, w_out, norm2_g, w_router_group, w_router_expert,
           w_expert_gate, w_expert_up, w_expert_down, norm_f_g):
    bsz, s, d = x.shape
    depth = w_in.shape[0]
    wg_all = w_expert_gate.reshape(depth * N_EXPERTS, d, D_EXPERT)
    wu_all = w_expert_up.reshape(depth * N_EXPERTS, d, D_EXPERT)
    wd_all = w_expert_down.reshape(depth * N_EXPERTS, D_EXPERT, d)
    w_in_b = w_in.astype(BF16)
    assert d == D_MODEL and s % (C_PAIRS[-1][1] * BLOCK) == 0
    tabs = _rope_tables(positions)
    for l in range(depth):
        z = _in_proj(x, norm1_g[l], w_in_b, b_in.reshape(depth, 1, D_IN), l, tabs)
        oa = _attn_a(z, attn_sinks[l])
        ob = _sgu(z, sgu_ln_g[l], sgu_ln_b[l], w_spatial[l], b_spatial[l])
        oc = _attn_c(z)
        x2, h2, route, counts, pos = _merge(
            x.reshape(bsz * s, d), oa.reshape(bsz * s, A_Q_W), ob.reshape(bsz * s, B_WIDTH),
            oc.reshape(bsz * s, C_OUT_W), z.reshape(bsz * s, D_IN), w_proj_a[l], w_proj_b[l], w_proj_c[l], w_out[l],
            norm2_g[l], w_router_group[l], w_router_expert[l])
        x = _moe(h2, route, counts, pos, x2, l, wg_all, wu_all, wd_all, norm_f_g, l == depth - 1).reshape(bsz, s, d)
    return x
```

```python
import functools

import jax
import jax.numpy as jnp
from jax import lax
from jax.experimental import pallas as pl
from jax.experimental.pallas import tpu as pltpu
from jax.experimental.pallas import tpu_sc as plsc

F32 = jnp.float32
BF16 = jnp.bfloat16

D_MODEL = 1024
HEAD_DIM = 64
ROT_DIM = HEAD_DIM // 4
ROPE_THETA = 500000.0
BLOCK = 128
EPS = 1e-5
NEG_INF = -1e30

A_Q_HEADS = 16
A_KV_HEADS = 4
A_REP = A_Q_HEADS // A_KV_HEADS
A_WINDOW = 128
A_Q_W = A_Q_HEADS * HEAD_DIM
A_KV_W = A_KV_HEADS * HEAD_DIM

B_GROUPS = 12
B_CH = 64
B_WIDTH = B_GROUPS * B_CH
B_CHUNK = 128

C_PAIRS = ((128, 1), (512, 4), (2048, 16))
C_HEADS_PER_GROUP = 4
C_HEADS = C_HEADS_PER_GROUP * len(C_PAIRS)
C_OUT_W = C_HEADS_PER_GROUP * HEAD_DIM

N_BRANCH = 3
GATE_W = N_BRANCH * D_MODEL
D_IN = A_Q_W + 2 * A_KV_W + 2 * B_WIDTH + 3 * C_HEADS * HEAD_DIM + GATE_W

N_EXPERT_GROUPS = 4
EXPERTS_PER_GROUP = 8
N_EXPERTS = N_EXPERT_GROUPS * EXPERTS_PER_GROUP
TOP_K = 2
D_EXPERT = 256

LANES = 128
MXU_W = 256
VMEM_LIMIT = 56 * 1024 * 1024

IN_TN = 256
COL_AQ = 0
COL_AK = A_Q_W // IN_TN
COL_AV = COL_AK + A_KV_W // IN_TN
COL_BU = COL_AV + A_KV_W // IN_TN
COL_BV = COL_BU + B_WIDTH // IN_TN
COL_CQ = COL_BV + B_WIDTH // IN_TN
COL_CK = COL_CQ + C_HEADS * HEAD_DIM // IN_TN
COL_CV = COL_CK + C_HEADS * HEAD_DIM // IN_TN
COL_G = COL_CV + C_HEADS * HEAD_DIM // IN_TN
N_COL_TILES = D_IN // IN_TN

ROUTE_W = LANES
ROUTE_E0 = N_EXPERT_GROUPS
ROUTE_RANK0 = 4
SUBLANES = 8


def _cparams(sem):
    return pltpu.CompilerParams(dimension_semantics=sem, vmem_limit_bytes=VMEM_LIMIT)


def _rope_table_kernel(pos_ref, inv_ref, c_ref, s1_ref, s2_ref):
    lane = lax.broadcasted_iota(jnp.int32, (1, LANES), 1)
    d = lane & (HEAD_DIM - 1)
    ang = pos_ref[...].astype(F32) * inv_ref[...]
    c = jnp.cos(ang)
    s = jnp.sin(ang)
    half = ROT_DIM // 2
    c_ref[...] = jnp.where(d < ROT_DIM, c, 1.0)
    s1_ref[...] = jnp.where(d < half, -s, 0.0)
    s2_ref[...] = jnp.where((d >= half) & (d < ROT_DIM), s, 0.0)


def _rope_tables(positions):
    n = positions.size
    inv = ROPE_THETA ** (-jnp.arange(0, ROT_DIM, 2, dtype=F32) / ROT_DIM)
    inv_lane = jnp.tile(inv, LANES // inv.shape[0]).reshape(1, LANES)
    pos_b = jnp.broadcast_to(positions.reshape(n, 1), (n, LANES))
    tm = 1024
    spec = pl.BlockSpec((tm, LANES), lambda i: (i, 0))
    return pl.pallas_call(
        _rope_table_kernel,
        out_shape=[jax.ShapeDtypeStruct((n, LANES), F32)] * 3,
        grid=(n // tm,),
        in_specs=[spec, pl.BlockSpec((1, LANES), lambda i: (0, 0))],
        out_specs=[spec, spec, spec],
        compiler_params=_cparams(("parallel",)),
        name="rope_tables",
    )(pos_b, inv_lane)


IN_STEP_TILES = 3
IN_STEP_W = IN_STEP_TILES * IN_TN
IN_STEPS = N_COL_TILES // IN_STEP_TILES


def _is_rope_tile(tile):
    return tile < COL_AV or COL_CQ <= tile < COL_CV


_STEP_ROPES = tuple(tuple(_is_rope_tile(step * IN_STEP_TILES + u) for u in range(IN_STEP_TILES))
                    for step in range(IN_STEPS))


def _in_proj_kernel(x_ref, g_ref, w_ref, b_ref, c_ref, s1_ref, s2_ref, z_ref, h_ref,
                    acc0_ref, acc1_ref):
    j = pl.program_id(1)

    @pl.when(j == 0)
    def _():
        x = x_ref[0]
        ms = jnp.mean(x * x, -1, keepdims=True)
        h_ref[...] = (x * lax.rsqrt(ms + EPS) * g_ref[...]).astype(BF16)

    prev = j - 1

    def matmul_into(acc_ref):
        acc_ref[...] = jnp.dot(h_ref[...], w_ref[0], preferred_element_type=F32) + b_ref[0]

    def finish(acc_ref, ropes):
        for u, rope in enumerate(ropes):
            cols = slice(u * IN_TN, (u + 1) * IN_TN)
            if not rope:
                z_ref[0, :, cols] = acc_ref[:, cols].astype(BF16)
                continue
            tile = prev * IN_STEP_TILES + u
            is_q = (tile < COL_AK) | ((tile >= COL_CQ) & (tile < COL_CK))
            scale = jnp.where(is_q, HEAD_DIM ** -0.5, 1.0).astype(F32)
            for t in range(u * IN_TN // LANES, (u + 1) * IN_TN // LANES):
                lanes = slice(t * LANES, (t + 1) * LANES)
                a = acc_ref[:, lanes]
                r = (a * c_ref[...] + pltpu.roll(a, LANES - ROT_DIM // 2, 1) * s1_ref[...]
                     + pltpu.roll(a, ROT_DIM // 2, 1) * s2_ref[...])
                z_ref[0, :, lanes] = (r * scale).astype(BF16)

    accs = (acc0_ref, acc1_ref)
    patterns = sorted(set(_STEP_ROPES))
    pattern_id = jnp.int32(0)
    for step, ropes in enumerate(_STEP_ROPES):
        pattern_id = jnp.where(prev == step, patterns.index(ropes), pattern_id)
    inner = (j > 0) & (j < IN_STEPS)
    for parity in range(2):
        for pid, ropes in enumerate(patterns):
            @pl.when(inner & (j % 2 == parity) & (pattern_id == pid))
            def _(parity=parity, ropes=ropes):
                matmul_into(accs[parity])
                finish(accs[1 - parity], ropes)

    @pl.when(j == 0)
    def _():
        matmul_into(accs[0])

    @pl.when(j == IN_STEPS)
    def _():
        finish(accs[(IN_STEPS - 1) % 2], _STEP_ROPES[-1])


def _in_proj(x, g, w_all, b_all, layer, tabs):
    bsz, s, d = x.shape
    c, s1, s2 = tabs
    tab_spec = pl.BlockSpec((s, LANES), lambda bi, j: (bi, 0))
    last = IN_STEPS - 1
    return pl.pallas_call(
        _in_proj_kernel,
        out_shape=jax.ShapeDtypeStruct((bsz, s, D_IN), BF16),
        grid=(bsz, IN_STEPS + 1),
        in_specs=[
            pl.BlockSpec((1, s, d), lambda bi, j: (bi, 0, 0)),
            pl.BlockSpec((1, d), lambda bi, j: (0, 0)),
            pl.BlockSpec((1, d, IN_STEP_W), lambda bi, j: (layer, 0, jnp.minimum(j, last))),
            pl.BlockSpec((1, 1, IN_STEP_W), lambda bi, j: (layer, 0, jnp.minimum(j, last))),
            tab_spec, tab_spec, tab_spec,
        ],
        out_specs=pl.BlockSpec((1, s, IN_STEP_W), lambda bi, j: (bi, 0, jnp.maximum(j - 1, 0))),
        scratch_shapes=[pltpu.VMEM((s, d), BF16), pltpu.VMEM((s, IN_STEP_W), F32),
                        pltpu.VMEM((s, IN_STEP_W), F32)],
        compiler_params=_cparams(("parallel", "arbitrary")),
        name="in_proj",
    )(x, g.reshape(1, d), w_all, b_all, c, s1, s2)


STACK = C_HEADS_PER_GROUP * BLOCK
STEP_UNROLL = 4
A_STEP_UNROLL = 8
FAST_STRIDE = 4


def _head_stack(blk):
    seg = lax.broadcasted_iota(jnp.int32, (1, MXU_W), 1) // HEAD_DIM
    rowseg = lax.broadcasted_iota(jnp.int32, (STACK, 1), 0) // BLOCK
    return jnp.where(seg == rowseg, jnp.concatenate([blk] * C_HEADS_PER_GROUP, axis=0),
                     jnp.zeros((), blk.dtype))


def _band_step(q, kp_ref, vp_ref, blk, n, has_prev, diag_key, sink_col, want_lse):
    return _band_blocks([(q, blk, n)], kp_ref, vp_ref, has_prev, diag_key, sink_col, want_lse)[0]


def _band_blocks(jobs, kp_ref, vp_ref, has_prev, diag_key, sink_col, want_lse):
    contract = (((1,), (1,)), ((), ()))
    i_loc = lax.broadcasted_iota(jnp.int32, (STACK, 1), 0) & (BLOCK - 1)
    jj = lax.broadcasted_iota(jnp.int32, (1, BLOCK), 1)
    upper = jj > i_loc
    seg = lax.broadcasted_iota(jnp.int32, (1, MXU_W), 1) // HEAD_DIM
    zero_b = jnp.zeros((), BF16)

    def key_rows(blk):
        lo = pl.multiple_of(blk * BLOCK, BLOCK)
        if has_prev:
            return pl.ds(lo, 2 * BLOCK)
        return pl.ds(pl.multiple_of(lo + BLOCK, BLOCK), BLOCK)

    scores = [lax.dot_general(_head_stack(q), kp_ref[key_rows(blk), :], contract,
                              preferred_element_type=F32) for q, blk, _ in jobs]

    def own_lanes(x):
        out = x[0:BLOCK]
        for h in range(1, C_HEADS_PER_GROUP):
            out = jnp.where(seg == h, x[h * BLOCK:(h + 1) * BLOCK], out)
        return out

    probs, stats = [], []
    for s, (_, blk, n) in zip(scores, jobs):
        if has_prev:
            sp = s[:, :BLOCK] + jnp.where(n == 0, NEG_INF, 0.0).astype(F32)
            f = jnp.where(upper, sp, s[:, BLOCK:])
        else:
            f = jnp.where(upper, NEG_INF, s)
        m = jnp.max(f, -1, keepdims=True)
        if diag_key:
            sd = jnp.sum(jnp.where(jj == i_loc, sp, 0.0), -1, keepdims=True)
            m = jnp.maximum(m, sd)
        if sink_col is not None:
            m = jnp.maximum(m, sink_col)
        p = jnp.exp(f - m)
        den = jnp.sum(p, -1, keepdims=True)
        pd = None
        if diag_key:
            pd = jnp.exp(sd - m)
            den = den + pd
        if sink_col is not None:
            den = den + jnp.exp(sink_col - m)
        inv = 1.0 / den
        pn = (p * inv).astype(BF16)
        if has_prev:
            below = zero_b
            if diag_key:
                below = jnp.where(jj == i_loc, pd * inv, 0.0).astype(BF16)
            pn = jnp.concatenate([jnp.where(upper, pn, below), jnp.where(upper, zero_b, pn)],
                                 axis=1)
        probs.append(pn)
        stats.append((m, den))

    outs = []
    for pn, (m, den), (_, blk, _) in zip(probs, stats, jobs):
        o = own_lanes(jnp.dot(pn, vp_ref[key_rows(blk), :], preferred_element_type=F32))
        lse = own_lanes(m + jnp.log(den)) + jnp.zeros((BLOCK, MXU_W), F32) if want_lse else None
        outs.append((o, lse))
    return outs


def _replicate_head(blk, g):
    words = pltpu.bitcast(blk, jnp.uint32)
    tile = words[:, (g // 2) * LANES:(g // 2 + 1) * LANES]
    swapped = pltpu.roll(tile, HEAD_DIM, 1)
    low = lax.broadcasted_iota(jnp.int32, (1, LANES), 1) < HEAD_DIM
    both = jnp.where(low, tile, swapped) if g % 2 == 0 else jnp.where(low, swapped, tile)
    return pltpu.bitcast(jnp.concatenate([both, both], axis=1), BF16)


def _attn_a_kernel(sink_ref, q_ref, k_ref, v_ref, o_ref, kp_ref, vp_ref):
    nb = q_ref.shape[1] // BLOCK
    rowseg = lax.broadcasted_iota(jnp.int32, (STACK, 1), 0) // BLOCK
    kp_ref[pl.ds(0, BLOCK), :] = jnp.zeros((BLOCK, MXU_W), BF16)
    vp_ref[pl.ds(0, BLOCK), :] = jnp.zeros((BLOCK, MXU_W), BF16)
    for g in range(A_KV_HEADS):
        q_cols = slice(g * MXU_W, (g + 1) * MXU_W)

        def build(n, carry):
            rows = pl.ds(pl.multiple_of(n * BLOCK, BLOCK), BLOCK)
            dst = pl.ds(pl.multiple_of((n + 1) * BLOCK, BLOCK), BLOCK)
            kp_ref[dst, :] = _replicate_head(k_ref[0, rows, :], g)
            vp_ref[dst, :] = _replicate_head(v_ref[0, rows, :], g)
            return carry

        lax.fori_loop(0, nb, build, 0, unroll=8)
        sink_col = jnp.zeros((STACK, 1), F32)
        for r in range(A_REP):
            sink_col = jnp.where(rowseg == r, sink_ref[g * A_REP + r], sink_col)

        def step(it, carry):
            blocks = [it * A_STEP_UNROLL + u for u in range(A_STEP_UNROLL)]
            rows = [pl.ds(pl.multiple_of(n * BLOCK, BLOCK), BLOCK) for n in blocks]
            jobs = [(q_ref[0, r, q_cols], n, n) for r, n in zip(rows, blocks)]
            outs = _band_blocks(jobs, kp_ref, vp_ref, True, False, sink_col, False)
            for r, (o, _) in zip(rows, outs):
                o_ref[0, r, q_cols] = o.astype(o_ref.dtype)
            return carry

        lax.fori_loop(0, nb // A_STEP_UNROLL, step, 0)


def _attn_a(z, sinks):
    bsz, s, _ = z.shape
    nb = s // BLOCK
    grid_spec = pltpu.PrefetchScalarGridSpec(
        num_scalar_prefetch=1,
        grid=(bsz,),
        in_specs=[
            pl.BlockSpec((1, s, A_Q_W), lambda bi, sk: (bi, 0, 0)),
            pl.BlockSpec((1, s, A_KV_W), lambda bi, sk: (bi, 0, COL_AK)),
            pl.BlockSpec((1, s, A_KV_W), lambda bi, sk: (bi, 0, COL_AV)),
        ],
        out_specs=pl.BlockSpec((1, s, A_Q_W), lambda bi, sk: (bi, 0, 0)),
        scratch_shapes=[pltpu.VMEM(((nb + 1) * BLOCK, MXU_W), BF16),
                        pltpu.VMEM(((nb + 1) * BLOCK, MXU_W), BF16)],
    )
    return pl.pallas_call(
        _attn_a_kernel,
        out_shape=jax.ShapeDtypeStruct((bsz, s, A_Q_W), BF16),
        grid_spec=grid_spec,
        compiler_params=_cparams(("parallel",)),
        name="attn_a",
    )(sinks, z, z, z)


def _attn_c_kernel(q_ref, k_ref, v_ref, oc_ref, stage_ref, stage2_ref, pq_ref, pk_ref, pv_ref,
                   og_ref, lg_ref):
    s = q_ref.shape[1]
    nblk = s // BLOCK
    halves = MXU_W // LANES
    pk_ref[pl.ds(0, BLOCK), :] = jnp.zeros((BLOCK, MXU_W), BF16)
    pv_ref[pl.ds(0, BLOCK), :] = jnp.zeros((BLOCK, MXU_W), BF16)

    def class_rows(idx, nb, d):
        r = idx // nb
        n = idx - r * nb
        start = r + n * (BLOCK * d)
        rows = pl.ds(start, BLOCK, stride=d) if d > 1 else pl.ds(pl.multiple_of(start, BLOCK), BLOCK)
        return n, rows

    quarter = s // FAST_STRIDE

    def stage(src_ref, cols, d):
        x = src_ref[0, :, cols].astype(F32)
        for t in range(halves):
            stage_ref[t] = x[:, t * LANES:(t + 1) * LANES]
        if d > FAST_STRIDE:
            for t in range(halves):
                for r1 in range(FAST_STRIDE):
                    stage2_ref[t, pl.ds(r1 * quarter, quarter), :] = (
                        stage_ref[t, pl.ds(r1, quarter, stride=FAST_STRIDE), :])

    def staged_block(idx, nb, d):
        if d > FAST_STRIDE:
            assert nb == 1 and d == FAST_STRIDE * FAST_STRIDE
            start = (idx % FAST_STRIDE) * quarter + idx // FAST_STRIDE
            rows = pl.ds(start, BLOCK, stride=FAST_STRIDE)
            parts = [stage2_ref[t, rows, :] for t in range(halves)]
        else:
            _, rows = class_rows(idx, nb, d)
            parts = [stage_ref[t, rows, :] for t in range(halves)]
        return jnp.concatenate(parts, axis=1).astype(BF16)

    for g, (win, d) in enumerate(C_PAIRS):
        cols = slice(g * MXU_W, (g + 1) * MXU_W)
        nb = nblk // d
        has_prev = nb > 1
        assert win // d == BLOCK

        def class_major(dst_ref, offset):
            def build(idx, carry):
                dst = pl.ds(pl.multiple_of((idx + offset) * BLOCK, BLOCK), BLOCK)
                dst_ref[dst, :] = staged_block(idx, nb, d)
                return carry
            return build

        for src_ref, dst_ref, offset in ((k_ref, pk_ref, 1), (v_ref, pv_ref, 1), (q_ref, pq_ref, 0)):
            if d == 1:
                dst_ref[pl.ds(offset * BLOCK, s), :] = src_ref[0, :, cols]
                continue
            stage(src_ref, cols, d)
            lax.fori_loop(0, nblk, class_major(dst_ref, offset), 0, unroll=4)

        def step(it, carry):
            jobs, dsts = [], []
            for u in range(STEP_UNROLL):
                idx = it * STEP_UNROLL + u
                n, rows = class_rows(idx, nb, d)
                here = pl.ds(pl.multiple_of(idx * BLOCK, BLOCK), BLOCK)
                jobs.append((pq_ref[here, :], idx, n))
                dsts.append(rows)
            outs = _band_blocks(jobs, pk_ref, pv_ref, has_prev, has_prev, None, True)
            for rows, (o, lse) in zip(dsts, outs):
                for t in range(halves):
                    og_ref[g, t, rows, :] = o[:, t * LANES:(t + 1) * LANES]
                    lg_ref[g, t, rows, :] = lse[:, t * LANES:(t + 1) * LANES]
            return carry

        lax.fori_loop(0, nblk // STEP_UNROLL, step, 0)

    chunk = 2 * BLOCK

    def combine(i, carry):
        rows = pl.ds(pl.multiple_of(i * chunk, chunk), chunk)
        for t in range(halves):
            ls = [lg_ref[g, t, rows, :] for g in range(len(C_PAIRS))]
            m = jnp.maximum(jnp.maximum(ls[0], ls[1]), ls[2])
            es = [jnp.exp(l - m) for l in ls]
            num = (es[0] * og_ref[0, t, rows, :] + es[1] * og_ref[1, t, rows, :]
                   + es[2] * og_ref[2, t, rows, :])
            oc_ref[0, rows, t * LANES:(t + 1) * LANES] = (
                num / (es[0] + es[1] + es[2])).astype(oc_ref.dtype)
        return carry

    lax.fori_loop(0, s // chunk, combine, 0)


def _attn_c(z):
    bsz, s, _ = z.shape
    width = C_HEADS * HEAD_DIM
    halves = MXU_W // LANES

    def in_spec(col):
        return pl.BlockSpec((1, s, width), lambda bi: (bi, 0, col * IN_TN // width))

    return pl.pallas_call(
        _attn_c_kernel,
        out_shape=jax.ShapeDtypeStruct((bsz, s, C_OUT_W), BF16),
        grid=(bsz,),
        in_specs=[in_spec(COL_CQ), in_spec(COL_CK), in_spec(COL_CV)],
        out_specs=pl.BlockSpec((1, s, C_OUT_W), lambda bi: (bi, 0, 0)),
        scratch_shapes=[
            pltpu.VMEM((halves, s, LANES), F32),
            pltpu.VMEM((halves, s, LANES), F32),
            pltpu.VMEM((s, MXU_W), BF16),
            pltpu.VMEM((s + BLOCK, MXU_W), BF16),
            pltpu.VMEM((s + BLOCK, MXU_W), BF16),
            pltpu.VMEM((len(C_PAIRS), halves, s, LANES), F32),
            pltpu.VMEM((len(C_PAIRS), halves, s, LANES), F32),
        ],
        compiler_params=_cparams(("parallel",)),
        name="attn_c",
    )(z, z, z)


def _sgu_kernel(zu_ref, zv_ref, lng_ref, lnb_ref, ws_ref, bs_ref, o_ref):
    tm = zu_ref.shape[1]
    u = jax.nn.gelu(zu_ref[0]).astype(F32)
    v = jax.nn.gelu(zv_ref[0]).astype(F32)
    mu = jnp.mean(v, -1, keepdims=True)
    var = jnp.mean(jnp.square(v - mu), -1, keepdims=True)
    vn = ((v - mu) * lax.rsqrt(var + EPS) * lng_ref[...] + lnb_ref[...]).astype(BF16)
    causal = (lax.broadcasted_iota(jnp.int32, (B_CHUNK, 1), 0)
              >= lax.broadcasted_iota(jnp.int32, (1, B_CHUNK), 1))
    first = lax.broadcasted_iota(jnp.int32, (1, LANES), 1) < B_CH
    zero = jnp.zeros((), BF16)
    for p in range(B_WIDTH // LANES):
        w0 = jnp.where(causal, ws_ref[2 * p], zero)
        w1 = jnp.where(causal, ws_ref[2 * p + 1], zero)
        cols = slice(p * LANES, (p + 1) * LANES)
        for c in range(tm // B_CHUNK):
            rows = slice(c * B_CHUNK, (c + 1) * B_CHUNK)
            vv = vn[rows, cols]
            sv = jnp.where(first,
                           jnp.dot(w0, vv, preferred_element_type=F32),
                           jnp.dot(w1, vv, preferred_element_type=F32)) + bs_ref[:, cols]
            o_ref[0, rows, cols] = (u[rows, cols] * sv).astype(o_ref.dtype)


def _sgu(z, ln_g, ln_b, w_s, b_s):
    bsz, s, _ = z.shape
    tm = 512
    bias = jnp.repeat(b_s.T, B_CH, axis=1)
    col_u = COL_BU * IN_TN // B_WIDTH
    col_v = COL_BV * IN_TN // B_WIDTH
    return pl.pallas_call(
        _sgu_kernel,
        out_shape=jax.ShapeDtypeStruct((bsz, s, B_WIDTH), BF16),
        grid=(bsz, s // tm),
        in_specs=[
            pl.BlockSpec((1, tm, B_WIDTH), lambda bi, i: (bi, i, col_u)),
            pl.BlockSpec((1, tm, B_WIDTH), lambda bi, i: (bi, i, col_v)),
            pl.BlockSpec((1, B_WIDTH), lambda bi, i: (0, 0)),
            pl.BlockSpec((1, B_WIDTH), lambda bi, i: (0, 0)),
            pl.BlockSpec((B_GROUPS, B_CHUNK, B_CHUNK), lambda bi, i: (0, 0, 0)),
            pl.BlockSpec((B_CHUNK, B_WIDTH), lambda bi, i: (0, 0)),
        ],
        out_specs=pl.BlockSpec((1, tm, B_WIDTH), lambda bi, i: (bi, i, 0)),
        compiler_params=_cparams(("parallel", "parallel")),
        name="sgu",
    )(z, z, ln_g.reshape(1, B_WIDTH), ln_b.reshape(1, B_WIDTH), w_s.astype(BF16), bias)


def _route(logits):
    lane = lax.broadcasted_iota(jnp.int32, (1, ROUTE_W), 1)
    lane_f = lane.astype(F32)
    is_g = lane < N_EXPERT_GROUPS
    lg = jnp.where(is_g, logits, NEG_INF)
    mg = jnp.max(lg, -1, keepdims=True)
    pg_top = 1.0 / jnp.sum(jnp.where(is_g, jnp.exp(lg - mg), 0.0), -1, keepdims=True)
    g_idx = jnp.min(jnp.where(lg == mg, lane_f, float(ROUTE_W)), -1, keepdims=True)
    e_group = ((lane - ROUTE_E0) // EXPERTS_PER_GROUP).astype(F32)
    in_grp = (lane >= ROUTE_E0) & (lane < ROUTE_E0 + N_EXPERTS) & (e_group == g_idx)
    le = jnp.where(in_grp, logits, NEG_INF)
    m1 = jnp.max(le, -1, keepdims=True)
    i1 = jnp.min(jnp.where(le == m1, lane_f, float(ROUTE_W)), -1, keepdims=True)
    le2 = jnp.where(lane_f == i1, NEG_INF, le)
    m2 = jnp.max(le2, -1, keepdims=True)
    i2 = jnp.min(jnp.where(le2 == m2, lane_f, float(ROUTE_W)), -1, keepdims=True)
    t = jnp.exp(m2 - m1)
    w1 = pg_top / (1.0 + t)
    w2 = w1 * t
    picks = (i1 - ROUTE_E0, i2 - ROUTE_E0, w1, w2)
    out = jnp.zeros(logits.shape, F32)
    for k, val in enumerate(picks):
        out = jnp.where(lane == k, val, out)
    return out


def _merge_kernel(x_ref, oa_ref, ob_ref, oc_ref,
                  g0_ref, g1_ref, g2_ref, g3_ref, wa_ref, wb_ref, wc_ref, wo_ref, n2_ref, wr_ref,
                  xo_ref, h2_ref, route_ref, counts_ref, pos_ref, xn0_ref, xn1_ref, cnt_ref):
    i = pl.program_id(0)
    n_tiles = pl.num_programs(0) - 1

    def residual_into(xn_ref):
        zg = jnp.concatenate([g0_ref[...], g1_ref[...], g2_ref[...], g3_ref[...]],
                             axis=1).astype(F32)
        gates = jax.nn.sigmoid(zg)
        pa = jnp.dot(oa_ref[...], wa_ref[...], preferred_element_type=F32)
        pb = jnp.dot(ob_ref[...], wb_ref[...], preferred_element_type=F32)
        pc = jnp.dot(oc_ref[...], wc_ref[...], preferred_element_type=F32)
        merged = (gates[:, :D_MODEL] * pa + gates[:, D_MODEL:2 * D_MODEL] * pb
                  + gates[:, 2 * D_MODEL:] * pc)
        xn_ref[...] = x_ref[...] + jnp.dot(merged.astype(BF16), wo_ref[...],
                                           preferred_element_type=F32)

    def tail_from(xn_ref):
        xn = xn_ref[...]
        xo_ref[...] = xn
        ms = jnp.mean(xn * xn, -1, keepdims=True)
        h2 = xn * lax.rsqrt(ms + EPS) * n2_ref[...]
        h2_ref[...] = _pack_bf16_pairs(h2)
        route = _route(jnp.dot(h2.astype(BF16), wr_ref[...], preferred_element_type=F32))
        tm = route.shape[0]
        lane = lax.broadcasted_iota(jnp.int32, (1, ROUTE_W), 1)
        lane_f = lane.astype(F32)
        row = lax.broadcasted_iota(jnp.int32, (tm, 1), 0)
        hit = [lane_f == route[:, k:k + 1] for k in range(TOP_K)]
        onehot = jnp.where(hit[0], 1.0, 0.0) + jnp.where(hit[1], 1.0, 0.0)
        scan = onehot
        shift = 1
        while shift < tm:
            scan = scan + jnp.where(row >= shift, pltpu.roll(scan, shift, 0), 0.0)
            shift *= 2
        before = scan - onehot + cnt_ref[0:1, :]
        capacity = pos_ref.shape[1] * LANES
        diag = (row & (LANES - 1)) == lane
        first_row = pl.multiple_of((i - 1) * (tm // LANES), tm // LANES)
        for k in range(TOP_K):
            rank = jnp.sum(jnp.where(hit[k], before, 0.0), -1, keepdims=True)
            slot = jnp.where(diag, route[:, k:k + 1] * capacity + rank, 0.0)
            dense = jnp.concatenate(
                [jnp.sum(slot[r * LANES:(r + 1) * LANES], axis=0, keepdims=True)
                 for r in range(tm // LANES)], axis=0)
            pos_ref[k, pl.ds(first_row, tm // LANES), :] = dense.astype(jnp.int32)
        route_ref[...] = route
        cnt = cnt_ref[0:1, :] + jnp.sum(onehot, axis=0, keepdims=True)
        cnt_ref[...] = jnp.broadcast_to(cnt, cnt_ref.shape)
        counts_ref[...] = jnp.broadcast_to(cnt, counts_ref.shape)

    bufs = (xn0_ref, xn1_ref)
    for parity in range(2):
        @pl.when((i > 0) & (i < n_tiles) & (i % 2 == parity))
        def _(parity=parity):
            tail_from(bufs[1 - parity])
            residual_into(bufs[parity])

        @pl.when((i == n_tiles) & (i % 2 == parity))
        def _(parity=parity):
            tail_from(bufs[1 - parity])

    @pl.when(i == 0)
    def _():
        cnt_ref[...] = jnp.zeros_like(cnt_ref)
        residual_into(bufs[0])


def _merge(x2, oa, ob, oc, z2, wa, wb, wc, wo, n2g, w_rg, w_re):
    n, d = x2.shape
    tm = 512
    wr = jnp.concatenate(
        [w_rg, w_re, jnp.zeros((d, ROUTE_W - N_EXPERT_GROUPS - N_EXPERTS), w_rg.dtype)], axis=1)
    gate_w = GATE_W // 4
    gate_col0 = COL_G * IN_TN // gate_w

    n_tiles = n // tm
    last = n_tiles - 1

    def rows(width):
        return pl.BlockSpec((tm, width), lambda i: (jnp.minimum(i, last), 0))

    def lagged_rows(width):
        return pl.BlockSpec((tm, width), lambda i: (jnp.maximum(i - 1, 0), 0))

    def full(shape):
        return pl.BlockSpec(shape, lambda i: (0, 0))

    gate_specs = [pl.BlockSpec((tm, gate_w), lambda i, k=k: (jnp.minimum(i, last), gate_col0 + k))
                  for k in range(4)]
    return pl.pallas_call(
        _merge_kernel,
        out_shape=[jax.ShapeDtypeStruct((n, d), F32),
                   jax.ShapeDtypeStruct((n, d // 2), jnp.int32),
                   jax.ShapeDtypeStruct((n, ROUTE_W), F32),
                   jax.ShapeDtypeStruct((SUBLANES, ROUTE_W), F32),
                   jax.ShapeDtypeStruct((TOP_K, n // LANES, LANES), jnp.int32)],
        grid=(n_tiles + 1,),
        in_specs=[rows(d), rows(A_Q_W), rows(B_WIDTH), rows(C_OUT_W),
                  *gate_specs,
                  full((A_Q_W, d)), full((B_WIDTH, d)), full((C_OUT_W, d)), full((d, d)),
                  full((1, d)), full((d, ROUTE_W))],
        out_specs=[lagged_rows(d), lagged_rows(d // 2), lagged_rows(ROUTE_W),
                   full((SUBLANES, ROUTE_W)),
                   pl.BlockSpec((TOP_K, n // LANES, LANES), lambda i: (0, 0, 0))],
        scratch_shapes=[pltpu.VMEM((tm, d), F32), pltpu.VMEM((tm, d), F32),
                        pltpu.VMEM((SUBLANES, ROUTE_W), F32)],
        compiler_params=_cparams(("arbitrary",)),
        name="merge",
    )(x2, oa, ob, oc, z2, z2, z2, z2,
      wa.astype(BF16), wb.astype(BF16), wc.astype(BF16), wo.astype(BF16),
      n2g.reshape(1, d), wr.astype(BF16))


MOE_TM = 1024
SC_CORES = 2
SC_SUBCORES = 16
SC_WORKERS = SC_CORES * SC_SUBCORES
SC_CHUNK = 128


def _pack_bf16_pairs(x):
    w = x.shape[1] // 2
    lo = pltpu.bitcast(x[:, :w].astype(BF16).astype(F32), jnp.int32)
    hi = pltpu.bitcast(x[:, w:].astype(BF16).astype(F32), jnp.int32)
    return (hi & jnp.int32(-65536)) | lax.shift_right_logical(lo, jnp.int32(16))


def _unpack_bf16_pairs(p):
    lo = pltpu.bitcast(lax.shift_left(p, jnp.int32(16)), F32)
    hi = pltpu.bitcast(p & jnp.int32(-65536), F32)
    return jnp.concatenate([lo, hi], axis=1)


def _sc_gather_rows(table, idx):
    rows, width = idx.shape[0], table.shape[1]
    per_worker = rows // SC_WORKERS
    n_chunks = per_worker // SC_CHUNK
    assert per_worker * SC_WORKERS == rows and n_chunks * SC_CHUNK == per_worker
    mesh = plsc.VectorSubcoreMesh(core_axis_name="c", subcore_axis_name="s",
                                  num_cores=SC_CORES, num_subcores=SC_SUBCORES)

    @functools.partial(
        pl.kernel, mesh=mesh,
        out_type=jax.ShapeDtypeStruct((rows, width), table.dtype),
        scratch_types=[pltpu.VMEM((n_chunks, SC_CHUNK), jnp.int32),
                       pltpu.VMEM((SC_CHUNK, width), table.dtype),
                       pltpu.SemaphoreType.DMA],
        name="sc_gather_rows",
    )
    def gather(table_hbm, idx_hbm, out_hbm, idx_v, rows_v, sem):
        wid = lax.axis_index("s") * SC_CORES + lax.axis_index("c")
        pltpu.sync_copy(idx_hbm.at[wid], idx_v)
        base = wid * per_worker

        @pl.loop(0, n_chunks)
        def _(c):
            pltpu.async_copy(table_hbm.at[idx_v.at[c]], rows_v, sem).wait()
            pltpu.sync_copy(rows_v, out_hbm.at[pl.ds(base + c * SC_CHUNK, SC_CHUNK)])

    return gather(table, idx.reshape(SC_WORKERS, n_chunks, SC_CHUNK))


def _sc_scatter_rows(rows, pos, n_slots):
    n, width = rows.shape
    per_worker = n // SC_WORKERS
    n_chunks = per_worker // SC_CHUNK
    assert per_worker * SC_WORKERS == n and n_chunks * SC_CHUNK == per_worker
    mesh = plsc.VectorSubcoreMesh(core_axis_name="c", subcore_axis_name="s",
                                  num_cores=SC_CORES, num_subcores=SC_SUBCORES)

    @functools.partial(
        pl.kernel, mesh=mesh,
        out_type=jax.ShapeDtypeStruct((n_slots, width), rows.dtype),
        scratch_types=[pltpu.VMEM((TOP_K, n_chunks, SC_CHUNK), jnp.int32),
                       pltpu.VMEM((SC_CHUNK, width), rows.dtype),
                       pltpu.SemaphoreType.DMA],
        name="sc_scatter_rows",
    )
    def scatter(rows_hbm, pos_hbm, out_hbm, pos_v, rows_v, sem):
        wid = lax.axis_index("s") * SC_CORES + lax.axis_index("c")
        for k in range(TOP_K):
            pltpu.sync_copy(pos_hbm.at[k, wid], pos_v.at[k])
        base = wid * per_worker

        @pl.loop(0, n_chunks)
        def _(c):
            pltpu.sync_copy(rows_hbm.at[pl.ds(base + c * SC_CHUNK, SC_CHUNK)], rows_v)
            for k in range(TOP_K):
                pltpu.async_copy(rows_v, out_hbm.at[pos_v.at[k, c]], sem).wait()

    return scatter(rows, pos.reshape(TOP_K, SC_WORKERS, n_chunks, SC_CHUNK))


def _dispatch_plan(counts, capacity, n_steps, expert_base):
    counts = counts[0, :N_EXPERTS].astype(jnp.int32)
    tiles = (counts + MOE_TM - 1) // MOE_TM
    tile_ends = jnp.cumsum(tiles)
    step = jnp.arange(n_steps, dtype=jnp.int32)
    used = step < tile_ends[-1]
    expert = jnp.minimum(jnp.sum((step[:, None] >= tile_ends[None, :]).astype(jnp.int32), axis=1),
                         N_EXPERTS - 1)
    chunk = step - (tile_ends - tiles)[expert]
    rows = jnp.where(used, jnp.clip(counts[expert] - chunk * MOE_TM, 0, MOE_TM), 0)
    first = jnp.concatenate([jnp.ones((1,), jnp.int32),
                             (expert[1:] != expert[:-1]).astype(jnp.int32)])
    block = expert * (capacity // MOE_TM) + chunk
    last_used = jnp.sum(jnp.where(step == tile_ends[-1] - 1, block, 0))
    spare = N_EXPERTS * (capacity // MOE_TM)
    blocks = jnp.stack([jnp.where(used, block, last_used), jnp.where(used, block, spare)])
    return expert + expert_base, rows.astype(jnp.int32), first, blocks.astype(jnp.int32)


def _moe_tile_kernel(te_ref, tr_ref, tf_ref, tb_ref, xs_ref, wg_ref, wu_ref, wd_ref, ys_ref,
                     wg_s, wu_s, wd_s):
    t = pl.program_id(0)

    @pl.when(tf_ref[t] != 0)
    def _():
        wg_s[...] = wg_ref[0].astype(BF16)
        wu_s[...] = wu_ref[0].astype(BF16)
        wd_s[...] = wd_ref[0].astype(BF16)

    @pl.when(tr_ref[t] != 0)
    def _():
        occupied = lax.broadcasted_iota(jnp.int32, (MOE_TM, 1), 0) < tr_ref[t]
        x = jnp.where(occupied, _unpack_bf16_pairs(xs_ref[...]), 0.0).astype(BF16)
        hg = jnp.dot(x, wg_s[...], preferred_element_type=F32)
        hu = jnp.dot(x, wu_s[...], preferred_element_type=F32)
        a = (jax.nn.silu(hg) * hu).astype(BF16)
        ys_ref[...] = _pack_bf16_pairs(jnp.dot(a, wd_s[...], preferred_element_type=F32))

    @pl.when(tr_ref[t] == 0)
    def _():
        ys_ref[...] = jnp.zeros_like(ys_ref)


def _moe_tiles(xs, tile_expert, tile_rows, tile_first, tile_blocks, w_gate, w_up, w_down):
    n_slots, half = xs.shape
    d = 2 * half
    grid_spec = pltpu.PrefetchScalarGridSpec(
        num_scalar_prefetch=4,
        grid=(tile_expert.shape[0],),
        in_specs=[
            pl.BlockSpec((MOE_TM, half), lambda t, te, tr, tf, tb: (tb[0, t], 0)),
            pl.BlockSpec((1, d, D_EXPERT), lambda t, te, tr, tf, tb: (te[t], 0, 0)),
            pl.BlockSpec((1, d, D_EXPERT), lambda t, te, tr, tf, tb: (te[t], 0, 0)),
            pl.BlockSpec((1, D_EXPERT, d), lambda t, te, tr, tf, tb: (te[t], 0, 0)),
        ],
        out_specs=pl.BlockSpec((MOE_TM, half), lambda t, te, tr, tf, tb: (tb[1, t], 0)),
        scratch_shapes=[pltpu.VMEM((d, D_EXPERT), BF16), pltpu.VMEM((d, D_EXPERT), BF16),
                        pltpu.VMEM((D_EXPERT, d), BF16)],
    )
    return pl.pallas_call(
        _moe_tile_kernel,
        out_shape=jax.ShapeDtypeStruct((n_slots, half), jnp.int32),
        grid_spec=grid_spec,
        compiler_params=_cparams(("arbitrary",)),
        name="moe_tiles",
    )(tile_expert, tile_rows, tile_first, tile_blocks, xs, w_gate, w_up, w_down)


def _moe_combine_kernel(x_ref, y0_ref, y1_ref, r_ref, g_ref, o_ref, *, final_norm):
    w0 = r_ref[:, 2:3]
    w1 = r_ref[:, 3:4]
    x = x_ref[...] + w0 * _unpack_bf16_pairs(y0_ref[0]) + w1 * _unpack_bf16_pairs(y1_ref[0])
    if final_norm:
        ms = jnp.mean(x * x, -1, keepdims=True)
        x = x * lax.rsqrt(ms + EPS) * g_ref[...]
    o_ref[...] = x


def _moe_combine(x2, yg, route, norm_g, final_norm):
    n, d = x2.shape
    tm = 1024
    return pl.pallas_call(
        functools.partial(_moe_combine_kernel, final_norm=final_norm),
        out_shape=jax.ShapeDtypeStruct((n, d), F32),
        grid=(n // tm,),
        in_specs=[
            pl.BlockSpec((tm, d), lambda i: (i, 0)),
            pl.BlockSpec((1, tm, d // 2), lambda i: (0, i, 0)),
            pl.BlockSpec((1, tm, d // 2), lambda i: (1, i, 0)),
            pl.BlockSpec((tm, ROUTE_W), lambda i: (i, 0)),
            pl.BlockSpec((1, d), lambda i: (0, 0)),
        ],
        out_specs=pl.BlockSpec((tm, d), lambda i: (i, 0)),
        compiler_params=_cparams(("parallel",)),
        name="moe_combine",
    )(x2, yg, yg, route, norm_g.reshape(1, d))


def _moe(h2p, route, counts, pos, x2, layer, w_gate, w_up, w_down, norm_f_g, final_norm):
    n, d = x2.shape
    n_slots = N_EXPERTS * n + MOE_TM
    n_steps = TOP_K * n // MOE_TM + N_EXPERTS
    tile_expert, tile_rows, tile_first, tile_blocks = _dispatch_plan(
        counts, n, n_steps, layer * N_EXPERTS)
    xs = _sc_scatter_rows(h2p, pos, n_slots)
    ys = _moe_tiles(xs, tile_expert, tile_rows, tile_first, tile_blocks, w_gate, w_up, w_down)
    yg = _sc_gather_rows(ys, pos.reshape(-1)).reshape(TOP_K, n, d // 2)
    return _moe_combine(x2, yg, route, norm_f_g, final_norm)


def kernel(x, positions, norm1_g, w_in, b_in, attn_sinks, sgu_ln_g, sgu_ln_b, w_spatial, b_spatial,
           w_proj_a, w_proj_b, w_proj_c, w_out, norm2_g, w_router_group, w_router_expert,
           w_expert_gate, w_expert_up, w_expert_down, norm_f_g):
    bsz, s, d = x.shape
    depth = w_in.shape[0]
    wg_all = w_expert_gate.reshape(depth * N_EXPERTS, d, D_EXPERT)
    wu_all = w_expert_up.reshape(depth * N_EXPERTS, d, D_EXPERT)
    wd_all = w_expert_down.reshape(depth * N_EXPERTS, D_EXPERT, d)
    w_in_b = w_in.astype(BF16)
    assert d == D_MODEL and s % (C_PAIRS[-1][1] * BLOCK) == 0
    tabs = _rope_tables(positions)
    for l in range(depth):
        z = _in_proj(x, norm1_g[l], w_in_b, b_in.reshape(depth, 1, D_IN), l, tabs)
        oa = _attn_a(z, attn_sinks[l])
        ob = _sgu(z, sgu_ln_g[l], sgu_ln_b[l], w_spatial[l], b_spatial[l])
        oc = _attn_c(z)
        x2, h2, route, counts, pos = _merge(
            x.reshape(bsz * s, d), oa.reshape(bsz * s, A_Q_W), ob.reshape(bsz * s, B_WIDTH),
            oc.reshape(bsz * s, C_OUT_W), z.reshape(bsz * s, D_IN), w_proj_a[l], w_proj_b[l], w_proj_c[l], w_out[l],
            norm2_g[l], w_router_group[l], w_router_expert[l])
        x = _moe(h2, route, counts, pos, x2, l, wg_all, wu_all, wd_all, norm_f_g, l == depth - 1).reshape(bsz, s, d)
    return x
```

```python
import functools

import jax
import jax.numpy as jnp
from jax import lax
from jax.experimental import pallas as pl
from jax.experimental.pallas import tpu as pltpu
from jax.experimental.pallas import tpu_sc as plsc

F32 = jnp.float32
BF16 = jnp.bfloat16

D_MODEL = 1024
HEAD_DIM = 64
ROT_DIM = HEAD_DIM // 4
ROPE_THETA = 500000.0
BLOCK = 128
EPS = 1e-5
NEG_INF = -1e30

A_Q_HEADS = 16
A_KV_HEADS = 4
A_REP = A_Q_HEADS // A_KV_HEADS
A_WINDOW = 128
A_Q_W = A_Q_HEADS * HEAD_DIM
A_KV_W = A_KV_HEADS * HEAD_DIM

B_GROUPS = 12
B_CH = 64
B_WIDTH = B_GROUPS * B_CH
B_CHUNK = 128

C_PAIRS = ((128, 1), (512, 4), (2048, 16))
C_HEADS_PER_GROUP = 4
C_HEADS = C_HEADS_PER_GROUP * len(C_PAIRS)
C_OUT_W = C_HEADS_PER_GROUP * HEAD_DIM

N_BRANCH = 3
GATE_W = N_BRANCH * D_MODEL
D_IN = A_Q_W + 2 * A_KV_W + 2 * B_WIDTH + 3 * C_HEADS * HEAD_DIM + GATE_W

N_EXPERT_GROUPS = 4
EXPERTS_PER_GROUP = 8
N_EXPERTS = N_EXPERT_GROUPS * EXPERTS_PER_GROUP
TOP_K = 2
D_EXPERT = 256

LANES = 128
MXU_W = 256
VMEM_LIMIT = 56 * 1024 * 1024

IN_TN = 256
COL_AQ = 0
COL_AK = A_Q_W // IN_TN
COL_AV = COL_AK + A_KV_W // IN_TN
COL_BU = COL_AV + A_KV_W // IN_TN
COL_BV = COL_BU + B_WIDTH // IN_TN
COL_CQ = COL_BV + B_WIDTH // IN_TN
COL_CK = COL_CQ + C_HEADS * HEAD_DIM // IN_TN
COL_CV = COL_CK + C_HEADS * HEAD_DIM // IN_TN
COL_G = COL_CV + C_HEADS * HEAD_DIM // IN_TN
N_COL_TILES = D_IN // IN_TN

ROUTE_W = LANES
ROUTE_E0 = N_EXPERT_GROUPS
ROUTE_RANK0 = 4
SUBLANES = 8


def _cparams(sem):
    return pltpu.CompilerParams(dimension_semantics=sem, vmem_limit_bytes=VMEM_LIMIT)


def _rope_table_kernel(pos_ref, inv_ref, c_ref, s1_ref, s2_ref):
    lane = lax.broadcasted_iota(jnp.int32, (1, LANES), 1)
    d = lane & (HEAD_DIM - 1)
    ang = pos_ref[...].astype(F32) * inv_ref[...]
    c = jnp.cos(ang)
    s = jnp.sin(ang)
    half = ROT_DIM // 2
    c_ref[...] = jnp.where(d < ROT_DIM, c, 1.0)
    s1_ref[...] = jnp.where(d < half, -s, 0.0)
    s2_ref[...] = jnp.where((d >= half) & (d < ROT_DIM), s, 0.0)


def _rope_tables(positions):
    n = positions.size
    inv = ROPE_THETA ** (-jnp.arange(0, ROT_DIM, 2, dtype=F32) / ROT_DIM)
    inv_lane = jnp.tile(inv, LANES // inv.shape[0]).reshape(1, LANES)
    pos_b = jnp.broadcast_to(positions.reshape(n, 1), (n, LANES))
    tm = 1024
    spec = pl.BlockSpec((tm, LANES), lambda i: (i, 0))
    return pl.pallas_call(
        _rope_table_kernel,
        out_shape=[jax.ShapeDtypeStruct((n, LANES), F32)] * 3,
        grid=(n // tm,),
        in_specs=[spec, pl.BlockSpec((1, LANES), lambda i: (0, 0))],
        out_specs=[spec, spec, spec],
        compiler_params=_cparams(("parallel",)),
        name="rope_tables",
    )(pos_b, inv_lane)


IN_STEP_TILES = 3
IN_STEP_W = IN_STEP_TILES * IN_TN
IN_STEPS = N_COL_TILES // IN_STEP_TILES


def _is_rope_tile(tile):
    return tile < COL_AV or COL_CQ <= tile < COL_CV


_STEP_ROPES = tuple(tuple(_is_rope_tile(step * IN_STEP_TILES + u) for u in range(IN_STEP_TILES))
                    for step in range(IN_STEPS))


def _in_proj_kernel(x_ref, g_ref, w_ref, b_ref, c_ref, s1_ref, s2_ref, z_ref, h_ref,
                    acc0_ref, acc1_ref):
    j = pl.program_id(1)

    @pl.when(j == 0)
    def _():
        x = x_ref[0]
        ms = jnp.mean(x * x, -1, keepdims=True)
        h_ref[...] = (x * lax.rsqrt(ms + EPS) * g_ref[...]).astype(BF16)

    prev = j - 1

    def matmul_into(acc_ref):
        acc_ref[...] = jnp.dot(h_ref[...], w_ref[0], preferred_element_type=F32) + b_ref[0]

    def finish(acc_ref, ropes):
        for u, rope in enumerate(ropes):
            cols = slice(u * IN_TN, (u + 1) * IN_TN)
            if not rope:
                z_ref[0, :, cols] = acc_ref[:, cols].astype(BF16)
                continue
            tile = prev * IN_STEP_TILES + u
            is_q = (tile < COL_AK) | ((tile >= COL_CQ) & (tile < COL_CK))
            scale = jnp.where(is_q, HEAD_DIM ** -0.5, 1.0).astype(F32)
            for t in range(u * IN_TN // LANES, (u + 1) * IN_TN // LANES):
                lanes = slice(t * LANES, (t + 1) * LANES)
                a = acc_ref[:, lanes]
                r = (a * c_ref[...] + pltpu.roll(a, LANES - ROT_DIM // 2, 1) * s1_ref[...]
                     + pltpu.roll(a, ROT_DIM // 2, 1) * s2_ref[...])
                z_ref[0, :, lanes] = (r * scale).astype(BF16)

    accs = (acc0_ref, acc1_ref)
    patterns = sorted(set(_STEP_ROPES))
    pattern_id = jnp.int32(0)
    for step, ropes in enumerate(_STEP_ROPES):
        pattern_id = jnp.where(prev == step, patterns.index(ropes), pattern_id)
    inner = (j > 0) & (j < IN_STEPS)
    for parity in range(2):
        for pid, ropes in enumerate(patterns):
            @pl.when(inner & (j % 2 == parity) & (pattern_id == pid))
            def _(parity=parity, ropes=ropes):
                matmul_into(accs[parity])
                finish(accs[1 - parity], ropes)

    @pl.when(j == 0)
    def _():
        matmul_into(accs[0])

    @pl.when(j == IN_STEPS)
    def _():
        finish(accs[(IN_STEPS - 1) % 2], _STEP_ROPES[-1])


def _in_proj(x, g, w_all, b_all, layer, tabs):
    bsz, s, d = x.shape
    c, s1, s2 = tabs
    tab_spec = pl.BlockSpec((s, LANES), lambda bi, j: (bi, 0))
    last = IN_STEPS - 1
    return pl.pallas_call(
        _in_proj_kernel,
        out_shape=jax.ShapeDtypeStruct((bsz, s, D_IN), BF16),
        grid=(bsz, IN_STEPS + 1),
        in_specs=[
            pl.BlockSpec((1, s, d), lambda bi, j: (bi, 0, 0)),
            pl.BlockSpec((1, d), lambda bi, j: (0, 0)),
            pl.BlockSpec((1, d, IN_STEP_W), lambda bi, j: (layer, 0, jnp.minimum(j, last))),
            pl.BlockSpec((1, 1, IN_STEP_W), lambda bi, j: (layer, 0, jnp.minimum(j, last))),
            tab_spec, tab_spec, tab_spec,
        ],
        out_specs=pl.BlockSpec((1, s, IN_STEP_W), lambda bi, j: (bi, 0, jnp.maximum(j - 1, 0))),
        scratch_shapes=[pltpu.VMEM((s, d), BF16), pltpu.VMEM((s, IN_STEP_W), F32),
                        pltpu.VMEM((s, IN_STEP_W), F32)],
        compiler_params=_cparams(("parallel", "arbitrary")),
        name="in_proj",
    )(x, g.reshape(1, d), w_all, b_all, c, s1, s2)


STACK = C_HEADS_PER_GROUP * BLOCK
STEP_UNROLL = 4
A_STEP_UNROLL = 8
FAST_STRIDE = 4


def _head_stack(blk):
    seg = lax.broadcasted_iota(jnp.int32, (1, MXU_W), 1) // HEAD_DIM
    rowseg = lax.broadcasted_iota(jnp.int32, (STACK, 1), 0) // BLOCK
    return jnp.where(seg == rowseg, jnp.concatenate([blk] * C_HEADS_PER_GROUP, axis=0),
                     jnp.zeros((), blk.dtype))


def _band_step(q, kp_ref, vp_ref, blk, n, has_prev, diag_key, sink_col, want_lse):
    return _band_blocks([(q, blk, n)], kp_ref, vp_ref, has_prev, diag_key, sink_col, want_lse)[0]


def _band_blocks(jobs, kp_ref, vp_ref, has_prev, diag_key, sink_col, want_lse):
    contract = (((1,), (1,)), ((), ()))
    i_loc = lax.broadcasted_iota(jnp.int32, (STACK, 1), 0) & (BLOCK - 1)
    jj = lax.broadcasted_iota(jnp.int32, (1, BLOCK), 1)
    upper = jj > i_loc
    seg = lax.broadcasted_iota(jnp.int32, (1, MXU_W), 1) // HEAD_DIM
    zero_b = jnp.zeros((), BF16)

    def key_rows(blk):
        lo = pl.multiple_of(blk * BLOCK, BLOCK)
        if has_prev:
            return pl.ds(lo, 2 * BLOCK)
        return pl.ds(pl.multiple_of(lo + BLOCK, BLOCK), BLOCK)

    scores = [lax.dot_general(_head_stack(q), kp_ref[key_rows(blk), :], contract,
                              preferred_element_type=F32) for q, blk, _ in jobs]

    def own_lanes(x):
        out = x[0:BLOCK]
        for h in range(1, C_HEADS_PER_GROUP):
            out = jnp.where(seg == h, x[h * BLOCK:(h + 1) * BLOCK], out)
        return out

    probs, stats = [], []
    for s, (_, blk, n) in zip(scores, jobs):
        if has_prev:
            sp = s[:, :BLOCK] + jnp.where(n == 0, NEG_INF, 0.0).astype(F32)
            f = jnp.where(upper, sp, s[:, BLOCK:])
        else:
            f = jnp.where(upper, NEG_INF, s)
        m = jnp.max(f, -1, keepdims=True)
        if diag_key:
            sd = jnp.sum(jnp.where(jj == i_loc, sp, 0.0), -1, keepdims=True)
            m = jnp.maximum(m, sd)
        if sink_col is not None:
            m = jnp.maximum(m, sink_col)
        p = jnp.exp(f - m)
        den = jnp.sum(p, -1, keepdims=True)
        pd = None
        if diag_key:
            pd = jnp.exp(sd - m)
            den = den + pd
        if sink_col is not None:
            den = den + jnp.exp(sink_col - m)
        inv = 1.0 / den
        pn = (p * inv).astype(BF16)
        if has_prev:
            below = zero_b
            if diag_key:
                below = jnp.where(jj == i_loc, pd * inv, 0.0).astype(BF16)
            pn = jnp.concatenate([jnp.where(upper, pn, below), jnp.where(upper, zero_b, pn)],
                                 axis=1)
        probs.append(pn)
        stats.append((m, den))

    outs = []
    for pn, (m, den), (_, blk, _) in zip(probs, stats, jobs):
        o = own_lanes(jnp.dot(pn, vp_ref[key_rows(blk), :], preferred_element_type=F32))
        lse = own_lanes(m + jnp.log(den)) + jnp.zeros((BLOCK, MXU_W), F32) if want_lse else None
        outs.append((o, lse))
    return outs


def _replicate_head(blk, g):
    words = pltpu.bitcast(blk, jnp.uint32)
    tile = words[:, (g // 2) * LANES:(g // 2 + 1) * LANES]
    swapped = pltpu.roll(tile, HEAD_DIM, 1)
    low = lax.broadcasted_iota(jnp.int32, (1, LANES), 1) < HEAD_DIM
    both = jnp.where(low, tile, swapped) if g % 2 == 0 else jnp.where(low, swapped, tile)
    return pltpu.bitcast(jnp.concatenate([both, both], axis=1), BF16)


def _attn_a_kernel(sink_ref, q_ref, k_ref, v_ref, o_ref, kp_ref, vp_ref):
    nb = q_ref.shape[1] // BLOCK
    rowseg = lax.broadcasted_iota(jnp.int32, (STACK, 1), 0) // BLOCK
    kp_ref[pl.ds(0, BLOCK), :] = jnp.zeros((BLOCK, MXU_W), BF16)
    vp_ref[pl.ds(0, BLOCK), :] = jnp.zeros((BLOCK, MXU_W), BF16)
    for g in range(A_KV_HEADS):
        q_cols = slice(g * MXU_W, (g + 1) * MXU_W)

        def build(n, carry):
            rows = pl.ds(pl.multiple_of(n * BLOCK, BLOCK), BLOCK)
            dst = pl.ds(pl.multiple_of((n + 1) * BLOCK, BLOCK), BLOCK)
            kp_ref[dst, :] = _replicate_head(k_ref[0, rows, :], g)
            vp_ref[dst, :] = _replicate_head(v_ref[0, rows, :], g)
            return carry

        lax.fori_loop(0, nb, build, 0, unroll=8)
        sink_col = jnp.zeros((STACK, 1), F32)
        for r in range(A_REP):
            sink_col = jnp.where(rowseg == r, sink_ref[g * A_REP + r], sink_col)

        def step(it, carry):
            blocks = [it * A_STEP_UNROLL + u for u in range(A_STEP_UNROLL)]
            rows = [pl.ds(pl.multiple_of(n * BLOCK, BLOCK), BLOCK) for n in blocks]
            jobs = [(q_ref[0, r, q_cols], n, n) for r, n in zip(rows, blocks)]
            outs = _band_blocks(jobs, kp_ref, vp_ref, True, False, sink_col, False)
            for r, (o, _) in zip(rows, outs):
                o_ref[0, r, q_cols] = o.astype(o_ref.dtype)
            return carry

        lax.fori_loop(0, nb // A_STEP_UNROLL, step, 0)


def _attn_a(z, sinks):
    bsz, s, _ = z.shape
    nb = s // BLOCK
    grid_spec = pltpu.PrefetchScalarGridSpec(
        num_scalar_prefetch=1,
        grid=(bsz,),
        in_specs=[
            pl.BlockSpec((1, s, A_Q_W), lambda bi, sk: (bi, 0, 0)),
            pl.BlockSpec((1, s, A_KV_W), lambda bi, sk: (bi, 0, COL_AK)),
            pl.BlockSpec((1, s, A_KV_W), lambda bi, sk: (bi, 0, COL_AV)),
        ],
        out_specs=pl.BlockSpec((1, s, A_Q_W), lambda bi, sk: (bi, 0, 0)),
        scratch_shapes=[pltpu.VMEM(((nb + 1) * BLOCK, MXU_W), BF16),
                        pltpu.VMEM(((nb + 1) * BLOCK, MXU_W), BF16)],
    )
    return pl.pallas_call(
        _attn_a_kernel,
        out_shape=jax.ShapeDtypeStruct((bsz, s, A_Q_W), BF16),
        grid_spec=grid_spec,
        compiler_params=_cparams(("parallel",)),
        name="attn_a",
    )(sinks, z, z, z)


def _attn_c_kernel(q_ref, k_ref, v_ref, oc_ref, stage_ref, stage2_ref, pq_ref, pk_ref, pv_ref,
                   og_ref, lg_ref):
    s = q_ref.shape[1]
    nblk = s // BLOCK
    halves = MXU_W // LANES
    pk_ref[pl.ds(0, BLOCK), :] = jnp.zeros((BLOCK, MXU_W), BF16)
    pv_ref[pl.ds(0, BLOCK), :] = jnp.zeros((BLOCK, MXU_W), BF16)

    def class_rows(idx, nb, d):
        r = idx // nb
        n = idx - r * nb
        start = r + n * (BLOCK * d)
        rows = pl.ds(start, BLOCK, stride=d) if d > 1 else pl.ds(pl.multiple_of(start, BLOCK), BLOCK)
        return n, rows

    quarter = s // FAST_STRIDE

    def stage(src_ref, cols, d):
        x = src_ref[0, :, cols].astype(F32)
        for t in range(halves):
            stage_ref[t] = x[:, t * LANES:(t + 1) * LANES]
        if d > FAST_STRIDE:
            for t in range(halves):
                for r1 in range(FAST_STRIDE):
                    stage2_ref[t, pl.ds(r1 * quarter, quarter), :] = (
                        stage_ref[t, pl.ds(r1, quarter, stride=FAST_STRIDE), :])

    def staged_block(idx, nb, d):
        if d > FAST_STRIDE:
            assert nb == 1 and d == FAST_STRIDE * FAST_STRIDE
            start = (idx % FAST_STRIDE) * quarter + idx // FAST_STRIDE
            rows = pl.ds(start, BLOCK, stride=FAST_STRIDE)
            parts = [stage2_ref[t, rows, :] for t in range(halves)]
        else:
            _, rows = class_rows(idx, nb, d)
            parts = [stage_ref[t, rows, :] for t in range(halves)]
        return jnp.concatenate(parts, axis=1).astype(BF16)

    for g, (win, d) in enumerate(C_PAIRS):
        cols = slice(g * MXU_W, (g + 1) * MXU_W)
        nb = nblk // d
        has_prev = nb > 1
        assert win // d == BLOCK

        def class_major(dst_ref, offset):
            def build(idx, carry):
                dst = pl.ds(pl.multiple_of((idx + offset) * BLOCK, BLOCK), BLOCK)
                dst_ref[dst, :] = staged_block(idx, nb, d)
                return carry
            return build

        for src_ref, dst_ref, offset in ((k_ref, pk_ref, 1), (v_ref, pv_ref, 1), (q_ref, pq_ref, 0)):
            if d == 1:
                dst_ref[pl.ds(offset * BLOCK, s), :] = src_ref[0, :, cols]
                continue
            stage(src_ref, cols, d)
            lax.fori_loop(0, nblk, class_major(dst_ref, offset), 0, unroll=4)

        def step(it, carry):
            jobs, dsts = [], []
            for u in range(STEP_UNROLL):
                idx = it * STEP_UNROLL + u
                n, rows = class_rows(idx, nb, d)
                here = pl.ds(pl.multiple_of(idx * BLOCK, BLOCK), BLOCK)
                jobs.append((pq_ref[here, :], idx, n))
                dsts.append(rows)
            outs = _band_blocks(jobs, pk_ref, pv_ref, has_prev, has_prev, None, True)
            for rows, (o, lse) in zip(dsts, outs):
                for t in range(halves):
                    og_ref[g, t, rows, :] = o[:, t * LANES:(t + 1) * LANES]
                    lg_ref[g, t, rows, :] = lse[:, t * LANES:(t + 1) * LANES]
            return carry

        lax.fori_loop(0, nblk // STEP_UNROLL, step, 0)

    chunk = 2 * BLOCK

    def combine(i, carry):
        rows = pl.ds(pl.multiple_of(i * chunk, chunk), chunk)
        for t in range(halves):
            ls = [lg_ref[g, t, rows, :] for g in range(len(C_PAIRS))]
            m = jnp.maximum(jnp.maximum(ls[0], ls[1]), ls[2])
            es = [jnp.exp(l - m) for l in ls]
            num = (es[0] * og_ref[0, t, rows, :] + es[1] * og_ref[1, t, rows, :]
                   + es[2] * og_ref[2, t, rows, :])
            oc_ref[0, rows, t * LANES:(t + 1) * LANES] = (
                num / (es[0] + es[1] + es[2])).astype(oc_ref.dtype)
        return carry

    lax.fori_loop(0, s // chunk, combine, 0)


def _attn_c(z):
    bsz, s, _ = z.shape
    width = C_HEADS * HEAD_DIM
    halves = MXU_W // LANES

    def in_spec(col):
        return pl.BlockSpec((1, s, width), lambda bi: (bi, 0, col * IN_TN // width))

    return pl.pallas_call(
        _attn_c_kernel,
        out_shape=jax.ShapeDtypeStruct((bsz, s, C_OUT_W), BF16),
        grid=(bsz,),
        in_specs=[in_spec(COL_CQ), in_spec(COL_CK), in_spec(COL_CV)],
        out_specs=pl.BlockSpec((1, s, C_OUT_W), lambda bi: (bi, 0, 0)),
        scratch_shapes=[
            pltpu.VMEM((halves, s, LANES), F32),
            pltpu.VMEM((halves, s, LANES), F32),
            pltpu.VMEM((s, MXU_W), BF16),
            pltpu.VMEM((s + BLOCK, MXU_W), BF16),
            pltpu.VMEM((s + BLOCK, MXU_W), BF16),
            pltpu.VMEM((len(C_PAIRS), halves, s, LANES), F32),
            pltpu.VMEM((len(C_PAIRS), halves, s, LANES), F32),
        ],
        compiler_params=_cparams(("parallel",)),
        name="attn_c",
    )(z, z, z)


def _sgu_kernel(zu_ref, zv_ref, lng_ref, lnb_ref, ws_ref, bs_ref, o_ref):
    tm = zu_ref.shape[1]
    u = jax.nn.gelu(zu_ref[0]).astype(F32)
    v = jax.nn.gelu(zv_ref[0]).astype(F32)
    mu = jnp.mean(v, -1, keepdims=True)
    var = jnp.mean(jnp.square(v - mu), -1, keepdims=True)
    vn = ((v - mu) * lax.rsqrt(var + EPS) * lng_ref[...] + lnb_ref[...]).astype(BF16)
    causal = (lax.broadcasted_iota(jnp.int32, (B_CHUNK, 1), 0)
              >= lax.broadcasted_iota(jnp.int32, (1, B_CHUNK), 1))
    first = lax.broadcasted_iota(jnp.int32, (1, LANES), 1) < B_CH
    zero = jnp.zeros((), BF16)
    for p in range(B_WIDTH // LANES):
        w0 = jnp.where(causal, ws_ref[2 * p], zero)
        w1 = jnp.where(causal, ws_ref[2 * p + 1], zero)
        cols = slice(p * LANES, (p + 1) * LANES)
        for c in range(tm // B_CHUNK):
            rows = slice(c * B_CHUNK, (c + 1) * B_CHUNK)
            vv = vn[rows, cols]
            sv = jnp.where(first,
                           jnp.dot(w0, vv, preferred_element_type=F32),
                           jnp.dot(w1, vv, preferred_element_type=F32)) + bs_ref[:, cols]
            o_ref[0, rows, cols] = (u[rows, cols] * sv).astype(o_ref.dtype)


def _sgu(z, ln_g, ln_b, w_s, b_s):
    bsz, s, _ = z.shape
    tm = 512
    bias = jnp.repeat(b_s.T, B_CH, axis=1)
    col_u = COL_BU * IN_TN // B_WIDTH
    col_v = COL_BV * IN_TN // B_WIDTH
    return pl.pallas_call(
        _sgu_kernel,
        out_shape=jax.ShapeDtypeStruct((bsz, s, B_WIDTH), BF16),
        grid=(bsz, s // tm),
        in_specs=[
            pl.BlockSpec((1, tm, B_WIDTH), lambda bi, i: (bi, i, col_u)),
            pl.BlockSpec((1, tm, B_WIDTH), lambda bi, i: (bi, i, col_v)),
            pl.BlockSpec((1, B_WIDTH), lambda bi, i: (0, 0)),
            pl.BlockSpec((1, B_WIDTH), lambda bi, i: (0, 0)),
            pl.BlockSpec((B_GROUPS, B_CHUNK, B_CHUNK), lambda bi, i: (0, 0, 0)),
            pl.BlockSpec((B_CHUNK, B_WIDTH), lambda bi, i: (0, 0)),
        ],
        out_specs=pl.BlockSpec((1, tm, B_WIDTH), lambda bi, i: (bi, i, 0)),
        compiler_params=_cparams(("parallel", "parallel")),
        name="sgu",
    )(z, z, ln_g.reshape(1, B_WIDTH), ln_b.reshape(1, B_WIDTH), w_s.astype(BF16), bias)


def _route(logits):
    lane = lax.broadcasted_iota(jnp.int32, (1, ROUTE_W), 1)
    lane_f = lane.astype(F32)
    is_g = lane < N_EXPERT_GROUPS
    lg = jnp.where(is_g, logits, NEG_INF)
    mg = jnp.max(lg, -1, keepdims=True)
    pg_top = 1.0 / jnp.sum(jnp.where(is_g, jnp.exp(lg - mg), 0.0), -1, keepdims=True)
    g_idx = jnp.min(jnp.where(lg == mg, lane_f, float(ROUTE_W)), -1, keepdims=True)
    e_group = ((lane - ROUTE_E0) // EXPERTS_PER_GROUP).astype(F32)
    in_grp = (lane >= ROUTE_E0) & (lane < ROUTE_E0 + N_EXPERTS) & (e_group == g_idx)
    le = jnp.where(in_grp, logits, NEG_INF)
    m1 = jnp.max(le, -1, keepdims=True)
    i1 = jnp.min(jnp.where(le == m1, lane_f, float(ROUTE_W)), -1, keepdims=True)
    le2 = jnp.where(lane_f == i1, NEG_INF, le)
    m2 = jnp.max(le2, -1, keepdims=True)
    i2 = jnp.min(jnp.where(le2 == m2, lane_f, float(ROUTE_W)), -1, keepdims=True)
    t = jnp.exp(m2 - m1)
    w1 = pg_top / (1.0 + t)
    w2 = w1 * t
    picks = (i1 - ROUTE_E0, i2 - ROUTE_E0, w1, w2)
    out = jnp.zeros(logits.shape, F32)
    for k, val in enumerate(picks):
        out = jnp.where(lane == k, val, out)
    return out


def _merge_kernel(x_ref, oa_ref, ob_ref, oc_ref,
                  g0_ref, g1_ref, g2_ref, g3_ref, wa_ref, wb_ref, wc_ref, wo_ref, n2_ref, wr_ref,
                  xo_ref, h2_ref, route_ref, counts_ref, pos_ref, xn0_ref, xn1_ref, cnt_ref):
    i = pl.program_id(0)
    n_tiles = pl.num_programs(0) - 1

    def residual_into(xn_ref):
        zg = jnp.concatenate([g0_ref[...], g1_ref[...], g2_ref[...], g3_ref[...]],
                             axis=1).astype(F32)
        gates = jax.nn.sigmoid(zg)
        pa = jnp.dot(oa_ref[...], wa_ref[...], preferred_element_type=F32)
        pb = jnp.dot(ob_ref[...], wb_ref[...], preferred_element_type=F32)
        pc = jnp.dot(oc_ref[...], wc_ref[...], preferred_element_type=F32)
        merged = (gates[:, :D_MODEL] * pa + gates[:, D_MODEL:2 * D_MODEL] * pb
                  + gates[:, 2 * D_MODEL:] * pc)
        xn_ref[...] = x_ref[...] + jnp.dot(merged.astype(BF16), wo_ref[...],
                                           preferred_element_type=F32)

    def tail_from(xn_ref):
        xn = xn_ref[...]
        xo_ref[...] = xn
        ms = jnp.mean(xn * xn, -1, keepdims=True)
        h2 = xn * lax.rsqrt(ms + EPS) * n2_ref[...]
        h2_ref[...] = _pack_bf16_pairs(h2)
        route = _route(jnp.dot(h2.astype(BF16), wr_ref[...], preferred_element_type=F32))
        tm = route.shape[0]
        lane = lax.broadcasted_iota(jnp.int32, (1, ROUTE_W), 1)
        lane_f = lane.astype(F32)
        row = lax.broadcasted_iota(jnp.int32, (tm, 1), 0)
        hit = [lane_f == route[:, k:k + 1] for k in range(TOP_K)]
        onehot = jnp.where(hit[0], 1.0, 0.0) + jnp.where(hit[1], 1.0, 0.0)
        scan = onehot
        shift = 1
        while shift < tm:
            scan = scan + jnp.where(row >= shift, pltpu.roll(scan, shift, 0), 0.0)
            shift *= 2
        before = scan - onehot + cnt_ref[0:1, :]
        capacity = pos_ref.shape[1] * LANES
        diag = (row & (LANES - 1)) == lane
        first_row = pl.multiple_of((i - 1) * (tm // LANES), tm // LANES)
        for k in range(TOP_K):
            rank = jnp.sum(jnp.where(hit[k], before, 0.0), -1, keepdims=True)
            slot = jnp.where(diag, route[:, k:k + 1] * capacity + rank, 0.0)
            dense = jnp.concatenate(
                [jnp.sum(slot[r * LANES:(r + 1) * LANES], axis=0, keepdims=True)
                 for r in range(tm // LANES)], axis=0)
            pos_ref[k, pl.ds(first_row, tm // LANES), :] = dense.astype(jnp.int32)
        route_ref[...] = route
        cnt = cnt_ref[0:1, :] + jnp.sum(onehot, axis=0, keepdims=True)
        cnt_ref[...] = jnp.broadcast_to(cnt, cnt_ref.shape)
        counts_ref[...] = jnp.broadcast_to(cnt, counts_ref.shape)

    bufs = (xn0_ref, xn1_ref)
    for parity in range(2):
        @pl.when((i > 0) & (i < n_tiles) & (i % 2 == parity))
        def _(parity=parity):
            tail_from(bufs[1 - parity])
            residual_into(bufs[parity])

        @pl.when((i == n_tiles) & (i % 2 == parity))
        def _(parity=parity):
            tail_from(bufs[1 - parity])

    @pl.when(i == 0)
    def _():
        cnt_ref[...] = jnp.zeros_like(cnt_ref)
        residual_into(bufs[0])


def _merge(x2, oa, ob, oc, z2, wa, wb, wc, wo, n2g, w_rg, w_re):
    n, d = x2.shape
    tm = 512
    wr = jnp.concatenate(
        [w_rg, w_re, jnp.zeros((d, ROUTE_W - N_EXPERT_GROUPS - N_EXPERTS), w_rg.dtype)], axis=1)
    gate_w = GATE_W // 4
    gate_col0 = COL_G * IN_TN // gate_w

    n_tiles = n // tm
    last = n_tiles - 1

    def rows(width):
        return pl.BlockSpec((tm, width), lambda i: (jnp.minimum(i, last), 0))

    def lagged_rows(width):
        return pl.BlockSpec((tm, width), lambda i: (jnp.maximum(i - 1, 0), 0))

    def full(shape):
        return pl.BlockSpec(shape, lambda i: (0, 0))

    gate_specs = [pl.BlockSpec((tm, gate_w), lambda i, k=k: (jnp.minimum(i, last), gate_col0 + k))
                  for k in range(4)]
    return pl.pallas_call(
        _merge_kernel,
        out_shape=[jax.ShapeDtypeStruct((n, d), F32),
                   jax.ShapeDtypeStruct((n, d // 2), jnp.int32),
                   jax.ShapeDtypeStruct((n, ROUTE_W), F32),
                   jax.ShapeDtypeStruct((SUBLANES, ROUTE_W), F32),
                   jax.ShapeDtypeStruct((TOP_K, n // LANES, LANES), jnp.int32)],
        grid=(n_tiles + 1,),
        in_specs=[rows(d), rows(A_Q_W), rows(B_WIDTH), rows(C_OUT_W),
                  *gate_specs,
                  full((A_Q_W, d)), full((B_WIDTH, d)), full((C_OUT_W, d)), full((d, d)),
                  full((1, d)), full((d, ROUTE_W))],
        out_specs=[lagged_rows(d), lagged_rows(d // 2), lagged_rows(ROUTE_W),
                   full((SUBLANES, ROUTE_W)),
                   pl.BlockSpec((TOP_K, n // LANES, LANES), lambda i: (0, 0, 0))],
        scratch_shapes=[pltpu.VMEM((tm, d), F32), pltpu.VMEM((tm, d), F32),
                        pltpu.VMEM((SUBLANES, ROUTE_W), F32)],
        compiler_params=_cparams(("arbitrary",)),
        name="merge",
    )(x2, oa, ob, oc, z2, z2, z2, z2,
      wa.astype(BF16), wb.astype(BF16), wc.astype(BF16), wo.astype(BF16),
      n2g.reshape(1, d), wr.astype(BF16))


MOE_TM = 1024
MOE_SUB = 512
SC_CORES = 2
SC_SUBCORES = 16
SC_WORKERS = SC_CORES * SC_SUBCORES
SC_CHUNK = 128


def _pack_bf16_pairs(x):
    w = x.shape[1] // 2
    lo = pltpu.bitcast(x[:, :w].astype(BF16).astype(F32), jnp.int32)
    hi = pltpu.bitcast(x[:, w:].astype(BF16).astype(F32), jnp.int32)
    return (hi & jnp.int32(-65536)) | lax.shift_right_logical(lo, jnp.int32(16))


def _unpack_bf16_pairs(p):
    lo = pltpu.bitcast(lax.shift_left(p, jnp.int32(16)), F32)
    hi = pltpu.bitcast(p & jnp.int32(-65536), F32)
    return jnp.concatenate([lo, hi], axis=1)


def _sc_gather_rows(table, idx):
    rows, width = idx.shape[0], table.shape[1]
    per_worker = rows // SC_WORKERS
    n_chunks = per_worker // SC_CHUNK
    assert per_worker * SC_WORKERS == rows and n_chunks * SC_CHUNK == per_worker
    mesh = plsc.VectorSubcoreMesh(core_axis_name="c", subcore_axis_name="s",
                                  num_cores=SC_CORES, num_subcores=SC_SUBCORES)

    @functools.partial(
        pl.kernel, mesh=mesh,
        out_type=jax.ShapeDtypeStruct((rows, width), table.dtype),
        scratch_types=[pltpu.VMEM((n_chunks, SC_CHUNK), jnp.int32),
                       pltpu.VMEM((SC_CHUNK, width), table.dtype),
                       pltpu.SemaphoreType.DMA],
        name="sc_gather_rows",
    )
    def gather(table_hbm, idx_hbm, out_hbm, idx_v, rows_v, sem):
        wid = lax.axis_index("s") * SC_CORES + lax.axis_index("c")
        pltpu.sync_copy(idx_hbm.at[wid], idx_v)
        base = wid * per_worker

        @pl.loop(0, n_chunks)
        def _(c):
            pltpu.async_copy(table_hbm.at[idx_v.at[c]], rows_v, sem).wait()
            pltpu.sync_copy(rows_v, out_hbm.at[pl.ds(base + c * SC_CHUNK, SC_CHUNK)])

    return gather(table, idx.reshape(SC_WORKERS, n_chunks, SC_CHUNK))


def _sc_scatter_rows(rows, pos, n_slots):
    n, width = rows.shape
    per_worker = n // SC_WORKERS
    n_chunks = per_worker // SC_CHUNK
    assert per_worker * SC_WORKERS == n and n_chunks * SC_CHUNK == per_worker
    mesh = plsc.VectorSubcoreMesh(core_axis_name="c", subcore_axis_name="s",
                                  num_cores=SC_CORES, num_subcores=SC_SUBCORES)

    @functools.partial(
        pl.kernel, mesh=mesh,
        out_type=jax.ShapeDtypeStruct((n_slots, width), rows.dtype),
        scratch_types=[pltpu.VMEM((TOP_K, n_chunks, SC_CHUNK), jnp.int32),
                       pltpu.VMEM((SC_CHUNK, width), rows.dtype),
                       pltpu.SemaphoreType.DMA],
        name="sc_scatter_rows",
    )
    def scatter(rows_hbm, pos_hbm, out_hbm, pos_v, rows_v, sem):
        wid = lax.axis_index("s") * SC_CORES + lax.axis_index("c")
        for k in range(TOP_K):
            pltpu.sync_copy(pos_hbm.at[k, wid], pos_v.at[k])
        base = wid * per_worker

        @pl.loop(0, n_chunks)
        def _(c):
            pltpu.sync_copy(rows_hbm.at[pl.ds(base + c * SC_CHUNK, SC_CHUNK)], rows_v)
            for k in range(TOP_K):
                pltpu.async_copy(rows_v, out_hbm.at[pos_v.at[k, c]], sem).wait()

    return scatter(rows, pos.reshape(TOP_K, SC_WORKERS, n_chunks, SC_CHUNK))


def _dispatch_plan(counts, capacity, n_steps, expert_base):
    counts = counts[0, :N_EXPERTS].astype(jnp.int32)
    tiles = (counts + MOE_TM - 1) // MOE_TM
    tile_ends = jnp.cumsum(tiles)
    step = jnp.arange(n_steps, dtype=jnp.int32)
    used = step < tile_ends[-1]
    expert = jnp.minimum(jnp.sum((step[:, None] >= tile_ends[None, :]).astype(jnp.int32), axis=1),
                         N_EXPERTS - 1)
    chunk = step - (tile_ends - tiles)[expert]
    rows = jnp.where(used, jnp.clip(counts[expert] - chunk * MOE_TM, 0, MOE_TM), 0)
    first = jnp.concatenate([jnp.ones((1,), jnp.int32),
                             (expert[1:] != expert[:-1]).astype(jnp.int32)])
    block = expert * (capacity // MOE_TM) + chunk
    last_used = jnp.sum(jnp.where(step == tile_ends[-1] - 1, block, 0))
    spare = N_EXPERTS * (capacity // MOE_TM)
    blocks = jnp.stack([jnp.where(used, block, last_used), jnp.where(used, block, spare)])
    return expert + expert_base, rows.astype(jnp.int32), first, blocks.astype(jnp.int32)


def _moe_tile_kernel(te_ref, tr_ref, tf_ref, tb_ref, xs_ref, wg_ref, wu_ref, wd_ref, ys_ref,
                     wg_s, wu_s, wd_s):
    t = pl.program_id(0)

    @pl.when(tf_ref[t] != 0)
    def _():
        wg_s[...] = wg_ref[0].astype(BF16)
        wu_s[...] = wu_ref[0].astype(BF16)
        wd_s[...] = wd_ref[0].astype(BF16)

    for piece in range(MOE_TM // MOE_SUB):
        rows = slice(piece * MOE_SUB, (piece + 1) * MOE_SUB)
        filled = tr_ref[t] - piece * MOE_SUB

        @pl.when(filled > 0)
        def _(rows=rows, filled=filled):
            occupied = lax.broadcasted_iota(jnp.int32, (MOE_SUB, 1), 0) < filled
            x = jnp.where(occupied, _unpack_bf16_pairs(xs_ref[rows, :]), 0.0).astype(BF16)
            hg = jnp.dot(x, wg_s[...], preferred_element_type=F32)
            hu = jnp.dot(x, wu_s[...], preferred_element_type=F32)
            a = (jax.nn.silu(hg) * hu).astype(BF16)
            ys_ref[rows, :] = _pack_bf16_pairs(jnp.dot(a, wd_s[...], preferred_element_type=F32))

        @pl.when(filled <= 0)
        def _(rows=rows):
            ys_ref[rows, :] = jnp.zeros((MOE_SUB, ys_ref.shape[1]), ys_ref.dtype)


def _moe_tiles(xs, tile_expert, tile_rows, tile_first, tile_blocks, w_gate, w_up, w_down):
    n_slots, half = xs.shape
    d = 2 * half
    grid_spec = pltpu.PrefetchScalarGridSpec(
        num_scalar_prefetch=4,
        grid=(tile_expert.shape[0],),
        in_specs=[
            pl.BlockSpec((MOE_TM, half), lambda t, te, tr, tf, tb: (tb[0, t], 0)),
            pl.BlockSpec((1, d, D_EXPERT), lambda t, te, tr, tf, tb: (te[t], 0, 0)),
            pl.BlockSpec((1, d, D_EXPERT), lambda t, te, tr, tf, tb: (te[t], 0, 0)),
            pl.BlockSpec((1, D_EXPERT, d), lambda t, te, tr, tf, tb: (te[t], 0, 0)),
        ],
        out_specs=pl.BlockSpec((MOE_TM, half), lambda t, te, tr, tf, tb: (tb[1, t], 0)),
        scratch_shapes=[pltpu.VMEM((d, D_EXPERT), BF16), pltpu.VMEM((d, D_EXPERT), BF16),
                        pltpu.VMEM((D_EXPERT, d), BF16)],
    )
    return pl.pallas_call(
        _moe_tile_kernel,
        out_shape=jax.ShapeDtypeStruct((n_slots, half), jnp.int32),
        grid_spec=grid_spec,
        compiler_params=_cparams(("arbitrary",)),
        name="moe_tiles",
    )(tile_expert, tile_rows, tile_first, tile_blocks, xs, w_gate, w_up, w_down)


def _moe_combine_kernel(x_ref, y0_ref, y1_ref, r_ref, g_ref, o_ref, *, final_norm):
    w0 = r_ref[:, 2:3]
    w1 = r_ref[:, 3:4]
    x = x_ref[...] + w0 * _unpack_bf16_pairs(y0_ref[0]) + w1 * _unpack_bf16_pairs(y1_ref[0])
    if final_norm:
        ms = jnp.mean(x * x, -1, keepdims=True)
        x = x * lax.rsqrt(ms + EPS) * g_ref[...]
    o_ref[...] = x


def _moe_combine(x2, yg, route, norm_g, final_norm):
    n, d = x2.shape
    tm = 1024
    return pl.pallas_call(
        functools.partial(_moe_combine_kernel, final_norm=final_norm),
        out_shape=jax.ShapeDtypeStruct((n, d), F32),
        grid=(n // tm,),
        in_specs=[
            pl.BlockSpec((tm, d), lambda i: (i, 0)),
            pl.BlockSpec((1, tm, d // 2), lambda i: (0, i, 0)),
            pl.BlockSpec((1, tm, d // 2), lambda i: (1, i, 0)),
            pl.BlockSpec((tm, ROUTE_W), lambda i: (i, 0)),
            pl.BlockSpec((1, d), lambda i: (0, 0)),
        ],
        out_specs=pl.BlockSpec((tm, d), lambda i: (i, 0)),
        compiler_params=_cparams(("parallel",)),
        name="moe_combine",
    )(x2, yg, yg, route, norm_g.reshape(1, d))


def _moe(h2p, route, counts, pos, x2, layer, w_gate, w_up, w_down, norm_f_g, final_norm):
    n, d = x2.shape
    n_slots = N_EXPERTS * n + MOE_TM
    n_steps = TOP_K * n // MOE_TM + N_EXPERTS
    tile_expert, tile_rows, tile_first, tile_blocks = _dispatch_plan(
        counts, n, n_steps, layer * N_EXPERTS)
    xs = _sc_scatter_rows(h2p, pos, n_slots)
    ys = _moe_tiles(xs, tile_expert, tile_rows, tile_first, tile_blocks, w_gate, w_up, w_down)
    yg = _sc_gather_rows(ys, pos.reshape(-1)).reshape(TOP_K, n, d // 2)
    return _moe_combine(x2, yg, route, norm_f_g, final_norm)


def kernel(x, positions, norm1_g, w_in, b_in, attn_sinks, sgu_ln_g, sgu_ln_b, w_spatial, b_spatial,
           w_proj_a, w_proj_b, w_proj_c, w_out, norm2_g, w_router_group, w_router_expert,
           w_expert_gate, w_expert_up, w_expert_down, norm_f_g):
    bsz, s, d = x.shape
    depth = w_in.shape[0]
    wg_all = w_expert_gate.reshape(depth * N_EXPERTS, d, D_EXPERT)
    wu_all = w_expert_up.reshape(depth * N_EXPERTS, d, D_EXPERT)
    wd_all = w_expert_down.reshape(depth * N_EXPERTS, D_EXPERT, d)
    w_in_b = w_in.astype(BF16)
    assert d == D_MODEL and s % (C_PAIRS[-1][1] * BLOCK) == 0
    tabs = _rope_tables(positions)
    for l in range(depth):
        z = _in_proj(x, norm1_g[l], w_in_b, b_in.reshape(depth, 1, D_IN), l, tabs)
        oa = _attn_a(z, attn_sinks[l])
        ob = _sgu(z, sgu_ln_g[l], sgu_ln_b[l], w_spatial[l], b_spatial[l])
        oc = _attn_c(z)
        x2, h2, route, counts, pos = _merge(
            x.reshape(bsz * s, d), oa.reshape(bsz * s, A_Q_W), ob.reshape(bsz * s, B_WIDTH),
            oc.reshape(bsz * s, C_OUT_W), z.reshape(bsz * s, D_IN), w_proj_a[l], w_proj_b[l], w_proj_c[l], w_out[l],
            norm2_g[l], w_router_group[l], w_router_expert[l])
        x = _moe(h2, route, counts, pos, x2, l, wg_all, wu_all, wd_all, norm_f_g, l == depth - 1).reshape(bsz, s, d)
    return x
```

```python
import functools

import jax
import jax.numpy as jnp
from jax import lax
from jax.experimental import pallas as pl
from jax.experimental.pallas import tpu as pltpu
from jax.experimental.pallas import tpu_sc as plsc

F32 = jnp.float32
BF16 = jnp.bfloat16

D_MODEL = 1024
HEAD_DIM = 64
ROT_DIM = HEAD_DIM // 4
ROPE_THETA = 500000.0
BLOCK = 128
EPS = 1e-5
NEG_INF = -1e30

A_Q_HEADS = 16
A_KV_HEADS = 4
A_REP = A_Q_HEADS // A_KV_HEADS
A_WINDOW = 128
A_Q_W = A_Q_HEADS * HEAD_DIM
A_KV_W = A_KV_HEADS * HEAD_DIM

B_GROUPS = 12
B_CH = 64
B_WIDTH = B_GROUPS * B_CH
B_CHUNK = 128

C_PAIRS = ((128, 1), (512, 4), (2048, 16))
C_HEADS_PER_GROUP = 4
C_HEADS = C_HEADS_PER_GROUP * len(C_PAIRS)
C_OUT_W = C_HEADS_PER_GROUP * HEAD_DIM

N_BRANCH = 3
GATE_W = N_BRANCH * D_MODEL
D_IN = A_Q_W + 2 * A_KV_W + 2 * B_WIDTH + 3 * C_HEADS * HEAD_DIM + GATE_W

N_EXPERT_GROUPS = 4
EXPERTS_PER_GROUP = 8
N_EXPERTS = N_EXPERT_GROUPS * EXPERTS_PER_GROUP
TOP_K = 2
D_EXPERT = 256

LANES = 128
MXU_W = 256
VMEM_LIMIT = 56 * 1024 * 1024

IN_TN = 256
COL_AQ = 0
COL_AK = A_Q_W // IN_TN
COL_AV = COL_AK + A_KV_W // IN_TN
COL_BU = COL_AV + A_KV_W // IN_TN
COL_BV = COL_BU + B_WIDTH // IN_TN
COL_CQ = COL_BV + B_WIDTH // IN_TN
COL_CK = COL_CQ + C_HEADS * HEAD_DIM // IN_TN
COL_CV = COL_CK + C_HEADS * HEAD_DIM // IN_TN
COL_G = COL_CV + C_HEADS * HEAD_DIM // IN_TN
N_COL_TILES = D_IN // IN_TN

ROUTE_W = LANES
ROUTE_E0 = N_EXPERT_GROUPS
SUBLANES = 8


def _cparams(sem):
    return pltpu.CompilerParams(dimension_semantics=sem, vmem_limit_bytes=VMEM_LIMIT)


def _rope_table_kernel(pos_ref, inv_ref, c_ref, s1_ref, s2_ref):
    lane = lax.broadcasted_iota(jnp.int32, (1, LANES), 1)
    d = lane & (HEAD_DIM - 1)
    ang = pos_ref[...].astype(F32) * inv_ref[...]
    c = jnp.cos(ang)
    s = jnp.sin(ang)
    half = ROT_DIM // 2
    c_ref[...] = jnp.where(d < ROT_DIM, c, 1.0)
    s1_ref[...] = jnp.where(d < half, -s, 0.0)
    s2_ref[...] = jnp.where((d >= half) & (d < ROT_DIM), s, 0.0)


def _rope_tables(positions):
    n = positions.size
    inv = ROPE_THETA ** (-jnp.arange(0, ROT_DIM, 2, dtype=F32) / ROT_DIM)
    inv_lane = jnp.tile(inv, LANES // inv.shape[0]).reshape(1, LANES)
    pos_b = jnp.broadcast_to(positions.reshape(n, 1), (n, LANES))
    tm = 1024
    spec = pl.BlockSpec((tm, LANES), lambda i: (i, 0))
    return pl.pallas_call(
        _rope_table_kernel,
        out_shape=[jax.ShapeDtypeStruct((n, LANES), F32)] * 3,
        grid=(n // tm,),
        in_specs=[spec, pl.BlockSpec((1, LANES), lambda i: (0, 0))],
        out_specs=[spec, spec, spec],
        compiler_params=_cparams(("parallel",)),
        name="rope_tables",
    )(pos_b, inv_lane)


IN_STEP_TILES = 3
IN_STEP_W = IN_STEP_TILES * IN_TN
IN_STEPS = N_COL_TILES // IN_STEP_TILES


def _is_rope_tile(tile):
    return tile < COL_AV or COL_CQ <= tile < COL_CV


_STEP_ROPES = tuple(tuple(_is_rope_tile(step * IN_STEP_TILES + u) for u in range(IN_STEP_TILES))
                    for step in range(IN_STEPS))


def _in_proj_kernel(x_ref, g_ref, w_ref, b_ref, c_ref, s1_ref, s2_ref, z_ref, h_ref,
                    acc0_ref, acc1_ref):
    j = pl.program_id(1)

    @pl.when(j == 0)
    def _():
        x = x_ref[0]
        ms = jnp.mean(x * x, -1, keepdims=True)
        h_ref[...] = (x * lax.rsqrt(ms + EPS) * g_ref[...]).astype(BF16)

    prev = j - 1

    def matmul_into(acc_ref):
        acc_ref[...] = jnp.dot(h_ref[...], w_ref[0], preferred_element_type=F32) + b_ref[0]

    def finish(acc_ref, ropes):
        for u, rope in enumerate(ropes):
            cols = slice(u * IN_TN, (u + 1) * IN_TN)
            if not rope:
                z_ref[0, :, cols] = acc_ref[:, cols].astype(BF16)
                continue
            tile = prev * IN_STEP_TILES + u
            is_q = (tile < COL_AK) | ((tile >= COL_CQ) & (tile < COL_CK))
            scale = jnp.where(is_q, HEAD_DIM ** -0.5, 1.0).astype(F32)
            for t in range(u * IN_TN // LANES, (u + 1) * IN_TN // LANES):
                lanes = slice(t * LANES, (t + 1) * LANES)
                a = acc_ref[:, lanes]
                r = (a * c_ref[...] + pltpu.roll(a, LANES - ROT_DIM // 2, 1) * s1_ref[...]
                     + pltpu.roll(a, ROT_DIM // 2, 1) * s2_ref[...])
                z_ref[0, :, lanes] = (r * scale).astype(BF16)

    accs = (acc0_ref, acc1_ref)
    patterns = sorted(set(_STEP_ROPES))
    pattern_id = jnp.int32(0)
    for step, ropes in enumerate(_STEP_ROPES):
        pattern_id = jnp.where(prev == step, patterns.index(ropes), pattern_id)
    inner = (j > 0) & (j < IN_STEPS)
    for parity in range(2):
        for pid, ropes in enumerate(patterns):
            @pl.when(inner & (j % 2 == parity) & (pattern_id == pid))
            def _(parity=parity, ropes=ropes):
                matmul_into(accs[parity])
                finish(accs[1 - parity], ropes)

    @pl.when(j == 0)
    def _():
        matmul_into(accs[0])

    @pl.when(j == IN_STEPS)
    def _():
        finish(accs[(IN_STEPS - 1) % 2], _STEP_ROPES[-1])


def _in_proj(x, g, w_all, b_all, layer, tabs):
    bsz, s, d = x.shape
    c, s1, s2 = tabs
    tab_spec = pl.BlockSpec((s, LANES), lambda bi, j: (bi, 0))
    last = IN_STEPS - 1
    return pl.pallas_call(
        _in_proj_kernel,
        out_shape=jax.ShapeDtypeStruct((bsz, s, D_IN), BF16),
        grid=(bsz, IN_STEPS + 1),
        in_specs=[
            pl.BlockSpec((1, s, d), lambda bi, j: (bi, 0, 0)),
            pl.BlockSpec((1, d), lambda bi, j: (0, 0)),
            pl.BlockSpec((1, d, IN_STEP_W), lambda bi, j: (layer, 0, jnp.minimum(j, last))),
            pl.BlockSpec((1, 1, IN_STEP_W), lambda bi, j: (layer, 0, jnp.minimum(j, last))),
            tab_spec, tab_spec, tab_spec,
        ],
        out_specs=pl.BlockSpec((1, s, IN_STEP_W), lambda bi, j: (bi, 0, jnp.maximum(j - 1, 0))),
        scratch_shapes=[pltpu.VMEM((s, d), BF16), pltpu.VMEM((s, IN_STEP_W), F32),
                        pltpu.VMEM((s, IN_STEP_W), F32)],
        compiler_params=_cparams(("parallel", "arbitrary")),
        name="in_proj",
    )(x, g.reshape(1, d), w_all, b_all, c, s1, s2)


STACK = C_HEADS_PER_GROUP * BLOCK
STEP_UNROLL = 4
A_STEP_UNROLL = 8
FAST_STRIDE = 4


def _head_stack(blk):
    seg = lax.broadcasted_iota(jnp.int32, (1, MXU_W), 1) // HEAD_DIM
    rowseg = lax.broadcasted_iota(jnp.int32, (STACK, 1), 0) // BLOCK
    return jnp.where(seg == rowseg, jnp.concatenate([blk] * C_HEADS_PER_GROUP, axis=0),
                     jnp.zeros((), blk.dtype))


def _band_blocks(jobs, kp_ref, vp_ref, has_prev, diag_key, sink_col, want_lse):
    contract = (((1,), (1,)), ((), ()))
    i_loc = lax.broadcasted_iota(jnp.int32, (STACK, 1), 0) & (BLOCK - 1)
    jj = lax.broadcasted_iota(jnp.int32, (1, BLOCK), 1)
    upper = jj > i_loc
    seg = lax.broadcasted_iota(jnp.int32, (1, MXU_W), 1) // HEAD_DIM
    zero_b = jnp.zeros((), BF16)

    def key_rows(blk):
        lo = pl.multiple_of(blk * BLOCK, BLOCK)
        if has_prev:
            return pl.ds(lo, 2 * BLOCK)
        return pl.ds(pl.multiple_of(lo + BLOCK, BLOCK), BLOCK)

    scores = [lax.dot_general(_head_stack(q), kp_ref[key_rows(blk), :], contract,
                              preferred_element_type=F32) for q, blk, _ in jobs]

    def own_lanes(x):
        out = x[0:BLOCK]
        for h in range(1, C_HEADS_PER_GROUP):
            out = jnp.where(seg == h, x[h * BLOCK:(h + 1) * BLOCK], out)
        return out

    probs, stats = [], []
    for s, (_, blk, n) in zip(scores, jobs):
        if has_prev:
            sp = s[:, :BLOCK] + jnp.where(n == 0, NEG_INF, 0.0).astype(F32)
            f = jnp.where(upper, sp, s[:, BLOCK:])
        else:
            f = jnp.where(upper, NEG_INF, s)
        m = jnp.max(f, -1, keepdims=True)
        if diag_key:
            sd = jnp.sum(jnp.where(jj == i_loc, sp, 0.0), -1, keepdims=True)
            m = jnp.maximum(m, sd)
        if sink_col is not None:
            m = jnp.maximum(m, sink_col)
        p = jnp.exp(f - m)
        den = jnp.sum(p, -1, keepdims=True)
        pd = None
        if diag_key:
            pd = jnp.exp(sd - m)
            den = den + pd
        if sink_col is not None:
            den = den + jnp.exp(sink_col - m)
        inv = 1.0 / den
        pn = (p * inv).astype(BF16)
        if has_prev:
            below = zero_b
            if diag_key:
                below = jnp.where(jj == i_loc, pd * inv, 0.0).astype(BF16)
            pn = jnp.concatenate([jnp.where(upper, pn, below), jnp.where(upper, zero_b, pn)],
                                 axis=1)
        probs.append(pn)
        stats.append((m, den))

    outs = []
    for pn, (m, den), (_, blk, _) in zip(probs, stats, jobs):
        o = own_lanes(jnp.dot(pn, vp_ref[key_rows(blk), :], preferred_element_type=F32))
        lse = own_lanes(m + jnp.log(den)) + jnp.zeros((BLOCK, MXU_W), F32) if want_lse else None
        outs.append((o, lse))
    return outs


def _replicate_head(blk, g):
    words = pltpu.bitcast(blk, jnp.uint32)
    tile = words[:, (g // 2) * LANES:(g // 2 + 1) * LANES]
    swapped = pltpu.roll(tile, HEAD_DIM, 1)
    low = lax.broadcasted_iota(jnp.int32, (1, LANES), 1) < HEAD_DIM
    both = jnp.where(low, tile, swapped) if g % 2 == 0 else jnp.where(low, swapped, tile)
    return pltpu.bitcast(jnp.concatenate([both, both], axis=1), BF16)


def _attn_a_kernel(sink_ref, q_ref, k_ref, v_ref, o_ref, kp_ref, vp_ref):
    nb = q_ref.shape[1] // BLOCK
    rowseg = lax.broadcasted_iota(jnp.int32, (STACK, 1), 0) // BLOCK
    kp_ref[pl.ds(0, BLOCK), :] = jnp.zeros((BLOCK, MXU_W), BF16)
    vp_ref[pl.ds(0, BLOCK), :] = jnp.zeros((BLOCK, MXU_W), BF16)
    for g in range(A_KV_HEADS):
        q_cols = slice(g * MXU_W, (g + 1) * MXU_W)

        def build(n, carry):
            rows = pl.ds(pl.multiple_of(n * BLOCK, BLOCK), BLOCK)
            dst = pl.ds(pl.multiple_of((n + 1) * BLOCK, BLOCK), BLOCK)
            kp_ref[dst, :] = _replicate_head(k_ref[0, rows, :], g)
            vp_ref[dst, :] = _replicate_head(v_ref[0, rows, :], g)
            return carry

        lax.fori_loop(0, nb, build, 0, unroll=8)
        sink_col = jnp.zeros((STACK, 1), F32)
        for r in range(A_REP):
            sink_col = jnp.where(rowseg == r, sink_ref[g * A_REP + r], sink_col)

        def step(it, carry):
            blocks = [it * A_STEP_UNROLL + u for u in range(A_STEP_UNROLL)]
            rows = [pl.ds(pl.multiple_of(n * BLOCK, BLOCK), BLOCK) for n in blocks]
            jobs = [(q_ref[0, r, q_cols], n, n) for r, n in zip(rows, blocks)]
            outs = _band_blocks(jobs, kp_ref, vp_ref, True, False, sink_col, False)
            for r, (o, _) in zip(rows, outs):
                o_ref[0, r, q_cols] = o.astype(o_ref.dtype)
            return carry

        lax.fori_loop(0, nb // A_STEP_UNROLL, step, 0)


def _attn_a(z, sinks):
    bsz, s, _ = z.shape
    nb = s // BLOCK
    grid_spec = pltpu.PrefetchScalarGridSpec(
        num_scalar_prefetch=1,
        grid=(bsz,),
        in_specs=[
            pl.BlockSpec((1, s, A_Q_W), lambda bi, sk: (bi, 0, 0)),
            pl.BlockSpec((1, s, A_KV_W), lambda bi, sk: (bi, 0, COL_AK)),
            pl.BlockSpec((1, s, A_KV_W), lambda bi, sk: (bi, 0, COL_AV)),
        ],
        out_specs=pl.BlockSpec((1, s, A_Q_W), lambda bi, sk: (bi, 0, 0)),
        scratch_shapes=[pltpu.VMEM(((nb + 1) * BLOCK, MXU_W), BF16),
                        pltpu.VMEM(((nb + 1) * BLOCK, MXU_W), BF16)],
    )
    return pl.pallas_call(
        _attn_a_kernel,
        out_shape=jax.ShapeDtypeStruct((bsz, s, A_Q_W), BF16),
        grid_spec=grid_spec,
        compiler_params=_cparams(("parallel",)),
        name="attn_a",
    )(sinks, z, z, z)


def _attn_c_kernel(q_ref, k_ref, v_ref, oc_ref, stage_ref, stage2_ref, pq_ref, pk_ref, pv_ref,
                   og_ref, lg_ref):
    s = q_ref.shape[1]
    nblk = s // BLOCK
    halves = MXU_W // LANES
    pk_ref[pl.ds(0, BLOCK), :] = jnp.zeros((BLOCK, MXU_W), BF16)
    pv_ref[pl.ds(0, BLOCK), :] = jnp.zeros((BLOCK, MXU_W), BF16)

    def class_rows(idx, nb, d):
        r = idx // nb
        n = idx - r * nb
        start = r + n * (BLOCK * d)
        rows = pl.ds(start, BLOCK, stride=d) if d > 1 else pl.ds(pl.multiple_of(start, BLOCK), BLOCK)
        return n, rows

    quarter = s // FAST_STRIDE

    def stage(src_ref, cols, d):
        x = src_ref[0, :, cols].astype(F32)
        for t in range(halves):
            stage_ref[t] = x[:, t * LANES:(t + 1) * LANES]
        if d > FAST_STRIDE:
            for t in range(halves):
                for r1 in range(FAST_STRIDE):
                    stage2_ref[t, pl.ds(r1 * quarter, quarter), :] = (
                        stage_ref[t, pl.ds(r1, quarter, stride=FAST_STRIDE), :])

    def staged_block(idx, nb, d):
        if d > FAST_STRIDE:
            assert nb == 1 and d == FAST_STRIDE * FAST_STRIDE
            start = (idx % FAST_STRIDE) * quarter + idx // FAST_STRIDE
            rows = pl.ds(start, BLOCK, stride=FAST_STRIDE)
            parts = [stage2_ref[t, rows, :] for t in range(halves)]
        else:
            _, rows = class_rows(idx, nb, d)
            parts = [stage_ref[t, rows, :] for t in range(halves)]
        return jnp.concatenate(parts, axis=1).astype(BF16)

    for g, (win, d) in enumerate(C_PAIRS):
        cols = slice(g * MXU_W, (g + 1) * MXU_W)
        nb = nblk // d
        has_prev = nb > 1
        assert win // d == BLOCK

        def class_major(dst_ref, offset):
            def build(idx, carry):
                dst = pl.ds(pl.multiple_of((idx + offset) * BLOCK, BLOCK), BLOCK)
                dst_ref[dst, :] = staged_block(idx, nb, d)
                return carry
            return build

        for src_ref, dst_ref, offset in ((k_ref, pk_ref, 1), (v_ref, pv_ref, 1), (q_ref, pq_ref, 0)):
            if d == 1:
                dst_ref[pl.ds(offset * BLOCK, s), :] = src_ref[0, :, cols]
                continue
            stage(src_ref, cols, d)
            lax.fori_loop(0, nblk, class_major(dst_ref, offset), 0, unroll=4)

        def step(it, carry):
            jobs, dsts = [], []
            for u in range(STEP_UNROLL):
                idx = it * STEP_UNROLL + u
                n, rows = class_rows(idx, nb, d)
                here = pl.ds(pl.multiple_of(idx * BLOCK, BLOCK), BLOCK)
                jobs.append((pq_ref[here, :], idx, n))
                dsts.append(rows)
            outs = _band_blocks(jobs, pk_ref, pv_ref, has_prev, has_prev, None, True)
            for rows, (o, lse) in zip(dsts, outs):
                for t in range(halves):
                    og_ref[g, t, rows, :] = o[:, t * LANES:(t + 1) * LANES]
                    lg_ref[g, t, rows, :] = lse[:, t * LANES:(t + 1) * LANES]
            return carry

        lax.fori_loop(0, nblk // STEP_UNROLL, step, 0)

    chunk = 2 * BLOCK

    def combine(i, carry):
        rows = pl.ds(pl.multiple_of(i * chunk, chunk), chunk)
        for t in range(halves):
            ls = [lg_ref[g, t, rows, :] for g in range(len(C_PAIRS))]
            m = jnp.maximum(jnp.maximum(ls[0], ls[1]), ls[2])
            es = [jnp.exp(l - m) for l in ls]
            num = (es[0] * og_ref[0, t, rows, :] + es[1] * og_ref[1, t, rows, :]
                   + es[2] * og_ref[2, t, rows, :])
            oc_ref[0, rows, t * LANES:(t + 1) * LANES] = (
                num / (es[0] + es[1] + es[2])).astype(oc_ref.dtype)
        return carry

    lax.fori_loop(0, s // chunk, combine, 0)


def _attn_c(z):
    bsz, s, _ = z.shape
    width = C_HEADS * HEAD_DIM
    halves = MXU_W // LANES

    def in_spec(col):
        return pl.BlockSpec((1, s, width), lambda bi: (bi, 0, col * IN_TN // width))

    return pl.pallas_call(
        _attn_c_kernel,
        out_shape=jax.ShapeDtypeStruct((bsz, s, C_OUT_W), BF16),
        grid=(bsz,),
        in_specs=[in_spec(COL_CQ), in_spec(COL_CK), in_spec(COL_CV)],
        out_specs=pl.BlockSpec((1, s, C_OUT_W), lambda bi: (bi, 0, 0)),
        scratch_shapes=[
            pltpu.VMEM((halves, s, LANES), F32),
            pltpu.VMEM((halves, s, LANES), F32),
            pltpu.VMEM((s, MXU_W), BF16),
            pltpu.VMEM((s + BLOCK, MXU_W), BF16),
            pltpu.VMEM((s + BLOCK, MXU_W), BF16),
            pltpu.VMEM((len(C_PAIRS), halves, s, LANES), F32),
            pltpu.VMEM((len(C_PAIRS), halves, s, LANES), F32),
        ],
        compiler_params=_cparams(("parallel",)),
        name="attn_c",
    )(z, z, z)


def _sgu_kernel(zu_ref, zv_ref, lng_ref, lnb_ref, ws_ref, bs_ref, o_ref):
    tm = zu_ref.shape[1]
    u = jax.nn.gelu(zu_ref[0]).astype(F32)
    v = jax.nn.gelu(zv_ref[0]).astype(F32)
    mu = jnp.mean(v, -1, keepdims=True)
    var = jnp.mean(jnp.square(v - mu), -1, keepdims=True)
    vn = ((v - mu) * lax.rsqrt(var + EPS) * lng_ref[...] + lnb_ref[...]).astype(BF16)
    causal = (lax.broadcasted_iota(jnp.int32, (B_CHUNK, 1), 0)
              >= lax.broadcasted_iota(jnp.int32, (1, B_CHUNK), 1))
    first = lax.broadcasted_iota(jnp.int32, (1, LANES), 1) < B_CH
    zero = jnp.zeros((), BF16)
    for p in range(B_WIDTH // LANES):
        w0 = jnp.where(causal, ws_ref[2 * p], zero)
        w1 = jnp.where(causal, ws_ref[2 * p + 1], zero)
        cols = slice(p * LANES, (p + 1) * LANES)
        for c in range(tm // B_CHUNK):
            rows = slice(c * B_CHUNK, (c + 1) * B_CHUNK)
            vv = vn[rows, cols]
            sv = jnp.where(first,
                           jnp.dot(w0, vv, preferred_element_type=F32),
                           jnp.dot(w1, vv, preferred_element_type=F32)) + bs_ref[:, cols]
            o_ref[0, rows, cols] = (u[rows, cols] * sv).astype(o_ref.dtype)


def _sgu(z, ln_g, ln_b, w_s, b_s):
    bsz, s, _ = z.shape
    tm = 512
    bias = jnp.repeat(b_s.T, B_CH, axis=1)
    col_u = COL_BU * IN_TN // B_WIDTH
    col_v = COL_BV * IN_TN // B_WIDTH
    return pl.pallas_call(
        _sgu_kernel,
        out_shape=jax.ShapeDtypeStruct((bsz, s, B_WIDTH), BF16),
        grid=(bsz, s // tm),
        in_specs=[
            pl.BlockSpec((1, tm, B_WIDTH), lambda bi, i: (bi, i, col_u)),
            pl.BlockSpec((1, tm, B_WIDTH), lambda bi, i: (bi, i, col_v)),
            pl.BlockSpec((1, B_WIDTH), lambda bi, i: (0, 0)),
            pl.BlockSpec((1, B_WIDTH), lambda bi, i: (0, 0)),
            pl.BlockSpec((B_GROUPS, B_CHUNK, B_CHUNK), lambda bi, i: (0, 0, 0)),
            pl.BlockSpec((B_CHUNK, B_WIDTH), lambda bi, i: (0, 0)),
        ],
        out_specs=pl.BlockSpec((1, tm, B_WIDTH), lambda bi, i: (bi, i, 0)),
        compiler_params=_cparams(("parallel", "parallel")),
        name="sgu",
    )(z, z, ln_g.reshape(1, B_WIDTH), ln_b.reshape(1, B_WIDTH), w_s.astype(BF16), bias)


def _route(logits):
    lane = lax.broadcasted_iota(jnp.int32, (1, ROUTE_W), 1)
    lane_f = lane.astype(F32)
    is_g = lane < N_EXPERT_GROUPS
    lg = jnp.where(is_g, logits, NEG_INF)
    mg = jnp.max(lg, -1, keepdims=True)
    pg_top = 1.0 / jnp.sum(jnp.where(is_g, jnp.exp(lg - mg), 0.0), -1, keepdims=True)
    g_idx = jnp.min(jnp.where(lg == mg, lane_f, float(ROUTE_W)), -1, keepdims=True)
    e_group = ((lane - ROUTE_E0) // EXPERTS_PER_GROUP).astype(F32)
    in_grp = (lane >= ROUTE_E0) & (lane < ROUTE_E0 + N_EXPERTS) & (e_group == g_idx)
    le = jnp.where(in_grp, logits, NEG_INF)
    m1 = jnp.max(le, -1, keepdims=True)
    i1 = jnp.min(jnp.where(le == m1, lane_f, float(ROUTE_W)), -1, keepdims=True)
    le2 = jnp.where(lane_f == i1, NEG_INF, le)
    m2 = jnp.max(le2, -1, keepdims=True)
    i2 = jnp.min(jnp.where(le2 == m2, lane_f, float(ROUTE_W)), -1, keepdims=True)
    t = jnp.exp(m2 - m1)
    w1 = pg_top / (1.0 + t)
    w2 = w1 * t
    picks = (i1 - ROUTE_E0, i2 - ROUTE_E0, w1, w2)
    out = jnp.zeros(logits.shape, F32)
    for k, val in enumerate(picks):
        out = jnp.where(lane == k, val, out)
    return out


def _merge_kernel(x_ref, oa_ref, ob_ref, oc_ref,
                  g0_ref, g1_ref, g2_ref, g3_ref, wa_ref, wb_ref, wc_ref, wo_ref, n2_ref, wr_ref,
                  xo_ref, h2_ref, route_ref, counts_ref, pos_ref, xn0_ref, xn1_ref, cnt_ref):
    i = pl.program_id(0)
    n_tiles = pl.num_programs(0) - 1

    def residual_into(xn_ref):
        zg = jnp.concatenate([g0_ref[...], g1_ref[...], g2_ref[...], g3_ref[...]],
                             axis=1).astype(F32)
        gates = jax.nn.sigmoid(zg)
        pa = jnp.dot(oa_ref[...], wa_ref[...], preferred_element_type=F32)
        pb = jnp.dot(ob_ref[...], wb_ref[...], preferred_element_type=F32)
        pc = jnp.dot(oc_ref[...], wc_ref[...], preferred_element_type=F32)
        merged = (gates[:, :D_MODEL] * pa + gates[:, D_MODEL:2 * D_MODEL] * pb
                  + gates[:, 2 * D_MODEL:] * pc)
        xn_ref[...] = x_ref[...] + jnp.dot(merged.astype(BF16), wo_ref[...],
                                           preferred_element_type=F32)

    def tail_from(xn_ref):
        xn = xn_ref[...]
        xo_ref[...] = xn
        ms = jnp.mean(xn * xn, -1, keepdims=True)
        h2 = xn * lax.rsqrt(ms + EPS) * n2_ref[...]
        h2_ref[...] = _pack_bf16_pairs(h2)
        route = _route(jnp.dot(h2.astype(BF16), wr_ref[...], preferred_element_type=F32))
        tm = route.shape[0]
        lane = lax.broadcasted_iota(jnp.int32, (1, ROUTE_W), 1)
        lane_f = lane.astype(F32)
        row = lax.broadcasted_iota(jnp.int32, (tm, 1), 0)
        hit = [lane_f == route[:, k:k + 1] for k in range(TOP_K)]
        onehot = jnp.where(hit[0], 1.0, 0.0) + jnp.where(hit[1], 1.0, 0.0)
        scan = onehot
        shift = 1
        while shift < tm:
            scan = scan + jnp.where(row >= shift, pltpu.roll(scan, shift, 0), 0.0)
            shift *= 2
        before = scan - onehot + cnt_ref[0:1, :]
        capacity = pos_ref.shape[1] * LANES
        diag = (row & (LANES - 1)) == lane
        first_row = pl.multiple_of((i - 1) * (tm // LANES), tm // LANES)
        for k in range(TOP_K):
            rank = jnp.sum(jnp.where(hit[k], before, 0.0), -1, keepdims=True)
            slot = jnp.where(diag, route[:, k:k + 1] * capacity + rank, 0.0)
            dense = jnp.concatenate(
                [jnp.sum(slot[r * LANES:(r + 1) * LANES], axis=0, keepdims=True)
                 for r in range(tm // LANES)], axis=0)
            pos_ref[k, pl.ds(first_row, tm // LANES), :] = dense.astype(jnp.int32)
        route_ref[...] = route
        cnt = cnt_ref[0:1, :] + jnp.sum(onehot, axis=0, keepdims=True)
        cnt_ref[...] = jnp.broadcast_to(cnt, cnt_ref.shape)
        counts_ref[...] = jnp.broadcast_to(cnt, counts_ref.shape)

    bufs = (xn0_ref, xn1_ref)
    for parity in range(2):
        @pl.when((i > 0) & (i < n_tiles) & (i % 2 == parity))
        def _(parity=parity):
            tail_from(bufs[1 - parity])
            residual_into(bufs[parity])

        @pl.when((i == n_tiles) & (i % 2 == parity))
        def _(parity=parity):
            tail_from(bufs[1 - parity])

    @pl.when(i == 0)
    def _():
        cnt_ref[...] = jnp.zeros_like(cnt_ref)
        residual_into(bufs[0])


def _merge(x2, oa, ob, oc, z2, wa, wb, wc, wo, n2g, w_rg, w_re):
    n, d = x2.shape
    tm = 512
    wr = jnp.concatenate(
        [w_rg, w_re, jnp.zeros((d, ROUTE_W - N_EXPERT_GROUPS - N_EXPERTS), w_rg.dtype)], axis=1)
    gate_w = GATE_W // 4
    gate_col0 = COL_G * IN_TN // gate_w

    n_tiles = n // tm
    last = n_tiles - 1

    def rows(width):
        return pl.BlockSpec((tm, width), lambda i: (jnp.minimum(i, last), 0))

    def lagged_rows(width):
        return pl.BlockSpec((tm, width), lambda i: (jnp.maximum(i - 1, 0), 0))

    def full(shape):
        return pl.BlockSpec(shape, lambda i: (0, 0))

    gate_specs = [pl.BlockSpec((tm, gate_w), lambda i, k=k: (jnp.minimum(i, last), gate_col0 + k))
                  for k in range(4)]
    return pl.pallas_call(
        _merge_kernel,
        out_shape=[jax.ShapeDtypeStruct((n, d), F32),
                   jax.ShapeDtypeStruct((n, d // 2), jnp.int32),
                   jax.ShapeDtypeStruct((n, ROUTE_W), F32),
                   jax.ShapeDtypeStruct((SUBLANES, ROUTE_W), F32),
                   jax.ShapeDtypeStruct((TOP_K, n // LANES, LANES), jnp.int32)],
        grid=(n_tiles + 1,),
        in_specs=[rows(d), rows(A_Q_W), rows(B_WIDTH), rows(C_OUT_W),
                  *gate_specs,
                  full((A_Q_W, d)), full((B_WIDTH, d)), full((C_OUT_W, d)), full((d, d)),
                  full((1, d)), full((d, ROUTE_W))],
        out_specs=[lagged_rows(d), lagged_rows(d // 2), lagged_rows(ROUTE_W),
                   full((SUBLANES, ROUTE_W)),
                   pl.BlockSpec((TOP_K, n // LANES, LANES), lambda i: (0, 0, 0))],
        scratch_shapes=[pltpu.VMEM((tm, d), F32), pltpu.VMEM((tm, d), F32),
                        pltpu.VMEM((SUBLANES, ROUTE_W), F32)],
        compiler_params=_cparams(("arbitrary",)),
        name="merge",
    )(x2, oa, ob, oc, z2, z2, z2, z2,
      wa.astype(BF16), wb.astype(BF16), wc.astype(BF16), wo.astype(BF16),
      n2g.reshape(1, d), wr.astype(BF16))


MOE_TM = 1024
SC_CORES = 2
SC_SUBCORES = 16
SC_WORKERS = SC_CORES * SC_SUBCORES
SC_CHUNK = 128


def _pack_bf16_pairs(x):
    w = x.shape[1] // 2
    lo = pltpu.bitcast(x[:, :w].astype(BF16).astype(F32), jnp.int32)
    hi = pltpu.bitcast(x[:, w:].astype(BF16).astype(F32), jnp.int32)
    return (hi & jnp.int32(-65536)) | lax.shift_right_logical(lo, jnp.int32(16))


def _unpack_bf16_pairs(p):
    lo = pltpu.bitcast(lax.shift_left(p, jnp.int32(16)), F32)
    hi = pltpu.bitcast(p & jnp.int32(-65536), F32)
    return jnp.concatenate([lo, hi], axis=1)


def _sc_gather_rows(table, idx):
    rows, width = idx.shape[0], table.shape[1]
    per_worker = rows // SC_WORKERS
    n_chunks = per_worker // SC_CHUNK
    assert per_worker * SC_WORKERS == rows and n_chunks * SC_CHUNK == per_worker
    mesh = plsc.VectorSubcoreMesh(core_axis_name="c", subcore_axis_name="s",
                                  num_cores=SC_CORES, num_subcores=SC_SUBCORES)

    @functools.partial(
        pl.kernel, mesh=mesh,
        out_type=jax.ShapeDtypeStruct((rows, width), table.dtype),
        scratch_types=[pltpu.VMEM((n_chunks, SC_CHUNK), jnp.int32),
                       pltpu.VMEM((SC_CHUNK, width), table.dtype),
                       pltpu.SemaphoreType.DMA],
        name="sc_gather_rows",
    )
    def gather(table_hbm, idx_hbm, out_hbm, idx_v, rows_v, sem):
        wid = lax.axis_index("s") * SC_CORES + lax.axis_index("c")
        pltpu.sync_copy(idx_hbm.at[wid], idx_v)
        base = wid * per_worker

        @pl.loop(0, n_chunks)
        def _(c):
            pltpu.async_copy(table_hbm.at[idx_v.at[c]], rows_v, sem).wait()
            pltpu.sync_copy(rows_v, out_hbm.at[pl.ds(base + c * SC_CHUNK, SC_CHUNK)])

    return gather(table, idx.reshape(SC_WORKERS, n_chunks, SC_CHUNK))


def _sc_scatter_rows(rows, pos, n_slots):
    n, width = rows.shape
    per_worker = n // SC_WORKERS
    n_chunks = per_worker // SC_CHUNK
    assert per_worker * SC_WORKERS == n and n_chunks * SC_CHUNK == per_worker
    mesh = plsc.VectorSubcoreMesh(core_axis_name="c", subcore_axis_name="s",
                                  num_cores=SC_CORES, num_subcores=SC_SUBCORES)

    @functools.partial(
        pl.kernel, mesh=mesh,
        out_type=jax.ShapeDtypeStruct((n_slots, width), rows.dtype),
        scratch_types=[pltpu.VMEM((TOP_K, n_chunks, SC_CHUNK), jnp.int32),
                       pltpu.VMEM((SC_CHUNK, width), rows.dtype),
                       pltpu.SemaphoreType.DMA],
        name="sc_scatter_rows",
    )
    def scatter(rows_hbm, pos_hbm, out_hbm, pos_v, rows_v, sem):
        wid = lax.axis_index("s") * SC_CORES + lax.axis_index("c")
        for k in range(TOP_K):
            pltpu.sync_copy(pos_hbm.at[k, wid], pos_v.at[k])
        base = wid * per_worker

        @pl.loop(0, n_chunks)
        def _(c):
            pltpu.sync_copy(rows_hbm.at[pl.ds(base + c * SC_CHUNK, SC_CHUNK)], rows_v)
            for k in range(TOP_K):
                pltpu.async_copy(rows_v, out_hbm.at[pos_v.at[k, c]], sem).wait()

    return scatter(rows, pos.reshape(TOP_K, SC_WORKERS, n_chunks, SC_CHUNK))


def _dispatch_plan(counts, capacity, n_steps, expert_base):
    counts = counts[0, :N_EXPERTS].astype(jnp.int32)
    tiles = (counts + MOE_TM - 1) // MOE_TM
    tile_ends = jnp.cumsum(tiles)
    step = jnp.arange(n_steps, dtype=jnp.int32)
    used = step < tile_ends[-1]
    expert = jnp.minimum(jnp.sum((step[:, None] >= tile_ends[None, :]).astype(jnp.int32), axis=1),
                         N_EXPERTS - 1)
    chunk = step - (tile_ends - tiles)[expert]
    rows = jnp.where(used, jnp.clip(counts[expert] - chunk * MOE_TM, 0, MOE_TM), 0)
    first = jnp.concatenate([jnp.ones((1,), jnp.int32),
                             (expert[1:] != expert[:-1]).astype(jnp.int32)])
    block = expert * (capacity // MOE_TM) + chunk
    last_used = jnp.sum(jnp.where(step == tile_ends[-1] - 1, block, 0))
    spare = N_EXPERTS * (capacity // MOE_TM)
    blocks = jnp.stack([jnp.where(used, block, last_used), jnp.where(used, block, spare)])
    return expert + expert_base, rows.astype(jnp.int32), first, blocks.astype(jnp.int32)


def _moe_tile_kernel(te_ref, tr_ref, tf_ref, tb_ref, xs_ref, wg_ref, wu_ref, wd_ref, ys_ref,
                     wg_s, wu_s, wd_s):
    t = pl.program_id(0)

    @pl.when(tf_ref[t] != 0)
    def _():
        wg_s[...] = wg_ref[0].astype(BF16)
        wu_s[...] = wu_ref[0].astype(BF16)
        wd_s[...] = wd_ref[0].astype(BF16)

    @pl.when(tr_ref[t] != 0)
    def _():
        occupied = lax.broadcasted_iota(jnp.int32, (MOE_TM, 1), 0) < tr_ref[t]
        x = jnp.where(occupied, _unpack_bf16_pairs(xs_ref[...]), 0.0).astype(BF16)
        hg = jnp.dot(x, wg_s[...], preferred_element_type=F32)
        hu = jnp.dot(x, wu_s[...], preferred_element_type=F32)
        a = (jax.nn.silu(hg) * hu).astype(BF16)
        ys_ref[...] = _pack_bf16_pairs(jnp.dot(a, wd_s[...], preferred_element_type=F32))

    @pl.when(tr_ref[t] == 0)
    def _():
        ys_ref[...] = jnp.zeros_like(ys_ref)


def _moe_tiles(xs, tile_expert, tile_rows, tile_first, tile_blocks, w_gate, w_up, w_down):
    n_slots, half = xs.shape
    d = 2 * half
    grid_spec = pltpu.PrefetchScalarGridSpec(
        num_scalar_prefetch=4,
        grid=(tile_expert.shape[0],),
        in_specs=[
            pl.BlockSpec((MOE_TM, half), lambda t, te, tr, tf, tb: (tb[0, t], 0)),
            pl.BlockSpec((1, d, D_EXPERT), lambda t, te, tr, tf, tb: (te[t], 0, 0)),
            pl.BlockSpec((1, d, D_EXPERT), lambda t, te, tr, tf, tb: (te[t], 0, 0)),
            pl.BlockSpec((1, D_EXPERT, d), lambda t, te, tr, tf, tb: (te[t], 0, 0)),
        ],
        out_specs=pl.BlockSpec((MOE_TM, half), lambda t, te, tr, tf, tb: (tb[1, t], 0)),
        scratch_shapes=[pltpu.VMEM((d, D_EXPERT), BF16), pltpu.VMEM((d, D_EXPERT), BF16),
                        pltpu.VMEM((D_EXPERT, d), BF16)],
    )
    return pl.pallas_call(
        _moe_tile_kernel,
        out_shape=jax.ShapeDtypeStruct((n_slots, half), jnp.int32),
        grid_spec=grid_spec,
        compiler_params=_cparams(("arbitrary",)),
        name="moe_tiles",
    )(tile_expert, tile_rows, tile_first, tile_blocks, xs, w_gate, w_up, w_down)


def _moe_combine_kernel(x_ref, y0_ref, y1_ref, r_ref, g_ref, o_ref, *, final_norm):
    w0 = r_ref[:, 2:3]
    w1 = r_ref[:, 3:4]
    x = x_ref[...] + w0 * _unpack_bf16_pairs(y0_ref[0]) + w1 * _unpack_bf16_pairs(y1_ref[0])
    if final_norm:
        ms = jnp.mean(x * x, -1, keepdims=True)
        x = x * lax.rsqrt(ms + EPS) * g_ref[...]
    o_ref[...] = x


def _moe_combine(x2, yg, route, norm_g, final_norm):
    n, d = x2.shape
    tm = 1024
    return pl.pallas_call(
        functools.partial(_moe_combine_kernel, final_norm=final_norm),
        out_shape=jax.ShapeDtypeStruct((n, d), F32),
        grid=(n // tm,),
        in_specs=[
            pl.BlockSpec((tm, d), lambda i: (i, 0)),
            pl.BlockSpec((1, tm, d // 2), lambda i: (0, i, 0)),
            pl.BlockSpec((1, tm, d // 2), lambda i: (1, i, 0)),
            pl.BlockSpec((tm, ROUTE_W), lambda i: (i, 0)),
            pl.BlockSpec((1, d), lambda i: (0, 0)),
        ],
        out_specs=pl.BlockSpec((tm, d), lambda i: (i, 0)),
        compiler_params=_cparams(("parallel",)),
        name="moe_combine",
    )(x2, yg, yg, route, norm_g.reshape(1, d))


def _moe(h2p, route, counts, pos, x2, layer, w_gate, w_up, w_down, norm_f_g, final_norm):
    n, d = x2.shape
    n_slots = N_EXPERTS * n + MOE_TM
    n_steps = TOP_K * n // MOE_TM + N_EXPERTS
    tile_expert, tile_rows, tile_first, tile_blocks = _dispatch_plan(
        counts, n, n_steps, layer * N_EXPERTS)
    xs = _sc_scatter_rows(h2p, pos, n_slots)
    ys = _moe_tiles(xs, tile_expert, tile_rows, tile_first, tile_blocks, w_gate, w_up, w_down)
    yg = _sc_gather_rows(ys, pos.reshape(-1)).reshape(TOP_K, n, d // 2)
    return _moe_combine(x2, yg, route, norm_f_g, final_norm)


def kernel(x, positions, norm1_g, w_in, b_in, attn_sinks, sgu_ln_g, sgu_ln_b, w_spatial, b_spatial,
           w_proj_a, w_proj_b, w_proj_c, w_out, norm2_g, w_router_group, w_router_expert,
           w_expert_gate, w_expert_up, w_expert_down, norm_f_g):
    bsz, s, d = x.shape
    depth = w_in.shape[0]
    wg_all = w_expert_gate.reshape(depth * N_EXPERTS, d, D_EXPERT)
    wu_all = w_expert_up.reshape(depth * N_EXPERTS, d, D_EXPERT)
    wd_all = w_expert_down.reshape(depth * N_EXPERTS, D_EXPERT, d)
    w_in_b = w_in.astype(BF16)
    assert d == D_MODEL and s % (C_PAIRS[-1][1] * BLOCK) == 0
    tabs = _rope_tables(positions)
    for l in range(depth):
        z = _in_proj(x, norm1_g[l], w_in_b, b_in.reshape(depth, 1, D_IN), l, tabs)
        oa = _attn_a(z, attn_sinks[l])
        ob = _sgu(z, sgu_ln_g[l], sgu_ln_b[l], w_spatial[l], b_spatial[l])
        oc = _attn_c(z)
        x2, h2, route, counts, pos = _merge(
            x.reshape(bsz * s, d), oa.reshape(bsz * s, A_Q_W), ob.reshape(bsz * s, B_WIDTH),
            oc.reshape(bsz * s, C_OUT_W), z.reshape(bsz * s, D_IN), w_proj_a[l], w_proj_b[l], w_proj_c[l], w_out[l],
            norm2_g[l], w_router_group[l], w_router_expert[l])
        x = _moe(h2, route, counts, pos, x2, l, wg_all, wu_all, wd_all, norm_f_g, l == depth - 1).reshape(bsz, s, d)
    return x
```

```python
import functools

import jax
import jax.numpy as jnp
from jax import lax
from jax.experimental import pallas as pl
from jax.experimental.pallas import tpu as pltpu
from jax.experimental.pallas import tpu_sc as plsc

F32 = jnp.float32
BF16 = jnp.bfloat16

D_MODEL = 1024
HEAD_DIM = 64
ROT_DIM = HEAD_DIM // 4
ROPE_THETA = 500000.0
BLOCK = 128
EPS = 1e-5
NEG_INF = -1e30

A_Q_HEADS = 16
A_KV_HEADS = 4
A_REP = A_Q_HEADS // A_KV_HEADS
A_WINDOW = 128
A_Q_W = A_Q_HEADS * HEAD_DIM
A_KV_W = A_KV_HEADS * HEAD_DIM

B_GROUPS = 12
B_CH = 64
B_WIDTH = B_GROUPS * B_CH
B_CHUNK = 128

C_PAIRS = ((128, 1), (512, 4), (2048, 16))
C_HEADS_PER_GROUP = 4
C_HEADS = C_HEADS_PER_GROUP * len(C_PAIRS)
C_OUT_W = C_HEADS_PER_GROUP * HEAD_DIM

N_BRANCH = 3
GATE_W = N_BRANCH * D_MODEL
D_IN = A_Q_W + 2 * A_KV_W + 2 * B_WIDTH + 3 * C_HEADS * HEAD_DIM + GATE_W

N_EXPERT_GROUPS = 4
EXPERTS_PER_GROUP = 8
N_EXPERTS = N_EXPERT_GROUPS * EXPERTS_PER_GROUP
TOP_K = 2
D_EXPERT = 256

LANES = 128
MXU_W = 256
VMEM_LIMIT = 56 * 1024 * 1024

IN_TN = 256
COL_AQ = 0
COL_AK = A_Q_W // IN_TN
COL_AV = COL_AK + A_KV_W // IN_TN
COL_BU = COL_AV + A_KV_W // IN_TN
COL_BV = COL_BU + B_WIDTH // IN_TN
COL_CQ = COL_BV + B_WIDTH // IN_TN
COL_CK = COL_CQ + C_HEADS * HEAD_DIM // IN_TN
COL_CV = COL_CK + C_HEADS * HEAD_DIM // IN_TN
COL_G = COL_CV + C_HEADS * HEAD_DIM // IN_TN
N_COL_TILES = D_IN // IN_TN

ROUTE_W = LANES
ROUTE_E0 = N_EXPERT_GROUPS
SUBLANES = 8


def _cparams(sem):
    return pltpu.CompilerParams(dimension_semantics=sem, vmem_limit_bytes=VMEM_LIMIT)


def _rope_table_kernel(pos_ref, inv_ref, c_ref, s1_ref, s2_ref):
    lane = lax.broadcasted_iota(jnp.int32, (1, LANES), 1)
    d = lane & (HEAD_DIM - 1)
    ang = pos_ref[...].astype(F32) * inv_ref[...]
    c = jnp.cos(ang)
    s = jnp.sin(ang)
    half = ROT_DIM // 2
    c_ref[...] = jnp.where(d < ROT_DIM, c, 1.0)
    s1_ref[...] = jnp.where(d < half, -s, 0.0)
    s2_ref[...] = jnp.where((d >= half) & (d < ROT_DIM), s, 0.0)


def _rope_tables(positions):
    n = positions.size
    inv = ROPE_THETA ** (-jnp.arange(0, ROT_DIM, 2, dtype=F32) / ROT_DIM)
    inv_lane = jnp.tile(inv, LANES // inv.shape[0]).reshape(1, LANES)
    pos_b = jnp.broadcast_to(positions.reshape(n, 1), (n, LANES))
    tm = 1024
    spec = pl.BlockSpec((tm, LANES), lambda i: (i, 0))
    return pl.pallas_call(
        _rope_table_kernel,
        out_shape=[jax.ShapeDtypeStruct((n, LANES), F32)] * 3,
        grid=(n // tm,),
        in_specs=[spec, pl.BlockSpec((1, LANES), lambda i: (0, 0))],
        out_specs=[spec, spec, spec],
        compiler_params=_cparams(("parallel",)),
        name="rope_tables",
    )(pos_b, inv_lane)


IN_STEP_TILES = 3
IN_STEP_W = IN_STEP_TILES * IN_TN
IN_STEPS = N_COL_TILES // IN_STEP_TILES


def _is_rope_tile(tile):
    return tile < COL_AV or COL_CQ <= tile < COL_CV


_STEP_ROPES = tuple(tuple(_is_rope_tile(step * IN_STEP_TILES + u) for u in range(IN_STEP_TILES))
                    for step in range(IN_STEPS))


def _in_proj_kernel(x_ref, g_ref, w_ref, b_ref, c_ref, s1_ref, s2_ref, z_ref, h_ref,
                    acc0_ref, acc1_ref):
    s = pl.program_id(0)
    total = pl.num_programs(0) - 1
    j = lax.rem(s, IN_STEPS)

    @pl.when((j == 0) & (s < total))
    def _():
        x = x_ref[0]
        ms = jnp.mean(x * x, -1, keepdims=True)
        h_ref[...] = (x * lax.rsqrt(ms + EPS) * g_ref[...]).astype(BF16)

    prev = jnp.where(j == 0, IN_STEPS - 1, j - 1)

    def matmul_into(acc_ref):
        acc_ref[...] = jnp.dot(h_ref[...], w_ref[0], preferred_element_type=F32) + b_ref[0]

    def finish(acc_ref, ropes):
        for u, rope in enumerate(ropes):
            cols = slice(u * IN_TN, (u + 1) * IN_TN)
            if not rope:
                z_ref[0, :, cols] = acc_ref[:, cols].astype(BF16)
                continue
            tile = prev * IN_STEP_TILES + u
            is_q = (tile < COL_AK) | ((tile >= COL_CQ) & (tile < COL_CK))
            scale = jnp.where(is_q, HEAD_DIM ** -0.5, 1.0).astype(F32)
            for t in range(u * IN_TN // LANES, (u + 1) * IN_TN // LANES):
                lanes = slice(t * LANES, (t + 1) * LANES)
                a = acc_ref[:, lanes]
                r = (a * c_ref[...] + pltpu.roll(a, LANES - ROT_DIM // 2, 1) * s1_ref[...]
                     + pltpu.roll(a, ROT_DIM // 2, 1) * s2_ref[...])
                z_ref[0, :, lanes] = (r * scale).astype(BF16)

    accs = (acc0_ref, acc1_ref)
    patterns = sorted(set(_STEP_ROPES))
    pattern_id = jnp.int32(0)
    for step, ropes in enumerate(_STEP_ROPES):
        pattern_id = jnp.where(prev == step, patterns.index(ropes), pattern_id)
    assert not any(_STEP_ROPES[-1])
    inner = (s > 0) & (s < total)
    for parity in range(2):
        for pid, ropes in enumerate(patterns):
            @pl.when(inner & (lax.rem(s, 2) == parity) & (pattern_id == pid))
            def _(parity=parity, ropes=ropes):
                matmul_into(accs[parity])
                finish(accs[1 - parity], ropes)

        @pl.when((s == total) & (lax.rem(s, 2) == parity))
        def _(parity=parity):
            finish(accs[1 - parity], _STEP_ROPES[-1])

    @pl.when(s == 0)
    def _():
        matmul_into(accs[0])


def _in_proj(x, g, w_all, b_all, layer, tabs):
    bsz, s, d = x.shape
    c, s1, s2 = tabs
    total = bsz * IN_STEPS

    def row(t):
        return jnp.minimum(t, total - 1) // IN_STEPS

    def col(t):
        return lax.rem(jnp.minimum(t, total - 1), IN_STEPS)

    def lag(t):
        return jnp.maximum(t - 1, 0)

    tab_spec = pl.BlockSpec((s, LANES), lambda t: (row(t), 0))
    return pl.pallas_call(
        _in_proj_kernel,
        out_shape=jax.ShapeDtypeStruct((bsz, s, D_IN), BF16),
        grid=(total + 1,),
        in_specs=[
            pl.BlockSpec((1, s, d), lambda t: (row(t), 0, 0)),
            pl.BlockSpec((1, d), lambda t: (0, 0)),
            pl.BlockSpec((1, d, IN_STEP_W), lambda t: (layer, 0, col(t))),
            pl.BlockSpec((1, 1, IN_STEP_W), lambda t: (layer, 0, col(t))),
            tab_spec, tab_spec, tab_spec,
        ],
        out_specs=pl.BlockSpec((1, s, IN_STEP_W), lambda t: (row(lag(t)), 0, col(lag(t)))),
        scratch_shapes=[pltpu.VMEM((s, d), BF16), pltpu.VMEM((s, IN_STEP_W), F32),
                        pltpu.VMEM((s, IN_STEP_W), F32)],
        compiler_params=_cparams(("arbitrary",)),
        name="in_proj",
    )(x, g.reshape(1, d), w_all, b_all, c, s1, s2)


STACK = C_HEADS_PER_GROUP * BLOCK
STEP_UNROLL = 4
A_STEP_UNROLL = 8
FAST_STRIDE = 4


def _head_stack(blk):
    seg = lax.broadcasted_iota(jnp.int32, (1, MXU_W), 1) // HEAD_DIM
    rowseg = lax.broadcasted_iota(jnp.int32, (STACK, 1), 0) // BLOCK
    return jnp.where(seg == rowseg, jnp.concatenate([blk] * C_HEADS_PER_GROUP, axis=0),
                     jnp.zeros((), blk.dtype))


def _band_blocks(jobs, kp_ref, vp_ref, has_prev, diag_key, sink_col, want_lse):
    contract = (((1,), (1,)), ((), ()))
    i_loc = lax.broadcasted_iota(jnp.int32, (STACK, 1), 0) & (BLOCK - 1)
    jj = lax.broadcasted_iota(jnp.int32, (1, BLOCK), 1)
    upper = jj > i_loc
    seg = lax.broadcasted_iota(jnp.int32, (1, MXU_W), 1) // HEAD_DIM
    zero_b = jnp.zeros((), BF16)

    def key_rows(blk):
        lo = pl.multiple_of(blk * BLOCK, BLOCK)
        if has_prev:
            return pl.ds(lo, 2 * BLOCK)
        return pl.ds(pl.multiple_of(lo + BLOCK, BLOCK), BLOCK)

    scores = [lax.dot_general(_head_stack(q), kp_ref[key_rows(blk), :], contract,
                              preferred_element_type=F32) for q, blk, _ in jobs]

    def own_lanes(x):
        out = x[0:BLOCK]
        for h in range(1, C_HEADS_PER_GROUP):
            out = jnp.where(seg == h, x[h * BLOCK:(h + 1) * BLOCK], out)
        return out

    probs, stats = [], []
    for s, (_, blk, n) in zip(scores, jobs):
        if has_prev:
            sp = s[:, :BLOCK] + jnp.where(n == 0, NEG_INF, 0.0).astype(F32)
            f = jnp.where(upper, sp, s[:, BLOCK:])
        else:
            f = jnp.where(upper, NEG_INF, s)
        m = jnp.max(f, -1, keepdims=True)
        if diag_key:
            sd = jnp.sum(jnp.where(jj == i_loc, sp, 0.0), -1, keepdims=True)
            m = jnp.maximum(m, sd)
        if sink_col is not None:
            m = jnp.maximum(m, sink_col)
        p = jnp.exp(f - m)
        den = jnp.sum(p, -1, keepdims=True)
        pd = None
        if diag_key:
            pd = jnp.exp(sd - m)
            den = den + pd
        if sink_col is not None:
            den = den + jnp.exp(sink_col - m)
        inv = 1.0 / den
        pn = (p * inv).astype(BF16)
        if has_prev:
            below = zero_b
            if diag_key:
                below = jnp.where(jj == i_loc, pd * inv, 0.0).astype(BF16)
            pn = jnp.concatenate([jnp.where(upper, pn, below), jnp.where(upper, zero_b, pn)],
                                 axis=1)
        probs.append(pn)
        stats.append((m, den))

    outs = []
    for pn, (m, den), (_, blk, _) in zip(probs, stats, jobs):
        o = own_lanes(jnp.dot(pn, vp_ref[key_rows(blk), :], preferred_element_type=F32))
        lse = own_lanes(m + jnp.log(den)) + jnp.zeros((BLOCK, MXU_W), F32) if want_lse else None
        outs.append((o, lse))
    return outs


def _replicate_head(blk, g):
    words = pltpu.bitcast(blk, jnp.uint32)
    tile = words[:, (g // 2) * LANES:(g // 2 + 1) * LANES]
    swapped = pltpu.roll(tile, HEAD_DIM, 1)
    low = lax.broadcasted_iota(jnp.int32, (1, LANES), 1) < HEAD_DIM
    both = jnp.where(low, tile, swapped) if g % 2 == 0 else jnp.where(low, swapped, tile)
    return pltpu.bitcast(jnp.concatenate([both, both], axis=1), BF16)


def _attn_a_kernel(sink_ref, q_ref, k_ref, v_ref, o_ref, kp_ref, vp_ref):
    nb = q_ref.shape[1] // BLOCK
    rowseg = lax.broadcasted_iota(jnp.int32, (STACK, 1), 0) // BLOCK
    kp_ref[pl.ds(0, BLOCK), :] = jnp.zeros((BLOCK, MXU_W), BF16)
    vp_ref[pl.ds(0, BLOCK), :] = jnp.zeros((BLOCK, MXU_W), BF16)
    for g in range(A_KV_HEADS):
        q_cols = slice(g * MXU_W, (g + 1) * MXU_W)

        def build(n, carry):
            rows = pl.ds(pl.multiple_of(n * BLOCK, BLOCK), BLOCK)
            dst = pl.ds(pl.multiple_of((n + 1) * BLOCK, BLOCK), BLOCK)
            kp_ref[dst, :] = _replicate_head(k_ref[0, rows, :], g)
            vp_ref[dst, :] = _replicate_head(v_ref[0, rows, :], g)
            return carry

        lax.fori_loop(0, nb, build, 0, unroll=8)
        sink_col = jnp.zeros((STACK, 1), F32)
        for r in range(A_REP):
            sink_col = jnp.where(rowseg == r, sink_ref[g * A_REP + r], sink_col)

        def step(it, carry):
            blocks = [it * A_STEP_UNROLL + u for u in range(A_STEP_UNROLL)]
            rows = [pl.ds(pl.multiple_of(n * BLOCK, BLOCK), BLOCK) for n in blocks]
            jobs = [(q_ref[0, r, q_cols], n, n) for r, n in zip(rows, blocks)]
            outs = _band_blocks(jobs, kp_ref, vp_ref, True, False, sink_col, False)
            for r, (o, _) in zip(rows, outs):
                o_ref[0, r, q_cols] = o.astype(o_ref.dtype)
            return carry

        lax.fori_loop(0, nb // A_STEP_UNROLL, step, 0)


def _attn_a(z, sinks):
    bsz, s, _ = z.shape
    nb = s // BLOCK
    grid_spec = pltpu.PrefetchScalarGridSpec(
        num_scalar_prefetch=1,
        grid=(bsz,),
        in_specs=[
            pl.BlockSpec((1, s, A_Q_W), lambda bi, sk: (bi, 0, 0)),
            pl.BlockSpec((1, s, A_KV_W), lambda bi, sk: (bi, 0, COL_AK)),
            pl.BlockSpec((1, s, A_KV_W), lambda bi, sk: (bi, 0, COL_AV)),
        ],
        out_specs=pl.BlockSpec((1, s, A_Q_W), lambda bi, sk: (bi, 0, 0)),
        scratch_shapes=[pltpu.VMEM(((nb + 1) * BLOCK, MXU_W), BF16),
                        pltpu.VMEM(((nb + 1) * BLOCK, MXU_W), BF16)],
    )
    return pl.pallas_call(
        _attn_a_kernel,
        out_shape=jax.ShapeDtypeStruct((bsz, s, A_Q_W), BF16),
        grid_spec=grid_spec,
        compiler_params=_cparams(("parallel",)),
        name="attn_a",
    )(sinks, z, z, z)


def _attn_c_kernel(q_ref, k_ref, v_ref, oc_ref, stage_ref, stage2_ref, pq_ref, pk_ref, pv_ref,
                   og_ref, lg_ref):
    s = q_ref.shape[1]
    nblk = s // BLOCK
    halves = MXU_W // LANES
    pk_ref[pl.ds(0, BLOCK), :] = jnp.zeros((BLOCK, MXU_W), BF16)
    pv_ref[pl.ds(0, BLOCK), :] = jnp.zeros((BLOCK, MXU_W), BF16)

    def class_rows(idx, nb, d):
        r = idx // nb
        n = idx - r * nb
        start = r + n * (BLOCK * d)
        rows = pl.ds(start, BLOCK, stride=d) if d > 1 else pl.ds(pl.multiple_of(start, BLOCK), BLOCK)
        return n, rows

    quarter = s // FAST_STRIDE

    def stage(src_ref, cols, d):
        x = src_ref[0, :, cols].astype(F32)
        for t in range(halves):
            stage_ref[t] = x[:, t * LANES:(t + 1) * LANES]
        if d > FAST_STRIDE:
            for t in range(halves):
                for r1 in range(FAST_STRIDE):
                    stage2_ref[t, pl.ds(r1 * quarter, quarter), :] = (
                        stage_ref[t, pl.ds(r1, quarter, stride=FAST_STRIDE), :])

    def staged_block(idx, nb, d):
        if d > FAST_STRIDE:
            assert nb == 1 and d == FAST_STRIDE * FAST_STRIDE
            start = (idx % FAST_STRIDE) * quarter + idx // FAST_STRIDE
            rows = pl.ds(start, BLOCK, stride=FAST_STRIDE)
            parts = [stage2_ref[t, rows, :] for t in range(halves)]
        else:
            _, rows = class_rows(idx, nb, d)
            parts = [stage_ref[t, rows, :] for t in range(halves)]
        return jnp.concatenate(parts, axis=1).astype(BF16)

    for g, (win, d) in enumerate(C_PAIRS):
        cols = slice(g * MXU_W, (g + 1) * MXU_W)
        nb = nblk // d
        has_prev = nb > 1
        assert win // d == BLOCK

        def class_major(dst_ref, offset):
            def build(idx, carry):
                dst = pl.ds(pl.multiple_of((idx + offset) * BLOCK, BLOCK), BLOCK)
                dst_ref[dst, :] = staged_block(idx, nb, d)
                return carry
            return build

        for src_ref, dst_ref, offset in ((k_ref, pk_ref, 1), (v_ref, pv_ref, 1), (q_ref, pq_ref, 0)):
            if d == 1:
                dst_ref[pl.ds(offset * BLOCK, s), :] = src_ref[0, :, cols]
                continue
            stage(src_ref, cols, d)
            lax.fori_loop(0, nblk, class_major(dst_ref, offset), 0, unroll=4)

        def step(it, carry):
            jobs, dsts = [], []
            for u in range(STEP_UNROLL):
                idx = it * STEP_UNROLL + u
                n, rows = class_rows(idx, nb, d)
                here = pl.ds(pl.multiple_of(idx * BLOCK, BLOCK), BLOCK)
                jobs.append((pq_ref[here, :], idx, n))
                dsts.append(rows)
            outs = _band_blocks(jobs, pk_ref, pv_ref, has_prev, has_prev, None, True)
            for rows, (o, lse) in zip(dsts, outs):
                for t in range(halves):
                    og_ref[g, t, rows, :] = o[:, t * LANES:(t + 1) * LANES]
                    lg_ref[g, t, rows, :] = lse[:, t * LANES:(t + 1) * LANES]
            return carry

        lax.fori_loop(0, nblk // STEP_UNROLL, step, 0)

    chunk = 2 * BLOCK

    def combine(i, carry):
        rows = pl.ds(pl.multiple_of(i * chunk, chunk), chunk)
        for t in range(halves):
            ls = [lg_ref[g, t, rows, :] for g in range(len(C_PAIRS))]
            m = jnp.maximum(jnp.maximum(ls[0], ls[1]), ls[2])
            es = [jnp.exp(l - m) for l in ls]
            num = (es[0] * og_ref[0, t, rows, :] + es[1] * og_ref[1, t, rows, :]
                   + es[2] * og_ref[2, t, rows, :])
            oc_ref[0, rows, t * LANES:(t + 1) * LANES] = (
                num / (es[0] + es[1] + es[2])).astype(oc_ref.dtype)
        return carry

    lax.fori_loop(0, s // chunk, combine, 0)


def _attn_c(z):
    bsz, s, _ = z.shape
    width = C_HEADS * HEAD_DIM
    halves = MXU_W // LANES

    def in_spec(col):
        return pl.BlockSpec((1, s, width), lambda bi: (bi, 0, col * IN_TN // width))

    return pl.pallas_call(
        _attn_c_kernel,
        out_shape=jax.ShapeDtypeStruct((bsz, s, C_OUT_W), BF16),
        grid=(bsz,),
        in_specs=[in_spec(COL_CQ), in_spec(COL_CK), in_spec(COL_CV)],
        out_specs=pl.BlockSpec((1, s, C_OUT_W), lambda bi: (bi, 0, 0)),
        scratch_shapes=[
            pltpu.VMEM((halves, s, LANES), F32),
            pltpu.VMEM((halves, s, LANES), F32),
            pltpu.VMEM((s, MXU_W), BF16),
            pltpu.VMEM((s + BLOCK, MXU_W), BF16),
            pltpu.VMEM((s + BLOCK, MXU_W), BF16),
            pltpu.VMEM((len(C_PAIRS), halves, s, LANES), F32),
            pltpu.VMEM((len(C_PAIRS), halves, s, LANES), F32),
        ],
        compiler_params=_cparams(("parallel",)),
        name="attn_c",
    )(z, z, z)


def _sgu_kernel(zu_ref, zv_ref, lng_ref, lnb_ref, ws_ref, bs_ref, o_ref):
    tm = zu_ref.shape[1]
    u = jax.nn.gelu(zu_ref[0]).astype(F32)
    v = jax.nn.gelu(zv_ref[0]).astype(F32)
    mu = jnp.mean(v, -1, keepdims=True)
    var = jnp.mean(jnp.square(v - mu), -1, keepdims=True)
    vn = ((v - mu) * lax.rsqrt(var + EPS) * lng_ref[...] + lnb_ref[...]).astype(BF16)
    causal = (lax.broadcasted_iota(jnp.int32, (B_CHUNK, 1), 0)
              >= lax.broadcasted_iota(jnp.int32, (1, B_CHUNK), 1))
    first = lax.broadcasted_iota(jnp.int32, (1, LANES), 1) < B_CH
    zero = jnp.zeros((), BF16)
    for p in range(B_WIDTH // LANES):
        w0 = jnp.where(causal, ws_ref[2 * p], zero)
        w1 = jnp.where(causal, ws_ref[2 * p + 1], zero)
        cols = slice(p * LANES, (p + 1) * LANES)
        for c in range(tm // B_CHUNK):
            rows = slice(c * B_CHUNK, (c + 1) * B_CHUNK)
            vv = vn[rows, cols]
            sv = jnp.where(first,
                           jnp.dot(w0, vv, preferred_element_type=F32),
                           jnp.dot(w1, vv, preferred_element_type=F32)) + bs_ref[:, cols]
            o_ref[0, rows, cols] = (u[rows, cols] * sv).astype(o_ref.dtype)


def _sgu(z, ln_g, ln_b, w_s, b_s):
    bsz, s, _ = z.shape
    tm = 512
    bias = jnp.repeat(b_s.T, B_CH, axis=1)
    col_u = COL_BU * IN_TN // B_WIDTH
    col_v = COL_BV * IN_TN // B_WIDTH
    return pl.pallas_call(
        _sgu_kernel,
        out_shape=jax.ShapeDtypeStruct((bsz, s, B_WIDTH), BF16),
        grid=(bsz, s // tm),
        in_specs=[
            pl.BlockSpec((1, tm, B_WIDTH), lambda bi, i: (bi, i, col_u)),
            pl.BlockSpec((1, tm, B_WIDTH), lambda bi, i: (bi, i, col_v)),
            pl.BlockSpec((1, B_WIDTH), lambda bi, i: (0, 0)),
            pl.BlockSpec((1, B_WIDTH), lambda bi, i: (0, 0)),
            pl.BlockSpec((B_GROUPS, B_CHUNK, B_CHUNK), lambda bi, i: (0, 0, 0)),
            pl.BlockSpec((B_CHUNK, B_WIDTH), lambda bi, i: (0, 0)),
        ],
        out_specs=pl.BlockSpec((1, tm, B_WIDTH), lambda bi, i: (bi, i, 0)),
        compiler_params=_cparams(("parallel", "parallel")),
        name="sgu",
    )(z, z, ln_g.reshape(1, B_WIDTH), ln_b.reshape(1, B_WIDTH), w_s.astype(BF16), bias)


def _route(logits):
    lane = lax.broadcasted_iota(jnp.int32, (1, ROUTE_W), 1)
    lane_f = lane.astype(F32)
    is_g = lane < N_EXPERT_GROUPS
    lg = jnp.where(is_g, logits, NEG_INF)
    mg = jnp.max(lg, -1, keepdims=True)
    pg_top = 1.0 / jnp.sum(jnp.where(is_g, jnp.exp(lg - mg), 0.0), -1, keepdims=True)
    g_idx = jnp.min(jnp.where(lg == mg, lane_f, float(ROUTE_W)), -1, keepdims=True)
    e_group = ((lane - ROUTE_E0) // EXPERTS_PER_GROUP).astype(F32)
    in_grp = (lane >= ROUTE_E0) & (lane < ROUTE_E0 + N_EXPERTS) & (e_group == g_idx)
    le = jnp.where(in_grp, logits, NEG_INF)
    m1 = jnp.max(le, -1, keepdims=True)
    i1 = jnp.min(jnp.where(le == m1, lane_f, float(ROUTE_W)), -1, keepdims=True)
    le2 = jnp.where(lane_f == i1, NEG_INF, le)
    m2 = jnp.max(le2, -1, keepdims=True)
    i2 = jnp.min(jnp.where(le2 == m2, lane_f, float(ROUTE_W)), -1, keepdims=True)
    t = jnp.exp(m2 - m1)
    w1 = pg_top / (1.0 + t)
    w2 = w1 * t
    picks = (i1 - ROUTE_E0, i2 - ROUTE_E0, w1, w2)
    out = jnp.zeros(logits.shape, F32)
    for k, val in enumerate(picks):
        out = jnp.where(lane == k, val, out)
    return out


def _merge_kernel(x_ref, oa_ref, ob_ref, oc_ref,
                  g0_ref, g1_ref, g2_ref, g3_ref, wa_ref, wb_ref, wc_ref, wo_ref, n2_ref, wr_ref,
                  xo_ref, h2_ref, route_ref, counts_ref, pos_ref, xn0_ref, xn1_ref, cnt_ref):
    i = pl.program_id(0)
    n_tiles = pl.num_programs(0) - 1

    def residual_into(xn_ref):
        zg = jnp.concatenate([g0_ref[...], g1_ref[...], g2_ref[...], g3_ref[...]],
                             axis=1).astype(F32)
        gates = jax.nn.sigmoid(zg)
        pa = jnp.dot(oa_ref[...], wa_ref[...], preferred_element_type=F32)
        pb = jnp.dot(ob_ref[...], wb_ref[...], preferred_element_type=F32)
        pc = jnp.dot(oc_ref[...], wc_ref[...], preferred_element_type=F32)
        merged = (gates[:, :D_MODEL] * pa + gates[:, D_MODEL:2 * D_MODEL] * pb
                  + gates[:, 2 * D_MODEL:] * pc)
        xn_ref[...] = x_ref[...] + jnp.dot(merged.astype(BF16), wo_ref[...],
                                           preferred_element_type=F32)

    def tail_from(xn_ref):
        xn = xn_ref[...]
        xo_ref[...] = xn
        ms = jnp.mean(xn * xn, -1, keepdims=True)
        h2 = xn * lax.rsqrt(ms + EPS) * n2_ref[...]
        h2_ref[...] = _pack_bf16_pairs(h2)
        route = _route(jnp.dot(h2.astype(BF16), wr_ref[...], preferred_element_type=F32))
        tm = route.shape[0]
        lane = lax.broadcasted_iota(jnp.int32, (1, ROUTE_W), 1)
        lane_f = lane.astype(F32)
        row = lax.broadcasted_iota(jnp.int32, (tm, 1), 0)
        hit = [lane_f == route[:, k:k + 1] for k in range(TOP_K)]
        onehot = jnp.where(hit[0], 1.0, 0.0) + jnp.where(hit[1], 1.0, 0.0)
        scan = onehot
        shift = 1
        while shift < tm:
            scan = scan + jnp.where(row >= shift, pltpu.roll(scan, shift, 0), 0.0)
            shift *= 2
        before = scan - onehot + cnt_ref[0:1, :]
        capacity = pos_ref.shape[1] * LANES
        diag = (row & (LANES - 1)) == lane
        first_row = pl.multiple_of((i - 1) * (tm // LANES), tm // LANES)
        for k in range(TOP_K):
            rank = jnp.sum(jnp.where(hit[k], before, 0.0), -1, keepdims=True)
            slot = jnp.where(diag, route[:, k:k + 1] * capacity + rank, 0.0)
            dense = jnp.concatenate(
                [jnp.sum(slot[r * LANES:(r + 1) * LANES], axis=0, keepdims=True)
                 for r in range(tm // LANES)], axis=0)
            pos_ref[k, pl.ds(first_row, tm // LANES), :] = dense.astype(jnp.int32)
        route_ref[...] = route
        cnt = cnt_ref[0:1, :] + jnp.sum(onehot, axis=0, keepdims=True)
        cnt_ref[...] = jnp.broadcast_to(cnt, cnt_ref.shape)
        counts_ref[...] = jnp.broadcast_to(cnt, counts_ref.shape)

    bufs = (xn0_ref, xn1_ref)
    for parity in range(2):
        @pl.when((i > 0) & (i < n_tiles) & (i % 2 == parity))
        def _(parity=parity):
            tail_from(bufs[1 - parity])
            residual_into(bufs[parity])

        @pl.when((i == n_tiles) & (i % 2 == parity))
        def _(parity=parity):
            tail_from(bufs[1 - parity])

    @pl.when(i == 0)
    def _():
        cnt_ref[...] = jnp.zeros_like(cnt_ref)
        residual_into(bufs[0])


def _merge(x2, oa, ob, oc, z2, wa, wb, wc, wo, n2g, w_rg, w_re):
    n, d = x2.shape
    tm = 512
    wr = jnp.concatenate(
        [w_rg, w_re, jnp.zeros((d, ROUTE_W - N_EXPERT_GROUPS - N_EXPERTS), w_rg.dtype)], axis=1)
    gate_w = GATE_W // 4
    gate_col0 = COL_G * IN_TN // gate_w

    n_tiles = n // tm
    last = n_tiles - 1

    def rows(width):
        return pl.BlockSpec((tm, width), lambda i: (jnp.minimum(i, last), 0))

    def lagged_rows(width):
        return pl.BlockSpec((tm, width), lambda i: (jnp.maximum(i - 1, 0), 0))

    def full(shape):
        return pl.BlockSpec(shape, lambda i: (0, 0))

    gate_specs = [pl.BlockSpec((tm, gate_w), lambda i, k=k: (jnp.minimum(i, last), gate_col0 + k))
                  for k in range(4)]
    return pl.pallas_call(
        _merge_kernel,
        out_shape=[jax.ShapeDtypeStruct((n, d), F32),
                   jax.ShapeDtypeStruct((n, d // 2), jnp.int32),
                   jax.ShapeDtypeStruct((n, ROUTE_W), F32),
                   jax.ShapeDtypeStruct((SUBLANES, ROUTE_W), F32),
                   jax.ShapeDtypeStruct((TOP_K, n // LANES, LANES), jnp.int32)],
        grid=(n_tiles + 1,),
        in_specs=[rows(d), rows(A_Q_W), rows(B_WIDTH), rows(C_OUT_W),
                  *gate_specs,
                  full((A_Q_W, d)), full((B_WIDTH, d)), full((C_OUT_W, d)), full((d, d)),
                  full((1, d)), full((d, ROUTE_W))],
        out_specs=[lagged_rows(d), lagged_rows(d // 2), lagged_rows(ROUTE_W),
                   full((SUBLANES, ROUTE_W)),
                   pl.BlockSpec((TOP_K, n // LANES, LANES), lambda i: (0, 0, 0))],
        scratch_shapes=[pltpu.VMEM((tm, d), F32), pltpu.VMEM((tm, d), F32),
                        pltpu.VMEM((SUBLANES, ROUTE_W), F32)],
        compiler_params=_cparams(("arbitrary",)),
        name="merge",
    )(x2, oa, ob, oc, z2, z2, z2, z2,
      wa.astype(BF16), wb.astype(BF16), wc.astype(BF16), wo.astype(BF16),
      n2g.reshape(1, d), wr.astype(BF16))


MOE_TM = 1024
SC_CORES = 2
SC_SUBCORES = 16
SC_WORKERS = SC_CORES * SC_SUBCORES
SC_CHUNK = 128


def _pack_bf16_pairs(x):
    w = x.shape[1] // 2
    lo = pltpu.bitcast(x[:, :w].astype(BF16).astype(F32), jnp.int32)
    hi = pltpu.bitcast(x[:, w:].astype(BF16).astype(F32), jnp.int32)
    return (hi & jnp.int32(-65536)) | lax.shift_right_logical(lo, jnp.int32(16))


def _unpack_bf16_pairs(p):
    lo = pltpu.bitcast(lax.shift_left(p, jnp.int32(16)), F32)
    hi = pltpu.bitcast(p & jnp.int32(-65536), F32)
    return jnp.concatenate([lo, hi], axis=1)


def _sc_gather_rows(table, idx):
    rows, width = idx.shape[0], table.shape[1]
    per_worker = rows // SC_WORKERS
    n_chunks = per_worker // SC_CHUNK
    assert per_worker * SC_WORKERS == rows and n_chunks * SC_CHUNK == per_worker
    mesh = plsc.VectorSubcoreMesh(core_axis_name="c", subcore_axis_name="s",
                                  num_cores=SC_CORES, num_subcores=SC_SUBCORES)

    @functools.partial(
        pl.kernel, mesh=mesh,
        out_type=jax.ShapeDtypeStruct((rows, width), table.dtype),
        scratch_types=[pltpu.VMEM((n_chunks, SC_CHUNK), jnp.int32),
                       pltpu.VMEM((SC_CHUNK, width), table.dtype),
                       pltpu.SemaphoreType.DMA],
        name="sc_gather_rows",
    )
    def gather(table_hbm, idx_hbm, out_hbm, idx_v, rows_v, sem):
        wid = lax.axis_index("s") * SC_CORES + lax.axis_index("c")
        pltpu.sync_copy(idx_hbm.at[wid], idx_v)
        base = wid * per_worker

        @pl.loop(0, n_chunks)
        def _(c):
            pltpu.async_copy(table_hbm.at[idx_v.at[c]], rows_v, sem).wait()
            pltpu.sync_copy(rows_v, out_hbm.at[pl.ds(base + c * SC_CHUNK, SC_CHUNK)])

    return gather(table, idx.reshape(SC_WORKERS, n_chunks, SC_CHUNK))


def _sc_scatter_rows(rows, pos, n_slots):
    n, width = rows.shape
    per_worker = n // SC_WORKERS
    n_chunks = per_worker // SC_CHUNK
    assert per_worker * SC_WORKERS == n and n_chunks * SC_CHUNK == per_worker
    mesh = plsc.VectorSubcoreMesh(core_axis_name="c", subcore_axis_name="s",
                                  num_cores=SC_CORES, num_subcores=SC_SUBCORES)

    @functools.partial(
        pl.kernel, mesh=mesh,
        out_type=jax.ShapeDtypeStruct((n_slots, width), rows.dtype),
        scratch_types=[pltpu.VMEM((TOP_K, n_chunks, SC_CHUNK), jnp.int32),
                       pltpu.VMEM((SC_CHUNK, width), rows.dtype),
                       pltpu.SemaphoreType.DMA],
        name="sc_scatter_rows",
    )
    def scatter(rows_hbm, pos_hbm, out_hbm, pos_v, rows_v, sem):
        wid = lax.axis_index("s") * SC_CORES + lax.axis_index("c")
        for k in range(TOP_K):
            pltpu.sync_copy(pos_hbm.at[k, wid], pos_v.at[k])
        base = wid * per_worker

        @pl.loop(0, n_chunks)
        def _(c):
            pltpu.sync_copy(rows_hbm.at[pl.ds(base + c * SC_CHUNK, SC_CHUNK)], rows_v)
            for k in range(TOP_K):
                pltpu.async_copy(rows_v, out_hbm.at[pos_v.at[k, c]], sem).wait()

    return scatter(rows, pos.reshape(TOP_K, SC_WORKERS, n_chunks, SC_CHUNK))


def _dispatch_plan(counts, capacity, n_steps, expert_base):
    counts = counts[0, :N_EXPERTS].astype(jnp.int32)
    tiles = (counts + MOE_TM - 1) // MOE_TM
    tile_ends = jnp.cumsum(tiles)
    step = jnp.arange(n_steps, dtype=jnp.int32)
    used = step < tile_ends[-1]
    expert = jnp.minimum(jnp.sum((step[:, None] >= tile_ends[None, :]).astype(jnp.int32), axis=1),
                         N_EXPERTS - 1)
    chunk = step - (tile_ends - tiles)[expert]
    rows = jnp.where(used, jnp.clip(counts[expert] - chunk * MOE_TM, 0, MOE_TM), 0)
    first = jnp.concatenate([jnp.ones((1,), jnp.int32),
                             (expert[1:] != expert[:-1]).astype(jnp.int32)])
    block = expert * (capacity // MOE_TM) + chunk
    last_used = jnp.sum(jnp.where(step == tile_ends[-1] - 1, block, 0))
    spare = N_EXPERTS * (capacity // MOE_TM)
    blocks = jnp.stack([jnp.where(used, block, last_used), jnp.where(used, block, spare)])
    return expert + expert_base, rows.astype(jnp.int32), first, blocks.astype(jnp.int32)


def _moe_tile_kernel(te_ref, tr_ref, tf_ref, tb_ref, xs_ref, wg_ref, wu_ref, wd_ref, ys_ref,
                     wg_s, wu_s, wd_s):
    t = pl.program_id(0)

    @pl.when(tf_ref[t] != 0)
    def _():
        wg_s[...] = wg_ref[0].astype(BF16)
        wu_s[...] = wu_ref[0].astype(BF16)
        wd_s[...] = wd_ref[0].astype(BF16)

    @pl.when(tr_ref[t] != 0)
    def _():
        occupied = lax.broadcasted_iota(jnp.int32, (MOE_TM, 1), 0) < tr_ref[t]
        x = jnp.where(occupied, _unpack_bf16_pairs(xs_ref[...]), 0.0).astype(BF16)
        hg = jnp.dot(x, wg_s[...], preferred_element_type=F32)
        hu = jnp.dot(x, wu_s[...], preferred_element_type=F32)
        a = (jax.nn.silu(hg) * hu).astype(BF16)
        ys_ref[...] = _pack_bf16_pairs(jnp.dot(a, wd_s[...], preferred_element_type=F32))

    @pl.when(tr_ref[t] == 0)
    def _():
        ys_ref[...] = jnp.zeros_like(ys_ref)


def _moe_tiles(xs, tile_expert, tile_rows, tile_first, tile_blocks, w_gate, w_up, w_down):
    n_slots, half = xs.shape
    d = 2 * half
    grid_spec = pltpu.PrefetchScalarGridSpec(
        num_scalar_prefetch=4,
        grid=(tile_expert.shape[0],),
        in_specs=[
            pl.BlockSpec((MOE_TM, half), lambda t, te, tr, tf, tb: (tb[0, t], 0)),
            pl.BlockSpec((1, d, D_EXPERT), lambda t, te, tr, tf, tb: (te[t], 0, 0)),
            pl.BlockSpec((1, d, D_EXPERT), lambda t, te, tr, tf, tb: (te[t], 0, 0)),
            pl.BlockSpec((1, D_EXPERT, d), lambda t, te, tr, tf, tb: (te[t], 0, 0)),
        ],
        out_specs=pl.BlockSpec((MOE_TM, half), lambda t, te, tr, tf, tb: (tb[1, t], 0)),
        scratch_shapes=[pltpu.VMEM((d, D_EXPERT), BF16), pltpu.VMEM((d, D_EXPERT), BF16),
                        pltpu.VMEM((D_EXPERT, d), BF16)],
    )
    return pl.pallas_call(
        _moe_tile_kernel,
        out_shape=jax.ShapeDtypeStruct((n_slots, half), jnp.int32),
        grid_spec=grid_spec,
        compiler_params=_cparams(("arbitrary",)),
        name="moe_tiles",
    )(tile_expert, tile_rows, tile_first, tile_blocks, xs, w_gate, w_up, w_down)


def _moe_combine_kernel(x_ref, y0_ref, y1_ref, r_ref, g_ref, o_ref, *, final_norm):
    w0 = r_ref[:, 2:3]
    w1 = r_ref[:, 3:4]
    x = x_ref[...] + w0 * _unpack_bf16_pairs(y0_ref[0]) + w1 * _unpack_bf16_pairs(y1_ref[0])
    if final_norm:
        ms = jnp.mean(x * x, -1, keepdims=True)
        x = x * lax.rsqrt(ms + EPS) * g_ref[...]
    o_ref[...] = x


def _moe_combine(x2, yg, route, norm_g, final_norm):
    n, d = x2.shape
    tm = 1024
    return pl.pallas_call(
        functools.partial(_moe_combine_kernel, final_norm=final_norm),
        out_shape=jax.ShapeDtypeStruct((n, d), F32),
        grid=(n // tm,),
        in_specs=[
            pl.BlockSpec((tm, d), lambda i: (i, 0)),
            pl.BlockSpec((1, tm, d // 2), lambda i: (0, i, 0)),
            pl.BlockSpec((1, tm, d // 2), lambda i: (1, i, 0)),
            pl.BlockSpec((tm, ROUTE_W), lambda i: (i, 0)),
            pl.BlockSpec((1, d), lambda i: (0, 0)),
        ],
        out_specs=pl.BlockSpec((tm, d), lambda i: (i, 0)),
        compiler_params=_cparams(("parallel",)),
        name="moe_combine",
    )(x2, yg, yg, route, norm_g.reshape(1, d))


def _moe(h2p, route, counts, pos, x2, layer, w_gate, w_up, w_down, norm_f_g, final_norm):
    n, d = x2.shape
    n_slots = N_EXPERTS * n + MOE_TM
    n_steps = TOP_K * n // MOE_TM + N_EXPERTS
    tile_expert, tile_rows, tile_first, tile_blocks = _dispatch_plan(
        counts, n, n_steps, layer * N_EXPERTS)
    xs = _sc_scatter_rows(h2p, pos, n_slots)
    ys = _moe_tiles(xs, tile_expert, tile_rows, tile_first, tile_blocks, w_gate, w_up, w_down)
    yg = _sc_gather_rows(ys, pos.reshape(-1)).reshape(TOP_K, n, d // 2)
    return _moe_combine(x2, yg, route, norm_f_g, final_norm)


def kernel(x, positions, norm1_g, w_in, b_in, attn_sinks, sgu_ln_g, sgu_ln_b, w_spatial, b_spatial,
           w_proj_a, w_proj_b, w_proj_c, w_out, norm2_g, w_router_group, w_router_expert,
           w_expert_gate, w_expert_up, w_expert_down, norm_f_g):
    bsz, s, d = x.shape
    depth = w_in.shape[0]
    wg_all = w_expert_gate.reshape(depth * N_EXPERTS, d, D_EXPERT)
    wu_all = w_expert_up.reshape(depth * N_EXPERTS, d, D_EXPERT)
    wd_all = w_expert_down.reshape(depth * N_EXPERTS, D_EXPERT, d)
    w_in_b = w_in.astype(BF16)
    assert d == D_MODEL and s % (C_PAIRS[-1][1] * BLOCK) == 0
    tabs = _rope_tables(positions)
    for l in range(depth):
        z = _in_proj(x, norm1_g[l], w_in_b, b_in.reshape(depth, 1, D_IN), l, tabs)
        oa = _attn_a(z, attn_sinks[l])
        ob = _sgu(z, sgu_ln_g[l], sgu_ln_b[l], w_spatial[l], b_spatial[l])
        oc = _attn_c(z)
        x2, h2, route, counts, pos = _merge(
            x.reshape(bsz * s, d), oa.reshape(bsz * s, A_Q_W), ob.reshape(bsz * s, B_WIDTH),
            oc.reshape(bsz * s, C_OUT_W), z.reshape(bsz * s, D_IN), w_proj_a[l], w_proj_b[l], w_proj_c[l], w_out[l],
            norm2_g[l], w_router_group[l], w_router_expert[l])
        x = _moe(h2, route, counts, pos, x2, l, wg_all, wu_all, wd_all, norm_f_g, l == depth - 1).reshape(bsz, s, d)
    return x
```

```python
import functools

import jax
import jax.numpy as jnp
from jax import lax
from jax.experimental import pallas as pl
from jax.experimental.pallas import tpu as pltpu
from jax.experimental.pallas import tpu_sc as plsc

F32 = jnp.float32
BF16 = jnp.bfloat16

D_MODEL = 1024
HEAD_DIM = 64
ROT_DIM = HEAD_DIM // 4
ROPE_THETA = 500000.0
BLOCK = 128
EPS = 1e-5
NEG_INF = -1e30

A_Q_HEADS = 16
A_KV_HEADS = 4
A_REP = A_Q_HEADS // A_KV_HEADS
A_WINDOW = 128
A_Q_W = A_Q_HEADS * HEAD_DIM
A_KV_W = A_KV_HEADS * HEAD_DIM

B_GROUPS = 12
B_CH = 64
B_WIDTH = B_GROUPS * B_CH
B_CHUNK = 128

C_PAIRS = ((128, 1), (512, 4), (2048, 16))
C_HEADS_PER_GROUP = 4
C_HEADS = C_HEADS_PER_GROUP * len(C_PAIRS)
C_OUT_W = C_HEADS_PER_GROUP * HEAD_DIM

N_BRANCH = 3
GATE_W = N_BRANCH * D_MODEL
D_IN = A_Q_W + 2 * A_KV_W + 2 * B_WIDTH + 3 * C_HEADS * HEAD_DIM + GATE_W

N_EXPERT_GROUPS = 4
EXPERTS_PER_GROUP = 8
N_EXPERTS = N_EXPERT_GROUPS * EXPERTS_PER_GROUP
TOP_K = 2
D_EXPERT = 256

LANES = 128
MXU_W = 256
VMEM_LIMIT = 56 * 1024 * 1024

IN_TN = 256
COL_AQ = 0
COL_AK = A_Q_W // IN_TN
COL_AV = COL_AK + A_KV_W // IN_TN
COL_BU = COL_AV + A_KV_W // IN_TN
COL_BV = COL_BU + B_WIDTH // IN_TN
COL_CQ = COL_BV + B_WIDTH // IN_TN
COL_CK = COL_CQ + C_HEADS * HEAD_DIM // IN_TN
COL_CV = COL_CK + C_HEADS * HEAD_DIM // IN_TN
COL_G = COL_CV + C_HEADS * HEAD_DIM // IN_TN
N_COL_TILES = D_IN // IN_TN

ROUTE_W = LANES
ROUTE_E0 = N_EXPERT_GROUPS
SUBLANES = 8


def _cparams(sem):
    return pltpu.CompilerParams(dimension_semantics=sem, vmem_limit_bytes=VMEM_LIMIT)


def _rope_table_kernel(pos_ref, inv_ref, c_ref, s_ref):
    lane = lax.broadcasted_iota(jnp.int32, (1, LANES), 1)
    d = lane & (HEAD_DIM - 1)
    ang = pos_ref[...].astype(F32) * inv_ref[...]
    c = jnp.cos(ang)
    s = jnp.sin(ang)
    half = ROT_DIM // 2
    c_ref[...] = jnp.where(d < ROT_DIM, c, 1.0)
    s_ref[...] = jnp.where(d < half, -s, jnp.where(d < ROT_DIM, s, 0.0))


def _rope_tables(positions):
    n = positions.size
    inv = ROPE_THETA ** (-jnp.arange(0, ROT_DIM, 2, dtype=F32) / ROT_DIM)
    inv_lane = jnp.tile(inv, LANES // inv.shape[0]).reshape(1, LANES)
    pos_b = jnp.broadcast_to(positions.reshape(n, 1), (n, LANES))
    tm = 1024
    spec = pl.BlockSpec((tm, LANES), lambda i: (i, 0))
    return pl.pallas_call(
        _rope_table_kernel,
        out_shape=[jax.ShapeDtypeStruct((n, LANES), F32)] * 2,
        grid=(n // tm,),
        in_specs=[spec, pl.BlockSpec((1, LANES), lambda i: (0, 0))],
        out_specs=[spec, spec],
        compiler_params=_cparams(("parallel",)),
        name="rope_tables",
    )(pos_b, inv_lane)


IN_STEP_TILES = 3
IN_STEP_W = IN_STEP_TILES * IN_TN
IN_STEPS = N_COL_TILES // IN_STEP_TILES


def _is_rope_tile(tile):
    return tile < COL_AV or COL_CQ <= tile < COL_CV


_STEP_ROPES = tuple(tuple(_is_rope_tile(step * IN_STEP_TILES + u) for u in range(IN_STEP_TILES))
                    for step in range(IN_STEPS))


def _in_proj_kernel(x_ref, g_ref, w_ref, b_ref, c_ref, s_ref, z_ref, h_ref, acc0_ref, acc1_ref):
    s = pl.program_id(0)
    total = pl.num_programs(0) - 1
    j = lax.rem(s, IN_STEPS)

    @pl.when((j == 0) & (s < total))
    def _():
        x = x_ref[0]
        ms = jnp.mean(x * x, -1, keepdims=True)
        h_ref[...] = (x * lax.rsqrt(ms + EPS) * g_ref[...]).astype(BF16)

    prev = jnp.where(j == 0, IN_STEPS - 1, j - 1)

    def matmul_into(acc_ref):
        acc_ref[...] = jnp.dot(h_ref[...], w_ref[0].astype(BF16),
                               preferred_element_type=F32) + b_ref[0]

    first_half = (lax.broadcasted_iota(jnp.int32, (1, LANES), 1) & (HEAD_DIM - 1)) < ROT_DIM // 2

    def finish(acc_ref, ropes):
        for u, rope in enumerate(ropes):
            cols = slice(u * IN_TN, (u + 1) * IN_TN)
            if not rope:
                z_ref[0, :, cols] = acc_ref[:, cols].astype(BF16)
                continue
            tile = prev * IN_STEP_TILES + u
            is_q = (tile < COL_AK) | ((tile >= COL_CQ) & (tile < COL_CK))
            scale = jnp.where(is_q, HEAD_DIM ** -0.5, 1.0).astype(F32)
            for t in range(u * IN_TN // LANES, (u + 1) * IN_TN // LANES):
                lanes = slice(t * LANES, (t + 1) * LANES)
                a = acc_ref[:, lanes]
                partner = jnp.where(first_half, pltpu.roll(a, LANES - ROT_DIM // 2, 1),
                                    pltpu.roll(a, ROT_DIM // 2, 1))
                r = a * c_ref[...] + partner * s_ref[...]
                z_ref[0, :, lanes] = (r * scale).astype(BF16)

    accs = (acc0_ref, acc1_ref)
    patterns = sorted(set(_STEP_ROPES))
    pattern_id = jnp.int32(0)
    for step, ropes in enumerate(_STEP_ROPES):
        pattern_id = jnp.where(prev == step, patterns.index(ropes), pattern_id)
    assert not any(_STEP_ROPES[-1])
    inner = (s > 0) & (s < total)
    for parity in range(2):
        for pid, ropes in enumerate(patterns):
            @pl.when(inner & (lax.rem(s, 2) == parity) & (pattern_id == pid))
            def _(parity=parity, ropes=ropes):
                matmul_into(accs[parity])
                finish(accs[1 - parity], ropes)

        @pl.when((s == total) & (lax.rem(s, 2) == parity))
        def _(parity=parity):
            finish(accs[1 - parity], _STEP_ROPES[-1])

    @pl.when(s == 0)
    def _():
        matmul_into(accs[0])


def _in_proj(x, g, w_all, b_all, layer, tabs):
    bsz, s, d = x.shape
    cos_tab, sin_tab = tabs
    total = bsz * IN_STEPS

    def row(t):
        return jnp.minimum(t, total - 1) // IN_STEPS

    def col(t):
        return lax.rem(jnp.minimum(t, total - 1), IN_STEPS)

    def lag(t):
        return jnp.maximum(t - 1, 0)

    tab_spec = pl.BlockSpec((s, LANES), lambda t: (row(t), 0), pipeline_mode=pl.Buffered(1))
    return pl.pallas_call(
        _in_proj_kernel,
        out_shape=jax.ShapeDtypeStruct((bsz, s, D_IN), BF16),
        grid=(total + 1,),
        in_specs=[
            pl.BlockSpec((1, s, d), lambda t: (row(t), 0, 0)),
            pl.BlockSpec((1, d), lambda t: (0, 0)),
            pl.BlockSpec((1, d, IN_STEP_W), lambda t: (layer, 0, col(t))),
            pl.BlockSpec((1, 1, IN_STEP_W), lambda t: (layer, 0, col(t))),
            tab_spec, tab_spec,
        ],
        out_specs=pl.BlockSpec((1, s, IN_STEP_W), lambda t: (row(lag(t)), 0, col(lag(t)))),
        scratch_shapes=[pltpu.VMEM((s, d), BF16), pltpu.VMEM((s, IN_STEP_W), F32),
                        pltpu.VMEM((s, IN_STEP_W), F32)],
        compiler_params=_cparams(("arbitrary",)),
        name="in_proj",
    )(x, g.reshape(1, d), w_all, b_all, cos_tab, sin_tab)


STACK = C_HEADS_PER_GROUP * BLOCK
STEP_UNROLL = 4
A_STEP_UNROLL = 8
FAST_STRIDE = 4


def _head_stack(blk):
    seg = lax.broadcasted_iota(jnp.int32, (1, MXU_W), 1) // HEAD_DIM
    rowseg = lax.broadcasted_iota(jnp.int32, (STACK, 1), 0) // BLOCK
    return jnp.where(seg == rowseg, jnp.concatenate([blk] * C_HEADS_PER_GROUP, axis=0),
                     jnp.zeros((), blk.dtype))


def _band_blocks(jobs, kp_ref, vp_ref, has_prev, diag_key, sink_col, want_lse):
    contract = (((1,), (1,)), ((), ()))
    i_loc = lax.broadcasted_iota(jnp.int32, (STACK, 1), 0) & (BLOCK - 1)
    jj = lax.broadcasted_iota(jnp.int32, (1, BLOCK), 1)
    upper = jj > i_loc
    seg = lax.broadcasted_iota(jnp.int32, (1, MXU_W), 1) // HEAD_DIM
    zero_b = jnp.zeros((), BF16)

    def key_rows(blk):
        lo = pl.multiple_of(blk * BLOCK, BLOCK)
        if has_prev:
            return pl.ds(lo, 2 * BLOCK)
        return pl.ds(pl.multiple_of(lo + BLOCK, BLOCK), BLOCK)

    scores = [lax.dot_general(_head_stack(q), kp_ref[key_rows(blk), :], contract,
                              preferred_element_type=F32) for q, blk, _ in jobs]

    def own_lanes(x):
        out = x[0:BLOCK]
        for h in range(1, C_HEADS_PER_GROUP):
            out = jnp.where(seg == h, x[h * BLOCK:(h + 1) * BLOCK], out)
        return out

    probs, stats = [], []
    for s, (_, blk, n) in zip(scores, jobs):
        if has_prev:
            sp = s[:, :BLOCK] + jnp.where(n == 0, NEG_INF, 0.0).astype(F32)
            f = jnp.where(upper, sp, s[:, BLOCK:])
        else:
            f = jnp.where(upper, NEG_INF, s)
        m = jnp.max(f, -1, keepdims=True)
        if diag_key:
            sd = jnp.sum(jnp.where(jj == i_loc, sp, 0.0), -1, keepdims=True)
            m = jnp.maximum(m, sd)
        if sink_col is not None:
            m = jnp.maximum(m, sink_col)
        p = jnp.exp(f - m)
        den = jnp.sum(p, -1, keepdims=True)
        pd = None
        if diag_key:
            pd = jnp.exp(sd - m)
            den = den + pd
        if sink_col is not None:
            den = den + jnp.exp(sink_col - m)
        inv = 1.0 / den
        pn = (p * inv).astype(BF16)
        if has_prev:
            below = zero_b
            if diag_key:
                below = jnp.where(jj == i_loc, pd * inv, 0.0).astype(BF16)
            pn = jnp.concatenate([jnp.where(upper, pn, below), jnp.where(upper, zero_b, pn)],
                                 axis=1)
        probs.append(pn)
        stats.append((m, den))

    outs = []
    for pn, (m, den), (_, blk, _) in zip(probs, stats, jobs):
        o = own_lanes(jnp.dot(pn, vp_ref[key_rows(blk), :], preferred_element_type=F32))
        lse = own_lanes(m + jnp.log(den)) + jnp.zeros((BLOCK, MXU_W), F32) if want_lse else None
        outs.append((o, lse))
    return outs


def _replicate_head(blk, g):
    words = pltpu.bitcast(blk, jnp.uint32)
    tile = words[:, (g // 2) * LANES:(g // 2 + 1) * LANES]
    swapped = pltpu.roll(tile, HEAD_DIM, 1)
    low = lax.broadcasted_iota(jnp.int32, (1, LANES), 1) < HEAD_DIM
    both = jnp.where(low, tile, swapped) if g % 2 == 0 else jnp.where(low, swapped, tile)
    return pltpu.bitcast(jnp.concatenate([both, both], axis=1), BF16)


def _attn_a_kernel(sink_ref, q_ref, k_ref, v_ref, o_ref, kp_ref, vp_ref):
    nb = q_ref.shape[1] // BLOCK
    rowseg = lax.broadcasted_iota(jnp.int32, (STACK, 1), 0) // BLOCK
    kp_ref[pl.ds(0, BLOCK), :] = jnp.zeros((BLOCK, MXU_W), BF16)
    vp_ref[pl.ds(0, BLOCK), :] = jnp.zeros((BLOCK, MXU_W), BF16)
    for g in range(A_KV_HEADS):
        q_cols = slice(g * MXU_W, (g + 1) * MXU_W)

        def build(n, carry):
            rows = pl.ds(pl.multiple_of(n * BLOCK, BLOCK), BLOCK)
            dst = pl.ds(pl.multiple_of((n + 1) * BLOCK, BLOCK), BLOCK)
            kp_ref[dst, :] = _replicate_head(k_ref[0, rows, :], g)
            vp_ref[dst, :] = _replicate_head(v_ref[0, rows, :], g)
            return carry

        lax.fori_loop(0, nb, build, 0, unroll=8)
        sink_col = jnp.zeros((STACK, 1), F32)
        for r in range(A_REP):
            sink_col = jnp.where(rowseg == r, sink_ref[g * A_REP + r], sink_col)

        def step(it, carry):
            blocks = [it * A_STEP_UNROLL + u for u in range(A_STEP_UNROLL)]
            rows = [pl.ds(pl.multiple_of(n * BLOCK, BLOCK), BLOCK) for n in blocks]
            jobs = [(q_ref[0, r, q_cols], n, n) for r, n in zip(rows, blocks)]
            outs = _band_blocks(jobs, kp_ref, vp_ref, True, False, sink_col, False)
            for r, (o, _) in zip(rows, outs):
                o_ref[0, r, q_cols] = o.astype(o_ref.dtype)
            return carry

        lax.fori_loop(0, nb // A_STEP_UNROLL, step, 0)


def _attn_a(z, sinks):
    bsz, s, _ = z.shape
    nb = s // BLOCK
    grid_spec = pltpu.PrefetchScalarGridSpec(
        num_scalar_prefetch=1,
        grid=(bsz,),
        in_specs=[
            pl.BlockSpec((1, s, A_Q_W), lambda bi, sk: (bi, 0, 0)),
            pl.BlockSpec((1, s, A_KV_W), lambda bi, sk: (bi, 0, COL_AK)),
            pl.BlockSpec((1, s, A_KV_W), lambda bi, sk: (bi, 0, COL_AV)),
        ],
        out_specs=pl.BlockSpec((1, s, A_Q_W), lambda bi, sk: (bi, 0, 0)),
        scratch_shapes=[pltpu.VMEM(((nb + 1) * BLOCK, MXU_W), BF16),
                        pltpu.VMEM(((nb + 1) * BLOCK, MXU_W), BF16)],
    )
    return pl.pallas_call(
        _attn_a_kernel,
        out_shape=jax.ShapeDtypeStruct((bsz, s, A_Q_W), BF16),
        grid_spec=grid_spec,
        compiler_params=_cparams(("parallel",)),
        name="attn_a",
    )(sinks, z, z, z)


def _attn_c_kernel(q_ref, k_ref, v_ref, oc_ref, stage_ref, stage2_ref, pq_ref, pk_ref, pv_ref,
                   og_ref, lg_ref):
    s = q_ref.shape[1]
    nblk = s // BLOCK
    halves = MXU_W // LANES
    pk_ref[pl.ds(0, BLOCK), :] = jnp.zeros((BLOCK, MXU_W), BF16)
    pv_ref[pl.ds(0, BLOCK), :] = jnp.zeros((BLOCK, MXU_W), BF16)

    def class_rows(idx, nb, d):
        r = idx // nb
        n = idx - r * nb
        start = r + n * (BLOCK * d)
        rows = pl.ds(start, BLOCK, stride=d) if d > 1 else pl.ds(pl.multiple_of(start, BLOCK), BLOCK)
        return n, rows

    quarter = s // FAST_STRIDE

    def stage(src_ref, cols, d):
        x = src_ref[0, :, cols].astype(F32)
        for t in range(halves):
            stage_ref[t] = x[:, t * LANES:(t + 1) * LANES]
        if d > FAST_STRIDE:
            for t in range(halves):
                for r1 in range(FAST_STRIDE):
                    stage2_ref[t, pl.ds(r1 * quarter, quarter), :] = (
                        stage_ref[t, pl.ds(r1, quarter, stride=FAST_STRIDE), :])

    def staged_block(idx, nb, d):
        if d > FAST_STRIDE:
            assert nb == 1 and d == FAST_STRIDE * FAST_STRIDE
            start = (idx % FAST_STRIDE) * quarter + idx // FAST_STRIDE
            rows = pl.ds(start, BLOCK, stride=FAST_STRIDE)
            parts = [stage2_ref[t, rows, :] for t in range(halves)]
        else:
            _, rows = class_rows(idx, nb, d)
            parts = [stage_ref[t, rows, :] for t in range(halves)]
        return jnp.concatenate(parts, axis=1).astype(BF16)

    for g, (win, d) in enumerate(C_PAIRS):
        cols = slice(g * MXU_W, (g + 1) * MXU_W)
        nb = nblk // d
        has_prev = nb > 1
        assert win // d == BLOCK

        def class_major(dst_ref, offset):
            def build(idx, carry):
                dst = pl.ds(pl.multiple_of((idx + offset) * BLOCK, BLOCK), BLOCK)
                dst_ref[dst, :] = staged_block(idx, nb, d)
                return carry
            return build

        for src_ref, dst_ref, offset in ((k_ref, pk_ref, 1), (v_ref, pv_ref, 1), (q_ref, pq_ref, 0)):
            if d == 1:
                dst_ref[pl.ds(offset * BLOCK, s), :] = src_ref[0, :, cols]
                continue
            stage(src_ref, cols, d)
            lax.fori_loop(0, nblk, class_major(dst_ref, offset), 0, unroll=4)

        def step(it, carry):
            jobs, dsts = [], []
            for u in range(STEP_UNROLL):
                idx = it * STEP_UNROLL + u
                n, rows = class_rows(idx, nb, d)
                here = pl.ds(pl.multiple_of(idx * BLOCK, BLOCK), BLOCK)
                jobs.append((pq_ref[here, :], idx, n))
                dsts.append(rows)
            outs = _band_blocks(jobs, pk_ref, pv_ref, has_prev, has_prev, None, True)
            for rows, (o, lse) in zip(dsts, outs):
                for t in range(halves):
                    og_ref[g, t, rows, :] = o[:, t * LANES:(t + 1) * LANES]
                    lg_ref[g, t, rows, :] = lse[:, t * LANES:(t + 1) * LANES]
            return carry

        lax.fori_loop(0, nblk // STEP_UNROLL, step, 0)

    chunk = 2 * BLOCK

    def combine(i, carry):
        rows = pl.ds(pl.multiple_of(i * chunk, chunk), chunk)
        for t in range(halves):
            ls = [lg_ref[g, t, rows, :] for g in range(len(C_PAIRS))]
            m = jnp.maximum(jnp.maximum(ls[0], ls[1]), ls[2])
            es = [jnp.exp(l - m) for l in ls]
            num = (es[0] * og_ref[0, t, rows, :] + es[1] * og_ref[1, t, rows, :]
                   + es[2] * og_ref[2, t, rows, :])
            oc_ref[0, rows, t * LANES:(t + 1) * LANES] = (
                num / (es[0] + es[1] + es[2])).astype(oc_ref.dtype)
        return carry

    lax.fori_loop(0, s // chunk, combine, 0)


def _attn_c(z):
    bsz, s, _ = z.shape
    width = C_HEADS * HEAD_DIM
    halves = MXU_W // LANES

    def in_spec(col):
        return pl.BlockSpec((1, s, width), lambda bi: (bi, 0, col * IN_TN // width))

    return pl.pallas_call(
        _attn_c_kernel,
        out_shape=jax.ShapeDtypeStruct((bsz, s, C_OUT_W), BF16),
        grid=(bsz,),
        in_specs=[in_spec(COL_CQ), in_spec(COL_CK), in_spec(COL_CV)],
        out_specs=pl.BlockSpec((1, s, C_OUT_W), lambda bi: (bi, 0, 0)),
        scratch_shapes=[
            pltpu.VMEM((halves, s, LANES), F32),
            pltpu.VMEM((halves, s, LANES), F32),
            pltpu.VMEM((s, MXU_W), BF16),
            pltpu.VMEM((s + BLOCK, MXU_W), BF16),
            pltpu.VMEM((s + BLOCK, MXU_W), BF16),
            pltpu.VMEM((len(C_PAIRS), halves, s, LANES), F32),
            pltpu.VMEM((len(C_PAIRS), halves, s, LANES), F32),
        ],
        compiler_params=_cparams(("parallel",)),
        name="attn_c",
    )(z, z, z)


def _sgu_kernel(zu_ref, zv_ref, lng_ref, lnb_ref, ws_ref, bs_ref, o_ref):
    tm = zu_ref.shape[1]
    u = jax.nn.gelu(zu_ref[0]).astype(F32)
    v = jax.nn.gelu(zv_ref[0]).astype(F32)
    mu = jnp.mean(v, -1, keepdims=True)
    var = jnp.mean(jnp.square(v - mu), -1, keepdims=True)
    vn = ((v - mu) * lax.rsqrt(var + EPS) * lng_ref[...] + lnb_ref[...]).astype(BF16)
    causal = (lax.broadcasted_iota(jnp.int32, (B_CHUNK, 1), 0)
              >= lax.broadcasted_iota(jnp.int32, (1, B_CHUNK), 1))
    first = lax.broadcasted_iota(jnp.int32, (1, LANES), 1) < B_CH
    zero = jnp.zeros((), BF16)
    for p in range(B_WIDTH // LANES):
        w0 = jnp.where(causal, ws_ref[2 * p], zero)
        w1 = jnp.where(causal, ws_ref[2 * p + 1], zero)
        cols = slice(p * LANES, (p + 1) * LANES)
        for c in range(tm // B_CHUNK):
            rows = slice(c * B_CHUNK, (c + 1) * B_CHUNK)
            vv = vn[rows, cols]
            sv = jnp.where(first,
                           jnp.dot(w0, vv, preferred_element_type=F32),
                           jnp.dot(w1, vv, preferred_element_type=F32)) + bs_ref[:, cols]
            o_ref[0, rows, cols] = (u[rows, cols] * sv).astype(o_ref.dtype)


def _sgu(z, ln_g, ln_b, w_s, b_s):
    bsz, s, _ = z.shape
    tm = 512
    bias = jnp.repeat(b_s.T, B_CH, axis=1)
    col_u = COL_BU * IN_TN // B_WIDTH
    col_v = COL_BV * IN_TN // B_WIDTH
    return pl.pallas_call(
        _sgu_kernel,
        out_shape=jax.ShapeDtypeStruct((bsz, s, B_WIDTH), BF16),
        grid=(bsz, s // tm),
        in_specs=[
            pl.BlockSpec((1, tm, B_WIDTH), lambda bi, i: (bi, i, col_u)),
            pl.BlockSpec((1, tm, B_WIDTH), lambda bi, i: (bi, i, col_v)),
            pl.BlockSpec((1, B_WIDTH), lambda bi, i: (0, 0)),
            pl.BlockSpec((1, B_WIDTH), lambda bi, i: (0, 0)),
            pl.BlockSpec((B_GROUPS, B_CHUNK, B_CHUNK), lambda bi, i: (0, 0, 0)),
            pl.BlockSpec((B_CHUNK, B_WIDTH), lambda bi, i: (0, 0)),
        ],
        out_specs=pl.BlockSpec((1, tm, B_WIDTH), lambda bi, i: (bi, i, 0)),
        compiler_params=_cparams(("parallel", "parallel")),
        name="sgu",
    )(z, z, ln_g.reshape(1, B_WIDTH), ln_b.reshape(1, B_WIDTH), w_s.astype(BF16), bias)


def _route(logits):
    lane = lax.broadcasted_iota(jnp.int32, (1, ROUTE_W), 1)
    lane_f = lane.astype(F32)
    is_g = lane < N_EXPERT_GROUPS
    lg = jnp.where(is_g, logits, NEG_INF)
    mg = jnp.max(lg, -1, keepdims=True)
    pg_top = 1.0 / jnp.sum(jnp.where(is_g, jnp.exp(lg - mg), 0.0), -1, keepdims=True)
    g_idx = jnp.min(jnp.where(lg == mg, lane_f, float(ROUTE_W)), -1, keepdims=True)
    e_group = ((lane - ROUTE_E0) // EXPERTS_PER_GROUP).astype(F32)
    in_grp = (lane >= ROUTE_E0) & (lane < ROUTE_E0 + N_EXPERTS) & (e_group == g_idx)
    le = jnp.where(in_grp, logits, NEG_INF)
    m1 = jnp.max(le, -1, keepdims=True)
    i1 = jnp.min(jnp.where(le == m1, lane_f, float(ROUTE_W)), -1, keepdims=True)
    le2 = jnp.where(lane_f == i1, NEG_INF, le)
    m2 = jnp.max(le2, -1, keepdims=True)
    i2 = jnp.min(jnp.where(le2 == m2, lane_f, float(ROUTE_W)), -1, keepdims=True)
    t = jnp.exp(m2 - m1)
    w1 = pg_top / (1.0 + t)
    w2 = w1 * t
    picks = (i1 - ROUTE_E0, i2 - ROUTE_E0, w1, w2)
    out = jnp.zeros(logits.shape, F32)
    for k, val in enumerate(picks):
        out = jnp.where(lane == k, val, out)
    return out


def _merge_kernel(x_ref, oa_ref, ob_ref, oc_ref,
                  g0_ref, g1_ref, g2_ref, g3_ref, wa_ref, wb_ref, wc_ref, wo_ref, n2_ref, wr_ref,
                  xo_ref, h2_ref, route_ref, counts_ref, pos_ref, xn0_ref, xn1_ref, cnt_ref):
    i = pl.program_id(0)
    n_tiles = pl.num_programs(0) - 1

    def residual_into(xn_ref):
        zg = jnp.concatenate([g0_ref[...], g1_ref[...], g2_ref[...], g3_ref[...]],
                             axis=1).astype(F32)
        gates = jax.nn.sigmoid(zg)
        pa = jnp.dot(oa_ref[...], wa_ref[...], preferred_element_type=F32)
        pb = jnp.dot(ob_ref[...], wb_ref[...], preferred_element_type=F32)
        pc = jnp.dot(oc_ref[...], wc_ref[...], preferred_element_type=F32)
        merged = (gates[:, :D_MODEL] * pa + gates[:, D_MODEL:2 * D_MODEL] * pb
                  + gates[:, 2 * D_MODEL:] * pc)
        xn_ref[...] = x_ref[...] + jnp.dot(merged.astype(BF16), wo_ref[...],
                                           preferred_element_type=F32)

    def tail_from(xn_ref):
        xn = xn_ref[...]
        xo_ref[...] = xn
        ms = jnp.mean(xn * xn, -1, keepdims=True)
        h2 = xn * lax.rsqrt(ms + EPS) * n2_ref[...]
        h2_ref[...] = _pack_bf16_pairs(h2)
        route = _route(jnp.dot(h2.astype(BF16), wr_ref[...], preferred_element_type=F32))
        tm = route.shape[0]
        lane = lax.broadcasted_iota(jnp.int32, (1, ROUTE_W), 1)
        lane_f = lane.astype(F32)
        row = lax.broadcasted_iota(jnp.int32, (tm, 1), 0)
        hit = [lane_f == route[:, k:k + 1] for k in range(TOP_K)]
        onehot = jnp.where(hit[0], 1.0, 0.0) + jnp.where(hit[1], 1.0, 0.0)
        scan = onehot
        shift = 1
        while shift < tm:
            scan = scan + jnp.where(row >= shift, pltpu.roll(scan, shift, 0), 0.0)
            shift *= 2
        before = scan - onehot + cnt_ref[0:1, :]
        capacity = pos_ref.shape[1] * LANES
        diag = (row & (LANES - 1)) == lane
        first_row = pl.multiple_of((i - 1) * (tm // LANES), tm // LANES)
        for k in range(TOP_K):
            rank = jnp.sum(jnp.where(hit[k], before, 0.0), -1, keepdims=True)
            slot = jnp.where(diag, route[:, k:k + 1] * capacity + rank, 0.0)
            dense = jnp.concatenate(
                [jnp.sum(slot[r * LANES:(r + 1) * LANES], axis=0, keepdims=True)
                 for r in range(tm // LANES)], axis=0)
            pos_ref[k, pl.ds(first_row, tm // LANES), :] = dense.astype(jnp.int32)
        route_ref[...] = route
        cnt = cnt_ref[0:1, :] + jnp.sum(onehot, axis=0, keepdims=True)
        cnt_ref[...] = jnp.broadcast_to(cnt, cnt_ref.shape)
        counts_ref[...] = jnp.broadcast_to(cnt, counts_ref.shape)

    bufs = (xn0_ref, xn1_ref)
    for parity in range(2):
        @pl.when((i > 0) & (i < n_tiles) & (i % 2 == parity))
        def _(parity=parity):
            tail_from(bufs[1 - parity])
            residual_into(bufs[parity])

        @pl.when((i == n_tiles) & (i % 2 == parity))
        def _(parity=parity):
            tail_from(bufs[1 - parity])

    @pl.when(i == 0)
    def _():
        cnt_ref[...] = jnp.zeros_like(cnt_ref)
        residual_into(bufs[0])


def _merge(x2, oa, ob, oc, z2, wa, wb, wc, wo, n2g, w_rg, w_re):
    n, d = x2.shape
    tm = 512
    wr = jnp.concatenate(
        [w_rg, w_re, jnp.zeros((d, ROUTE_W - N_EXPERT_GROUPS - N_EXPERTS), w_rg.dtype)], axis=1)
    gate_w = GATE_W // 4
    gate_col0 = COL_G * IN_TN // gate_w

    n_tiles = n // tm
    last = n_tiles - 1

    def rows(width):
        return pl.BlockSpec((tm, width), lambda i: (jnp.minimum(i, last), 0))

    def lagged_rows(width):
        return pl.BlockSpec((tm, width), lambda i: (jnp.maximum(i - 1, 0), 0))

    def full(shape):
        return pl.BlockSpec(shape, lambda i: (0, 0))

    gate_specs = [pl.BlockSpec((tm, gate_w), lambda i, k=k: (jnp.minimum(i, last), gate_col0 + k))
                  for k in range(4)]
    return pl.pallas_call(
        _merge_kernel,
        out_shape=[jax.ShapeDtypeStruct((n, d), F32),
                   jax.ShapeDtypeStruct((n, d // 2), jnp.int32),
                   jax.ShapeDtypeStruct((n, ROUTE_W), F32),
                   jax.ShapeDtypeStruct((SUBLANES, ROUTE_W), F32),
                   jax.ShapeDtypeStruct((TOP_K, n // LANES, LANES), jnp.int32)],
        grid=(n_tiles + 1,),
        in_specs=[rows(d), rows(A_Q_W), rows(B_WIDTH), rows(C_OUT_W),
                  *gate_specs,
                  full((A_Q_W, d)), full((B_WIDTH, d)), full((C_OUT_W, d)), full((d, d)),
                  full((1, d)), full((d, ROUTE_W))],
        out_specs=[lagged_rows(d), lagged_rows(d // 2), lagged_rows(ROUTE_W),
                   full((SUBLANES, ROUTE_W)),
                   pl.BlockSpec((TOP_K, n // LANES, LANES), lambda i: (0, 0, 0))],
        scratch_shapes=[pltpu.VMEM((tm, d), F32), pltpu.VMEM((tm, d), F32),
                        pltpu.VMEM((SUBLANES, ROUTE_W), F32)],
        compiler_params=_cparams(("arbitrary",)),
        name="merge",
    )(x2, oa, ob, oc, z2, z2, z2, z2,
      wa.astype(BF16), wb.astype(BF16), wc.astype(BF16), wo.astype(BF16),
      n2g.reshape(1, d), wr.astype(BF16))


MOE_TM = 1024
SC_CORES = 2
SC_SUBCORES = 16
SC_WORKERS = SC_CORES * SC_SUBCORES
SC_CHUNK = 128


def _pack_bf16_pairs(x):
    w = x.shape[1] // 2
    lo = pltpu.bitcast(x[:, :w].astype(BF16).astype(F32), jnp.int32)
    hi = pltpu.bitcast(x[:, w:].astype(BF16).astype(F32), jnp.int32)
    return (hi & jnp.int32(-65536)) | lax.shift_right_logical(lo, jnp.int32(16))


def _unpack_bf16_pairs(p):
    lo = pltpu.bitcast(lax.shift_left(p, jnp.int32(16)), F32)
    hi = pltpu.bitcast(p & jnp.int32(-65536), F32)
    return jnp.concatenate([lo, hi], axis=1)


def _sc_gather_rows(table, idx):
    rows, width = idx.shape[0], table.shape[1]
    per_worker = rows // SC_WORKERS
    n_chunks = per_worker // SC_CHUNK
    assert per_worker * SC_WORKERS == rows and n_chunks * SC_CHUNK == per_worker
    mesh = plsc.VectorSubcoreMesh(core_axis_name="c", subcore_axis_name="s",
                                  num_cores=SC_CORES, num_subcores=SC_SUBCORES)

    @functools.partial(
        pl.kernel, mesh=mesh,
        out_type=jax.ShapeDtypeStruct((rows, width), table.dtype),
        scratch_types=[pltpu.VMEM((n_chunks, SC_CHUNK), jnp.int32),
                       pltpu.VMEM((SC_CHUNK, width), table.dtype),
                       pltpu.SemaphoreType.DMA],
        name="sc_gather_rows",
    )
    def gather(table_hbm, idx_hbm, out_hbm, idx_v, rows_v, sem):
        wid = lax.axis_index("s") * SC_CORES + lax.axis_index("c")
        pltpu.sync_copy(idx_hbm.at[wid], idx_v)
        base = wid * per_worker

        @pl.loop(0, n_chunks)
        def _(c):
            pltpu.async_copy(table_hbm.at[idx_v.at[c]], rows_v, sem).wait()
            pltpu.sync_copy(rows_v, out_hbm.at[pl.ds(base + c * SC_CHUNK, SC_CHUNK)])

    return gather(table, idx.reshape(SC_WORKERS, n_chunks, SC_CHUNK))


def _sc_scatter_rows(rows, pos, n_slots):
    n, width = rows.shape
    per_worker = n // SC_WORKERS
    n_chunks = per_worker // SC_CHUNK
    assert per_worker * SC_WORKERS == n and n_chunks * SC_CHUNK == per_worker
    mesh = plsc.VectorSubcoreMesh(core_axis_name="c", subcore_axis_name="s",
                                  num_cores=SC_CORES, num_subcores=SC_SUBCORES)

    @functools.partial(
        pl.kernel, mesh=mesh,
        out_type=jax.ShapeDtypeStruct((n_slots, width), rows.dtype),
        scratch_types=[pltpu.VMEM((TOP_K, n_chunks, SC_CHUNK), jnp.int32),
                       pltpu.VMEM((SC_CHUNK, width), rows.dtype),
                       pltpu.SemaphoreType.DMA],
        name="sc_scatter_rows",
    )
    def scatter(rows_hbm, pos_hbm, out_hbm, pos_v, rows_v, sem):
        wid = lax.axis_index("s") * SC_CORES + lax.axis_index("c")
        for k in range(TOP_K):
            pltpu.sync_copy(pos_hbm.at[k, wid], pos_v.at[k])
        base = wid * per_worker

        @pl.loop(0, n_chunks)
        def _(c):
            pltpu.sync_copy(rows_hbm.at[pl.ds(base + c * SC_CHUNK, SC_CHUNK)], rows_v)
            for k in range(TOP_K):
                pltpu.async_copy(rows_v, out_hbm.at[pos_v.at[k, c]], sem).wait()

    return scatter(rows, pos.reshape(TOP_K, SC_WORKERS, n_chunks, SC_CHUNK))


def _dispatch_plan(counts, capacity, n_steps, expert_base):
    counts = counts[0, :N_EXPERTS].astype(jnp.int32)
    tiles = (counts + MOE_TM - 1) // MOE_TM
    tile_ends = jnp.cumsum(tiles)
    step = jnp.arange(n_steps, dtype=jnp.int32)
    used = step < tile_ends[-1]
    expert = jnp.minimum(jnp.sum((step[:, None] >= tile_ends[None, :]).astype(jnp.int32), axis=1),
                         N_EXPERTS - 1)
    chunk = step - (tile_ends - tiles)[expert]
    rows = jnp.where(used, jnp.clip(counts[expert] - chunk * MOE_TM, 0, MOE_TM), 0)
    first = jnp.concatenate([jnp.ones((1,), jnp.int32),
                             (expert[1:] != expert[:-1]).astype(jnp.int32)])
    block = expert * (capacity // MOE_TM) + chunk
    last_used = jnp.sum(jnp.where(step == tile_ends[-1] - 1, block, 0))
    spare = N_EXPERTS * (capacity // MOE_TM)
    blocks = jnp.stack([jnp.where(used, block, last_used), jnp.where(used, block, spare)])
    return expert + expert_base, rows.astype(jnp.int32), first, blocks.astype(jnp.int32)


def _moe_tile_kernel(te_ref, tr_ref, tf_ref, tb_ref, xs_ref, wg_ref, wu_ref, wd_ref, ys_ref,
                     wg_s, wu_s, wd_s):
    t = pl.program_id(0)

    @pl.when(tf_ref[t] != 0)
    def _():
        wg_s[...] = wg_ref[0].astype(BF16)
        wu_s[...] = wu_ref[0].astype(BF16)
        wd_s[...] = wd_ref[0].astype(BF16)

    @pl.when(tr_ref[t] != 0)
    def _():
        occupied = lax.broadcasted_iota(jnp.int32, (MOE_TM, 1), 0) < tr_ref[t]
        x = jnp.where(occupied, _unpack_bf16_pairs(xs_ref[...]), 0.0).astype(BF16)
        hg = jnp.dot(x, wg_s[...], preferred_element_type=F32)
        hu = jnp.dot(x, wu_s[...], preferred_element_type=F32)
        a = (jax.nn.silu(hg) * hu).astype(BF16)
        ys_ref[...] = _pack_bf16_pairs(jnp.dot(a, wd_s[...], preferred_element_type=F32))

    @pl.when(tr_ref[t] == 0)
    def _():
        ys_ref[...] = jnp.zeros_like(ys_ref)


def _moe_tiles(xs, tile_expert, tile_rows, tile_first, tile_blocks, w_gate, w_up, w_down):
    n_slots, half = xs.shape
    d = 2 * half
    grid_spec = pltpu.PrefetchScalarGridSpec(
        num_scalar_prefetch=4,
        grid=(tile_expert.shape[0],),
        in_specs=[
            pl.BlockSpec((MOE_TM, half), lambda t, te, tr, tf, tb: (tb[0, t], 0)),
            pl.BlockSpec((1, d, D_EXPERT), lambda t, te, tr, tf, tb: (te[t], 0, 0)),
            pl.BlockSpec((1, d, D_EXPERT), lambda t, te, tr, tf, tb: (te[t], 0, 0)),
            pl.BlockSpec((1, D_EXPERT, d), lambda t, te, tr, tf, tb: (te[t], 0, 0)),
        ],
        out_specs=pl.BlockSpec((MOE_TM, half), lambda t, te, tr, tf, tb: (tb[1, t], 0)),
        scratch_shapes=[pltpu.VMEM((d, D_EXPERT), BF16), pltpu.VMEM((d, D_EXPERT), BF16),
                        pltpu.VMEM((D_EXPERT, d), BF16)],
    )
    return pl.pallas_call(
        _moe_tile_kernel,
        out_shape=jax.ShapeDtypeStruct((n_slots, half), jnp.int32),
        grid_spec=grid_spec,
        compiler_params=_cparams(("arbitrary",)),
        name="moe_tiles",
    )(tile_expert, tile_rows, tile_first, tile_blocks, xs, w_gate, w_up, w_down)


def _moe_combine_kernel(x_ref, y0_ref, y1_ref, r_ref, g_ref, o_ref, *, final_norm):
    w0 = r_ref[:, 2:3]
    w1 = r_ref[:, 3:4]
    x = x_ref[...] + w0 * _unpack_bf16_pairs(y0_ref[0]) + w1 * _unpack_bf16_pairs(y1_ref[0])
    if final_norm:
        ms = jnp.mean(x * x, -1, keepdims=True)
        x = x * lax.rsqrt(ms + EPS) * g_ref[...]
    o_ref[...] = x


def _moe_combine(x2, yg, route, norm_g, final_norm):
    n, d = x2.shape
    tm = 1024
    return pl.pallas_call(
        functools.partial(_moe_combine_kernel, final_norm=final_norm),
        out_shape=jax.ShapeDtypeStruct((n, d), F32),
        grid=(n // tm,),
        in_specs=[
            pl.BlockSpec((tm, d), lambda i: (i, 0)),
            pl.BlockSpec((1, tm, d // 2), lambda i: (0, i, 0)),
            pl.BlockSpec((1, tm, d // 2), lambda i: (1, i, 0)),
            pl.BlockSpec((tm, ROUTE_W), lambda i: (i, 0)),
            pl.BlockSpec((1, d), lambda i: (0, 0)),
        ],
        out_specs=pl.BlockSpec((tm, d), lambda i: (i, 0)),
        compiler_params=_cparams(("parallel",)),
        name="moe_combine",
    )(x2, yg, yg, route, norm_g.reshape(1, d))


def _moe(h2p, route, counts, pos, x2, layer, w_gate, w_up, w_down, norm_f_g, final_norm):
    n, d = x2.shape
    n_slots = N_EXPERTS * n + MOE_TM
    n_steps = TOP_K * n // MOE_TM + N_EXPERTS
    tile_expert, tile_rows, tile_first, tile_blocks = _dispatch_plan(
        counts, n, n_steps, layer * N_EXPERTS)
    xs = _sc_scatter_rows(h2p, pos, n_slots)
    ys = _moe_tiles(xs, tile_expert, tile_rows, tile_first, tile_blocks, w_gate, w_up, w_down)
    yg = _sc_gather_rows(ys, pos.reshape(-1)).reshape(TOP_K, n, d // 2)
    return _moe_combine(x2, yg, route, norm_f_g, final_norm)


def kernel(x, positions, norm1_g, w_in, b_in, attn_sinks, sgu_ln_g, sgu_ln_b, w_spatial, b_spatial,
           w_proj_a, w_proj_b, w_proj_c, w_out, norm2_g, w_router_group, w_router_expert,
           w_expert_gate, w_expert_up, w_expert_down, norm_f_g):
    bsz, s, d = x.shape
    depth = w_in.shape[0]
    wg_all = w_expert_gate.reshape(depth * N_EXPERTS, d, D_EXPERT)
    wu_all = w_expert_up.reshape(depth * N_EXPERTS, d, D_EXPERT)
    wd_all = w_expert_down.reshape(depth * N_EXPERTS, D_EXPERT, d)
    assert d == D_MODEL and s % (C_PAIRS[-1][1] * BLOCK) == 0
    tabs = _rope_tables(positions)
    for l in range(depth):
        z = _in_proj(x, norm1_g[l], w_in, b_in.reshape(depth, 1, D_IN), l, tabs)
        oa = _attn_a(z, attn_sinks[l])
        ob = _sgu(z, sgu_ln_g[l], sgu_ln_b[l], w_spatial[l], b_spatial[l])
        oc = _attn_c(z)
        x2, h2, route, counts, pos = _merge(
            x.reshape(bsz * s, d), oa.reshape(bsz * s, A_Q_W), ob.reshape(bsz * s, B_WIDTH),
            oc.reshape(bsz * s, C_OUT_W), z.reshape(bsz * s, D_IN), w_proj_a[l], w_proj_b[l], w_proj_c[l], w_out[l],
            norm2_g[l], w_router_group[l], w_router_expert[l])
        x = _moe(h2, route, counts, pos, x2, l, wg_all, wu_all, wd_all, norm_f_g, l == depth - 1).reshape(bsz, s, d)
    return x
```

```python
import functools

import jax
import jax.numpy as jnp
from jax import lax
from jax.experimental import pallas as pl
from jax.experimental.pallas import tpu as pltpu
from jax.experimental.pallas import tpu_sc as plsc

F32 = jnp.float32
BF16 = jnp.bfloat16

D_MODEL = 1024
HEAD_DIM = 64
ROT_DIM = HEAD_DIM // 4
ROPE_THETA = 500000.0
BLOCK = 128
EPS = 1e-5
NEG_INF = -1e30

A_Q_HEADS = 16
A_KV_HEADS = 4
A_REP = A_Q_HEADS // A_KV_HEADS
A_WINDOW = 128
A_Q_W = A_Q_HEADS * HEAD_DIM
A_KV_W = A_KV_HEADS * HEAD_DIM

B_GROUPS = 12
B_CH = 64
B_WIDTH = B_GROUPS * B_CH
B_CHUNK = 128

C_PAIRS = ((128, 1), (512, 4), (2048, 16))
C_HEADS_PER_GROUP = 4
C_HEADS = C_HEADS_PER_GROUP * len(C_PAIRS)
C_OUT_W = C_HEADS_PER_GROUP * HEAD_DIM

N_BRANCH = 3
GATE_W = N_BRANCH * D_MODEL
D_IN = A_Q_W + 2 * A_KV_W + 2 * B_WIDTH + 3 * C_HEADS * HEAD_DIM + GATE_W

N_EXPERT_GROUPS = 4
EXPERTS_PER_GROUP = 8
N_EXPERTS = N_EXPERT_GROUPS * EXPERTS_PER_GROUP
TOP_K = 2
D_EXPERT = 256

LANES = 128
MXU_W = 256
VMEM_LIMIT = 56 * 1024 * 1024

IN_TN = 256
COL_AQ = 0
COL_AK = A_Q_W // IN_TN
COL_AV = COL_AK + A_KV_W // IN_TN
COL_BU = COL_AV + A_KV_W // IN_TN
COL_BV = COL_BU + B_WIDTH // IN_TN
COL_CQ = COL_BV + B_WIDTH // IN_TN
COL_CK = COL_CQ + C_HEADS * HEAD_DIM // IN_TN
COL_CV = COL_CK + C_HEADS * HEAD_DIM // IN_TN
COL_G = COL_CV + C_HEADS * HEAD_DIM // IN_TN
N_COL_TILES = D_IN // IN_TN

ROUTE_W = LANES
ROUTE_E0 = N_EXPERT_GROUPS
SUBLANES = 8


def _cparams(sem):
    return pltpu.CompilerParams(dimension_semantics=sem, vmem_limit_bytes=VMEM_LIMIT)


ROPE_PACK = LANES // ROT_DIM


def _rope_table_kernel(pos_ref, inv_ref, c_ref, s_ref):
    rows = pos_ref.shape[0]
    lane = lax.broadcasted_iota(jnp.int32, (1, LANES), 1)
    ang = pos_ref[...].astype(F32) * inv_ref[...]
    c = jnp.cos(ang)
    s = jnp.sin(ang)
    s = jnp.where((lane & (ROT_DIM - 1)) < ROT_DIM // 2, -s, s)
    rotary = (lane & (HEAD_DIM - 1)) < ROT_DIM
    first_head = lane < HEAD_DIM
    for q in range(ROPE_PACK):
        shift = (LANES - ROT_DIM * q) % LANES
        for src, dst_ref, fill in ((c, c_ref, 1.0), (s, s_ref, 0.0)):
            mine = pltpu.roll(src, shift, 1) if shift else src
            both = jnp.where(first_head, mine, pltpu.roll(mine, HEAD_DIM, 1))
            dst_ref[pl.ds(q, rows, stride=ROPE_PACK), :] = jnp.where(rotary, both, fill)


def _rope_tables(positions):
    n = positions.size
    inv = ROPE_THETA ** (-jnp.arange(0, ROT_DIM, 2, dtype=F32) / ROT_DIM)
    inv_lane = jnp.tile(inv, LANES // inv.shape[0]).reshape(1, LANES)
    packed = jnp.repeat(positions.reshape(n // ROPE_PACK, ROPE_PACK), ROT_DIM, axis=1)
    tm = 1024
    out_spec = pl.BlockSpec((tm, LANES), lambda i: (i, 0))
    return pl.pallas_call(
        _rope_table_kernel,
        out_shape=[jax.ShapeDtypeStruct((n, LANES), F32)] * 2,
        grid=(n // tm,),
        in_specs=[pl.BlockSpec((tm // ROPE_PACK, LANES), lambda i: (i, 0)),
                  pl.BlockSpec((1, LANES), lambda i: (0, 0))],
        out_specs=[out_spec, out_spec],
        compiler_params=_cparams(("parallel",)),
        name="rope_tables",
    )(packed, inv_lane)


IN_STEP_TILES = 3
IN_STEP_W = IN_STEP_TILES * IN_TN
IN_STEPS = N_COL_TILES // IN_STEP_TILES


def _is_rope_tile(tile):
    return tile < COL_AV or COL_CQ <= tile < COL_CV


_STEP_ROPES = tuple(tuple(_is_rope_tile(step * IN_STEP_TILES + u) for u in range(IN_STEP_TILES))
                    for step in range(IN_STEPS))


def _in_proj_kernel(x_ref, g_ref, w_ref, b_ref, c_ref, s_ref, z_ref, h_ref, acc0_ref, acc1_ref):
    s = pl.program_id(0)
    total = pl.num_programs(0) - 1
    j = lax.rem(s, IN_STEPS)

    @pl.when((j == 0) & (s < total))
    def _():
        x = x_ref[0]
        ms = jnp.mean(x * x, -1, keepdims=True)
        h_ref[...] = (x * lax.rsqrt(ms + EPS) * g_ref[...]).astype(BF16)

    prev = jnp.where(j == 0, IN_STEPS - 1, j - 1)

    def matmul_into(acc_ref):
        acc_ref[...] = jnp.dot(h_ref[...], w_ref[0].astype(BF16),
                               preferred_element_type=F32) + b_ref[0]

    first_half = (lax.broadcasted_iota(jnp.int32, (1, LANES), 1) & (HEAD_DIM - 1)) < ROT_DIM // 2

    def finish(acc_ref, ropes):
        for u, rope in enumerate(ropes):
            cols = slice(u * IN_TN, (u + 1) * IN_TN)
            if not rope:
                z_ref[0, :, cols] = acc_ref[:, cols].astype(BF16)
                continue
            tile = prev * IN_STEP_TILES + u
            is_q = (tile < COL_AK) | ((tile >= COL_CQ) & (tile < COL_CK))
            scale = jnp.where(is_q, HEAD_DIM ** -0.5, 1.0).astype(F32)
            for t in range(u * IN_TN // LANES, (u + 1) * IN_TN // LANES):
                lanes = slice(t * LANES, (t + 1) * LANES)
                a = acc_ref[:, lanes]
                partner = jnp.where(first_half, pltpu.roll(a, LANES - ROT_DIM // 2, 1),
                                    pltpu.roll(a, ROT_DIM // 2, 1))
                r = a * c_ref[...] + partner * s_ref[...]
                z_ref[0, :, lanes] = (r * scale).astype(BF16)

    accs = (acc0_ref, acc1_ref)
    patterns = sorted(set(_STEP_ROPES))
    pattern_id = jnp.int32(0)
    for step, ropes in enumerate(_STEP_ROPES):
        pattern_id = jnp.where(prev == step, patterns.index(ropes), pattern_id)
    assert not any(_STEP_ROPES[-1])
    inner = (s > 0) & (s < total)
    for parity in range(2):
        for pid, ropes in enumerate(patterns):
            @pl.when(inner & (lax.rem(s, 2) == parity) & (pattern_id == pid))
            def _(parity=parity, ropes=ropes):
                matmul_into(accs[parity])
                finish(accs[1 - parity], ropes)

        @pl.when((s == total) & (lax.rem(s, 2) == parity))
        def _(parity=parity):
            finish(accs[1 - parity], _STEP_ROPES[-1])

    @pl.when(s == 0)
    def _():
        matmul_into(accs[0])


def _in_proj(x, g, w_all, b_all, layer, tabs):
    bsz, s, d = x.shape
    cos_tab, sin_tab = tabs
    total = bsz * IN_STEPS

    def row(t):
        return jnp.minimum(t, total - 1) // IN_STEPS

    def col(t):
        return lax.rem(jnp.minimum(t, total - 1), IN_STEPS)

    def lag(t):
        return jnp.maximum(t - 1, 0)

    tab_spec = pl.BlockSpec((s, LANES), lambda t: (row(t), 0), pipeline_mode=pl.Buffered(1))
    return pl.pallas_call(
        _in_proj_kernel,
        out_shape=jax.ShapeDtypeStruct((bsz, s, D_IN), BF16),
        grid=(total + 1,),
        in_specs=[
            pl.BlockSpec((1, s, d), lambda t: (row(t), 0, 0)),
            pl.BlockSpec((1, d), lambda t: (0, 0)),
            pl.BlockSpec((1, d, IN_STEP_W), lambda t: (layer, 0, col(t))),
            pl.BlockSpec((1, 1, IN_STEP_W), lambda t: (layer, 0, col(t))),
            tab_spec, tab_spec,
        ],
        out_specs=pl.BlockSpec((1, s, IN_STEP_W), lambda t: (row(lag(t)), 0, col(lag(t)))),
        scratch_shapes=[pltpu.VMEM((s, d), BF16), pltpu.VMEM((s, IN_STEP_W), F32),
                        pltpu.VMEM((s, IN_STEP_W), F32)],
        compiler_params=_cparams(("arbitrary",)),
        name="in_proj",
    )(x, g.reshape(1, d), w_all, b_all, cos_tab, sin_tab)


STACK = C_HEADS_PER_GROUP * BLOCK
STEP_UNROLL = 4
A_STEP_UNROLL = 8
FAST_STRIDE = 4


def _head_stack(blk):
    seg = lax.broadcasted_iota(jnp.int32, (1, MXU_W), 1) // HEAD_DIM
    rowseg = lax.broadcasted_iota(jnp.int32, (STACK, 1), 0) // BLOCK
    return jnp.where(seg == rowseg, jnp.concatenate([blk] * C_HEADS_PER_GROUP, axis=0),
                     jnp.zeros((), blk.dtype))


def _band_blocks(jobs, kp_ref, vp_ref, has_prev, diag_key, sink_col, want_lse):
    contract = (((1,), (1,)), ((), ()))
    i_loc = lax.broadcasted_iota(jnp.int32, (STACK, 1), 0) & (BLOCK - 1)
    jj = lax.broadcasted_iota(jnp.int32, (1, BLOCK), 1)
    upper = jj > i_loc
    seg = lax.broadcasted_iota(jnp.int32, (1, MXU_W), 1) // HEAD_DIM
    zero_b = jnp.zeros((), BF16)

    def key_rows(blk):
        lo = pl.multiple_of(blk * BLOCK, BLOCK)
        if has_prev:
            return pl.ds(lo, 2 * BLOCK)
        return pl.ds(pl.multiple_of(lo + BLOCK, BLOCK), BLOCK)

    scores = [lax.dot_general(_head_stack(q), kp_ref[key_rows(blk), :], contract,
                              preferred_element_type=F32) for q, blk, _ in jobs]

    def own_lanes(x):
        out = x[0:BLOCK]
        for h in range(1, C_HEADS_PER_GROUP):
            out = jnp.where(seg == h, x[h * BLOCK:(h + 1) * BLOCK], out)
        return out

    probs, stats = [], []
    for s, (_, blk, n) in zip(scores, jobs):
        if has_prev:
            sp = s[:, :BLOCK] + jnp.where(n == 0, NEG_INF, 0.0).astype(F32)
            f = jnp.where(upper, sp, s[:, BLOCK:])
        else:
            f = jnp.where(upper, NEG_INF, s)
        m = jnp.max(f, -1, keepdims=True)
        if diag_key:
            sd = jnp.sum(jnp.where(jj == i_loc, sp, 0.0), -1, keepdims=True)
            m = jnp.maximum(m, sd)
        if sink_col is not None:
            m = jnp.maximum(m, sink_col)
        p = jnp.exp(f - m)
        den = jnp.sum(p, -1, keepdims=True)
        pd = None
        if diag_key:
            pd = jnp.exp(sd - m)
            den = den + pd
        if sink_col is not None:
            den = den + jnp.exp(sink_col - m)
        inv = 1.0 / den
        pn = (p * inv).astype(BF16)
        if has_prev:
            below = zero_b
            if diag_key:
                below = jnp.where(jj == i_loc, pd * inv, 0.0).astype(BF16)
            pn = jnp.concatenate([jnp.where(upper, pn, below), jnp.where(upper, zero_b, pn)],
                                 axis=1)
        probs.append(pn)
        stats.append((m, den))

    outs = []
    for pn, (m, den), (_, blk, _) in zip(probs, stats, jobs):
        o = own_lanes(jnp.dot(pn, vp_ref[key_rows(blk), :], preferred_element_type=F32))
        lse = own_lanes(m + jnp.log(den)) + jnp.zeros((BLOCK, MXU_W), F32) if want_lse else None
        outs.append((o, lse))
    return outs


def _replicate_head(blk, g):
    words = pltpu.bitcast(blk, jnp.uint32)
    tile = words[:, (g // 2) * LANES:(g // 2 + 1) * LANES]
    swapped = pltpu.roll(tile, HEAD_DIM, 1)
    low = lax.broadcasted_iota(jnp.int32, (1, LANES), 1) < HEAD_DIM
    both = jnp.where(low, tile, swapped) if g % 2 == 0 else jnp.where(low, swapped, tile)
    return pltpu.bitcast(jnp.concatenate([both, both], axis=1), BF16)


def _attn_a_kernel(sink_ref, q_ref, k_ref, v_ref, o_ref, kp_ref, vp_ref):
    nb = q_ref.shape[1] // BLOCK
    rowseg = lax.broadcasted_iota(jnp.int32, (STACK, 1), 0) // BLOCK
    kp_ref[pl.ds(0, BLOCK), :] = jnp.zeros((BLOCK, MXU_W), BF16)
    vp_ref[pl.ds(0, BLOCK), :] = jnp.zeros((BLOCK, MXU_W), BF16)
    for g in range(A_KV_HEADS):
        q_cols = slice(g * MXU_W, (g + 1) * MXU_W)

        def build(n, carry):
            rows = pl.ds(pl.multiple_of(n * BLOCK, BLOCK), BLOCK)
            dst = pl.ds(pl.multiple_of((n + 1) * BLOCK, BLOCK), BLOCK)
            kp_ref[dst, :] = _replicate_head(k_ref[0, rows, :], g)
            vp_ref[dst, :] = _replicate_head(v_ref[0, rows, :], g)
            return carry

        lax.fori_loop(0, nb, build, 0, unroll=8)
        sink_col = jnp.zeros((STACK, 1), F32)
        for r in range(A_REP):
            sink_col = jnp.where(rowseg == r, sink_ref[g * A_REP + r], sink_col)

        def step(it, carry):
            blocks = [it * A_STEP_UNROLL + u for u in range(A_STEP_UNROLL)]
            rows = [pl.ds(pl.multiple_of(n * BLOCK, BLOCK), BLOCK) for n in blocks]
            jobs = [(q_ref[0, r, q_cols], n, n) for r, n in zip(rows, blocks)]
            outs = _band_blocks(jobs, kp_ref, vp_ref, True, False, sink_col, False)
            for r, (o, _) in zip(rows, outs):
                o_ref[0, r, q_cols] = o.astype(o_ref.dtype)
            return carry

        lax.fori_loop(0, nb // A_STEP_UNROLL, step, 0)


def _attn_a(z, sinks):
    bsz, s, _ = z.shape
    nb = s // BLOCK
    grid_spec = pltpu.PrefetchScalarGridSpec(
        num_scalar_prefetch=1,
        grid=(bsz,),
        in_specs=[
            pl.BlockSpec((1, s, A_Q_W), lambda bi, sk: (bi, 0, 0)),
            pl.BlockSpec((1, s, A_KV_W), lambda bi, sk: (bi, 0, COL_AK)),
            pl.BlockSpec((1, s, A_KV_W), lambda bi, sk: (bi, 0, COL_AV)),
        ],
        out_specs=pl.BlockSpec((1, s, A_Q_W), lambda bi, sk: (bi, 0, 0)),
        scratch_shapes=[pltpu.VMEM(((nb + 1) * BLOCK, MXU_W), BF16),
                        pltpu.VMEM(((nb + 1) * BLOCK, MXU_W), BF16)],
    )
    return pl.pallas_call(
        _attn_a_kernel,
        out_shape=jax.ShapeDtypeStruct((bsz, s, A_Q_W), BF16),
        grid_spec=grid_spec,
        compiler_params=_cparams(("parallel",)),
        name="attn_a",
    )(sinks, z, z, z)


def _attn_c_kernel(q_ref, k_ref, v_ref, oc_ref, stage_ref, stage2_ref, pq_ref, pk_ref, pv_ref,
                   og_ref, lg_ref):
    s = q_ref.shape[1]
    nblk = s // BLOCK
    halves = MXU_W // LANES
    pk_ref[pl.ds(0, BLOCK), :] = jnp.zeros((BLOCK, MXU_W), BF16)
    pv_ref[pl.ds(0, BLOCK), :] = jnp.zeros((BLOCK, MXU_W), BF16)

    def class_rows(idx, nb, d):
        r = idx // nb
        n = idx - r * nb
        start = r + n * (BLOCK * d)
        rows = pl.ds(start, BLOCK, stride=d) if d > 1 else pl.ds(pl.multiple_of(start, BLOCK), BLOCK)
        return n, rows

    quarter = s // FAST_STRIDE

    def stage(src_ref, cols, d):
        x = src_ref[0, :, cols].astype(F32)
        for t in range(halves):
            stage_ref[t] = x[:, t * LANES:(t + 1) * LANES]
        if d > FAST_STRIDE:
            for t in range(halves):
                for r1 in range(FAST_STRIDE):
                    stage2_ref[t, pl.ds(r1 * quarter, quarter), :] = (
                        stage_ref[t, pl.ds(r1, quarter, stride=FAST_STRIDE), :])

    def staged_block(idx, nb, d):
        if d > FAST_STRIDE:
            assert nb == 1 and d == FAST_STRIDE * FAST_STRIDE
            start = (idx % FAST_STRIDE) * quarter + idx // FAST_STRIDE
            rows = pl.ds(start, BLOCK, stride=FAST_STRIDE)
            parts = [stage2_ref[t, rows, :] for t in range(halves)]
        else:
            _, rows = class_rows(idx, nb, d)
            parts = [stage_ref[t, rows, :] for t in range(halves)]
        return jnp.concatenate(parts, axis=1).astype(BF16)

    for g, (win, d) in enumerate(C_PAIRS):
        cols = slice(g * MXU_W, (g + 1) * MXU_W)
        nb = nblk // d
        has_prev = nb > 1
        assert win // d == BLOCK

        def class_major(dst_ref, offset):
            def build(idx, carry):
                dst = pl.ds(pl.multiple_of((idx + offset) * BLOCK, BLOCK), BLOCK)
                dst_ref[dst, :] = staged_block(idx, nb, d)
                return carry
            return build

        for src_ref, dst_ref, offset in ((k_ref, pk_ref, 1), (v_ref, pv_ref, 1), (q_ref, pq_ref, 0)):
            if d == 1:
                dst_ref[pl.ds(offset * BLOCK, s), :] = src_ref[0, :, cols]
                continue
            stage(src_ref, cols, d)
            lax.fori_loop(0, nblk, class_major(dst_ref, offset), 0, unroll=4)

        def step(it, carry):
            jobs, dsts = [], []
            for u in range(STEP_UNROLL):
                idx = it * STEP_UNROLL + u
                n, rows = class_rows(idx, nb, d)
                here = pl.ds(pl.multiple_of(idx * BLOCK, BLOCK), BLOCK)
                jobs.append((pq_ref[here, :], idx, n))
                dsts.append(rows)
            outs = _band_blocks(jobs, pk_ref, pv_ref, has_prev, has_prev, None, True)
            for rows, (o, lse) in zip(dsts, outs):
                for t in range(halves):
                    og_ref[g, t, rows, :] = o[:, t * LANES:(t + 1) * LANES]
                    lg_ref[g, t, rows, :] = lse[:, t * LANES:(t + 1) * LANES]
            return carry

        lax.fori_loop(0, nblk // STEP_UNROLL, step, 0)

    chunk = 2 * BLOCK

    def combine(i, carry):
        rows = pl.ds(pl.multiple_of(i * chunk, chunk), chunk)
        for t in range(halves):
            ls = [lg_ref[g, t, rows, :] for g in range(len(C_PAIRS))]
            m = jnp.maximum(jnp.maximum(ls[0], ls[1]), ls[2])
            es = [jnp.exp(l - m) for l in ls]
            num = (es[0] * og_ref[0, t, rows, :] + es[1] * og_ref[1, t, rows, :]
                   + es[2] * og_ref[2, t, rows, :])
            oc_ref[0, rows, t * LANES:(t + 1) * LANES] = (
                num / (es[0] + es[1] + es[2])).astype(oc_ref.dtype)
        return carry

    lax.fori_loop(0, s // chunk, combine, 0)


def _attn_c(z):
    bsz, s, _ = z.shape
    width = C_HEADS * HEAD_DIM
    halves = MXU_W // LANES

    def in_spec(col):
        return pl.BlockSpec((1, s, width), lambda bi: (bi, 0, col * IN_TN // width))

    return pl.pallas_call(
        _attn_c_kernel,
        out_shape=jax.ShapeDtypeStruct((bsz, s, C_OUT_W), BF16),
        grid=(bsz,),
        in_specs=[in_spec(COL_CQ), in_spec(COL_CK), in_spec(COL_CV)],
        out_specs=pl.BlockSpec((1, s, C_OUT_W), lambda bi: (bi, 0, 0)),
        scratch_shapes=[
            pltpu.VMEM((halves, s, LANES), F32),
            pltpu.VMEM((halves, s, LANES), F32),
            pltpu.VMEM((s, MXU_W), BF16),
            pltpu.VMEM((s + BLOCK, MXU_W), BF16),
            pltpu.VMEM((s + BLOCK, MXU_W), BF16),
            pltpu.VMEM((len(C_PAIRS), halves, s, LANES), F32),
            pltpu.VMEM((len(C_PAIRS), halves, s, LANES), F32),
        ],
        compiler_params=_cparams(("parallel",)),
        name="attn_c",
    )(z, z, z)


def _sgu_kernel(zu_ref, zv_ref, lng_ref, lnb_ref, ws_ref, bs_ref, o_ref):
    tm = zu_ref.shape[1]
    u = jax.nn.gelu(zu_ref[0]).astype(F32)
    v = jax.nn.gelu(zv_ref[0]).astype(F32)
    mu = jnp.mean(v, -1, keepdims=True)
    var = jnp.mean(jnp.square(v - mu), -1, keepdims=True)
    vn = ((v - mu) * lax.rsqrt(var + EPS) * lng_ref[...] + lnb_ref[...]).astype(BF16)
    causal = (lax.broadcasted_iota(jnp.int32, (B_CHUNK, 1), 0)
              >= lax.broadcasted_iota(jnp.int32, (1, B_CHUNK), 1))
    first = lax.broadcasted_iota(jnp.int32, (1, LANES), 1) < B_CH
    zero = jnp.zeros((), BF16)
    for p in range(B_WIDTH // LANES):
        w0 = jnp.where(causal, ws_ref[2 * p], zero)
        w1 = jnp.where(causal, ws_ref[2 * p + 1], zero)
        cols = slice(p * LANES, (p + 1) * LANES)
        for c in range(tm // B_CHUNK):
            rows = slice(c * B_CHUNK, (c + 1) * B_CHUNK)
            vv = vn[rows, cols]
            sv = jnp.where(first,
                           jnp.dot(w0, vv, preferred_element_type=F32),
                           jnp.dot(w1, vv, preferred_element_type=F32)) + bs_ref[:, cols]
            o_ref[0, rows, cols] = (u[rows, cols] * sv).astype(o_ref.dtype)


def _sgu(z, ln_g, ln_b, w_s, b_s):
    bsz, s, _ = z.shape
    tm = 512
    bias = jnp.repeat(b_s.T, B_CH, axis=1)
    col_u = COL_BU * IN_TN // B_WIDTH
    col_v = COL_BV * IN_TN // B_WIDTH
    return pl.pallas_call(
        _sgu_kernel,
        out_shape=jax.ShapeDtypeStruct((bsz, s, B_WIDTH), BF16),
        grid=(bsz, s // tm),
        in_specs=[
            pl.BlockSpec((1, tm, B_WIDTH), lambda bi, i: (bi, i, col_u)),
            pl.BlockSpec((1, tm, B_WIDTH), lambda bi, i: (bi, i, col_v)),
            pl.BlockSpec((1, B_WIDTH), lambda bi, i: (0, 0)),
            pl.BlockSpec((1, B_WIDTH), lambda bi, i: (0, 0)),
            pl.BlockSpec((B_GROUPS, B_CHUNK, B_CHUNK), lambda bi, i: (0, 0, 0)),
            pl.BlockSpec((B_CHUNK, B_WIDTH), lambda bi, i: (0, 0)),
        ],
        out_specs=pl.BlockSpec((1, tm, B_WIDTH), lambda bi, i: (bi, i, 0)),
        compiler_params=_cparams(("parallel", "parallel")),
        name="sgu",
    )(z, z, ln_g.reshape(1, B_WIDTH), ln_b.reshape(1, B_WIDTH), w_s.astype(BF16), bias)


def _route(logits):
    lane = lax.broadcasted_iota(jnp.int32, (1, ROUTE_W), 1)
    lane_f = lane.astype(F32)
    is_g = lane < N_EXPERT_GROUPS
    lg = jnp.where(is_g, logits, NEG_INF)
    mg = jnp.max(lg, -1, keepdims=True)
    pg_top = 1.0 / jnp.sum(jnp.where(is_g, jnp.exp(lg - mg), 0.0), -1, keepdims=True)
    g_idx = jnp.min(jnp.where(lg == mg, lane_f, float(ROUTE_W)), -1, keepdims=True)
    e_group = ((lane - ROUTE_E0) // EXPERTS_PER_GROUP).astype(F32)
    in_grp = (lane >= ROUTE_E0) & (lane < ROUTE_E0 + N_EXPERTS) & (e_group == g_idx)
    le = jnp.where(in_grp, logits, NEG_INF)
    m1 = jnp.max(le, -1, keepdims=True)
    i1 = jnp.min(jnp.where(le == m1, lane_f, float(ROUTE_W)), -1, keepdims=True)
    le2 = jnp.where(lane_f == i1, NEG_INF, le)
    m2 = jnp.max(le2, -1, keepdims=True)
    i2 = jnp.min(jnp.where(le2 == m2, lane_f, float(ROUTE_W)), -1, keepdims=True)
    t = jnp.exp(m2 - m1)
    w1 = pg_top / (1.0 + t)
    w2 = w1 * t
    picks = (i1 - ROUTE_E0, i2 - ROUTE_E0, w1, w2)
    out = jnp.zeros(logits.shape, F32)
    for k, val in enumerate(picks):
        out = jnp.where(lane == k, val, out)
    return out


def _merge_kernel(x_ref, oa_ref, ob_ref, oc_ref,
                  g0_ref, g1_ref, g2_ref, g3_ref, wa_ref, wb_ref, wc_ref, wo_ref, n2_ref, wr_ref,
                  xo_ref, h2_ref, route_ref, counts_ref, pos_ref, xn0_ref, xn1_ref, cnt_ref):
    i = pl.program_id(0)
    n_tiles = pl.num_programs(0) - 1

    def residual_into(xn_ref):
        zg = jnp.concatenate([g0_ref[...], g1_ref[...], g2_ref[...], g3_ref[...]],
                             axis=1).astype(F32)
        gates = jax.nn.sigmoid(zg)
        pa = jnp.dot(oa_ref[...], wa_ref[...], preferred_element_type=F32)
        pb = jnp.dot(ob_ref[...], wb_ref[...], preferred_element_type=F32)
        pc = jnp.dot(oc_ref[...], wc_ref[...], preferred_element_type=F32)
        merged = (gates[:, :D_MODEL] * pa + gates[:, D_MODEL:2 * D_MODEL] * pb
                  + gates[:, 2 * D_MODEL:] * pc)
        xn_ref[...] = x_ref[...] + jnp.dot(merged.astype(BF16), wo_ref[...],
                                           preferred_element_type=F32)

    def tail_from(xn_ref):
        xn = xn_ref[...]
        xo_ref[...] = xn
        ms = jnp.mean(xn * xn, -1, keepdims=True)
        h2 = xn * lax.rsqrt(ms + EPS) * n2_ref[...]
        h2_ref[...] = _pack_bf16_pairs(h2)
        route = _route(jnp.dot(h2.astype(BF16), wr_ref[...], preferred_element_type=F32))
        tm = route.shape[0]
        lane = lax.broadcasted_iota(jnp.int32, (1, ROUTE_W), 1)
        lane_f = lane.astype(F32)
        row = lax.broadcasted_iota(jnp.int32, (tm, 1), 0)
        hit = [lane_f == route[:, k:k + 1] for k in range(TOP_K)]
        onehot = jnp.where(hit[0], 1.0, 0.0) + jnp.where(hit[1], 1.0, 0.0)
        scan = onehot
        shift = 1
        while shift < tm:
            scan = scan + jnp.where(row >= shift, pltpu.roll(scan, shift, 0), 0.0)
            shift *= 2
        before = scan - onehot + cnt_ref[0:1, :]
        capacity = pos_ref.shape[1] * LANES
        diag = (row & (LANES - 1)) == lane
        first_row = pl.multiple_of((i - 1) * (tm // LANES), tm // LANES)
        for k in range(TOP_K):
            rank = jnp.sum(jnp.where(hit[k], before, 0.0), -1, keepdims=True)
            slot = jnp.where(diag, route[:, k:k + 1] * capacity + rank, 0.0)
            dense = jnp.concatenate(
                [jnp.sum(slot[r * LANES:(r + 1) * LANES], axis=0, keepdims=True)
                 for r in range(tm // LANES)], axis=0)
            pos_ref[k, pl.ds(first_row, tm // LANES), :] = dense.astype(jnp.int32)
        route_ref[...] = route
        cnt = cnt_ref[0:1, :] + jnp.sum(onehot, axis=0, keepdims=True)
        cnt_ref[...] = jnp.broadcast_to(cnt, cnt_ref.shape)
        counts_ref[...] = jnp.broadcast_to(cnt, counts_ref.shape)

    bufs = (xn0_ref, xn1_ref)
    for parity in range(2):
        @pl.when((i > 0) & (i < n_tiles) & (i % 2 == parity))
        def _(parity=parity):
            tail_from(bufs[1 - parity])
            residual_into(bufs[parity])

        @pl.when((i == n_tiles) & (i % 2 == parity))
        def _(parity=parity):
            tail_from(bufs[1 - parity])

    @pl.when(i == 0)
    def _():
        cnt_ref[...] = jnp.zeros_like(cnt_ref)
        residual_into(bufs[0])


def _merge(x2, oa, ob, oc, z2, wa, wb, wc, wo, n2g, w_rg, w_re):
    n, d = x2.shape
    tm = 512
    wr = jnp.concatenate(
        [w_rg, w_re, jnp.zeros((d, ROUTE_W - N_EXPERT_GROUPS - N_EXPERTS), w_rg.dtype)], axis=1)
    gate_w = GATE_W // 4
    gate_col0 = COL_G * IN_TN // gate_w

    n_tiles = n // tm
    last = n_tiles - 1

    def rows(width):
        return pl.BlockSpec((tm, width), lambda i: (jnp.minimum(i, last), 0))

    def lagged_rows(width):
        return pl.BlockSpec((tm, width), lambda i: (jnp.maximum(i - 1, 0), 0))

    def full(shape):
        return pl.BlockSpec(shape, lambda i: (0, 0))

    gate_specs = [pl.BlockSpec((tm, gate_w), lambda i, k=k: (jnp.minimum(i, last), gate_col0 + k))
                  for k in range(4)]
    return pl.pallas_call(
        _merge_kernel,
        out_shape=[jax.ShapeDtypeStruct((n, d), F32),
                   jax.ShapeDtypeStruct((n, d // 2), jnp.int32),
                   jax.ShapeDtypeStruct((n, ROUTE_W), F32),
                   jax.ShapeDtypeStruct((SUBLANES, ROUTE_W), F32),
                   jax.ShapeDtypeStruct((TOP_K, n // LANES, LANES), jnp.int32)],
        grid=(n_tiles + 1,),
        in_specs=[rows(d), rows(A_Q_W), rows(B_WIDTH), rows(C_OUT_W),
                  *gate_specs,
                  full((A_Q_W, d)), full((B_WIDTH, d)), full((C_OUT_W, d)), full((d, d)),
                  full((1, d)), full((d, ROUTE_W))],
        out_specs=[lagged_rows(d), lagged_rows(d // 2), lagged_rows(ROUTE_W),
                   full((SUBLANES, ROUTE_W)),
                   pl.BlockSpec((TOP_K, n // LANES, LANES), lambda i: (0, 0, 0))],
        scratch_shapes=[pltpu.VMEM((tm, d), F32), pltpu.VMEM((tm, d), F32),
                        pltpu.VMEM((SUBLANES, ROUTE_W), F32)],
        compiler_params=_cparams(("arbitrary",)),
        name="merge",
    )(x2, oa, ob, oc, z2, z2, z2, z2,
      wa.astype(BF16), wb.astype(BF16), wc.astype(BF16), wo.astype(BF16),
      n2g.reshape(1, d), wr.astype(BF16))


MOE_TM = 1024
SC_CORES = 2
SC_SUBCORES = 16
SC_WORKERS = SC_CORES * SC_SUBCORES
SC_CHUNK = 128


def _pack_bf16_pairs(x):
    w = x.shape[1] // 2
    lo = pltpu.bitcast(x[:, :w].astype(BF16).astype(F32), jnp.int32)
    hi = pltpu.bitcast(x[:, w:].astype(BF16).astype(F32), jnp.int32)
    return (hi & jnp.int32(-65536)) | lax.shift_right_logical(lo, jnp.int32(16))


def _unpack_bf16_pairs(p):
    lo = pltpu.bitcast(lax.shift_left(p, jnp.int32(16)), F32)
    hi = pltpu.bitcast(p & jnp.int32(-65536), F32)
    return jnp.concatenate([lo, hi], axis=1)


def _sc_gather_rows(table, idx):
    rows, width = idx.shape[0], table.shape[1]
    per_worker = rows // SC_WORKERS
    n_chunks = per_worker // SC_CHUNK
    assert per_worker * SC_WORKERS == rows and n_chunks * SC_CHUNK == per_worker
    mesh = plsc.VectorSubcoreMesh(core_axis_name="c", subcore_axis_name="s",
                                  num_cores=SC_CORES, num_subcores=SC_SUBCORES)

    @functools.partial(
        pl.kernel, mesh=mesh,
        out_type=jax.ShapeDtypeStruct((rows, width), table.dtype),
        scratch_types=[pltpu.VMEM((n_chunks, SC_CHUNK), jnp.int32),
                       pltpu.VMEM((SC_CHUNK, width), table.dtype),
                       pltpu.SemaphoreType.DMA],
        name="sc_gather_rows",
    )
    def gather(table_hbm, idx_hbm, out_hbm, idx_v, rows_v, sem):
        wid = lax.axis_index("s") * SC_CORES + lax.axis_index("c")
        pltpu.sync_copy(idx_hbm.at[wid], idx_v)
        base = wid * per_worker

        @pl.loop(0, n_chunks)
        def _(c):
            pltpu.async_copy(table_hbm.at[idx_v.at[c]], rows_v, sem).wait()
            pltpu.sync_copy(rows_v, out_hbm.at[pl.ds(base + c * SC_CHUNK, SC_CHUNK)])

    return gather(table, idx.reshape(SC_WORKERS, n_chunks, SC_CHUNK))


def _sc_scatter_rows(rows, pos, n_slots):
    n, width = rows.shape
    per_worker = n // SC_WORKERS
    n_chunks = per_worker // SC_CHUNK
    assert per_worker * SC_WORKERS == n and n_chunks * SC_CHUNK == per_worker
    mesh = plsc.VectorSubcoreMesh(core_axis_name="c", subcore_axis_name="s",
                                  num_cores=SC_CORES, num_subcores=SC_SUBCORES)

    @functools.partial(
        pl.kernel, mesh=mesh,
        out_type=jax.ShapeDtypeStruct((n_slots, width), rows.dtype),
        scratch_types=[pltpu.VMEM((TOP_K, n_chunks, SC_CHUNK), jnp.int32),
                       pltpu.VMEM((SC_CHUNK, width), rows.dtype),
                       pltpu.SemaphoreType.DMA],
        name="sc_scatter_rows",
    )
    def scatter(rows_hbm, pos_hbm, out_hbm, pos_v, rows_v, sem):
        wid = lax.axis_index("s") * SC_CORES + lax.axis_index("c")
        for k in range(TOP_K):
            pltpu.sync_copy(pos_hbm.at[k, wid], pos_v.at[k])
        base = wid * per_worker

        @pl.loop(0, n_chunks)
        def _(c):
            pltpu.sync_copy(rows_hbm.at[pl.ds(base + c * SC_CHUNK, SC_CHUNK)], rows_v)
            for k in range(TOP_K):
                pltpu.async_copy(rows_v, out_hbm.at[pos_v.at[k, c]], sem).wait()

    return scatter(rows, pos.reshape(TOP_K, SC_WORKERS, n_chunks, SC_CHUNK))


def _dispatch_plan(counts, capacity, n_steps, expert_base):
    counts = counts[0, :N_EXPERTS].astype(jnp.int32)
    tiles = (counts + MOE_TM - 1) // MOE_TM
    tile_ends = jnp.cumsum(tiles)
    step = jnp.arange(n_steps, dtype=jnp.int32)
    used = step < tile_ends[-1]
    expert = jnp.minimum(jnp.sum((step[:, None] >= tile_ends[None, :]).astype(jnp.int32), axis=1),
                         N_EXPERTS - 1)
    chunk = step - (tile_ends - tiles)[expert]
    rows = jnp.where(used, jnp.clip(counts[expert] - chunk * MOE_TM, 0, MOE_TM), 0)
    first = jnp.concatenate([jnp.ones((1,), jnp.int32),
                             (expert[1:] != expert[:-1]).astype(jnp.int32)])
    block = expert * (capacity // MOE_TM) + chunk
    last_used = jnp.sum(jnp.where(step == tile_ends[-1] - 1, block, 0))
    spare = N_EXPERTS * (capacity // MOE_TM)
    blocks = jnp.stack([jnp.where(used, block, last_used), jnp.where(used, block, spare)])
    return expert + expert_base, rows.astype(jnp.int32), first, blocks.astype(jnp.int32)


def _moe_tile_kernel(te_ref, tr_ref, tf_ref, tb_ref, xs_ref, wg_ref, wu_ref, wd_ref, ys_ref,
                     wg_s, wu_s, wd_s):
    t = pl.program_id(0)

    @pl.when(tf_ref[t] != 0)
    def _():
        wg_s[...] = wg_ref[0].astype(BF16)
        wu_s[...] = wu_ref[0].astype(BF16)
        wd_s[...] = wd_ref[0].astype(BF16)

    @pl.when(tr_ref[t] != 0)
    def _():
        occupied = lax.broadcasted_iota(jnp.int32, (MOE_TM, 1), 0) < tr_ref[t]
        x = jnp.where(occupied, _unpack_bf16_pairs(xs_ref[...]), 0.0).astype(BF16)
        hg = jnp.dot(x, wg_s[...], preferred_element_type=F32)
        hu = jnp.dot(x, wu_s[...], preferred_element_type=F32)
        a = (jax.nn.silu(hg) * hu).astype(BF16)
        ys_ref[...] = _pack_bf16_pairs(jnp.dot(a, wd_s[...], preferred_element_type=F32))

    @pl.when(tr_ref[t] == 0)
    def _():
        ys_ref[...] = jnp.zeros_like(ys_ref)


def _moe_tiles(xs, tile_expert, tile_rows, tile_first, tile_blocks, w_gate, w_up, w_down):
    n_slots, half = xs.shape
    d = 2 * half
    grid_spec = pltpu.PrefetchScalarGridSpec(
        num_scalar_prefetch=4,
        grid=(tile_expert.shape[0],),
        in_specs=[
            pl.BlockSpec((MOE_TM, half), lambda t, te, tr, tf, tb: (tb[0, t], 0)),
            pl.BlockSpec((1, d, D_EXPERT), lambda t, te, tr, tf, tb: (te[t], 0, 0)),
            pl.BlockSpec((1, d, D_EXPERT), lambda t, te, tr, tf, tb: (te[t], 0, 0)),
            pl.BlockSpec((1, D_EXPERT, d), lambda t, te, tr, tf, tb: (te[t], 0, 0)),
        ],
        out_specs=pl.BlockSpec((MOE_TM, half), lambda t, te, tr, tf, tb: (tb[1, t], 0)),
        scratch_shapes=[pltpu.VMEM((d, D_EXPERT), BF16), pltpu.VMEM((d, D_EXPERT), BF16),
                        pltpu.VMEM((D_EXPERT, d), BF16)],
    )
    return pl.pallas_call(
        _moe_tile_kernel,
        out_shape=jax.ShapeDtypeStruct((n_slots, half), jnp.int32),
        grid_spec=grid_spec,
        compiler_params=_cparams(("arbitrary",)),
        name="moe_tiles",
    )(tile_expert, tile_rows, tile_first, tile_blocks, xs, w_gate, w_up, w_down)


def _moe_combine_kernel(x_ref, y0_ref, y1_ref, r_ref, g_ref, o_ref, *, final_norm):
    w0 = r_ref[:, 2:3]
    w1 = r_ref[:, 3:4]
    x = x_ref[...] + w0 * _unpack_bf16_pairs(y0_ref[0]) + w1 * _unpack_bf16_pairs(y1_ref[0])
    if final_norm:
        ms = jnp.mean(x * x, -1, keepdims=True)
        x = x * lax.rsqrt(ms + EPS) * g_ref[...]
    o_ref[...] = x


def _moe_combine(x2, yg, route, norm_g, final_norm):
    n, d = x2.shape
    tm = 1024
    return pl.pallas_call(
        functools.partial(_moe_combine_kernel, final_norm=final_norm),
        out_shape=jax.ShapeDtypeStruct((n, d), F32),
        grid=(n // tm,),
        in_specs=[
            pl.BlockSpec((tm, d), lambda i: (i, 0)),
            pl.BlockSpec((1, tm, d // 2), lambda i: (0, i, 0)),
            pl.BlockSpec((1, tm, d // 2), lambda i: (1, i, 0)),
            pl.BlockSpec((tm, ROUTE_W), lambda i: (i, 0)),
            pl.BlockSpec((1, d), lambda i: (0, 0)),
        ],
        out_specs=pl.BlockSpec((tm, d), lambda i: (i, 0)),
        compiler_params=_cparams(("parallel",)),
        name="moe_combine",
    )(x2, yg, yg, route, norm_g.reshape(1, d))


def _moe(h2p, route, counts, pos, x2, layer, w_gate, w_up, w_down, norm_f_g, final_norm):
    n, d = x2.shape
    n_slots = N_EXPERTS * n + MOE_TM
    n_steps = TOP_K * n // MOE_TM + N_EXPERTS
    tile_expert, tile_rows, tile_first, tile_blocks = _dispatch_plan(
        counts, n, n_steps, layer * N_EXPERTS)
    xs = _sc_scatter_rows(h2p, pos, n_slots)
    ys = _moe_tiles(xs, tile_expert, tile_rows, tile_first, tile_blocks, w_gate, w_up, w_down)
    yg = _sc_gather_rows(ys, pos.reshape(-1)).reshape(TOP_K, n, d // 2)
    return _moe_combine(x2, yg, route, norm_f_g, final_norm)


def kernel(x, positions, norm1_g, w_in, b_in, attn_sinks, sgu_ln_g, sgu_ln_b, w_spatial, b_spatial,
           w_proj_a, w_proj_b, w_proj_c, w_out, norm2_g, w_router_group, w_router_expert,
           w_expert_gate, w_expert_up, w_expert_down, norm_f_g):
    bsz, s, d = x.shape
    depth = w_in.shape[0]
    wg_all = w_expert_gate.reshape(depth * N_EXPERTS, d, D_EXPERT)
    wu_all = w_expert_up.reshape(depth * N_EXPERTS, d, D_EXPERT)
    wd_all = w_expert_down.reshape(depth * N_EXPERTS, D_EXPERT, d)
    assert d == D_MODEL and s % (C_PAIRS[-1][1] * BLOCK) == 0
    tabs = _rope_tables(positions)
    for l in range(depth):
        z = _in_proj(x, norm1_g[l], w_in, b_in.reshape(depth, 1, D_IN), l, tabs)
        oa = _attn_a(z, attn_sinks[l])
        ob = _sgu(z, sgu_ln_g[l], sgu_ln_b[l], w_spatial[l], b_spatial[l])
        oc = _attn_c(z)
        x2, h2, route, counts, pos = _merge(
            x.reshape(bsz * s, d), oa.reshape(bsz * s, A_Q_W), ob.reshape(bsz * s, B_WIDTH),
            oc.reshape(bsz * s, C_OUT_W), z.reshape(bsz * s, D_IN), w_proj_a[l], w_proj_b[l], w_proj_c[l], w_out[l],
            norm2_g[l], w_router_group[l], w_router_expert[l])
        x = _moe(h2, route, counts, pos, x2, l, wg_all, wu_all, wd_all, norm_f_g, l == depth - 1).reshape(bsz, s, d)
    return x
```

```python
import functools

import jax
import jax.numpy as jnp
from jax import lax
from jax.experimental import pallas as pl
from jax.experimental.pallas import tpu as pltpu
from jax.experimental.pallas import tpu_sc as plsc

F32 = jnp.float32
BF16 = jnp.bfloat16

D_MODEL = 1024
HEAD_DIM = 64
ROT_DIM = HEAD_DIM // 4
ROPE_THETA = 500000.0
BLOCK = 128
EPS = 1e-5
NEG_INF = -1e30

A_Q_HEADS = 16
A_KV_HEADS = 4
A_REP = A_Q_HEADS // A_KV_HEADS
A_WINDOW = 128
A_Q_W = A_Q_HEADS * HEAD_DIM
A_KV_W = A_KV_HEADS * HEAD_DIM

B_GROUPS = 12
B_CH = 64
B_WIDTH = B_GROUPS * B_CH
B_CHUNK = 128

C_PAIRS = ((128, 1), (512, 4), (2048, 16))
C_HEADS_PER_GROUP = 4
C_HEADS = C_HEADS_PER_GROUP * len(C_PAIRS)
C_OUT_W = C_HEADS_PER_GROUP * HEAD_DIM

N_BRANCH = 3
GATE_W = N_BRANCH * D_MODEL
D_IN = A_Q_W + 2 * A_KV_W + 2 * B_WIDTH + 3 * C_HEADS * HEAD_DIM + GATE_W

N_EXPERT_GROUPS = 4
EXPERTS_PER_GROUP = 8
N_EXPERTS = N_EXPERT_GROUPS * EXPERTS_PER_GROUP
TOP_K = 2
D_EXPERT = 256

LANES = 128
MXU_W = 256
VMEM_LIMIT = 56 * 1024 * 1024

IN_TN = 256
COL_AQ = 0
COL_AK = A_Q_W // IN_TN
COL_AV = COL_AK + A_KV_W // IN_TN
COL_BU = COL_AV + A_KV_W // IN_TN
COL_BV = COL_BU + B_WIDTH // IN_TN
COL_CQ = COL_BV + B_WIDTH // IN_TN
COL_CK = COL_CQ + C_HEADS * HEAD_DIM // IN_TN
COL_CV = COL_CK + C_HEADS * HEAD_DIM // IN_TN
COL_G = COL_CV + C_HEADS * HEAD_DIM // IN_TN
N_COL_TILES = D_IN // IN_TN

ROUTE_W = LANES
ROUTE_E0 = N_EXPERT_GROUPS
SUBLANES = 8


def _cparams(sem):
    return pltpu.CompilerParams(dimension_semantics=sem, vmem_limit_bytes=VMEM_LIMIT)


ROPE_PACK = LANES // ROT_DIM


def _rope_table_kernel(pos_ref, inv_ref, c_ref, s_ref):
    rows = pos_ref.shape[0]
    lane = lax.broadcasted_iota(jnp.int32, (1, LANES), 1)
    ang = pos_ref[...].astype(F32) * inv_ref[...]
    c = jnp.cos(ang)
    s = jnp.sin(ang)
    s = jnp.where((lane & (ROT_DIM - 1)) < ROT_DIM // 2, -s, s)
    rotary = (lane & (HEAD_DIM - 1)) < ROT_DIM
    first_head = lane < HEAD_DIM
    for q in range(ROPE_PACK):
        shift = (LANES - ROT_DIM * q) % LANES
        for src, dst_ref, fill in ((c, c_ref, 1.0), (s, s_ref, 0.0)):
            mine = pltpu.roll(src, shift, 1) if shift else src
            both = jnp.where(first_head, mine, pltpu.roll(mine, HEAD_DIM, 1))
            dst_ref[pl.ds(q, rows, stride=ROPE_PACK), :] = jnp.where(rotary, both, fill)


def _rope_tables(positions):
    n = positions.size
    inv = ROPE_THETA ** (-jnp.arange(0, ROT_DIM, 2, dtype=F32) / ROT_DIM)
    inv_lane = jnp.tile(inv, LANES // inv.shape[0]).reshape(1, LANES)
    packed = jnp.repeat(positions.reshape(n // ROPE_PACK, ROPE_PACK), ROT_DIM, axis=1)
    tm = 1024
    out_spec = pl.BlockSpec((tm, LANES), lambda i: (i, 0))
    return pl.pallas_call(
        _rope_table_kernel,
        out_shape=[jax.ShapeDtypeStruct((n, LANES), F32)] * 2,
        grid=(n // tm,),
        in_specs=[pl.BlockSpec((tm // ROPE_PACK, LANES), lambda i: (i, 0)),
                  pl.BlockSpec((1, LANES), lambda i: (0, 0))],
        out_specs=[out_spec, out_spec],
        compiler_params=_cparams(("parallel",)),
        name="rope_tables",
    )(packed, inv_lane)


IN_STEP_TILES = 3
IN_STEP_W = IN_STEP_TILES * IN_TN
IN_STEPS = N_COL_TILES // IN_STEP_TILES


def _is_rope_tile(tile):
    return tile < COL_AV or COL_CQ <= tile < COL_CV


_STEP_ROPES = tuple(tuple(_is_rope_tile(step * IN_STEP_TILES + u) for u in range(IN_STEP_TILES))
                    for step in range(IN_STEPS))


def _in_proj_kernel(x_ref, g_ref, w_ref, b_ref, c_ref, s_ref, z_ref, h_ref, acc0_ref, acc1_ref):
    s = pl.program_id(0)
    total = pl.num_programs(0) - 1
    j = lax.rem(s, IN_STEPS)

    @pl.when((j == 0) & (s < total))
    def _():
        x = x_ref[0]
        ms = jnp.mean(x * x, -1, keepdims=True)
        h_ref[...] = (x * lax.rsqrt(ms + EPS) * g_ref[...]).astype(BF16)

    prev = jnp.where(j == 0, IN_STEPS - 1, j - 1)

    def matmul_into(acc_ref):
        acc_ref[...] = jnp.dot(h_ref[...], w_ref[0].astype(BF16),
                               preferred_element_type=F32) + b_ref[0]

    first_half = (lax.broadcasted_iota(jnp.int32, (1, LANES), 1) & (HEAD_DIM - 1)) < ROT_DIM // 2

    def finish(acc_ref, ropes):
        for u, rope in enumerate(ropes):
            cols = slice(u * IN_TN, (u + 1) * IN_TN)
            if not rope:
                z_ref[0, :, cols] = acc_ref[:, cols].astype(BF16)
                continue
            tile = prev * IN_STEP_TILES + u
            is_q = (tile < COL_AK) | ((tile >= COL_CQ) & (tile < COL_CK))
            scale = jnp.where(is_q, HEAD_DIM ** -0.5, 1.0).astype(F32)
            for t in range(u * IN_TN // LANES, (u + 1) * IN_TN // LANES):
                lanes = slice(t * LANES, (t + 1) * LANES)
                a = acc_ref[:, lanes]
                partner = jnp.where(first_half, pltpu.roll(a, LANES - ROT_DIM // 2, 1),
                                    pltpu.roll(a, ROT_DIM // 2, 1))
                r = a * c_ref[...] + partner * s_ref[...]
                z_ref[0, :, lanes] = (r * scale).astype(BF16)

    accs = (acc0_ref, acc1_ref)
    patterns = sorted(set(_STEP_ROPES))
    pattern_id = jnp.int32(0)
    for step, ropes in enumerate(_STEP_ROPES):
        pattern_id = jnp.where(prev == step, patterns.index(ropes), pattern_id)
    assert not any(_STEP_ROPES[-1])
    inner = (s > 0) & (s < total)
    for parity in range(2):
        for pid, ropes in enumerate(patterns):
            @pl.when(inner & (lax.rem(s, 2) == parity) & (pattern_id == pid))
            def _(parity=parity, ropes=ropes):
                matmul_into(accs[parity])
                finish(accs[1 - parity], ropes)

        @pl.when((s == total) & (lax.rem(s, 2) == parity))
        def _(parity=parity):
            finish(accs[1 - parity], _STEP_ROPES[-1])

    @pl.when(s == 0)
    def _():
        matmul_into(accs[0])


def _in_proj(x, g, w_all, b_all, layer, tabs):
    bsz, s, d = x.shape
    cos_tab, sin_tab = tabs
    total = bsz * IN_STEPS

    def row(t):
        return jnp.minimum(t, total - 1) // IN_STEPS

    def col(t):
        return lax.rem(jnp.minimum(t, total - 1), IN_STEPS)

    def lag(t):
        return jnp.maximum(t - 1, 0)

    tab_spec = pl.BlockSpec((s, LANES), lambda t: (row(t), 0), pipeline_mode=pl.Buffered(1))
    return pl.pallas_call(
        _in_proj_kernel,
        out_shape=jax.ShapeDtypeStruct((bsz, s, D_IN), BF16),
        grid=(total + 1,),
        in_specs=[
            pl.BlockSpec((1, s, d), lambda t: (row(t), 0, 0)),
            pl.BlockSpec((1, d), lambda t: (0, 0)),
            pl.BlockSpec((1, d, IN_STEP_W), lambda t: (layer, 0, col(t))),
            pl.BlockSpec((1, 1, IN_STEP_W), lambda t: (layer, 0, col(t))),
            tab_spec, tab_spec,
        ],
        out_specs=pl.BlockSpec((1, s, IN_STEP_W), lambda t: (row(lag(t)), 0, col(lag(t)))),
        scratch_shapes=[pltpu.VMEM((s, d), BF16), pltpu.VMEM((s, IN_STEP_W), F32),
                        pltpu.VMEM((s, IN_STEP_W), F32)],
        compiler_params=_cparams(("arbitrary",)),
        name="in_proj",
    )(x, g.reshape(1, d), w_all, b_all, cos_tab, sin_tab)


STACK = C_HEADS_PER_GROUP * BLOCK
STEP_UNROLL = 4
A_STEP_UNROLL = 8
FAST_STRIDE = 4


def _head_stack(blk):
    seg = lax.broadcasted_iota(jnp.int32, (1, MXU_W), 1) // HEAD_DIM
    rowseg = lax.broadcasted_iota(jnp.int32, (STACK, 1), 0) // BLOCK
    return jnp.where(seg == rowseg, jnp.concatenate([blk] * C_HEADS_PER_GROUP, axis=0),
                     jnp.zeros((), blk.dtype))


def _band_blocks(jobs, kp_ref, vp_ref, has_prev, diag_key, sink_col, want_lse):
    contract = (((1,), (1,)), ((), ()))
    i_loc = lax.broadcasted_iota(jnp.int32, (STACK, 1), 0) & (BLOCK - 1)
    jj = lax.broadcasted_iota(jnp.int32, (1, BLOCK), 1)
    upper = jj > i_loc
    seg = lax.broadcasted_iota(jnp.int32, (1, MXU_W), 1) // HEAD_DIM
    zero_b = jnp.zeros((), BF16)

    def key_rows(blk):
        lo = pl.multiple_of(blk * BLOCK, BLOCK)
        if has_prev:
            return pl.ds(lo, 2 * BLOCK)
        return pl.ds(pl.multiple_of(lo + BLOCK, BLOCK), BLOCK)

    scores = [lax.dot_general(_head_stack(q), kp_ref[key_rows(blk), :], contract,
                              preferred_element_type=F32) for q, blk, _ in jobs]

    def own_lanes(x):
        out = x[0:BLOCK]
        for h in range(1, C_HEADS_PER_GROUP):
            out = jnp.where(seg == h, x[h * BLOCK:(h + 1) * BLOCK], out)
        return out

    probs, stats = [], []
    for s, (_, blk, n) in zip(scores, jobs):
        if has_prev:
            sp = s[:, :BLOCK] + jnp.where(n == 0, NEG_INF, 0.0).astype(F32)
            f = jnp.where(upper, sp, s[:, BLOCK:])
        else:
            f = jnp.where(upper, NEG_INF, s)
        m = jnp.max(f, -1, keepdims=True)
        if diag_key:
            sd = jnp.sum(jnp.where(jj == i_loc, sp, 0.0), -1, keepdims=True)
            m = jnp.maximum(m, sd)
        if sink_col is not None:
            m = jnp.maximum(m, sink_col)
        p = jnp.exp(f - m)
        den = jnp.sum(p, -1, keepdims=True)
        pd = None
        if diag_key:
            pd = jnp.exp(sd - m)
            den = den + pd
        if sink_col is not None:
            den = den + jnp.exp(sink_col - m)
        inv = 1.0 / den
        pn = (p * inv).astype(BF16)
        if has_prev:
            below = zero_b
            if diag_key:
                below = jnp.where(jj == i_loc, pd * inv, 0.0).astype(BF16)
            pn = jnp.concatenate([jnp.where(upper, pn, below), jnp.where(upper, zero_b, pn)],
                                 axis=1)
        probs.append(pn)
        stats.append((m, den))

    outs = []
    for pn, (m, den), (_, blk, _) in zip(probs, stats, jobs):
        o = own_lanes(jnp.dot(pn, vp_ref[key_rows(blk), :], preferred_element_type=F32))
        lse = own_lanes(m + jnp.log(den)) + jnp.zeros((BLOCK, MXU_W), F32) if want_lse else None
        outs.append((o, lse))
    return outs


def _replicate_head(blk, g):
    words = pltpu.bitcast(blk, jnp.uint32)
    tile = words[:, (g // 2) * LANES:(g // 2 + 1) * LANES]
    swapped = pltpu.roll(tile, HEAD_DIM, 1)
    low = lax.broadcasted_iota(jnp.int32, (1, LANES), 1) < HEAD_DIM
    both = jnp.where(low, tile, swapped) if g % 2 == 0 else jnp.where(low, swapped, tile)
    return pltpu.bitcast(jnp.concatenate([both, both], axis=1), BF16)


def _attn_a_kernel(sink_ref, q_ref, k_ref, v_ref, o_ref, kp_ref, vp_ref):
    nb = q_ref.shape[1] // BLOCK
    rowseg = lax.broadcasted_iota(jnp.int32, (STACK, 1), 0) // BLOCK
    kp_ref[pl.ds(0, BLOCK), :] = jnp.zeros((BLOCK, MXU_W), BF16)
    vp_ref[pl.ds(0, BLOCK), :] = jnp.zeros((BLOCK, MXU_W), BF16)
    for g in range(A_KV_HEADS):
        q_cols = slice(g * MXU_W, (g + 1) * MXU_W)

        def build(n, carry):
            rows = pl.ds(pl.multiple_of(n * BLOCK, BLOCK), BLOCK)
            dst = pl.ds(pl.multiple_of((n + 1) * BLOCK, BLOCK), BLOCK)
            kp_ref[dst, :] = _replicate_head(k_ref[0, rows, :], g)
            vp_ref[dst, :] = _replicate_head(v_ref[0, rows, :], g)
            return carry

        lax.fori_loop(0, nb, build, 0, unroll=8)
        sink_col = jnp.zeros((STACK, 1), F32)
        for r in range(A_REP):
            sink_col = jnp.where(rowseg == r, sink_ref[g * A_REP + r], sink_col)

        def step(it, carry):
            blocks = [it * A_STEP_UNROLL + u for u in range(A_STEP_UNROLL)]
            rows = [pl.ds(pl.multiple_of(n * BLOCK, BLOCK), BLOCK) for n in blocks]
            jobs = [(q_ref[0, r, q_cols], n, n) for r, n in zip(rows, blocks)]
            outs = _band_blocks(jobs, kp_ref, vp_ref, True, False, sink_col, False)
            for r, (o, _) in zip(rows, outs):
                o_ref[0, r, q_cols] = o.astype(o_ref.dtype)
            return carry

        lax.fori_loop(0, nb // A_STEP_UNROLL, step, 0)


def _attn_a(z, sinks):
    bsz, s, _ = z.shape
    nb = s // BLOCK
    grid_spec = pltpu.PrefetchScalarGridSpec(
        num_scalar_prefetch=1,
        grid=(bsz,),
        in_specs=[
            pl.BlockSpec((1, s, A_Q_W), lambda bi, sk: (bi, 0, 0)),
            pl.BlockSpec((1, s, A_KV_W), lambda bi, sk: (bi, 0, COL_AK)),
            pl.BlockSpec((1, s, A_KV_W), lambda bi, sk: (bi, 0, COL_AV)),
        ],
        out_specs=pl.BlockSpec((1, s, A_Q_W), lambda bi, sk: (bi, 0, 0)),
        scratch_shapes=[pltpu.VMEM(((nb + 1) * BLOCK, MXU_W), BF16),
                        pltpu.VMEM(((nb + 1) * BLOCK, MXU_W), BF16)],
    )
    return pl.pallas_call(
        _attn_a_kernel,
        out_shape=jax.ShapeDtypeStruct((bsz, s, A_Q_W), BF16),
        grid_spec=grid_spec,
        compiler_params=_cparams(("parallel",)),
        name="attn_a",
    )(sinks, z, z, z)


def _attn_c_kernel(q_ref, k_ref, v_ref, oc_ref, stage_ref, stage2_ref, pq_ref, pk_ref, pv_ref,
                   og_ref, lg_ref):
    s = q_ref.shape[1]
    nblk = s // BLOCK
    halves = MXU_W // LANES
    pk_ref[pl.ds(0, BLOCK), :] = jnp.zeros((BLOCK, MXU_W), BF16)
    pv_ref[pl.ds(0, BLOCK), :] = jnp.zeros((BLOCK, MXU_W), BF16)

    def class_rows(idx, nb, d):
        r = idx // nb
        n = idx - r * nb
        start = r + n * (BLOCK * d)
        rows = pl.ds(start, BLOCK, stride=d) if d > 1 else pl.ds(pl.multiple_of(start, BLOCK), BLOCK)
        return n, rows

    quarter = s // FAST_STRIDE

    def stage(src_ref, cols, d):
        x = src_ref[0, :, cols].astype(F32)
        for t in range(halves):
            stage_ref[t] = x[:, t * LANES:(t + 1) * LANES]
        if d > FAST_STRIDE:
            for t in range(halves):
                for r1 in range(FAST_STRIDE):
                    stage2_ref[t, pl.ds(r1 * quarter, quarter), :] = (
                        stage_ref[t, pl.ds(r1, quarter, stride=FAST_STRIDE), :])

    def staged_block(idx, nb, d):
        if d > FAST_STRIDE:
            assert nb == 1 and d == FAST_STRIDE * FAST_STRIDE
            start = (idx % FAST_STRIDE) * quarter + idx // FAST_STRIDE
            rows = pl.ds(start, BLOCK, stride=FAST_STRIDE)
            parts = [stage2_ref[t, rows, :] for t in range(halves)]
        else:
            _, rows = class_rows(idx, nb, d)
            parts = [stage_ref[t, rows, :] for t in range(halves)]
        return jnp.concatenate(parts, axis=1).astype(BF16)

    for g, (win, d) in enumerate(C_PAIRS):
        cols = slice(g * MXU_W, (g + 1) * MXU_W)
        nb = nblk // d
        has_prev = nb > 1
        assert win // d == BLOCK

        def class_major(dst_ref, offset):
            def build(idx, carry):
                dst = pl.ds(pl.multiple_of((idx + offset) * BLOCK, BLOCK), BLOCK)
                dst_ref[dst, :] = staged_block(idx, nb, d)
                return carry
            return build

        for src_ref, dst_ref, offset in ((k_ref, pk_ref, 1), (v_ref, pv_ref, 1), (q_ref, pq_ref, 0)):
            if d == 1:
                dst_ref[pl.ds(offset * BLOCK, s), :] = src_ref[0, :, cols]
                continue
            stage(src_ref, cols, d)
            lax.fori_loop(0, nblk, class_major(dst_ref, offset), 0, unroll=4)

        def step(it, carry):
            jobs, dsts = [], []
            for u in range(STEP_UNROLL):
                idx = it * STEP_UNROLL + u
                n, rows = class_rows(idx, nb, d)
                here = pl.ds(pl.multiple_of(idx * BLOCK, BLOCK), BLOCK)
                jobs.append((pq_ref[here, :], idx, n))
                dsts.append(rows)
            outs = _band_blocks(jobs, pk_ref, pv_ref, has_prev, has_prev, None, True)
            for rows, (o, lse) in zip(dsts, outs):
                for t in range(halves):
                    og_ref[g, t, rows, :] = o[:, t * LANES:(t + 1) * LANES]
                    lg_ref[g, t, rows, :] = lse[:, t * LANES:(t + 1) * LANES]
            return carry

        lax.fori_loop(0, nblk // STEP_UNROLL, step, 0)

    chunk = 2 * BLOCK

    def combine(i, carry):
        rows = pl.ds(pl.multiple_of(i * chunk, chunk), chunk)
        for t in range(halves):
            ls = [lg_ref[g, t, rows, :] for g in range(len(C_PAIRS))]
            m = jnp.maximum(jnp.maximum(ls[0], ls[1]), ls[2])
            es = [jnp.exp(l - m) for l in ls]
            num = (es[0] * og_ref[0, t, rows, :] + es[1] * og_ref[1, t, rows, :]
                   + es[2] * og_ref[2, t, rows, :])
            oc_ref[0, rows, t * LANES:(t + 1) * LANES] = (
                num / (es[0] + es[1] + es[2])).astype(oc_ref.dtype)
        return carry

    lax.fori_loop(0, s // chunk, combine, 0)


def _attn_c(z):
    bsz, s, _ = z.shape
    width = C_HEADS * HEAD_DIM
    halves = MXU_W // LANES

    def in_spec(col):
        return pl.BlockSpec((1, s, width), lambda bi: (bi, 0, col * IN_TN // width))

    return pl.pallas_call(
        _attn_c_kernel,
        out_shape=jax.ShapeDtypeStruct((bsz, s, C_OUT_W), BF16),
        grid=(bsz,),
        in_specs=[in_spec(COL_CQ), in_spec(COL_CK), in_spec(COL_CV)],
        out_specs=pl.BlockSpec((1, s, C_OUT_W), lambda bi: (bi, 0, 0)),
        scratch_shapes=[
            pltpu.VMEM((halves, s, LANES), F32),
            pltpu.VMEM((halves, s, LANES), F32),
            pltpu.VMEM((s, MXU_W), BF16),
            pltpu.VMEM((s + BLOCK, MXU_W), BF16),
            pltpu.VMEM((s + BLOCK, MXU_W), BF16),
            pltpu.VMEM((len(C_PAIRS), halves, s, LANES), F32),
            pltpu.VMEM((len(C_PAIRS), halves, s, LANES), F32),
        ],
        compiler_params=_cparams(("parallel",)),
        name="attn_c",
    )(z, z, z)


def _sgu_kernel(zu_ref, zv_ref, lng_ref, lnb_ref, ws_ref, bs_ref, o_ref):
    tm = zu_ref.shape[1]
    u = jax.nn.gelu(zu_ref[0]).astype(F32)
    v = jax.nn.gelu(zv_ref[0]).astype(F32)
    mu = jnp.mean(v, -1, keepdims=True)
    var = jnp.mean(jnp.square(v - mu), -1, keepdims=True)
    vn = ((v - mu) * lax.rsqrt(var + EPS) * lng_ref[...] + lnb_ref[...]).astype(BF16)
    causal = (lax.broadcasted_iota(jnp.int32, (B_CHUNK, 1), 0)
              >= lax.broadcasted_iota(jnp.int32, (1, B_CHUNK), 1))
    first = lax.broadcasted_iota(jnp.int32, (1, LANES), 1) < B_CH
    zero = jnp.zeros((), BF16)
    for p in range(B_WIDTH // LANES):
        w0 = jnp.where(causal, ws_ref[2 * p], zero)
        w1 = jnp.where(causal, ws_ref[2 * p + 1], zero)
        cols = slice(p * LANES, (p + 1) * LANES)
        for c in range(tm // B_CHUNK):
            rows = slice(c * B_CHUNK, (c + 1) * B_CHUNK)
            vv = vn[rows, cols]
            sv = jnp.where(first,
                           jnp.dot(w0, vv, preferred_element_type=F32),
                           jnp.dot(w1, vv, preferred_element_type=F32)) + bs_ref[:, cols]
            o_ref[0, rows, cols] = (u[rows, cols] * sv).astype(o_ref.dtype)


def _sgu(z, ln_g, ln_b, w_s, b_s):
    bsz, s, _ = z.shape
    tm = 512
    bias = jnp.repeat(b_s.T, B_CH, axis=1)
    col_u = COL_BU * IN_TN // B_WIDTH
    col_v = COL_BV * IN_TN // B_WIDTH
    return pl.pallas_call(
        _sgu_kernel,
        out_shape=jax.ShapeDtypeStruct((bsz, s, B_WIDTH), BF16),
        grid=(bsz, s // tm),
        in_specs=[
            pl.BlockSpec((1, tm, B_WIDTH), lambda bi, i: (bi, i, col_u)),
            pl.BlockSpec((1, tm, B_WIDTH), lambda bi, i: (bi, i, col_v)),
            pl.BlockSpec((1, B_WIDTH), lambda bi, i: (0, 0)),
            pl.BlockSpec((1, B_WIDTH), lambda bi, i: (0, 0)),
            pl.BlockSpec((B_GROUPS, B_CHUNK, B_CHUNK), lambda bi, i: (0, 0, 0)),
            pl.BlockSpec((B_CHUNK, B_WIDTH), lambda bi, i: (0, 0)),
        ],
        out_specs=pl.BlockSpec((1, tm, B_WIDTH), lambda bi, i: (bi, i, 0)),
        compiler_params=_cparams(("parallel", "parallel")),
        name="sgu",
    )(z, z, ln_g.reshape(1, B_WIDTH), ln_b.reshape(1, B_WIDTH), w_s.astype(BF16), bias)


def _route(logits):
    lane = lax.broadcasted_iota(jnp.int32, (1, ROUTE_W), 1)
    lane_f = lane.astype(F32)
    is_g = lane < N_EXPERT_GROUPS
    lg = jnp.where(is_g, logits, NEG_INF)
    mg = jnp.max(lg, -1, keepdims=True)
    pg_top = 1.0 / jnp.sum(jnp.where(is_g, jnp.exp(lg - mg), 0.0), -1, keepdims=True)
    g_idx = jnp.min(jnp.where(lg == mg, lane_f, float(ROUTE_W)), -1, keepdims=True)
    e_group = ((lane - ROUTE_E0) // EXPERTS_PER_GROUP).astype(F32)
    in_grp = (lane >= ROUTE_E0) & (lane < ROUTE_E0 + N_EXPERTS) & (e_group == g_idx)
    le = jnp.where(in_grp, logits, NEG_INF)
    m1 = jnp.max(le, -1, keepdims=True)
    i1 = jnp.min(jnp.where(le == m1, lane_f, float(ROUTE_W)), -1, keepdims=True)
    le2 = jnp.where(lane_f == i1, NEG_INF, le)
    m2 = jnp.max(le2, -1, keepdims=True)
    i2 = jnp.min(jnp.where(le2 == m2, lane_f, float(ROUTE_W)), -1, keepdims=True)
    t = jnp.exp(m2 - m1)
    w1 = pg_top / (1.0 + t)
    w2 = w1 * t
    picks = (i1 - ROUTE_E0, i2 - ROUTE_E0, w1, w2)
    out = jnp.zeros(logits.shape, F32)
    for k, val in enumerate(picks):
        out = jnp.where(lane == k, val, out)
    return out


def _merge_kernel(x_ref, oa_ref, ob_ref, oc_ref,
                  g0_ref, g1_ref, g2_ref, g3_ref, wa_ref, wb_ref, wc_ref, wo_ref, n2_ref, wr_ref,
                  xo_ref, h2_ref, route_ref, counts_ref, pos_ref, xn0_ref, xn1_ref, cnt_ref):
    i = pl.program_id(0)
    n_tiles = pl.num_programs(0) - 1

    def residual_into(xn_ref):
        zg = jnp.concatenate([g0_ref[...], g1_ref[...], g2_ref[...], g3_ref[...]],
                             axis=1).astype(F32)
        gates = jax.nn.sigmoid(zg)
        halves = (slice(0, D_MODEL // 2), slice(D_MODEL // 2, D_MODEL))
        branch = []
        for cols in halves:
            branch.append((jnp.dot(oa_ref[...], wa_ref[:, cols], preferred_element_type=F32),
                           jnp.dot(ob_ref[...], wb_ref[:, cols], preferred_element_type=F32),
                           jnp.dot(oc_ref[...], wc_ref[:, cols], preferred_element_type=F32)))
        xn = x_ref[...]
        for cols, (pa, pb, pc) in zip(halves, branch):
            ga, gb, gc = (gates[:, k * D_MODEL:(k + 1) * D_MODEL][:, cols] for k in range(N_BRANCH))
            merged = ga * pa + gb * pb + gc * pc
            xn = xn + jnp.dot(merged.astype(BF16), wo_ref[cols, :], preferred_element_type=F32)
        xn_ref[...] = xn

    def tail_from(xn_ref):
        xn = xn_ref[...]
        xo_ref[...] = xn
        ms = jnp.mean(xn * xn, -1, keepdims=True)
        h2 = xn * lax.rsqrt(ms + EPS) * n2_ref[...]
        h2_ref[...] = _pack_bf16_pairs(h2)
        route = _route(jnp.dot(h2.astype(BF16), wr_ref[...], preferred_element_type=F32))
        tm = route.shape[0]
        lane = lax.broadcasted_iota(jnp.int32, (1, ROUTE_W), 1)
        lane_f = lane.astype(F32)
        row = lax.broadcasted_iota(jnp.int32, (tm, 1), 0)
        hit = [lane_f == route[:, k:k + 1] for k in range(TOP_K)]
        onehot = jnp.where(hit[0], 1.0, 0.0) + jnp.where(hit[1], 1.0, 0.0)
        scan = onehot
        shift = 1
        while shift < tm:
            scan = scan + jnp.where(row >= shift, pltpu.roll(scan, shift, 0), 0.0)
            shift *= 2
        before = scan - onehot + cnt_ref[0:1, :]
        capacity = pos_ref.shape[1] * LANES
        diag = (row & (LANES - 1)) == lane
        first_row = pl.multiple_of((i - 1) * (tm // LANES), tm // LANES)
        for k in range(TOP_K):
            rank = jnp.sum(jnp.where(hit[k], before, 0.0), -1, keepdims=True)
            slot = jnp.where(diag, route[:, k:k + 1] * capacity + rank, 0.0)
            dense = jnp.concatenate(
                [jnp.sum(slot[r * LANES:(r + 1) * LANES], axis=0, keepdims=True)
                 for r in range(tm // LANES)], axis=0)
            pos_ref[k, pl.ds(first_row, tm // LANES), :] = dense.astype(jnp.int32)
        route_ref[...] = route
        cnt = cnt_ref[0:1, :] + jnp.sum(onehot, axis=0, keepdims=True)
        cnt_ref[...] = jnp.broadcast_to(cnt, cnt_ref.shape)
        counts_ref[...] = jnp.broadcast_to(cnt, counts_ref.shape)

    bufs = (xn0_ref, xn1_ref)
    for parity in range(2):
        @pl.when((i > 0) & (i < n_tiles) & (i % 2 == parity))
        def _(parity=parity):
            tail_from(bufs[1 - parity])
            residual_into(bufs[parity])

        @pl.when((i == n_tiles) & (i % 2 == parity))
        def _(parity=parity):
            tail_from(bufs[1 - parity])

    @pl.when(i == 0)
    def _():
        cnt_ref[...] = jnp.zeros_like(cnt_ref)
        residual_into(bufs[0])


def _merge(x2, oa, ob, oc, z2, wa, wb, wc, wo, n2g, w_rg, w_re):
    n, d = x2.shape
    tm = 512
    wr = jnp.concatenate(
        [w_rg, w_re, jnp.zeros((d, ROUTE_W - N_EXPERT_GROUPS - N_EXPERTS), w_rg.dtype)], axis=1)
    gate_w = GATE_W // 4
    gate_col0 = COL_G * IN_TN // gate_w

    n_tiles = n // tm
    last = n_tiles - 1

    def rows(width):
        return pl.BlockSpec((tm, width), lambda i: (jnp.minimum(i, last), 0))

    def lagged_rows(width):
        return pl.BlockSpec((tm, width), lambda i: (jnp.maximum(i - 1, 0), 0))

    def full(shape):
        return pl.BlockSpec(shape, lambda i: (0, 0))

    gate_specs = [pl.BlockSpec((tm, gate_w), lambda i, k=k: (jnp.minimum(i, last), gate_col0 + k))
                  for k in range(4)]
    return pl.pallas_call(
        _merge_kernel,
        out_shape=[jax.ShapeDtypeStruct((n, d), F32),
                   jax.ShapeDtypeStruct((n, d // 2), jnp.int32),
                   jax.ShapeDtypeStruct((n, ROUTE_W), F32),
                   jax.ShapeDtypeStruct((SUBLANES, ROUTE_W), F32),
                   jax.ShapeDtypeStruct((TOP_K, n // LANES, LANES), jnp.int32)],
        grid=(n_tiles + 1,),
        in_specs=[rows(d), rows(A_Q_W), rows(B_WIDTH), rows(C_OUT_W),
                  *gate_specs,
                  full((A_Q_W, d)), full((B_WIDTH, d)), full((C_OUT_W, d)), full((d, d)),
                  full((1, d)), full((d, ROUTE_W))],
        out_specs=[lagged_rows(d), lagged_rows(d // 2), lagged_rows(ROUTE_W),
                   full((SUBLANES, ROUTE_W)),
                   pl.BlockSpec((TOP_K, n // LANES, LANES), lambda i: (0, 0, 0))],
        scratch_shapes=[pltpu.VMEM((tm, d), F32), pltpu.VMEM((tm, d), F32),
                        pltpu.VMEM((SUBLANES, ROUTE_W), F32)],
        compiler_params=_cparams(("arbitrary",)),
        name="merge",
    )(x2, oa, ob, oc, z2, z2, z2, z2,
      wa.astype(BF16), wb.astype(BF16), wc.astype(BF16), wo.astype(BF16),
      n2g.reshape(1, d), wr.astype(BF16))


MOE_TM = 1024
SC_CORES = 2
SC_SUBCORES = 16
SC_WORKERS = SC_CORES * SC_SUBCORES
SC_CHUNK = 128


def _pack_bf16_pairs(x):
    w = x.shape[1] // 2
    lo = pltpu.bitcast(x[:, :w].astype(BF16).astype(F32), jnp.int32)
    hi = pltpu.bitcast(x[:, w:].astype(BF16).astype(F32), jnp.int32)
    return (hi & jnp.int32(-65536)) | lax.shift_right_logical(lo, jnp.int32(16))


def _unpack_bf16_pairs(p):
    lo = pltpu.bitcast(lax.shift_left(p, jnp.int32(16)), F32)
    hi = pltpu.bitcast(p & jnp.int32(-65536), F32)
    return jnp.concatenate([lo, hi], axis=1)


def _sc_gather_rows(table, idx):
    rows, width = idx.shape[0], table.shape[1]
    per_worker = rows // SC_WORKERS
    n_chunks = per_worker // SC_CHUNK
    assert per_worker * SC_WORKERS == rows and n_chunks * SC_CHUNK == per_worker
    mesh = plsc.VectorSubcoreMesh(core_axis_name="c", subcore_axis_name="s",
                                  num_cores=SC_CORES, num_subcores=SC_SUBCORES)

    @functools.partial(
        pl.kernel, mesh=mesh,
        out_type=jax.ShapeDtypeStruct((rows, width), table.dtype),
        scratch_types=[pltpu.VMEM((n_chunks, SC_CHUNK), jnp.int32),
                       pltpu.VMEM((SC_CHUNK, width), table.dtype),
                       pltpu.SemaphoreType.DMA],
        name="sc_gather_rows",
    )
    def gather(table_hbm, idx_hbm, out_hbm, idx_v, rows_v, sem):
        wid = lax.axis_index("s") * SC_CORES + lax.axis_index("c")
        pltpu.sync_copy(idx_hbm.at[wid], idx_v)
        base = wid * per_worker

        @pl.loop(0, n_chunks)
        def _(c):
            pltpu.async_copy(table_hbm.at[idx_v.at[c]], rows_v, sem).wait()
            pltpu.sync_copy(rows_v, out_hbm.at[pl.ds(base + c * SC_CHUNK, SC_CHUNK)])

    return gather(table, idx.reshape(SC_WORKERS, n_chunks, SC_CHUNK))


def _sc_scatter_rows(rows, pos, n_slots):
    n, width = rows.shape
    per_worker = n // SC_WORKERS
    n_chunks = per_worker // SC_CHUNK
    assert per_worker * SC_WORKERS == n and n_chunks * SC_CHUNK == per_worker
    mesh = plsc.VectorSubcoreMesh(core_axis_name="c", subcore_axis_name="s",
                                  num_cores=SC_CORES, num_subcores=SC_SUBCORES)

    @functools.partial(
        pl.kernel, mesh=mesh,
        out_type=jax.ShapeDtypeStruct((n_slots, width), rows.dtype),
        scratch_types=[pltpu.VMEM((TOP_K, n_chunks, SC_CHUNK), jnp.int32),
                       pltpu.VMEM((SC_CHUNK, width), rows.dtype),
                       pltpu.SemaphoreType.DMA],
        name="sc_scatter_rows",
    )
    def scatter(rows_hbm, pos_hbm, out_hbm, pos_v, rows_v, sem):
        wid = lax.axis_index("s") * SC_CORES + lax.axis_index("c")
        for k in range(TOP_K):
            pltpu.sync_copy(pos_hbm.at[k, wid], pos_v.at[k])
        base = wid * per_worker

        @pl.loop(0, n_chunks)
        def _(c):
            pltpu.sync_copy(rows_hbm.at[pl.ds(base + c * SC_CHUNK, SC_CHUNK)], rows_v)
            for k in range(TOP_K):
                pltpu.async_copy(rows_v, out_hbm.at[pos_v.at[k, c]], sem).wait()

    return scatter(rows, pos.reshape(TOP_K, SC_WORKERS, n_chunks, SC_CHUNK))


def _dispatch_plan(counts, capacity, n_steps, expert_base):
    counts = counts[0, :N_EXPERTS].astype(jnp.int32)
    tiles = (counts + MOE_TM - 1) // MOE_TM
    tile_ends = jnp.cumsum(tiles)
    step = jnp.arange(n_steps, dtype=jnp.int32)
    used = step < tile_ends[-1]
    expert = jnp.minimum(jnp.sum((step[:, None] >= tile_ends[None, :]).astype(jnp.int32), axis=1),
                         N_EXPERTS - 1)
    chunk = step - (tile_ends - tiles)[expert]
    rows = jnp.where(used, jnp.clip(counts[expert] - chunk * MOE_TM, 0, MOE_TM), 0)
    first = jnp.concatenate([jnp.ones((1,), jnp.int32),
                             (expert[1:] != expert[:-1]).astype(jnp.int32)])
    block = expert * (capacity // MOE_TM) + chunk
    last_used = jnp.sum(jnp.where(step == tile_ends[-1] - 1, block, 0))
    spare = N_EXPERTS * (capacity // MOE_TM)
    blocks = jnp.stack([jnp.where(used, block, last_used), jnp.where(used, block, spare)])
    return expert + expert_base, rows.astype(jnp.int32), first, blocks.astype(jnp.int32)


def _moe_tile_kernel(te_ref, tr_ref, tf_ref, tb_ref, xs_ref, wg_ref, wu_ref, wd_ref, ys_ref,
                     wg_s, wu_s, wd_s):
    t = pl.program_id(0)

    @pl.when(tf_ref[t] != 0)
    def _():
        wg_s[...] = wg_ref[0].astype(BF16)
        wu_s[...] = wu_ref[0].astype(BF16)
        wd_s[...] = wd_ref[0].astype(BF16)

    @pl.when(tr_ref[t] != 0)
    def _():
        occupied = lax.broadcasted_iota(jnp.int32, (MOE_TM, 1), 0) < tr_ref[t]
        x = jnp.where(occupied, _unpack_bf16_pairs(xs_ref[...]), 0.0).astype(BF16)
        hg = jnp.dot(x, wg_s[...], preferred_element_type=F32)
        hu = jnp.dot(x, wu_s[...], preferred_element_type=F32)
        a = (jax.nn.silu(hg) * hu).astype(BF16)
        ys_ref[...] = _pack_bf16_pairs(jnp.dot(a, wd_s[...], preferred_element_type=F32))

    @pl.when(tr_ref[t] == 0)
    def _():
        ys_ref[...] = jnp.zeros_like(ys_ref)


def _moe_tiles(xs, tile_expert, tile_rows, tile_first, tile_blocks, w_gate, w_up, w_down):
    n_slots, half = xs.shape
    d = 2 * half
    grid_spec = pltpu.PrefetchScalarGridSpec(
        num_scalar_prefetch=4,
        grid=(tile_expert.shape[0],),
        in_specs=[
            pl.BlockSpec((MOE_TM, half), lambda t, te, tr, tf, tb: (tb[0, t], 0)),
            pl.BlockSpec((1, d, D_EXPERT), lambda t, te, tr, tf, tb: (te[t], 0, 0)),
            pl.BlockSpec((1, d, D_EXPERT), lambda t, te, tr, tf, tb: (te[t], 0, 0)),
            pl.BlockSpec((1, D_EXPERT, d), lambda t, te, tr, tf, tb: (te[t], 0, 0)),
        ],
        out_specs=pl.BlockSpec((MOE_TM, half), lambda t, te, tr, tf, tb: (tb[1, t], 0)),
        scratch_shapes=[pltpu.VMEM((d, D_EXPERT), BF16), pltpu.VMEM((d, D_EXPERT), BF16),
                        pltpu.VMEM((D_EXPERT, d), BF16)],
    )
    return pl.pallas_call(
        _moe_tile_kernel,
        out_shape=jax.ShapeDtypeStruct((n_slots, half), jnp.int32),
        grid_spec=grid_spec,
        compiler_params=_cparams(("arbitrary",)),
        name="moe_tiles",
    )(tile_expert, tile_rows, tile_first, tile_blocks, xs, w_gate, w_up, w_down)


def _moe_combine_kernel(x_ref, y0_ref, y1_ref, r_ref, g_ref, o_ref, *, final_norm):
    w0 = r_ref[:, 2:3]
    w1 = r_ref[:, 3:4]
    x = x_ref[...] + w0 * _unpack_bf16_pairs(y0_ref[0]) + w1 * _unpack_bf16_pairs(y1_ref[0])
    if final_norm:
        ms = jnp.mean(x * x, -1, keepdims=True)
        x = x * lax.rsqrt(ms + EPS) * g_ref[...]
    o_ref[...] = x


def _moe_combine(x2, yg, route, norm_g, final_norm):
    n, d = x2.shape
    tm = 1024
    return pl.pallas_call(
        functools.partial(_moe_combine_kernel, final_norm=final_norm),
        out_shape=jax.ShapeDtypeStruct((n, d), F32),
        grid=(n // tm,),
        in_specs=[
            pl.BlockSpec((tm, d), lambda i: (i, 0)),
            pl.BlockSpec((1, tm, d // 2), lambda i: (0, i, 0)),
            pl.BlockSpec((1, tm, d // 2), lambda i: (1, i, 0)),
            pl.BlockSpec((tm, ROUTE_W), lambda i: (i, 0)),
            pl.BlockSpec((1, d), lambda i: (0, 0)),
        ],
        out_specs=pl.BlockSpec((tm, d), lambda i: (i, 0)),
        compiler_params=_cparams(("parallel",)),
        name="moe_combine",
    )(x2, yg, yg, route, norm_g.reshape(1, d))


def _moe(h2p, route, counts, pos, x2, layer, w_gate, w_up, w_down, norm_f_g, final_norm):
    n, d = x2.shape
    n_slots = N_EXPERTS * n + MOE_TM
    n_steps = TOP_K * n // MOE_TM + N_EXPERTS
    tile_expert, tile_rows, tile_first, tile_blocks = _dispatch_plan(
        counts, n, n_steps, layer * N_EXPERTS)
    xs = _sc_scatter_rows(h2p, pos, n_slots)
    ys = _moe_tiles(xs, tile_expert, tile_rows, tile_first, tile_blocks, w_gate, w_up, w_down)
    yg = _sc_gather_rows(ys, pos.reshape(-1)).reshape(TOP_K, n, d // 2)
    return _moe_combine(x2, yg, route, norm_f_g, final_norm)


def kernel(x, positions, norm1_g, w_in, b_in, attn_sinks, sgu_ln_g, sgu_ln_b, w_spatial, b_spatial,
           w_proj_a, w_proj_b, w_proj_c, w_out, norm2_g, w_router_group, w_router_expert,
           w_expert_gate, w_expert_up, w_expert_down, norm_f_g):
    bsz, s, d = x.shape
    depth = w_in.shape[0]
    wg_all = w_expert_gate.reshape(depth * N_EXPERTS, d, D_EXPERT)
    wu_all = w_expert_up.reshape(depth * N_EXPERTS, d, D_EXPERT)
    wd_all = w_expert_down.reshape(depth * N_EXPERTS, D_EXPERT, d)
    assert d == D_MODEL and s % (C_PAIRS[-1][1] * BLOCK) == 0
    tabs = _rope_tables(positions)
    for l in range(depth):
        z = _in_proj(x, norm1_g[l], w_in, b_in.reshape(depth, 1, D_IN), l, tabs)
        oa = _attn_a(z, attn_sinks[l])
        ob = _sgu(z, sgu_ln_g[l], sgu_ln_b[l], w_spatial[l], b_spatial[l])
        oc = _attn_c(z)
        x2, h2, route, counts, pos = _merge(
            x.reshape(bsz * s, d), oa.reshape(bsz * s, A_Q_W), ob.reshape(bsz * s, B_WIDTH),
            oc.reshape(bsz * s, C_OUT_W), z.reshape(bsz * s, D_IN), w_proj_a[l], w_proj_b[l], w_proj_c[l], w_out[l],
            norm2_g[l], w_router_group[l], w_router_expert[l])
        x = _moe(h2, route, counts, pos, x2, l, wg_all, wu_all, wd_all, norm_f_g, l == depth - 1).reshape(bsz, s, d)
    return x
```

```python
import functools

import jax
import jax.numpy as jnp
from jax import lax
from jax.experimental import pallas as pl
from jax.experimental.pallas import tpu as pltpu
from jax.experimental.pallas import tpu_sc as plsc

F32 = jnp.float32
BF16 = jnp.bfloat16

D_MODEL = 1024
HEAD_DIM = 64
ROT_DIM = HEAD_DIM // 4
ROPE_THETA = 500000.0
BLOCK = 128
EPS = 1e-5
NEG_INF = -1e30

A_Q_HEADS = 16
A_KV_HEADS = 4
A_REP = A_Q_HEADS // A_KV_HEADS
A_WINDOW = 128
A_Q_W = A_Q_HEADS * HEAD_DIM
A_KV_W = A_KV_HEADS * HEAD_DIM

B_GROUPS = 12
B_CH = 64
B_WIDTH = B_GROUPS * B_CH
B_CHUNK = 128

C_PAIRS = ((128, 1), (512, 4), (2048, 16))
C_HEADS_PER_GROUP = 4
C_HEADS = C_HEADS_PER_GROUP * len(C_PAIRS)
C_OUT_W = C_HEADS_PER_GROUP * HEAD_DIM

N_BRANCH = 3
GATE_W = N_BRANCH * D_MODEL
D_IN = A_Q_W + 2 * A_KV_W + 2 * B_WIDTH + 3 * C_HEADS * HEAD_DIM + GATE_W

N_EXPERT_GROUPS = 4
EXPERTS_PER_GROUP = 8
N_EXPERTS = N_EXPERT_GROUPS * EXPERTS_PER_GROUP
TOP_K = 2
D_EXPERT = 256

LANES = 128
MXU_W = 256
VMEM_LIMIT = 56 * 1024 * 1024

IN_TN = 256
COL_AQ = 0
COL_AK = A_Q_W // IN_TN
COL_AV = COL_AK + A_KV_W // IN_TN
COL_BU = COL_AV + A_KV_W // IN_TN
COL_BV = COL_BU + B_WIDTH // IN_TN
COL_CQ = COL_BV + B_WIDTH // IN_TN
COL_CK = COL_CQ + C_HEADS * HEAD_DIM // IN_TN
COL_CV = COL_CK + C_HEADS * HEAD_DIM // IN_TN
COL_G = COL_CV + C_HEADS * HEAD_DIM // IN_TN
N_COL_TILES = D_IN // IN_TN

ROUTE_W = LANES
ROUTE_E0 = N_EXPERT_GROUPS
SUBLANES = 8


def _cparams(sem):
    return pltpu.CompilerParams(dimension_semantics=sem, vmem_limit_bytes=VMEM_LIMIT)


ROPE_PACK = LANES // ROT_DIM


def _rope_table_kernel(pos_ref, inv_ref, c_ref, s_ref):
    rows = pos_ref.shape[0]
    lane = lax.broadcasted_iota(jnp.int32, (1, LANES), 1)
    ang = pos_ref[...].astype(F32) * inv_ref[...]
    c = jnp.cos(ang)
    s = jnp.sin(ang)
    s = jnp.where((lane & (ROT_DIM - 1)) < ROT_DIM // 2, -s, s)
    rotary = (lane & (HEAD_DIM - 1)) < ROT_DIM
    first_head = lane < HEAD_DIM
    for q in range(ROPE_PACK):
        shift = (LANES - ROT_DIM * q) % LANES
        for src, dst_ref, fill in ((c, c_ref, 1.0), (s, s_ref, 0.0)):
            mine = pltpu.roll(src, shift, 1) if shift else src
            both = jnp.where(first_head, mine, pltpu.roll(mine, HEAD_DIM, 1))
            dst_ref[pl.ds(q, rows, stride=ROPE_PACK), :] = jnp.where(rotary, both, fill)


def _rope_tables(positions):
    n = positions.size
    inv = ROPE_THETA ** (-jnp.arange(0, ROT_DIM, 2, dtype=F32) / ROT_DIM)
    inv_lane = jnp.tile(inv, LANES // inv.shape[0]).reshape(1, LANES)
    packed = jnp.repeat(positions.reshape(n // ROPE_PACK, ROPE_PACK), ROT_DIM, axis=1)
    tm = 1024
    out_spec = pl.BlockSpec((tm, LANES), lambda i: (i, 0))
    return pl.pallas_call(
        _rope_table_kernel,
        out_shape=[jax.ShapeDtypeStruct((n, LANES), F32)] * 2,
        grid=(n // tm,),
        in_specs=[pl.BlockSpec((tm // ROPE_PACK, LANES), lambda i: (i, 0)),
                  pl.BlockSpec((1, LANES), lambda i: (0, 0))],
        out_specs=[out_spec, out_spec],
        compiler_params=_cparams(("parallel",)),
        name="rope_tables",
    )(packed, inv_lane)


IN_STEP_TILES = 3
IN_STEP_W = IN_STEP_TILES * IN_TN
IN_STEPS = N_COL_TILES // IN_STEP_TILES


def _is_rope_tile(tile):
    return tile < COL_AV or COL_CQ <= tile < COL_CV


_STEP_ROPES = tuple(tuple(_is_rope_tile(step * IN_STEP_TILES + u) for u in range(IN_STEP_TILES))
                    for step in range(IN_STEPS))


def _in_proj_kernel(x_ref, g_ref, w_ref, b_ref, c_ref, s_ref, z_ref, h_ref, acc0_ref, acc1_ref):
    s = pl.program_id(0)
    total = pl.num_programs(0) - 1
    j = lax.rem(s, IN_STEPS)

    @pl.when((j == 0) & (s < total))
    def _():
        x = x_ref[0]
        ms = jnp.mean(x * x, -1, keepdims=True)
        h_ref[...] = (x * lax.rsqrt(ms + EPS) * g_ref[...]).astype(BF16)

    prev = jnp.where(j == 0, IN_STEPS - 1, j - 1)

    def matmul_into(acc_ref):
        acc_ref[...] = jnp.dot(h_ref[...], w_ref[0].astype(BF16),
                               preferred_element_type=F32) + b_ref[0]

    first_half = (lax.broadcasted_iota(jnp.int32, (1, LANES), 1) & (HEAD_DIM - 1)) < ROT_DIM // 2

    def finish(acc_ref, ropes):
        for u, rope in enumerate(ropes):
            cols = slice(u * IN_TN, (u + 1) * IN_TN)
            if not rope:
                z_ref[0, :, cols] = acc_ref[:, cols].astype(BF16)
                continue
            tile = prev * IN_STEP_TILES + u
            is_q = (tile < COL_AK) | ((tile >= COL_CQ) & (tile < COL_CK))
            scale = jnp.where(is_q, HEAD_DIM ** -0.5, 1.0).astype(F32)
            for t in range(u * IN_TN // LANES, (u + 1) * IN_TN // LANES):
                lanes = slice(t * LANES, (t + 1) * LANES)
                a = acc_ref[:, lanes]
                partner = jnp.where(first_half, pltpu.roll(a, LANES - ROT_DIM // 2, 1),
                                    pltpu.roll(a, ROT_DIM // 2, 1))
                r = a * c_ref[...] + partner * s_ref[...]
                z_ref[0, :, lanes] = (r * scale).astype(BF16)

    accs = (acc0_ref, acc1_ref)
    patterns = sorted(set(_STEP_ROPES))
    pattern_id = jnp.int32(0)
    for step, ropes in enumerate(_STEP_ROPES):
        pattern_id = jnp.where(prev == step, patterns.index(ropes), pattern_id)
    assert not any(_STEP_ROPES[-1])
    inner = (s > 0) & (s < total)
    for parity in range(2):
        for pid, ropes in enumerate(patterns):
            @pl.when(inner & (lax.rem(s, 2) == parity) & (pattern_id == pid))
            def _(parity=parity, ropes=ropes):
                matmul_into(accs[parity])
                finish(accs[1 - parity], ropes)

        @pl.when((s == total) & (lax.rem(s, 2) == parity))
        def _(parity=parity):
            finish(accs[1 - parity], _STEP_ROPES[-1])

    @pl.when(s == 0)
    def _():
        matmul_into(accs[0])


def _in_proj(x, g, w_all, b_all, layer, tabs):
    bsz, s, d = x.shape
    cos_tab, sin_tab = tabs
    total = bsz * IN_STEPS

    def row(t):
        return jnp.minimum(t, total - 1) // IN_STEPS

    def col(t):
        return lax.rem(jnp.minimum(t, total - 1), IN_STEPS)

    def lag(t):
        return jnp.maximum(t - 1, 0)

    tab_spec = pl.BlockSpec((s, LANES), lambda t: (row(t), 0), pipeline_mode=pl.Buffered(1))
    return pl.pallas_call(
        _in_proj_kernel,
        out_shape=jax.ShapeDtypeStruct((bsz, s, D_IN), BF16),
        grid=(total + 1,),
        in_specs=[
            pl.BlockSpec((1, s, d), lambda t: (row(t), 0, 0)),
            pl.BlockSpec((1, d), lambda t: (0, 0)),
            pl.BlockSpec((1, d, IN_STEP_W), lambda t: (layer, 0, col(t))),
            pl.BlockSpec((1, 1, IN_STEP_W), lambda t: (layer, 0, col(t))),
            tab_spec, tab_spec,
        ],
        out_specs=pl.BlockSpec((1, s, IN_STEP_W), lambda t: (row(lag(t)), 0, col(lag(t)))),
        scratch_shapes=[pltpu.VMEM((s, d), BF16), pltpu.VMEM((s, IN_STEP_W), F32),
                        pltpu.VMEM((s, IN_STEP_W), F32)],
        compiler_params=_cparams(("arbitrary",)),
        name="in_proj",
    )(x, g.reshape(1, d), w_all, b_all, cos_tab, sin_tab)


STACK = C_HEADS_PER_GROUP * BLOCK
STEP_UNROLL = 8
A_STEP_UNROLL = 8
BAND_LOOKAHEAD = 4
FAST_STRIDE = 4


def _head_stack(blk):
    seg = lax.broadcasted_iota(jnp.int32, (1, MXU_W), 1) // HEAD_DIM
    rowseg = lax.broadcasted_iota(jnp.int32, (STACK, 1), 0) // BLOCK
    return jnp.where(seg == rowseg, jnp.concatenate([blk] * C_HEADS_PER_GROUP, axis=0),
                     jnp.zeros((), blk.dtype))


def _band_blocks(jobs, kp_ref, vp_ref, has_prev, diag_key, sink_col, want_lse):
    contract = (((1,), (1,)), ((), ()))
    i_loc = lax.broadcasted_iota(jnp.int32, (STACK, 1), 0) & (BLOCK - 1)
    jj = lax.broadcasted_iota(jnp.int32, (1, BLOCK), 1)
    upper = jj > i_loc
    seg = lax.broadcasted_iota(jnp.int32, (1, MXU_W), 1) // HEAD_DIM
    zero_b = jnp.zeros((), BF16)

    def key_rows(blk):
        lo = pl.multiple_of(blk * BLOCK, BLOCK)
        if has_prev:
            return pl.ds(lo, 2 * BLOCK)
        return pl.ds(pl.multiple_of(lo + BLOCK, BLOCK), BLOCK)

    def score(job):
        q, blk, _ = job
        return lax.dot_general(_head_stack(q), kp_ref[key_rows(blk), :], contract,
                               preferred_element_type=F32)

    def own_lanes(x):
        out = x[0:BLOCK]
        for h in range(1, C_HEADS_PER_GROUP):
            out = jnp.where(seg == h, x[h * BLOCK:(h + 1) * BLOCK], out)
        return out

    scores = {i: score(job) for i, job in enumerate(jobs[:BAND_LOOKAHEAD])}
    outs = []
    for i, (_, blk, n) in enumerate(jobs):
        s = scores.pop(i)
        if has_prev:
            sp = s[:, :BLOCK] + jnp.where(n == 0, NEG_INF, 0.0).astype(F32)
            f = jnp.where(upper, sp, s[:, BLOCK:])
        else:
            f = jnp.where(upper, NEG_INF, s)
        m = jnp.max(f, -1, keepdims=True)
        if diag_key:
            sd = jnp.sum(jnp.where(jj == i_loc, sp, 0.0), -1, keepdims=True)
            m = jnp.maximum(m, sd)
        if sink_col is not None:
            m = jnp.maximum(m, sink_col)
        p = jnp.exp(f - m)
        den = jnp.sum(p, -1, keepdims=True)
        pd = None
        if diag_key:
            pd = jnp.exp(sd - m)
            den = den + pd
        if sink_col is not None:
            den = den + jnp.exp(sink_col - m)
        inv = 1.0 / den
        pn = (p * inv).astype(BF16)
        if has_prev:
            below = zero_b
            if diag_key:
                below = jnp.where(jj == i_loc, pd * inv, 0.0).astype(BF16)
            pn = jnp.concatenate([jnp.where(upper, pn, below), jnp.where(upper, zero_b, pn)],
                                 axis=1)
        o = own_lanes(jnp.dot(pn, vp_ref[key_rows(blk), :], preferred_element_type=F32))
        lse = own_lanes(m + jnp.log(den)) + jnp.zeros((BLOCK, MXU_W), F32) if want_lse else None
        outs.append((o, lse))
        if i + BAND_LOOKAHEAD < len(jobs):
            scores[i + BAND_LOOKAHEAD] = score(jobs[i + BAND_LOOKAHEAD])
    return outs


def _replicate_head(blk, g):
    words = pltpu.bitcast(blk, jnp.uint32)
    tile = words[:, (g // 2) * LANES:(g // 2 + 1) * LANES]
    swapped = pltpu.roll(tile, HEAD_DIM, 1)
    low = lax.broadcasted_iota(jnp.int32, (1, LANES), 1) < HEAD_DIM
    both = jnp.where(low, tile, swapped) if g % 2 == 0 else jnp.where(low, swapped, tile)
    return pltpu.bitcast(jnp.concatenate([both, both], axis=1), BF16)


def _attn_a_kernel(sink_ref, q_ref, k_ref, v_ref, o_ref, kp_ref, vp_ref):
    nb = q_ref.shape[1] // BLOCK
    rowseg = lax.broadcasted_iota(jnp.int32, (STACK, 1), 0) // BLOCK
    kp_ref[pl.ds(0, BLOCK), :] = jnp.zeros((BLOCK, MXU_W), BF16)
    vp_ref[pl.ds(0, BLOCK), :] = jnp.zeros((BLOCK, MXU_W), BF16)
    for g in range(A_KV_HEADS):
        q_cols = slice(g * MXU_W, (g + 1) * MXU_W)

        def build(n, carry):
            rows = pl.ds(pl.multiple_of(n * BLOCK, BLOCK), BLOCK)
            dst = pl.ds(pl.multiple_of((n + 1) * BLOCK, BLOCK), BLOCK)
            kp_ref[dst, :] = _replicate_head(k_ref[0, rows, :], g)
            vp_ref[dst, :] = _replicate_head(v_ref[0, rows, :], g)
            return carry

        lax.fori_loop(0, nb, build, 0, unroll=8)
        sink_col = jnp.zeros((STACK, 1), F32)
        for r in range(A_REP):
            sink_col = jnp.where(rowseg == r, sink_ref[g * A_REP + r], sink_col)

        def step(it, carry):
            blocks = [it * A_STEP_UNROLL + u for u in range(A_STEP_UNROLL)]
            rows = [pl.ds(pl.multiple_of(n * BLOCK, BLOCK), BLOCK) for n in blocks]
            jobs = [(q_ref[0, r, q_cols], n, n) for r, n in zip(rows, blocks)]
            outs = _band_blocks(jobs, kp_ref, vp_ref, True, False, sink_col, False)
            for r, (o, _) in zip(rows, outs):
                o_ref[0, r, q_cols] = o.astype(o_ref.dtype)
            return carry

        lax.fori_loop(0, nb // A_STEP_UNROLL, step, 0)


def _attn_a(z, sinks):
    bsz, s, _ = z.shape
    nb = s // BLOCK
    grid_spec = pltpu.PrefetchScalarGridSpec(
        num_scalar_prefetch=1,
        grid=(bsz,),
        in_specs=[
            pl.BlockSpec((1, s, A_Q_W), lambda bi, sk: (bi, 0, 0)),
            pl.BlockSpec((1, s, A_KV_W), lambda bi, sk: (bi, 0, COL_AK)),
            pl.BlockSpec((1, s, A_KV_W), lambda bi, sk: (bi, 0, COL_AV)),
        ],
        out_specs=pl.BlockSpec((1, s, A_Q_W), lambda bi, sk: (bi, 0, 0)),
        scratch_shapes=[pltpu.VMEM(((nb + 1) * BLOCK, MXU_W), BF16),
                        pltpu.VMEM(((nb + 1) * BLOCK, MXU_W), BF16)],
    )
    return pl.pallas_call(
        _attn_a_kernel,
        out_shape=jax.ShapeDtypeStruct((bsz, s, A_Q_W), BF16),
        grid_spec=grid_spec,
        compiler_params=_cparams(("parallel",)),
        name="attn_a",
    )(sinks, z, z, z)


def _attn_c_kernel(q_ref, k_ref, v_ref, oc_ref, stage_ref, stage2_ref, pq_ref, pk_ref, pv_ref,
                   og_ref, lg_ref):
    s = q_ref.shape[1]
    nblk = s // BLOCK
    halves = MXU_W // LANES
    pk_ref[pl.ds(0, BLOCK), :] = jnp.zeros((BLOCK, MXU_W), BF16)
    pv_ref[pl.ds(0, BLOCK), :] = jnp.zeros((BLOCK, MXU_W), BF16)

    def class_rows(idx, nb, d):
        r = idx // nb
        n = idx - r * nb
        start = r + n * (BLOCK * d)
        rows = pl.ds(start, BLOCK, stride=d) if d > 1 else pl.ds(pl.multiple_of(start, BLOCK), BLOCK)
        return n, rows

    quarter = s // FAST_STRIDE

    def stage(src_ref, cols, d):
        x = src_ref[0, :, cols].astype(F32)
        for t in range(halves):
            stage_ref[t] = x[:, t * LANES:(t + 1) * LANES]
        if d > FAST_STRIDE:
            for t in range(halves):
                for r1 in range(FAST_STRIDE):
                    stage2_ref[t, pl.ds(r1 * quarter, quarter), :] = (
                        stage_ref[t, pl.ds(r1, quarter, stride=FAST_STRIDE), :])

    def staged_block(idx, nb, d):
        if d > FAST_STRIDE:
            assert nb == 1 and d == FAST_STRIDE * FAST_STRIDE
            start = (idx % FAST_STRIDE) * quarter + idx // FAST_STRIDE
            rows = pl.ds(start, BLOCK, stride=FAST_STRIDE)
            parts = [stage2_ref[t, rows, :] for t in range(halves)]
        else:
            _, rows = class_rows(idx, nb, d)
            parts = [stage_ref[t, rows, :] for t in range(halves)]
        return jnp.concatenate(parts, axis=1).astype(BF16)

    for g, (win, d) in enumerate(C_PAIRS):
        cols = slice(g * MXU_W, (g + 1) * MXU_W)
        nb = nblk // d
        has_prev = nb > 1
        assert win // d == BLOCK

        def class_major(dst_ref, offset):
            def build(idx, carry):
                dst = pl.ds(pl.multiple_of((idx + offset) * BLOCK, BLOCK), BLOCK)
                dst_ref[dst, :] = staged_block(idx, nb, d)
                return carry
            return build

        for src_ref, dst_ref, offset in ((k_ref, pk_ref, 1), (v_ref, pv_ref, 1), (q_ref, pq_ref, 0)):
            if d == 1:
                dst_ref[pl.ds(offset * BLOCK, s), :] = src_ref[0, :, cols]
                continue
            stage(src_ref, cols, d)
            lax.fori_loop(0, nblk, class_major(dst_ref, offset), 0, unroll=4)

        def step(it, carry):
            jobs, dsts = [], []
            for u in range(STEP_UNROLL):
                idx = it * STEP_UNROLL + u
                n, rows = class_rows(idx, nb, d)
                here = pl.ds(pl.multiple_of(idx * BLOCK, BLOCK), BLOCK)
                jobs.append((pq_ref[here, :], idx, n))
                dsts.append(rows)
            outs = _band_blocks(jobs, pk_ref, pv_ref, has_prev, has_prev, None, True)
            for rows, (o, lse) in zip(dsts, outs):
                for t in range(halves):
                    og_ref[g, t, rows, :] = o[:, t * LANES:(t + 1) * LANES]
                    lg_ref[g, t, rows, :] = lse[:, t * LANES:(t + 1) * LANES]
            return carry

        lax.fori_loop(0, nblk // STEP_UNROLL, step, 0)

    chunk = 2 * BLOCK

    def combine(i, carry):
        rows = pl.ds(pl.multiple_of(i * chunk, chunk), chunk)
        for t in range(halves):
            ls = [lg_ref[g, t, rows, :] for g in range(len(C_PAIRS))]
            m = jnp.maximum(jnp.maximum(ls[0], ls[1]), ls[2])
            es = [jnp.exp(l - m) for l in ls]
            num = (es[0] * og_ref[0, t, rows, :] + es[1] * og_ref[1, t, rows, :]
                   + es[2] * og_ref[2, t, rows, :])
            oc_ref[0, rows, t * LANES:(t + 1) * LANES] = (
                num / (es[0] + es[1] + es[2])).astype(oc_ref.dtype)
        return carry

    lax.fori_loop(0, s // chunk, combine, 0)


def _attn_c(z):
    bsz, s, _ = z.shape
    width = C_HEADS * HEAD_DIM
    halves = MXU_W // LANES

    def in_spec(col):
        return pl.BlockSpec((1, s, width), lambda bi: (bi, 0, col * IN_TN // width))

    return pl.pallas_call(
        _attn_c_kernel,
        out_shape=jax.ShapeDtypeStruct((bsz, s, C_OUT_W), BF16),
        grid=(bsz,),
        in_specs=[in_spec(COL_CQ), in_spec(COL_CK), in_spec(COL_CV)],
        out_specs=pl.BlockSpec((1, s, C_OUT_W), lambda bi: (bi, 0, 0)),
        scratch_shapes=[
            pltpu.VMEM((halves, s, LANES), F32),
            pltpu.VMEM((halves, s, LANES), F32),
            pltpu.VMEM((s, MXU_W), BF16),
            pltpu.VMEM((s + BLOCK, MXU_W), BF16),
            pltpu.VMEM((s + BLOCK, MXU_W), BF16),
            pltpu.VMEM((len(C_PAIRS), halves, s, LANES), F32),
            pltpu.VMEM((len(C_PAIRS), halves, s, LANES), F32),
        ],
        compiler_params=_cparams(("parallel",)),
        name="attn_c",
    )(z, z, z)


def _sgu_kernel(zu_ref, zv_ref, lng_ref, lnb_ref, ws_ref, bs_ref, o_ref):
    tm = zu_ref.shape[1]
    u = jax.nn.gelu(zu_ref[0]).astype(F32)
    v = jax.nn.gelu(zv_ref[0]).astype(F32)
    mu = jnp.mean(v, -1, keepdims=True)
    var = jnp.mean(jnp.square(v - mu), -1, keepdims=True)
    vn = ((v - mu) * lax.rsqrt(var + EPS) * lng_ref[...] + lnb_ref[...]).astype(BF16)
    causal = (lax.broadcasted_iota(jnp.int32, (B_CHUNK, 1), 0)
              >= lax.broadcasted_iota(jnp.int32, (1, B_CHUNK), 1))
    first = lax.broadcasted_iota(jnp.int32, (1, LANES), 1) < B_CH
    zero = jnp.zeros((), BF16)
    for p in range(B_WIDTH // LANES):
        w0 = jnp.where(causal, ws_ref[2 * p], zero)
        w1 = jnp.where(causal, ws_ref[2 * p + 1], zero)
        cols = slice(p * LANES, (p + 1) * LANES)
        for c in range(tm // B_CHUNK):
            rows = slice(c * B_CHUNK, (c + 1) * B_CHUNK)
            vv = vn[rows, cols]
            sv = jnp.where(first,
                           jnp.dot(w0, vv, preferred_element_type=F32),
                           jnp.dot(w1, vv, preferred_element_type=F32)) + bs_ref[:, cols]
            o_ref[0, rows, cols] = (u[rows, cols] * sv).astype(o_ref.dtype)


def _sgu(z, ln_g, ln_b, w_s, b_s):
    bsz, s, _ = z.shape
    tm = 512
    bias = jnp.repeat(b_s.T, B_CH, axis=1)
    col_u = COL_BU * IN_TN // B_WIDTH
    col_v = COL_BV * IN_TN // B_WIDTH
    return pl.pallas_call(
        _sgu_kernel,
        out_shape=jax.ShapeDtypeStruct((bsz, s, B_WIDTH), BF16),
        grid=(bsz, s // tm),
        in_specs=[
            pl.BlockSpec((1, tm, B_WIDTH), lambda bi, i: (bi, i, col_u)),
            pl.BlockSpec((1, tm, B_WIDTH), lambda bi, i: (bi, i, col_v)),
            pl.BlockSpec((1, B_WIDTH), lambda bi, i: (0, 0)),
            pl.BlockSpec((1, B_WIDTH), lambda bi, i: (0, 0)),
            pl.BlockSpec((B_GROUPS, B_CHUNK, B_CHUNK), lambda bi, i: (0, 0, 0)),
            pl.BlockSpec((B_CHUNK, B_WIDTH), lambda bi, i: (0, 0)),
        ],
        out_specs=pl.BlockSpec((1, tm, B_WIDTH), lambda bi, i: (bi, i, 0)),
        compiler_params=_cparams(("parallel", "parallel")),
        name="sgu",
    )(z, z, ln_g.reshape(1, B_WIDTH), ln_b.reshape(1, B_WIDTH), w_s.astype(BF16), bias)


def _route(logits):
    lane = lax.broadcasted_iota(jnp.int32, (1, ROUTE_W), 1)
    lane_f = lane.astype(F32)
    is_g = lane < N_EXPERT_GROUPS
    lg = jnp.where(is_g, logits, NEG_INF)
    mg = jnp.max(lg, -1, keepdims=True)
    pg_top = 1.0 / jnp.sum(jnp.where(is_g, jnp.exp(lg - mg), 0.0), -1, keepdims=True)
    g_idx = jnp.min(jnp.where(lg == mg, lane_f, float(ROUTE_W)), -1, keepdims=True)
    e_group = ((lane - ROUTE_E0) // EXPERTS_PER_GROUP).astype(F32)
    in_grp = (lane >= ROUTE_E0) & (lane < ROUTE_E0 + N_EXPERTS) & (e_group == g_idx)
    le = jnp.where(in_grp, logits, NEG_INF)
    m1 = jnp.max(le, -1, keepdims=True)
    i1 = jnp.min(jnp.where(le == m1, lane_f, float(ROUTE_W)), -1, keepdims=True)
    le2 = jnp.where(lane_f == i1, NEG_INF, le)
    m2 = jnp.max(le2, -1, keepdims=True)
    i2 = jnp.min(jnp.where(le2 == m2, lane_f, float(ROUTE_W)), -1, keepdims=True)
    t = jnp.exp(m2 - m1)
    w1 = pg_top / (1.0 + t)
    w2 = w1 * t
    picks = (i1 - ROUTE_E0, i2 - ROUTE_E0, w1, w2)
    out = jnp.zeros(logits.shape, F32)
    for k, val in enumerate(picks):
        out = jnp.where(lane == k, val, out)
    return out


def _merge_kernel(x_ref, oa_ref, ob_ref, oc_ref,
                  g0_ref, g1_ref, g2_ref, g3_ref, wa_ref, wb_ref, wc_ref, wo_ref, n2_ref, wr_ref,
                  xo_ref, h2_ref, route_ref, counts_ref, pos_ref, xn0_ref, xn1_ref, cnt_ref):
    i = pl.program_id(0)
    n_tiles = pl.num_programs(0) - 1

    def residual_into(xn_ref):
        zg = jnp.concatenate([g0_ref[...], g1_ref[...], g2_ref[...], g3_ref[...]],
                             axis=1).astype(F32)
        gates = jax.nn.sigmoid(zg)
        halves = (slice(0, D_MODEL // 2), slice(D_MODEL // 2, D_MODEL))
        branch = []
        for cols in halves:
            branch.append((jnp.dot(oa_ref[...], wa_ref[:, cols], preferred_element_type=F32),
                           jnp.dot(ob_ref[...], wb_ref[:, cols], preferred_element_type=F32),
                           jnp.dot(oc_ref[...], wc_ref[:, cols], preferred_element_type=F32)))
        xn = x_ref[...]
        for cols, (pa, pb, pc) in zip(halves, branch):
            ga, gb, gc = (gates[:, k * D_MODEL:(k + 1) * D_MODEL][:, cols] for k in range(N_BRANCH))
            merged = ga * pa + gb * pb + gc * pc
            xn = xn + jnp.dot(merged.astype(BF16), wo_ref[cols, :], preferred_element_type=F32)
        xn_ref[...] = xn

    def tail_from(xn_ref):
        xn = xn_ref[...]
        xo_ref[...] = xn
        ms = jnp.mean(xn * xn, -1, keepdims=True)
        h2 = xn * lax.rsqrt(ms + EPS) * n2_ref[...]
        h2_ref[...] = _pack_bf16_pairs(h2)
        route = _route(jnp.dot(h2.astype(BF16), wr_ref[...], preferred_element_type=F32))
        tm = route.shape[0]
        lane = lax.broadcasted_iota(jnp.int32, (1, ROUTE_W), 1)
        lane_f = lane.astype(F32)
        row = lax.broadcasted_iota(jnp.int32, (tm, 1), 0)
        hit = [lane_f == route[:, k:k + 1] for k in range(TOP_K)]
        onehot = jnp.where(hit[0], 1.0, 0.0) + jnp.where(hit[1], 1.0, 0.0)
        scan = onehot
        shift = 1
        while shift < tm:
            scan = scan + jnp.where(row >= shift, pltpu.roll(scan, shift, 0), 0.0)
            shift *= 2
        before = scan - onehot + cnt_ref[0:1, :]
        capacity = pos_ref.shape[1] * LANES
        diag = (row & (LANES - 1)) == lane
        first_row = pl.multiple_of((i - 1) * (tm // LANES), tm // LANES)
        for k in range(TOP_K):
            rank = jnp.sum(jnp.where(hit[k], before, 0.0), -1, keepdims=True)
            slot = jnp.where(diag, route[:, k:k + 1] * capacity + rank, 0.0)
            dense = jnp.concatenate(
                [jnp.sum(slot[r * LANES:(r + 1) * LANES], axis=0, keepdims=True)
                 for r in range(tm // LANES)], axis=0)
            pos_ref[k, pl.ds(first_row, tm // LANES), :] = dense.astype(jnp.int32)
        route_ref[...] = route
        cnt = cnt_ref[0:1, :] + jnp.sum(onehot, axis=0, keepdims=True)
        cnt_ref[...] = jnp.broadcast_to(cnt, cnt_ref.shape)
        counts_ref[...] = jnp.broadcast_to(cnt, counts_ref.shape)

    bufs = (xn0_ref, xn1_ref)
    for parity in range(2):
        @pl.when((i > 0) & (i < n_tiles) & (i % 2 == parity))
        def _(parity=parity):
            tail_from(bufs[1 - parity])
            residual_into(bufs[parity])

        @pl.when((i == n_tiles) & (i % 2 == parity))
        def _(parity=parity):
            tail_from(bufs[1 - parity])

    @pl.when(i == 0)
    def _():
        cnt_ref[...] = jnp.zeros_like(cnt_ref)
        residual_into(bufs[0])


def _merge(x2, oa, ob, oc, z2, wa, wb, wc, wo, n2g, w_rg, w_re):
    n, d = x2.shape
    tm = 512
    wr = jnp.concatenate(
        [w_rg, w_re, jnp.zeros((d, ROUTE_W - N_EXPERT_GROUPS - N_EXPERTS), w_rg.dtype)], axis=1)
    gate_w = GATE_W // 4
    gate_col0 = COL_G * IN_TN // gate_w

    n_tiles = n // tm
    last = n_tiles - 1

    def rows(width):
        return pl.BlockSpec((tm, width), lambda i: (jnp.minimum(i, last), 0))

    def lagged_rows(width):
        return pl.BlockSpec((tm, width), lambda i: (jnp.maximum(i - 1, 0), 0))

    def full(shape):
        return pl.BlockSpec(shape, lambda i: (0, 0))

    gate_specs = [pl.BlockSpec((tm, gate_w), lambda i, k=k: (jnp.minimum(i, last), gate_col0 + k))
                  for k in range(4)]
    return pl.pallas_call(
        _merge_kernel,
        out_shape=[jax.ShapeDtypeStruct((n, d), F32),
                   jax.ShapeDtypeStruct((n, d // 2), jnp.int32),
                   jax.ShapeDtypeStruct((n, ROUTE_W), F32),
                   jax.ShapeDtypeStruct((SUBLANES, ROUTE_W), F32),
                   jax.ShapeDtypeStruct((TOP_K, n // LANES, LANES), jnp.int32)],
        grid=(n_tiles + 1,),
        in_specs=[rows(d), rows(A_Q_W), rows(B_WIDTH), rows(C_OUT_W),
                  *gate_specs,
                  full((A_Q_W, d)), full((B_WIDTH, d)), full((C_OUT_W, d)), full((d, d)),
                  full((1, d)), full((d, ROUTE_W))],
        out_specs=[lagged_rows(d), lagged_rows(d // 2), lagged_rows(ROUTE_W),
                   full((SUBLANES, ROUTE_W)),
                   pl.BlockSpec((TOP_K, n // LANES, LANES), lambda i: (0, 0, 0))],
        scratch_shapes=[pltpu.VMEM((tm, d), F32), pltpu.VMEM((tm, d), F32),
                        pltpu.VMEM((SUBLANES, ROUTE_W), F32)],
        compiler_params=_cparams(("arbitrary",)),
        name="merge",
    )(x2, oa, ob, oc, z2, z2, z2, z2,
      wa.astype(BF16), wb.astype(BF16), wc.astype(BF16), wo.astype(BF16),
      n2g.reshape(1, d), wr.astype(BF16))


MOE_TM = 1024
SC_CORES = 2
SC_SUBCORES = 16
SC_WORKERS = SC_CORES * SC_SUBCORES
SC_CHUNK = 128


def _pack_bf16_pairs(x):
    w = x.shape[1] // 2
    lo = pltpu.bitcast(x[:, :w].astype(BF16).astype(F32), jnp.int32)
    hi = pltpu.bitcast(x[:, w:].astype(BF16).astype(F32), jnp.int32)
    return (hi & jnp.int32(-65536)) | lax.shift_right_logical(lo, jnp.int32(16))


def _unpack_bf16_pairs(p):
    lo = pltpu.bitcast(lax.shift_left(p, jnp.int32(16)), F32)
    hi = pltpu.bitcast(p & jnp.int32(-65536), F32)
    return jnp.concatenate([lo, hi], axis=1)


def _sc_gather_rows(table, idx):
    rows, width = idx.shape[0], table.shape[1]
    per_worker = rows // SC_WORKERS
    n_chunks = per_worker // SC_CHUNK
    assert per_worker * SC_WORKERS == rows and n_chunks * SC_CHUNK == per_worker
    mesh = plsc.VectorSubcoreMesh(core_axis_name="c", subcore_axis_name="s",
                                  num_cores=SC_CORES, num_subcores=SC_SUBCORES)

    @functools.partial(
        pl.kernel, mesh=mesh,
        out_type=jax.ShapeDtypeStruct((rows, width), table.dtype),
        scratch_types=[pltpu.VMEM((n_chunks, SC_CHUNK), jnp.int32),
                       pltpu.VMEM((SC_CHUNK, width), table.dtype),
                       pltpu.SemaphoreType.DMA],
        name="sc_gather_rows",
    )
    def gather(table_hbm, idx_hbm, out_hbm, idx_v, rows_v, sem):
        wid = lax.axis_index("s") * SC_CORES + lax.axis_index("c")
        pltpu.sync_copy(idx_hbm.at[wid], idx_v)
        base = wid * per_worker

        @pl.loop(0, n_chunks)
        def _(c):
            pltpu.async_copy(table_hbm.at[idx_v.at[c]], rows_v, sem).wait()
            pltpu.sync_copy(rows_v, out_hbm.at[pl.ds(base + c * SC_CHUNK, SC_CHUNK)])

    return gather(table, idx.reshape(SC_WORKERS, n_chunks, SC_CHUNK))


def _sc_scatter_rows(rows, pos, n_slots):
    n, width = rows.shape
    per_worker = n // SC_WORKERS
    n_chunks = per_worker // SC_CHUNK
    assert per_worker * SC_WORKERS == n and n_chunks * SC_CHUNK == per_worker
    mesh = plsc.VectorSubcoreMesh(core_axis_name="c", subcore_axis_name="s",
                                  num_cores=SC_CORES, num_subcores=SC_SUBCORES)

    @functools.partial(
        pl.kernel, mesh=mesh,
        out_type=jax.ShapeDtypeStruct((n_slots, width), rows.dtype),
        scratch_types=[pltpu.VMEM((TOP_K, n_chunks, SC_CHUNK), jnp.int32),
                       pltpu.VMEM((SC_CHUNK, width), rows.dtype),
                       pltpu.SemaphoreType.DMA],
        name="sc_scatter_rows",
    )
    def scatter(rows_hbm, pos_hbm, out_hbm, pos_v, rows_v, sem):
        wid = lax.axis_index("s") * SC_CORES + lax.axis_index("c")
        for k in range(TOP_K):
            pltpu.sync_copy(pos_hbm.at[k, wid], pos_v.at[k])
        base = wid * per_worker

        @pl.loop(0, n_chunks)
        def _(c):
            pltpu.sync_copy(rows_hbm.at[pl.ds(base + c * SC_CHUNK, SC_CHUNK)], rows_v)
            for k in range(TOP_K):
                pltpu.async_copy(rows_v, out_hbm.at[pos_v.at[k, c]], sem).wait()

    return scatter(rows, pos.reshape(TOP_K, SC_WORKERS, n_chunks, SC_CHUNK))


def _dispatch_plan(counts, capacity, n_steps, expert_base):
    counts = counts[0, :N_EXPERTS].astype(jnp.int32)
    tiles = (counts + MOE_TM - 1) // MOE_TM
    tile_ends = jnp.cumsum(tiles)
    step = jnp.arange(n_steps, dtype=jnp.int32)
    used = step < tile_ends[-1]
    expert = jnp.minimum(jnp.sum((step[:, None] >= tile_ends[None, :]).astype(jnp.int32), axis=1),
                         N_EXPERTS - 1)
    chunk = step - (tile_ends - tiles)[expert]
    rows = jnp.where(used, jnp.clip(counts[expert] - chunk * MOE_TM, 0, MOE_TM), 0)
    first = jnp.concatenate([jnp.ones((1,), jnp.int32),
                             (expert[1:] != expert[:-1]).astype(jnp.int32)])
    block = expert * (capacity // MOE_TM) + chunk
    last_used = jnp.sum(jnp.where(step == tile_ends[-1] - 1, block, 0))
    spare = N_EXPERTS * (capacity // MOE_TM)
    blocks = jnp.stack([jnp.where(used, block, last_used), jnp.where(used, block, spare)])
    return expert + expert_base, rows.astype(jnp.int32), first, blocks.astype(jnp.int32)


def _moe_tile_kernel(te_ref, tr_ref, tf_ref, tb_ref, xs_ref, wg_ref, wu_ref, wd_ref, ys_ref,
                     wg_s, wu_s, wd_s):
    t = pl.program_id(0)

    @pl.when(tf_ref[t] != 0)
    def _():
        wg_s[...] = wg_ref[0].astype(BF16)
        wu_s[...] = wu_ref[0].astype(BF16)
        wd_s[...] = wd_ref[0].astype(BF16)

    @pl.when(tr_ref[t] != 0)
    def _():
        occupied = lax.broadcasted_iota(jnp.int32, (MOE_TM, 1), 0) < tr_ref[t]
        x = jnp.where(occupied, _unpack_bf16_pairs(xs_ref[...]), 0.0).astype(BF16)
        hg = jnp.dot(x, wg_s[...], preferred_element_type=F32)
        hu = jnp.dot(x, wu_s[...], preferred_element_type=F32)
        a = (jax.nn.silu(hg) * hu).astype(BF16)
        ys_ref[...] = _pack_bf16_pairs(jnp.dot(a, wd_s[...], preferred_element_type=F32))

    @pl.when(tr_ref[t] == 0)
    def _():
        ys_ref[...] = jnp.zeros_like(ys_ref)


def _moe_tiles(xs, tile_expert, tile_rows, tile_first, tile_blocks, w_gate, w_up, w_down):
    n_slots, half = xs.shape
    d = 2 * half
    grid_spec = pltpu.PrefetchScalarGridSpec(
        num_scalar_prefetch=4,
        grid=(tile_expert.shape[0],),
        in_specs=[
            pl.BlockSpec((MOE_TM, half), lambda t, te, tr, tf, tb: (tb[0, t], 0)),
            pl.BlockSpec((1, d, D_EXPERT), lambda t, te, tr, tf, tb: (te[t], 0, 0)),
            pl.BlockSpec((1, d, D_EXPERT), lambda t, te, tr, tf, tb: (te[t], 0, 0)),
            pl.BlockSpec((1, D_EXPERT, d), lambda t, te, tr, tf, tb: (te[t], 0, 0)),
        ],
        out_specs=pl.BlockSpec((MOE_TM, half), lambda t, te, tr, tf, tb: (tb[1, t], 0)),
        scratch_shapes=[pltpu.VMEM((d, D_EXPERT), BF16), pltpu.VMEM((d, D_EXPERT), BF16),
                        pltpu.VMEM((D_EXPERT, d), BF16)],
    )
    return pl.pallas_call(
        _moe_tile_kernel,
        out_shape=jax.ShapeDtypeStruct((n_slots, half), jnp.int32),
        grid_spec=grid_spec,
        compiler_params=_cparams(("arbitrary",)),
        name="moe_tiles",
    )(tile_expert, tile_rows, tile_first, tile_blocks, xs, w_gate, w_up, w_down)


def _moe_combine_kernel(x_ref, y0_ref, y1_ref, r_ref, g_ref, o_ref, *, final_norm):
    w0 = r_ref[:, 2:3]
    w1 = r_ref[:, 3:4]
    x = x_ref[...] + w0 * _unpack_bf16_pairs(y0_ref[0]) + w1 * _unpack_bf16_pairs(y1_ref[0])
    if final_norm:
        ms = jnp.mean(x * x, -1, keepdims=True)
        x = x * lax.rsqrt(ms + EPS) * g_ref[...]
    o_ref[...] = x


def _moe_combine(x2, yg, route, norm_g, final_norm):
    n, d = x2.shape
    tm = 1024
    return pl.pallas_call(
        functools.partial(_moe_combine_kernel, final_norm=final_norm),
        out_shape=jax.ShapeDtypeStruct((n, d), F32),
        grid=(n // tm,),
        in_specs=[
            pl.BlockSpec((tm, d), lambda i: (i, 0)),
            pl.BlockSpec((1, tm, d // 2), lambda i: (0, i, 0)),
            pl.BlockSpec((1, tm, d // 2), lambda i: (1, i, 0)),
            pl.BlockSpec((tm, ROUTE_W), lambda i: (i, 0)),
            pl.BlockSpec((1, d), lambda i: (0, 0)),
        ],
        out_specs=pl.BlockSpec((tm, d), lambda i: (i, 0)),
        compiler_params=_cparams(("parallel",)),
        name="moe_combine",
    )(x2, yg, yg, route, norm_g.reshape(1, d))


def _moe(h2p, route, counts, pos, x2, layer, w_gate, w_up, w_down, norm_f_g, final_norm):
    n, d = x2.shape
    n_slots = N_EXPERTS * n + MOE_TM
    n_steps = TOP_K * n // MOE_TM + N_EXPERTS
    tile_expert, tile_rows, tile_first, tile_blocks = _dispatch_plan(
        counts, n, n_steps, layer * N_EXPERTS)
    xs = _sc_scatter_rows(h2p, pos, n_slots)
    ys = _moe_tiles(xs, tile_expert, tile_rows, tile_first, tile_blocks, w_gate, w_up, w_down)
    yg = _sc_gather_rows(ys, pos.reshape(-1)).reshape(TOP_K, n, d // 2)
    return _moe_combine(x2, yg, route, norm_f_g, final_norm)


def kernel(x, positions, norm1_g, w_in, b_in, attn_sinks, sgu_ln_g, sgu_ln_b, w_spatial, b_spatial,
           w_proj_a, w_proj_b, w_proj_c, w_out, norm2_g, w_router_group, w_router_expert,
           w_expert_gate, w_expert_up, w_expert_down, norm_f_g):
    bsz, s, d = x.shape
    depth = w_in.shape[0]
    wg_all = w_expert_gate.reshape(depth * N_EXPERTS, d, D_EXPERT)
    wu_all = w_expert_up.reshape(depth * N_EXPERTS, d, D_EXPERT)
    wd_all = w_expert_down.reshape(depth * N_EXPERTS, D_EXPERT, d)
    assert d == D_MODEL and s % (C_PAIRS[-1][1] * BLOCK) == 0
    tabs = _rope_tables(positions)
    for l in range(depth):
        z = _in_proj(x, norm1_g[l], w_in, b_in.reshape(depth, 1, D_IN), l, tabs)
        oa = _attn_a(z, attn_sinks[l])
        ob = _sgu(z, sgu_ln_g[l], sgu_ln_b[l], w_spatial[l], b_spatial[l])
        oc = _attn_c(z)
        x2, h2, route, counts, pos = _merge(
            x.reshape(bsz * s, d), oa.reshape(bsz * s, A_Q_W), ob.reshape(bsz * s, B_WIDTH),
            oc.reshape(bsz * s, C_OUT_W), z.reshape(bsz * s, D_IN), w_proj_a[l], w_proj_b[l], w_proj_c[l], w_out[l],
            norm2_g[l], w_router_group[l], w_router_expert[l])
        x = _moe(h2, route, counts, pos, x2, l, wg_all, wu_all, wd_all, norm_f_g, l == depth - 1).reshape(bsz, s, d)
    return x
```

```python
import functools

import jax
import jax.numpy as jnp
from jax import lax
from jax.experimental import pallas as pl
from jax.experimental.pallas import tpu as pltpu
from jax.experimental.pallas import tpu_sc as plsc

F32 = jnp.float32
BF16 = jnp.bfloat16

D_MODEL = 1024
HEAD_DIM = 64
ROT_DIM = HEAD_DIM // 4
ROPE_THETA = 500000.0
BLOCK = 128
EPS = 1e-5
NEG_INF = -1e30

A_Q_HEADS = 16
A_KV_HEADS = 4
A_REP = A_Q_HEADS // A_KV_HEADS
A_WINDOW = 128
A_Q_W = A_Q_HEADS * HEAD_DIM
A_KV_W = A_KV_HEADS * HEAD_DIM

B_GROUPS = 12
B_CH = 64
B_WIDTH = B_GROUPS * B_CH
B_CHUNK = 128

C_PAIRS = ((128, 1), (512, 4), (2048, 16))
C_HEADS_PER_GROUP = 4
C_HEADS = C_HEADS_PER_GROUP * len(C_PAIRS)
C_OUT_W = C_HEADS_PER_GROUP * HEAD_DIM

N_BRANCH = 3
GATE_W = N_BRANCH * D_MODEL
D_IN = A_Q_W + 2 * A_KV_W + 2 * B_WIDTH + 3 * C_HEADS * HEAD_DIM + GATE_W

N_EXPERT_GROUPS = 4
EXPERTS_PER_GROUP = 8
N_EXPERTS = N_EXPERT_GROUPS * EXPERTS_PER_GROUP
TOP_K = 2
D_EXPERT = 256

LANES = 128
MXU_W = 256
VMEM_LIMIT = 56 * 1024 * 1024

IN_TN = 256
COL_AQ = 0
COL_AK = A_Q_W // IN_TN
COL_AV = COL_AK + A_KV_W // IN_TN
COL_BU = COL_AV + A_KV_W // IN_TN
COL_BV = COL_BU + B_WIDTH // IN_TN
COL_CQ = COL_BV + B_WIDTH // IN_TN
COL_CK = COL_CQ + C_HEADS * HEAD_DIM // IN_TN
COL_CV = COL_CK + C_HEADS * HEAD_DIM // IN_TN
COL_G = COL_CV + C_HEADS * HEAD_DIM // IN_TN
N_COL_TILES = D_IN // IN_TN

ROUTE_W = LANES
ROUTE_E0 = N_EXPERT_GROUPS
SUBLANES = 8


def _cparams(sem):
    return pltpu.CompilerParams(dimension_semantics=sem, vmem_limit_bytes=VMEM_LIMIT)


ROPE_PACK = LANES // ROT_DIM


def _rope_table_kernel(pos_ref, inv_ref, c_ref, s_ref):
    rows = pos_ref.shape[0]
    lane = lax.broadcasted_iota(jnp.int32, (1, LANES), 1)
    ang = pos_ref[...].astype(F32) * inv_ref[...]
    c = jnp.cos(ang)
    s = jnp.sin(ang)
    s = jnp.where((lane & (ROT_DIM - 1)) < ROT_DIM // 2, -s, s)
    rotary = (lane & (HEAD_DIM - 1)) < ROT_DIM
    first_head = lane < HEAD_DIM
    for q in range(ROPE_PACK):
        shift = (LANES - ROT_DIM * q) % LANES
        for src, dst_ref, fill in ((c, c_ref, 1.0), (s, s_ref, 0.0)):
            mine = pltpu.roll(src, shift, 1) if shift else src
            both = jnp.where(first_head, mine, pltpu.roll(mine, HEAD_DIM, 1))
            dst_ref[pl.ds(q, rows, stride=ROPE_PACK), :] = jnp.where(rotary, both, fill)


def _rope_tables(positions):
    n = positions.size
    inv = ROPE_THETA ** (-jnp.arange(0, ROT_DIM, 2, dtype=F32) / ROT_DIM)
    inv_lane = jnp.tile(inv, LANES // inv.shape[0]).reshape(1, LANES)
    packed = jnp.repeat(positions.reshape(n // ROPE_PACK, ROPE_PACK), ROT_DIM, axis=1)
    tm = 1024
    out_spec = pl.BlockSpec((tm, LANES), lambda i: (i, 0))
    return pl.pallas_call(
        _rope_table_kernel,
        out_shape=[jax.ShapeDtypeStruct((n, LANES), F32)] * 2,
        grid=(n // tm,),
        in_specs=[pl.BlockSpec((tm // ROPE_PACK, LANES), lambda i: (i, 0)),
                  pl.BlockSpec((1, LANES), lambda i: (0, 0))],
        out_specs=[out_spec, out_spec],
        compiler_params=_cparams(("parallel",)),
        name="rope_tables",
    )(packed, inv_lane)


IN_STEP_TILES = 3
IN_STEP_W = IN_STEP_TILES * IN_TN
IN_STEPS = N_COL_TILES // IN_STEP_TILES


def _is_rope_tile(tile):
    return tile < COL_AV or COL_CQ <= tile < COL_CV


_STEP_ROPES = tuple(tuple(_is_rope_tile(step * IN_STEP_TILES + u) for u in range(IN_STEP_TILES))
                    for step in range(IN_STEPS))


def _in_proj_kernel(x_ref, g_ref, w_ref, b_ref, c_ref, s_ref, z_ref, h_ref, acc0_ref, acc1_ref):
    s = pl.program_id(0)
    total = pl.num_programs(0) - 1
    j = lax.rem(s, IN_STEPS)

    @pl.when((j == 0) & (s < total))
    def _():
        x = x_ref[0]
        ms = jnp.mean(x * x, -1, keepdims=True)
        h_ref[...] = (x * lax.rsqrt(ms + EPS) * g_ref[...]).astype(BF16)

    prev = jnp.where(j == 0, IN_STEPS - 1, j - 1)

    def matmul_into(acc_ref):
        acc_ref[...] = jnp.dot(h_ref[...], w_ref[0].astype(BF16),
                               preferred_element_type=F32) + b_ref[0]

    first_half = (lax.broadcasted_iota(jnp.int32, (1, LANES), 1) & (HEAD_DIM - 1)) < ROT_DIM // 2

    def finish(acc_ref, ropes):
        for u, rope in enumerate(ropes):
            cols = slice(u * IN_TN, (u + 1) * IN_TN)
            if not rope:
                z_ref[0, :, cols] = acc_ref[:, cols].astype(BF16)
                continue
            tile = prev * IN_STEP_TILES + u
            is_q = (tile < COL_AK) | ((tile >= COL_CQ) & (tile < COL_CK))
            scale = jnp.where(is_q, HEAD_DIM ** -0.5, 1.0).astype(F32)
            for t in range(u * IN_TN // LANES, (u + 1) * IN_TN // LANES):
                lanes = slice(t * LANES, (t + 1) * LANES)
                a = acc_ref[:, lanes]
                partner = jnp.where(first_half, pltpu.roll(a, LANES - ROT_DIM // 2, 1),
                                    pltpu.roll(a, ROT_DIM // 2, 1))
                r = a * c_ref[...] + partner * s_ref[...]
                z_ref[0, :, lanes] = (r * scale).astype(BF16)

    accs = (acc0_ref, acc1_ref)
    patterns = sorted(set(_STEP_ROPES))
    pattern_id = jnp.int32(0)
    for step, ropes in enumerate(_STEP_ROPES):
        pattern_id = jnp.where(prev == step, patterns.index(ropes), pattern_id)
    assert not any(_STEP_ROPES[-1])
    inner = (s > 0) & (s < total)
    for parity in range(2):
        for pid, ropes in enumerate(patterns):
            @pl.when(inner & (lax.rem(s, 2) == parity) & (pattern_id == pid))
            def _(parity=parity, ropes=ropes):
                matmul_into(accs[parity])
                finish(accs[1 - parity], ropes)

        @pl.when((s == total) & (lax.rem(s, 2) == parity))
        def _(parity=parity):
            finish(accs[1 - parity], _STEP_ROPES[-1])

    @pl.when(s == 0)
    def _():
        matmul_into(accs[0])


def _in_proj(x, g, w_all, b_all, layer, tabs):
    bsz, s, d = x.shape
    cos_tab, sin_tab = tabs
    total = bsz * IN_STEPS

    def row(t):
        return jnp.minimum(t, total - 1) // IN_STEPS

    def col(t):
        return lax.rem(jnp.minimum(t, total - 1), IN_STEPS)

    def lag(t):
        return jnp.maximum(t - 1, 0)

    tab_spec = pl.BlockSpec((s, LANES), lambda t: (row(t), 0), pipeline_mode=pl.Buffered(1))
    return pl.pallas_call(
        _in_proj_kernel,
        out_shape=jax.ShapeDtypeStruct((bsz, s, D_IN), BF16),
        grid=(total + 1,),
        in_specs=[
            pl.BlockSpec((1, s, d), lambda t: (row(t), 0, 0)),
            pl.BlockSpec((1, d), lambda t: (0, 0)),
            pl.BlockSpec((1, d, IN_STEP_W), lambda t: (layer, 0, col(t))),
            pl.BlockSpec((1, 1, IN_STEP_W), lambda t: (layer, 0, col(t))),
            tab_spec, tab_spec,
        ],
        out_specs=pl.BlockSpec((1, s, IN_STEP_W), lambda t: (row(lag(t)), 0, col(lag(t)))),
        scratch_shapes=[pltpu.VMEM((s, d), BF16), pltpu.VMEM((s, IN_STEP_W), F32),
                        pltpu.VMEM((s, IN_STEP_W), F32)],
        compiler_params=_cparams(("arbitrary",)),
        name="in_proj",
    )(x, g.reshape(1, d), w_all, b_all, cos_tab, sin_tab)


STACK = C_HEADS_PER_GROUP * BLOCK
STEP_UNROLL = 8
A_STEP_UNROLL = 8
BAND_LOOKAHEAD = 4
FAST_STRIDE = 4


def _head_stack(blk):
    seg = lax.broadcasted_iota(jnp.int32, (1, MXU_W), 1) // HEAD_DIM
    rowseg = lax.broadcasted_iota(jnp.int32, (STACK, 1), 0) // BLOCK
    return jnp.where(seg == rowseg, jnp.concatenate([blk] * C_HEADS_PER_GROUP, axis=0),
                     jnp.zeros((), blk.dtype))


def _band_blocks(jobs, kp_ref, vp_ref, has_prev, diag_key, sink_col, want_lse):
    contract = (((1,), (1,)), ((), ()))
    i_loc = lax.broadcasted_iota(jnp.int32, (STACK, 1), 0) & (BLOCK - 1)
    jj = lax.broadcasted_iota(jnp.int32, (1, BLOCK), 1)
    upper = jj > i_loc
    seg = lax.broadcasted_iota(jnp.int32, (1, MXU_W), 1) // HEAD_DIM
    zero_b = jnp.zeros((), BF16)

    def key_rows(blk):
        lo = pl.multiple_of(blk * BLOCK, BLOCK)
        if has_prev:
            return pl.ds(lo, 2 * BLOCK)
        return pl.ds(pl.multiple_of(lo + BLOCK, BLOCK), BLOCK)

    def score(job):
        q, blk, _ = job
        return lax.dot_general(_head_stack(q), kp_ref[key_rows(blk), :], contract,
                               preferred_element_type=F32)

    def own_lanes(x):
        out = x[0:BLOCK]
        for h in range(1, C_HEADS_PER_GROUP):
            out = jnp.where(seg == h, x[h * BLOCK:(h + 1) * BLOCK], out)
        return out

    scores = {i: score(job) for i, job in enumerate(jobs[:BAND_LOOKAHEAD])}
    outs = []
    for i, (_, blk, n) in enumerate(jobs):
        s = scores.pop(i)
        if has_prev:
            sp = s[:, :BLOCK] + jnp.where(n == 0, NEG_INF, 0.0).astype(F32)
            f = jnp.where(upper, sp, s[:, BLOCK:])
        else:
            f = jnp.where(upper, NEG_INF, s)
        m = jnp.max(f, -1, keepdims=True)
        if diag_key:
            sd = jnp.sum(jnp.where(jj == i_loc, sp, 0.0), -1, keepdims=True)
            m = jnp.maximum(m, sd)
        if sink_col is not None:
            m = jnp.maximum(m, sink_col)
        p = jnp.exp(f - m)
        den = jnp.sum(p, -1, keepdims=True)
        pd = None
        if diag_key:
            pd = jnp.exp(sd - m)
            den = den + pd
        if sink_col is not None:
            den = den + jnp.exp(sink_col - m)
        inv = 1.0 / den
        pn = (p * inv).astype(BF16)
        if has_prev:
            below = zero_b
            if diag_key:
                below = jnp.where(jj == i_loc, pd * inv, 0.0).astype(BF16)
            pn = jnp.concatenate([jnp.where(upper, pn, below), jnp.where(upper, zero_b, pn)],
                                 axis=1)
        o = own_lanes(jnp.dot(pn, vp_ref[key_rows(blk), :], preferred_element_type=F32))
        lse = own_lanes(m + jnp.log(den)) if want_lse else None
        outs.append((o, lse))
        if i + BAND_LOOKAHEAD < len(jobs):
            scores[i + BAND_LOOKAHEAD] = score(jobs[i + BAND_LOOKAHEAD])
    return outs


def _replicate_head(blk, g):
    words = pltpu.bitcast(blk, jnp.uint32)
    tile = words[:, (g // 2) * LANES:(g // 2 + 1) * LANES]
    swapped = pltpu.roll(tile, HEAD_DIM, 1)
    low = lax.broadcasted_iota(jnp.int32, (1, LANES), 1) < HEAD_DIM
    both = jnp.where(low, tile, swapped) if g % 2 == 0 else jnp.where(low, swapped, tile)
    return pltpu.bitcast(jnp.concatenate([both, both], axis=1), BF16)


def _attn_a_kernel(sink_ref, q_ref, k_ref, v_ref, o_ref, kp_ref, vp_ref):
    nb = q_ref.shape[1] // BLOCK
    rowseg = lax.broadcasted_iota(jnp.int32, (STACK, 1), 0) // BLOCK
    kp_ref[pl.ds(0, BLOCK), :] = jnp.zeros((BLOCK, MXU_W), BF16)
    vp_ref[pl.ds(0, BLOCK), :] = jnp.zeros((BLOCK, MXU_W), BF16)
    for g in range(A_KV_HEADS):
        q_cols = slice(g * MXU_W, (g + 1) * MXU_W)

        def build(n, carry):
            rows = pl.ds(pl.multiple_of(n * BLOCK, BLOCK), BLOCK)
            dst = pl.ds(pl.multiple_of((n + 1) * BLOCK, BLOCK), BLOCK)
            kp_ref[dst, :] = _replicate_head(k_ref[0, rows, :], g)
            vp_ref[dst, :] = _replicate_head(v_ref[0, rows, :], g)
            return carry

        lax.fori_loop(0, nb, build, 0, unroll=8)
        sink_col = jnp.zeros((STACK, 1), F32)
        for r in range(A_REP):
            sink_col = jnp.where(rowseg == r, sink_ref[g * A_REP + r], sink_col)

        def step(it, carry):
            blocks = [it * A_STEP_UNROLL + u for u in range(A_STEP_UNROLL)]
            rows = [pl.ds(pl.multiple_of(n * BLOCK, BLOCK), BLOCK) for n in blocks]
            jobs = [(q_ref[0, r, q_cols], n, n) for r, n in zip(rows, blocks)]
            outs = _band_blocks(jobs, kp_ref, vp_ref, True, False, sink_col, False)
            for r, (o, _) in zip(rows, outs):
                o_ref[0, r, q_cols] = o.astype(o_ref.dtype)
            return carry

        lax.fori_loop(0, nb // A_STEP_UNROLL, step, 0)


def _attn_a(z, sinks):
    bsz, s, _ = z.shape
    nb = s // BLOCK
    grid_spec = pltpu.PrefetchScalarGridSpec(
        num_scalar_prefetch=1,
        grid=(bsz,),
        in_specs=[
            pl.BlockSpec((1, s, A_Q_W), lambda bi, sk: (bi, 0, 0)),
            pl.BlockSpec((1, s, A_KV_W), lambda bi, sk: (bi, 0, COL_AK)),
            pl.BlockSpec((1, s, A_KV_W), lambda bi, sk: (bi, 0, COL_AV)),
        ],
        out_specs=pl.BlockSpec((1, s, A_Q_W), lambda bi, sk: (bi, 0, 0)),
        scratch_shapes=[pltpu.VMEM(((nb + 1) * BLOCK, MXU_W), BF16),
                        pltpu.VMEM(((nb + 1) * BLOCK, MXU_W), BF16)],
    )
    return pl.pallas_call(
        _attn_a_kernel,
        out_shape=jax.ShapeDtypeStruct((bsz, s, A_Q_W), BF16),
        grid_spec=grid_spec,
        compiler_params=_cparams(("parallel",)),
        name="attn_a",
    )(sinks, z, z, z)


def _attn_c_kernel(q_ref, k_ref, v_ref, oc_ref, stage_ref, stage2_ref, pq_ref, pk_ref, pv_ref,
                   og_ref, lg_ref):
    s = q_ref.shape[1]
    nblk = s // BLOCK
    halves = MXU_W // LANES
    pk_ref[pl.ds(0, BLOCK), :] = jnp.zeros((BLOCK, MXU_W), BF16)
    pv_ref[pl.ds(0, BLOCK), :] = jnp.zeros((BLOCK, MXU_W), BF16)

    def class_rows(idx, nb, d):
        r = idx // nb
        n = idx - r * nb
        start = r + n * (BLOCK * d)
        rows = pl.ds(start, BLOCK, stride=d) if d > 1 else pl.ds(pl.multiple_of(start, BLOCK), BLOCK)
        return n, rows

    quarter = s // FAST_STRIDE

    def stage(src_ref, cols, d):
        x = src_ref[0, :, cols].astype(F32)
        for t in range(halves):
            stage_ref[t] = x[:, t * LANES:(t + 1) * LANES]
        if d > FAST_STRIDE:
            for t in range(halves):
                for r1 in range(FAST_STRIDE):
                    stage2_ref[t, pl.ds(r1 * quarter, quarter), :] = (
                        stage_ref[t, pl.ds(r1, quarter, stride=FAST_STRIDE), :])

    def staged_block(idx, nb, d):
        if d > FAST_STRIDE:
            assert nb == 1 and d == FAST_STRIDE * FAST_STRIDE
            start = (idx % FAST_STRIDE) * quarter + idx // FAST_STRIDE
            rows = pl.ds(start, BLOCK, stride=FAST_STRIDE)
            parts = [stage2_ref[t, rows, :] for t in range(halves)]
        else:
            _, rows = class_rows(idx, nb, d)
            parts = [stage_ref[t, rows, :] for t in range(halves)]
        return jnp.concatenate(parts, axis=1).astype(BF16)

    for g, (win, d) in enumerate(C_PAIRS):
        cols = slice(g * MXU_W, (g + 1) * MXU_W)
        nb = nblk // d
        has_prev = nb > 1
        assert win // d == BLOCK

        def class_major(dst_ref, offset):
            def build(idx, carry):
                dst = pl.ds(pl.multiple_of((idx + offset) * BLOCK, BLOCK), BLOCK)
                dst_ref[dst, :] = staged_block(idx, nb, d)
                return carry
            return build

        for src_ref, dst_ref, offset in ((k_ref, pk_ref, 1), (v_ref, pv_ref, 1), (q_ref, pq_ref, 0)):
            if d == 1:
                dst_ref[pl.ds(offset * BLOCK, s), :] = src_ref[0, :, cols]
                continue
            stage(src_ref, cols, d)
            lax.fori_loop(0, nblk, class_major(dst_ref, offset), 0, unroll=4)

        def step(it, carry):
            jobs, dsts = [], []
            for u in range(STEP_UNROLL):
                idx = it * STEP_UNROLL + u
                n, rows = class_rows(idx, nb, d)
                here = pl.ds(pl.multiple_of(idx * BLOCK, BLOCK), BLOCK)
                jobs.append((pq_ref[here, :], idx, n))
                dsts.append(rows)
            outs = _band_blocks(jobs, pk_ref, pv_ref, has_prev, has_prev, None, True)
            for rows, (o, lse) in zip(dsts, outs):
                for t in range(halves):
                    og_ref[g, t, rows, :] = o[:, t * LANES:(t + 1) * LANES]
                    lg_ref[g, t, rows, :] = lse[:, t * LANES:(t + 1) * LANES]
            return carry

        lax.fori_loop(0, nblk // STEP_UNROLL, step, 0)

    chunk = 2 * BLOCK

    def combine(i, carry):
        rows = pl.ds(pl.multiple_of(i * chunk, chunk), chunk)
        for t in range(halves):
            ls = [lg_ref[g, t, rows, :] for g in range(len(C_PAIRS))]
            m = jnp.maximum(jnp.maximum(ls[0], ls[1]), ls[2])
            es = [jnp.exp(l - m) for l in ls]
            num = (es[0] * og_ref[0, t, rows, :] + es[1] * og_ref[1, t, rows, :]
                   + es[2] * og_ref[2, t, rows, :])
            oc_ref[0, rows, t * LANES:(t + 1) * LANES] = (
                num / (es[0] + es[1] + es[2])).astype(oc_ref.dtype)
        return carry

    lax.fori_loop(0, s // chunk, combine, 0)


def _attn_c(z):
    bsz, s, _ = z.shape
    width = C_HEADS * HEAD_DIM
    halves = MXU_W // LANES

    def in_spec(col):
        return pl.BlockSpec((1, s, width), lambda bi: (bi, 0, col * IN_TN // width))

    return pl.pallas_call(
        _attn_c_kernel,
        out_shape=jax.ShapeDtypeStruct((bsz, s, C_OUT_W), BF16),
        grid=(bsz,),
        in_specs=[in_spec(COL_CQ), in_spec(COL_CK), in_spec(COL_CV)],
        out_specs=pl.BlockSpec((1, s, C_OUT_W), lambda bi: (bi, 0, 0)),
        scratch_shapes=[
            pltpu.VMEM((halves, s, LANES), F32),
            pltpu.VMEM((halves, s, LANES), F32),
            pltpu.VMEM((s, MXU_W), BF16),
            pltpu.VMEM((s + BLOCK, MXU_W), BF16),
            pltpu.VMEM((s + BLOCK, MXU_W), BF16),
            pltpu.VMEM((len(C_PAIRS), halves, s, LANES), F32),
            pltpu.VMEM((len(C_PAIRS), halves, s, LANES), F32),
        ],
        compiler_params=_cparams(("parallel",)),
        name="attn_c",
    )(z, z, z)


def _sgu_kernel(zu_ref, zv_ref, lng_ref, lnb_ref, ws_ref, bs_ref, o_ref):
    tm = zu_ref.shape[1]
    u = jax.nn.gelu(zu_ref[0]).astype(F32)
    v = jax.nn.gelu(zv_ref[0]).astype(F32)
    mu = jnp.mean(v, -1, keepdims=True)
    var = jnp.mean(jnp.square(v - mu), -1, keepdims=True)
    vn = ((v - mu) * lax.rsqrt(var + EPS) * lng_ref[...] + lnb_ref[...]).astype(BF16)
    causal = (lax.broadcasted_iota(jnp.int32, (B_CHUNK, 1), 0)
              >= lax.broadcasted_iota(jnp.int32, (1, B_CHUNK), 1))
    first = lax.broadcasted_iota(jnp.int32, (1, LANES), 1) < B_CH
    zero = jnp.zeros((), BF16)
    for p in range(B_WIDTH // LANES):
        w0 = jnp.where(causal, ws_ref[2 * p], zero)
        w1 = jnp.where(causal, ws_ref[2 * p + 1], zero)
        cols = slice(p * LANES, (p + 1) * LANES)
        for c in range(tm // B_CHUNK):
            rows = slice(c * B_CHUNK, (c + 1) * B_CHUNK)
            vv = vn[rows, cols]
            sv = jnp.where(first,
                           jnp.dot(w0, vv, preferred_element_type=F32),
                           jnp.dot(w1, vv, preferred_element_type=F32)) + bs_ref[:, cols]
            o_ref[0, rows, cols] = (u[rows, cols] * sv).astype(o_ref.dtype)


def _sgu(z, ln_g, ln_b, w_s, b_s):
    bsz, s, _ = z.shape
    tm = 512
    bias = jnp.repeat(b_s.T, B_CH, axis=1)
    col_u = COL_BU * IN_TN // B_WIDTH
    col_v = COL_BV * IN_TN // B_WIDTH
    return pl.pallas_call(
        _sgu_kernel,
        out_shape=jax.ShapeDtypeStruct((bsz, s, B_WIDTH), BF16),
        grid=(bsz, s // tm),
        in_specs=[
            pl.BlockSpec((1, tm, B_WIDTH), lambda bi, i: (bi, i, col_u)),
            pl.BlockSpec((1, tm, B_WIDTH), lambda bi, i: (bi, i, col_v)),
            pl.BlockSpec((1, B_WIDTH), lambda bi, i: (0, 0)),
            pl.BlockSpec((1, B_WIDTH), lambda bi, i: (0, 0)),
            pl.BlockSpec((B_GROUPS, B_CHUNK, B_CHUNK), lambda bi, i: (0, 0, 0)),
            pl.BlockSpec((B_CHUNK, B_WIDTH), lambda bi, i: (0, 0)),
        ],
        out_specs=pl.BlockSpec((1, tm, B_WIDTH), lambda bi, i: (bi, i, 0)),
        compiler_params=_cparams(("parallel", "parallel")),
        name="sgu",
    )(z, z, ln_g.reshape(1, B_WIDTH), ln_b.reshape(1, B_WIDTH), w_s.astype(BF16), bias)


def _route(logits):
    lane = lax.broadcasted_iota(jnp.int32, (1, ROUTE_W), 1)
    lane_f = lane.astype(F32)
    is_g = lane < N_EXPERT_GROUPS
    lg = jnp.where(is_g, logits, NEG_INF)
    mg = jnp.max(lg, -1, keepdims=True)
    pg_top = 1.0 / jnp.sum(jnp.where(is_g, jnp.exp(lg - mg), 0.0), -1, keepdims=True)
    g_idx = jnp.min(jnp.where(lg == mg, lane_f, float(ROUTE_W)), -1, keepdims=True)
    e_group = ((lane - ROUTE_E0) // EXPERTS_PER_GROUP).astype(F32)
    in_grp = (lane >= ROUTE_E0) & (lane < ROUTE_E0 + N_EXPERTS) & (e_group == g_idx)
    le = jnp.where(in_grp, logits, NEG_INF)
    m1 = jnp.max(le, -1, keepdims=True)
    i1 = jnp.min(jnp.where(le == m1, lane_f, float(ROUTE_W)), -1, keepdims=True)
    le2 = jnp.where(lane_f == i1, NEG_INF, le)
    m2 = jnp.max(le2, -1, keepdims=True)
    i2 = jnp.min(jnp.where(le2 == m2, lane_f, float(ROUTE_W)), -1, keepdims=True)
    t = jnp.exp(m2 - m1)
    w1 = pg_top / (1.0 + t)
    w2 = w1 * t
    picks = (i1 - ROUTE_E0, i2 - ROUTE_E0, w1, w2)
    out = jnp.zeros(logits.shape, F32)
    for k, val in enumerate(picks):
        out = jnp.where(lane == k, val, out)
    return out


def _merge_kernel(x_ref, oa_ref, ob_ref, oc_ref,
                  g0_ref, g1_ref, g2_ref, g3_ref, wa_ref, wb_ref, wc_ref, wo_ref, n2_ref, wr_ref,
                  xo_ref, h2_ref, route_ref, counts_ref, pos_ref, xn0_ref, xn1_ref, cnt_ref):
    i = pl.program_id(0)
    n_tiles = pl.num_programs(0) - 1

    def residual_into(xn_ref):
        zg = jnp.concatenate([g0_ref[...], g1_ref[...], g2_ref[...], g3_ref[...]],
                             axis=1).astype(F32)
        gates = jax.nn.sigmoid(zg)
        halves = (slice(0, D_MODEL // 2), slice(D_MODEL // 2, D_MODEL))
        branch = []
        for cols in halves:
            branch.append((jnp.dot(oa_ref[...], wa_ref[:, cols], preferred_element_type=F32),
                           jnp.dot(ob_ref[...], wb_ref[:, cols], preferred_element_type=F32),
                           jnp.dot(oc_ref[...], wc_ref[:, cols], preferred_element_type=F32)))
        xn = x_ref[...]
        for cols, (pa, pb, pc) in zip(halves, branch):
            ga, gb, gc = (gates[:, k * D_MODEL:(k + 1) * D_MODEL][:, cols] for k in range(N_BRANCH))
            merged = ga * pa + gb * pb + gc * pc
            xn = xn + jnp.dot(merged.astype(BF16), wo_ref[cols, :], preferred_element_type=F32)
        xn_ref[...] = xn

    def tail_from(xn_ref):
        xn = xn_ref[...]
        xo_ref[...] = xn
        ms = jnp.mean(xn * xn, -1, keepdims=True)
        h2 = xn * lax.rsqrt(ms + EPS) * n2_ref[...]
        h2_ref[...] = _pack_bf16_pairs(h2)
        route = _route(jnp.dot(h2.astype(BF16), wr_ref[...], preferred_element_type=F32))
        tm = route.shape[0]
        lane = lax.broadcasted_iota(jnp.int32, (1, ROUTE_W), 1)
        lane_f = lane.astype(F32)
        row = lax.broadcasted_iota(jnp.int32, (tm, 1), 0)
        hit = [lane_f == route[:, k:k + 1] for k in range(TOP_K)]
        onehot = jnp.where(hit[0], 1.0, 0.0) + jnp.where(hit[1], 1.0, 0.0)
        scan = onehot
        shift = 1
        while shift < tm:
            scan = scan + jnp.where(row >= shift, pltpu.roll(scan, shift, 0), 0.0)
            shift *= 2
        before = scan - onehot + cnt_ref[0:1, :]
        capacity = pos_ref.shape[1] * LANES
        diag = (row & (LANES - 1)) == lane
        first_row = pl.multiple_of((i - 1) * (tm // LANES), tm // LANES)
        for k in range(TOP_K):
            rank = jnp.sum(jnp.where(hit[k], before, 0.0), -1, keepdims=True)
            slot = jnp.where(diag, route[:, k:k + 1] * capacity + rank, 0.0)
            dense = jnp.concatenate(
                [jnp.sum(slot[r * LANES:(r + 1) * LANES], axis=0, keepdims=True)
                 for r in range(tm // LANES)], axis=0)
            pos_ref[k, pl.ds(first_row, tm // LANES), :] = dense.astype(jnp.int32)
        route_ref[...] = route
        cnt = cnt_ref[0:1, :] + jnp.sum(onehot, axis=0, keepdims=True)
        cnt_ref[...] = jnp.broadcast_to(cnt, cnt_ref.shape)
        counts_ref[...] = jnp.broadcast_to(cnt, counts_ref.shape)

    bufs = (xn0_ref, xn1_ref)
    for parity in range(2):
        @pl.when((i > 0) & (i < n_tiles) & (i % 2 == parity))
        def _(parity=parity):
            tail_from(bufs[1 - parity])
            residual_into(bufs[parity])

        @pl.when((i == n_tiles) & (i % 2 == parity))
        def _(parity=parity):
            tail_from(bufs[1 - parity])

    @pl.when(i == 0)
    def _():
        cnt_ref[...] = jnp.zeros_like(cnt_ref)
        residual_into(bufs[0])


def _merge(x2, oa, ob, oc, z2, wa, wb, wc, wo, n2g, w_rg, w_re):
    n, d = x2.shape
    tm = 512
    wr = jnp.concatenate(
        [w_rg, w_re, jnp.zeros((d, ROUTE_W - N_EXPERT_GROUPS - N_EXPERTS), w_rg.dtype)], axis=1)
    gate_w = GATE_W // 4
    gate_col0 = COL_G * IN_TN // gate_w

    n_tiles = n // tm
    last = n_tiles - 1

    def rows(width):
        return pl.BlockSpec((tm, width), lambda i: (jnp.minimum(i, last), 0))

    def lagged_rows(width):
        return pl.BlockSpec((tm, width), lambda i: (jnp.maximum(i - 1, 0), 0))

    def full(shape):
        return pl.BlockSpec(shape, lambda i: (0, 0))

    gate_specs = [pl.BlockSpec((tm, gate_w), lambda i, k=k: (jnp.minimum(i, last), gate_col0 + k))
                  for k in range(4)]
    return pl.pallas_call(
        _merge_kernel,
        out_shape=[jax.ShapeDtypeStruct((n, d), F32),
                   jax.ShapeDtypeStruct((n, d // 2), jnp.int32),
                   jax.ShapeDtypeStruct((n, ROUTE_W), F32),
                   jax.ShapeDtypeStruct((SUBLANES, ROUTE_W), F32),
                   jax.ShapeDtypeStruct((TOP_K, n // LANES, LANES), jnp.int32)],
        grid=(n_tiles + 1,),
        in_specs=[rows(d), rows(A_Q_W), rows(B_WIDTH), rows(C_OUT_W),
                  *gate_specs,
                  full((A_Q_W, d)), full((B_WIDTH, d)), full((C_OUT_W, d)), full((d, d)),
                  full((1, d)), full((d, ROUTE_W))],
        out_specs=[lagged_rows(d), lagged_rows(d // 2), lagged_rows(ROUTE_W),
                   full((SUBLANES, ROUTE_W)),
                   pl.BlockSpec((TOP_K, n // LANES, LANES), lambda i: (0, 0, 0))],
        scratch_shapes=[pltpu.VMEM((tm, d), F32), pltpu.VMEM((tm, d), F32),
                        pltpu.VMEM((SUBLANES, ROUTE_W), F32)],
        compiler_params=_cparams(("arbitrary",)),
        name="merge",
    )(x2, oa, ob, oc, z2, z2, z2, z2,
      wa.astype(BF16), wb.astype(BF16), wc.astype(BF16), wo.astype(BF16),
      n2g.reshape(1, d), wr.astype(BF16))


MOE_TM = 1024
SC_CORES = 2
SC_SUBCORES = 16
SC_WORKERS = SC_CORES * SC_SUBCORES
SC_CHUNK = 128


def _pack_bf16_pairs(x):
    w = x.shape[1] // 2
    lo = pltpu.bitcast(x[:, :w].astype(BF16).astype(F32), jnp.int32)
    hi = pltpu.bitcast(x[:, w:].astype(BF16).astype(F32), jnp.int32)
    return (hi & jnp.int32(-65536)) | lax.shift_right_logical(lo, jnp.int32(16))


def _unpack_bf16_pairs(p):
    lo = pltpu.bitcast(lax.shift_left(p, jnp.int32(16)), F32)
    hi = pltpu.bitcast(p & jnp.int32(-65536), F32)
    return jnp.concatenate([lo, hi], axis=1)


def _sc_gather_rows(table, idx):
    rows, width = idx.shape[0], table.shape[1]
    per_worker = rows // SC_WORKERS
    n_chunks = per_worker // SC_CHUNK
    assert per_worker * SC_WORKERS == rows and n_chunks * SC_CHUNK == per_worker
    mesh = plsc.VectorSubcoreMesh(core_axis_name="c", subcore_axis_name="s",
                                  num_cores=SC_CORES, num_subcores=SC_SUBCORES)

    @functools.partial(
        pl.kernel, mesh=mesh,
        out_type=jax.ShapeDtypeStruct((rows, width), table.dtype),
        scratch_types=[pltpu.VMEM((n_chunks, SC_CHUNK), jnp.int32),
                       pltpu.VMEM((SC_CHUNK, width), table.dtype),
                       pltpu.SemaphoreType.DMA],
        name="sc_gather_rows",
    )
    def gather(table_hbm, idx_hbm, out_hbm, idx_v, rows_v, sem):
        wid = lax.axis_index("s") * SC_CORES + lax.axis_index("c")
        pltpu.sync_copy(idx_hbm.at[wid], idx_v)
        base = wid * per_worker

        @pl.loop(0, n_chunks)
        def _(c):
            pltpu.async_copy(table_hbm.at[idx_v.at[c]], rows_v, sem).wait()
            pltpu.sync_copy(rows_v, out_hbm.at[pl.ds(base + c * SC_CHUNK, SC_CHUNK)])

    return gather(table, idx.reshape(SC_WORKERS, n_chunks, SC_CHUNK))


def _sc_scatter_rows(rows, pos, n_slots):
    n, width = rows.shape
    per_worker = n // SC_WORKERS
    n_chunks = per_worker // SC_CHUNK
    assert per_worker * SC_WORKERS == n and n_chunks * SC_CHUNK == per_worker
    mesh = plsc.VectorSubcoreMesh(core_axis_name="c", subcore_axis_name="s",
                                  num_cores=SC_CORES, num_subcores=SC_SUBCORES)

    @functools.partial(
        pl.kernel, mesh=mesh,
        out_type=jax.ShapeDtypeStruct((n_slots, width), rows.dtype),
        scratch_types=[pltpu.VMEM((TOP_K, n_chunks, SC_CHUNK), jnp.int32),
                       pltpu.VMEM((SC_CHUNK, width), rows.dtype),
                       pltpu.SemaphoreType.DMA],
        name="sc_scatter_rows",
    )
    def scatter(rows_hbm, pos_hbm, out_hbm, pos_v, rows_v, sem):
        wid = lax.axis_index("s") * SC_CORES + lax.axis_index("c")
        for k in range(TOP_K):
            pltpu.sync_copy(pos_hbm.at[k, wid], pos_v.at[k])
        base = wid * per_worker

        @pl.loop(0, n_chunks)
        def _(c):
            pltpu.sync_copy(rows_hbm.at[pl.ds(base + c * SC_CHUNK, SC_CHUNK)], rows_v)
            for k in range(TOP_K):
                pltpu.async_copy(rows_v, out_hbm.at[pos_v.at[k, c]], sem).wait()

    return scatter(rows, pos.reshape(TOP_K, SC_WORKERS, n_chunks, SC_CHUNK))


def _dispatch_plan(counts, capacity, n_steps, expert_base):
    counts = counts[0, :N_EXPERTS].astype(jnp.int32)
    tiles = (counts + MOE_TM - 1) // MOE_TM
    tile_ends = jnp.cumsum(tiles)
    step = jnp.arange(n_steps, dtype=jnp.int32)
    used = step < tile_ends[-1]
    expert = jnp.minimum(jnp.sum((step[:, None] >= tile_ends[None, :]).astype(jnp.int32), axis=1),
                         N_EXPERTS - 1)
    chunk = step - (tile_ends - tiles)[expert]
    rows = jnp.where(used, jnp.clip(counts[expert] - chunk * MOE_TM, 0, MOE_TM), 0)
    first = jnp.concatenate([jnp.ones((1,), jnp.int32),
                             (expert[1:] != expert[:-1]).astype(jnp.int32)])
    block = expert * (capacity // MOE_TM) + chunk
    last_used = jnp.sum(jnp.where(step == tile_ends[-1] - 1, block, 0))
    spare = N_EXPERTS * (capacity // MOE_TM)
    blocks = jnp.stack([jnp.where(used, block, last_used), jnp.where(used, block, spare)])
    return expert + expert_base, rows.astype(jnp.int32), first, blocks.astype(jnp.int32)


def _moe_tile_kernel(te_ref, tr_ref, tf_ref, tb_ref, xs_ref, wg_ref, wu_ref, wd_ref, ys_ref,
                     wg_s, wu_s, wd_s):
    t = pl.program_id(0)

    @pl.when(tf_ref[t] != 0)
    def _():
        wg_s[...] = wg_ref[0].astype(BF16)
        wu_s[...] = wu_ref[0].astype(BF16)
        wd_s[...] = wd_ref[0].astype(BF16)

    @pl.when(tr_ref[t] != 0)
    def _():
        occupied = lax.broadcasted_iota(jnp.int32, (MOE_TM, 1), 0) < tr_ref[t]
        x = jnp.where(occupied, _unpack_bf16_pairs(xs_ref[...]), 0.0).astype(BF16)
        hg = jnp.dot(x, wg_s[...], preferred_element_type=F32)
        hu = jnp.dot(x, wu_s[...], preferred_element_type=F32)
        a = (jax.nn.silu(hg) * hu).astype(BF16)
        ys_ref[...] = _pack_bf16_pairs(jnp.dot(a, wd_s[...], preferred_element_type=F32))

    @pl.when(tr_ref[t] == 0)
    def _():
        ys_ref[...] = jnp.zeros_like(ys_ref)


def _moe_tiles(xs, tile_expert, tile_rows, tile_first, tile_blocks, w_gate, w_up, w_down):
    n_slots, half = xs.shape
    d = 2 * half
    grid_spec = pltpu.PrefetchScalarGridSpec(
        num_scalar_prefetch=4,
        grid=(tile_expert.shape[0],),
        in_specs=[
            pl.BlockSpec((MOE_TM, half), lambda t, te, tr, tf, tb: (tb[0, t], 0)),
            pl.BlockSpec((1, d, D_EXPERT), lambda t, te, tr, tf, tb: (te[t], 0, 0)),
            pl.BlockSpec((1, d, D_EXPERT), lambda t, te, tr, tf, tb: (te[t], 0, 0)),
            pl.BlockSpec((1, D_EXPERT, d), lambda t, te, tr, tf, tb: (te[t], 0, 0)),
        ],
        out_specs=pl.BlockSpec((MOE_TM, half), lambda t, te, tr, tf, tb: (tb[1, t], 0)),
        scratch_shapes=[pltpu.VMEM((d, D_EXPERT), BF16), pltpu.VMEM((d, D_EXPERT), BF16),
                        pltpu.VMEM((D_EXPERT, d), BF16)],
    )
    return pl.pallas_call(
        _moe_tile_kernel,
        out_shape=jax.ShapeDtypeStruct((n_slots, half), jnp.int32),
        grid_spec=grid_spec,
        compiler_params=_cparams(("arbitrary",)),
        name="moe_tiles",
    )(tile_expert, tile_rows, tile_first, tile_blocks, xs, w_gate, w_up, w_down)


def _moe_combine_kernel(x_ref, y0_ref, y1_ref, r_ref, g_ref, o_ref, *, final_norm):
    w0 = r_ref[:, 2:3]
    w1 = r_ref[:, 3:4]
    x = x_ref[...] + w0 * _unpack_bf16_pairs(y0_ref[0]) + w1 * _unpack_bf16_pairs(y1_ref[0])
    if final_norm:
        ms = jnp.mean(x * x, -1, keepdims=True)
        x = x * lax.rsqrt(ms + EPS) * g_ref[...]
    o_ref[...] = x


def _moe_combine(x2, yg, route, norm_g, final_norm):
    n, d = x2.shape
    tm = 1024
    return pl.pallas_call(
        functools.partial(_moe_combine_kernel, final_norm=final_norm),
        out_shape=jax.ShapeDtypeStruct((n, d), F32),
        grid=(n // tm,),
        in_specs=[
            pl.BlockSpec((tm, d), lambda i: (i, 0)),
            pl.BlockSpec((1, tm, d // 2), lambda i: (0, i, 0)),
            pl.BlockSpec((1, tm, d // 2), lambda i: (1, i, 0)),
            pl.BlockSpec((tm, ROUTE_W), lambda i: (i, 0)),
            pl.BlockSpec((1, d), lambda i: (0, 0)),
        ],
        out_specs=pl.BlockSpec((tm, d), lambda i: (i, 0)),
        compiler_params=_cparams(("parallel",)),
        name="moe_combine",
    )(x2, yg, yg, route, norm_g.reshape(1, d))


def _moe(h2p, route, counts, pos, x2, layer, w_gate, w_up, w_down, norm_f_g, final_norm):
    n, d = x2.shape
    n_slots = N_EXPERTS * n + MOE_TM
    n_steps = TOP_K * n // MOE_TM + N_EXPERTS
    tile_expert, tile_rows, tile_first, tile_blocks = _dispatch_plan(
        counts, n, n_steps, layer * N_EXPERTS)
    xs = _sc_scatter_rows(h2p, pos, n_slots)
    ys = _moe_tiles(xs, tile_expert, tile_rows, tile_first, tile_blocks, w_gate, w_up, w_down)
    yg = _sc_gather_rows(ys, pos.reshape(-1)).reshape(TOP_K, n, d // 2)
    return _moe_combine(x2, yg, route, norm_f_g, final_norm)


def kernel(x, positions, norm1_g, w_in, b_in, attn_sinks, sgu_ln_g, sgu_ln_b, w_spatial, b_spatial,
           w_proj_a, w_proj_b, w_proj_c, w_out, norm2_g, w_router_group, w_router_expert,
           w_expert_gate, w_expert_up, w_expert_down, norm_f_g):
    bsz, s, d = x.shape
    depth = w_in.shape[0]
    wg_all = w_expert_gate.reshape(depth * N_EXPERTS, d, D_EXPERT)
    wu_all = w_expert_up.reshape(depth * N_EXPERTS, d, D_EXPERT)
    wd_all = w_expert_down.reshape(depth * N_EXPERTS, D_EXPERT, d)
    assert d == D_MODEL and s % (C_PAIRS[-1][1] * BLOCK) == 0
    tabs = _rope_tables(positions)
    for l in range(depth):
        z = _in_proj(x, norm1_g[l], w_in, b_in.reshape(depth, 1, D_IN), l, tabs)
        oa = _attn_a(z, attn_sinks[l])
        ob = _sgu(z, sgu_ln_g[l], sgu_ln_b[l], w_spatial[l], b_spatial[l])
        oc = _attn_c(z)
        x2, h2, route, counts, pos = _merge(
            x.reshape(bsz * s, d), oa.reshape(bsz * s, A_Q_W), ob.reshape(bsz * s, B_WIDTH),
            oc.reshape(bsz * s, C_OUT_W), z.reshape(bsz * s, D_IN), w_proj_a[l], w_proj_b[l], w_proj_c[l], w_out[l],
            norm2_g[l], w_router_group[l], w_router_expert[l])
        x = _moe(h2, route, counts, pos, x2, l, wg_all, wu_all, wd_all, norm_f_g, l == depth - 1).reshape(bsz, s, d)
    return x
```
